```python
import jax, jax.numpy as jnp
from jax import lax
import numpy as np

D_MODEL = 2048
BATCH = 8
SEQ = 2048
DEPTH = 1

MEM_LEN = 256
EPS = 1e-6
CHUNK = 128
A_GROUPS = 4
A_GROUP_CH = 128
A_WIDTH = A_GROUPS * A_GROUP_CH
WINDOW = 128
B_HEADS = 16
B_KV_HEADS = 2
B_HEAD_DIM = 64
B_WIDTH = B_HEADS * B_HEAD_DIM
B_KV_WIDTH = B_KV_HEADS * B_HEAD_DIM
ROPE_DIM = B_HEAD_DIM // 4
ROPE_THETA = 500000.0
C_HEADS = 4
C_HEAD_DIM = 128
C_WIDTH = C_HEADS * C_HEAD_DIM
N_BRANCH = 3
D_FF = 5632
CONV_W = 3

SPLITS = list(np.cumsum([A_WIDTH, A_WIDTH, B_WIDTH, B_KV_WIDTH, B_KV_WIDTH, C_WIDTH]).tolist())
IN_COLS = 2 * A_WIDTH + B_WIDTH + 2 * B_KV_WIDTH + C_WIDTH + N_BRANCH * D_MODEL

kernel_name = "hybrid_gated_parallel_mixers"


def rmsnorm(x, g):
    xf = x.astype(jnp.float32)
    y = xf * lax.rsqrt(jnp.mean(xf * xf, axis=-1, keepdims=True) + EPS)
    return (y * g.astype(jnp.float32)).astype(x.dtype)


def partial_rope(x, pos):
    half = ROPE_DIM // 2
    inv = ROPE_THETA ** (-jnp.arange(half, dtype=jnp.float32) / half)
    ang = pos.astype(jnp.float32)[..., None] * inv
    cos = jnp.cos(ang)[:, :, None, :]
    sin = jnp.sin(ang)[:, :, None, :]
    xr = x[..., :ROPE_DIM].astype(jnp.float32)
    x1, x2 = xr[..., :half], xr[..., half:]
    rot = jnp.concatenate([x1 * cos - x2 * sin, x2 * cos + x1 * sin], axis=-1)
    return jnp.concatenate([rot.astype(x.dtype), x[..., ROPE_DIM:]], axis=-1)


def chunked_spatial_gating(u, v, g_v, w_s, b_s):
    bn, s_len, _ = u.shape
    nc = s_len // CHUNK
    u = jax.nn.gelu(u)
    v = rmsnorm(jax.nn.gelu(v), g_v)
    v = v.reshape(bn, nc, CHUNK, A_GROUPS, A_GROUP_CH)
    causal = jnp.tril(jnp.ones((CHUNK, CHUNK), dtype=bool))
    w = jnp.where(causal[None], w_s, jnp.zeros_like(w_s))
    s = jnp.einsum('gts,bnsgc->bntgc', w, v) + b_s.T[None, None, :, :, None]
    return u * s.reshape(bn, s_len, A_WIDTH)


def sliding_window_gqa(q, k, v, g_q, g_k, sinks, pos):
    bn, s_len = q.shape[:2]
    nb = s_len // WINDOW
    rep = B_HEADS // B_KV_HEADS
    q = partial_rope(rmsnorm(q, g_q), pos)
    k = partial_rope(rmsnorm(k, g_k), pos)
    qb = q.reshape(bn, nb, WINDOW, B_KV_HEADS, rep, B_HEAD_DIM)
    kb = k.reshape(bn, nb, WINDOW, B_KV_HEADS, B_HEAD_DIM)
    vb = v.reshape(bn, nb, WINDOW, B_KV_HEADS, B_HEAD_DIM)
    pad_k = jnp.zeros_like(kb[:, :1])
    pad_v = jnp.zeros_like(vb[:, :1])
    k2 = jnp.concatenate([jnp.concatenate([pad_k, kb[:, :-1]], axis=1), kb], axis=2)
    v2 = jnp.concatenate([jnp.concatenate([pad_v, vb[:, :-1]], axis=1), vb], axis=2)
    s = jnp.einsum('bnqhrd,bnkhd->bnhrqk', qb, k2,
                   preferred_element_type=jnp.float32) * (B_HEAD_DIM ** -0.5)
    qi = jnp.arange(WINDOW)[:, None] + WINDOW
    kj = jnp.arange(2 * WINDOW)[None, :]
    rel = qi - kj
    band = (rel >= 0) & (rel < WINDOW)
    blk = jnp.arange(nb)[:, None, None]
    valid = band[None] & ((kj[None] >= WINDOW) | (blk > 0))
    s = jnp.where(valid[None, :, None, None], s, -jnp.inf)
    sink_col = jnp.broadcast_to(
        sinks.astype(jnp.float32).reshape(1, 1, B_KV_HEADS, rep, 1, 1), s.shape[:-1] + (1,))
    p = jax.nn.softmax(jnp.concatenate([s, sink_col], axis=-1), axis=-1)[..., :-1]
    o = jnp.einsum('bnhrqk,bnkhd->bnqhrd', p.astype(v.dtype), v2)
    return o.reshape(bn, s_len, B_WIDTH)


def memory_cross_attention(q, mem_h, w_mem_kv, g_q, g_k):
    bn, s_len = q.shape[:2]
    m_len = mem_h.shape[1]
    kv = (mem_h @ w_mem_kv).reshape(bn, m_len, 2, C_HEADS, C_HEAD_DIM)
    k, v = kv[:, :, 0], kv[:, :, 1]
    q = rmsnorm(q, g_q)
    k = rmsnorm(k, g_k)
    s = jnp.einsum('bshd,bmhd->bhsm', q, k,
                   preferred_element_type=jnp.float32) * (C_HEAD_DIM ** -0.5)
    p = jax.nn.softmax(s, axis=-1)
    o = jnp.einsum('bhsm,bmhd->bshd', p.astype(v.dtype), v)
    return o.reshape(bn, s_len, C_WIDTH)


def gated_conv_ffn(h, w_up, conv_w, conv_b, w_down):
    up = h @ w_up
    c = up.shape[-1]
    up = lax.conv_general_dilated(
        up, conv_w[:, None, :].astype(up.dtype), window_strides=(1,),
        padding=[(CONV_W - 1, 0)], dimension_numbers=('NWC', 'WIO', 'NWC'),
        feature_group_count=c) + conv_b
    a, b = up[..., :D_FF], up[..., D_FF:]
    return (jax.nn.silu(a) * b) @ w_down


def _fwd_setup_inputs(seed: int = 0) -> dict:
    key = jax.random.key(seed)
    ks = jax.random.split(key, 26)
    f32 = jnp.float32
    L = DEPTH

    def nrm(k, shape, scale):
        return jax.random.normal(k, shape, f32) * scale

    def gain(k, n):
        return 1.0 + 0.02 * jax.random.normal(k, (L, n), f32)

    x = nrm(ks[0], (BATCH, SEQ, D_MODEL), 1.0)
    mem = nrm(ks[1], (BATCH, MEM_LEN, D_MODEL), 1.0)
    positions = (jax.random.randint(ks[2], (BATCH, 1), 0, 4096, jnp.int32)
                 + jnp.arange(SEQ, dtype=jnp.int32)[None, :])
    return {
        "x": x,
        "mem": mem,
        "positions": positions,
        "g_mix": gain(ks[3], D_MODEL),
        "w_in": nrm(ks[4], (L, D_MODEL, IN_COLS), D_MODEL ** -0.5),
        "g_a_v": gain(ks[5], A_WIDTH),
        "w_spatial": nrm(ks[6], (L, A_GROUPS, CHUNK, CHUNK), CHUNK ** -0.5),
        "b_spatial": 1.0 + 0.1 * jax.random.normal(ks[7], (L, A_GROUPS, CHUNK), f32),
        "g_b_q": gain(ks[8], B_HEAD_DIM),
        "g_b_k": gain(ks[9], B_HEAD_DIM),
        "sinks": nrm(ks[10], (L, B_HEADS), 0.5),
        "g_mem": gain(ks[11], D_MODEL),
        "w_mem_kv": nrm(ks[12], (L, D_MODEL, 2 * C_WIDTH), D_MODEL ** -0.5),
        "g_c_q": gain(ks[13], C_HEAD_DIM),
        "g_c_k": gain(ks[14], C_HEAD_DIM),
        "w_branch_a": nrm(ks[15], (L, A_WIDTH, D_MODEL), A_WIDTH ** -0.5),
        "w_branch_b": nrm(ks[16], (L, B_WIDTH, D_MODEL), B_WIDTH ** -0.5),
        "w_branch_c": nrm(ks[17], (L, C_WIDTH, D_MODEL), C_WIDTH ** -0.5),
        "w_out": nrm(ks[18], (L, D_MODEL, D_MODEL), D_MODEL ** -0.5),
        "g_ffn": gain(ks[19], D_MODEL),
        "w_up": nrm(ks[20], (L, D_MODEL, 2 * D_FF), D_MODEL ** -0.5),
        "conv_w": nrm(ks[21], (L, CONV_W, 2 * D_FF), CONV_W ** -0.5),
        "conv_b": nrm(ks[22], (L, 2 * D_FF), 0.01),
        "w_down": nrm(ks[23], (L, D_FF, D_MODEL), D_FF ** -0.5),
    }


def _fwd_reference(x, mem, positions, g_mix, w_in, g_a_v, w_spatial, b_spatial, g_b_q, g_b_k,
              sinks, g_mem, w_mem_kv, g_c_q, g_c_k, w_branch_a, w_branch_b, w_branch_c,
              w_out, g_ffn, w_up, conv_w, conv_b, w_down):
    bn, s_len, _ = x.shape
    for l in range(DEPTH):
        h = rmsnorm(x, g_mix[l])
        proj = h @ w_in[l]
        u_a, v_a, q_b, k_b, v_b, q_c, gates = jnp.split(proj, SPLITS, axis=-1)
        y_a = chunked_spatial_gating(u_a, v_a, g_a_v[l], w_spatial[l], b_spatial[l])
        y_b = sliding_window_gqa(
            q_b.reshape(bn, s_len, B_HEADS, B_HEAD_DIM),
            k_b.reshape(bn, s_len, B_KV_HEADS, B_HEAD_DIM),
            v_b.reshape(bn, s_len, B_KV_HEADS, B_HEAD_DIM),
            g_b_q[l], g_b_k[l], sinks[l], positions)
        y_c = memory_cross_attention(
            q_c.reshape(bn, s_len, C_HEADS, C_HEAD_DIM), rmsnorm(mem, g_mem[l]),
            w_mem_kv[l], g_c_q[l], g_c_k[l])
        gate = jax.nn.sigmoid(gates.reshape(bn, s_len, N_BRANCH, D_MODEL))
        merged = (gate[:, :, 0] * (y_a @ w_branch_a[l])
                  + gate[:, :, 1] * (y_b @ w_branch_b[l])
                  + gate[:, :, 2] * (y_c @ w_branch_c[l]))
        x = x + merged @ w_out[l]
        x = x + gated_conv_ffn(rmsnorm(x, g_ffn[l]), w_up[l], conv_w[l], conv_b[l], w_down[l])
    return x


import jax as _jax
import jax.numpy as _jnp

TWIN_FORMAT = 'train_step'
FWD_PARAMS = ['x', 'mem', 'positions', 'g_mix', 'w_in', 'g_a_v', 'w_spatial', 'b_spatial', 'g_b_q', 'g_b_k', 'sinks', 'g_mem', 'w_mem_kv', 'g_c_q', 'g_c_k', 'w_branch_a', 'w_branch_b', 'w_branch_c', 'w_out', 'g_ffn', 'w_up', 'conv_w', 'conv_b', 'w_down']
TWIN_WEIGHTS = ['g_mix', 'w_in', 'g_a_v', 'w_spatial', 'b_spatial', 'g_b_q', 'g_b_k', 'sinks', 'g_mem', 'w_mem_kv', 'g_c_q', 'g_c_k', 'w_branch_a', 'w_branch_b', 'w_branch_c', 'w_out', 'g_ffn', 'w_up', 'conv_w', 'conv_b', 'w_down']
TWIN_DIFF_INPUT = 'x'
TWIN_INPUTS = ['x', 'mem', 'positions', 'g_mix', 'w_in', 'g_a_v', 'w_spatial', 'b_spatial', 'g_b_q', 'g_b_k', 'sinks', 'g_mem', 'w_mem_kv', 'g_c_q', 'g_c_k', 'w_branch_a', 'w_branch_b', 'w_branch_c', 'w_out', 'g_ffn', 'w_up', 'conv_w', 'conv_b', 'w_down', 'loss_target', 'm_g_mix', 'm_w_in', 'm_g_a_v', 'm_w_spatial', 'm_b_spatial', 'm_g_b_q', 'm_g_b_k', 'm_sinks', 'm_g_mem', 'm_w_mem_kv', 'm_g_c_q', 'm_g_c_k', 'm_w_branch_a', 'm_w_branch_b', 'm_w_branch_c', 'm_w_out', 'm_g_ffn', 'm_w_up', 'm_conv_w', 'm_conv_b', 'm_w_down', 'v_g_mix', 'v_w_in', 'v_g_a_v', 'v_w_spatial', 'v_b_spatial', 'v_g_b_q', 'v_g_b_k', 'v_sinks', 'v_g_mem', 'v_w_mem_kv', 'v_g_c_q', 'v_g_c_k', 'v_w_branch_a', 'v_w_branch_b', 'v_w_branch_c', 'v_w_out', 'v_g_ffn', 'v_w_up', 'v_conv_w', 'v_conv_b', 'v_w_down']
TWIN_OUTPUTS = ['loss', 'grad_x', 'grad_g_mix', 'grad_w_in', 'grad_g_a_v', 'grad_w_spatial', 'grad_b_spatial', 'grad_g_b_q', 'grad_g_b_k', 'grad_sinks', 'grad_g_mem', 'grad_w_mem_kv', 'grad_g_c_q', 'grad_g_c_k', 'grad_w_branch_a', 'grad_w_branch_b', 'grad_w_branch_c', 'grad_w_out', 'grad_g_ffn', 'grad_w_up', 'grad_conv_w', 'grad_conv_b', 'grad_w_down', 'delta_g_mix', 'delta_w_in', 'delta_g_a_v', 'delta_w_spatial', 'delta_b_spatial', 'delta_g_b_q', 'delta_g_b_k', 'delta_sinks', 'delta_g_mem', 'delta_w_mem_kv', 'delta_g_c_q', 'delta_g_c_k', 'delta_w_branch_a', 'delta_w_branch_b', 'delta_w_branch_c', 'delta_w_out', 'delta_g_ffn', 'delta_w_up', 'delta_conv_w', 'delta_conv_b', 'delta_w_down', 'new_m_g_mix', 'new_m_w_in', 'new_m_g_a_v', 'new_m_w_spatial', 'new_m_b_spatial', 'new_m_g_b_q', 'new_m_g_b_k', 'new_m_sinks', 'new_m_g_mem', 'new_m_w_mem_kv', 'new_m_g_c_q', 'new_m_g_c_k', 'new_m_w_branch_a', 'new_m_w_branch_b', 'new_m_w_branch_c', 'new_m_w_out', 'new_m_g_ffn', 'new_m_w_up', 'new_m_conv_w', 'new_m_conv_b', 'new_m_w_down', 'new_v_g_mix', 'new_v_w_in', 'new_v_g_a_v', 'new_v_w_spatial', 'new_v_b_spatial', 'new_v_g_b_q', 'new_v_g_b_k', 'new_v_sinks', 'new_v_g_mem', 'new_v_w_mem_kv', 'new_v_g_c_q', 'new_v_g_c_k', 'new_v_w_branch_a', 'new_v_w_branch_b', 'new_v_w_branch_c', 'new_v_w_out', 'new_v_g_ffn', 'new_v_w_up', 'new_v_conv_w', 'new_v_conv_b', 'new_v_w_down']
TWIN_LEAF_KINDS = {'loss': 'loss', 'grad_x': 'grad_x', 'grad_g_mix': 'grad_w', 'grad_w_in': 'grad_w', 'grad_g_a_v': 'grad_w', 'grad_w_spatial': 'grad_w', 'grad_b_spatial': 'grad_w', 'grad_g_b_q': 'grad_w', 'grad_g_b_k': 'grad_w', 'grad_sinks': 'grad_w', 'grad_g_mem': 'grad_w', 'grad_w_mem_kv': 'grad_w', 'grad_g_c_q': 'grad_w', 'grad_g_c_k': 'grad_w', 'grad_w_branch_a': 'grad_w', 'grad_w_branch_b': 'grad_w', 'grad_w_branch_c': 'grad_w', 'grad_w_out': 'grad_w', 'grad_g_ffn': 'grad_w', 'grad_w_up': 'grad_w', 'grad_conv_w': 'grad_w', 'grad_conv_b': 'grad_w', 'grad_w_down': 'grad_w', 'delta_g_mix': 'delta_w', 'delta_w_in': 'delta_w', 'delta_g_a_v': 'delta_w', 'delta_w_spatial': 'delta_w', 'delta_b_spatial': 'delta_w', 'delta_g_b_q': 'delta_w', 'delta_g_b_k': 'delta_w', 'delta_sinks': 'delta_w', 'delta_g_mem': 'delta_w', 'delta_w_mem_kv': 'delta_w', 'delta_g_c_q': 'delta_w', 'delta_g_c_k': 'delta_w', 'delta_w_branch_a': 'delta_w', 'delta_w_branch_b': 'delta_w', 'delta_w_branch_c': 'delta_w', 'delta_w_out': 'delta_w', 'delta_g_ffn': 'delta_w', 'delta_w_up': 'delta_w', 'delta_conv_w': 'delta_w', 'delta_conv_b': 'delta_w', 'delta_w_down': 'delta_w', 'new_m_g_mix': 'new_m', 'new_m_w_in': 'new_m', 'new_m_g_a_v': 'new_m', 'new_m_w_spatial': 'new_m', 'new_m_b_spatial': 'new_m', 'new_m_g_b_q': 'new_m', 'new_m_g_b_k': 'new_m', 'new_m_sinks': 'new_m', 'new_m_g_mem': 'new_m', 'new_m_w_mem_kv': 'new_m', 'new_m_g_c_q': 'new_m', 'new_m_g_c_k': 'new_m', 'new_m_w_branch_a': 'new_m', 'new_m_w_branch_b': 'new_m', 'new_m_w_branch_c': 'new_m', 'new_m_w_out': 'new_m', 'new_m_g_ffn': 'new_m', 'new_m_w_up': 'new_m', 'new_m_conv_w': 'new_m', 'new_m_conv_b': 'new_m', 'new_m_w_down': 'new_m', 'new_v_g_mix': 'new_v', 'new_v_w_in': 'new_v', 'new_v_g_a_v': 'new_v', 'new_v_w_spatial': 'new_v', 'new_v_b_spatial': 'new_v', 'new_v_g_b_q': 'new_v', 'new_v_g_b_k': 'new_v', 'new_v_sinks': 'new_v', 'new_v_g_mem': 'new_v', 'new_v_w_mem_kv': 'new_v', 'new_v_g_c_q': 'new_v', 'new_v_g_c_k': 'new_v', 'new_v_w_branch_a': 'new_v', 'new_v_w_branch_b': 'new_v', 'new_v_w_branch_c': 'new_v', 'new_v_w_out': 'new_v', 'new_v_g_ffn': 'new_v', 'new_v_w_up': 'new_v', 'new_v_conv_w': 'new_v', 'new_v_conv_b': 'new_v', 'new_v_w_down': 'new_v'}


def _forward(args):
    return _fwd_reference(*[args[k] for k in FWD_PARAMS])


def _output_shape():
    out = _jax.eval_shape(lambda: _forward(_fwd_setup_inputs(0)))
    return out.shape, out.dtype

N_MICROBATCH = 1
ADAM_LR = 0.001
ADAM_B1 = 0.9
ADAM_B2 = 0.999
ADAM_EPS = 1e-08
ADAM_WD = 0.01
ADAM_STEP = 10
PER_EXAMPLE_BATCH_AXIS = {'x': 0, 'mem': 0, 'positions': 0, 'loss_target': 0}
SHARED_INPUTS = []
_WEIGHT_DTYPES = {'g_mix': _jnp.float32, 'w_in': _jnp.float32, 'g_a_v': _jnp.float32, 'w_spatial': _jnp.float32, 'b_spatial': _jnp.float32, 'g_b_q': _jnp.float32, 'g_b_k': _jnp.float32, 'sinks': _jnp.float32, 'g_mem': _jnp.float32, 'w_mem_kv': _jnp.float32, 'g_c_q': _jnp.float32, 'g_c_k': _jnp.float32, 'w_branch_a': _jnp.float32, 'w_branch_b': _jnp.float32, 'w_branch_c': _jnp.float32, 'w_out': _jnp.float32, 'g_ffn': _jnp.float32, 'w_up': _jnp.float32, 'conv_w': _jnp.float32, 'conv_b': _jnp.float32, 'w_down': _jnp.float32}
MOMENT_SCALE = {'g_mix': 1.943631e+00, 'w_in': 6.398380e-02, 'g_a_v': 2.070370e+00, 'w_spatial': 1.322434e+00, 'b_spatial': 4.244427e+00, 'g_b_q': 9.567296e-01, 'g_b_k': 9.509471e-01, 'sinks': 1.561891e-01, 'g_mem': 4.670687e-02, 'w_mem_kv': 5.943441e-02, 'g_c_q': 3.703142e-01, 'g_c_k': 3.714733e-01, 'w_branch_a': 4.244176e-01, 'w_branch_b': 2.392310e-02, 'w_branch_c': 4.236298e-02, 'w_out': 3.705119e-01, 'g_ffn': 6.549402e+00, 'w_up': 1.460713e-01, 'conv_w': 9.153376e-01, 'conv_b': 8.222339e-01, 'w_down': 8.851261e-02}


def _to_microbatches(a, axis):
    t = _jnp.moveaxis(a, axis, 0)
    t = t.reshape((N_MICROBATCH, t.shape[0] // N_MICROBATCH) + t.shape[1:])
    return _jnp.moveaxis(t, 1, axis + 1)


def setup_inputs(seed: int = 0) -> dict:
    inp = _fwd_setup_inputs(seed)
    key = _jax.random.fold_in(_jax.random.key(seed), 7919)
    shape, _ = _output_shape()
    out = dict(inp)
    out["loss_target"] = _jax.random.normal(_jax.random.fold_in(key, 0), shape, _jnp.float32)
    for i, name in enumerate(TWIN_WEIGHTS):
        w = inp[name].astype(_jnp.float32)
        if MOMENT_SCALE is None:
            s = _jnp.sqrt(_jnp.mean(_jnp.square(w)) + 1e-30)
        else:
            s = MOMENT_SCALE[name]
        km, kv = _jax.random.split(_jax.random.fold_in(key, i + 1))
        out[name] = w
        out["m_" + name] = s * _jax.random.normal(km, w.shape, _jnp.float32)
        out["v_" + name] = (s * s) * _jax.random.uniform(kv, w.shape, _jnp.float32, 0.5, 1.5)
    if N_MICROBATCH > 1:
        for name, axis in PER_EXAMPLE_BATCH_AXIS.items():
            out[name] = _to_microbatches(out[name], axis)
    return {'x': out['x'], 'mem': out['mem'], 'positions': out['positions'], 'g_mix': out['g_mix'], 'w_in': out['w_in'], 'g_a_v': out['g_a_v'], 'w_spatial': out['w_spatial'], 'b_spatial': out['b_spatial'], 'g_b_q': out['g_b_q'], 'g_b_k': out['g_b_k'], 'sinks': out['sinks'], 'g_mem': out['g_mem'], 'w_mem_kv': out['w_mem_kv'], 'g_c_q': out['g_c_q'], 'g_c_k': out['g_c_k'], 'w_branch_a': out['w_branch_a'], 'w_branch_b': out['w_branch_b'], 'w_branch_c': out['w_branch_c'], 'w_out': out['w_out'], 'g_ffn': out['g_ffn'], 'w_up': out['w_up'], 'conv_w': out['conv_w'], 'conv_b': out['conv_b'], 'w_down': out['w_down'], 'loss_target': out['loss_target'], 'm_g_mix': out['m_g_mix'], 'm_w_in': out['m_w_in'], 'm_g_a_v': out['m_g_a_v'], 'm_w_spatial': out['m_w_spatial'], 'm_b_spatial': out['m_b_spatial'], 'm_g_b_q': out['m_g_b_q'], 'm_g_b_k': out['m_g_b_k'], 'm_sinks': out['m_sinks'], 'm_g_mem': out['m_g_mem'], 'm_w_mem_kv': out['m_w_mem_kv'], 'm_g_c_q': out['m_g_c_q'], 'm_g_c_k': out['m_g_c_k'], 'm_w_branch_a': out['m_w_branch_a'], 'm_w_branch_b': out['m_w_branch_b'], 'm_w_branch_c': out['m_w_branch_c'], 'm_w_out': out['m_w_out'], 'm_g_ffn': out['m_g_ffn'], 'm_w_up': out['m_w_up'], 'm_conv_w': out['m_conv_w'], 'm_conv_b': out['m_conv_b'], 'm_w_down': out['m_w_down'], 'v_g_mix': out['v_g_mix'], 'v_w_in': out['v_w_in'], 'v_g_a_v': out['v_g_a_v'], 'v_w_spatial': out['v_w_spatial'], 'v_b_spatial': out['v_b_spatial'], 'v_g_b_q': out['v_g_b_q'], 'v_g_b_k': out['v_g_b_k'], 'v_sinks': out['v_sinks'], 'v_g_mem': out['v_g_mem'], 'v_w_mem_kv': out['v_w_mem_kv'], 'v_g_c_q': out['v_g_c_q'], 'v_g_c_k': out['v_g_c_k'], 'v_w_branch_a': out['v_w_branch_a'], 'v_w_branch_b': out['v_w_branch_b'], 'v_w_branch_c': out['v_w_branch_c'], 'v_w_out': out['v_w_out'], 'v_g_ffn': out['v_g_ffn'], 'v_w_up': out['v_w_up'], 'v_conv_w': out['v_conv_w'], 'v_conv_b': out['v_conv_b'], 'v_w_down': out['v_w_down']}


def _loss(weights, diff, rest, loss_target):
    with _jax.named_scope("forward"):
        args = {**rest, TWIN_DIFF_INPUT: diff, **{k: w.astype(_WEIGHT_DTYPES[k]) for k, w in weights.items()}}
        y = _forward(args)
    with _jax.named_scope("loss_head"):
        err = _jnp.square(y.astype(_jnp.float32) - loss_target)
        return 0.5 * _jnp.sum(_jnp.mean(err, axis=-1)) if err.ndim else 0.5 * err


def _adamw(w, g, m, v):
    m = ADAM_B1 * m + (1.0 - ADAM_B1) * g
    v = ADAM_B2 * v + (1.0 - ADAM_B2) * _jnp.square(g)
    m_hat = m / (1.0 - ADAM_B1 ** ADAM_STEP)
    v_hat = v / (1.0 - ADAM_B2 ** ADAM_STEP)
    delta = -ADAM_LR * (m_hat / (_jnp.sqrt(v_hat) + ADAM_EPS) + ADAM_WD * w)
    return delta, m, v


def reference(x, mem, positions, g_mix, w_in, g_a_v, w_spatial, b_spatial, g_b_q, g_b_k, sinks, g_mem, w_mem_kv, g_c_q, g_c_k, w_branch_a, w_branch_b, w_branch_c, w_out, g_ffn, w_up, conv_w, conv_b, w_down, loss_target, m_g_mix, m_w_in, m_g_a_v, m_w_spatial, m_b_spatial, m_g_b_q, m_g_b_k, m_sinks, m_g_mem, m_w_mem_kv, m_g_c_q, m_g_c_k, m_w_branch_a, m_w_branch_b, m_w_branch_c, m_w_out, m_g_ffn, m_w_up, m_conv_w, m_conv_b, m_w_down, v_g_mix, v_w_in, v_g_a_v, v_w_spatial, v_b_spatial, v_g_b_q, v_g_b_k, v_sinks, v_g_mem, v_w_mem_kv, v_g_c_q, v_g_c_k, v_w_branch_a, v_w_branch_b, v_w_branch_c, v_w_out, v_g_ffn, v_w_up, v_conv_w, v_conv_b, v_w_down):
    given = dict(x=x, mem=mem, positions=positions, g_mix=g_mix, w_in=w_in, g_a_v=g_a_v, w_spatial=w_spatial, b_spatial=b_spatial, g_b_q=g_b_q, g_b_k=g_b_k, sinks=sinks, g_mem=g_mem, w_mem_kv=w_mem_kv, g_c_q=g_c_q, g_c_k=g_c_k, w_branch_a=w_branch_a, w_branch_b=w_branch_b, w_branch_c=w_branch_c, w_out=w_out, g_ffn=g_ffn, w_up=w_up, conv_w=conv_w, conv_b=conv_b, w_down=w_down, loss_target=loss_target, m_g_mix=m_g_mix, m_w_in=m_w_in, m_g_a_v=m_g_a_v, m_w_spatial=m_w_spatial, m_b_spatial=m_b_spatial, m_g_b_q=m_g_b_q, m_g_b_k=m_g_b_k, m_sinks=m_sinks, m_g_mem=m_g_mem, m_w_mem_kv=m_w_mem_kv, m_g_c_q=m_g_c_q, m_g_c_k=m_g_c_k, m_w_branch_a=m_w_branch_a, m_w_branch_b=m_w_branch_b, m_w_branch_c=m_w_branch_c, m_w_out=m_w_out, m_g_ffn=m_g_ffn, m_w_up=m_w_up, m_conv_w=m_conv_w, m_conv_b=m_conv_b, m_w_down=m_w_down, v_g_mix=v_g_mix, v_w_in=v_w_in, v_g_a_v=v_g_a_v, v_w_spatial=v_w_spatial, v_b_spatial=v_b_spatial, v_g_b_q=v_g_b_q, v_g_b_k=v_g_b_k, v_sinks=v_sinks, v_g_mem=v_g_mem, v_w_mem_kv=v_w_mem_kv, v_g_c_q=v_g_c_q, v_g_c_k=v_g_c_k, v_w_branch_a=v_w_branch_a, v_w_branch_b=v_w_branch_b, v_w_branch_c=v_w_branch_c, v_w_out=v_w_out, v_g_ffn=v_g_ffn, v_w_up=v_w_up, v_conv_w=v_conv_w, v_conv_b=v_conv_b, v_w_down=v_w_down)
    weights = {n: given[n] for n in TWIN_WEIGHTS}
    shared = {n: given[n] for n in SHARED_INPUTS}
    per_example = {n: given[n] for n in ['x', 'mem', 'positions']}
    grad_fn = _jax.value_and_grad(_loss, argnums=(0, 1))

    def one_microbatch(ex, loss_target):
        ex = dict(ex)
        diff = ex.pop(TWIN_DIFF_INPUT)
        return grad_fn(weights, diff, {**shared, **ex}, loss_target)

    if N_MICROBATCH == 1:
        loss, (grad_w, grad_x) = one_microbatch(per_example, given["loss_target"])
    else:
        def body(carry, xs):
            loss_sum, grad_sum = carry
            l_k, (gw_k, gx_k) = one_microbatch(xs[0], xs[1])
            with _jax.named_scope("update"):
                return (loss_sum + l_k, _jax.tree.map(_jnp.add, grad_sum, gw_k)), gx_k

        init = (_jnp.zeros((), _jnp.float32), _jax.tree.map(_jnp.zeros_like, weights))
        (loss, grad_w), grad_x = _jax.lax.scan(body, init, (per_example, given["loss_target"]))
    with _jax.named_scope("update"):
        delta_w, new_m, new_v = {}, {}, {}
        for n in TWIN_WEIGHTS:
            delta_w[n], new_m[n], new_v[n] = _adamw(weights[n], grad_w[n], given["m_" + n], given["v_" + n])
    return (loss, grad_x, *[grad_w[n] for n in TWIN_WEIGHTS], *[delta_w[n] for n in TWIN_WEIGHTS],
            *[new_m[n] for n in TWIN_WEIGHTS], *[new_v[n] for n in TWIN_WEIGHTS])
```

```python
import functools

import jax
import jax.numpy as jnp
from jax import lax
from jax.experimental import pallas as pl
from jax.experimental.pallas import tpu as pltpu

F32 = jnp.float32
BF = jnp.bfloat16
EPS = 1e-6
NEG = -1e30

N_DEV = 8
CHUNK = 128
A_GROUPS = 4
A_WIDTH = 512
B_HEADS = 16
B_KV_HEADS = 2
B_HEAD_DIM = 64
B_WIDTH = 1024
B_KV_WIDTH = 128
ROPE_DIM = 16
ROPE_THETA = 500000.0
C_HEADS = 4
C_HEAD_DIM = 128
C_WIDTH = 512
GATE_OFF = 2 * A_WIDTH + B_WIDTH + 2 * B_KV_WIDTH + C_WIDTH

ADAM_LR = 0.001
ADAM_B1 = 0.9
ADAM_B2 = 0.999
ADAM_EPS = 1e-08
ADAM_WD = 0.01
ADAM_STEP = 10

VMEM_LIMIT = 48 * 1024 * 1024
MESH = pl.DeviceIdType.MESH


def _pick(n, prefs):
    for p in prefs:
        if p <= n and n % p == 0:
            return p
    return n


def _params(sem):
    return pltpu.CompilerParams(dimension_semantics=sem, vmem_limit_bytes=VMEM_LIMIT)


def _mm(a, b, mode, out_dtype, name, *, resid=None, b_stack=False, a_parts=0, b_parts=0, out_parts=0,
        out_stack=False, tm=1024, tn=1024, tk=1024):
    if mode == "nn":
        M = a.shape[-2]
        K = a.shape[-1] * max(a_parts, 1)
        N = b.shape[-1] * (N_DEV if b_stack else 1)
        dims = (((1,), (0,)), ((), ()))
    elif mode == "nt":
        M = a.shape[-2]
        K = a.shape[-1] * max(a_parts, 1)
        N = b.shape[-2]
        dims = (((1,), (1,)), ((), ()))
    else:
        K = a.shape[-2]
        M = a.shape[-1]
        N = b.shape[-1] * max(b_parts, 1)
        dims = (((0,), (0,)), ((), ()))
    if b_stack and mode == "nn":
        tn = b.shape[-1]
    if b_stack and mode == "nt":
        tk = b.shape[-1]
    if out_stack:
        tn = N // N_DEV
    tm, tn, tk = _pick(M, (tm,)), _pick(N, (tn,)), _pick(K, (tk,))
    if M % tm or N % tn or K % tk:
        raise ValueError(f"{name}: tiles {tm},{tn},{tk} do not divide {M},{N},{K}")
    nm, nn, nk = M // tm, N // tn, K // tk

    def parts_idx(t, ntile, parts):
        per = ntile // parts
        return t // per, t % per

    if mode in ("nn", "nt"):
        if a_parts:
            a_spec = pl.BlockSpec((None, tm, tk), lambda m, n, k: (parts_idx(k, nk, a_parts)[0], m, parts_idx(k, nk, a_parts)[1]))
        else:
            a_spec = pl.BlockSpec((tm, tk), lambda m, n, k: (m, k))
    else:
        a_spec = pl.BlockSpec((tk, tm), lambda m, n, k: (k, m))
    if mode == "nn":
        if b_stack:
            b_spec = pl.BlockSpec((None, tk, tn), lambda m, n, k: (n, k, 0))
        else:
            b_spec = pl.BlockSpec((tk, tn), lambda m, n, k: (k, n))
    elif mode == "nt":
        if b_stack:
            b_spec = pl.BlockSpec((None, tn, tk), lambda m, n, k: (k, n, 0))
        else:
            b_spec = pl.BlockSpec((tn, tk), lambda m, n, k: (n, k))
    else:
        if b_parts:
            b_spec = pl.BlockSpec((None, tk, tn), lambda m, n, k: (parts_idx(n, nn, b_parts)[0], k, parts_idx(n, nn, b_parts)[1]))
        else:
            b_spec = pl.BlockSpec((tk, tn), lambda m, n, k: (k, n))
    if out_stack:
        out_shape = jax.ShapeDtypeStruct((N_DEV, M, tn), out_dtype)
        o_spec = pl.BlockSpec((None, tm, tn), lambda m, n, k: (n, m, 0))
    elif out_parts:
        out_shape = jax.ShapeDtypeStruct((out_parts, M, N // out_parts), out_dtype)
        o_spec = pl.BlockSpec((None, tm, tn), lambda m, n, k: (parts_idx(n, nn, out_parts)[0], m, parts_idx(n, nn, out_parts)[1]))
    else:
        out_shape = jax.ShapeDtypeStruct((M, N), out_dtype)
        o_spec = pl.BlockSpec((tm, tn), lambda m, n, k: (m, n))
    has_resid = resid is not None

    def body(*refs):
        if has_resid:
            a_ref, b_ref, r_ref, o_ref, acc = refs
        else:
            a_ref, b_ref, o_ref, acc = refs
        k = pl.program_id(2)

        @pl.when(k == 0)
        def _():
            acc[...] = jnp.zeros_like(acc)

        acc[...] += lax.dot_general(a_ref[...], b_ref[...], dims, preferred_element_type=F32)

        @pl.when(k == nk - 1)
        def _():
            res = acc[...]
            if has_resid:
                res = res + r_ref[...]
            o_ref[...] = res.astype(o_ref.dtype)

    in_specs = [a_spec, b_spec]
    args = [a, b]
    if has_resid:
        in_specs.append(pl.BlockSpec((tm, tn), lambda m, n, k: (m, n)))
        args.append(resid)
    return pl.pallas_call(
        body, name=name, grid=(nm, nn, nk), in_specs=in_specs, out_specs=o_spec, out_shape=out_shape,
        scratch_shapes=[pltpu.VMEM((tm, tn), F32)],
        compiler_params=_params(("parallel", "parallel", "arbitrary")),
    )(*args)


def _rms_fwd(x, g, name):
    R, D = x.shape
    tr = _pick(R, (256,))

    def body(x_ref, g_ref, h_ref, r_ref):
        xv = x_ref[...]
        r = lax.rsqrt(jnp.mean(xv * xv, axis=-1, keepdims=True) + EPS)
        h_ref[...] = (xv * r * g_ref[...]).astype(BF)
        r_ref[...] = r

    return pl.pallas_call(
        body, name=name, grid=(R // tr,),
        in_specs=[pl.BlockSpec((tr, D), lambda i: (i, 0)), pl.BlockSpec((1, D), lambda i: (0, 0))],
        out_specs=[pl.BlockSpec((tr, D), lambda i: (i, 0)), pl.BlockSpec((tr, 1), lambda i: (i, 0))],
        out_shape=[jax.ShapeDtypeStruct((R, D), BF), jax.ShapeDtypeStruct((R, 1), F32)],
        compiler_params=_params(("parallel",)),
    )(x, g)


def _rms_bwd(x, r, g, dh, dres, name):
    R, D = x.shape
    tr = _pick(R, (256,))
    has_res = dres is not None

    def body(*refs):
        if has_res:
            x_ref, r_ref, g_ref, dh_ref, dres_ref, dx_ref, dxb_ref, dg_ref = refs
        else:
            x_ref, r_ref, g_ref, dh_ref, dx_ref, dxb_ref, dg_ref = refs
        i = pl.program_id(0)
        xv, rv, dhv = x_ref[...], r_ref[...], dh_ref[...]
        gy = dhv * g_ref[...]
        c = jnp.sum(xv * gy, axis=-1, keepdims=True)
        dx = rv * gy - xv * (rv * rv * rv) * (c * (1.0 / D))
        if has_res:
            dx = dx + dres_ref[...]
        dx_ref[...] = dx
        dxb_ref[...] = dx.astype(BF)
        part = jnp.sum(dhv * xv * rv, axis=0, keepdims=True)

        @pl.when(i == 0)
        def _():
            dg_ref[...] = part

        @pl.when(i > 0)
        def _():
            dg_ref[...] += part

    row = pl.BlockSpec((tr, D), lambda i: (i, 0))
    in_specs = [row, pl.BlockSpec((tr, 1), lambda i: (i, 0)), pl.BlockSpec((1, D), lambda i: (0, 0)), row]
    args = [x, r, g, dh]
    if has_res:
        in_specs.append(row)
        args.append(dres)
    return pl.pallas_call(
        body, name=name, grid=(R // tr,), in_specs=in_specs,
        out_specs=[row, row, pl.BlockSpec((1, D), lambda i: (0, 0))],
        out_shape=[jax.ShapeDtypeStruct((R, D), F32), jax.ShapeDtypeStruct((R, D), BF), jax.ShapeDtypeStruct((1, D), F32)],
        compiler_params=_params(("arbitrary",)),
    )(*args)


def _a_chunk(us, vs, gvs, ws, bs):
    r_i = lax.broadcasted_iota(jnp.int32, (CHUNK, CHUNK), 0)
    c_i = lax.broadcasted_iota(jnp.int32, (CHUNK, CHUNK), 1)
    causal = r_i >= c_i
    vg = [jax.nn.gelu(v) for v in vs]
    ss = sum(jnp.sum(v * v, axis=-1, keepdims=True) for v in vg)
    r = lax.rsqrt(ss * (1.0 / A_WIDTH) + EPS)
    ys = []
    for g in range(A_GROUPS):
        vn = vg[g] * r * gvs[g]
        w = jnp.where(causal, ws[g], 0.0)
        s = jnp.dot(w.astype(BF), vn.astype(BF), preferred_element_type=F32) + bs[g]
        ys.append(jax.nn.gelu(us[g]) * s)
    return ys


def _a_split(u_ref, v_ref, g_ref, w_ref, b_ref):
    sl = [slice(g * 128, (g + 1) * 128) for g in range(A_GROUPS)]
    return ([u_ref[:, s] for s in sl], [v_ref[:, s] for s in sl], [g_ref[:, s] for s in sl],
            [w_ref[g] for g in range(A_GROUPS)], [b_ref[:, g:g + 1] for g in range(A_GROUPS)])


def _a_specs(S):
    return [pl.BlockSpec((CHUNK, A_WIDTH), lambda n: (n, 0)), pl.BlockSpec((CHUNK, A_WIDTH), lambda n: (n, 1)),
            pl.BlockSpec((1, A_WIDTH), lambda n: (0, 0)), pl.BlockSpec((A_GROUPS, CHUNK, CHUNK), lambda n: (0, 0, 0)),
            pl.BlockSpec((CHUNK, A_GROUPS), lambda n: (0, 0))]


def _a_fwd(proj, g_v, w_s, b_t):
    S = proj.shape[0]

    def body(u_ref, v_ref, g_ref, w_ref, b_ref, y_ref):
        ys = _a_chunk(*_a_split(u_ref, v_ref, g_ref, w_ref, b_ref))
        for g in range(A_GROUPS):
            y_ref[:, g * 128:(g + 1) * 128] = ys[g].astype(BF)

    return pl.pallas_call(
        body, name="a_fwd", grid=(S // CHUNK,), in_specs=_a_specs(S),
        out_specs=pl.BlockSpec((CHUNK, A_WIDTH), lambda n: (n, 0)),
        out_shape=jax.ShapeDtypeStruct((S, A_WIDTH), BF), compiler_params=_params(("parallel",)),
    )(proj, proj, g_v, w_s, b_t)


def _a_bwd(proj, g_v, w_s, b_t, dy):
    S = proj.shape[0]

    def body(u_ref, v_ref, g_ref, w_ref, b_ref, dy_ref, duv_ref, dg_ref, dw_ref, db_ref):
        n = pl.program_id(0)
        dys = [dy_ref[:, g * 128:(g + 1) * 128] for g in range(A_GROUPS)]
        _, vjp = jax.vjp(_a_chunk, *_a_split(u_ref, v_ref, g_ref, w_ref, b_ref))
        dus, dvs, dgs, dws, dbs = vjp(dys)

        @pl.when(n == 0)
        def _():
            dg_ref[...] = jnp.zeros_like(dg_ref)
            dw_ref[...] = jnp.zeros_like(dw_ref)
            db_ref[...] = jnp.zeros_like(db_ref)

        for g in range(A_GROUPS):
            duv_ref[:, g * 128:(g + 1) * 128] = dus[g].astype(BF)
            duv_ref[:, A_WIDTH + g * 128:A_WIDTH + (g + 1) * 128] = dvs[g].astype(BF)
            dg_ref[:, g * 128:(g + 1) * 128] += dgs[g]
            dw_ref[g] += dws[g]
            db_ref[:, g:g + 1] += dbs[g]

    return pl.pallas_call(
        body, name="a_bwd", grid=(S // CHUNK,),
        in_specs=_a_specs(S) + [pl.BlockSpec((CHUNK, A_WIDTH), lambda n: (n, 0))],
        out_specs=[pl.BlockSpec((CHUNK, 2 * A_WIDTH), lambda n: (n, 0)), pl.BlockSpec((1, A_WIDTH), lambda n: (0, 0)),
                   pl.BlockSpec((A_GROUPS, CHUNK, CHUNK), lambda n: (0, 0, 0)), pl.BlockSpec((CHUNK, A_GROUPS), lambda n: (0, 0))],
        out_shape=[jax.ShapeDtypeStruct((S, 2 * A_WIDTH), BF), jax.ShapeDtypeStruct((1, A_WIDTH), F32),
                   jax.ShapeDtypeStruct((A_GROUPS, CHUNK, CHUNK), F32), jax.ShapeDtypeStruct((CHUNK, A_GROUPS), F32)],
        compiler_params=_params(("arbitrary",)),
    )(proj, proj, g_v, w_s, b_t, dy)


def _half_mask(shape, which):
    lane = lax.broadcasted_iota(jnp.int32, shape, len(shape) - 1)
    return (lane >= 64) == (which == 1)


def _pair_norm_rope(x, g, ct, sa, sb):
    lo = _half_mask(x.shape, 0)
    x2 = x * x
    ss_lo = jnp.sum(jnp.where(lo, x2, 0.0), axis=-1, keepdims=True)
    ss_hi = jnp.sum(jnp.where(lo, 0.0, x2), axis=-1, keepdims=True)
    r = jnp.where(lo, lax.rsqrt(ss_lo * (1.0 / B_HEAD_DIM) + EPS), lax.rsqrt(ss_hi * (1.0 / B_HEAD_DIM) + EPS))
    xr = x * r
    xn = xr * g
    out = xn * ct + pltpu.roll(xn, 120, 1) * sa + pltpu.roll(xn, 8, 1) * sb
    return out, xr, r


def _pair_norm_rope_bwd(x, g, ct, sa, sb, dout):
    lo = _half_mask(x.shape, 0)
    _, xr, r = _pair_norm_rope(x, g, ct, sa, sb)
    dxn = dout * ct + pltpu.roll(dout * sa, 8, 1) + pltpu.roll(dout * sb, 120, 1)
    gy = dxn * g
    t = xr * gy
    c_lo = jnp.sum(jnp.where(lo, t, 0.0), axis=-1, keepdims=True)
    c_hi = jnp.sum(jnp.where(lo, 0.0, t), axis=-1, keepdims=True)
    c = jnp.where(lo, c_lo, c_hi)
    dx = r * (gy - xr * c * (1.0 / B_HEAD_DIM))
    dg = jnp.sum(dxn * xr, axis=0, keepdims=True)
    return dx, dg


def _b_pre(proj, gq2, gk2, ct, sa, sb):
    S = proj.shape[0]
    tr = _pick(S, (256,))
    n_pair = B_WIDTH // 128

    def body(q_ref, k_ref, gq_ref, gk_ref, ct_ref, sa_ref, sb_ref, qn_ref, kn_ref):
        ct_v, sa_v, sb_v = ct_ref[...], sa_ref[...], sb_ref[...]
        for p in range(n_pair):
            o, _, _ = _pair_norm_rope(q_ref[:, p * 128:(p + 1) * 128], gq_ref[...], ct_v, sa_v, sb_v)
            qn_ref[:, p * 128:(p + 1) * 128] = o.astype(BF)
        o, _, _ = _pair_norm_rope(k_ref[...], gk_ref[...], ct_v, sa_v, sb_v)
        kn_ref[...] = o.astype(BF)

    tab = pl.BlockSpec((tr, 128), lambda i: (i, 0))
    gsp = pl.BlockSpec((1, 128), lambda i: (0, 0))
    return pl.pallas_call(
        body, name="b_pre", grid=(S // tr,),
        in_specs=[pl.BlockSpec((tr, B_WIDTH), lambda i: (i, 1)), pl.BlockSpec((tr, 128), lambda i: (i, 2 * B_WIDTH // 128)),
                  gsp, gsp, tab, tab, tab],
        out_specs=[pl.BlockSpec((tr, B_WIDTH), lambda i: (i, 0)), tab],
        out_shape=[jax.ShapeDtypeStruct((S, B_WIDTH), BF), jax.ShapeDtypeStruct((S, 128), BF)],
        compiler_params=_params(("parallel",)),
    )(proj, proj, gq2, gk2, ct, sa, sb)


def _b_pre_bwd(proj, gq2, gk2, ct, sa, sb, dqn, dkn, dv):
    S = proj.shape[0]
    tr = _pick(S, (256,))
    n_pair = B_WIDTH // 128

    def body(q_ref, k_ref, gq_ref, gk_ref, ct_ref, sa_ref, sb_ref, dqn_ref, dkn_ref, dv_ref, dqkv_ref, dgq_ref, dgk_ref):
        i = pl.program_id(0)
        ct_v, sa_v, sb_v = ct_ref[...], sa_ref[...], sb_ref[...]
        dgq = jnp.zeros((1, 128), F32)
        for p in range(n_pair):
            sl = slice(p * 128, (p + 1) * 128)
            dx, dg = _pair_norm_rope_bwd(q_ref[:, sl], gq_ref[...], ct_v, sa_v, sb_v, dqn_ref[:, sl])
            dqkv_ref[:, sl] = dx.astype(BF)
            dgq = dgq + dg
        dx, dgk = _pair_norm_rope_bwd(k_ref[...], gk_ref[...], ct_v, sa_v, sb_v, dkn_ref[...])
        dqkv_ref[:, B_WIDTH:B_WIDTH + 128] = dx.astype(BF)
        dqkv_ref[:, B_WIDTH + 128:B_WIDTH + 256] = dv_ref[...].astype(BF)

        @pl.when(i == 0)
        def _():
            dgq_ref[...] = dgq
            dgk_ref[...] = dgk

        @pl.when(i > 0)
        def _():
            dgq_ref[...] += dgq
            dgk_ref[...] += dgk

    tab = pl.BlockSpec((tr, 128), lambda i: (i, 0))
    gsp = pl.BlockSpec((1, 128), lambda i: (0, 0))
    return pl.pallas_call(
        body, name="b_pre_bwd", grid=(S // tr,),
        in_specs=[pl.BlockSpec((tr, B_WIDTH), lambda i: (i, 1)), pl.BlockSpec((tr, 128), lambda i: (i, 2 * B_WIDTH // 128)),
                  gsp, gsp, tab, tab, tab, pl.BlockSpec((tr, B_WIDTH), lambda i: (i, 0)), tab, tab],
        out_specs=[pl.BlockSpec((tr, B_WIDTH + 256), lambda i: (i, 0)), gsp, gsp],
        out_shape=[jax.ShapeDtypeStruct((S, B_WIDTH + 256), BF), jax.ShapeDtypeStruct((1, 128), F32), jax.ShapeDtypeStruct((1, 128), F32)],
        compiler_params=_params(("arbitrary",)),
    )(proj, proj, gq2, gk2, ct, sa, sb, dqn, dkn, dv)


def _b_dup(x2, g):
    d = jnp.where(_half_mask(x2.shape, g), x2, 0.0)
    return (d + pltpu.roll(d, 64, 1)).astype(BF)


def _b_valid(n):
    row = lax.broadcasted_iota(jnp.int32, (CHUNK, 2 * CHUNK), 0)
    col = lax.broadcasted_iota(jnp.int32, (CHUNK, 2 * CHUNK), 1)
    rel = row + CHUNK - col
    return (rel >= 0) & (rel < CHUNK) & ((col >= CHUNK) | (n > 0))


def _b_probs(qm, kd, valid, sink):
    s = lax.dot_general(qm, kd, (((1,), (1,)), ((), ())), preferred_element_type=F32) * (B_HEAD_DIM ** -0.5)
    s = jnp.where(valid, s, NEG)
    m = jnp.maximum(jnp.max(s, axis=-1, keepdims=True), sink)
    e = jnp.exp(s - m)
    es = jnp.exp(sink - m)
    inv = 1.0 / (jnp.sum(e, axis=-1, keepdims=True) + es)
    return e * inv, es * inv


def _b_kv_specs(S):
    prev = lambda n: (jnp.maximum(n - 1, 0), 0)
    cur = lambda n: (n, 0)
    v_col = (2 * B_WIDTH + B_KV_WIDTH) // 128
    return [pl.BlockSpec((CHUNK, 128), prev), pl.BlockSpec((CHUNK, 128), cur),
            pl.BlockSpec((CHUNK, 128), lambda n: (jnp.maximum(n - 1, 0), v_col)), pl.BlockSpec((CHUNK, 128), lambda n: (n, v_col))]


def _b_attn_fwd(qn, kn, proj, sinks):
    S = qn.shape[0]

    def body(s_ref, q_ref, kp_ref, kc_ref, vp_ref, vc_ref, y_ref):
        n = pl.program_id(0)
        valid = _b_valid(n)
        k2 = jnp.concatenate([kp_ref[...], kc_ref[...]], axis=0).astype(F32)
        v2 = jnp.concatenate([vp_ref[...], vc_ref[...]], axis=0)
        for g in range(B_KV_HEADS):
            kd, vd = _b_dup(k2, g), _b_dup(v2, g)
            for pp in range(B_HEADS // B_KV_HEADS // 2):
                p = g * (B_HEADS // B_KV_HEADS // 2) + pp
                q_pair = q_ref[:, p * 128:(p + 1) * 128]
                o_pair = jnp.zeros((CHUNK, 128), F32)
                for hf in range(2):
                    hm = _half_mask((CHUNK, 128), hf)
                    qm = jnp.where(hm, q_pair, jnp.zeros_like(q_pair))
                    pr, _ = _b_probs(qm, kd, valid, s_ref[0, 2 * p + hf])
                    o = jnp.dot(pr.astype(BF), vd, preferred_element_type=F32)
                    o_pair = o_pair + jnp.where(hm, o, 0.0)
                y_ref[:, p * 128:(p + 1) * 128] = o_pair.astype(BF)

    return pl.pallas_call(
        body, name="b_attn_fwd", grid=(S // CHUNK,),
        in_specs=[pl.BlockSpec(memory_space=pltpu.SMEM), pl.BlockSpec((CHUNK, B_WIDTH), lambda n: (n, 0))] + _b_kv_specs(S),
        out_specs=pl.BlockSpec((CHUNK, B_WIDTH), lambda n: (n, 0)),
        out_shape=jax.ShapeDtypeStruct((S, B_WIDTH), BF), compiler_params=_params(("arbitrary",)),
    )(sinks, qn, kn, kn, proj, proj)


def _b_attn_bwd(qn, kn, proj, sinks, dy):
    S = qn.shape[0]

    def body(s_ref, q_ref, kp_ref, kc_ref, vp_ref, vc_ref, dy_ref, dq_ref, dk_ref, dv_ref, ds_ref):
        n = pl.program_id(0)

        @pl.when(n == 0)
        def _():
            dk_ref[...] = jnp.zeros_like(dk_ref)
            dv_ref[...] = jnp.zeros_like(dv_ref)
            ds_ref[...] = jnp.zeros_like(ds_ref)

        valid = _b_valid(n)
        k2 = jnp.concatenate([kp_ref[...], kc_ref[...]], axis=0).astype(F32)
        v2 = jnp.concatenate([vp_ref[...], vc_ref[...]], axis=0)
        lane = lax.broadcasted_iota(jnp.int32, (CHUNK, 128), 1)
        dk2 = jnp.zeros((2 * CHUNK, 128), F32)
        dv2 = jnp.zeros((2 * CHUNK, 128), F32)
        dsink = jnp.zeros((CHUNK, 128), F32)
        scale = B_HEAD_DIM ** -0.5
        for g in range(B_KV_HEADS):
            kd, vd = _b_dup(k2, g), _b_dup(v2, g)
            dk_acc = jnp.zeros((2 * CHUNK, 128), F32)
            dv_acc = jnp.zeros((2 * CHUNK, 128), F32)
            for pp in range(B_HEADS // B_KV_HEADS // 2):
                p = g * (B_HEADS // B_KV_HEADS // 2) + pp
                q_pair = q_ref[:, p * 128:(p + 1) * 128]
                do_pair = dy_ref[:, p * 128:(p + 1) * 128]
                dq_pair = jnp.zeros((CHUNK, 128), F32)
                for hf in range(2):
                    h = 2 * p + hf
                    hm = _half_mask((CHUNK, 128), hf)
                    qm = jnp.where(hm, q_pair, jnp.zeros_like(q_pair))
                    do = jnp.where(hm, do_pair, 0.0)
                    do_b = do.astype(BF)
                    pr, ps = _b_probs(qm, kd, valid, s_ref[0, h])
                    pr_b = pr.astype(BF)
                    o = jnp.dot(pr_b, vd, preferred_element_type=F32)
                    delta = jnp.sum(do * o, axis=-1, keepdims=True)
                    dp = lax.dot_general(do_b, vd, (((1,), (1,)), ((), ())), preferred_element_type=F32)
                    dsc = (pr * (dp - delta) * scale).astype(BF)
                    dsink = dsink + jnp.where(lane == h, -ps * delta, 0.0)
                    dq = jnp.dot(dsc, kd, preferred_element_type=F32)
                    dq_pair = dq_pair + jnp.where(hm, dq, 0.0)
                    dk_acc = dk_acc + lax.dot_general(dsc, qm, (((0,), (0,)), ((), ())), preferred_element_type=F32)
                    dv_acc = dv_acc + lax.dot_general(pr_b, do_b, (((0,), (0,)), ((), ())), preferred_element_type=F32)
                dq_ref[:, p * 128:(p + 1) * 128] = dq_pair
            gm = _half_mask((2 * CHUNK, 128), g)
            dk2 = dk2 + jnp.where(gm, dk_acc + pltpu.roll(dk_acc, 64, 1), 0.0)
            dv2 = dv2 + jnp.where(gm, dv_acc + pltpu.roll(dv_acc, 64, 1), 0.0)
        ds_ref[...] += dsink
        cur = pl.ds(pl.multiple_of(n * CHUNK, CHUNK), CHUNK)
        dk_ref[cur, :] += dk2[CHUNK:]
        dv_ref[cur, :] += dv2[CHUNK:]

        @pl.when(n > 0)
        def _():
            prv = pl.ds(pl.multiple_of((n - 1) * CHUNK, CHUNK), CHUNK)
            dk_ref[prv, :] += dk2[:CHUNK]
            dv_ref[prv, :] += dv2[:CHUNK]

    full = pl.BlockSpec((S, 128), lambda n: (0, 0))
    return pl.pallas_call(
        body, name="b_attn_bwd", grid=(S // CHUNK,),
        in_specs=[pl.BlockSpec(memory_space=pltpu.SMEM), pl.BlockSpec((CHUNK, B_WIDTH), lambda n: (n, 0))] + _b_kv_specs(S)
        + [pl.BlockSpec((CHUNK, B_WIDTH), lambda n: (n, 0))],
        out_specs=[pl.BlockSpec((CHUNK, B_WIDTH), lambda n: (n, 0)), full, full, pl.BlockSpec((CHUNK, 128), lambda n: (0, 0))],
        out_shape=[jax.ShapeDtypeStruct((S, B_WIDTH), F32), jax.ShapeDtypeStruct((S, 128), F32), jax.ShapeDtypeStruct((S, 128), F32),
                   jax.ShapeDtypeStruct((CHUNK, 128), F32)],
        compiler_params=_params(("arbitrary",)),
    )(sinks, qn, kn, kn, proj, proj, dy)


def _c_block(q, k, v, gq, gk):
    qn = q * lax.rsqrt(jnp.mean(q * q, axis=-1, keepdims=True) + EPS) * gq
    kn = k * lax.rsqrt(jnp.mean(k * k, axis=-1, keepdims=True) + EPS) * gk
    s = lax.dot_general(qn.astype(BF), kn.astype(BF), (((1,), (1,)), ((), ())), preferred_element_type=F32) * (C_HEAD_DIM ** -0.5)
    p = jax.nn.softmax(s, axis=-1)
    return jnp.dot(p.astype(BF), v.astype(BF), preferred_element_type=F32)


def _c_specs(S, M, tq):
    q_col = (2 * A_WIDTH + B_WIDTH + 2 * B_KV_WIDTH) // 128
    return [pl.BlockSpec((tq, 128), lambda h, i: (i, q_col + h)), pl.BlockSpec((M, 128), lambda h, i: (0, h)),
            pl.BlockSpec((M, 128), lambda h, i: (0, C_HEADS + h)), pl.BlockSpec((1, 128), lambda h, i: (0, 0)),
            pl.BlockSpec((1, 128), lambda h, i: (0, 0))]


def _c_fwd(proj, kv, gq, gk):
    S, M = proj.shape[0], kv.shape[0]
    tq = _pick(S, (512,))

    def body(q_ref, k_ref, v_ref, gq_ref, gk_ref, y_ref):
        y_ref[...] = _c_block(q_ref[...], k_ref[...], v_ref[...], gq_ref[...], gk_ref[...]).astype(BF)

    return pl.pallas_call(
        body, name="c_fwd", grid=(C_HEADS, S // tq), in_specs=_c_specs(S, M, tq),
        out_specs=pl.BlockSpec((tq, 128), lambda h, i: (i, h)),
        out_shape=jax.ShapeDtypeStruct((S, C_WIDTH), BF), compiler_params=_params(("parallel", "parallel")),
    )(proj, kv, kv, gq, gk)


def _c_bwd(proj, kv, gq, gk, dy):
    S, M = proj.shape[0], kv.shape[0]
    tq = _pick(S, (512,))

    def body(q_ref, k_ref, v_ref, gq_ref, gk_ref, dy_ref, dq_ref, dk_ref, dv_ref, dgq_ref, dgk_ref):
        i = pl.program_id(1)
        _, vjp = jax.vjp(_c_block, q_ref[...], k_ref[...], v_ref[...], gq_ref[...], gk_ref[...])
        dq, dk, dv, dgq, dgk = vjp(dy_ref[...])
        dq_ref[...] = dq.astype(BF)

        @pl.when(i == 0)
        def _():
            dk_ref[...] = dk
            dv_ref[...] = dv
            dgq_ref[...] = dgq
            dgk_ref[...] = dgk

        @pl.when(i > 0)
        def _():
            dk_ref[...] += dk
            dv_ref[...] += dv
            dgq_ref[...] += dgq
            dgk_ref[...] += dgk

    return pl.pallas_call(
        body, name="c_bwd", grid=(C_HEADS, S // tq),
        in_specs=_c_specs(S, M, tq) + [pl.BlockSpec((tq, 128), lambda h, i: (i, h))],
        out_specs=[pl.BlockSpec((tq, 128), lambda h, i: (i, h)), pl.BlockSpec((M, 128), lambda h, i: (0, h)),
                   pl.BlockSpec((M, 128), lambda h, i: (0, h)), pl.BlockSpec((None, 1, 128), lambda h, i: (h, 0, 0)),
                   pl.BlockSpec((None, 1, 128), lambda h, i: (h, 0, 0))],
        out_shape=[jax.ShapeDtypeStruct((S, C_WIDTH), BF), jax.ShapeDtypeStruct((M, C_WIDTH), F32), jax.ShapeDtypeStruct((M, C_WIDTH), F32),
                   jax.ShapeDtypeStruct((C_HEADS, 1, 128), F32), jax.ShapeDtypeStruct((C_HEADS, 1, 128), F32)],
        compiler_params=_params(("parallel", "arbitrary")),
    )(proj, kv, kv, gq, gk, dy)


def _merge_specs(S, D, tr, tc):
    off = GATE_OFF // tc
    nd = D // tc
    gates = [pl.BlockSpec((tr, tc), functools.partial(lambda b, i, j: (i, off + b * nd + j), b)) for b in range(3)]
    zs = [pl.BlockSpec((tr, tc), lambda i, j: (i, j)) for _ in range(3)]
    return gates + zs


def _merge_fwd(proj, za, zb, zc):
    S, D = za.shape
    tr, tc = _pick(S, (512,)), _pick(D, (256,))

    def body(ga_ref, gb_ref, gc_ref, za_ref, zb_ref, zc_ref, m_ref):
        acc = jax.nn.sigmoid(ga_ref[...]) * za_ref[...].astype(F32)
        acc = acc + jax.nn.sigmoid(gb_ref[...]) * zb_ref[...].astype(F32)
        acc = acc + jax.nn.sigmoid(gc_ref[...]) * zc_ref[...].astype(F32)
        m_ref[...] = acc.astype(BF)

    return pl.pallas_call(
        body, name="merge_fwd", grid=(S // tr, D // tc), in_specs=_merge_specs(S, D, tr, tc),
        out_specs=pl.BlockSpec((tr, tc), lambda i, j: (i, j)), out_shape=jax.ShapeDtypeStruct((S, D), BF),
        compiler_params=_params(("parallel", "parallel")),
    )(proj, proj, proj, za, zb, zc)


def _merge_bwd(proj, za, zb, zc, dm):
    S, D = za.shape
    tr, tc = _pick(S, (512,)), _pick(D, (256,))
    nd = D // tc

    def body(ga_ref, gb_ref, gc_ref, za_ref, zb_ref, zc_ref, dm_ref, dza_ref, dzb_ref, dzc_ref, dga_ref, dgb_ref, dgc_ref):
        dmv = dm_ref[...]
        for g_ref, z_ref, dz_ref, dg_ref in ((ga_ref, za_ref, dza_ref, dga_ref), (gb_ref, zb_ref, dzb_ref, dgb_ref),
                                             (gc_ref, zc_ref, dzc_ref, dgc_ref)):
            sg = jax.nn.sigmoid(g_ref[...])
            dz_ref[...] = (sg * dmv).astype(BF)
            dg_ref[...] = (dmv * z_ref[...].astype(F32) * sg * (1.0 - sg)).astype(BF)

    tile = pl.BlockSpec((tr, tc), lambda i, j: (i, j))
    return pl.pallas_call(
        body, name="merge_bwd", grid=(S // tr, D // tc), in_specs=_merge_specs(S, D, tr, tc) + [tile],
        out_specs=[tile, tile, tile, tile, tile, tile],
        out_shape=[jax.ShapeDtypeStruct((S, D), BF)] * 6,
        compiler_params=_params(("parallel", "parallel")),
    )(proj, proj, proj, za, zb, zc, dm)


def _shift_down(u, k):
    t = lax.broadcasted_iota(jnp.int32, u.shape, 0)
    return jnp.where(t >= k, pltpu.roll(u, k, 0), 0.0)


def _shift_up(u, k):
    n = u.shape[0]
    t = lax.broadcasted_iota(jnp.int32, u.shape, 0)
    return jnp.where(t < n - k, pltpu.roll(u, n - k, 0), 0.0)


def _conv3(u, w, b):
    return u * w[2:3] + _shift_down(u, 1) * w[1:2] + _shift_down(u, 2) * w[0:1] + b


def _ffn_specs(S, F, tc):
    return [pl.BlockSpec((2, S, tc), lambda j: (0, 0, j)), pl.BlockSpec((2, 3, tc), lambda j: (0, 0, j)),
            pl.BlockSpec((2, 1, tc), lambda j: (0, 0, j))]


def _ffn_act_fwd(up3, cw3, cb3):
    _, S, F = up3.shape
    tc = _pick(F, (128,))

    def body(u_ref, w_ref, b_ref, o_ref):
        ca = _conv3(u_ref[0], w_ref[0], b_ref[0])
        cb = _conv3(u_ref[1], w_ref[1], b_ref[1])
        o_ref[...] = (ca * jax.nn.sigmoid(ca) * cb).astype(BF)

    return pl.pallas_call(
        body, name="ffn_act_fwd", grid=(F // tc,), in_specs=_ffn_specs(S, F, tc),
        out_specs=pl.BlockSpec((S, tc), lambda j: (0, j)), out_shape=jax.ShapeDtypeStruct((S, F), BF),
        compiler_params=_params(("parallel",)),
    )(up3, cw3, cb3)


def _ffn_act_bwd(up3, cw3, cb3, dact):
    _, S, F = up3.shape
    tc = _pick(F, (128,))

    def body(u_ref, w_ref, b_ref, da_ref, du_ref, dw_ref, db_ref):
        ca = _conv3(u_ref[0], w_ref[0], b_ref[0])
        cb = _conv3(u_ref[1], w_ref[1], b_ref[1])
        sg = jax.nn.sigmoid(ca)
        dav = da_ref[...]
        dcs = (dav * cb * sg * (1.0 + ca * (1.0 - sg)), dav * ca * sg)
        for part in range(2):
            dc, w, u = dcs[part], w_ref[part], u_ref[part]
            du_ref[part] = (dc * w[2:3] + _shift_up(dc, 1) * w[1:2] + _shift_up(dc, 2) * w[0:1]).astype(BF)
            dw_ref[part, 2:3, :] = jnp.sum(dc * u, axis=0, keepdims=True)
            dw_ref[part, 1:2, :] = jnp.sum(dc * _shift_down(u, 1), axis=0, keepdims=True)
            dw_ref[part, 0:1, :] = jnp.sum(dc * _shift_down(u, 2), axis=0, keepdims=True)
            db_ref[part] = jnp.sum(dc, axis=0, keepdims=True)

    return pl.pallas_call(
        body, name="ffn_act_bwd", grid=(F // tc,),
        in_specs=_ffn_specs(S, F, tc) + [pl.BlockSpec((S, tc), lambda j: (0, j))],
        out_specs=[pl.BlockSpec((2, S, tc), lambda j: (0, 0, j)), pl.BlockSpec((2, 3, tc), lambda j: (0, 0, j)),
                   pl.BlockSpec((2, 1, tc), lambda j: (0, 0, j))],
        out_shape=[jax.ShapeDtypeStruct((2, S, F), BF), jax.ShapeDtypeStruct((2, 3, F), F32), jax.ShapeDtypeStruct((2, 1, F), F32)],
        compiler_params=_params(("parallel",)),
    )(up3, cw3, cb3, dact)


def _loss(y, target):
    S, D = y.shape
    tr = _pick(S, (256,))

    def body(y_ref, t_ref, dy_ref, dyb_ref, l_ref):
        i = pl.program_id(0)
        e = y_ref[...] - t_ref[...]
        dy = e * (1.0 / D)
        dy_ref[...] = dy
        dyb_ref[...] = dy.astype(BF)
        part = jnp.sum(jnp.sum(e * e, axis=-1, keepdims=True), axis=0, keepdims=True) * (0.5 / D)

        @pl.when(i == 0)
        def _():
            l_ref[...] = jnp.zeros_like(l_ref)

        l_ref[...] += part

    row = pl.BlockSpec((tr, D), lambda i: (i, 0))
    return pl.pallas_call(
        body, name="loss", grid=(S // tr,), in_specs=[row, row],
        out_specs=[row, row, pl.BlockSpec((8, 128), lambda i: (0, 0))],
        out_shape=[jax.ShapeDtypeStruct((S, D), F32), jax.ShapeDtypeStruct((S, D), BF), jax.ShapeDtypeStruct((8, 128), F32)],
        compiler_params=_params(("arbitrary",)),
    )(y, target)


ANY = pl.BlockSpec(memory_space=pl.ANY)


def _allgather(shards, name):
    n = len(shards)

    def body(*refs):
        ins, outs = refs[:n], refs[n:2 * n]
        send_sems, recv_sems, local_sems = refs[2 * n:]
        x, y, c = lax.axis_index("x"), lax.axis_index("y"), lax.axis_index("c")
        me, sibling = (x, y, c), (x, y, 1 - c)
        chips = [(1 - x, y), (x, 1 - y), (1 - x, 1 - y)]

        def blk(w, px, py, pc):
            return outs[w].at[4 * px + 2 * py + pc]

        def copy(w, k, block, to, src=None):
            return pltpu.make_async_remote_copy(
                src_ref=blk(w, *block) if src is None else src, dst_ref=blk(w, *block),
                send_sem=send_sems.at[w, k], recv_sem=recv_sems.at[w, k], device_id=to, device_id_type=MESH)

        started = []
        mine = []
        for w in range(n):
            mine.append(pltpu.make_async_copy(ins[w], blk(w, *me), local_sems.at[w]))
            mine[-1].start()
            first = [copy(w, 0, me, sibling, src=ins[w])]
            first += [copy(w, 1 + j, me, (*chip, c), src=ins[w]) for j, chip in enumerate(chips)]
            for cp in first:
                cp.start()
            started += first
        for w in range(n):
            for j, chip in enumerate(chips):
                copy(w, 1 + j, (*chip, c), me).wait_recv()
                fwd = copy(w, 4 + j, (*chip, c), sibling)
                fwd.start()
                started.append(fwd)
        for w in range(n):
            copy(w, 0, sibling, me).wait_recv()
            for j, chip in enumerate(chips):
                copy(w, 4 + j, (*chip, 1 - c), me).wait_recv()
        for cp in started:
            cp.wait_send()
        for cp in mine:
            cp.wait()

    outs = pl.pallas_call(
        body, name=name, in_specs=[ANY] * n, out_specs=[ANY] * n,
        out_shape=[jax.ShapeDtypeStruct((N_DEV,) + s.shape, s.dtype) for s in shards],
        scratch_shapes=[pltpu.SemaphoreType.DMA((n, 7)), pltpu.SemaphoreType.DMA((n, 7)), pltpu.SemaphoreType.DMA((n,))],
    )(*shards)
    return list(outs)


def _sibling_exchange(grads, name):
    n = len(grads)

    def body(*refs):
        ins, outs = refs[:n], refs[n:2 * n]
        send_sems, recv_sems = refs[2 * n:]
        x, y, c = lax.axis_index("x"), lax.axis_index("y"), lax.axis_index("c")
        copies = [pltpu.make_async_remote_copy(
            src_ref=ins[w].at[:, 1 - c], dst_ref=outs[w], send_sem=send_sems.at[w], recv_sem=recv_sems.at[w],
            device_id=(x, y, 1 - c), device_id_type=MESH) for w in range(n)]
        for cp in copies:
            cp.start()
        for cp in copies:
            cp.wait()

    outs = pl.pallas_call(
        body, name=name, in_specs=[ANY] * n, out_specs=[ANY] * n,
        out_shape=[jax.ShapeDtypeStruct((g.shape[0],) + g.shape[2:], g.dtype) for g in grads],
        scratch_shapes=[pltpu.SemaphoreType.DMA((n,)), pltpu.SemaphoreType.DMA((n,))],
    )(*grads)
    return list(outs)


def _chip_exchange(sums, name):
    n = len(sums)

    def body(*refs):
        ins, outs = refs[:n], refs[n:2 * n]
        send_sems, recv_sems = refs[2 * n:]
        x, y, c = lax.axis_index("x"), lax.axis_index("y"), lax.axis_index("c")
        chips = [(1 - x, y), (x, 1 - y), (1 - x, 1 - y)]
        copies = []
        for w in range(n):
            for k, (px, py) in enumerate(chips):
                copies.append(pltpu.make_async_remote_copy(
                    src_ref=ins[w].at[2 * px + py], dst_ref=outs[w].at[k], send_sem=send_sems.at[w, k],
                    recv_sem=recv_sems.at[w, k], device_id=(px, py, c), device_id_type=MESH))
        for cp in copies:
            cp.start()
        for cp in copies:
            cp.wait()

    outs = pl.pallas_call(
        body, name=name, in_specs=[ANY] * n, out_specs=[ANY] * n,
        out_shape=[jax.ShapeDtypeStruct((3,) + s.shape[1:], s.dtype) for s in sums],
        scratch_shapes=[pltpu.SemaphoreType.DMA((n, 3)), pltpu.SemaphoreType.DMA((n, 3))],
    )(*sums)
    return list(outs)


def _row_tile(r, c):
    want = max(8, (256 * 1024) // c)
    for t in (512, 256, 128, 64, 32, 16, 8):
        if t <= want and r % t == 0:
            return t
    return r


def _pair_add(g4, recv, core, name):
    _, _, r, c = g4.shape
    tr = _row_tile(r, c)

    def body(core_ref, a_ref, b_ref, o_ref):
        o_ref[...] = (a_ref[...].astype(F32) + b_ref[...].astype(F32)).astype(BF)

    return pl.pallas_call(
        body, name=name,
        grid_spec=pltpu.PrefetchScalarGridSpec(
            num_scalar_prefetch=1, grid=(4, r // tr),
            in_specs=[pl.BlockSpec((None, None, tr, c), lambda p, i, s: (p, s[0], i, 0)), pl.BlockSpec((None, tr, c), lambda p, i, s: (p, i, 0))],
            out_specs=pl.BlockSpec((None, tr, c), lambda p, i, s: (p, i, 0))),
        out_shape=jax.ShapeDtypeStruct((4, r, c), BF), compiler_params=_params(("parallel", "parallel")),
    )(core, g4, recv)


def _adam_math(w, g, m, v):
    m = ADAM_B1 * m + (1.0 - ADAM_B1) * g
    v = ADAM_B2 * v + (1.0 - ADAM_B2) * (g * g)
    m_hat = m / (1.0 - ADAM_B1 ** ADAM_STEP)
    v_hat = v / (1.0 - ADAM_B2 ** ADAM_STEP)
    delta = -ADAM_LR * (m_hat / (jnp.sqrt(v_hat) + ADAM_EPS) + ADAM_WD * w)
    return delta, m, v


def _adamw_big(sums, recv, chip, w, m, v, name):
    r, c = w.shape
    tr = _row_tile(r, c) // 2 if _row_tile(r, c) >= 16 else _row_tile(r, c)

    def body(chip_ref, s_ref, r_ref, w_ref, m_ref, v_ref, g_out, d_out, m_out, v_out):
        g = s_ref[...].astype(F32) + r_ref[0].astype(F32)
        g = g + r_ref[1].astype(F32)
        g = g + r_ref[2].astype(F32)
        delta, mn, vn = _adam_math(w_ref[...], g, m_ref[...], v_ref[...])
        g_out[...] = g
        d_out[...] = delta
        m_out[...] = mn
        v_out[...] = vn

    row = pl.BlockSpec((tr, c), lambda i, s: (i, 0))
    return pl.pallas_call(
        body, name=name,
        grid_spec=pltpu.PrefetchScalarGridSpec(
            num_scalar_prefetch=1, grid=(r // tr,),
            in_specs=[pl.BlockSpec((None, tr, c), lambda i, s: (s[0], i, 0)), pl.BlockSpec((3, tr, c), lambda i, s: (0, i, 0)), row, row, row],
            out_specs=[row, row, row, row]),
        out_shape=[jax.ShapeDtypeStruct((r, c), F32)] * 4, compiler_params=_params(("parallel",)),
    )(chip, sums, recv, w, m, v)


def _adamw_small(parts, w, m, v, name):
    R = w.shape[0]
    tr = _pick(R, (256, 128, 64, 32, 16, 8))

    def body(p_ref, w_ref, m_ref, v_ref, g_out, d_out, m_out, v_out):
        g = p_ref[0]
        for d in range(1, N_DEV):
            g = g + p_ref[d]
        delta, mn, vn = _adam_math(w_ref[...], g, m_ref[...], v_ref[...])
        g_out[...] = g
        d_out[...] = delta
        m_out[...] = mn
        v_out[...] = vn

    row = pl.BlockSpec((tr, 128), lambda i: (i, 0))
    return pl.pallas_call(
        body, name=name, grid=(R // tr,),
        in_specs=[pl.BlockSpec((N_DEV, tr, 128), lambda i: (0, i, 0)), row, row, row], out_specs=[row, row, row, row],
        out_shape=[jax.ShapeDtypeStruct((R, 128), F32)] * 4, compiler_params=_params(("parallel",)),
    )(parts, w, m, v)


def _adamw_plain(g, w, m, v, name):
    def body(g_ref, w_ref, m_ref, v_ref, d_out, m_out, v_out):
        delta, mn, vn = _adam_math(w_ref[...], g_ref[...], m_ref[...], v_ref[...])
        d_out[...] = delta
        m_out[...] = mn
        v_out[...] = vn

    return pl.pallas_call(body, name=name, out_shape=[jax.ShapeDtypeStruct(w.shape, F32)] * 3)(g, w, m, v)


def _pack(arrays):
    rows = []
    for a in arrays:
        flat = a.reshape(-1).astype(F32)
        n = flat.shape[0]
        padded = -(-n // 1024) * 1024
        rows.append(jnp.pad(flat, (0, padded - n)).reshape(padded // 128, 128))
    return jnp.concatenate(rows, axis=0)


def _unpack(packed, shapes):
    out, row = [], 0
    for s in shapes:
        n = 1
        for d in s:
            n *= d
        nrow = -(-n // 1024) * 8
        out.append(packed[row:row + nrow].reshape(-1)[:n].reshape(s))
        row += nrow
    return out


def kernel(x, mem, positions, g_mix, w_in, g_a_v, w_spatial, b_spatial, g_b_q, g_b_k, sinks, g_mem, w_mem_kv, g_c_q, g_c_k, w_branch_a, w_branch_b, w_branch_c, w_out, g_ffn, w_up, conv_w, conv_b, w_down, loss_target, m_g_mix, m_w_in, m_g_a_v, m_w_spatial, m_b_spatial, m_g_b_q, m_g_b_k, m_sinks, m_g_mem, m_w_mem_kv, m_g_c_q, m_g_c_k, m_w_branch_a, m_w_branch_b, m_w_branch_c, m_w_out, m_g_ffn, m_w_up, m_conv_w, m_conv_b, m_w_down, v_g_mix, v_w_in, v_g_a_v, v_w_spatial, v_b_spatial, v_g_b_q, v_g_b_k, v_sinks, v_g_mem, v_w_mem_kv, v_g_c_q, v_g_c_k, v_w_branch_a, v_w_branch_b, v_w_branch_c, v_w_out, v_g_ffn, v_w_up, v_conv_w, v_conv_b, v_w_down):
    S, D = x.shape[1], x.shape[2]
    M = mem.shape[1]
    F = w_down.shape[1] * N_DEV
    in_cols = w_in.shape[2] * N_DEV
    ax, ay, ac = lax.axis_index("x"), lax.axis_index("y"), lax.axis_index("c")
    core = jnp.reshape(ac, (1,)).astype(jnp.int32)
    chip = jnp.reshape(2 * ax + ay, (1,)).astype(jnp.int32)
    me = 4 * ax + 2 * ay + ac

    x2, mem2, tgt2 = x[0], mem[0], loss_target[0]

    big = dict(w_in=w_in[0], w_mem_kv=w_mem_kv[0], w_branch_a=w_branch_a[0], w_branch_b=w_branch_b[0],
               w_branch_c=w_branch_c[0], w_out=w_out[0], w_up=w_up[0], w_down=w_down[0])
    names = list(big)
    gathered = _allgather([big[k].astype(BF) for k in names] + [conv_w[0]], "ag_weights")
    W = dict(zip(names, gathered[:-1]))
    w_in_f = W["w_in"].transpose(1, 0, 2).reshape(D, in_cols)
    w_kv_f = W["w_mem_kv"].reshape(D, 2 * C_WIDTH)
    w_out_f = W["w_out"].reshape(D, D)
    w_down_f = W["w_down"].reshape(F, D)
    cw3 = gathered[-1].reshape(2, N_DEV // 2, 3, 2 * F // N_DEV).transpose(0, 2, 1, 3).reshape(2, 3, F)
    cb3 = conv_b.reshape(2, 1, F)

    half = ROPE_DIM // 2
    inv = ROPE_THETA ** (-jnp.arange(half, dtype=F32) / half)
    ang = positions[0].astype(F32)[:, None] * inv
    cos, sin = jnp.cos(ang), jnp.sin(ang)
    one, zero = jnp.ones((S, B_HEAD_DIM - ROPE_DIM), F32), jnp.zeros((S, B_HEAD_DIM - ROPE_DIM), F32)
    z8 = jnp.zeros((S, half), F32)
    ct = jnp.tile(jnp.concatenate([cos, cos, one], axis=1), (1, 2))
    sa = jnp.tile(jnp.concatenate([-sin, z8, zero], axis=1), (1, 2))
    sb = jnp.tile(jnp.concatenate([z8, sin, zero], axis=1), (1, 2))
    gq2, gk2 = jnp.tile(g_b_q, (1, 2)), jnp.tile(g_b_k, (1, 2))
    b_t = b_spatial[0].T

    h, rstd1 = _rms_fwd(x2, g_mix, "rms1_fwd")
    proj = _mm(h, w_in_f, "nn", F32, "mm_proj", tn=1280)
    y_a = _a_fwd(proj, g_a_v, w_spatial[0], b_t)
    qn, kn = _b_pre(proj, gq2, gk2, ct, sa, sb)
    y_b = _b_attn_fwd(qn, kn, proj, sinks)
    mem_h, rstd_m = _rms_fwd(mem2, g_mem, "rmsmem_fwd")
    kv = _mm(mem_h, w_kv_f, "nn", F32, "mm_kv")
    y_c = _c_fwd(proj, kv, g_c_q, g_c_k)
    z_a = _mm(y_a, W["w_branch_a"], "nn", BF, "mm_za", b_stack=True)
    z_b = _mm(y_b, W["w_branch_b"], "nn", BF, "mm_zb", b_stack=True)
    z_c = _mm(y_c, W["w_branch_c"], "nn", BF, "mm_zc", b_stack=True)
    merged = _merge_fwd(proj, z_a, z_b, z_c)
    x1 = _mm(merged, w_out_f, "nn", F32, "mm_x1", resid=x2)
    h2, rstd2 = _rms_fwd(x1, g_ffn, "rms2_fwd")
    up3 = _mm(h2, W["w_up"], "nn", F32, "mm_up", b_stack=True, out_parts=2)
    act = _ffn_act_fwd(up3, cw3, cb3)
    y = _mm(act, w_down_f, "nn", F32, "mm_y", resid=x1, tk=512)
    dy, dy_b, loss_acc = _loss(y, tgt2)
    loss = lax.psum(loss_acc[0, 0], ("x", "y", "c"))

    d_act = _mm(dy_b, w_down_f, "nt", F32, "mm_dact", tn=1408)
    g_down = _mm(act, dy_b, "tn", BF, "mm_gdown", tm=1408, tk=512)
    d_up3, d_cw3, d_cb3 = _ffn_act_bwd(up3, cw3, cb3, d_act)
    d_h2 = _mm(d_up3, W["w_up"], "nt", F32, "mm_dh2", a_parts=2, b_stack=True)
    g_up = _mm(h2, d_up3, "tn", BF, "mm_gup", b_parts=2, out_stack=True, tk=512)
    dx1, dx1_b, d_g_ffn = _rms_bwd(x1, rstd2, g_ffn, d_h2, dy, "rms2_bwd")
    d_merged = _mm(dx1_b, w_out_f, "nt", F32, "mm_dmerged")
    g_out = _mm(merged, dx1_b, "tn", BF, "mm_gout", tk=512)
    dz_a, dz_b, dz_c, dga, dgb, dgc = _merge_bwd(proj, z_a, z_b, z_c, d_merged)
    dy_a = _mm(dz_a, W["w_branch_a"], "nt", F32, "mm_dya", b_stack=True)
    dy_b_ = _mm(dz_b, W["w_branch_b"], "nt", F32, "mm_dyb", b_stack=True)
    dy_c = _mm(dz_c, W["w_branch_c"], "nt", F32, "mm_dyc", b_stack=True)
    g_ba = _mm(y_a, dz_a, "tn", BF, "mm_gba", out_stack=True, tk=512)
    g_bb = _mm(y_b, dz_b, "tn", BF, "mm_gbb", out_stack=True, tk=512)
    g_bc = _mm(y_c, dz_c, "tn", BF, "mm_gbc", out_stack=True, tk=512)
    d_uv, d_g_a_v, d_w_s, d_b_t = _a_bwd(proj, g_a_v, w_spatial[0], b_t, dy_a)
    dqn, dkn, dv_b, dsink_rows = _b_attn_bwd(qn, kn, proj, sinks, dy_b_)
    d_qkv, d_gq2, d_gk2 = _b_pre_bwd(proj, gq2, gk2, ct, sa, sb, dqn, dkn, dv_b)
    dq_c, dk_c, dv_c, d_gcq, d_gck = _c_bwd(proj, kv, g_c_q, g_c_k, dy_c)
    dkv_b = jnp.concatenate([dk_c, dv_c], axis=1).astype(BF)
    d_memh = _mm(dkv_b, w_kv_f, "nt", F32, "mm_dmemh")
    g_kv = _mm(mem_h, dkv_b, "tn", BF, "mm_gkv", tk=512)
    _, _, d_g_mem = _rms_bwd(mem2, rstd_m, g_mem, d_memh, None, "rmsmem_bwd")
    dproj = jnp.concatenate([d_uv, d_qkv, dq_c, dga, dgb, dgc], axis=1)
    d_h = _mm(dproj, w_in_f, "nt", F32, "mm_dh", tk=1280)
    g_in = _mm(h, dproj, "tn", BF, "mm_gin", tn=1280, tk=512)
    grad_x, _, d_g_mix = _rms_bwd(x2, rstd1, g_mix, d_h, dx1, "rms1_bwd")

    c_in = in_cols // N_DEV
    stacked = [g_in.reshape(D, N_DEV, c_in).transpose(1, 0, 2), g_kv.reshape(N_DEV, D // N_DEV, 2 * C_WIDTH), g_ba, g_bb, g_bc,
               g_out.reshape(N_DEV, D // N_DEV, D), g_up, g_down.reshape(N_DEV, F // N_DEV, D)]
    g4 = [g.reshape(4, 2, g.shape[1], g.shape[2]) for g in stacked]
    from_sibling = _sibling_exchange(g4, "rs_sibling")
    chip_sums = [_pair_add(a, b, core, "rs_add_" + k) for k, a, b in zip(names, g4, from_sibling)]
    from_chips = _chip_exchange(chip_sums, "rs_chips")
    big_out = {}
    moments = dict(w_in=(m_w_in, v_w_in), w_mem_kv=(m_w_mem_kv, v_w_mem_kv), w_branch_a=(m_w_branch_a, v_w_branch_a),
                   w_branch_b=(m_w_branch_b, v_w_branch_b), w_branch_c=(m_w_branch_c, v_w_branch_c), w_out=(m_w_out, v_w_out),
                   w_up=(m_w_up, v_w_up), w_down=(m_w_down, v_w_down))
    for k, s, r in zip(names, chip_sums, from_chips):
        res = _adamw_big(s, r, chip, big[k], moments[k][0][0], moments[k][1][0], "adamw_" + k)
        big_out[k] = [a[None] for a in res]

    d_conv_w = d_cw3.reshape(2, 3, N_DEV // 2, 2 * F // N_DEV).transpose(1, 0, 2, 3).reshape(3, 2 * F)
    small_names = ["g_mix", "g_a_v", "w_spatial", "b_spatial", "g_b_q", "g_b_k", "sinks", "g_mem", "g_c_q", "g_c_k", "g_ffn", "conv_b"]
    small_w = dict(g_mix=g_mix, g_a_v=g_a_v, w_spatial=w_spatial, b_spatial=b_spatial, g_b_q=g_b_q, g_b_k=g_b_k, sinks=sinks,
                   g_mem=g_mem, g_c_q=g_c_q, g_c_k=g_c_k, g_ffn=g_ffn, conv_b=conv_b)
    small_m = dict(g_mix=m_g_mix, g_a_v=m_g_a_v, w_spatial=m_w_spatial, b_spatial=m_b_spatial, g_b_q=m_g_b_q, g_b_k=m_g_b_k,
                   sinks=m_sinks, g_mem=m_g_mem, g_c_q=m_g_c_q, g_c_k=m_g_c_k, g_ffn=m_g_ffn, conv_b=m_conv_b)
    small_v = dict(g_mix=v_g_mix, g_a_v=v_g_a_v, w_spatial=v_w_spatial, b_spatial=v_b_spatial, g_b_q=v_g_b_q, g_b_k=v_g_b_k,
                   sinks=v_sinks, g_mem=v_g_mem, g_c_q=v_g_c_q, g_c_k=v_g_c_k, g_ffn=v_g_ffn, conv_b=v_conv_b)
    small_g = dict(
        g_mix=d_g_mix, g_a_v=d_g_a_v, w_spatial=d_w_s, b_spatial=d_b_t.T,
        g_b_q=d_gq2.reshape(2, B_HEAD_DIM).sum(0), g_b_k=d_gk2.reshape(2, B_HEAD_DIM).sum(0),
        sinks=dsink_rows.sum(0)[:B_HEADS], g_mem=d_g_mem, g_c_q=d_gcq.sum(0), g_c_k=d_gck.sum(0), g_ffn=d_g_ffn,
        conv_b=d_cb3)
    cw_zero = jnp.zeros((3, 2 * F), F32)
    packed_g = _pack([small_g[k] for k in small_names] + [d_conv_w])
    packed_w = _pack([small_w[k] for k in small_names] + [cw_zero])
    packed_m = _pack([small_m[k] for k in small_names] + [cw_zero])
    packed_v = _pack([small_v[k] for k in small_names] + [cw_zero])
    parts = _allgather([packed_g], "ag_small")[0]
    sg, sd, sm, sv = _adamw_small(parts, packed_w, packed_m, packed_v, "adamw_small")
    shapes = [small_w[k].shape for k in small_names] + [(3, 2 * F)]
    sg_l, sd_l, sm_l, sv_l = (_unpack(p, shapes) for p in (sg, sd, sm, sv))
    small_out = {k: [sg_l[i], sd_l[i], sm_l[i], sv_l[i]] for i, k in enumerate(small_names)}
    c_cw = 2 * F // N_DEV
    g_cw = lax.dynamic_slice_in_dim(sg_l[-1], me * c_cw, c_cw, axis=1)
    cw_res = _adamw_plain(g_cw, conv_w[0], m_conv_w[0], v_conv_w[0], "adamw_conv_w")
    big_out["conv_w"] = [g_cw[None]] + [a[None] for a in cw_res]

    order = ["g_mix", "w_in", "g_a_v", "w_spatial", "b_spatial", "g_b_q", "g_b_k", "sinks", "g_mem", "w_mem_kv", "g_c_q", "g_c_k",
             "w_branch_a", "w_branch_b", "w_branch_c", "w_out", "g_ffn", "w_up", "conv_w", "conv_b", "w_down"]
    res = {**small_out, **big_out}
    outs = [loss, grad_x[None]]
    for field in range(4):
        outs += [res[k][field] for k in order]
    return tuple(outs)
```

```python
import functools

import jax
import jax.numpy as jnp
from jax import lax
from jax.experimental import pallas as pl
from jax.experimental.pallas import tpu as pltpu
from jax.experimental.pallas import tpu_sc as plsc

F32 = jnp.float32
BF = jnp.bfloat16
EPS = 1e-6
NEG = -1e30

N_DEV = 8
CHUNK = 128
A_GROUPS = 4
A_WIDTH = 512
B_HEADS = 16
B_KV_HEADS = 2
B_HEAD_DIM = 64
B_WIDTH = 1024
B_KV_WIDTH = 128
ROPE_DIM = 16
ROPE_THETA = 500000.0
C_HEADS = 4
C_HEAD_DIM = 128
C_WIDTH = 512
GATE_OFF = 2 * A_WIDTH + B_WIDTH + 2 * B_KV_WIDTH + C_WIDTH

ADAM_LR = 0.001
ADAM_B1 = 0.9
ADAM_B2 = 0.999
ADAM_EPS = 1e-08
ADAM_WD = 0.01
ADAM_STEP = 10

VMEM_LIMIT = 48 * 1024 * 1024
MESH = pl.DeviceIdType.MESH


def _pick(n, prefs):
    for p in prefs:
        if p <= n and n % p == 0:
            return p
    return n


def _params(sem):
    return pltpu.CompilerParams(dimension_semantics=sem, vmem_limit_bytes=VMEM_LIMIT)


def _mm(a, b, mode, out_dtype, name, *, resid=None, b_stack=False, a_parts=0, b_parts=0, out_parts=0,
        out_stack=False, tm=1024, tn=1024, tk=1024, after=()):
    if mode == "nn":
        M = a.shape[-2]
        K = a.shape[-1] * max(a_parts, 1)
        N = b.shape[-1] * (N_DEV if b_stack else 1)
        dims = (((1,), (0,)), ((), ()))
    elif mode == "nt":
        M = a.shape[-2]
        K = a.shape[-1] * max(a_parts, 1)
        N = b.shape[-2]
        dims = (((1,), (1,)), ((), ()))
    else:
        K = a.shape[-2]
        M = a.shape[-1]
        N = b.shape[-1] * max(b_parts, 1)
        dims = (((0,), (0,)), ((), ()))
    if b_stack and mode == "nn":
        tn = b.shape[-1]
    if b_stack and mode == "nt":
        tk = b.shape[-1]
    if out_stack:
        tn = N // N_DEV
    tm, tn, tk = _pick(M, (tm,)), _pick(N, (tn,)), _pick(K, (tk,))
    if M % tm or N % tn or K % tk:
        raise ValueError(f"{name}: tiles {tm},{tn},{tk} do not divide {M},{N},{K}")
    nm, nn, nk = M // tm, N // tn, K // tk

    def parts_idx(t, ntile, parts):
        per = ntile // parts
        return t // per, t % per

    if mode in ("nn", "nt"):
        if a_parts:
            a_spec = pl.BlockSpec((None, tm, tk), lambda m, n, k: (parts_idx(k, nk, a_parts)[0], m, parts_idx(k, nk, a_parts)[1]))
        else:
            a_spec = pl.BlockSpec((tm, tk), lambda m, n, k: (m, k))
    else:
        a_spec = pl.BlockSpec((tk, tm), lambda m, n, k: (k, m))
    if mode == "nn":
        if b_stack:
            b_spec = pl.BlockSpec((None, tk, tn), lambda m, n, k: (n, k, 0))
        else:
            b_spec = pl.BlockSpec((tk, tn), lambda m, n, k: (k, n))
    elif mode == "nt":
        if b_stack:
            b_spec = pl.BlockSpec((None, tn, tk), lambda m, n, k: (k, n, 0))
        else:
            b_spec = pl.BlockSpec((tn, tk), lambda m, n, k: (n, k))
    else:
        if b_parts:
            b_spec = pl.BlockSpec((None, tk, tn), lambda m, n, k: (parts_idx(n, nn, b_parts)[0], k, parts_idx(n, nn, b_parts)[1]))
        else:
            b_spec = pl.BlockSpec((tk, tn), lambda m, n, k: (k, n))
    if out_stack:
        out_shape = jax.ShapeDtypeStruct((N_DEV, M, tn), out_dtype)
        o_spec = pl.BlockSpec((None, tm, tn), lambda m, n, k: (n, m, 0))
    elif out_parts:
        out_shape = jax.ShapeDtypeStruct((out_parts, M, N // out_parts), out_dtype)
        o_spec = pl.BlockSpec((None, tm, tn), lambda m, n, k: (parts_idx(n, nn, out_parts)[0], m, parts_idx(n, nn, out_parts)[1]))
    else:
        out_shape = jax.ShapeDtypeStruct((M, N), out_dtype)
        o_spec = pl.BlockSpec((tm, tn), lambda m, n, k: (m, n))
    has_resid = resid is not None

    def body(*refs):
        o_ref, acc = refs[-2:]
        a_ref, b_ref = refs[:2]
        r_ref = refs[2] if has_resid else None
        k = pl.program_id(2)

        @pl.when(k == 0)
        def _():
            acc[...] = jnp.zeros_like(acc)

        acc[...] += lax.dot_general(a_ref[...], b_ref[...], dims, preferred_element_type=F32)

        @pl.when(k == nk - 1)
        def _():
            res = acc[...]
            if has_resid:
                res = res + r_ref[...]
            o_ref[...] = res.astype(o_ref.dtype)

    in_specs = [a_spec, b_spec]
    args = [a, b]
    if has_resid:
        in_specs.append(pl.BlockSpec((tm, tn), lambda m, n, k: (m, n)))
        args.append(resid)
    for t in after:
        in_specs.append(pl.BlockSpec(memory_space=pl.ANY))
        args.append(t)
    return pl.pallas_call(
        body, name=name, grid=(nm, nn, nk), in_specs=in_specs, out_specs=o_spec, out_shape=out_shape,
        scratch_shapes=[pltpu.VMEM((tm, tn), F32)],
        compiler_params=_params(("parallel", "parallel", "arbitrary")),
    )(*args)


def _rms_fwd(x, g, name):
    R, D = x.shape
    tr = _pick(R, (256,))

    def body(x_ref, g_ref, h_ref, r_ref):
        xv = x_ref[...]
        r = lax.rsqrt(jnp.mean(xv * xv, axis=-1, keepdims=True) + EPS)
        h_ref[...] = (xv * r * g_ref[...]).astype(BF)
        r_ref[...] = r

    return pl.pallas_call(
        body, name=name, grid=(R // tr,),
        in_specs=[pl.BlockSpec((tr, D), lambda i: (i, 0)), pl.BlockSpec((1, D), lambda i: (0, 0))],
        out_specs=[pl.BlockSpec((tr, D), lambda i: (i, 0)), pl.BlockSpec((tr, 1), lambda i: (i, 0))],
        out_shape=[jax.ShapeDtypeStruct((R, D), BF), jax.ShapeDtypeStruct((R, 1), F32)],
        compiler_params=_params(("parallel",)),
    )(x, g)


def _rms_bwd(x, r, g, dh, dres, name):
    R, D = x.shape
    tr = _pick(R, (256,))
    has_res = dres is not None

    def body(*refs):
        if has_res:
            x_ref, r_ref, g_ref, dh_ref, dres_ref, dx_ref, dxb_ref, dg_ref = refs
        else:
            x_ref, r_ref, g_ref, dh_ref, dx_ref, dxb_ref, dg_ref = refs
        i = pl.program_id(0)
        xv, rv, dhv = x_ref[...], r_ref[...], dh_ref[...]
        gy = dhv * g_ref[...]
        c = jnp.sum(xv * gy, axis=-1, keepdims=True)
        dx = rv * gy - xv * (rv * rv * rv) * (c * (1.0 / D))
        if has_res:
            dx = dx + dres_ref[...]
        dx_ref[...] = dx
        dxb_ref[...] = dx.astype(BF)
        part = jnp.sum(dhv * xv * rv, axis=0, keepdims=True)

        @pl.when(i == 0)
        def _():
            dg_ref[...] = part

        @pl.when(i > 0)
        def _():
            dg_ref[...] += part

    row = pl.BlockSpec((tr, D), lambda i: (i, 0))
    in_specs = [row, pl.BlockSpec((tr, 1), lambda i: (i, 0)), pl.BlockSpec((1, D), lambda i: (0, 0)), row]
    args = [x, r, g, dh]
    if has_res:
        in_specs.append(row)
        args.append(dres)
    return pl.pallas_call(
        body, name=name, grid=(R // tr,), in_specs=in_specs,
        out_specs=[row, row, pl.BlockSpec((1, D), lambda i: (0, 0))],
        out_shape=[jax.ShapeDtypeStruct((R, D), F32), jax.ShapeDtypeStruct((R, D), BF), jax.ShapeDtypeStruct((1, D), F32)],
        compiler_params=_params(("arbitrary",)),
    )(*args)


def _a_chunk(us, vs, gvs, ws, bs):
    r_i = lax.broadcasted_iota(jnp.int32, (CHUNK, CHUNK), 0)
    c_i = lax.broadcasted_iota(jnp.int32, (CHUNK, CHUNK), 1)
    causal = r_i >= c_i
    vg = [jax.nn.gelu(v) for v in vs]
    ss = sum(jnp.sum(v * v, axis=-1, keepdims=True) for v in vg)
    r = lax.rsqrt(ss * (1.0 / A_WIDTH) + EPS)
    ys = []
    for g in range(A_GROUPS):
        vn = vg[g] * r * gvs[g]
        w = jnp.where(causal, ws[g], 0.0)
        s = jnp.dot(w.astype(BF), vn.astype(BF), preferred_element_type=F32) + bs[g]
        ys.append(jax.nn.gelu(us[g]) * s)
    return ys


def _a_split(u_ref, v_ref, g_ref, w_ref, b_ref):
    sl = [slice(g * 128, (g + 1) * 128) for g in range(A_GROUPS)]
    return ([u_ref[:, s] for s in sl], [v_ref[:, s] for s in sl], [g_ref[:, s] for s in sl],
            [w_ref[g] for g in range(A_GROUPS)], [b_ref[:, g:g + 1] for g in range(A_GROUPS)])


def _a_specs(S):
    return [pl.BlockSpec((CHUNK, A_WIDTH), lambda n: (n, 0)), pl.BlockSpec((CHUNK, A_WIDTH), lambda n: (n, 1)),
            pl.BlockSpec((1, A_WIDTH), lambda n: (0, 0)), pl.BlockSpec((A_GROUPS, CHUNK, CHUNK), lambda n: (0, 0, 0)),
            pl.BlockSpec((CHUNK, A_GROUPS), lambda n: (0, 0))]


def _a_fwd(proj, g_v, w_s, b_t):
    S = proj.shape[0]

    def body(u_ref, v_ref, g_ref, w_ref, b_ref, y_ref):
        ys = _a_chunk(*_a_split(u_ref, v_ref, g_ref, w_ref, b_ref))
        for g in range(A_GROUPS):
            y_ref[:, g * 128:(g + 1) * 128] = ys[g].astype(BF)

    return pl.pallas_call(
        body, name="a_fwd", grid=(S // CHUNK,), in_specs=_a_specs(S),
        out_specs=pl.BlockSpec((CHUNK, A_WIDTH), lambda n: (n, 0)),
        out_shape=jax.ShapeDtypeStruct((S, A_WIDTH), BF), compiler_params=_params(("parallel",)),
    )(proj, proj, g_v, w_s, b_t)


def _a_bwd(proj, g_v, w_s, b_t, dy):
    S = proj.shape[0]

    def body(u_ref, v_ref, g_ref, w_ref, b_ref, dy_ref, duv_ref, dg_ref, dw_ref, db_ref):
        n = pl.program_id(0)
        dys = [dy_ref[:, g * 128:(g + 1) * 128] for g in range(A_GROUPS)]
        _, vjp = jax.vjp(_a_chunk, *_a_split(u_ref, v_ref, g_ref, w_ref, b_ref))
        dus, dvs, dgs, dws, dbs = vjp(dys)

        @pl.when(n == 0)
        def _():
            dg_ref[...] = jnp.zeros_like(dg_ref)
            dw_ref[...] = jnp.zeros_like(dw_ref)
            db_ref[...] = jnp.zeros_like(db_ref)

        for g in range(A_GROUPS):
            duv_ref[:, g * 128:(g + 1) * 128] = dus[g].astype(BF)
            duv_ref[:, A_WIDTH + g * 128:A_WIDTH + (g + 1) * 128] = dvs[g].astype(BF)
            dg_ref[:, g * 128:(g + 1) * 128] += dgs[g]
            dw_ref[g] += dws[g]
            db_ref[:, g:g + 1] += dbs[g]

    return pl.pallas_call(
        body, name="a_bwd", grid=(S // CHUNK,),
        in_specs=_a_specs(S) + [pl.BlockSpec((CHUNK, A_WIDTH), lambda n: (n, 0))],
        out_specs=[pl.BlockSpec((CHUNK, 2 * A_WIDTH), lambda n: (n, 0)), pl.BlockSpec((1, A_WIDTH), lambda n: (0, 0)),
                   pl.BlockSpec((A_GROUPS, CHUNK, CHUNK), lambda n: (0, 0, 0)), pl.BlockSpec((CHUNK, A_GROUPS), lambda n: (0, 0))],
        out_shape=[jax.ShapeDtypeStruct((S, 2 * A_WIDTH), BF), jax.ShapeDtypeStruct((1, A_WIDTH), F32),
                   jax.ShapeDtypeStruct((A_GROUPS, CHUNK, CHUNK), F32), jax.ShapeDtypeStruct((CHUNK, A_GROUPS), F32)],
        compiler_params=_params(("arbitrary",)),
    )(proj, proj, g_v, w_s, b_t, dy)


def _half_mask(shape, which):
    lane = lax.broadcasted_iota(jnp.int32, shape, len(shape) - 1)
    return (lane >= 64) == (which == 1)


def _pair_norm_rope(x, g, ct, sa, sb):
    lo = _half_mask(x.shape, 0)
    x2 = x * x
    ss_lo = jnp.sum(jnp.where(lo, x2, 0.0), axis=-1, keepdims=True)
    ss_hi = jnp.sum(jnp.where(lo, 0.0, x2), axis=-1, keepdims=True)
    r = jnp.where(lo, lax.rsqrt(ss_lo * (1.0 / B_HEAD_DIM) + EPS), lax.rsqrt(ss_hi * (1.0 / B_HEAD_DIM) + EPS))
    xr = x * r
    xn = xr * g
    out = xn * ct + pltpu.roll(xn, 120, 1) * sa + pltpu.roll(xn, 8, 1) * sb
    return out, xr, r


def _pair_norm_rope_bwd(x, g, ct, sa, sb, dout):
    lo = _half_mask(x.shape, 0)
    _, xr, r = _pair_norm_rope(x, g, ct, sa, sb)
    dxn = dout * ct + pltpu.roll(dout * sa, 8, 1) + pltpu.roll(dout * sb, 120, 1)
    gy = dxn * g
    t = xr * gy
    c_lo = jnp.sum(jnp.where(lo, t, 0.0), axis=-1, keepdims=True)
    c_hi = jnp.sum(jnp.where(lo, 0.0, t), axis=-1, keepdims=True)
    c = jnp.where(lo, c_lo, c_hi)
    dx = r * (gy - xr * c * (1.0 / B_HEAD_DIM))
    dg = jnp.sum(dxn * xr, axis=0, keepdims=True)
    return dx, dg


def _b_pre(proj, gq2, gk2, ct, sa, sb):
    S = proj.shape[0]
    tr = _pick(S, (256,))
    n_pair = B_WIDTH // 128

    def body(q_ref, k_ref, gq_ref, gk_ref, ct_ref, sa_ref, sb_ref, qn_ref, kn_ref):
        ct_v, sa_v, sb_v = ct_ref[...], sa_ref[...], sb_ref[...]
        for p in range(n_pair):
            o, _, _ = _pair_norm_rope(q_ref[:, p * 128:(p + 1) * 128], gq_ref[...], ct_v, sa_v, sb_v)
            qn_ref[:, p * 128:(p + 1) * 128] = o.astype(BF)
        o, _, _ = _pair_norm_rope(k_ref[...], gk_ref[...], ct_v, sa_v, sb_v)
        kn_ref[...] = o.astype(BF)

    tab = pl.BlockSpec((tr, 128), lambda i: (i, 0))
    gsp = pl.BlockSpec((1, 128), lambda i: (0, 0))
    return pl.pallas_call(
        body, name="b_pre", grid=(S // tr,),
        in_specs=[pl.BlockSpec((tr, B_WIDTH), lambda i: (i, 1)), pl.BlockSpec((tr, 128), lambda i: (i, 2 * B_WIDTH // 128)),
                  gsp, gsp, tab, tab, tab],
        out_specs=[pl.BlockSpec((tr, B_WIDTH), lambda i: (i, 0)), tab],
        out_shape=[jax.ShapeDtypeStruct((S, B_WIDTH), BF), jax.ShapeDtypeStruct((S, 128), BF)],
        compiler_params=_params(("parallel",)),
    )(proj, proj, gq2, gk2, ct, sa, sb)


def _b_pre_bwd(proj, gq2, gk2, ct, sa, sb, dqn, dkn, dv):
    S = proj.shape[0]
    tr = _pick(S, (256,))
    n_pair = B_WIDTH // 128

    def body(q_ref, k_ref, gq_ref, gk_ref, ct_ref, sa_ref, sb_ref, dqn_ref, dkn_ref, dv_ref, dqkv_ref, dgq_ref, dgk_ref):
        i = pl.program_id(0)
        ct_v, sa_v, sb_v = ct_ref[...], sa_ref[...], sb_ref[...]
        dgq = jnp.zeros((1, 128), F32)
        for p in range(n_pair):
            sl = slice(p * 128, (p + 1) * 128)
            dx, dg = _pair_norm_rope_bwd(q_ref[:, sl], gq_ref[...], ct_v, sa_v, sb_v, dqn_ref[:, sl])
            dqkv_ref[:, sl] = dx.astype(BF)
            dgq = dgq + dg
        dx, dgk = _pair_norm_rope_bwd(k_ref[...], gk_ref[...], ct_v, sa_v, sb_v, dkn_ref[...])
        dqkv_ref[:, B_WIDTH:B_WIDTH + 128] = dx.astype(BF)
        dqkv_ref[:, B_WIDTH + 128:B_WIDTH + 256] = dv_ref[...].astype(BF)

        @pl.when(i == 0)
        def _():
            dgq_ref[...] = dgq
            dgk_ref[...] = dgk

        @pl.when(i > 0)
        def _():
            dgq_ref[...] += dgq
            dgk_ref[...] += dgk

    tab = pl.BlockSpec((tr, 128), lambda i: (i, 0))
    gsp = pl.BlockSpec((1, 128), lambda i: (0, 0))
    return pl.pallas_call(
        body, name="b_pre_bwd", grid=(S // tr,),
        in_specs=[pl.BlockSpec((tr, B_WIDTH), lambda i: (i, 1)), pl.BlockSpec((tr, 128), lambda i: (i, 2 * B_WIDTH // 128)),
                  gsp, gsp, tab, tab, tab, pl.BlockSpec((tr, B_WIDTH), lambda i: (i, 0)), tab, tab],
        out_specs=[pl.BlockSpec((tr, B_WIDTH + 256), lambda i: (i, 0)), gsp, gsp],
        out_shape=[jax.ShapeDtypeStruct((S, B_WIDTH + 256), BF), jax.ShapeDtypeStruct((1, 128), F32), jax.ShapeDtypeStruct((1, 128), F32)],
        compiler_params=_params(("arbitrary",)),
    )(proj, proj, gq2, gk2, ct, sa, sb, dqn, dkn, dv)


def _b_dup(x2, g):
    d = jnp.where(_half_mask(x2.shape, g), x2, 0.0)
    return (d + pltpu.roll(d, 64, 1)).astype(BF)


def _b_valid(n):
    row = lax.broadcasted_iota(jnp.int32, (CHUNK, 2 * CHUNK), 0)
    col = lax.broadcasted_iota(jnp.int32, (CHUNK, 2 * CHUNK), 1)
    rel = row + CHUNK - col
    return (rel >= 0) & (rel < CHUNK) & ((col >= CHUNK) | (n > 0))


def _b_probs(qm, kd, valid, sink):
    s = lax.dot_general(qm, kd, (((1,), (1,)), ((), ())), preferred_element_type=F32) * (B_HEAD_DIM ** -0.5)
    s = jnp.where(valid, s, NEG)
    m = jnp.maximum(jnp.max(s, axis=-1, keepdims=True), sink)
    e = jnp.exp(s - m)
    es = jnp.exp(sink - m)
    inv = 1.0 / (jnp.sum(e, axis=-1, keepdims=True) + es)
    return e * inv, es * inv


def _b_kv_specs(S):
    prev = lambda n: (jnp.maximum(n - 1, 0), 0)
    cur = lambda n: (n, 0)
    v_col = (2 * B_WIDTH + B_KV_WIDTH) // 128
    return [pl.BlockSpec((CHUNK, 128), prev), pl.BlockSpec((CHUNK, 128), cur),
            pl.BlockSpec((CHUNK, 128), lambda n: (jnp.maximum(n - 1, 0), v_col)), pl.BlockSpec((CHUNK, 128), lambda n: (n, v_col))]


def _b_attn_fwd(qn, kn, proj, sinks):
    S = qn.shape[0]

    def body(s_ref, q_ref, kp_ref, kc_ref, vp_ref, vc_ref, y_ref):
        n = pl.program_id(0)
        valid = _b_valid(n)
        k2 = jnp.concatenate([kp_ref[...], kc_ref[...]], axis=0).astype(F32)
        v2 = jnp.concatenate([vp_ref[...], vc_ref[...]], axis=0)
        for g in range(B_KV_HEADS):
            kd, vd = _b_dup(k2, g), _b_dup(v2, g)
            for pp in range(B_HEADS // B_KV_HEADS // 2):
                p = g * (B_HEADS // B_KV_HEADS // 2) + pp
                q_pair = q_ref[:, p * 128:(p + 1) * 128]
                o_pair = jnp.zeros((CHUNK, 128), F32)
                for hf in range(2):
                    hm = _half_mask((CHUNK, 128), hf)
                    qm = jnp.where(hm, q_pair, jnp.zeros_like(q_pair))
                    pr, _ = _b_probs(qm, kd, valid, s_ref[0, 2 * p + hf])
                    o = jnp.dot(pr.astype(BF), vd, preferred_element_type=F32)
                    o_pair = o_pair + jnp.where(hm, o, 0.0)
                y_ref[:, p * 128:(p + 1) * 128] = o_pair.astype(BF)

    return pl.pallas_call(
        body, name="b_attn_fwd", grid=(S // CHUNK,),
        in_specs=[pl.BlockSpec(memory_space=pltpu.SMEM), pl.BlockSpec((CHUNK, B_WIDTH), lambda n: (n, 0))] + _b_kv_specs(S),
        out_specs=pl.BlockSpec((CHUNK, B_WIDTH), lambda n: (n, 0)),
        out_shape=jax.ShapeDtypeStruct((S, B_WIDTH), BF), compiler_params=_params(("arbitrary",)),
    )(sinks, qn, kn, kn, proj, proj)


def _b_attn_bwd(qn, kn, proj, sinks, dy):
    S = qn.shape[0]

    def body(s_ref, q_ref, kp_ref, kc_ref, vp_ref, vc_ref, dy_ref, dq_ref, dk_ref, dv_ref, ds_ref):
        n = pl.program_id(0)

        @pl.when(n == 0)
        def _():
            dk_ref[...] = jnp.zeros_like(dk_ref)
            dv_ref[...] = jnp.zeros_like(dv_ref)
            ds_ref[...] = jnp.zeros_like(ds_ref)

        valid = _b_valid(n)
        k2 = jnp.concatenate([kp_ref[...], kc_ref[...]], axis=0).astype(F32)
        v2 = jnp.concatenate([vp_ref[...], vc_ref[...]], axis=0)
        lane = lax.broadcasted_iota(jnp.int32, (CHUNK, 128), 1)
        dk2 = jnp.zeros((2 * CHUNK, 128), F32)
        dv2 = jnp.zeros((2 * CHUNK, 128), F32)
        dsink = jnp.zeros((CHUNK, 128), F32)
        scale = B_HEAD_DIM ** -0.5
        for g in range(B_KV_HEADS):
            kd, vd = _b_dup(k2, g), _b_dup(v2, g)
            dk_acc = jnp.zeros((2 * CHUNK, 128), F32)
            dv_acc = jnp.zeros((2 * CHUNK, 128), F32)
            for pp in range(B_HEADS // B_KV_HEADS // 2):
                p = g * (B_HEADS // B_KV_HEADS // 2) + pp
                q_pair = q_ref[:, p * 128:(p + 1) * 128]
                do_pair = dy_ref[:, p * 128:(p + 1) * 128]
                dq_pair = jnp.zeros((CHUNK, 128), F32)
                for hf in range(2):
                    h = 2 * p + hf
                    hm = _half_mask((CHUNK, 128), hf)
                    qm = jnp.where(hm, q_pair, jnp.zeros_like(q_pair))
                    do = jnp.where(hm, do_pair, 0.0)
                    do_b = do.astype(BF)
                    pr, ps = _b_probs(qm, kd, valid, s_ref[0, h])
                    pr_b = pr.astype(BF)
                    o = jnp.dot(pr_b, vd, preferred_element_type=F32)
                    delta = jnp.sum(do * o, axis=-1, keepdims=True)
                    dp = lax.dot_general(do_b, vd, (((1,), (1,)), ((), ())), preferred_element_type=F32)
                    dsc = (pr * (dp - delta) * scale).astype(BF)
                    dsink = dsink + jnp.where(lane == h, -ps * delta, 0.0)
                    dq = jnp.dot(dsc, kd, preferred_element_type=F32)
                    dq_pair = dq_pair + jnp.where(hm, dq, 0.0)
                    dk_acc = dk_acc + lax.dot_general(dsc, qm, (((0,), (0,)), ((), ())), preferred_element_type=F32)
                    dv_acc = dv_acc + lax.dot_general(pr_b, do_b, (((0,), (0,)), ((), ())), preferred_element_type=F32)
                dq_ref[:, p * 128:(p + 1) * 128] = dq_pair
            gm = _half_mask((2 * CHUNK, 128), g)
            dk2 = dk2 + jnp.where(gm, dk_acc + pltpu.roll(dk_acc, 64, 1), 0.0)
            dv2 = dv2 + jnp.where(gm, dv_acc + pltpu.roll(dv_acc, 64, 1), 0.0)
        ds_ref[...] += dsink
        cur = pl.ds(pl.multiple_of(n * CHUNK, CHUNK), CHUNK)
        dk_ref[cur, :] += dk2[CHUNK:]
        dv_ref[cur, :] += dv2[CHUNK:]

        @pl.when(n > 0)
        def _():
            prv = pl.ds(pl.multiple_of((n - 1) * CHUNK, CHUNK), CHUNK)
            dk_ref[prv, :] += dk2[:CHUNK]
            dv_ref[prv, :] += dv2[:CHUNK]

    full = pl.BlockSpec((S, 128), lambda n: (0, 0))
    return pl.pallas_call(
        body, name="b_attn_bwd", grid=(S // CHUNK,),
        in_specs=[pl.BlockSpec(memory_space=pltpu.SMEM), pl.BlockSpec((CHUNK, B_WIDTH), lambda n: (n, 0))] + _b_kv_specs(S)
        + [pl.BlockSpec((CHUNK, B_WIDTH), lambda n: (n, 0))],
        out_specs=[pl.BlockSpec((CHUNK, B_WIDTH), lambda n: (n, 0)), full, full, pl.BlockSpec((CHUNK, 128), lambda n: (0, 0))],
        out_shape=[jax.ShapeDtypeStruct((S, B_WIDTH), F32), jax.ShapeDtypeStruct((S, 128), F32), jax.ShapeDtypeStruct((S, 128), F32),
                   jax.ShapeDtypeStruct((CHUNK, 128), F32)],
        compiler_params=_params(("arbitrary",)),
    )(sinks, qn, kn, kn, proj, proj, dy)


def _c_block(q, k, v, gq, gk):
    qn = q * lax.rsqrt(jnp.mean(q * q, axis=-1, keepdims=True) + EPS) * gq
    kn = k * lax.rsqrt(jnp.mean(k * k, axis=-1, keepdims=True) + EPS) * gk
    s = lax.dot_general(qn.astype(BF), kn.astype(BF), (((1,), (1,)), ((), ())), preferred_element_type=F32) * (C_HEAD_DIM ** -0.5)
    p = jax.nn.softmax(s, axis=-1)
    return jnp.dot(p.astype(BF), v.astype(BF), preferred_element_type=F32)


def _c_specs(S, M, tq):
    q_col = (2 * A_WIDTH + B_WIDTH + 2 * B_KV_WIDTH) // 128
    return [pl.BlockSpec((tq, 128), lambda h, i: (i, q_col + h)), pl.BlockSpec((M, 128), lambda h, i: (0, h)),
            pl.BlockSpec((M, 128), lambda h, i: (0, C_HEADS + h)), pl.BlockSpec((1, 128), lambda h, i: (0, 0)),
            pl.BlockSpec((1, 128), lambda h, i: (0, 0))]


def _c_fwd(proj, kv, gq, gk):
    S, M = proj.shape[0], kv.shape[0]
    tq = _pick(S, (512,))

    def body(q_ref, k_ref, v_ref, gq_ref, gk_ref, y_ref):
        y_ref[...] = _c_block(q_ref[...], k_ref[...], v_ref[...], gq_ref[...], gk_ref[...]).astype(BF)

    return pl.pallas_call(
        body, name="c_fwd", grid=(C_HEADS, S // tq), in_specs=_c_specs(S, M, tq),
        out_specs=pl.BlockSpec((tq, 128), lambda h, i: (i, h)),
        out_shape=jax.ShapeDtypeStruct((S, C_WIDTH), BF), compiler_params=_params(("parallel", "parallel")),
    )(proj, kv, kv, gq, gk)


def _c_bwd(proj, kv, gq, gk, dy):
    S, M = proj.shape[0], kv.shape[0]
    tq = _pick(S, (512,))

    def body(q_ref, k_ref, v_ref, gq_ref, gk_ref, dy_ref, dq_ref, dk_ref, dv_ref, dgq_ref, dgk_ref):
        i = pl.program_id(1)
        _, vjp = jax.vjp(_c_block, q_ref[...], k_ref[...], v_ref[...], gq_ref[...], gk_ref[...])
        dq, dk, dv, dgq, dgk = vjp(dy_ref[...])
        dq_ref[...] = dq.astype(BF)

        @pl.when(i == 0)
        def _():
            dk_ref[...] = dk
            dv_ref[...] = dv
            dgq_ref[...] = dgq
            dgk_ref[...] = dgk

        @pl.when(i > 0)
        def _():
            dk_ref[...] += dk
            dv_ref[...] += dv
            dgq_ref[...] += dgq
            dgk_ref[...] += dgk

    return pl.pallas_call(
        body, name="c_bwd", grid=(C_HEADS, S // tq),
        in_specs=_c_specs(S, M, tq) + [pl.BlockSpec((tq, 128), lambda h, i: (i, h))],
        out_specs=[pl.BlockSpec((tq, 128), lambda h, i: (i, h)), pl.BlockSpec((M, 128), lambda h, i: (0, h)),
                   pl.BlockSpec((M, 128), lambda h, i: (0, h)), pl.BlockSpec((None, 1, 128), lambda h, i: (h, 0, 0)),
                   pl.BlockSpec((None, 1, 128), lambda h, i: (h, 0, 0))],
        out_shape=[jax.ShapeDtypeStruct((S, C_WIDTH), BF), jax.ShapeDtypeStruct((M, C_WIDTH), F32), jax.ShapeDtypeStruct((M, C_WIDTH), F32),
                   jax.ShapeDtypeStruct((C_HEADS, 1, 128), F32), jax.ShapeDtypeStruct((C_HEADS, 1, 128), F32)],
        compiler_params=_params(("parallel", "arbitrary")),
    )(proj, kv, kv, gq, gk, dy)


def _merge_specs(S, D, tr, tc):
    off = GATE_OFF // tc
    nd = D // tc
    gates = [pl.BlockSpec((tr, tc), functools.partial(lambda b, i, j: (i, off + b * nd + j), b)) for b in range(3)]
    zs = [pl.BlockSpec((tr, tc), lambda i, j: (i, j)) for _ in range(3)]
    return gates + zs


def _merge_fwd(proj, za, zb, zc):
    S, D = za.shape
    tr, tc = _pick(S, (512,)), _pick(D, (256,))

    def body(ga_ref, gb_ref, gc_ref, za_ref, zb_ref, zc_ref, m_ref):
        acc = jax.nn.sigmoid(ga_ref[...]) * za_ref[...].astype(F32)
        acc = acc + jax.nn.sigmoid(gb_ref[...]) * zb_ref[...].astype(F32)
        acc = acc + jax.nn.sigmoid(gc_ref[...]) * zc_ref[...].astype(F32)
        m_ref[...] = acc.astype(BF)

    return pl.pallas_call(
        body, name="merge_fwd", grid=(S // tr, D // tc), in_specs=_merge_specs(S, D, tr, tc),
        out_specs=pl.BlockSpec((tr, tc), lambda i, j: (i, j)), out_shape=jax.ShapeDtypeStruct((S, D), BF),
        compiler_params=_params(("parallel", "parallel")),
    )(proj, proj, proj, za, zb, zc)


def _merge_bwd(proj, za, zb, zc, dm):
    S, D = za.shape
    tr, tc = _pick(S, (512,)), _pick(D, (256,))
    nd = D // tc

    def body(ga_ref, gb_ref, gc_ref, za_ref, zb_ref, zc_ref, dm_ref, dza_ref, dzb_ref, dzc_ref, dga_ref, dgb_ref, dgc_ref):
        dmv = dm_ref[...]
        for g_ref, z_ref, dz_ref, dg_ref in ((ga_ref, za_ref, dza_ref, dga_ref), (gb_ref, zb_ref, dzb_ref, dgb_ref),
                                             (gc_ref, zc_ref, dzc_ref, dgc_ref)):
            sg = jax.nn.sigmoid(g_ref[...])
            dz_ref[...] = (sg * dmv).astype(BF)
            dg_ref[...] = (dmv * z_ref[...].astype(F32) * sg * (1.0 - sg)).astype(BF)

    tile = pl.BlockSpec((tr, tc), lambda i, j: (i, j))
    return pl.pallas_call(
        body, name="merge_bwd", grid=(S // tr, D // tc), in_specs=_merge_specs(S, D, tr, tc) + [tile],
        out_specs=[tile, tile, tile, tile, tile, tile],
        out_shape=[jax.ShapeDtypeStruct((S, D), BF)] * 6,
        compiler_params=_params(("parallel", "parallel")),
    )(proj, proj, proj, za, zb, zc, dm)


def _shift_down(u, k):
    t = lax.broadcasted_iota(jnp.int32, u.shape, 0)
    return jnp.where(t >= k, pltpu.roll(u, k, 0), 0.0)


def _shift_up(u, k):
    n = u.shape[0]
    t = lax.broadcasted_iota(jnp.int32, u.shape, 0)
    return jnp.where(t < n - k, pltpu.roll(u, n - k, 0), 0.0)


def _conv3(u, w, b):
    return u * w[2:3] + _shift_down(u, 1) * w[1:2] + _shift_down(u, 2) * w[0:1] + b


def _ffn_specs(S, F, tc):
    return [pl.BlockSpec((2, S, tc), lambda j: (0, 0, j)), pl.BlockSpec((2, 3, tc), lambda j: (0, 0, j)),
            pl.BlockSpec((2, 1, tc), lambda j: (0, 0, j))]


def _ffn_act_fwd(up3, cw3, cb3):
    _, S, F = up3.shape
    tc = _pick(F, (128,))

    def body(u_ref, w_ref, b_ref, o_ref):
        ca = _conv3(u_ref[0], w_ref[0], b_ref[0])
        cb = _conv3(u_ref[1], w_ref[1], b_ref[1])
        o_ref[...] = (ca * jax.nn.sigmoid(ca) * cb).astype(BF)

    return pl.pallas_call(
        body, name="ffn_act_fwd", grid=(F // tc,), in_specs=_ffn_specs(S, F, tc),
        out_specs=pl.BlockSpec((S, tc), lambda j: (0, j)), out_shape=jax.ShapeDtypeStruct((S, F), BF),
        compiler_params=_params(("parallel",)),
    )(up3, cw3, cb3)


def _ffn_act_bwd(up3, cw3, cb3, dact):
    _, S, F = up3.shape
    tc = _pick(F, (128,))

    def body(u_ref, w_ref, b_ref, da_ref, du_ref, dw_ref, db_ref):
        ca = _conv3(u_ref[0], w_ref[0], b_ref[0])
        cb = _conv3(u_ref[1], w_ref[1], b_ref[1])
        sg = jax.nn.sigmoid(ca)
        dav = da_ref[...]
        dcs = (dav * cb * sg * (1.0 + ca * (1.0 - sg)), dav * ca * sg)
        for part in range(2):
            dc, w, u = dcs[part], w_ref[part], u_ref[part]
            du_ref[part] = (dc * w[2:3] + _shift_up(dc, 1) * w[1:2] + _shift_up(dc, 2) * w[0:1]).astype(BF)
            dw_ref[part, 2:3, :] = jnp.sum(dc * u, axis=0, keepdims=True)
            dw_ref[part, 1:2, :] = jnp.sum(dc * _shift_down(u, 1), axis=0, keepdims=True)
            dw_ref[part, 0:1, :] = jnp.sum(dc * _shift_down(u, 2), axis=0, keepdims=True)
            db_ref[part] = jnp.sum(dc, axis=0, keepdims=True)

    return pl.pallas_call(
        body, name="ffn_act_bwd", grid=(F // tc,),
        in_specs=_ffn_specs(S, F, tc) + [pl.BlockSpec((S, tc), lambda j: (0, j))],
        out_specs=[pl.BlockSpec((2, S, tc), lambda j: (0, 0, j)), pl.BlockSpec((2, 3, tc), lambda j: (0, 0, j)),
                   pl.BlockSpec((2, 1, tc), lambda j: (0, 0, j))],
        out_shape=[jax.ShapeDtypeStruct((2, S, F), BF), jax.ShapeDtypeStruct((2, 3, F), F32), jax.ShapeDtypeStruct((2, 1, F), F32)],
        compiler_params=_params(("parallel",)),
    )(up3, cw3, cb3, dact)


def _loss(y, target):
    S, D = y.shape
    tr = _pick(S, (256,))

    def body(y_ref, t_ref, dy_ref, dyb_ref, l_ref):
        i = pl.program_id(0)
        e = y_ref[...] - t_ref[...]
        dy = e * (1.0 / D)
        dy_ref[...] = dy
        dyb_ref[...] = dy.astype(BF)
        part = jnp.sum(jnp.sum(e * e, axis=-1, keepdims=True), axis=0, keepdims=True) * (0.5 / D)

        @pl.when(i == 0)
        def _():
            l_ref[...] = jnp.zeros_like(l_ref)

        l_ref[...] += part

    row = pl.BlockSpec((tr, D), lambda i: (i, 0))
    return pl.pallas_call(
        body, name="loss", grid=(S // tr,), in_specs=[row, row],
        out_specs=[row, row, pl.BlockSpec((8, 128), lambda i: (0, 0))],
        out_shape=[jax.ShapeDtypeStruct((S, D), F32), jax.ShapeDtypeStruct((S, D), BF), jax.ShapeDtypeStruct((8, 128), F32)],
        compiler_params=_params(("arbitrary",)),
    )(y, target)


ANY = pl.BlockSpec(memory_space=pl.ANY)


def _allgather(shards, name):
    n = len(shards)

    def body(*refs):
        ins, outs = refs[:n], refs[n:2 * n]
        send_sems, recv_sems, local_sems = refs[2 * n:]
        x, y, c = lax.axis_index("x"), lax.axis_index("y"), lax.axis_index("c")
        me, sibling = (x, y, c), (x, y, 1 - c)
        chips = [(1 - x, y), (x, 1 - y), (1 - x, 1 - y)]

        def blk(w, px, py, pc):
            return outs[w].at[4 * px + 2 * py + pc]

        def copy(w, k, block, to, src=None):
            return pltpu.make_async_remote_copy(
                src_ref=blk(w, *block) if src is None else src, dst_ref=blk(w, *block),
                send_sem=send_sems.at[w, k], recv_sem=recv_sems.at[w, k], device_id=to, device_id_type=MESH)

        started = []
        mine = []
        for w in range(n):
            mine.append(pltpu.make_async_copy(ins[w], blk(w, *me), local_sems.at[w]))
            mine[-1].start()
            first = [copy(w, 0, me, sibling, src=ins[w])]
            first += [copy(w, 1 + j, me, (*chip, c), src=ins[w]) for j, chip in enumerate(chips)]
            for cp in first:
                cp.start()
            started += first
        for w in range(n):
            for j, chip in enumerate(chips):
                copy(w, 1 + j, (*chip, c), me).wait_recv()
                fwd = copy(w, 4 + j, (*chip, c), sibling)
                fwd.start()
                started.append(fwd)
        for w in range(n):
            copy(w, 0, sibling, me).wait_recv()
            for j, chip in enumerate(chips):
                copy(w, 4 + j, (*chip, 1 - c), me).wait_recv()
        for cp in started:
            cp.wait_send()
        for cp in mine:
            cp.wait()

    outs = pl.pallas_call(
        body, name=name, in_specs=[ANY] * n, out_specs=[ANY] * n,
        out_shape=[jax.ShapeDtypeStruct((N_DEV,) + s.shape, s.dtype) for s in shards],
        scratch_shapes=[pltpu.SemaphoreType.DMA((n, 7)), pltpu.SemaphoreType.DMA((n, 7)), pltpu.SemaphoreType.DMA((n,))],
    )(*shards)
    return list(outs)


def _allgather_seq(shards, name, collective_id, after=()):
    n = len(shards)
    n_after = len(after)

    def body(*refs):
        ins, outs = refs[:n], refs[n + n_after:2 * n + n_after]
        send_sems, recv_sems, local_sems = refs[2 * n + n_after:]
        x, y, c = lax.axis_index("x"), lax.axis_index("y"), lax.axis_index("c")
        me, sibling = (x, y, c), (x, y, 1 - c)
        chips = [(1 - x, y), (x, 1 - y), (1 - x, 1 - y)]
        barrier = pltpu.get_barrier_semaphore()
        for peer in [sibling] + [(*chip, c) for chip in chips]:
            pl.semaphore_signal(barrier, inc=1, device_id=peer, device_id_type=MESH)
        pl.semaphore_wait(barrier, 4)

        def blk(w, px, py, pc):
            return outs[w].at[4 * px + 2 * py + pc]

        def copy(w, k, block, to, src=None):
            return pltpu.make_async_remote_copy(
                src_ref=blk(w, *block) if src is None else src, dst_ref=blk(w, *block),
                send_sem=send_sems.at[7 * w + k], recv_sem=recv_sems.at[7 * w + k], device_id=to, device_id_type=MESH)

        started = []
        mine = []
        for w in range(n):
            mine.append(pltpu.make_async_copy(ins[w], blk(w, *me), local_sems.at[w]))
            mine[-1].start()
            first = [copy(w, 0, me, sibling, src=ins[w])]
            first += [copy(w, 1 + j, me, (*chip, c), src=ins[w]) for j, chip in enumerate(chips)]
            for cp in first:
                cp.start()
            started += first
        for w in range(n):
            for j, chip in enumerate(chips):
                copy(w, 1 + j, (*chip, c), me).wait_recv()
                fwd = copy(w, 4 + j, (*chip, c), sibling)
                fwd.start()
                started.append(fwd)
        for w in range(n):
            copy(w, 0, sibling, me).wait_recv()
            for j, chip in enumerate(chips):
                copy(w, 4 + j, (*chip, 1 - c), me).wait_recv()
        for cp in started:
            cp.wait_send()
        for cp in mine:
            cp.wait()

    outs = pl.kernel(
        body, name=name, out_type=[jax.ShapeDtypeStruct((N_DEV,) + s.shape, s.dtype) for s in shards],
        mesh=plsc.ScalarSubcoreMesh(axis_name="seq", num_cores=1),
        scratch_types=[pltpu.SemaphoreType.DMA((7 * n,)), pltpu.SemaphoreType.DMA((7 * n,)), pltpu.SemaphoreType.DMA((n,))],
        compiler_params=pltpu.CompilerParams(collective_id=collective_id),
    )(*shards, *after)
    return list(outs)


def _sibling_exchange(grads, name):
    n = len(grads)

    def body(*refs):
        ins, outs = refs[:n], refs[n:2 * n]
        send_sems, recv_sems = refs[2 * n:]
        x, y, c = lax.axis_index("x"), lax.axis_index("y"), lax.axis_index("c")
        copies = [pltpu.make_async_remote_copy(
            src_ref=ins[w].at[:, 1 - c], dst_ref=outs[w], send_sem=send_sems.at[w], recv_sem=recv_sems.at[w],
            device_id=(x, y, 1 - c), device_id_type=MESH) for w in range(n)]
        for cp in copies:
            cp.start()
        for cp in copies:
            cp.wait()

    outs = pl.pallas_call(
        body, name=name, in_specs=[ANY] * n, out_specs=[ANY] * n,
        out_shape=[jax.ShapeDtypeStruct((g.shape[0],) + g.shape[2:], g.dtype) for g in grads],
        scratch_shapes=[pltpu.SemaphoreType.DMA((n,)), pltpu.SemaphoreType.DMA((n,))],
    )(*grads)
    return list(outs)


def _chip_exchange(sums, name):
    n = len(sums)

    def body(*refs):
        ins, outs = refs[:n], refs[n:2 * n]
        send_sems, recv_sems = refs[2 * n:]
        x, y, c = lax.axis_index("x"), lax.axis_index("y"), lax.axis_index("c")
        chips = [(1 - x, y), (x, 1 - y), (1 - x, 1 - y)]
        copies = []
        for w in range(n):
            for k, (px, py) in enumerate(chips):
                copies.append(pltpu.make_async_remote_copy(
                    src_ref=ins[w].at[2 * px + py], dst_ref=outs[w].at[k], send_sem=send_sems.at[w, k],
                    recv_sem=recv_sems.at[w, k], device_id=(px, py, c), device_id_type=MESH))
        for cp in copies:
            cp.start()
        for cp in copies:
            cp.wait()

    outs = pl.pallas_call(
        body, name=name, in_specs=[ANY] * n, out_specs=[ANY] * n,
        out_shape=[jax.ShapeDtypeStruct((3,) + s.shape[1:], s.dtype) for s in sums],
        scratch_shapes=[pltpu.SemaphoreType.DMA((n, 3)), pltpu.SemaphoreType.DMA((n, 3))],
    )(*sums)
    return list(outs)


def _row_tile(r, c):
    want = max(8, (256 * 1024) // c)
    for t in (512, 256, 128, 64, 32, 16, 8):
        if t <= want and r % t == 0:
            return t
    return r


def _pair_add(g4, recv, core, name):
    _, _, r, c = g4.shape
    tr = _row_tile(r, c)

    def body(core_ref, a_ref, b_ref, o_ref):
        o_ref[...] = (a_ref[...].astype(F32) + b_ref[...].astype(F32)).astype(BF)

    return pl.pallas_call(
        body, name=name,
        grid_spec=pltpu.PrefetchScalarGridSpec(
            num_scalar_prefetch=1, grid=(4, r // tr),
            in_specs=[pl.BlockSpec((None, None, tr, c), lambda p, i, s: (p, s[0], i, 0)), pl.BlockSpec((None, tr, c), lambda p, i, s: (p, i, 0))],
            out_specs=pl.BlockSpec((None, tr, c), lambda p, i, s: (p, i, 0))),
        out_shape=jax.ShapeDtypeStruct((4, r, c), BF), compiler_params=_params(("parallel", "parallel")),
    )(core, g4, recv)


def _adam_math(w, g, m, v):
    m = ADAM_B1 * m + (1.0 - ADAM_B1) * g
    v = ADAM_B2 * v + (1.0 - ADAM_B2) * (g * g)
    m_hat = m / (1.0 - ADAM_B1 ** ADAM_STEP)
    v_hat = v / (1.0 - ADAM_B2 ** ADAM_STEP)
    delta = -ADAM_LR * (m_hat / (jnp.sqrt(v_hat) + ADAM_EPS) + ADAM_WD * w)
    return delta, m, v


def _adamw_big(sums, recv, chip, w, m, v, name):
    r, c = w.shape
    tr = _row_tile(r, c) // 2 if _row_tile(r, c) >= 16 else _row_tile(r, c)

    def body(chip_ref, s_ref, r_ref, w_ref, m_ref, v_ref, g_out, d_out, m_out, v_out):
        g = s_ref[...].astype(F32) + r_ref[0].astype(F32)
        g = g + r_ref[1].astype(F32)
        g = g + r_ref[2].astype(F32)
        delta, mn, vn = _adam_math(w_ref[...], g, m_ref[...], v_ref[...])
        g_out[...] = g
        d_out[...] = delta
        m_out[...] = mn
        v_out[...] = vn

    row = pl.BlockSpec((tr, c), lambda i, s: (i, 0))
    return pl.pallas_call(
        body, name=name,
        grid_spec=pltpu.PrefetchScalarGridSpec(
            num_scalar_prefetch=1, grid=(r // tr,),
            in_specs=[pl.BlockSpec((None, tr, c), lambda i, s: (s[0], i, 0)), pl.BlockSpec((3, tr, c), lambda i, s: (0, i, 0)), row, row, row],
            out_specs=[row, row, row, row]),
        out_shape=[jax.ShapeDtypeStruct((r, c), F32)] * 4, compiler_params=_params(("parallel",)),
    )(chip, sums, recv, w, m, v)


def _adamw_small(parts, w, m, v, name):
    R = w.shape[0]
    tr = _pick(R, (256, 128, 64, 32, 16, 8))

    def body(p_ref, w_ref, m_ref, v_ref, g_out, d_out, m_out, v_out):
        g = p_ref[0]
        for d in range(1, N_DEV):
            g = g + p_ref[d]
        delta, mn, vn = _adam_math(w_ref[...], g, m_ref[...], v_ref[...])
        g_out[...] = g
        d_out[...] = delta
        m_out[...] = mn
        v_out[...] = vn

    row = pl.BlockSpec((tr, 128), lambda i: (i, 0))
    return pl.pallas_call(
        body, name=name, grid=(R // tr,),
        in_specs=[pl.BlockSpec((N_DEV, tr, 128), lambda i: (0, i, 0)), row, row, row], out_specs=[row, row, row, row],
        out_shape=[jax.ShapeDtypeStruct((R, 128), F32)] * 4, compiler_params=_params(("parallel",)),
    )(parts, w, m, v)


def _adamw_plain(g, w, m, v, name):
    def body(g_ref, w_ref, m_ref, v_ref, d_out, m_out, v_out):
        delta, mn, vn = _adam_math(w_ref[...], g_ref[...], m_ref[...], v_ref[...])
        d_out[...] = delta
        m_out[...] = mn
        v_out[...] = vn

    return pl.pallas_call(body, name=name, out_shape=[jax.ShapeDtypeStruct(w.shape, F32)] * 3)(g, w, m, v)


def _pack(arrays):
    rows = []
    for a in arrays:
        flat = a.reshape(-1).astype(F32)
        n = flat.shape[0]
        padded = -(-n // 1024) * 1024
        rows.append(jnp.pad(flat, (0, padded - n)).reshape(padded // 128, 128))
    return jnp.concatenate(rows, axis=0)


def _unpack(packed, shapes):
    out, row = [], 0
    for s in shapes:
        n = 1
        for d in s:
            n *= d
        nrow = -(-n // 1024) * 8
        out.append(packed[row:row + nrow].reshape(-1)[:n].reshape(s))
        row += nrow
    return out


def kernel(x, mem, positions, g_mix, w_in, g_a_v, w_spatial, b_spatial, g_b_q, g_b_k, sinks, g_mem, w_mem_kv, g_c_q, g_c_k, w_branch_a, w_branch_b, w_branch_c, w_out, g_ffn, w_up, conv_w, conv_b, w_down, loss_target, m_g_mix, m_w_in, m_g_a_v, m_w_spatial, m_b_spatial, m_g_b_q, m_g_b_k, m_sinks, m_g_mem, m_w_mem_kv, m_g_c_q, m_g_c_k, m_w_branch_a, m_w_branch_b, m_w_branch_c, m_w_out, m_g_ffn, m_w_up, m_conv_w, m_conv_b, m_w_down, v_g_mix, v_w_in, v_g_a_v, v_w_spatial, v_b_spatial, v_g_b_q, v_g_b_k, v_sinks, v_g_mem, v_w_mem_kv, v_g_c_q, v_g_c_k, v_w_branch_a, v_w_branch_b, v_w_branch_c, v_w_out, v_g_ffn, v_w_up, v_conv_w, v_conv_b, v_w_down):
    S, D = x.shape[1], x.shape[2]
    M = mem.shape[1]
    F = w_down.shape[1] * N_DEV
    in_cols = w_in.shape[2] * N_DEV
    ax, ay, ac = lax.axis_index("x"), lax.axis_index("y"), lax.axis_index("c")
    core = jnp.reshape(ac, (1,)).astype(jnp.int32)
    chip = jnp.reshape(2 * ax + ay, (1,)).astype(jnp.int32)
    me = 4 * ax + 2 * ay + ac

    x2, mem2, tgt2 = x[0], mem[0], loss_target[0]

    big = dict(w_in=w_in[0], w_mem_kv=w_mem_kv[0], w_branch_a=w_branch_a[0], w_branch_b=w_branch_b[0],
               w_branch_c=w_branch_c[0], w_out=w_out[0], w_up=w_up[0], w_down=w_down[0])
    names = list(big)
    cast = {k: big[k].astype(BF) for k in names}
    groups = [["w_in"], ["w_mem_kv", "w_branch_a", "w_branch_b", "w_branch_c", "w_out"], ["w_up"], ["w_down"]]
    W = {}
    gathered = _allgather([conv_w[0]], "ag_conv_w")
    token = ()
    for gi, grp in enumerate(groups):
        res = _allgather_seq([cast[k] for k in grp], f"ag_seq{gi}", gi, after=token)
        W.update(zip(grp, res))
        token = (res[0],)
    w_in_f = W["w_in"].transpose(1, 0, 2).reshape(D, in_cols)
    w_kv_f = W["w_mem_kv"].reshape(D, 2 * C_WIDTH)
    w_out_f = W["w_out"].reshape(D, D)
    w_down_f = W["w_down"].reshape(F, D)
    cw3 = gathered[-1].reshape(2, N_DEV // 2, 3, 2 * F // N_DEV).transpose(0, 2, 1, 3).reshape(2, 3, F)
    cb3 = conv_b.reshape(2, 1, F)

    half = ROPE_DIM // 2
    inv = ROPE_THETA ** (-jnp.arange(half, dtype=F32) / half)
    ang = positions[0].astype(F32)[:, None] * inv
    cos, sin = jnp.cos(ang), jnp.sin(ang)
    one, zero = jnp.ones((S, B_HEAD_DIM - ROPE_DIM), F32), jnp.zeros((S, B_HEAD_DIM - ROPE_DIM), F32)
    z8 = jnp.zeros((S, half), F32)
    ct = jnp.tile(jnp.concatenate([cos, cos, one], axis=1), (1, 2))
    sa = jnp.tile(jnp.concatenate([-sin, z8, zero], axis=1), (1, 2))
    sb = jnp.tile(jnp.concatenate([z8, sin, zero], axis=1), (1, 2))
    gq2, gk2 = jnp.tile(g_b_q, (1, 2)), jnp.tile(g_b_k, (1, 2))
    b_t = b_spatial[0].T

    h, rstd1 = _rms_fwd(x2, g_mix, "rms1_fwd")
    proj = _mm(h, w_in_f, "nn", F32, "mm_proj", tn=1280)
    y_a = _a_fwd(proj, g_a_v, w_spatial[0], b_t)
    qn, kn = _b_pre(proj, gq2, gk2, ct, sa, sb)
    y_b = _b_attn_fwd(qn, kn, proj, sinks)
    mem_h, rstd_m = _rms_fwd(mem2, g_mem, "rmsmem_fwd")
    kv = _mm(mem_h, w_kv_f, "nn", F32, "mm_kv", after=(y_b,))
    y_c = _c_fwd(proj, kv, g_c_q, g_c_k)
    z_a = _mm(y_a, W["w_branch_a"], "nn", BF, "mm_za", b_stack=True, after=(y_b,))
    z_b = _mm(y_b, W["w_branch_b"], "nn", BF, "mm_zb", b_stack=True)
    z_c = _mm(y_c, W["w_branch_c"], "nn", BF, "mm_zc", b_stack=True)
    merged = _merge_fwd(proj, z_a, z_b, z_c)
    x1 = _mm(merged, w_out_f, "nn", F32, "mm_x1", resid=x2)
    h2, rstd2 = _rms_fwd(x1, g_ffn, "rms2_fwd")
    up3 = _mm(h2, W["w_up"], "nn", F32, "mm_up", b_stack=True, out_parts=2)
    act = _ffn_act_fwd(up3, cw3, cb3)
    y = _mm(act, w_down_f, "nn", F32, "mm_y", resid=x1, tk=512)
    dy, dy_b, loss_acc = _loss(y, tgt2)
    loss = lax.psum(loss_acc[0, 0], ("x", "y", "c"))

    d_act = _mm(dy_b, w_down_f, "nt", F32, "mm_dact", tn=1408)
    g_down = _mm(act, dy_b, "tn", BF, "mm_gdown", tm=1408, tk=512)
    d_up3, d_cw3, d_cb3 = _ffn_act_bwd(up3, cw3, cb3, d_act)
    d_h2 = _mm(d_up3, W["w_up"], "nt", F32, "mm_dh2", a_parts=2, b_stack=True)
    g_up = _mm(h2, d_up3, "tn", BF, "mm_gup", b_parts=2, out_stack=True, tk=512)
    dx1, dx1_b, d_g_ffn = _rms_bwd(x1, rstd2, g_ffn, d_h2, dy, "rms2_bwd")
    d_merged = _mm(dx1_b, w_out_f, "nt", F32, "mm_dmerged")
    g_out = _mm(merged, dx1_b, "tn", BF, "mm_gout", tk=512)
    dz_a, dz_b, dz_c, dga, dgb, dgc = _merge_bwd(proj, z_a, z_b, z_c, d_merged)
    dy_a = _mm(dz_a, W["w_branch_a"], "nt", F32, "mm_dya", b_stack=True)
    dy_b_ = _mm(dz_b, W["w_branch_b"], "nt", F32, "mm_dyb", b_stack=True)
    dy_c = _mm(dz_c, W["w_branch_c"], "nt", F32, "mm_dyc", b_stack=True)
    g_ba = _mm(y_a, dz_a, "tn", BF, "mm_gba", out_stack=True, tk=512)
    g_bb = _mm(y_b, dz_b, "tn", BF, "mm_gbb", out_stack=True, tk=512)
    g_bc = _mm(y_c, dz_c, "tn", BF, "mm_gbc", out_stack=True, tk=512)
    d_uv, d_g_a_v, d_w_s, d_b_t = _a_bwd(proj, g_a_v, w_spatial[0], b_t, dy_a)
    dqn, dkn, dv_b, dsink_rows = _b_attn_bwd(qn, kn, proj, sinks, dy_b_)
    d_qkv, d_gq2, d_gk2 = _b_pre_bwd(proj, gq2, gk2, ct, sa, sb, dqn, dkn, dv_b)
    dq_c, dk_c, dv_c, d_gcq, d_gck = _c_bwd(proj, kv, g_c_q, g_c_k, dy_c)
    dkv_b = jnp.concatenate([dk_c, dv_c], axis=1).astype(BF)
    d_memh = _mm(dkv_b, w_kv_f, "nt", F32, "mm_dmemh")
    g_kv = _mm(mem_h, dkv_b, "tn", BF, "mm_gkv", tk=512)
    _, _, d_g_mem = _rms_bwd(mem2, rstd_m, g_mem, d_memh, None, "rmsmem_bwd")
    dproj = jnp.concatenate([d_uv, d_qkv, dq_c, dga, dgb, dgc], axis=1)
    d_h = _mm(dproj, w_in_f, "nt", F32, "mm_dh", tk=1280)
    g_in = _mm(h, dproj, "tn", BF, "mm_gin", tn=1280, tk=512)
    grad_x, _, d_g_mix = _rms_bwd(x2, rstd1, g_mix, d_h, dx1, "rms1_bwd")

    c_in = in_cols // N_DEV
    stacked = [g_in.reshape(D, N_DEV, c_in).transpose(1, 0, 2), g_kv.reshape(N_DEV, D // N_DEV, 2 * C_WIDTH), g_ba, g_bb, g_bc,
               g_out.reshape(N_DEV, D // N_DEV, D), g_up, g_down.reshape(N_DEV, F // N_DEV, D)]
    g4 = [g.reshape(4, 2, g.shape[1], g.shape[2]) for g in stacked]
    from_sibling = _sibling_exchange(g4, "rs_sibling")
    chip_sums = [_pair_add(a, b, core, "rs_add_" + k) for k, a, b in zip(names, g4, from_sibling)]
    from_chips = _chip_exchange(chip_sums, "rs_chips")
    big_out = {}
    moments = dict(w_in=(m_w_in, v_w_in), w_mem_kv=(m_w_mem_kv, v_w_mem_kv), w_branch_a=(m_w_branch_a, v_w_branch_a),
                   w_branch_b=(m_w_branch_b, v_w_branch_b), w_branch_c=(m_w_branch_c, v_w_branch_c), w_out=(m_w_out, v_w_out),
                   w_up=(m_w_up, v_w_up), w_down=(m_w_down, v_w_down))
    for k, s, r in zip(names, chip_sums, from_chips):
        res = _adamw_big(s, r, chip, big[k], moments[k][0][0], moments[k][1][0], "adamw_" + k)
        big_out[k] = [a[None] for a in res]

    d_conv_w = d_cw3.reshape(2, 3, N_DEV // 2, 2 * F // N_DEV).transpose(1, 0, 2, 3).reshape(3, 2 * F)
    small_names = ["g_mix", "g_a_v", "w_spatial", "b_spatial", "g_b_q", "g_b_k", "sinks", "g_mem", "g_c_q", "g_c_k", "g_ffn", "conv_b"]
    small_w = dict(g_mix=g_mix, g_a_v=g_a_v, w_spatial=w_spatial, b_spatial=b_spatial, g_b_q=g_b_q, g_b_k=g_b_k, sinks=sinks,
                   g_mem=g_mem, g_c_q=g_c_q, g_c_k=g_c_k, g_ffn=g_ffn, conv_b=conv_b)
    small_m = dict(g_mix=m_g_mix, g_a_v=m_g_a_v, w_spatial=m_w_spatial, b_spatial=m_b_spatial, g_b_q=m_g_b_q, g_b_k=m_g_b_k,
                   sinks=m_sinks, g_mem=m_g_mem, g_c_q=m_g_c_q, g_c_k=m_g_c_k, g_ffn=m_g_ffn, conv_b=m_conv_b)
    small_v = dict(g_mix=v_g_mix, g_a_v=v_g_a_v, w_spatial=v_w_spatial, b_spatial=v_b_spatial, g_b_q=v_g_b_q, g_b_k=v_g_b_k,
                   sinks=v_sinks, g_mem=v_g_mem, g_c_q=v_g_c_q, g_c_k=v_g_c_k, g_ffn=v_g_ffn, conv_b=v_conv_b)
    small_g = dict(
        g_mix=d_g_mix, g_a_v=d_g_a_v, w_spatial=d_w_s, b_spatial=d_b_t.T,
        g_b_q=d_gq2.reshape(2, B_HEAD_DIM).sum(0), g_b_k=d_gk2.reshape(2, B_HEAD_DIM).sum(0),
        sinks=dsink_rows.sum(0)[:B_HEADS], g_mem=d_g_mem, g_c_q=d_gcq.sum(0), g_c_k=d_gck.sum(0), g_ffn=d_g_ffn,
        conv_b=d_cb3)
    cw_zero = jnp.zeros((3, 2 * F), F32)
    packed_g = _pack([small_g[k] for k in small_names] + [d_conv_w])
    packed_w = _pack([small_w[k] for k in small_names] + [cw_zero])
    packed_m = _pack([small_m[k] for k in small_names] + [cw_zero])
    packed_v = _pack([small_v[k] for k in small_names] + [cw_zero])
    parts = _allgather([packed_g], "ag_small")[0]
    sg, sd, sm, sv = _adamw_small(parts, packed_w, packed_m, packed_v, "adamw_small")
    shapes = [small_w[k].shape for k in small_names] + [(3, 2 * F)]
    sg_l, sd_l, sm_l, sv_l = (_unpack(p, shapes) for p in (sg, sd, sm, sv))
    small_out = {k: [sg_l[i], sd_l[i], sm_l[i], sv_l[i]] for i, k in enumerate(small_names)}
    c_cw = 2 * F // N_DEV
    g_cw = lax.dynamic_slice_in_dim(sg_l[-1], me * c_cw, c_cw, axis=1)
    cw_res = _adamw_plain(g_cw, conv_w[0], m_conv_w[0], v_conv_w[0], "adamw_conv_w")
    big_out["conv_w"] = [g_cw[None]] + [a[None] for a in cw_res]

    order = ["g_mix", "w_in", "g_a_v", "w_spatial", "b_spatial", "g_b_q", "g_b_k", "sinks", "g_mem", "w_mem_kv", "g_c_q", "g_c_k",
             "w_branch_a", "w_branch_b", "w_branch_c", "w_out", "g_ffn", "w_up", "conv_w", "conv_b", "w_down"]
    res = {**small_out, **big_out}
    outs = [loss, grad_x[None]]
    for field in range(4):
        outs += [res[k][field] for k in order]
    return tuple(outs)
```

```python
import functools

import jax
import jax.numpy as jnp
from jax import lax
from jax.experimental import pallas as pl
from jax.experimental.pallas import tpu as pltpu
from jax.experimental.pallas import tpu_sc as plsc

F32 = jnp.float32
BF = jnp.bfloat16
EPS = 1e-6
NEG = -1e30

N_DEV = 8
CHUNK = 128
A_GROUPS = 4
A_WIDTH = 512
B_HEADS = 16
B_KV_HEADS = 2
B_HEAD_DIM = 64
B_WIDTH = 1024
B_KV_WIDTH = 128
ROPE_DIM = 16
ROPE_THETA = 500000.0
C_HEADS = 4
C_HEAD_DIM = 128
C_WIDTH = 512
GATE_OFF = 2 * A_WIDTH + B_WIDTH + 2 * B_KV_WIDTH + C_WIDTH

ADAM_LR = 0.001
ADAM_B1 = 0.9
ADAM_B2 = 0.999
ADAM_EPS = 1e-08
ADAM_WD = 0.01
ADAM_STEP = 10

VMEM_LIMIT = 48 * 1024 * 1024
MESH = pl.DeviceIdType.MESH


def _pick(n, prefs):
    for p in prefs:
        if p <= n and n % p == 0:
            return p
    return n


def _params(sem):
    return pltpu.CompilerParams(dimension_semantics=sem, vmem_limit_bytes=VMEM_LIMIT)


def _hide(body, n_seen, n_hidden):
    if not n_hidden:
        return body

    def wrapped(*refs):
        return body(*refs[:n_seen], *refs[n_seen + n_hidden:])

    return wrapped


def _hidden_specs(after):
    return [pl.BlockSpec(memory_space=pl.ANY) for _ in after]


def _token(x, name):
    def body(x_ref, o_ref):
        o_ref[...] = jnp.zeros_like(o_ref)

    return pl.pallas_call(body, name=name, in_specs=[pl.BlockSpec(memory_space=pl.ANY)],
                          out_shape=jax.ShapeDtypeStruct((8, 128), F32))(x)


def _mm(a, b, mode, out_dtype, name, *, resid=None, b_stack=False, a_parts=0, b_parts=0, out_parts=0,
        out_stack=False, tm=1024, tn=1024, tk=1024, after=()):
    if mode == "nn":
        M = a.shape[-2]
        K = a.shape[-1] * max(a_parts, 1)
        N = b.shape[-1] * (N_DEV if b_stack else 1)
        dims = (((1,), (0,)), ((), ()))
    elif mode == "nt":
        M = a.shape[-2]
        K = a.shape[-1] * max(a_parts, 1)
        N = b.shape[-2]
        dims = (((1,), (1,)), ((), ()))
    else:
        K = a.shape[-2]
        M = a.shape[-1]
        N = b.shape[-1] * max(b_parts, 1)
        dims = (((0,), (0,)), ((), ()))
    if b_stack and mode == "nn":
        tn = b.shape[-1]
    if b_stack and mode == "nt":
        tk = b.shape[-1]
    if out_stack:
        tn = N // N_DEV
    tm, tn, tk = _pick(M, (tm,)), _pick(N, (tn,)), _pick(K, (tk,))
    if M % tm or N % tn or K % tk:
        raise ValueError(f"{name}: tiles {tm},{tn},{tk} do not divide {M},{N},{K}")
    nm, nn, nk = M // tm, N // tn, K // tk

    def parts_idx(t, ntile, parts):
        per = ntile // parts
        return t // per, t % per

    if mode in ("nn", "nt"):
        if a_parts:
            a_spec = pl.BlockSpec((None, tm, tk), lambda m, n, k: (parts_idx(k, nk, a_parts)[0], m, parts_idx(k, nk, a_parts)[1]))
        else:
            a_spec = pl.BlockSpec((tm, tk), lambda m, n, k: (m, k))
    else:
        a_spec = pl.BlockSpec((tk, tm), lambda m, n, k: (k, m))
    if mode == "nn":
        if b_stack:
            b_spec = pl.BlockSpec((None, tk, tn), lambda m, n, k: (n, k, 0))
        else:
            b_spec = pl.BlockSpec((tk, tn), lambda m, n, k: (k, n))
    elif mode == "nt":
        if b_stack:
            b_spec = pl.BlockSpec((None, tn, tk), lambda m, n, k: (k, n, 0))
        else:
            b_spec = pl.BlockSpec((tn, tk), lambda m, n, k: (n, k))
    else:
        if b_parts:
            b_spec = pl.BlockSpec((None, tk, tn), lambda m, n, k: (parts_idx(n, nn, b_parts)[0], k, parts_idx(n, nn, b_parts)[1]))
        else:
            b_spec = pl.BlockSpec((tk, tn), lambda m, n, k: (k, n))
    if out_stack:
        out_shape = jax.ShapeDtypeStruct((N_DEV, M, tn), out_dtype)
        o_spec = pl.BlockSpec((None, tm, tn), lambda m, n, k: (n, m, 0))
    elif out_parts:
        out_shape = jax.ShapeDtypeStruct((out_parts, M, N // out_parts), out_dtype)
        o_spec = pl.BlockSpec((None, tm, tn), lambda m, n, k: (parts_idx(n, nn, out_parts)[0], m, parts_idx(n, nn, out_parts)[1]))
    else:
        out_shape = jax.ShapeDtypeStruct((M, N), out_dtype)
        o_spec = pl.BlockSpec((tm, tn), lambda m, n, k: (m, n))
    has_resid = resid is not None

    def body(*refs):
        o_ref, acc = refs[-2:]
        a_ref, b_ref = refs[:2]
        r_ref = refs[2] if has_resid else None
        k = pl.program_id(2)

        @pl.when(k == 0)
        def _():
            acc[...] = jnp.zeros_like(acc)

        acc[...] += lax.dot_general(a_ref[...], b_ref[...], dims, preferred_element_type=F32)

        @pl.when(k == nk - 1)
        def _():
            res = acc[...]
            if has_resid:
                res = res + r_ref[...]
            o_ref[...] = res.astype(o_ref.dtype)

    in_specs = [a_spec, b_spec]
    args = [a, b]
    if has_resid:
        in_specs.append(pl.BlockSpec((tm, tn), lambda m, n, k: (m, n)))
        args.append(resid)
    for t in after:
        in_specs.append(pl.BlockSpec(memory_space=pl.ANY))
        args.append(t)
    return pl.pallas_call(
        body, name=name, grid=(nm, nn, nk), in_specs=in_specs, out_specs=o_spec, out_shape=out_shape,
        scratch_shapes=[pltpu.VMEM((tm, tn), F32)],
        compiler_params=_params(("parallel", "parallel", "arbitrary")),
    )(*args)


def _rms_fwd(x, g, name):
    R, D = x.shape
    tr = _pick(R, (256,))

    def body(x_ref, g_ref, h_ref, r_ref):
        xv = x_ref[...]
        r = lax.rsqrt(jnp.mean(xv * xv, axis=-1, keepdims=True) + EPS)
        h_ref[...] = (xv * r * g_ref[...]).astype(BF)
        r_ref[...] = r

    return pl.pallas_call(
        body, name=name, grid=(R // tr,),
        in_specs=[pl.BlockSpec((tr, D), lambda i: (i, 0)), pl.BlockSpec((1, D), lambda i: (0, 0))],
        out_specs=[pl.BlockSpec((tr, D), lambda i: (i, 0)), pl.BlockSpec((tr, 1), lambda i: (i, 0))],
        out_shape=[jax.ShapeDtypeStruct((R, D), BF), jax.ShapeDtypeStruct((R, 1), F32)],
        compiler_params=_params(("parallel",)),
    )(x, g)


def _rms_bwd(x, r, g, dh, dres, name, after=()):
    R, D = x.shape
    tr = _pick(R, (256,))
    has_res = dres is not None

    def body(*refs):
        if has_res:
            x_ref, r_ref, g_ref, dh_ref, dres_ref, dx_ref, dxb_ref, dg_ref = refs
        else:
            x_ref, r_ref, g_ref, dh_ref, dx_ref, dxb_ref, dg_ref = refs
        i = pl.program_id(0)
        xv, rv, dhv = x_ref[...], r_ref[...], dh_ref[...]
        gy = dhv * g_ref[...]
        c = jnp.sum(xv * gy, axis=-1, keepdims=True)
        dx = rv * gy - xv * (rv * rv * rv) * (c * (1.0 / D))
        if has_res:
            dx = dx + dres_ref[...]
        dx_ref[...] = dx
        dxb_ref[...] = dx.astype(BF)
        part = jnp.sum(dhv * xv * rv, axis=0, keepdims=True)

        @pl.when(i == 0)
        def _():
            dg_ref[...] = part

        @pl.when(i > 0)
        def _():
            dg_ref[...] += part

    row = pl.BlockSpec((tr, D), lambda i: (i, 0))
    in_specs = [row, pl.BlockSpec((tr, 1), lambda i: (i, 0)), pl.BlockSpec((1, D), lambda i: (0, 0)), row]
    args = [x, r, g, dh]
    if has_res:
        in_specs.append(row)
        args.append(dres)
    return pl.pallas_call(
        _hide(body, len(args), len(after)), name=name, grid=(R // tr,), in_specs=in_specs + _hidden_specs(after),
        out_specs=[row, row, pl.BlockSpec((1, D), lambda i: (0, 0))],
        out_shape=[jax.ShapeDtypeStruct((R, D), F32), jax.ShapeDtypeStruct((R, D), BF), jax.ShapeDtypeStruct((1, D), F32)],
        compiler_params=_params(("arbitrary",)),
    )(*args, *after)


def _a_chunk(us, vs, gvs, ws, bs):
    r_i = lax.broadcasted_iota(jnp.int32, (CHUNK, CHUNK), 0)
    c_i = lax.broadcasted_iota(jnp.int32, (CHUNK, CHUNK), 1)
    causal = r_i >= c_i
    vg = [jax.nn.gelu(v) for v in vs]
    ss = sum(jnp.sum(v * v, axis=-1, keepdims=True) for v in vg)
    r = lax.rsqrt(ss * (1.0 / A_WIDTH) + EPS)
    ys = []
    for g in range(A_GROUPS):
        vn = vg[g] * r * gvs[g]
        w = jnp.where(causal, ws[g], 0.0)
        s = jnp.dot(w.astype(BF), vn.astype(BF), preferred_element_type=F32) + bs[g]
        ys.append(jax.nn.gelu(us[g]) * s)
    return ys


def _a_split(u_ref, v_ref, g_ref, w_ref, b_ref):
    sl = [slice(g * 128, (g + 1) * 128) for g in range(A_GROUPS)]
    return ([u_ref[:, s] for s in sl], [v_ref[:, s] for s in sl], [g_ref[:, s] for s in sl],
            [w_ref[g] for g in range(A_GROUPS)], [b_ref[:, g:g + 1] for g in range(A_GROUPS)])


def _a_specs(S):
    return [pl.BlockSpec((CHUNK, A_WIDTH), lambda n: (n, 0)), pl.BlockSpec((CHUNK, A_WIDTH), lambda n: (n, 1)),
            pl.BlockSpec((1, A_WIDTH), lambda n: (0, 0)), pl.BlockSpec((A_GROUPS, CHUNK, CHUNK), lambda n: (0, 0, 0)),
            pl.BlockSpec((CHUNK, A_GROUPS), lambda n: (0, 0))]


def _a_fwd(proj, g_v, w_s, b_t):
    S = proj.shape[0]

    def body(u_ref, v_ref, g_ref, w_ref, b_ref, y_ref):
        ys = _a_chunk(*_a_split(u_ref, v_ref, g_ref, w_ref, b_ref))
        for g in range(A_GROUPS):
            y_ref[:, g * 128:(g + 1) * 128] = ys[g].astype(BF)

    return pl.pallas_call(
        body, name="a_fwd", grid=(S // CHUNK,), in_specs=_a_specs(S),
        out_specs=pl.BlockSpec((CHUNK, A_WIDTH), lambda n: (n, 0)),
        out_shape=jax.ShapeDtypeStruct((S, A_WIDTH), BF), compiler_params=_params(("parallel",)),
    )(proj, proj, g_v, w_s, b_t)


def _a_bwd(proj, g_v, w_s, b_t, dy, after=()):
    S = proj.shape[0]

    def body(u_ref, v_ref, g_ref, w_ref, b_ref, dy_ref, duv_ref, dg_ref, dw_ref, db_ref):
        n = pl.program_id(0)
        dys = [dy_ref[:, g * 128:(g + 1) * 128] for g in range(A_GROUPS)]
        _, vjp = jax.vjp(_a_chunk, *_a_split(u_ref, v_ref, g_ref, w_ref, b_ref))
        dus, dvs, dgs, dws, dbs = vjp(dys)

        @pl.when(n == 0)
        def _():
            dg_ref[...] = jnp.zeros_like(dg_ref)
            dw_ref[...] = jnp.zeros_like(dw_ref)
            db_ref[...] = jnp.zeros_like(db_ref)

        for g in range(A_GROUPS):
            duv_ref[:, g * 128:(g + 1) * 128] = dus[g].astype(BF)
            duv_ref[:, A_WIDTH + g * 128:A_WIDTH + (g + 1) * 128] = dvs[g].astype(BF)
            dg_ref[:, g * 128:(g + 1) * 128] += dgs[g]
            dw_ref[g] += dws[g]
            db_ref[:, g:g + 1] += dbs[g]

    return pl.pallas_call(
        _hide(body, 6, len(after)), name="a_bwd", grid=(S // CHUNK,),
        in_specs=_a_specs(S) + [pl.BlockSpec((CHUNK, A_WIDTH), lambda n: (n, 0))] + _hidden_specs(after),
        out_specs=[pl.BlockSpec((CHUNK, 2 * A_WIDTH), lambda n: (n, 0)), pl.BlockSpec((1, A_WIDTH), lambda n: (0, 0)),
                   pl.BlockSpec((A_GROUPS, CHUNK, CHUNK), lambda n: (0, 0, 0)), pl.BlockSpec((CHUNK, A_GROUPS), lambda n: (0, 0))],
        out_shape=[jax.ShapeDtypeStruct((S, 2 * A_WIDTH), BF), jax.ShapeDtypeStruct((1, A_WIDTH), F32),
                   jax.ShapeDtypeStruct((A_GROUPS, CHUNK, CHUNK), F32), jax.ShapeDtypeStruct((CHUNK, A_GROUPS), F32)],
        compiler_params=_params(("arbitrary",)),
    )(proj, proj, g_v, w_s, b_t, dy, *after)


def _half_mask(shape, which):
    lane = lax.broadcasted_iota(jnp.int32, shape, len(shape) - 1)
    return (lane >= 64) == (which == 1)


def _pair_norm_rope(x, g, ct, sa, sb):
    lo = _half_mask(x.shape, 0)
    x2 = x * x
    ss_lo = jnp.sum(jnp.where(lo, x2, 0.0), axis=-1, keepdims=True)
    ss_hi = jnp.sum(jnp.where(lo, 0.0, x2), axis=-1, keepdims=True)
    r = jnp.where(lo, lax.rsqrt(ss_lo * (1.0 / B_HEAD_DIM) + EPS), lax.rsqrt(ss_hi * (1.0 / B_HEAD_DIM) + EPS))
    xr = x * r
    xn = xr * g
    out = xn * ct + pltpu.roll(xn, 120, 1) * sa + pltpu.roll(xn, 8, 1) * sb
    return out, xr, r


def _pair_norm_rope_bwd(x, g, ct, sa, sb, dout):
    lo = _half_mask(x.shape, 0)
    _, xr, r = _pair_norm_rope(x, g, ct, sa, sb)
    dxn = dout * ct + pltpu.roll(dout * sa, 8, 1) + pltpu.roll(dout * sb, 120, 1)
    gy = dxn * g
    t = xr * gy
    c_lo = jnp.sum(jnp.where(lo, t, 0.0), axis=-1, keepdims=True)
    c_hi = jnp.sum(jnp.where(lo, 0.0, t), axis=-1, keepdims=True)
    c = jnp.where(lo, c_lo, c_hi)
    dx = r * (gy - xr * c * (1.0 / B_HEAD_DIM))
    dg = jnp.sum(dxn * xr, axis=0, keepdims=True)
    return dx, dg


def _b_pre(proj, gq2, gk2, ct, sa, sb):
    S = proj.shape[0]
    tr = _pick(S, (256,))
    n_pair = B_WIDTH // 128

    def body(q_ref, k_ref, gq_ref, gk_ref, ct_ref, sa_ref, sb_ref, qn_ref, kn_ref):
        ct_v, sa_v, sb_v = ct_ref[...], sa_ref[...], sb_ref[...]
        for p in range(n_pair):
            o, _, _ = _pair_norm_rope(q_ref[:, p * 128:(p + 1) * 128], gq_ref[...], ct_v, sa_v, sb_v)
            qn_ref[:, p * 128:(p + 1) * 128] = o.astype(BF)
        o, _, _ = _pair_norm_rope(k_ref[...], gk_ref[...], ct_v, sa_v, sb_v)
        kn_ref[...] = o.astype(BF)

    tab = pl.BlockSpec((tr, 128), lambda i: (i, 0))
    gsp = pl.BlockSpec((1, 128), lambda i: (0, 0))
    return pl.pallas_call(
        body, name="b_pre", grid=(S // tr,),
        in_specs=[pl.BlockSpec((tr, B_WIDTH), lambda i: (i, 1)), pl.BlockSpec((tr, 128), lambda i: (i, 2 * B_WIDTH // 128)),
                  gsp, gsp, tab, tab, tab],
        out_specs=[pl.BlockSpec((tr, B_WIDTH), lambda i: (i, 0)), tab],
        out_shape=[jax.ShapeDtypeStruct((S, B_WIDTH), BF), jax.ShapeDtypeStruct((S, 128), BF)],
        compiler_params=_params(("parallel",)),
    )(proj, proj, gq2, gk2, ct, sa, sb)


def _b_pre_bwd(proj, gq2, gk2, ct, sa, sb, dqn, dkn, dv):
    S = proj.shape[0]
    tr = _pick(S, (256,))
    n_pair = B_WIDTH // 128

    def body(q_ref, k_ref, gq_ref, gk_ref, ct_ref, sa_ref, sb_ref, dqn_ref, dkn_ref, dv_ref, dqkv_ref, dgq_ref, dgk_ref):
        i = pl.program_id(0)
        ct_v, sa_v, sb_v = ct_ref[...], sa_ref[...], sb_ref[...]
        dgq = jnp.zeros((1, 128), F32)
        for p in range(n_pair):
            sl = slice(p * 128, (p + 1) * 128)
            dx, dg = _pair_norm_rope_bwd(q_ref[:, sl], gq_ref[...], ct_v, sa_v, sb_v, dqn_ref[:, sl])
            dqkv_ref[:, sl] = dx.astype(BF)
            dgq = dgq + dg
        dx, dgk = _pair_norm_rope_bwd(k_ref[...], gk_ref[...], ct_v, sa_v, sb_v, dkn_ref[...])
        dqkv_ref[:, B_WIDTH:B_WIDTH + 128] = dx.astype(BF)
        dqkv_ref[:, B_WIDTH + 128:B_WIDTH + 256] = dv_ref[...].astype(BF)

        @pl.when(i == 0)
        def _():
            dgq_ref[...] = dgq
            dgk_ref[...] = dgk

        @pl.when(i > 0)
        def _():
            dgq_ref[...] += dgq
            dgk_ref[...] += dgk

    tab = pl.BlockSpec((tr, 128), lambda i: (i, 0))
    gsp = pl.BlockSpec((1, 128), lambda i: (0, 0))
    return pl.pallas_call(
        body, name="b_pre_bwd", grid=(S // tr,),
        in_specs=[pl.BlockSpec((tr, B_WIDTH), lambda i: (i, 1)), pl.BlockSpec((tr, 128), lambda i: (i, 2 * B_WIDTH // 128)),
                  gsp, gsp, tab, tab, tab, pl.BlockSpec((tr, B_WIDTH), lambda i: (i, 0)), tab, tab],
        out_specs=[pl.BlockSpec((tr, B_WIDTH + 256), lambda i: (i, 0)), gsp, gsp],
        out_shape=[jax.ShapeDtypeStruct((S, B_WIDTH + 256), BF), jax.ShapeDtypeStruct((1, 128), F32), jax.ShapeDtypeStruct((1, 128), F32)],
        compiler_params=_params(("arbitrary",)),
    )(proj, proj, gq2, gk2, ct, sa, sb, dqn, dkn, dv)


def _b_dup(x2, g):
    d = jnp.where(_half_mask(x2.shape, g), x2, 0.0)
    return (d + pltpu.roll(d, 64, 1)).astype(BF)


def _b_valid(n):
    row = lax.broadcasted_iota(jnp.int32, (CHUNK, 2 * CHUNK), 0)
    col = lax.broadcasted_iota(jnp.int32, (CHUNK, 2 * CHUNK), 1)
    rel = row + CHUNK - col
    return (rel >= 0) & (rel < CHUNK) & ((col >= CHUNK) | (n > 0))


def _b_probs(qm, kd, valid, sink):
    s = lax.dot_general(qm, kd, (((1,), (1,)), ((), ())), preferred_element_type=F32) * (B_HEAD_DIM ** -0.5)
    s = jnp.where(valid, s, NEG)
    m = jnp.maximum(jnp.max(s, axis=-1, keepdims=True), sink)
    e = jnp.exp(s - m)
    es = jnp.exp(sink - m)
    inv = 1.0 / (jnp.sum(e, axis=-1, keepdims=True) + es)
    return e * inv, es * inv


def _b_kv_specs(S):
    prev = lambda n: (jnp.maximum(n - 1, 0), 0)
    cur = lambda n: (n, 0)
    v_col = (2 * B_WIDTH + B_KV_WIDTH) // 128
    return [pl.BlockSpec((CHUNK, 128), prev), pl.BlockSpec((CHUNK, 128), cur),
            pl.BlockSpec((CHUNK, 128), lambda n: (jnp.maximum(n - 1, 0), v_col)), pl.BlockSpec((CHUNK, 128), lambda n: (n, v_col))]


def _b_attn_fwd(qn, kn, proj, sinks):
    S = qn.shape[0]

    def body(s_ref, q_ref, kp_ref, kc_ref, vp_ref, vc_ref, y_ref):
        n = pl.program_id(0)
        valid = _b_valid(n)
        k2 = jnp.concatenate([kp_ref[...], kc_ref[...]], axis=0).astype(F32)
        v2 = jnp.concatenate([vp_ref[...], vc_ref[...]], axis=0)
        for g in range(B_KV_HEADS):
            kd, vd = _b_dup(k2, g), _b_dup(v2, g)
            for pp in range(B_HEADS // B_KV_HEADS // 2):
                p = g * (B_HEADS // B_KV_HEADS // 2) + pp
                q_pair = q_ref[:, p * 128:(p + 1) * 128]
                o_pair = jnp.zeros((CHUNK, 128), F32)
                for hf in range(2):
                    hm = _half_mask((CHUNK, 128), hf)
                    qm = jnp.where(hm, q_pair, jnp.zeros_like(q_pair))
                    pr, _ = _b_probs(qm, kd, valid, s_ref[0, 2 * p + hf])
                    o = jnp.dot(pr.astype(BF), vd, preferred_element_type=F32)
                    o_pair = o_pair + jnp.where(hm, o, 0.0)
                y_ref[:, p * 128:(p + 1) * 128] = o_pair.astype(BF)

    return pl.pallas_call(
        body, name="b_attn_fwd", grid=(S // CHUNK,),
        in_specs=[pl.BlockSpec(memory_space=pltpu.SMEM), pl.BlockSpec((CHUNK, B_WIDTH), lambda n: (n, 0))] + _b_kv_specs(S),
        out_specs=pl.BlockSpec((CHUNK, B_WIDTH), lambda n: (n, 0)),
        out_shape=jax.ShapeDtypeStruct((S, B_WIDTH), BF), compiler_params=_params(("arbitrary",)),
    )(sinks, qn, kn, kn, proj, proj)


def _b_attn_bwd(qn, kn, proj, sinks, dy, after=()):
    S = qn.shape[0]

    def body(s_ref, q_ref, kp_ref, kc_ref, vp_ref, vc_ref, dy_ref, dq_ref, dk_ref, dv_ref, ds_ref):
        n = pl.program_id(0)

        @pl.when(n == 0)
        def _():
            dk_ref[...] = jnp.zeros_like(dk_ref)
            dv_ref[...] = jnp.zeros_like(dv_ref)
            ds_ref[...] = jnp.zeros_like(ds_ref)

        valid = _b_valid(n)
        k2 = jnp.concatenate([kp_ref[...], kc_ref[...]], axis=0).astype(F32)
        v2 = jnp.concatenate([vp_ref[...], vc_ref[...]], axis=0)
        lane = lax.broadcasted_iota(jnp.int32, (CHUNK, 128), 1)
        dk2 = jnp.zeros((2 * CHUNK, 128), F32)
        dv2 = jnp.zeros((2 * CHUNK, 128), F32)
        dsink = jnp.zeros((CHUNK, 128), F32)
        scale = B_HEAD_DIM ** -0.5
        for g in range(B_KV_HEADS):
            kd, vd = _b_dup(k2, g), _b_dup(v2, g)
            dk_acc = jnp.zeros((2 * CHUNK, 128), F32)
            dv_acc = jnp.zeros((2 * CHUNK, 128), F32)
            for pp in range(B_HEADS // B_KV_HEADS // 2):
                p = g * (B_HEADS // B_KV_HEADS // 2) + pp
                q_pair = q_ref[:, p * 128:(p + 1) * 128]
                do_pair = dy_ref[:, p * 128:(p + 1) * 128]
                dq_pair = jnp.zeros((CHUNK, 128), F32)
                for hf in range(2):
                    h = 2 * p + hf
                    hm = _half_mask((CHUNK, 128), hf)
                    qm = jnp.where(hm, q_pair, jnp.zeros_like(q_pair))
                    do = jnp.where(hm, do_pair, 0.0)
                    do_b = do.astype(BF)
                    pr, ps = _b_probs(qm, kd, valid, s_ref[0, h])
                    pr_b = pr.astype(BF)
                    o = jnp.dot(pr_b, vd, preferred_element_type=F32)
                    delta = jnp.sum(do * o, axis=-1, keepdims=True)
                    dp = lax.dot_general(do_b, vd, (((1,), (1,)), ((), ())), preferred_element_type=F32)
                    dsc = (pr * (dp - delta) * scale).astype(BF)
                    dsink = dsink + jnp.where(lane == h, -ps * delta, 0.0)
                    dq = jnp.dot(dsc, kd, preferred_element_type=F32)
                    dq_pair = dq_pair + jnp.where(hm, dq, 0.0)
                    dk_acc = dk_acc + lax.dot_general(dsc, qm, (((0,), (0,)), ((), ())), preferred_element_type=F32)
                    dv_acc = dv_acc + lax.dot_general(pr_b, do_b, (((0,), (0,)), ((), ())), preferred_element_type=F32)
                dq_ref[:, p * 128:(p + 1) * 128] = dq_pair
            gm = _half_mask((2 * CHUNK, 128), g)
            dk2 = dk2 + jnp.where(gm, dk_acc + pltpu.roll(dk_acc, 64, 1), 0.0)
            dv2 = dv2 + jnp.where(gm, dv_acc + pltpu.roll(dv_acc, 64, 1), 0.0)
        ds_ref[...] += dsink
        cur = pl.ds(pl.multiple_of(n * CHUNK, CHUNK), CHUNK)
        dk_ref[cur, :] += dk2[CHUNK:]
        dv_ref[cur, :] += dv2[CHUNK:]

        @pl.when(n > 0)
        def _():
            prv = pl.ds(pl.multiple_of((n - 1) * CHUNK, CHUNK), CHUNK)
            dk_ref[prv, :] += dk2[:CHUNK]
            dv_ref[prv, :] += dv2[:CHUNK]

    full = pl.BlockSpec((S, 128), lambda n: (0, 0))
    return pl.pallas_call(
        _hide(body, 7, len(after)), name="b_attn_bwd", grid=(S // CHUNK,),
        in_specs=[pl.BlockSpec(memory_space=pltpu.SMEM), pl.BlockSpec((CHUNK, B_WIDTH), lambda n: (n, 0))] + _b_kv_specs(S)
        + [pl.BlockSpec((CHUNK, B_WIDTH), lambda n: (n, 0))] + _hidden_specs(after),
        out_specs=[pl.BlockSpec((CHUNK, B_WIDTH), lambda n: (n, 0)), full, full, pl.BlockSpec((CHUNK, 128), lambda n: (0, 0))],
        out_shape=[jax.ShapeDtypeStruct((S, B_WIDTH), F32), jax.ShapeDtypeStruct((S, 128), F32), jax.ShapeDtypeStruct((S, 128), F32),
                   jax.ShapeDtypeStruct((CHUNK, 128), F32)],
        compiler_params=_params(("arbitrary",)),
    )(sinks, qn, kn, kn, proj, proj, dy, *after)


def _c_block(q, k, v, gq, gk):
    qn = q * lax.rsqrt(jnp.mean(q * q, axis=-1, keepdims=True) + EPS) * gq
    kn = k * lax.rsqrt(jnp.mean(k * k, axis=-1, keepdims=True) + EPS) * gk
    s = lax.dot_general(qn.astype(BF), kn.astype(BF), (((1,), (1,)), ((), ())), preferred_element_type=F32) * (C_HEAD_DIM ** -0.5)
    p = jax.nn.softmax(s, axis=-1)
    return jnp.dot(p.astype(BF), v.astype(BF), preferred_element_type=F32)


def _c_specs(S, M, tq):
    q_col = (2 * A_WIDTH + B_WIDTH + 2 * B_KV_WIDTH) // 128
    return [pl.BlockSpec((tq, 128), lambda h, i: (i, q_col + h)), pl.BlockSpec((M, 128), lambda h, i: (0, h)),
            pl.BlockSpec((M, 128), lambda h, i: (0, C_HEADS + h)), pl.BlockSpec((1, 128), lambda h, i: (0, 0)),
            pl.BlockSpec((1, 128), lambda h, i: (0, 0))]


def _c_fwd(proj, kv, gq, gk):
    S, M = proj.shape[0], kv.shape[0]
    tq = _pick(S, (512,))

    def body(q_ref, k_ref, v_ref, gq_ref, gk_ref, y_ref):
        y_ref[...] = _c_block(q_ref[...], k_ref[...], v_ref[...], gq_ref[...], gk_ref[...]).astype(BF)

    return pl.pallas_call(
        body, name="c_fwd", grid=(C_HEADS, S // tq), in_specs=_c_specs(S, M, tq),
        out_specs=pl.BlockSpec((tq, 128), lambda h, i: (i, h)),
        out_shape=jax.ShapeDtypeStruct((S, C_WIDTH), BF), compiler_params=_params(("parallel", "parallel")),
    )(proj, kv, kv, gq, gk)


def _c_bwd(proj, kv, gq, gk, dy):
    S, M = proj.shape[0], kv.shape[0]
    tq = _pick(S, (512,))

    def body(q_ref, k_ref, v_ref, gq_ref, gk_ref, dy_ref, dq_ref, dk_ref, dv_ref, dgq_ref, dgk_ref):
        i = pl.program_id(1)
        _, vjp = jax.vjp(_c_block, q_ref[...], k_ref[...], v_ref[...], gq_ref[...], gk_ref[...])
        dq, dk, dv, dgq, dgk = vjp(dy_ref[...])
        dq_ref[...] = dq.astype(BF)

        @pl.when(i == 0)
        def _():
            dk_ref[...] = dk
            dv_ref[...] = dv
            dgq_ref[...] = dgq
            dgk_ref[...] = dgk

        @pl.when(i > 0)
        def _():
            dk_ref[...] += dk
            dv_ref[...] += dv
            dgq_ref[...] += dgq
            dgk_ref[...] += dgk

    return pl.pallas_call(
        body, name="c_bwd", grid=(C_HEADS, S // tq),
        in_specs=_c_specs(S, M, tq) + [pl.BlockSpec((tq, 128), lambda h, i: (i, h))],
        out_specs=[pl.BlockSpec((tq, 128), lambda h, i: (i, h)), pl.BlockSpec((M, 128), lambda h, i: (0, h)),
                   pl.BlockSpec((M, 128), lambda h, i: (0, h)), pl.BlockSpec((None, 1, 128), lambda h, i: (h, 0, 0)),
                   pl.BlockSpec((None, 1, 128), lambda h, i: (h, 0, 0))],
        out_shape=[jax.ShapeDtypeStruct((S, C_WIDTH), BF), jax.ShapeDtypeStruct((M, C_WIDTH), F32), jax.ShapeDtypeStruct((M, C_WIDTH), F32),
                   jax.ShapeDtypeStruct((C_HEADS, 1, 128), F32), jax.ShapeDtypeStruct((C_HEADS, 1, 128), F32)],
        compiler_params=_params(("parallel", "arbitrary")),
    )(proj, kv, kv, gq, gk, dy)


def _merge_specs(S, D, tr, tc):
    off = GATE_OFF // tc
    nd = D // tc
    gates = [pl.BlockSpec((tr, tc), functools.partial(lambda b, i, j: (i, off + b * nd + j), b)) for b in range(3)]
    zs = [pl.BlockSpec((tr, tc), lambda i, j: (i, j)) for _ in range(3)]
    return gates + zs


def _merge_fwd(proj, za, zb, zc):
    S, D = za.shape
    tr, tc = _pick(S, (512,)), _pick(D, (256,))

    def body(ga_ref, gb_ref, gc_ref, za_ref, zb_ref, zc_ref, m_ref):
        acc = jax.nn.sigmoid(ga_ref[...]) * za_ref[...].astype(F32)
        acc = acc + jax.nn.sigmoid(gb_ref[...]) * zb_ref[...].astype(F32)
        acc = acc + jax.nn.sigmoid(gc_ref[...]) * zc_ref[...].astype(F32)
        m_ref[...] = acc.astype(BF)

    return pl.pallas_call(
        body, name="merge_fwd", grid=(S // tr, D // tc), in_specs=_merge_specs(S, D, tr, tc),
        out_specs=pl.BlockSpec((tr, tc), lambda i, j: (i, j)), out_shape=jax.ShapeDtypeStruct((S, D), BF),
        compiler_params=_params(("parallel", "parallel")),
    )(proj, proj, proj, za, zb, zc)


def _merge_bwd(proj, za, zb, zc, dm, after=()):
    S, D = za.shape
    tr, tc = _pick(S, (512,)), _pick(D, (256,))
    nd = D // tc

    def body(ga_ref, gb_ref, gc_ref, za_ref, zb_ref, zc_ref, dm_ref, dza_ref, dzb_ref, dzc_ref, dga_ref, dgb_ref, dgc_ref):
        dmv = dm_ref[...]
        for g_ref, z_ref, dz_ref, dg_ref in ((ga_ref, za_ref, dza_ref, dga_ref), (gb_ref, zb_ref, dzb_ref, dgb_ref),
                                             (gc_ref, zc_ref, dzc_ref, dgc_ref)):
            sg = jax.nn.sigmoid(g_ref[...])
            dz_ref[...] = (sg * dmv).astype(BF)
            dg_ref[...] = (dmv * z_ref[...].astype(F32) * sg * (1.0 - sg)).astype(BF)

    tile = pl.BlockSpec((tr, tc), lambda i, j: (i, j))
    return pl.pallas_call(
        _hide(body, 7, len(after)), name="merge_bwd", grid=(S // tr, D // tc),
        in_specs=_merge_specs(S, D, tr, tc) + [tile] + _hidden_specs(after),
        out_specs=[tile, tile, tile, tile, tile, tile],
        out_shape=[jax.ShapeDtypeStruct((S, D), BF)] * 6,
        compiler_params=_params(("parallel", "parallel")),
    )(proj, proj, proj, za, zb, zc, dm, *after)


def _shift_down(u, k):
    t = lax.broadcasted_iota(jnp.int32, u.shape, 0)
    return jnp.where(t >= k, pltpu.roll(u, k, 0), 0.0)


def _shift_up(u, k):
    n = u.shape[0]
    t = lax.broadcasted_iota(jnp.int32, u.shape, 0)
    return jnp.where(t < n - k, pltpu.roll(u, n - k, 0), 0.0)


def _conv3(u, w, b):
    return u * w[2:3] + _shift_down(u, 1) * w[1:2] + _shift_down(u, 2) * w[0:1] + b


def _ffn_specs(S, F, tc):
    return [pl.BlockSpec((2, S, tc), lambda j: (0, 0, j)), pl.BlockSpec((2, 3, tc), lambda j: (0, 0, j)),
            pl.BlockSpec((2, 1, tc), lambda j: (0, 0, j))]


def _ffn_act_fwd(up3, cw3, cb3):
    _, S, F = up3.shape
    tc = _pick(F, (128,))

    def body(u_ref, w_ref, b_ref, o_ref):
        ca = _conv3(u_ref[0], w_ref[0], b_ref[0])
        cb = _conv3(u_ref[1], w_ref[1], b_ref[1])
        o_ref[...] = (ca * jax.nn.sigmoid(ca) * cb).astype(BF)

    return pl.pallas_call(
        body, name="ffn_act_fwd", grid=(F // tc,), in_specs=_ffn_specs(S, F, tc),
        out_specs=pl.BlockSpec((S, tc), lambda j: (0, j)), out_shape=jax.ShapeDtypeStruct((S, F), BF),
        compiler_params=_params(("parallel",)),
    )(up3, cw3, cb3)


def _ffn_act_bwd(up3, cw3, cb3, dact, after=()):
    _, S, F = up3.shape
    tc = _pick(F, (128,))

    def body(u_ref, w_ref, b_ref, da_ref, du_ref, dw_ref, db_ref):
        ca = _conv3(u_ref[0], w_ref[0], b_ref[0])
        cb = _conv3(u_ref[1], w_ref[1], b_ref[1])
        sg = jax.nn.sigmoid(ca)
        dav = da_ref[...]
        dcs = (dav * cb * sg * (1.0 + ca * (1.0 - sg)), dav * ca * sg)
        for part in range(2):
            dc, w, u = dcs[part], w_ref[part], u_ref[part]
            du_ref[part] = (dc * w[2:3] + _shift_up(dc, 1) * w[1:2] + _shift_up(dc, 2) * w[0:1]).astype(BF)
            dw_ref[part, 2:3, :] = jnp.sum(dc * u, axis=0, keepdims=True)
            dw_ref[part, 1:2, :] = jnp.sum(dc * _shift_down(u, 1), axis=0, keepdims=True)
            dw_ref[part, 0:1, :] = jnp.sum(dc * _shift_down(u, 2), axis=0, keepdims=True)
            db_ref[part] = jnp.sum(dc, axis=0, keepdims=True)

    return pl.pallas_call(
        _hide(body, 4, len(after)), name="ffn_act_bwd", grid=(F // tc,),
        in_specs=_ffn_specs(S, F, tc) + [pl.BlockSpec((S, tc), lambda j: (0, j))] + _hidden_specs(after),
        out_specs=[pl.BlockSpec((2, S, tc), lambda j: (0, 0, j)), pl.BlockSpec((2, 3, tc), lambda j: (0, 0, j)),
                   pl.BlockSpec((2, 1, tc), lambda j: (0, 0, j))],
        out_shape=[jax.ShapeDtypeStruct((2, S, F), BF), jax.ShapeDtypeStruct((2, 3, F), F32), jax.ShapeDtypeStruct((2, 1, F), F32)],
        compiler_params=_params(("parallel",)),
    )(up3, cw3, cb3, dact, *after)


def _loss(y, target):
    S, D = y.shape
    tr = _pick(S, (256,))

    def body(y_ref, t_ref, dy_ref, dyb_ref, l_ref):
        i = pl.program_id(0)
        e = y_ref[...] - t_ref[...]
        dy = e * (1.0 / D)
        dy_ref[...] = dy
        dyb_ref[...] = dy.astype(BF)
        part = jnp.sum(jnp.sum(e * e, axis=-1, keepdims=True), axis=0, keepdims=True) * (0.5 / D)

        @pl.when(i == 0)
        def _():
            l_ref[...] = jnp.zeros_like(l_ref)

        l_ref[...] += part

    row = pl.BlockSpec((tr, D), lambda i: (i, 0))
    return pl.pallas_call(
        body, name="loss", grid=(S // tr,), in_specs=[row, row],
        out_specs=[row, row, pl.BlockSpec((8, 128), lambda i: (0, 0))],
        out_shape=[jax.ShapeDtypeStruct((S, D), F32), jax.ShapeDtypeStruct((S, D), BF), jax.ShapeDtypeStruct((8, 128), F32)],
        compiler_params=_params(("arbitrary",)),
    )(y, target)


ANY = pl.BlockSpec(memory_space=pl.ANY)


def _allgather(shards, name):
    n = len(shards)

    def body(*refs):
        ins, outs = refs[:n], refs[n:2 * n]
        send_sems, recv_sems, local_sems = refs[2 * n:]
        x, y, c = lax.axis_index("x"), lax.axis_index("y"), lax.axis_index("c")
        me, sibling = (x, y, c), (x, y, 1 - c)
        chips = [(1 - x, y), (x, 1 - y), (1 - x, 1 - y)]

        def blk(w, px, py, pc):
            return outs[w].at[4 * px + 2 * py + pc]

        def copy(w, k, block, to, src=None):
            return pltpu.make_async_remote_copy(
                src_ref=blk(w, *block) if src is None else src, dst_ref=blk(w, *block),
                send_sem=send_sems.at[w, k], recv_sem=recv_sems.at[w, k], device_id=to, device_id_type=MESH)

        started = []
        mine = []
        for w in range(n):
            mine.append(pltpu.make_async_copy(ins[w], blk(w, *me), local_sems.at[w]))
            mine[-1].start()
            first = [copy(w, 0, me, sibling, src=ins[w])]
            first += [copy(w, 1 + j, me, (*chip, c), src=ins[w]) for j, chip in enumerate(chips)]
            for cp in first:
                cp.start()
            started += first
        for w in range(n):
            for j, chip in enumerate(chips):
                copy(w, 1 + j, (*chip, c), me).wait_recv()
                fwd = copy(w, 4 + j, (*chip, c), sibling)
                fwd.start()
                started.append(fwd)
        for w in range(n):
            copy(w, 0, sibling, me).wait_recv()
            for j, chip in enumerate(chips):
                copy(w, 4 + j, (*chip, 1 - c), me).wait_recv()
        for cp in started:
            cp.wait_send()
        for cp in mine:
            cp.wait()

    outs = pl.pallas_call(
        body, name=name, in_specs=[ANY] * n, out_specs=[ANY] * n,
        out_shape=[jax.ShapeDtypeStruct((N_DEV,) + s.shape, s.dtype) for s in shards],
        scratch_shapes=[pltpu.SemaphoreType.DMA((n, 7)), pltpu.SemaphoreType.DMA((n, 7)), pltpu.SemaphoreType.DMA((n,))],
    )(*shards)
    return list(outs)


def _allgather_seq(shards, name, collective_id, after=()):
    n = len(shards)
    n_after = len(after)

    def body(*refs):
        ins, outs = refs[:n], refs[n + n_after:2 * n + n_after]
        send_sems, recv_sems, local_sems = refs[2 * n + n_after:]
        x, y, c = lax.axis_index("x"), lax.axis_index("y"), lax.axis_index("c")
        me, sibling = (x, y, c), (x, y, 1 - c)
        chips = [(1 - x, y), (x, 1 - y), (1 - x, 1 - y)]
        barrier = pltpu.get_barrier_semaphore()
        for peer in [sibling] + [(*chip, c) for chip in chips]:
            pl.semaphore_signal(barrier, inc=1, device_id=peer, device_id_type=MESH)
        pl.semaphore_wait(barrier, 4)

        def blk(w, px, py, pc):
            return outs[w].at[4 * px + 2 * py + pc]

        def copy(w, k, block, to, src=None):
            return pltpu.make_async_remote_copy(
                src_ref=blk(w, *block) if src is None else src, dst_ref=blk(w, *block),
                send_sem=send_sems.at[7 * w + k], recv_sem=recv_sems.at[7 * w + k], device_id=to, device_id_type=MESH)

        started = []
        mine = []
        for w in range(n):
            mine.append(pltpu.make_async_copy(ins[w], blk(w, *me), local_sems.at[w]))
            mine[-1].start()
            first = [copy(w, 0, me, sibling, src=ins[w])]
            first += [copy(w, 1 + j, me, (*chip, c), src=ins[w]) for j, chip in enumerate(chips)]
            for cp in first:
                cp.start()
            started += first
        for w in range(n):
            for j, chip in enumerate(chips):
                copy(w, 1 + j, (*chip, c), me).wait_recv()
                fwd = copy(w, 4 + j, (*chip, c), sibling)
                fwd.start()
                started.append(fwd)
        for w in range(n):
            copy(w, 0, sibling, me).wait_recv()
            for j, chip in enumerate(chips):
                copy(w, 4 + j, (*chip, 1 - c), me).wait_recv()
        for cp in started:
            cp.wait_send()
        for cp in mine:
            cp.wait()

    outs = pl.kernel(
        body, name=name, out_type=[jax.ShapeDtypeStruct((N_DEV,) + s.shape, s.dtype) for s in shards],
        mesh=plsc.ScalarSubcoreMesh(axis_name="seq", num_cores=1),
        scratch_types=[pltpu.SemaphoreType.DMA((7 * n,)), pltpu.SemaphoreType.DMA((7 * n,)), pltpu.SemaphoreType.DMA((n,))],
        compiler_params=pltpu.CompilerParams(collective_id=collective_id),
    )(*shards, *after)
    return list(outs)


def _sibling_exchange(grads, name):
    n = len(grads)

    def body(*refs):
        ins, outs = refs[:n], refs[n:2 * n]
        send_sems, recv_sems = refs[2 * n:]
        x, y, c = lax.axis_index("x"), lax.axis_index("y"), lax.axis_index("c")
        copies = [pltpu.make_async_remote_copy(
            src_ref=ins[w].at[:, 1 - c], dst_ref=outs[w], send_sem=send_sems.at[w], recv_sem=recv_sems.at[w],
            device_id=(x, y, 1 - c), device_id_type=MESH) for w in range(n)]
        for cp in copies:
            cp.start()
        for cp in copies:
            cp.wait()

    outs = pl.pallas_call(
        body, name=name, in_specs=[ANY] * n, out_specs=[ANY] * n,
        out_shape=[jax.ShapeDtypeStruct((g.shape[0],) + g.shape[2:], g.dtype) for g in grads],
        scratch_shapes=[pltpu.SemaphoreType.DMA((n,)), pltpu.SemaphoreType.DMA((n,))],
    )(*grads)
    return list(outs)


def _chip_exchange(sums, name, collective_id):
    n = len(sums)

    def body(*refs):
        ins, outs = refs[:n], refs[n:2 * n]
        send_sems, recv_sems = refs[2 * n:]
        x, y, c = lax.axis_index("x"), lax.axis_index("y"), lax.axis_index("c")
        chips = [(1 - x, y), (x, 1 - y), (1 - x, 1 - y)]
        barrier = pltpu.get_barrier_semaphore()
        for px, py in chips:
            pl.semaphore_signal(barrier, inc=1, device_id=(px, py, c), device_id_type=MESH)
        pl.semaphore_wait(barrier, 3)
        copies = []
        for w in range(n):
            for k, (px, py) in enumerate(chips):
                copies.append(pltpu.make_async_remote_copy(
                    src_ref=ins[w].at[2 * px + py], dst_ref=outs[w].at[k], send_sem=send_sems.at[3 * w + k],
                    recv_sem=recv_sems.at[3 * w + k], device_id=(px, py, c), device_id_type=MESH))
        for cp in copies:
            cp.start()
        for cp in copies:
            cp.wait()

    outs = pl.kernel(
        body, name=name, out_type=[jax.ShapeDtypeStruct((3,) + s.shape[1:], s.dtype) for s in sums],
        mesh=plsc.ScalarSubcoreMesh(axis_name="seq", num_cores=1),
        scratch_types=[pltpu.SemaphoreType.DMA((3 * n,)), pltpu.SemaphoreType.DMA((3 * n,))],
        compiler_params=pltpu.CompilerParams(collective_id=collective_id),
    )(*sums)
    return list(outs)


def _row_tile(r, c, elems=256 * 1024):
    want = max(8, elems // c)
    for t in range(min(want, r) // 8 * 8, 0, -8):
        if r % t == 0:
            return t
    return r


def _pair_add(g4, recv, core, name, after=()):
    _, _, r, c = g4.shape
    tr = _row_tile(r, c)

    def body(core_ref, a_ref, b_ref, o_ref):
        o_ref[...] = (a_ref[...].astype(F32) + b_ref[...].astype(F32)).astype(BF)

    return pl.pallas_call(
        _hide(body, 3, len(after)), name=name,
        grid_spec=pltpu.PrefetchScalarGridSpec(
            num_scalar_prefetch=1, grid=(4, r // tr),
            in_specs=[pl.BlockSpec((None, None, tr, c), lambda p, i, s: (p, s[0], i, 0)),
                      pl.BlockSpec((None, tr, c), lambda p, i, s: (p, i, 0))] + _hidden_specs(after),
            out_specs=pl.BlockSpec((None, tr, c), lambda p, i, s: (p, i, 0))),
        out_shape=jax.ShapeDtypeStruct((4, r, c), BF), compiler_params=_params(("parallel", "parallel")),
    )(core, g4, recv, *after)


def _adam_math(w, g, m, v):
    m = ADAM_B1 * m + (1.0 - ADAM_B1) * g
    v = ADAM_B2 * v + (1.0 - ADAM_B2) * (g * g)
    m_hat = m / (1.0 - ADAM_B1 ** ADAM_STEP)
    v_hat = v / (1.0 - ADAM_B2 ** ADAM_STEP)
    delta = -ADAM_LR * (m_hat / (jnp.sqrt(v_hat) + ADAM_EPS) + ADAM_WD * w)
    return delta, m, v


def _adamw_big(sums, recv, chip, w, m, v, name, after=()):
    r, c = w.shape
    tr = _row_tile(r, c, 128 * 1024)

    def body(chip_ref, s_ref, r_ref, w_ref, m_ref, v_ref, g_out, d_out, m_out, v_out):
        g = s_ref[...].astype(F32) + r_ref[0].astype(F32)
        g = g + r_ref[1].astype(F32)
        g = g + r_ref[2].astype(F32)
        delta, mn, vn = _adam_math(w_ref[...], g, m_ref[...], v_ref[...])
        g_out[...] = g
        d_out[...] = delta
        m_out[...] = mn
        v_out[...] = vn

    row = pl.BlockSpec((tr, c), lambda i, s: (i, 0))
    return pl.pallas_call(
        _hide(body, 6, len(after)), name=name,
        grid_spec=pltpu.PrefetchScalarGridSpec(
            num_scalar_prefetch=1, grid=(r // tr,),
            in_specs=[pl.BlockSpec((None, tr, c), lambda i, s: (s[0], i, 0)), pl.BlockSpec((3, tr, c), lambda i, s: (0, i, 0)),
                      row, row, row] + _hidden_specs(after),
            out_specs=[row, row, row, row]),
        out_shape=[jax.ShapeDtypeStruct((r, c), F32)] * 4, compiler_params=_params(("parallel",)),
    )(chip, sums, recv, w, m, v, *after)


def _adamw_small(parts, w, m, v, name):
    R = w.shape[0]
    tr = R

    def body(p_ref, w_ref, m_ref, v_ref, g_out, d_out, m_out, v_out):
        g = p_ref[0]
        for d in range(1, N_DEV):
            g = g + p_ref[d]
        delta, mn, vn = _adam_math(w_ref[...], g, m_ref[...], v_ref[...])
        g_out[...] = g
        d_out[...] = delta
        m_out[...] = mn
        v_out[...] = vn

    row = pl.BlockSpec((tr, 128), lambda i: (i, 0))
    return pl.pallas_call(
        body, name=name, grid=(R // tr,),
        in_specs=[pl.BlockSpec((N_DEV, tr, 128), lambda i: (0, i, 0)), row, row, row], out_specs=[row, row, row, row],
        out_shape=[jax.ShapeDtypeStruct((R, 128), F32)] * 4, compiler_params=_params(("parallel",)),
    )(parts, w, m, v)


def _adamw_plain(g, w, m, v, name):
    def body(g_ref, w_ref, m_ref, v_ref, d_out, m_out, v_out):
        delta, mn, vn = _adam_math(w_ref[...], g_ref[...], m_ref[...], v_ref[...])
        d_out[...] = delta
        m_out[...] = mn
        v_out[...] = vn

    return pl.pallas_call(body, name=name, out_shape=[jax.ShapeDtypeStruct(w.shape, F32)] * 3)(g, w, m, v)


def _pack(arrays):
    rows = []
    for a in arrays:
        flat = a.reshape(-1).astype(F32)
        n = flat.shape[0]
        padded = -(-n // 1024) * 1024
        rows.append(jnp.pad(flat, (0, padded - n)).reshape(padded // 128, 128))
    return jnp.concatenate(rows, axis=0)


def _unpack(packed, shapes):
    out, row = [], 0
    for s in shapes:
        n = 1
        for d in s:
            n *= d
        nrow = -(-n // 1024) * 8
        out.append(packed[row:row + nrow].reshape(-1)[:n].reshape(s))
        row += nrow
    return out


def kernel(x, mem, positions, g_mix, w_in, g_a_v, w_spatial, b_spatial, g_b_q, g_b_k, sinks, g_mem, w_mem_kv, g_c_q, g_c_k, w_branch_a, w_branch_b, w_branch_c, w_out, g_ffn, w_up, conv_w, conv_b, w_down, loss_target, m_g_mix, m_w_in, m_g_a_v, m_w_spatial, m_b_spatial, m_g_b_q, m_g_b_k, m_sinks, m_g_mem, m_w_mem_kv, m_g_c_q, m_g_c_k, m_w_branch_a, m_w_branch_b, m_w_branch_c, m_w_out, m_g_ffn, m_w_up, m_conv_w, m_conv_b, m_w_down, v_g_mix, v_w_in, v_g_a_v, v_w_spatial, v_b_spatial, v_g_b_q, v_g_b_k, v_sinks, v_g_mem, v_w_mem_kv, v_g_c_q, v_g_c_k, v_w_branch_a, v_w_branch_b, v_w_branch_c, v_w_out, v_g_ffn, v_w_up, v_conv_w, v_conv_b, v_w_down):
    S, D = x.shape[1], x.shape[2]
    M = mem.shape[1]
    F = w_down.shape[1] * N_DEV
    in_cols = w_in.shape[2] * N_DEV
    ax, ay, ac = lax.axis_index("x"), lax.axis_index("y"), lax.axis_index("c")
    core = jnp.reshape(ac, (1,)).astype(jnp.int32)
    chip = jnp.reshape(2 * ax + ay, (1,)).astype(jnp.int32)
    me = 4 * ax + 2 * ay + ac

    x2, mem2, tgt2 = x[0], mem[0], loss_target[0]

    big = dict(w_in=w_in[0].T, w_mem_kv=w_mem_kv[0], w_branch_a=w_branch_a[0], w_branch_b=w_branch_b[0],
               w_branch_c=w_branch_c[0], w_out=w_out[0], w_up=w_up[0], w_down=w_down[0])
    names = list(big)
    cast = {k: big[k].astype(BF) for k in names}
    W = {}
    gathered = _allgather([conv_w[0]], "ag_conv_w")
    cw3 = gathered[-1].reshape(2, N_DEV // 2, 3, 2 * F // N_DEV).transpose(0, 2, 1, 3).reshape(2, 3, F)
    cb3 = conv_b.reshape(2, 1, F)
    W["w_in"], = _allgather_seq([cast["w_in"]], "ag_seq0", 0)
    w_in_t = W["w_in"].reshape(in_cols, D)
    grp1 = ["w_mem_kv", "w_branch_a", "w_branch_b", "w_branch_c", "w_out"]
    W.update(zip(grp1, _allgather_seq([cast[k] for k in grp1], "ag_seq1", 1, after=(_token(w_in_t, "tok_w_in"),))))
    w_kv_f = W["w_mem_kv"].reshape(D, 2 * C_WIDTH)
    w_out_f = W["w_out"].reshape(D, D)

    half = ROPE_DIM // 2
    inv = ROPE_THETA ** (-jnp.arange(half, dtype=F32) / half)
    ang = positions[0].astype(F32)[:, None] * inv
    cos, sin = jnp.cos(ang), jnp.sin(ang)
    one, zero = jnp.ones((S, B_HEAD_DIM - ROPE_DIM), F32), jnp.zeros((S, B_HEAD_DIM - ROPE_DIM), F32)
    z8 = jnp.zeros((S, half), F32)
    ct = jnp.tile(jnp.concatenate([cos, cos, one], axis=1), (1, 2))
    sa = jnp.tile(jnp.concatenate([-sin, z8, zero], axis=1), (1, 2))
    sb = jnp.tile(jnp.concatenate([z8, sin, zero], axis=1), (1, 2))
    gq2, gk2 = jnp.tile(g_b_q, (1, 2)), jnp.tile(g_b_k, (1, 2))
    b_t = b_spatial[0].T

    h, rstd1 = _rms_fwd(x2, g_mix, "rms1_fwd")
    proj = _mm(h, w_in_t, "nt", F32, "mm_proj", tn=1280)
    y_a = _a_fwd(proj, g_a_v, w_spatial[0], b_t)
    qn, kn = _b_pre(proj, gq2, gk2, ct, sa, sb)
    y_b = _b_attn_fwd(qn, kn, proj, sinks)
    mem_h, rstd_m = _rms_fwd(mem2, g_mem, "rmsmem_fwd")
    kv = _mm(mem_h, w_kv_f, "nn", F32, "mm_kv", after=(y_b,))
    W["w_up"], = _allgather_seq([cast["w_up"]], "ag_seq2", 2, after=(kv,))
    y_c = _c_fwd(proj, kv, g_c_q, g_c_k)
    z_a = _mm(y_a, W["w_branch_a"], "nn", BF, "mm_za", b_stack=True, after=(y_b,))
    z_b = _mm(y_b, W["w_branch_b"], "nn", BF, "mm_zb", b_stack=True)
    z_c = _mm(y_c, W["w_branch_c"], "nn", BF, "mm_zc", b_stack=True)
    merged = _merge_fwd(proj, z_a, z_b, z_c)
    x1 = _mm(merged, w_out_f, "nn", F32, "mm_x1", resid=x2)
    h2, rstd2 = _rms_fwd(x1, g_ffn, "rms2_fwd")
    W["w_down"], = _allgather_seq([cast["w_down"]], "ag_seq3", 3, after=(W["w_up"], h2))
    w_down_f = W["w_down"].reshape(F, D)
    up3 = _mm(h2, W["w_up"], "nn", F32, "mm_up", b_stack=True, out_parts=2)
    act = _ffn_act_fwd(up3, cw3, cb3)
    y = _mm(act, w_down_f, "nn", F32, "mm_y", resid=x1, tk=512)
    dy, dy_b, loss_acc = _loss(y, tgt2)
    loss = lax.psum(loss_acc[0, 0], ("x", "y", "c"))

    reduced = {}

    def reduce_group(gi, keys, stacked, add_after):
        g4 = [g.reshape(4, 2, g.shape[1], g.shape[2]) for g in stacked]
        from_sibling = _sibling_exchange(g4, f"rs_sib{gi}")
        sums = [_pair_add(a, b, core, "rs_add_" + k, after=add_after) for k, a, b in zip(keys, g4, from_sibling)]
        from_chips = _chip_exchange(sums, f"rs_chip{gi}", 4 + gi)
        reduced.update(zip(keys, zip(sums, from_chips)))
        return tuple(sums)

    d_act = _mm(dy_b, w_down_f, "nt", F32, "mm_dact", tn=1408)
    g_down = _mm(act, dy_b, "tn", BF, "mm_gdown", tm=1408, tk=512)
    d_up3, d_cw3, d_cb3 = _ffn_act_bwd(up3, cw3, cb3, d_act, after=(g_down,))
    sums0 = reduce_group(0, ["w_down"], [g_down.reshape(N_DEV, F // N_DEV, D)], (d_up3,))
    d_h2 = _mm(d_up3, W["w_up"], "nt", F32, "mm_dh2", a_parts=2, b_stack=True, after=sums0)
    g_up = _mm(h2, d_up3, "tn", BF, "mm_gup", b_parts=2, out_stack=True, tk=512)
    dx1, dx1_b, d_g_ffn = _rms_bwd(x1, rstd2, g_ffn, d_h2, dy, "rms2_bwd", after=(g_up,))
    d_merged = _mm(dx1_b, w_out_f, "nt", F32, "mm_dmerged")
    g_out = _mm(merged, dx1_b, "tn", BF, "mm_gout", tk=512)
    sums1 = reduce_group(1, ["w_up"], [g_up], (g_out,))
    dz_a, dz_b, dz_c, dga, dgb, dgc = _merge_bwd(proj, z_a, z_b, z_c, d_merged, after=sums1)
    dy_a = _mm(dz_a, W["w_branch_a"], "nt", F32, "mm_dya", b_stack=True)
    dy_b_ = _mm(dz_b, W["w_branch_b"], "nt", F32, "mm_dyb", b_stack=True)
    dy_c = _mm(dz_c, W["w_branch_c"], "nt", F32, "mm_dyc", b_stack=True)
    g_ba = _mm(y_a, dz_a, "tn", BF, "mm_gba", out_stack=True, tk=512)
    g_bb = _mm(y_b, dz_b, "tn", BF, "mm_gbb", out_stack=True, tk=512)
    g_bc = _mm(y_c, dz_c, "tn", BF, "mm_gbc", out_stack=True, tk=512)
    d_uv, d_g_a_v, d_w_s, d_b_t = _a_bwd(proj, g_a_v, w_spatial[0], b_t, dy_a, after=(g_ba, g_bb, g_bc))
    sums2 = reduce_group(2, ["w_out", "w_branch_a", "w_branch_b", "w_branch_c"],
                         [g_out.reshape(N_DEV, D // N_DEV, D), g_ba, g_bb, g_bc], (d_uv,))
    dqn, dkn, dv_b, dsink_rows = _b_attn_bwd(qn, kn, proj, sinks, dy_b_, after=sums2)
    d_qkv, d_gq2, d_gk2 = _b_pre_bwd(proj, gq2, gk2, ct, sa, sb, dqn, dkn, dv_b)
    dq_c, dk_c, dv_c, d_gcq, d_gck = _c_bwd(proj, kv, g_c_q, g_c_k, dy_c)
    dkv_b = jnp.concatenate([dk_c, dv_c], axis=1).astype(BF)
    d_memh = _mm(dkv_b, w_kv_f, "nt", F32, "mm_dmemh")
    g_kv = _mm(mem_h, dkv_b, "tn", BF, "mm_gkv", tk=512)
    _, _, d_g_mem = _rms_bwd(mem2, rstd_m, g_mem, d_memh, None, "rmsmem_bwd")
    dproj = jnp.concatenate([d_uv, d_qkv, dq_c, dga, dgb, dgc], axis=1)
    g_in = _mm(dproj, h, "tn", BF, "mm_gin", tm=1280, tk=512)
    d_h = _mm(dproj, w_in_t, "nn", F32, "mm_dh", tk=1280, after=(g_in, g_kv))
    sums3 = reduce_group(3, ["w_in", "w_mem_kv"],
                         [g_in.reshape(N_DEV, in_cols // N_DEV, D), g_kv.reshape(N_DEV, D // N_DEV, 2 * C_WIDTH)], (d_h,))
    grad_x, _, d_g_mix = _rms_bwd(x2, rstd1, g_mix, d_h, dx1, "rms1_bwd", after=sums3)

    d_conv_w = d_cw3.reshape(2, 3, N_DEV // 2, 2 * F // N_DEV).transpose(1, 0, 2, 3).reshape(3, 2 * F)
    small_names = ["g_mix", "g_a_v", "w_spatial", "b_spatial", "g_b_q", "g_b_k", "sinks", "g_mem", "g_c_q", "g_c_k", "g_ffn", "conv_b"]
    small_w = dict(g_mix=g_mix, g_a_v=g_a_v, w_spatial=w_spatial, b_spatial=b_spatial, g_b_q=g_b_q, g_b_k=g_b_k, sinks=sinks,
                   g_mem=g_mem, g_c_q=g_c_q, g_c_k=g_c_k, g_ffn=g_ffn, conv_b=conv_b)
    small_m = dict(g_mix=m_g_mix, g_a_v=m_g_a_v, w_spatial=m_w_spatial, b_spatial=m_b_spatial, g_b_q=m_g_b_q, g_b_k=m_g_b_k,
                   sinks=m_sinks, g_mem=m_g_mem, g_c_q=m_g_c_q, g_c_k=m_g_c_k, g_ffn=m_g_ffn, conv_b=m_conv_b)
    small_v = dict(g_mix=v_g_mix, g_a_v=v_g_a_v, w_spatial=v_w_spatial, b_spatial=v_b_spatial, g_b_q=v_g_b_q, g_b_k=v_g_b_k,
                   sinks=v_sinks, g_mem=v_g_mem, g_c_q=v_g_c_q, g_c_k=v_g_c_k, g_ffn=v_g_ffn, conv_b=v_conv_b)
    small_g = dict(
        g_mix=d_g_mix, g_a_v=d_g_a_v, w_spatial=d_w_s, b_spatial=d_b_t.T,
        g_b_q=d_gq2.reshape(2, B_HEAD_DIM).sum(0), g_b_k=d_gk2.reshape(2, B_HEAD_DIM).sum(0),
        sinks=dsink_rows.sum(0)[:B_HEADS], g_mem=d_g_mem, g_c_q=d_gcq.sum(0), g_c_k=d_gck.sum(0), g_ffn=d_g_ffn,
        conv_b=d_cb3)
    cw_zero = jnp.zeros((3, 2 * F), F32)
    packed_g = _pack([small_g[k] for k in small_names] + [d_conv_w])
    packed_w = _pack([small_w[k] for k in small_names] + [cw_zero])
    packed_m = _pack([small_m[k] for k in small_names] + [cw_zero])
    packed_v = _pack([small_v[k] for k in small_names] + [cw_zero])
    parts = _allgather([packed_g], "ag_small")[0]
    sg, sd, sm, sv = _adamw_small(parts, packed_w, packed_m, packed_v, "adamw_small")
    shapes = [small_w[k].shape for k in small_names] + [(3, 2 * F)]
    sg_l, sd_l, sm_l, sv_l = (_unpack(p, shapes) for p in (sg, sd, sm, sv))
    small_out = {k: [sg_l[i], sd_l[i], sm_l[i], sv_l[i]] for i, k in enumerate(small_names)}
    c_cw = 2 * F // N_DEV
    g_cw = lax.dynamic_slice_in_dim(sg_l[-1], me * c_cw, c_cw, axis=1)
    cw_res = _adamw_plain(g_cw, conv_w[0], m_conv_w[0], v_conv_w[0], "adamw_conv_w")
    big_out = {"conv_w": [g_cw[None]] + [a[None] for a in cw_res]}

    moments = dict(w_in=(m_w_in, v_w_in), w_mem_kv=(m_w_mem_kv, v_w_mem_kv), w_branch_a=(m_w_branch_a, v_w_branch_a),
                   w_branch_b=(m_w_branch_b, v_w_branch_b), w_branch_c=(m_w_branch_c, v_w_branch_c), w_out=(m_w_out, v_w_out),
                   w_up=(m_w_up, v_w_up), w_down=(m_w_down, v_w_down))
    token = (grad_x, sg)
    for k in ["w_down", "w_up", "w_out", "w_branch_a", "w_branch_b", "w_branch_c", "w_mem_kv", "w_in"]:
        s, r = reduced[k]
        mk, vk = moments[k][0][0], moments[k][1][0]
        if k == "w_in":
            res = _adamw_big(s, r, chip, big[k], mk.T, vk.T, "adamw_" + k, after=token)
            big_out[k] = [a.T[None] for a in res]
        else:
            res = _adamw_big(s, r, chip, big[k], mk, vk, "adamw_" + k, after=token)
            big_out[k] = [a[None] for a in res]
        token = (res[0],)

    order = ["g_mix", "w_in", "g_a_v", "w_spatial", "b_spatial", "g_b_q", "g_b_k", "sinks", "g_mem", "w_mem_kv", "g_c_q", "g_c_k",
             "w_branch_a", "w_branch_b", "w_branch_c", "w_out", "g_ffn", "w_up", "conv_w", "conv_b", "w_down"]
    res = {**small_out, **big_out}
    outs = [loss, grad_x[None]]
    for field in range(4):
        outs += [res[k][field] for k in order]
    return tuple(outs)
```

```python
import functools

import jax
import jax.numpy as jnp
from jax import lax
from jax.experimental import pallas as pl
from jax.experimental.pallas import tpu as pltpu
from jax.experimental.pallas import tpu_sc as plsc

F32 = jnp.float32
BF = jnp.bfloat16
EPS = 1e-6
NEG = -1e30

N_DEV = 8
CHUNK = 128
A_GROUPS = 4
A_WIDTH = 512
B_HEADS = 16
B_KV_HEADS = 2
B_HEAD_DIM = 64
B_WIDTH = 1024
B_KV_WIDTH = 128
ROPE_DIM = 16
ROPE_THETA = 500000.0
C_HEADS = 4
C_HEAD_DIM = 128
C_WIDTH = 512
GATE_OFF = 2 * A_WIDTH + B_WIDTH + 2 * B_KV_WIDTH + C_WIDTH

ADAM_LR = 0.001
ADAM_B1 = 0.9
ADAM_B2 = 0.999
ADAM_EPS = 1e-08
ADAM_WD = 0.01
ADAM_STEP = 10

VMEM_LIMIT = 48 * 1024 * 1024
MESH = pl.DeviceIdType.MESH


def _pick(n, prefs):
    for p in prefs:
        if p <= n and n % p == 0:
            return p
    return n


def _params(sem):
    return pltpu.CompilerParams(dimension_semantics=sem, vmem_limit_bytes=VMEM_LIMIT)


def _hide(body, n_seen, n_hidden):
    if not n_hidden:
        return body

    def wrapped(*refs):
        return body(*refs[:n_seen], *refs[n_seen + n_hidden:])

    return wrapped


def _hidden_specs(after):
    return [pl.BlockSpec(memory_space=pl.ANY) for _ in after]


def _token(x, name):
    def body(x_ref, o_ref):
        o_ref[...] = jnp.zeros_like(o_ref)

    return pl.pallas_call(body, name=name, in_specs=[pl.BlockSpec(memory_space=pl.ANY)],
                          out_shape=jax.ShapeDtypeStruct((8, 128), F32))(x)


def _mm(a, b, mode, out_dtype, name, *, resid=None, b_stack=False, a_parts=0, b_parts=0, out_parts=0,
        out_stack=False, tm=1024, tn=1024, tk=2048, after=()):
    if mode == "nn":
        M = a.shape[-2]
        K = a.shape[-1] * max(a_parts, 1)
        N = b.shape[-1] * (N_DEV if b_stack else 1)
        dims = (((1,), (0,)), ((), ()))
    elif mode == "nt":
        M = a.shape[-2]
        K = a.shape[-1] * max(a_parts, 1)
        N = b.shape[-2]
        dims = (((1,), (1,)), ((), ()))
    else:
        K = a.shape[-2]
        M = a.shape[-1]
        N = b.shape[-1] * max(b_parts, 1)
        dims = (((0,), (0,)), ((), ()))
    if b_stack and mode == "nn":
        tn = b.shape[-1]
    if b_stack and mode == "nt":
        tk = b.shape[-1]
    if out_stack:
        tn = N // N_DEV
    tm, tn, tk = _pick(M, (tm,)), _pick(N, (tn,)), _pick(K, (tk,))
    if M % tm or N % tn or K % tk:
        raise ValueError(f"{name}: tiles {tm},{tn},{tk} do not divide {M},{N},{K}")
    nm, nn, nk = M // tm, N // tn, K // tk

    def parts_idx(t, ntile, parts):
        per = ntile // parts
        return t // per, t % per

    if mode in ("nn", "nt"):
        if a_parts:
            a_spec = pl.BlockSpec((None, tm, tk), lambda m, n, k: (parts_idx(k, nk, a_parts)[0], m, parts_idx(k, nk, a_parts)[1]))
        else:
            a_spec = pl.BlockSpec((tm, tk), lambda m, n, k: (m, k))
    else:
        a_spec = pl.BlockSpec((tk, tm), lambda m, n, k: (k, m))
    if mode == "nn":
        if b_stack:
            b_spec = pl.BlockSpec((None, tk, tn), lambda m, n, k: (n, k, 0))
        else:
            b_spec = pl.BlockSpec((tk, tn), lambda m, n, k: (k, n))
    elif mode == "nt":
        if b_stack:
            b_spec = pl.BlockSpec((None, tn, tk), lambda m, n, k: (k, n, 0))
        else:
            b_spec = pl.BlockSpec((tn, tk), lambda m, n, k: (n, k))
    else:
        if b_parts:
            b_spec = pl.BlockSpec((None, tk, tn), lambda m, n, k: (parts_idx(n, nn, b_parts)[0], k, parts_idx(n, nn, b_parts)[1]))
        else:
            b_spec = pl.BlockSpec((tk, tn), lambda m, n, k: (k, n))
    if out_stack:
        out_shape = jax.ShapeDtypeStruct((N_DEV, M, tn), out_dtype)
        o_spec = pl.BlockSpec((None, tm, tn), lambda m, n, k: (n, m, 0))
    elif out_parts:
        out_shape = jax.ShapeDtypeStruct((out_parts, M, N // out_parts), out_dtype)
        o_spec = pl.BlockSpec((None, tm, tn), lambda m, n, k: (parts_idx(n, nn, out_parts)[0], m, parts_idx(n, nn, out_parts)[1]))
    else:
        out_shape = jax.ShapeDtypeStruct((M, N), out_dtype)
        o_spec = pl.BlockSpec((tm, tn), lambda m, n, k: (m, n))
    has_resid = resid is not None

    def body(*refs):
        a_ref, b_ref = refs[:2]
        r_ref = refs[2] if has_resid else None
        if nk == 1:
            o_ref = refs[-1]
            res = lax.dot_general(a_ref[...], b_ref[...], dims, preferred_element_type=F32)
            if has_resid:
                res = res + r_ref[...]
            o_ref[...] = res.astype(o_ref.dtype)
            return
        o_ref, acc = refs[-2:]
        k = pl.program_id(2)

        @pl.when(k == 0)
        def _():
            acc[...] = jnp.zeros_like(acc)

        acc[...] += lax.dot_general(a_ref[...], b_ref[...], dims, preferred_element_type=F32)

        @pl.when(k == nk - 1)
        def _():
            res = acc[...]
            if has_resid:
                res = res + r_ref[...]
            o_ref[...] = res.astype(o_ref.dtype)

    in_specs = [a_spec, b_spec]
    args = [a, b]
    if has_resid:
        in_specs.append(pl.BlockSpec((tm, tn), lambda m, n, k: (m, n)))
        args.append(resid)
    for t in after:
        in_specs.append(pl.BlockSpec(memory_space=pl.ANY))
        args.append(t)
    return pl.pallas_call(
        body, name=name, grid=(nm, nn, nk), in_specs=in_specs, out_specs=o_spec, out_shape=out_shape,
        scratch_shapes=[pltpu.VMEM((tm, tn), F32)] if nk > 1 else [],
        compiler_params=_params(("parallel", "parallel", "arbitrary")),
    )(*args)


def _rms_fwd(x, g, name):
    R, D = x.shape
    tr = _pick(R, (256,))

    def body(x_ref, g_ref, h_ref, r_ref):
        xv = x_ref[...]
        r = lax.rsqrt(jnp.mean(xv * xv, axis=-1, keepdims=True) + EPS)
        h_ref[...] = (xv * r * g_ref[...]).astype(BF)
        r_ref[...] = r

    return pl.pallas_call(
        body, name=name, grid=(R // tr,),
        in_specs=[pl.BlockSpec((tr, D), lambda i: (i, 0)), pl.BlockSpec((1, D), lambda i: (0, 0))],
        out_specs=[pl.BlockSpec((tr, D), lambda i: (i, 0)), pl.BlockSpec((tr, 1), lambda i: (i, 0))],
        out_shape=[jax.ShapeDtypeStruct((R, D), BF), jax.ShapeDtypeStruct((R, 1), F32)],
        compiler_params=_params(("parallel",)),
    )(x, g)


def _rms_bwd(x, r, g, dh, dres, name, after=()):
    R, D = x.shape
    tr = _pick(R, (256,))
    has_res = dres is not None

    def body(*refs):
        if has_res:
            x_ref, r_ref, g_ref, dh_ref, dres_ref, dx_ref, dxb_ref, dg_ref = refs
        else:
            x_ref, r_ref, g_ref, dh_ref, dx_ref, dxb_ref, dg_ref = refs
        i = pl.program_id(0)
        xv, rv, dhv = x_ref[...], r_ref[...], dh_ref[...]
        gy = dhv * g_ref[...]
        c = jnp.sum(xv * gy, axis=-1, keepdims=True)
        dx = rv * gy - xv * (rv * rv * rv) * (c * (1.0 / D))
        if has_res:
            dx = dx + dres_ref[...]
        dx_ref[...] = dx
        dxb_ref[...] = dx.astype(BF)
        part = jnp.sum(dhv * xv * rv, axis=0, keepdims=True)

        @pl.when(i == 0)
        def _():
            dg_ref[...] = part

        @pl.when(i > 0)
        def _():
            dg_ref[...] += part

    row = pl.BlockSpec((tr, D), lambda i: (i, 0))
    in_specs = [row, pl.BlockSpec((tr, 1), lambda i: (i, 0)), pl.BlockSpec((1, D), lambda i: (0, 0)), row]
    args = [x, r, g, dh]
    if has_res:
        in_specs.append(row)
        args.append(dres)
    return pl.pallas_call(
        _hide(body, len(args), len(after)), name=name, grid=(R // tr,), in_specs=in_specs + _hidden_specs(after),
        out_specs=[row, row, pl.BlockSpec((1, D), lambda i: (0, 0))],
        out_shape=[jax.ShapeDtypeStruct((R, D), F32), jax.ShapeDtypeStruct((R, D), BF), jax.ShapeDtypeStruct((1, D), F32)],
        compiler_params=_params(("arbitrary",)),
    )(*args, *after)


def _a_chunk(us, vs, gvs, ws, bs):
    r_i = lax.broadcasted_iota(jnp.int32, (CHUNK, CHUNK), 0)
    c_i = lax.broadcasted_iota(jnp.int32, (CHUNK, CHUNK), 1)
    causal = r_i >= c_i
    vg = [jax.nn.gelu(v) for v in vs]
    ss = sum(jnp.sum(v * v, axis=-1, keepdims=True) for v in vg)
    r = lax.rsqrt(ss * (1.0 / A_WIDTH) + EPS)
    ys = []
    for g in range(A_GROUPS):
        vn = vg[g] * r * gvs[g]
        w = jnp.where(causal, ws[g], 0.0)
        s = jnp.dot(w.astype(BF), vn.astype(BF), preferred_element_type=F32) + bs[g]
        ys.append(jax.nn.gelu(us[g]) * s)
    return ys


def _a_split(u_ref, v_ref, g_ref, w_ref, b_ref):
    sl = [slice(g * 128, (g + 1) * 128) for g in range(A_GROUPS)]
    return ([u_ref[:, s] for s in sl], [v_ref[:, s] for s in sl], [g_ref[:, s] for s in sl],
            [w_ref[g] for g in range(A_GROUPS)], [b_ref[:, g:g + 1] for g in range(A_GROUPS)])


def _a_specs(S):
    return [pl.BlockSpec((CHUNK, A_WIDTH), lambda n: (n, 0)), pl.BlockSpec((CHUNK, A_WIDTH), lambda n: (n, 1)),
            pl.BlockSpec((1, A_WIDTH), lambda n: (0, 0)), pl.BlockSpec((A_GROUPS, CHUNK, CHUNK), lambda n: (0, 0, 0)),
            pl.BlockSpec((CHUNK, A_GROUPS), lambda n: (0, 0))]


def _a_fwd(proj, g_v, w_s, b_t):
    S = proj.shape[0]

    def body(u_ref, v_ref, g_ref, w_ref, b_ref, y_ref):
        ys = _a_chunk(*_a_split(u_ref, v_ref, g_ref, w_ref, b_ref))
        for g in range(A_GROUPS):
            y_ref[:, g * 128:(g + 1) * 128] = ys[g].astype(BF)

    return pl.pallas_call(
        body, name="a_fwd", grid=(S // CHUNK,), in_specs=_a_specs(S),
        out_specs=pl.BlockSpec((CHUNK, A_WIDTH), lambda n: (n, 0)),
        out_shape=jax.ShapeDtypeStruct((S, A_WIDTH), BF), compiler_params=_params(("parallel",)),
    )(proj, proj, g_v, w_s, b_t)


def _a_bwd(proj, g_v, w_s, b_t, dy, after=()):
    S = proj.shape[0]

    def body(u_ref, v_ref, g_ref, w_ref, b_ref, dy_ref, duv_ref, dg_ref, dw_ref, db_ref):
        n = pl.program_id(0)
        dys = [dy_ref[:, g * 128:(g + 1) * 128] for g in range(A_GROUPS)]
        _, vjp = jax.vjp(_a_chunk, *_a_split(u_ref, v_ref, g_ref, w_ref, b_ref))
        dus, dvs, dgs, dws, dbs = vjp(dys)

        @pl.when(n == 0)
        def _():
            dg_ref[...] = jnp.zeros_like(dg_ref)
            dw_ref[...] = jnp.zeros_like(dw_ref)
            db_ref[...] = jnp.zeros_like(db_ref)

        for g in range(A_GROUPS):
            duv_ref[:, g * 128:(g + 1) * 128] = dus[g].astype(BF)
            duv_ref[:, A_WIDTH + g * 128:A_WIDTH + (g + 1) * 128] = dvs[g].astype(BF)
            dg_ref[:, g * 128:(g + 1) * 128] += dgs[g]
            dw_ref[g] += dws[g]
            db_ref[:, g:g + 1] += dbs[g]

    return pl.pallas_call(
        _hide(body, 6, len(after)), name="a_bwd", grid=(S // CHUNK,),
        in_specs=_a_specs(S) + [pl.BlockSpec((CHUNK, A_WIDTH), lambda n: (n, 0))] + _hidden_specs(after),
        out_specs=[pl.BlockSpec((CHUNK, 2 * A_WIDTH), lambda n: (n, 0)), pl.BlockSpec((1, A_WIDTH), lambda n: (0, 0)),
                   pl.BlockSpec((A_GROUPS, CHUNK, CHUNK), lambda n: (0, 0, 0)), pl.BlockSpec((CHUNK, A_GROUPS), lambda n: (0, 0))],
        out_shape=[jax.ShapeDtypeStruct((S, 2 * A_WIDTH), BF), jax.ShapeDtypeStruct((1, A_WIDTH), F32),
                   jax.ShapeDtypeStruct((A_GROUPS, CHUNK, CHUNK), F32), jax.ShapeDtypeStruct((CHUNK, A_GROUPS), F32)],
        compiler_params=_params(("arbitrary",)),
    )(proj, proj, g_v, w_s, b_t, dy, *after)


def _half_mask(shape, which):
    lane = lax.broadcasted_iota(jnp.int32, shape, len(shape) - 1)
    return (lane >= 64) == (which == 1)


def _pair_norm_rope(x, g, ct, sa, sb):
    lo = _half_mask(x.shape, 0)
    x2 = x * x
    ss_lo = jnp.sum(jnp.where(lo, x2, 0.0), axis=-1, keepdims=True)
    ss_hi = jnp.sum(jnp.where(lo, 0.0, x2), axis=-1, keepdims=True)
    r = jnp.where(lo, lax.rsqrt(ss_lo * (1.0 / B_HEAD_DIM) + EPS), lax.rsqrt(ss_hi * (1.0 / B_HEAD_DIM) + EPS))
    xr = x * r
    xn = xr * g
    out = xn * ct + pltpu.roll(xn, 120, 1) * sa + pltpu.roll(xn, 8, 1) * sb
    return out, xr, r


def _pair_norm_rope_bwd(x, g, ct, sa, sb, dout):
    lo = _half_mask(x.shape, 0)
    _, xr, r = _pair_norm_rope(x, g, ct, sa, sb)
    dxn = dout * ct + pltpu.roll(dout * sa, 8, 1) + pltpu.roll(dout * sb, 120, 1)
    gy = dxn * g
    t = xr * gy
    c_lo = jnp.sum(jnp.where(lo, t, 0.0), axis=-1, keepdims=True)
    c_hi = jnp.sum(jnp.where(lo, 0.0, t), axis=-1, keepdims=True)
    c = jnp.where(lo, c_lo, c_hi)
    dx = r * (gy - xr * c * (1.0 / B_HEAD_DIM))
    dg = jnp.sum(dxn * xr, axis=0, keepdims=True)
    return dx, dg


def _b_pre(proj, gq2, gk2, ct, sa, sb):
    S = proj.shape[0]
    tr = _pick(S, (256,))
    n_pair = B_WIDTH // 128

    def body(q_ref, k_ref, gq_ref, gk_ref, ct_ref, sa_ref, sb_ref, qn_ref, kn_ref):
        ct_v, sa_v, sb_v = ct_ref[...], sa_ref[...], sb_ref[...]
        for p in range(n_pair):
            o, _, _ = _pair_norm_rope(q_ref[:, p * 128:(p + 1) * 128], gq_ref[...], ct_v, sa_v, sb_v)
            qn_ref[:, p * 128:(p + 1) * 128] = o.astype(BF)
        o, _, _ = _pair_norm_rope(k_ref[...], gk_ref[...], ct_v, sa_v, sb_v)
        kn_ref[...] = o.astype(BF)

    tab = pl.BlockSpec((tr, 128), lambda i: (i, 0))
    gsp = pl.BlockSpec((1, 128), lambda i: (0, 0))
    return pl.pallas_call(
        body, name="b_pre", grid=(S // tr,),
        in_specs=[pl.BlockSpec((tr, B_WIDTH), lambda i: (i, 1)), pl.BlockSpec((tr, 128), lambda i: (i, 2 * B_WIDTH // 128)),
                  gsp, gsp, tab, tab, tab],
        out_specs=[pl.BlockSpec((tr, B_WIDTH), lambda i: (i, 0)), tab],
        out_shape=[jax.ShapeDtypeStruct((S, B_WIDTH), BF), jax.ShapeDtypeStruct((S, 128), BF)],
        compiler_params=_params(("parallel",)),
    )(proj, proj, gq2, gk2, ct, sa, sb)


def _b_pre_bwd(proj, gq2, gk2, ct, sa, sb, dqn, dkn, dv):
    S = proj.shape[0]
    tr = _pick(S, (256,))
    n_pair = B_WIDTH // 128

    def body(q_ref, k_ref, gq_ref, gk_ref, ct_ref, sa_ref, sb_ref, dqn_ref, dkn_ref, dv_ref, dqkv_ref, dgq_ref, dgk_ref):
        i = pl.program_id(0)
        ct_v, sa_v, sb_v = ct_ref[...], sa_ref[...], sb_ref[...]
        dgq = jnp.zeros((1, 128), F32)
        for p in range(n_pair):
            sl = slice(p * 128, (p + 1) * 128)
            dx, dg = _pair_norm_rope_bwd(q_ref[:, sl], gq_ref[...], ct_v, sa_v, sb_v, dqn_ref[:, sl])
            dqkv_ref[:, sl] = dx.astype(BF)
            dgq = dgq + dg
        dx, dgk = _pair_norm_rope_bwd(k_ref[...], gk_ref[...], ct_v, sa_v, sb_v, dkn_ref[...])
        dqkv_ref[:, B_WIDTH:B_WIDTH + 128] = dx.astype(BF)
        dqkv_ref[:, B_WIDTH + 128:B_WIDTH + 256] = dv_ref[...].astype(BF)

        @pl.when(i == 0)
        def _():
            dgq_ref[...] = dgq
            dgk_ref[...] = dgk

        @pl.when(i > 0)
        def _():
            dgq_ref[...] += dgq
            dgk_ref[...] += dgk

    tab = pl.BlockSpec((tr, 128), lambda i: (i, 0))
    gsp = pl.BlockSpec((1, 128), lambda i: (0, 0))
    return pl.pallas_call(
        body, name="b_pre_bwd", grid=(S // tr,),
        in_specs=[pl.BlockSpec((tr, B_WIDTH), lambda i: (i, 1)), pl.BlockSpec((tr, 128), lambda i: (i, 2 * B_WIDTH // 128)),
                  gsp, gsp, tab, tab, tab, pl.BlockSpec((tr, B_WIDTH), lambda i: (i, 0)), tab, tab],
        out_specs=[pl.BlockSpec((tr, B_WIDTH + 256), lambda i: (i, 0)), gsp, gsp],
        out_shape=[jax.ShapeDtypeStruct((S, B_WIDTH + 256), BF), jax.ShapeDtypeStruct((1, 128), F32), jax.ShapeDtypeStruct((1, 128), F32)],
        compiler_params=_params(("arbitrary",)),
    )(proj, proj, gq2, gk2, ct, sa, sb, dqn, dkn, dv)


def _b_dup(x2, g):
    d = jnp.where(_half_mask(x2.shape, g), x2, 0.0)
    return (d + pltpu.roll(d, 64, 1)).astype(BF)


def _b_valid(n):
    row = lax.broadcasted_iota(jnp.int32, (CHUNK, 2 * CHUNK), 0)
    col = lax.broadcasted_iota(jnp.int32, (CHUNK, 2 * CHUNK), 1)
    rel = row + CHUNK - col
    return (rel >= 0) & (rel < CHUNK) & ((col >= CHUNK) | (n > 0))


def _b_probs(qm, kd, valid, sink):
    s = lax.dot_general(qm, kd, (((1,), (1,)), ((), ())), preferred_element_type=F32) * (B_HEAD_DIM ** -0.5)
    s = jnp.where(valid, s, NEG)
    m = jnp.maximum(jnp.max(s, axis=-1, keepdims=True), sink)
    e = jnp.exp(s - m)
    es = jnp.exp(sink - m)
    inv = 1.0 / (jnp.sum(e, axis=-1, keepdims=True) + es)
    return e * inv, es * inv


def _b_kv_specs(S):
    prev = lambda n: (jnp.maximum(n - 1, 0), 0)
    cur = lambda n: (n, 0)
    v_col = (2 * B_WIDTH + B_KV_WIDTH) // 128
    return [pl.BlockSpec((CHUNK, 128), prev), pl.BlockSpec((CHUNK, 128), cur),
            pl.BlockSpec((CHUNK, 128), lambda n: (jnp.maximum(n - 1, 0), v_col)), pl.BlockSpec((CHUNK, 128), lambda n: (n, v_col))]


def _b_attn_fwd(qn, kn, proj, sinks):
    S = qn.shape[0]

    def body(s_ref, q_ref, kp_ref, kc_ref, vp_ref, vc_ref, y_ref):
        n = pl.program_id(0)
        valid = _b_valid(n)
        k2 = jnp.concatenate([kp_ref[...], kc_ref[...]], axis=0).astype(F32)
        v2 = jnp.concatenate([vp_ref[...], vc_ref[...]], axis=0)
        for g in range(B_KV_HEADS):
            kd, vd = _b_dup(k2, g), _b_dup(v2, g)
            for pp in range(B_HEADS // B_KV_HEADS // 2):
                p = g * (B_HEADS // B_KV_HEADS // 2) + pp
                q_pair = q_ref[:, p * 128:(p + 1) * 128]
                o_pair = jnp.zeros((CHUNK, 128), F32)
                for hf in range(2):
                    hm = _half_mask((CHUNK, 128), hf)
                    qm = jnp.where(hm, q_pair, jnp.zeros_like(q_pair))
                    pr, _ = _b_probs(qm, kd, valid, s_ref[0, 2 * p + hf])
                    o = jnp.dot(pr.astype(BF), vd, preferred_element_type=F32)
                    o_pair = o_pair + jnp.where(hm, o, 0.0)
                y_ref[:, p * 128:(p + 1) * 128] = o_pair.astype(BF)

    return pl.pallas_call(
        body, name="b_attn_fwd", grid=(S // CHUNK,),
        in_specs=[pl.BlockSpec(memory_space=pltpu.SMEM), pl.BlockSpec((CHUNK, B_WIDTH), lambda n: (n, 0))] + _b_kv_specs(S),
        out_specs=pl.BlockSpec((CHUNK, B_WIDTH), lambda n: (n, 0)),
        out_shape=jax.ShapeDtypeStruct((S, B_WIDTH), BF), compiler_params=_params(("arbitrary",)),
    )(sinks, qn, kn, kn, proj, proj)


def _b_attn_bwd(qn, kn, proj, sinks, dy, after=()):
    S = qn.shape[0]

    def body(s_ref, q_ref, kp_ref, kc_ref, vp_ref, vc_ref, dy_ref, dq_ref, dk_ref, dv_ref, ds_ref):
        n = pl.program_id(0)

        @pl.when(n == 0)
        def _():
            dk_ref[...] = jnp.zeros_like(dk_ref)
            dv_ref[...] = jnp.zeros_like(dv_ref)
            ds_ref[...] = jnp.zeros_like(ds_ref)

        valid = _b_valid(n)
        k2 = jnp.concatenate([kp_ref[...], kc_ref[...]], axis=0).astype(F32)
        v2 = jnp.concatenate([vp_ref[...], vc_ref[...]], axis=0)
        lane = lax.broadcasted_iota(jnp.int32, (CHUNK, 128), 1)
        dk2 = jnp.zeros((2 * CHUNK, 128), F32)
        dv2 = jnp.zeros((2 * CHUNK, 128), F32)
        dsink = jnp.zeros((CHUNK, 128), F32)
        scale = B_HEAD_DIM ** -0.5
        for g in range(B_KV_HEADS):
            kd, vd = _b_dup(k2, g), _b_dup(v2, g)
            dk_acc = jnp.zeros((2 * CHUNK, 128), F32)
            dv_acc = jnp.zeros((2 * CHUNK, 128), F32)
            for pp in range(B_HEADS // B_KV_HEADS // 2):
                p = g * (B_HEADS // B_KV_HEADS // 2) + pp
                q_pair = q_ref[:, p * 128:(p + 1) * 128]
                do_pair = dy_ref[:, p * 128:(p + 1) * 128]
                dq_pair = jnp.zeros((CHUNK, 128), F32)
                for hf in range(2):
                    h = 2 * p + hf
                    hm = _half_mask((CHUNK, 128), hf)
                    qm = jnp.where(hm, q_pair, jnp.zeros_like(q_pair))
                    do = jnp.where(hm, do_pair, 0.0)
                    do_b = do.astype(BF)
                    pr, ps = _b_probs(qm, kd, valid, s_ref[0, h])
                    pr_b = pr.astype(BF)
                    o = jnp.dot(pr_b, vd, preferred_element_type=F32)
                    delta = jnp.sum(do * o, axis=-1, keepdims=True)
                    dp = lax.dot_general(do_b, vd, (((1,), (1,)), ((), ())), preferred_element_type=F32)
                    dsc = (pr * (dp - delta) * scale).astype(BF)
                    dsink = dsink + jnp.where(lane == h, -ps * delta, 0.0)
                    dq = jnp.dot(dsc, kd, preferred_element_type=F32)
                    dq_pair = dq_pair + jnp.where(hm, dq, 0.0)
                    dk_acc = dk_acc + lax.dot_general(dsc, qm, (((0,), (0,)), ((), ())), preferred_element_type=F32)
                    dv_acc = dv_acc + lax.dot_general(pr_b, do_b, (((0,), (0,)), ((), ())), preferred_element_type=F32)
                dq_ref[:, p * 128:(p + 1) * 128] = dq_pair
            gm = _half_mask((2 * CHUNK, 128), g)
            dk2 = dk2 + jnp.where(gm, dk_acc + pltpu.roll(dk_acc, 64, 1), 0.0)
            dv2 = dv2 + jnp.where(gm, dv_acc + pltpu.roll(dv_acc, 64, 1), 0.0)
        ds_ref[...] += dsink
        cur = pl.ds(pl.multiple_of(n * CHUNK, CHUNK), CHUNK)
        dk_ref[cur, :] += dk2[CHUNK:]
        dv_ref[cur, :] += dv2[CHUNK:]

        @pl.when(n > 0)
        def _():
            prv = pl.ds(pl.multiple_of((n - 1) * CHUNK, CHUNK), CHUNK)
            dk_ref[prv, :] += dk2[:CHUNK]
            dv_ref[prv, :] += dv2[:CHUNK]

    full = pl.BlockSpec((S, 128), lambda n: (0, 0))
    return pl.pallas_call(
        _hide(body, 7, len(after)), name="b_attn_bwd", grid=(S // CHUNK,),
        in_specs=[pl.BlockSpec(memory_space=pltpu.SMEM), pl.BlockSpec((CHUNK, B_WIDTH), lambda n: (n, 0))] + _b_kv_specs(S)
        + [pl.BlockSpec((CHUNK, B_WIDTH), lambda n: (n, 0))] + _hidden_specs(after),
        out_specs=[pl.BlockSpec((CHUNK, B_WIDTH), lambda n: (n, 0)), full, full, pl.BlockSpec((CHUNK, 128), lambda n: (0, 0))],
        out_shape=[jax.ShapeDtypeStruct((S, B_WIDTH), F32), jax.ShapeDtypeStruct((S, 128), F32), jax.ShapeDtypeStruct((S, 128), F32),
                   jax.ShapeDtypeStruct((CHUNK, 128), F32)],
        compiler_params=_params(("arbitrary",)),
    )(sinks, qn, kn, kn, proj, proj, dy, *after)


def _c_block(q, k, v, gq, gk):
    qn = q * lax.rsqrt(jnp.mean(q * q, axis=-1, keepdims=True) + EPS) * gq
    kn = k * lax.rsqrt(jnp.mean(k * k, axis=-1, keepdims=True) + EPS) * gk
    s = lax.dot_general(qn.astype(BF), kn.astype(BF), (((1,), (1,)), ((), ())), preferred_element_type=F32) * (C_HEAD_DIM ** -0.5)
    p = jax.nn.softmax(s, axis=-1)
    return jnp.dot(p.astype(BF), v.astype(BF), preferred_element_type=F32)


def _c_specs(S, M, tq):
    q_col = (2 * A_WIDTH + B_WIDTH + 2 * B_KV_WIDTH) // 128
    return [pl.BlockSpec((tq, 128), lambda h, i: (i, q_col + h)), pl.BlockSpec((M, 128), lambda h, i: (0, h)),
            pl.BlockSpec((M, 128), lambda h, i: (0, C_HEADS + h)), pl.BlockSpec((1, 128), lambda h, i: (0, 0)),
            pl.BlockSpec((1, 128), lambda h, i: (0, 0))]


def _c_fwd(proj, kv, gq, gk):
    S, M = proj.shape[0], kv.shape[0]
    tq = _pick(S, (512,))

    def body(q_ref, k_ref, v_ref, gq_ref, gk_ref, y_ref):
        y_ref[...] = _c_block(q_ref[...], k_ref[...], v_ref[...], gq_ref[...], gk_ref[...]).astype(BF)

    return pl.pallas_call(
        body, name="c_fwd", grid=(C_HEADS, S // tq), in_specs=_c_specs(S, M, tq),
        out_specs=pl.BlockSpec((tq, 128), lambda h, i: (i, h)),
        out_shape=jax.ShapeDtypeStruct((S, C_WIDTH), BF), compiler_params=_params(("parallel", "parallel")),
    )(proj, kv, kv, gq, gk)


def _c_bwd(proj, kv, gq, gk, dy):
    S, M = proj.shape[0], kv.shape[0]
    tq = _pick(S, (512,))

    def body(q_ref, k_ref, v_ref, gq_ref, gk_ref, dy_ref, dq_ref, dk_ref, dv_ref, dgq_ref, dgk_ref):
        i = pl.program_id(1)
        _, vjp = jax.vjp(_c_block, q_ref[...], k_ref[...], v_ref[...], gq_ref[...], gk_ref[...])
        dq, dk, dv, dgq, dgk = vjp(dy_ref[...])
        dq_ref[...] = dq.astype(BF)

        @pl.when(i == 0)
        def _():
            dk_ref[...] = dk
            dv_ref[...] = dv
            dgq_ref[...] = dgq
            dgk_ref[...] = dgk

        @pl.when(i > 0)
        def _():
            dk_ref[...] += dk
            dv_ref[...] += dv
            dgq_ref[...] += dgq
            dgk_ref[...] += dgk

    return pl.pallas_call(
        body, name="c_bwd", grid=(C_HEADS, S // tq),
        in_specs=_c_specs(S, M, tq) + [pl.BlockSpec((tq, 128), lambda h, i: (i, h))],
        out_specs=[pl.BlockSpec((tq, 128), lambda h, i: (i, h)), pl.BlockSpec((M, 128), lambda h, i: (0, h)),
                   pl.BlockSpec((M, 128), lambda h, i: (0, h)), pl.BlockSpec((None, 1, 128), lambda h, i: (h, 0, 0)),
                   pl.BlockSpec((None, 1, 128), lambda h, i: (h, 0, 0))],
        out_shape=[jax.ShapeDtypeStruct((S, C_WIDTH), BF), jax.ShapeDtypeStruct((M, C_WIDTH), F32), jax.ShapeDtypeStruct((M, C_WIDTH), F32),
                   jax.ShapeDtypeStruct((C_HEADS, 1, 128), F32), jax.ShapeDtypeStruct((C_HEADS, 1, 128), F32)],
        compiler_params=_params(("parallel", "arbitrary")),
    )(proj, kv, kv, gq, gk, dy)


def _merge_specs(S, D, tr, tc):
    off = GATE_OFF // tc
    nd = D // tc
    gates = [pl.BlockSpec((tr, tc), functools.partial(lambda b, i, j: (i, off + b * nd + j), b)) for b in range(3)]
    zs = [pl.BlockSpec((tr, tc), lambda i, j: (i, j)) for _ in range(3)]
    return gates + zs


def _merge_fwd(proj, za, zb, zc):
    S, D = za.shape
    tr, tc = _pick(S, (512,)), _pick(D, (256,))

    def body(ga_ref, gb_ref, gc_ref, za_ref, zb_ref, zc_ref, m_ref):
        acc = jax.nn.sigmoid(ga_ref[...]) * za_ref[...].astype(F32)
        acc = acc + jax.nn.sigmoid(gb_ref[...]) * zb_ref[...].astype(F32)
        acc = acc + jax.nn.sigmoid(gc_ref[...]) * zc_ref[...].astype(F32)
        m_ref[...] = acc.astype(BF)

    return pl.pallas_call(
        body, name="merge_fwd", grid=(S // tr, D // tc), in_specs=_merge_specs(S, D, tr, tc),
        out_specs=pl.BlockSpec((tr, tc), lambda i, j: (i, j)), out_shape=jax.ShapeDtypeStruct((S, D), BF),
        compiler_params=_params(("parallel", "parallel")),
    )(proj, proj, proj, za, zb, zc)


def _merge_bwd(proj, za, zb, zc, dm, after=()):
    S, D = za.shape
    tr, tc = _pick(S, (512,)), _pick(D, (256,))
    nd = D // tc

    def body(ga_ref, gb_ref, gc_ref, za_ref, zb_ref, zc_ref, dm_ref, dza_ref, dzb_ref, dzc_ref, dga_ref, dgb_ref, dgc_ref):
        dmv = dm_ref[...]
        for g_ref, z_ref, dz_ref, dg_ref in ((ga_ref, za_ref, dza_ref, dga_ref), (gb_ref, zb_ref, dzb_ref, dgb_ref),
                                             (gc_ref, zc_ref, dzc_ref, dgc_ref)):
            sg = jax.nn.sigmoid(g_ref[...])
            dz_ref[...] = (sg * dmv).astype(BF)
            dg_ref[...] = (dmv * z_ref[...].astype(F32) * sg * (1.0 - sg)).astype(BF)

    tile = pl.BlockSpec((tr, tc), lambda i, j: (i, j))
    return pl.pallas_call(
        _hide(body, 7, len(after)), name="merge_bwd", grid=(S // tr, D // tc),
        in_specs=_merge_specs(S, D, tr, tc) + [tile] + _hidden_specs(after),
        out_specs=[tile, tile, tile, tile, tile, tile],
        out_shape=[jax.ShapeDtypeStruct((S, D), BF)] * 6,
        compiler_params=_params(("parallel", "parallel")),
    )(proj, proj, proj, za, zb, zc, dm, *after)


def _shift_down(u, k):
    t = lax.broadcasted_iota(jnp.int32, u.shape, 0)
    return jnp.where(t >= k, pltpu.roll(u, k, 0), 0.0)


def _shift_up(u, k):
    n = u.shape[0]
    t = lax.broadcasted_iota(jnp.int32, u.shape, 0)
    return jnp.where(t < n - k, pltpu.roll(u, n - k, 0), 0.0)


def _conv3(u, w, b):
    return u * w[2:3] + _shift_down(u, 1) * w[1:2] + _shift_down(u, 2) * w[0:1] + b


def _ffn_specs(S, F, tc, c):
    per = c // tc

    def w_spec(half):
        return pl.BlockSpec((None, 3, tc), lambda j: (half * (N_DEV // 2) + j // per, 0, j % per))

    return [pl.BlockSpec((2, S, tc), lambda j: (0, 0, j)), w_spec(0), w_spec(1), pl.BlockSpec((2, 1, tc), lambda j: (0, 0, j))]


def _ffn_tile(F, c):
    tc = 128
    if c % tc or F % tc:
        raise ValueError(f"ffn tile {tc} does not divide {c}, {F}")
    return tc


def _ffn_act_fwd(up3, cws, cb3):
    _, S, F = up3.shape
    c = cws.shape[2]
    tc = _ffn_tile(F, c)

    def body(u_ref, wa_ref, wb_ref, b_ref, o_ref):
        ca = _conv3(u_ref[0], wa_ref[...], b_ref[0])
        cb = _conv3(u_ref[1], wb_ref[...], b_ref[1])
        o_ref[...] = (ca * jax.nn.sigmoid(ca) * cb).astype(BF)

    return pl.pallas_call(
        body, name="ffn_act_fwd", grid=(F // tc,), in_specs=_ffn_specs(S, F, tc, c),
        out_specs=pl.BlockSpec((S, tc), lambda j: (0, j)), out_shape=jax.ShapeDtypeStruct((S, F), BF),
        compiler_params=_params(("parallel",)),
    )(up3, cws, cws, cb3)


def _ffn_act_bwd(up3, cws, cb3, dact, after=()):
    _, S, F = up3.shape
    c = cws.shape[2]
    tc = _ffn_tile(F, c)

    def body(u_ref, wa_ref, wb_ref, b_ref, da_ref, du_ref, dw_ref, db_ref):
        w_refs = (wa_ref, wb_ref)
        ca = _conv3(u_ref[0], wa_ref[...], b_ref[0])
        cb = _conv3(u_ref[1], wb_ref[...], b_ref[1])
        sg = jax.nn.sigmoid(ca)
        dav = da_ref[...]
        dcs = (dav * cb * sg * (1.0 + ca * (1.0 - sg)), dav * ca * sg)
        for part in range(2):
            dc, w, u = dcs[part], w_refs[part][...], u_ref[part]
            du_ref[part] = (dc * w[2:3] + _shift_up(dc, 1) * w[1:2] + _shift_up(dc, 2) * w[0:1]).astype(BF)
            dw_ref[part, 2:3, :] = jnp.sum(dc * u, axis=0, keepdims=True)
            dw_ref[part, 1:2, :] = jnp.sum(dc * _shift_down(u, 1), axis=0, keepdims=True)
            dw_ref[part, 0:1, :] = jnp.sum(dc * _shift_down(u, 2), axis=0, keepdims=True)
            db_ref[part] = jnp.sum(dc, axis=0, keepdims=True)

    return pl.pallas_call(
        _hide(body, 5, len(after)), name="ffn_act_bwd", grid=(F // tc,),
        in_specs=_ffn_specs(S, F, tc, c) + [pl.BlockSpec((S, tc), lambda j: (0, j))] + _hidden_specs(after),
        out_specs=[pl.BlockSpec((2, S, tc), lambda j: (0, 0, j)), pl.BlockSpec((2, 3, tc), lambda j: (0, 0, j)),
                   pl.BlockSpec((2, 1, tc), lambda j: (0, 0, j))],
        out_shape=[jax.ShapeDtypeStruct((2, S, F), BF), jax.ShapeDtypeStruct((2, 3, F), F32), jax.ShapeDtypeStruct((2, 1, F), F32)],
        compiler_params=_params(("parallel",)),
    )(up3, cws, cws, cb3, dact, *after)


def _loss(y, target):
    S, D = y.shape
    tr = _pick(S, (256,))

    def body(y_ref, t_ref, dy_ref, dyb_ref, l_ref):
        i = pl.program_id(0)
        e = y_ref[...] - t_ref[...]
        dy = e * (1.0 / D)
        dy_ref[...] = dy
        dyb_ref[...] = dy.astype(BF)
        part = jnp.sum(jnp.sum(e * e, axis=-1, keepdims=True), axis=0, keepdims=True) * (0.5 / D)

        @pl.when(i == 0)
        def _():
            l_ref[...] = jnp.zeros_like(l_ref)

        l_ref[...] += part

    row = pl.BlockSpec((tr, D), lambda i: (i, 0))
    return pl.pallas_call(
        body, name="loss", grid=(S // tr,), in_specs=[row, row],
        out_specs=[row, row, pl.BlockSpec((8, 128), lambda i: (0, 0))],
        out_shape=[jax.ShapeDtypeStruct((S, D), F32), jax.ShapeDtypeStruct((S, D), BF), jax.ShapeDtypeStruct((8, 128), F32)],
        compiler_params=_params(("arbitrary",)),
    )(y, target)


ANY = pl.BlockSpec(memory_space=pl.ANY)


def _allgather(shards, name):
    n = len(shards)

    def body(*refs):
        ins, outs = refs[:n], refs[n:2 * n]
        send_sems, recv_sems, local_sems = refs[2 * n:]
        x, y, c = lax.axis_index("x"), lax.axis_index("y"), lax.axis_index("c")
        me, sibling = (x, y, c), (x, y, 1 - c)
        chips = [(1 - x, y), (x, 1 - y), (1 - x, 1 - y)]

        def blk(w, px, py, pc):
            return outs[w].at[4 * px + 2 * py + pc]

        def copy(w, k, block, to, src=None):
            return pltpu.make_async_remote_copy(
                src_ref=blk(w, *block) if src is None else src, dst_ref=blk(w, *block),
                send_sem=send_sems.at[w, k], recv_sem=recv_sems.at[w, k], device_id=to, device_id_type=MESH)

        started = []
        mine = []
        for w in range(n):
            mine.append(pltpu.make_async_copy(ins[w], blk(w, *me), local_sems.at[w]))
            mine[-1].start()
            first = [copy(w, 0, me, sibling, src=ins[w])]
            first += [copy(w, 1 + j, me, (*chip, c), src=ins[w]) for j, chip in enumerate(chips)]
            for cp in first:
                cp.start()
            started += first
        for w in range(n):
            for j, chip in enumerate(chips):
                copy(w, 1 + j, (*chip, c), me).wait_recv()
                fwd = copy(w, 4 + j, (*chip, c), sibling)
                fwd.start()
                started.append(fwd)
        for w in range(n):
            copy(w, 0, sibling, me).wait_recv()
            for j, chip in enumerate(chips):
                copy(w, 4 + j, (*chip, 1 - c), me).wait_recv()
        for cp in started:
            cp.wait_send()
        for cp in mine:
            cp.wait()

    outs = pl.pallas_call(
        body, name=name, in_specs=[ANY] * n, out_specs=[ANY] * n,
        out_shape=[jax.ShapeDtypeStruct((N_DEV,) + s.shape, s.dtype) for s in shards],
        scratch_shapes=[pltpu.SemaphoreType.DMA((n, 7)), pltpu.SemaphoreType.DMA((n, 7)), pltpu.SemaphoreType.DMA((n,))],
    )(*shards)
    return list(outs)


def _allgather_seq(shards, name, collective_id, after=()):
    n = len(shards)
    n_after = len(after)

    def body(*refs):
        ins, outs = refs[:n], refs[n + n_after:2 * n + n_after]
        send_sems, recv_sems, local_sems = refs[2 * n + n_after:]
        x, y, c = lax.axis_index("x"), lax.axis_index("y"), lax.axis_index("c")
        me, sibling = (x, y, c), (x, y, 1 - c)
        chips = [(1 - x, y), (x, 1 - y), (1 - x, 1 - y)]
        barrier = pltpu.get_barrier_semaphore()
        for peer in [sibling] + [(*chip, c) for chip in chips]:
            pl.semaphore_signal(barrier, inc=1, device_id=peer, device_id_type=MESH)
        pl.semaphore_wait(barrier, 4)

        def blk(w, px, py, pc):
            return outs[w].at[4 * px + 2 * py + pc]

        def copy(w, k, block, to, src=None):
            return pltpu.make_async_remote_copy(
                src_ref=blk(w, *block) if src is None else src, dst_ref=blk(w, *block),
                send_sem=send_sems.at[7 * w + k], recv_sem=recv_sems.at[7 * w + k], device_id=to, device_id_type=MESH)

        started = []
        mine = []
        for w in range(n):
            mine.append(pltpu.make_async_copy(ins[w], blk(w, *me), local_sems.at[w]))
            mine[-1].start()
            first = [copy(w, 0, me, sibling, src=ins[w])]
            first += [copy(w, 1 + j, me, (*chip, c), src=ins[w]) for j, chip in enumerate(chips)]
            for cp in first:
                cp.start()
            started += first
        for w in range(n):
            for j, chip in enumerate(chips):
                copy(w, 1 + j, (*chip, c), me).wait_recv()
                fwd = copy(w, 4 + j, (*chip, c), sibling)
                fwd.start()
                started.append(fwd)
        for w in range(n):
            copy(w, 0, sibling, me).wait_recv()
            for j, chip in enumerate(chips):
                copy(w, 4 + j, (*chip, 1 - c), me).wait_recv()
        for cp in started:
            cp.wait_send()
        for cp in mine:
            cp.wait()

    outs = pl.kernel(
        body, name=name, out_type=[jax.ShapeDtypeStruct((N_DEV,) + s.shape, s.dtype) for s in shards],
        mesh=plsc.ScalarSubcoreMesh(axis_name="seq", num_cores=1),
        scratch_types=[pltpu.SemaphoreType.DMA((7 * n,)), pltpu.SemaphoreType.DMA((7 * n,)), pltpu.SemaphoreType.DMA((n,))],
        compiler_params=pltpu.CompilerParams(collective_id=collective_id),
    )(*shards, *after)
    return list(outs)


def _sibling_exchange(grads, name):
    n = len(grads)

    def body(*refs):
        ins, outs = refs[:n], refs[n:2 * n]
        send_sems, recv_sems = refs[2 * n:]
        x, y, c = lax.axis_index("x"), lax.axis_index("y"), lax.axis_index("c")
        copies = [pltpu.make_async_remote_copy(
            src_ref=ins[w].at[:, 1 - c], dst_ref=outs[w], send_sem=send_sems.at[w], recv_sem=recv_sems.at[w],
            device_id=(x, y, 1 - c), device_id_type=MESH) for w in range(n)]
        for cp in copies:
            cp.start()
        for cp in copies:
            cp.wait()

    outs = pl.pallas_call(
        body, name=name, in_specs=[ANY] * n, out_specs=[ANY] * n,
        out_shape=[jax.ShapeDtypeStruct((g.shape[0],) + g.shape[2:], g.dtype) for g in grads],
        scratch_shapes=[pltpu.SemaphoreType.DMA((n,)), pltpu.SemaphoreType.DMA((n,))],
    )(*grads)
    return list(outs)


def _chip_exchange(sums, name, collective_id):
    n = len(sums)

    def body(*refs):
        ins, outs = refs[:n], refs[n:2 * n]
        send_sems, recv_sems = refs[2 * n:]
        x, y, c = lax.axis_index("x"), lax.axis_index("y"), lax.axis_index("c")
        chips = [(1 - x, y), (x, 1 - y), (1 - x, 1 - y)]
        barrier = pltpu.get_barrier_semaphore()
        for px, py in chips:
            pl.semaphore_signal(barrier, inc=1, device_id=(px, py, c), device_id_type=MESH)
        pl.semaphore_wait(barrier, 3)
        copies = []
        for w in range(n):
            for k, (px, py) in enumerate(chips):
                copies.append(pltpu.make_async_remote_copy(
                    src_ref=ins[w].at[2 * px + py], dst_ref=outs[w].at[k], send_sem=send_sems.at[3 * w + k],
                    recv_sem=recv_sems.at[3 * w + k], device_id=(px, py, c), device_id_type=MESH))
        for cp in copies:
            cp.start()
        for cp in copies:
            cp.wait()

    outs = pl.kernel(
        body, name=name, out_type=[jax.ShapeDtypeStruct((3,) + s.shape[1:], s.dtype) for s in sums],
        mesh=plsc.ScalarSubcoreMesh(axis_name="seq", num_cores=1),
        scratch_types=[pltpu.SemaphoreType.DMA((3 * n,)), pltpu.SemaphoreType.DMA((3 * n,))],
        compiler_params=pltpu.CompilerParams(collective_id=collective_id),
    )(*sums)
    return list(outs)


def _row_tile(r, c, elems=256 * 1024):
    want = max(8, elems // c)
    for t in range(min(want, r) // 8 * 8, 0, -8):
        if r % t == 0:
            return t
    return r


def _pair_add(g4, recv, core, name, after=()):
    _, _, r, c = g4.shape
    tr = _row_tile(r, c)

    def body(core_ref, a_ref, b_ref, o_ref):
        o_ref[...] = (a_ref[...].astype(F32) + b_ref[...].astype(F32)).astype(BF)

    return pl.pallas_call(
        _hide(body, 3, len(after)), name=name,
        grid_spec=pltpu.PrefetchScalarGridSpec(
            num_scalar_prefetch=1, grid=(4, r // tr),
            in_specs=[pl.BlockSpec((None, None, tr, c), lambda p, i, s: (p, s[0], i, 0)),
                      pl.BlockSpec((None, tr, c), lambda p, i, s: (p, i, 0))] + _hidden_specs(after),
            out_specs=pl.BlockSpec((None, tr, c), lambda p, i, s: (p, i, 0))),
        out_shape=jax.ShapeDtypeStruct((4, r, c), BF), compiler_params=_params(("parallel", "parallel")),
    )(core, g4, recv, *after)


def _adam_math(w, g, m, v):
    m = ADAM_B1 * m + (1.0 - ADAM_B1) * g
    v = ADAM_B2 * v + (1.0 - ADAM_B2) * (g * g)
    m_hat = m / (1.0 - ADAM_B1 ** ADAM_STEP)
    v_hat = v / (1.0 - ADAM_B2 ** ADAM_STEP)
    delta = -ADAM_LR * (m_hat / (jnp.sqrt(v_hat) + ADAM_EPS) + ADAM_WD * w)
    return delta, m, v


def _adamw_big(sums, recv, chip, w, m, v, name, after=()):
    r, c = w.shape
    tr = _row_tile(r, c, 128 * 1024)

    def body(chip_ref, s_ref, r_ref, w_ref, m_ref, v_ref, g_out, d_out, m_out, v_out):
        g = s_ref[...].astype(F32) + r_ref[0].astype(F32)
        g = g + r_ref[1].astype(F32)
        g = g + r_ref[2].astype(F32)
        delta, mn, vn = _adam_math(w_ref[...], g, m_ref[...], v_ref[...])
        g_out[...] = g
        d_out[...] = delta
        m_out[...] = mn
        v_out[...] = vn

    row = pl.BlockSpec((tr, c), lambda i, s: (i, 0))
    return pl.pallas_call(
        _hide(body, 6, len(after)), name=name,
        grid_spec=pltpu.PrefetchScalarGridSpec(
            num_scalar_prefetch=1, grid=(r // tr,),
            in_specs=[pl.BlockSpec((None, tr, c), lambda i, s: (s[0], i, 0)), pl.BlockSpec((3, tr, c), lambda i, s: (0, i, 0)),
                      row, row, row] + _hidden_specs(after),
            out_specs=[row, row, row, row]),
        out_shape=[jax.ShapeDtypeStruct((r, c), F32)] * 4, compiler_params=_params(("parallel",)),
    )(chip, sums, recv, w, m, v, *after)


def _adamw_small(parts, w, m, v, name):
    R = w.shape[0]
    tr = R

    def body(p_ref, w_ref, m_ref, v_ref, g_out, d_out, m_out, v_out):
        g = p_ref[0]
        for d in range(1, N_DEV):
            g = g + p_ref[d]
        delta, mn, vn = _adam_math(w_ref[...], g, m_ref[...], v_ref[...])
        g_out[...] = g
        d_out[...] = delta
        m_out[...] = mn
        v_out[...] = vn

    row = pl.BlockSpec((tr, 128), lambda i: (i, 0))
    return pl.pallas_call(
        body, name=name, grid=(R // tr,),
        in_specs=[pl.BlockSpec((N_DEV, tr, 128), lambda i: (0, i, 0)), row, row, row], out_specs=[row, row, row, row],
        out_shape=[jax.ShapeDtypeStruct((R, 128), F32)] * 4, compiler_params=_params(("parallel",)),
    )(parts, w, m, v)


def _adamw_plain(g, w, m, v, name):
    def body(g_ref, w_ref, m_ref, v_ref, d_out, m_out, v_out):
        delta, mn, vn = _adam_math(w_ref[...], g_ref[...], m_ref[...], v_ref[...])
        d_out[...] = delta
        m_out[...] = mn
        v_out[...] = vn

    return pl.pallas_call(body, name=name, out_shape=[jax.ShapeDtypeStruct(w.shape, F32)] * 3)(g, w, m, v)


def _pack(arrays):
    rows = []
    for a in arrays:
        flat = a.reshape(-1).astype(F32)
        n = flat.shape[0]
        padded = -(-n // 1024) * 1024
        rows.append(jnp.pad(flat, (0, padded - n)).reshape(padded // 128, 128))
    return jnp.concatenate(rows, axis=0)


def _unpack(packed, shapes):
    out, row = [], 0
    for s in shapes:
        n = 1
        for d in s:
            n *= d
        nrow = -(-n // 1024) * 8
        out.append(packed[row:row + nrow].reshape(-1)[:n].reshape(s))
        row += nrow
    return out


def kernel(x, mem, positions, g_mix, w_in, g_a_v, w_spatial, b_spatial, g_b_q, g_b_k, sinks, g_mem, w_mem_kv, g_c_q, g_c_k, w_branch_a, w_branch_b, w_branch_c, w_out, g_ffn, w_up, conv_w, conv_b, w_down, loss_target, m_g_mix, m_w_in, m_g_a_v, m_w_spatial, m_b_spatial, m_g_b_q, m_g_b_k, m_sinks, m_g_mem, m_w_mem_kv, m_g_c_q, m_g_c_k, m_w_branch_a, m_w_branch_b, m_w_branch_c, m_w_out, m_g_ffn, m_w_up, m_conv_w, m_conv_b, m_w_down, v_g_mix, v_w_in, v_g_a_v, v_w_spatial, v_b_spatial, v_g_b_q, v_g_b_k, v_sinks, v_g_mem, v_w_mem_kv, v_g_c_q, v_g_c_k, v_w_branch_a, v_w_branch_b, v_w_branch_c, v_w_out, v_g_ffn, v_w_up, v_conv_w, v_conv_b, v_w_down):
    S, D = x.shape[1], x.shape[2]
    M = mem.shape[1]
    F = w_down.shape[1] * N_DEV
    in_cols = w_in.shape[2] * N_DEV
    ax, ay, ac = lax.axis_index("x"), lax.axis_index("y"), lax.axis_index("c")
    core = jnp.reshape(ac, (1,)).astype(jnp.int32)
    chip = jnp.reshape(2 * ax + ay, (1,)).astype(jnp.int32)
    me = 4 * ax + 2 * ay + ac

    x2, mem2, tgt2 = x[0], mem[0], loss_target[0]

    big = dict(w_in=w_in[0].T, w_mem_kv=w_mem_kv[0], w_branch_a=w_branch_a[0], w_branch_b=w_branch_b[0],
               w_branch_c=w_branch_c[0], w_out=w_out[0], w_up=w_up[0], w_down=w_down[0])
    names = list(big)
    cast = {k: big[k].astype(BF) for k in names}
    W = {}
    cb3 = conv_b.reshape(2, 1, F)
    W["w_in"], = _allgather_seq([cast["w_in"]], "ag_seq0", 0)
    w_in_t = W["w_in"].reshape(in_cols, D)
    grp1 = ["w_mem_kv", "w_branch_a", "w_branch_b", "w_branch_c", "w_out"]
    res1 = _allgather_seq([cast[k] for k in grp1] + [conv_w[0]], "ag_seq1", 1, after=(_token(w_in_t, "tok_w_in"),))
    W.update(zip(grp1, res1))
    cw3 = res1[-1]
    w_kv_f = W["w_mem_kv"].reshape(D, 2 * C_WIDTH)
    w_out_f = W["w_out"].reshape(D, D)

    half = ROPE_DIM // 2
    inv = ROPE_THETA ** (-jnp.arange(half, dtype=F32) / half)
    ang = positions[0].astype(F32)[:, None] * inv
    cos, sin = jnp.cos(ang), jnp.sin(ang)
    one, zero = jnp.ones((S, B_HEAD_DIM - ROPE_DIM), F32), jnp.zeros((S, B_HEAD_DIM - ROPE_DIM), F32)
    z8 = jnp.zeros((S, half), F32)
    ct = jnp.tile(jnp.concatenate([cos, cos, one], axis=1), (1, 2))
    sa = jnp.tile(jnp.concatenate([-sin, z8, zero], axis=1), (1, 2))
    sb = jnp.tile(jnp.concatenate([z8, sin, zero], axis=1), (1, 2))
    gq2, gk2 = jnp.tile(g_b_q, (1, 2)), jnp.tile(g_b_k, (1, 2))
    b_t = b_spatial[0].T

    h, rstd1 = _rms_fwd(x2, g_mix, "rms1_fwd")
    proj = _mm(h, w_in_t, "nt", F32, "mm_proj", tn=1280)
    y_a = _a_fwd(proj, g_a_v, w_spatial[0], b_t)
    qn, kn = _b_pre(proj, gq2, gk2, ct, sa, sb)
    y_b = _b_attn_fwd(qn, kn, proj, sinks)
    mem_h, rstd_m = _rms_fwd(mem2, g_mem, "rmsmem_fwd")
    kv = _mm(mem_h, w_kv_f, "nn", F32, "mm_kv", after=(y_b,))
    W["w_up"], = _allgather_seq([cast["w_up"]], "ag_seq2", 2, after=(kv,))
    y_c = _c_fwd(proj, kv, g_c_q, g_c_k)
    z_a = _mm(y_a, W["w_branch_a"], "nn", BF, "mm_za", b_stack=True, after=(y_b,))
    z_b = _mm(y_b, W["w_branch_b"], "nn", BF, "mm_zb", b_stack=True)
    z_c = _mm(y_c, W["w_branch_c"], "nn", BF, "mm_zc", b_stack=True)
    merged = _merge_fwd(proj, z_a, z_b, z_c)
    x1 = _mm(merged, w_out_f, "nn", F32, "mm_x1", resid=x2)
    h2, rstd2 = _rms_fwd(x1, g_ffn, "rms2_fwd")
    W["w_down"], = _allgather_seq([cast["w_down"]], "ag_seq3", 3, after=(W["w_up"], h2))
    w_down_f = W["w_down"].reshape(F, D)
    up3 = _mm(h2, W["w_up"], "nn", F32, "mm_up", b_stack=True, out_parts=2)
    act = _ffn_act_fwd(up3, cw3, cb3)
    y = _mm(act, w_down_f, "nn", F32, "mm_y", resid=x1, tk=1408)
    dy, dy_b, loss_acc = _loss(y, tgt2)
    loss = lax.psum(loss_acc[0, 0], ("x", "y", "c"))

    reduced = {}

    def reduce_group(gi, keys, stacked, add_after):
        g4 = [g.reshape(4, 2, g.shape[1], g.shape[2]) for g in stacked]
        from_sibling = _sibling_exchange(g4, f"rs_sib{gi}")
        sums = [_pair_add(a, b, core, "rs_add_" + k, after=add_after) for k, a, b in zip(keys, g4, from_sibling)]
        from_chips = _chip_exchange(sums, f"rs_chip{gi}", 4 + gi)
        reduced.update(zip(keys, zip(sums, from_chips)))
        return tuple(sums)

    d_act = _mm(dy_b, w_down_f, "nt", F32, "mm_dact", tn=1408)
    g_down = _mm(act, dy_b, "tn", BF, "mm_gdown", tm=1408)
    d_up3, d_cw3, d_cb3 = _ffn_act_bwd(up3, cw3, cb3, d_act, after=(g_down,))
    sums0 = reduce_group(0, ["w_down"], [g_down.reshape(N_DEV, F // N_DEV, D)], (d_up3,))
    d_h2 = _mm(d_up3, W["w_up"], "nt", F32, "mm_dh2", a_parts=2, b_stack=True, after=sums0)
    g_up = _mm(h2, d_up3, "tn", BF, "mm_gup", b_parts=2, out_stack=True)
    dx1, dx1_b, d_g_ffn = _rms_bwd(x1, rstd2, g_ffn, d_h2, dy, "rms2_bwd", after=(g_up,))
    d_merged = _mm(dx1_b, w_out_f, "nt", F32, "mm_dmerged")
    g_out = _mm(merged, dx1_b, "tn", BF, "mm_gout")
    sums1 = reduce_group(1, ["w_up"], [g_up], (g_out,))
    dz_a, dz_b, dz_c, dga, dgb, dgc = _merge_bwd(proj, z_a, z_b, z_c, d_merged, after=sums1)
    dy_a = _mm(dz_a, W["w_branch_a"], "nt", F32, "mm_dya", b_stack=True)
    dy_b_ = _mm(dz_b, W["w_branch_b"], "nt", F32, "mm_dyb", b_stack=True)
    dy_c = _mm(dz_c, W["w_branch_c"], "nt", F32, "mm_dyc", b_stack=True)
    g_ba = _mm(y_a, dz_a, "tn", BF, "mm_gba", out_stack=True)
    g_bb = _mm(y_b, dz_b, "tn", BF, "mm_gbb", out_stack=True)
    g_bc = _mm(y_c, dz_c, "tn", BF, "mm_gbc", out_stack=True)
    d_uv, d_g_a_v, d_w_s, d_b_t = _a_bwd(proj, g_a_v, w_spatial[0], b_t, dy_a, after=(g_ba, g_bb, g_bc))
    sums2 = reduce_group(2, ["w_out", "w_branch_a", "w_branch_b", "w_branch_c"],
                         [g_out.reshape(N_DEV, D // N_DEV, D), g_ba, g_bb, g_bc], (d_uv,))
    dqn, dkn, dv_b, dsink_rows = _b_attn_bwd(qn, kn, proj, sinks, dy_b_, after=sums2)
    d_qkv, d_gq2, d_gk2 = _b_pre_bwd(proj, gq2, gk2, ct, sa, sb, dqn, dkn, dv_b)
    dq_c, dk_c, dv_c, d_gcq, d_gck = _c_bwd(proj, kv, g_c_q, g_c_k, dy_c)
    dkv_b = jnp.concatenate([dk_c, dv_c], axis=1).astype(BF)
    d_memh = _mm(dkv_b, w_kv_f, "nt", F32, "mm_dmemh")
    g_kv = _mm(mem_h, dkv_b, "tn", BF, "mm_gkv")
    _, _, d_g_mem = _rms_bwd(mem2, rstd_m, g_mem, d_memh, None, "rmsmem_bwd")
    dproj = jnp.concatenate([d_uv, d_qkv, dq_c, dga, dgb, dgc], axis=1)
    g_in = _mm(dproj, h, "tn", BF, "mm_gin", tm=1280)
    d_h = _mm(dproj, w_in_t, "nn", F32, "mm_dh", tk=1792, after=(g_in, g_kv))
    sums3 = reduce_group(3, ["w_in", "w_mem_kv"],
                         [g_in.reshape(N_DEV, in_cols // N_DEV, D), g_kv.reshape(N_DEV, D // N_DEV, 2 * C_WIDTH)], (d_h,))
    grad_x, _, d_g_mix = _rms_bwd(x2, rstd1, g_mix, d_h, dx1, "rms1_bwd", after=sums3)

    d_conv_w = d_cw3.reshape(2, 3, N_DEV // 2, 2 * F // N_DEV).transpose(1, 0, 2, 3).reshape(3, 2 * F)
    small_names = ["g_mix", "g_a_v", "w_spatial", "b_spatial", "g_b_q", "g_b_k", "sinks", "g_mem", "g_c_q", "g_c_k", "g_ffn", "conv_b"]
    small_w = dict(g_mix=g_mix, g_a_v=g_a_v, w_spatial=w_spatial, b_spatial=b_spatial, g_b_q=g_b_q, g_b_k=g_b_k, sinks=sinks,
                   g_mem=g_mem, g_c_q=g_c_q, g_c_k=g_c_k, g_ffn=g_ffn, conv_b=conv_b)
    small_m = dict(g_mix=m_g_mix, g_a_v=m_g_a_v, w_spatial=m_w_spatial, b_spatial=m_b_spatial, g_b_q=m_g_b_q, g_b_k=m_g_b_k,
                   sinks=m_sinks, g_mem=m_g_mem, g_c_q=m_g_c_q, g_c_k=m_g_c_k, g_ffn=m_g_ffn, conv_b=m_conv_b)
    small_v = dict(g_mix=v_g_mix, g_a_v=v_g_a_v, w_spatial=v_w_spatial, b_spatial=v_b_spatial, g_b_q=v_g_b_q, g_b_k=v_g_b_k,
                   sinks=v_sinks, g_mem=v_g_mem, g_c_q=v_g_c_q, g_c_k=v_g_c_k, g_ffn=v_g_ffn, conv_b=v_conv_b)
    small_g = dict(
        g_mix=d_g_mix, g_a_v=d_g_a_v, w_spatial=d_w_s, b_spatial=d_b_t.T,
        g_b_q=d_gq2.reshape(2, B_HEAD_DIM).sum(0), g_b_k=d_gk2.reshape(2, B_HEAD_DIM).sum(0),
        sinks=dsink_rows.sum(0)[:B_HEADS], g_mem=d_g_mem, g_c_q=d_gcq.sum(0), g_c_k=d_gck.sum(0), g_ffn=d_g_ffn,
        conv_b=d_cb3)
    cw_zero = jnp.zeros((3, 2 * F), F32)
    packed_g = _pack([small_g[k] for k in small_names] + [d_conv_w])
    packed_w = _pack([small_w[k] for k in small_names] + [cw_zero])
    packed_m = _pack([small_m[k] for k in small_names] + [cw_zero])
    packed_v = _pack([small_v[k] for k in small_names] + [cw_zero])
    parts = _allgather([packed_g], "ag_small")[0]
    sg, sd, sm, sv = _adamw_small(parts, packed_w, packed_m, packed_v, "adamw_small")
    shapes = [small_w[k].shape for k in small_names] + [(3, 2 * F)]
    sg_l, sd_l, sm_l, sv_l = (_unpack(p, shapes) for p in (sg, sd, sm, sv))
    small_out = {k: [sg_l[i], sd_l[i], sm_l[i], sv_l[i]] for i, k in enumerate(small_names)}
    c_cw = 2 * F // N_DEV
    g_cw = lax.dynamic_slice_in_dim(sg_l[-1], me * c_cw, c_cw, axis=1)
    cw_res = _adamw_plain(g_cw, conv_w[0], m_conv_w[0], v_conv_w[0], "adamw_conv_w")
    big_out = {"conv_w": [g_cw[None]] + [a[None] for a in cw_res]}

    moments = dict(w_in=(m_w_in, v_w_in), w_mem_kv=(m_w_mem_kv, v_w_mem_kv), w_branch_a=(m_w_branch_a, v_w_branch_a),
                   w_branch_b=(m_w_branch_b, v_w_branch_b), w_branch_c=(m_w_branch_c, v_w_branch_c), w_out=(m_w_out, v_w_out),
                   w_up=(m_w_up, v_w_up), w_down=(m_w_down, v_w_down))
    token = (grad_x, sg)
    for k in ["w_down", "w_up", "w_out", "w_branch_a", "w_branch_b", "w_branch_c", "w_mem_kv", "w_in"]:
        s, r = reduced[k]
        mk, vk = moments[k][0][0], moments[k][1][0]
        if k == "w_in":
            res = _adamw_big(s, r, chip, big[k], mk.T, vk.T, "adamw_" + k, after=token)
            big_out[k] = [a.T[None] for a in res]
        else:
            res = _adamw_big(s, r, chip, big[k], mk, vk, "adamw_" + k, after=token)
            big_out[k] = [a[None] for a in res]
        token = (res[0],)

    order = ["g_mix", "w_in", "g_a_v", "w_spatial", "b_spatial", "g_b_q", "g_b_k", "sinks", "g_mem", "w_mem_kv", "g_c_q", "g_c_k",
             "w_branch_a", "w_branch_b", "w_branch_c", "w_out", "g_ffn", "w_up", "conv_w", "conv_b", "w_down"]
    res = {**small_out, **big_out}
    outs = [loss, grad_x[None]]
    for field in range(4):
        outs += [res[k][field] for k in order]
    return tuple(outs)
```

```python
import functools

import jax
import jax.numpy as jnp
from jax import lax
from jax.experimental import pallas as pl
from jax.experimental.pallas import tpu as pltpu
from jax.experimental.pallas import tpu_sc as plsc

F32 = jnp.float32
BF = jnp.bfloat16
EPS = 1e-6
NEG = -1e30

N_DEV = 8
CHUNK = 128
A_GROUPS = 4
A_WIDTH = 512
B_HEADS = 16
B_KV_HEADS = 2
B_HEAD_DIM = 64
B_WIDTH = 1024
B_KV_WIDTH = 128
ROPE_DIM = 16
ROPE_THETA = 500000.0
C_HEADS = 4
C_HEAD_DIM = 128
C_WIDTH = 512
GATE_OFF = 2 * A_WIDTH + B_WIDTH + 2 * B_KV_WIDTH + C_WIDTH

ADAM_LR = 0.001
ADAM_B1 = 0.9
ADAM_B2 = 0.999
ADAM_EPS = 1e-08
ADAM_WD = 0.01
ADAM_STEP = 10

VMEM_LIMIT = 48 * 1024 * 1024
MESH = pl.DeviceIdType.MESH


def _pick(n, prefs):
    for p in prefs:
        if p <= n and n % p == 0:
            return p
    return n


def _params(sem):
    return pltpu.CompilerParams(dimension_semantics=sem, vmem_limit_bytes=VMEM_LIMIT)


def _hide(body, n_seen, n_hidden):
    if not n_hidden:
        return body

    def wrapped(*refs):
        return body(*refs[:n_seen], *refs[n_seen + n_hidden:])

    return wrapped


def _hidden_specs(after):
    return [pl.BlockSpec(memory_space=pl.ANY) for _ in after]


def _token(xs, name):
    def body(*refs):
        refs[-1][...] = jnp.zeros_like(refs[-1])

    return pl.pallas_call(body, name=name, in_specs=_hidden_specs(xs), out_shape=jax.ShapeDtypeStruct((8, 128), F32))(*xs)


def _mm(a, b, mode, out_dtype, name, *, resid=None, b_stack=False, a_parts=0, b_parts=0, out_parts=0,
        out_stack=False, tm=1024, tn=1024, tk=2048, after=()):
    if mode == "nn":
        M = a.shape[-2]
        K = a.shape[-1] * max(a_parts, 1)
        N = b.shape[-1] * (N_DEV if b_stack else 1)
        dims = (((1,), (0,)), ((), ()))
    elif mode == "nt":
        M = a.shape[-2]
        K = a.shape[-1] * max(a_parts, 1)
        N = b.shape[-2]
        dims = (((1,), (1,)), ((), ()))
    else:
        K = a.shape[-2]
        M = a.shape[-1]
        N = b.shape[-1] * max(b_parts, 1)
        dims = (((0,), (0,)), ((), ()))
    if b_stack and mode == "nn":
        tn = b.shape[-1]
    if b_stack and mode == "nt":
        tk = b.shape[-1]
    if out_stack:
        tn = N // N_DEV
    tm, tn, tk = _pick(M, (tm,)), _pick(N, (tn,)), _pick(K, (tk,))
    if M % tm or N % tn or K % tk:
        raise ValueError(f"{name}: tiles {tm},{tn},{tk} do not divide {M},{N},{K}")
    nm, nn, nk = M // tm, N // tn, K // tk

    def parts_idx(t, ntile, parts):
        per = ntile // parts
        return t // per, t % per

    if mode in ("nn", "nt"):
        if a_parts:
            a_spec = pl.BlockSpec((None, tm, tk), lambda m, n, k: (parts_idx(k, nk, a_parts)[0], m, parts_idx(k, nk, a_parts)[1]))
        else:
            a_spec = pl.BlockSpec((tm, tk), lambda m, n, k: (m, k))
    else:
        a_spec = pl.BlockSpec((tk, tm), lambda m, n, k: (k, m))
    if mode == "nn":
        if b_stack:
            b_spec = pl.BlockSpec((None, tk, tn), lambda m, n, k: (n, k, 0))
        else:
            b_spec = pl.BlockSpec((tk, tn), lambda m, n, k: (k, n))
    elif mode == "nt":
        if b_stack:
            b_spec = pl.BlockSpec((None, tn, tk), lambda m, n, k: (k, n, 0))
        else:
            b_spec = pl.BlockSpec((tn, tk), lambda m, n, k: (n, k))
    else:
        if b_parts:
            b_spec = pl.BlockSpec((None, tk, tn), lambda m, n, k: (parts_idx(n, nn, b_parts)[0], k, parts_idx(n, nn, b_parts)[1]))
        else:
            b_spec = pl.BlockSpec((tk, tn), lambda m, n, k: (k, n))
    if out_stack:
        out_shape = jax.ShapeDtypeStruct((N_DEV, M, tn), out_dtype)
        o_spec = pl.BlockSpec((None, tm, tn), lambda m, n, k: (n, m, 0))
    elif out_parts:
        out_shape = jax.ShapeDtypeStruct((out_parts, M, N // out_parts), out_dtype)
        o_spec = pl.BlockSpec((None, tm, tn), lambda m, n, k: (parts_idx(n, nn, out_parts)[0], m, parts_idx(n, nn, out_parts)[1]))
    else:
        out_shape = jax.ShapeDtypeStruct((M, N), out_dtype)
        o_spec = pl.BlockSpec((tm, tn), lambda m, n, k: (m, n))
    has_resid = resid is not None

    def body(*refs):
        a_ref, b_ref = refs[:2]
        r_ref = refs[2] if has_resid else None
        if nk == 1:
            o_ref = refs[-1]
            res = lax.dot_general(a_ref[...], b_ref[...], dims, preferred_element_type=F32)
            if has_resid:
                res = res + r_ref[...]
            o_ref[...] = res.astype(o_ref.dtype)
            return
        o_ref, acc = refs[-2:]
        k = pl.program_id(2)

        @pl.when(k == 0)
        def _():
            acc[...] = jnp.zeros_like(acc)

        acc[...] += lax.dot_general(a_ref[...], b_ref[...], dims, preferred_element_type=F32)

        @pl.when(k == nk - 1)
        def _():
            res = acc[...]
            if has_resid:
                res = res + r_ref[...]
            o_ref[...] = res.astype(o_ref.dtype)

    in_specs = [a_spec, b_spec]
    args = [a, b]
    if has_resid:
        in_specs.append(pl.BlockSpec((tm, tn), lambda m, n, k: (m, n)))
        args.append(resid)
    for t in after:
        in_specs.append(pl.BlockSpec(memory_space=pl.ANY))
        args.append(t)
    return pl.pallas_call(
        body, name=name, grid=(nm, nn, nk), in_specs=in_specs, out_specs=o_spec, out_shape=out_shape,
        scratch_shapes=[pltpu.VMEM((tm, tn), F32)] if nk > 1 else [],
        compiler_params=_params(("parallel", "parallel", "arbitrary")),
    )(*args)


def _rms_fwd(x, g, name):
    R, D = x.shape
    tr = _pick(R, (256,))

    def body(x_ref, g_ref, h_ref, r_ref):
        xv = x_ref[...]
        r = lax.rsqrt(jnp.mean(xv * xv, axis=-1, keepdims=True) + EPS)
        h_ref[...] = (xv * r * g_ref[...]).astype(BF)
        r_ref[...] = r

    return pl.pallas_call(
        body, name=name, grid=(R // tr,),
        in_specs=[pl.BlockSpec((tr, D), lambda i: (i, 0)), pl.BlockSpec((1, D), lambda i: (0, 0))],
        out_specs=[pl.BlockSpec((tr, D), lambda i: (i, 0)), pl.BlockSpec((tr, 1), lambda i: (i, 0))],
        out_shape=[jax.ShapeDtypeStruct((R, D), BF), jax.ShapeDtypeStruct((R, 1), F32)],
        compiler_params=_params(("parallel",)),
    )(x, g)


def _rms_bwd(x, r, g, dh, dres, name, after=()):
    R, D = x.shape
    tr = _pick(R, (256,))
    has_res = dres is not None

    def body(*refs):
        if has_res:
            x_ref, r_ref, g_ref, dh_ref, dres_ref, dx_ref, dxb_ref, dg_ref = refs
        else:
            x_ref, r_ref, g_ref, dh_ref, dx_ref, dxb_ref, dg_ref = refs
        i = pl.program_id(0)
        xv, rv, dhv = x_ref[...], r_ref[...], dh_ref[...]
        gy = dhv * g_ref[...]
        c = jnp.sum(xv * gy, axis=-1, keepdims=True)
        dx = rv * gy - xv * (rv * rv * rv) * (c * (1.0 / D))
        if has_res:
            dx = dx + dres_ref[...]
        dx_ref[...] = dx
        dxb_ref[...] = dx.astype(BF)
        part = jnp.sum(dhv * xv * rv, axis=0, keepdims=True)

        @pl.when(i == 0)
        def _():
            dg_ref[...] = part

        @pl.when(i > 0)
        def _():
            dg_ref[...] += part

    row = pl.BlockSpec((tr, D), lambda i: (i, 0))
    in_specs = [row, pl.BlockSpec((tr, 1), lambda i: (i, 0)), pl.BlockSpec((1, D), lambda i: (0, 0)), row]
    args = [x, r, g, dh]
    if has_res:
        in_specs.append(row)
        args.append(dres)
    return pl.pallas_call(
        _hide(body, len(args), len(after)), name=name, grid=(R // tr,), in_specs=in_specs + _hidden_specs(after),
        out_specs=[row, row, pl.BlockSpec((1, D), lambda i: (0, 0))],
        out_shape=[jax.ShapeDtypeStruct((R, D), F32), jax.ShapeDtypeStruct((R, D), BF), jax.ShapeDtypeStruct((1, D), F32)],
        compiler_params=_params(("arbitrary",)),
    )(*args, *after)


def _a_chunk(us, vs, gvs, ws, bs):
    r_i = lax.broadcasted_iota(jnp.int32, (CHUNK, CHUNK), 0)
    c_i = lax.broadcasted_iota(jnp.int32, (CHUNK, CHUNK), 1)
    causal = r_i >= c_i
    vg = [jax.nn.gelu(v) for v in vs]
    ss = sum(jnp.sum(v * v, axis=-1, keepdims=True) for v in vg)
    r = lax.rsqrt(ss * (1.0 / A_WIDTH) + EPS)
    ys = []
    for g in range(A_GROUPS):
        vn = vg[g] * r * gvs[g]
        w = jnp.where(causal, ws[g], 0.0)
        s = jnp.dot(w.astype(BF), vn.astype(BF), preferred_element_type=F32) + bs[g]
        ys.append(jax.nn.gelu(us[g]) * s)
    return ys


def _a_split(u_ref, v_ref, g_ref, w_ref, b_ref):
    sl = [slice(g * 128, (g + 1) * 128) for g in range(A_GROUPS)]
    return ([u_ref[:, s] for s in sl], [v_ref[:, s] for s in sl], [g_ref[:, s] for s in sl],
            [w_ref[g] for g in range(A_GROUPS)], [b_ref[:, g:g + 1] for g in range(A_GROUPS)])


def _a_specs(S):
    return [pl.BlockSpec((CHUNK, A_WIDTH), lambda n: (n, 0)), pl.BlockSpec((CHUNK, A_WIDTH), lambda n: (n, 1)),
            pl.BlockSpec((1, A_WIDTH), lambda n: (0, 0)), pl.BlockSpec((A_GROUPS, CHUNK, CHUNK), lambda n: (0, 0, 0)),
            pl.BlockSpec((CHUNK, A_GROUPS), lambda n: (0, 0))]


def _a_fwd(proj, g_v, w_s, b_t):
    S = proj.shape[0]

    def body(u_ref, v_ref, g_ref, w_ref, b_ref, y_ref):
        ys = _a_chunk(*_a_split(u_ref, v_ref, g_ref, w_ref, b_ref))
        for g in range(A_GROUPS):
            y_ref[:, g * 128:(g + 1) * 128] = ys[g].astype(BF)

    return pl.pallas_call(
        body, name="a_fwd", grid=(S // CHUNK,), in_specs=_a_specs(S),
        out_specs=pl.BlockSpec((CHUNK, A_WIDTH), lambda n: (n, 0)),
        out_shape=jax.ShapeDtypeStruct((S, A_WIDTH), BF), compiler_params=_params(("parallel",)),
    )(proj, proj, g_v, w_s, b_t)


def _a_bwd(proj, g_v, w_s, b_t, dy, after=()):
    S = proj.shape[0]

    def body(u_ref, v_ref, g_ref, w_ref, b_ref, dy_ref, duv_ref, dg_ref, dw_ref, db_ref):
        n = pl.program_id(0)
        dys = [dy_ref[:, g * 128:(g + 1) * 128] for g in range(A_GROUPS)]
        _, vjp = jax.vjp(_a_chunk, *_a_split(u_ref, v_ref, g_ref, w_ref, b_ref))
        dus, dvs, dgs, dws, dbs = vjp(dys)

        @pl.when(n == 0)
        def _():
            dg_ref[...] = jnp.zeros_like(dg_ref)
            dw_ref[...] = jnp.zeros_like(dw_ref)
            db_ref[...] = jnp.zeros_like(db_ref)

        for g in range(A_GROUPS):
            duv_ref[:, g * 128:(g + 1) * 128] = dus[g].astype(BF)
            duv_ref[:, A_WIDTH + g * 128:A_WIDTH + (g + 1) * 128] = dvs[g].astype(BF)
            dg_ref[:, g * 128:(g + 1) * 128] += dgs[g]
            dw_ref[g] += dws[g]
            db_ref[:, g:g + 1] += dbs[g]

    return pl.pallas_call(
        _hide(body, 6, len(after)), name="a_bwd", grid=(S // CHUNK,),
        in_specs=_a_specs(S) + [pl.BlockSpec((CHUNK, A_WIDTH), lambda n: (n, 0))] + _hidden_specs(after),
        out_specs=[pl.BlockSpec((CHUNK, 2 * A_WIDTH), lambda n: (n, 0)), pl.BlockSpec((1, A_WIDTH), lambda n: (0, 0)),
                   pl.BlockSpec((A_GROUPS, CHUNK, CHUNK), lambda n: (0, 0, 0)), pl.BlockSpec((CHUNK, A_GROUPS), lambda n: (0, 0))],
        out_shape=[jax.ShapeDtypeStruct((S, 2 * A_WIDTH), BF), jax.ShapeDtypeStruct((1, A_WIDTH), F32),
                   jax.ShapeDtypeStruct((A_GROUPS, CHUNK, CHUNK), F32), jax.ShapeDtypeStruct((CHUNK, A_GROUPS), F32)],
        compiler_params=_params(("arbitrary",)),
    )(proj, proj, g_v, w_s, b_t, dy, *after)


def _half_mask(shape, which):
    lane = lax.broadcasted_iota(jnp.int32, shape, len(shape) - 1)
    return (lane >= 64) == (which == 1)


def _pair_norm_rope(x, g, ct, sa, sb):
    lo = _half_mask(x.shape, 0)
    x2 = x * x
    ss_lo = jnp.sum(jnp.where(lo, x2, 0.0), axis=-1, keepdims=True)
    ss_hi = jnp.sum(jnp.where(lo, 0.0, x2), axis=-1, keepdims=True)
    r = jnp.where(lo, lax.rsqrt(ss_lo * (1.0 / B_HEAD_DIM) + EPS), lax.rsqrt(ss_hi * (1.0 / B_HEAD_DIM) + EPS))
    xr = x * r
    xn = xr * g
    out = xn * ct + pltpu.roll(xn, 120, 1) * sa + pltpu.roll(xn, 8, 1) * sb
    return out, xr, r


def _pair_norm_rope_bwd(x, g, ct, sa, sb, dout):
    lo = _half_mask(x.shape, 0)
    _, xr, r = _pair_norm_rope(x, g, ct, sa, sb)
    dxn = dout * ct + pltpu.roll(dout * sa, 8, 1) + pltpu.roll(dout * sb, 120, 1)
    gy = dxn * g
    t = xr * gy
    c_lo = jnp.sum(jnp.where(lo, t, 0.0), axis=-1, keepdims=True)
    c_hi = jnp.sum(jnp.where(lo, 0.0, t), axis=-1, keepdims=True)
    c = jnp.where(lo, c_lo, c_hi)
    dx = r * (gy - xr * c * (1.0 / B_HEAD_DIM))
    dg = jnp.sum(dxn * xr, axis=0, keepdims=True)
    return dx, dg


def _b_pre(proj, gq2, gk2, ct, sa, sb):
    S = proj.shape[0]
    tr = _pick(S, (256,))
    n_pair = B_WIDTH // 128

    def body(q_ref, k_ref, gq_ref, gk_ref, ct_ref, sa_ref, sb_ref, qn_ref, kn_ref):
        ct_v, sa_v, sb_v = ct_ref[...], sa_ref[...], sb_ref[...]
        for p in range(n_pair):
            o, _, _ = _pair_norm_rope(q_ref[:, p * 128:(p + 1) * 128], gq_ref[...], ct_v, sa_v, sb_v)
            qn_ref[:, p * 128:(p + 1) * 128] = o.astype(BF)
        o, _, _ = _pair_norm_rope(k_ref[...], gk_ref[...], ct_v, sa_v, sb_v)
        kn_ref[...] = o.astype(BF)

    tab = pl.BlockSpec((tr, 128), lambda i: (i, 0))
    gsp = pl.BlockSpec((1, 128), lambda i: (0, 0))
    return pl.pallas_call(
        body, name="b_pre", grid=(S // tr,),
        in_specs=[pl.BlockSpec((tr, B_WIDTH), lambda i: (i, 1)), pl.BlockSpec((tr, 128), lambda i: (i, 2 * B_WIDTH // 128)),
                  gsp, gsp, tab, tab, tab],
        out_specs=[pl.BlockSpec((tr, B_WIDTH), lambda i: (i, 0)), tab],
        out_shape=[jax.ShapeDtypeStruct((S, B_WIDTH), BF), jax.ShapeDtypeStruct((S, 128), BF)],
        compiler_params=_params(("parallel",)),
    )(proj, proj, gq2, gk2, ct, sa, sb)


def _b_pre_bwd(proj, gq2, gk2, ct, sa, sb, dqn, dkn, dv):
    S = proj.shape[0]
    tr = _pick(S, (256,))
    n_pair = B_WIDTH // 128

    def body(q_ref, k_ref, gq_ref, gk_ref, ct_ref, sa_ref, sb_ref, dqn_ref, dkn_ref, dv_ref, dqkv_ref, dgq_ref, dgk_ref):
        i = pl.program_id(0)
        ct_v, sa_v, sb_v = ct_ref[...], sa_ref[...], sb_ref[...]
        dgq = jnp.zeros((1, 128), F32)
        for p in range(n_pair):
            sl = slice(p * 128, (p + 1) * 128)
            dx, dg = _pair_norm_rope_bwd(q_ref[:, sl], gq_ref[...], ct_v, sa_v, sb_v, dqn_ref[:, sl])
            dqkv_ref[:, sl] = dx.astype(BF)
            dgq = dgq + dg
        dx, dgk = _pair_norm_rope_bwd(k_ref[...], gk_ref[...], ct_v, sa_v, sb_v, dkn_ref[...])
        dqkv_ref[:, B_WIDTH:B_WIDTH + 128] = dx.astype(BF)
        dqkv_ref[:, B_WIDTH + 128:B_WIDTH + 256] = dv_ref[...].astype(BF)

        @pl.when(i == 0)
        def _():
            dgq_ref[...] = dgq
            dgk_ref[...] = dgk

        @pl.when(i > 0)
        def _():
            dgq_ref[...] += dgq
            dgk_ref[...] += dgk

    tab = pl.BlockSpec((tr, 128), lambda i: (i, 0))
    gsp = pl.BlockSpec((1, 128), lambda i: (0, 0))
    return pl.pallas_call(
        body, name="b_pre_bwd", grid=(S // tr,),
        in_specs=[pl.BlockSpec((tr, B_WIDTH), lambda i: (i, 1)), pl.BlockSpec((tr, 128), lambda i: (i, 2 * B_WIDTH // 128)),
                  gsp, gsp, tab, tab, tab, pl.BlockSpec((tr, B_WIDTH), lambda i: (i, 0)), tab, tab],
        out_specs=[pl.BlockSpec((tr, B_WIDTH + 256), lambda i: (i, 0)), gsp, gsp],
        out_shape=[jax.ShapeDtypeStruct((S, B_WIDTH + 256), BF), jax.ShapeDtypeStruct((1, 128), F32), jax.ShapeDtypeStruct((1, 128), F32)],
        compiler_params=_params(("arbitrary",)),
    )(proj, proj, gq2, gk2, ct, sa, sb, dqn, dkn, dv)


def _b_dup(x2, g):
    d = jnp.where(_half_mask(x2.shape, g), x2, 0.0)
    return (d + pltpu.roll(d, 64, 1)).astype(BF)


PAIRS_PER_GROUP = B_HEADS // B_KV_HEADS // 2
GROUP_ROWS = PAIRS_PER_GROUP * CHUNK


def _b_valid(n):
    row = lax.broadcasted_iota(jnp.int32, (GROUP_ROWS, 2 * CHUNK), 0) & (CHUNK - 1)
    col = lax.broadcasted_iota(jnp.int32, (GROUP_ROWS, 2 * CHUNK), 1)
    rel = row + CHUNK - col
    return (rel >= 0) & (rel < CHUNK) & ((col >= CHUNK) | (n > 0))


def _b_blocks(x2, g):
    xd = _b_dup(x2, g)
    lo = _half_mask(xd.shape, 0)
    zero = jnp.zeros_like(xd)
    return jnp.concatenate([jnp.where(lo, xd, zero), jnp.where(lo, zero, xd)], axis=0)


def _b_sink_col(s_ref, g, hf):
    rb = lax.broadcasted_iota(jnp.int32, (GROUP_ROWS, 1), 0) // CHUNK
    col = jnp.zeros((GROUP_ROWS, 1), F32)
    for pp in range(PAIRS_PER_GROUP):
        col = jnp.where(rb == pp, s_ref[0, 2 * (g * PAIRS_PER_GROUP + pp) + hf], col)
    return col


def _b_probs(qs, kblk, valid, sinks):
    s = lax.dot_general(qs, kblk, (((1,), (1,)), ((), ())), preferred_element_type=F32) * (B_HEAD_DIM ** -0.5)
    out = []
    for hf in range(2):
        sh = jnp.where(valid, s[:, hf * 2 * CHUNK:(hf + 1) * 2 * CHUNK], NEG)
        m = jnp.maximum(jnp.max(sh, axis=-1, keepdims=True), sinks[hf])
        e = jnp.exp(sh - m)
        es = jnp.exp(sinks[hf] - m)
        inv = 1.0 / (jnp.sum(e, axis=-1, keepdims=True) + es)
        out.append((e * inv, es * inv))
    return out


def _b_fold(acc, g):
    lo = _half_mask((2 * CHUNK, 128), 0)
    t = jnp.where(lo, acc[:2 * CHUNK], 0.0) + jnp.where(lo, 0.0, acc[2 * CHUNK:])
    return jnp.where(_half_mask((2 * CHUNK, 128), g), t + pltpu.roll(t, 64, 1), 0.0)


def _b_kv_specs(S):
    prev = lambda n: (jnp.maximum(n - 1, 0), 0)
    cur = lambda n: (n, 0)
    v_col = (2 * B_WIDTH + B_KV_WIDTH) // 128
    return [pl.BlockSpec((CHUNK, 128), prev), pl.BlockSpec((CHUNK, 128), cur),
            pl.BlockSpec((CHUNK, 128), lambda n: (jnp.maximum(n - 1, 0), v_col)), pl.BlockSpec((CHUNK, 128), lambda n: (n, v_col))]


def _b_attn_fwd(qn, kn, proj, sinks):
    S = qn.shape[0]

    def body(s_ref, q_ref, kp_ref, kc_ref, vp_ref, vc_ref, y_ref):
        n = pl.program_id(0)
        valid = _b_valid(n)
        k2 = jnp.concatenate([kp_ref[...], kc_ref[...]], axis=0).astype(F32)
        v2 = jnp.concatenate([vp_ref[...], vc_ref[...]], axis=0)
        for g in range(B_KV_HEADS):
            pairs = [g * PAIRS_PER_GROUP + pp for pp in range(PAIRS_PER_GROUP)]
            qs = jnp.concatenate([q_ref[:, p * 128:(p + 1) * 128] for p in pairs], axis=0)
            probs = _b_probs(qs, _b_blocks(k2, g), valid, [_b_sink_col(s_ref, g, hf) for hf in range(2)])
            pcat = jnp.concatenate([probs[0][0].astype(BF), probs[1][0].astype(BF)], axis=1)
            o = jnp.dot(pcat, _b_blocks(v2, g), preferred_element_type=F32)
            for pp, p in enumerate(pairs):
                y_ref[:, p * 128:(p + 1) * 128] = o[pp * CHUNK:(pp + 1) * CHUNK].astype(BF)

    return pl.pallas_call(
        body, name="b_attn_fwd", grid=(S // CHUNK,),
        in_specs=[pl.BlockSpec(memory_space=pltpu.SMEM), pl.BlockSpec((CHUNK, B_WIDTH), lambda n: (n, 0))] + _b_kv_specs(S),
        out_specs=pl.BlockSpec((CHUNK, B_WIDTH), lambda n: (n, 0)),
        out_shape=jax.ShapeDtypeStruct((S, B_WIDTH), BF), compiler_params=_params(("arbitrary",)),
    )(sinks, qn, kn, kn, proj, proj)


def _b_attn_bwd(qn, kn, proj, sinks, dy, after=()):
    S = qn.shape[0]

    def body(s_ref, q_ref, kp_ref, kc_ref, vp_ref, vc_ref, dy_ref, dq_ref, dk_ref, dv_ref, ds_ref):
        n = pl.program_id(0)

        @pl.when(n == 0)
        def _():
            dk_ref[...] = jnp.zeros_like(dk_ref)
            dv_ref[...] = jnp.zeros_like(dv_ref)
            ds_ref[...] = jnp.zeros_like(ds_ref)

        valid = _b_valid(n)
        k2 = jnp.concatenate([kp_ref[...], kc_ref[...]], axis=0).astype(F32)
        v2 = jnp.concatenate([vp_ref[...], vc_ref[...]], axis=0)
        lane = lax.broadcasted_iota(jnp.int32, (CHUNK, 128), 1)
        dk2 = jnp.zeros((2 * CHUNK, 128), F32)
        dv2 = jnp.zeros((2 * CHUNK, 128), F32)
        dsink = jnp.zeros((CHUNK, 128), F32)
        scale = B_HEAD_DIM ** -0.5
        nt = (((1,), (1,)), ((), ()))
        tn = (((0,), (0,)), ((), ()))
        for g in range(B_KV_HEADS):
            pairs = [g * PAIRS_PER_GROUP + pp for pp in range(PAIRS_PER_GROUP)]
            qs = jnp.concatenate([q_ref[:, p * 128:(p + 1) * 128] for p in pairs], axis=0)
            do = jnp.concatenate([dy_ref[:, p * 128:(p + 1) * 128] for p in pairs], axis=0)
            do_b = do.astype(BF)
            kblk, vblk = _b_blocks(k2, g), _b_blocks(v2, g)
            probs = _b_probs(qs, kblk, valid, [_b_sink_col(s_ref, g, hf) for hf in range(2)])
            pcat = jnp.concatenate([probs[0][0].astype(BF), probs[1][0].astype(BF)], axis=1)
            o = jnp.dot(pcat, vblk, preferred_element_type=F32)
            dp = lax.dot_general(do_b, vblk, nt, preferred_element_type=F32)
            prod = do * o
            ds_halves = []
            for hf in range(2):
                pr, ps = probs[hf]
                delta = jnp.sum(jnp.where(_half_mask(prod.shape, hf), prod, 0.0), axis=-1, keepdims=True)
                ds_halves.append((pr * (dp[:, hf * 2 * CHUNK:(hf + 1) * 2 * CHUNK] - delta) * scale).astype(BF))
                t = -ps * delta
                for pp, p in enumerate(pairs):
                    dsink = dsink + jnp.where(lane == 2 * p + hf, t[pp * CHUNK:(pp + 1) * CHUNK], 0.0)
            dsc = jnp.concatenate(ds_halves, axis=1)
            dq = jnp.dot(dsc, kblk, preferred_element_type=F32)
            for pp, p in enumerate(pairs):
                dq_ref[:, p * 128:(p + 1) * 128] = dq[pp * CHUNK:(pp + 1) * CHUNK]
            dk2 = dk2 + _b_fold(lax.dot_general(dsc, qs, tn, preferred_element_type=F32), g)
            dv2 = dv2 + _b_fold(lax.dot_general(pcat, do_b, tn, preferred_element_type=F32), g)
        ds_ref[...] += dsink
        cur = pl.ds(pl.multiple_of(n * CHUNK, CHUNK), CHUNK)
        dk_ref[cur, :] += dk2[CHUNK:]
        dv_ref[cur, :] += dv2[CHUNK:]

        @pl.when(n > 0)
        def _():
            prv = pl.ds(pl.multiple_of((n - 1) * CHUNK, CHUNK), CHUNK)
            dk_ref[prv, :] += dk2[:CHUNK]
            dv_ref[prv, :] += dv2[:CHUNK]

    full = pl.BlockSpec((S, 128), lambda n: (0, 0))
    return pl.pallas_call(
        _hide(body, 7, len(after)), name="b_attn_bwd", grid=(S // CHUNK,),
        in_specs=[pl.BlockSpec(memory_space=pltpu.SMEM), pl.BlockSpec((CHUNK, B_WIDTH), lambda n: (n, 0))] + _b_kv_specs(S)
        + [pl.BlockSpec((CHUNK, B_WIDTH), lambda n: (n, 0))] + _hidden_specs(after),
        out_specs=[pl.BlockSpec((CHUNK, B_WIDTH), lambda n: (n, 0)), full, full, pl.BlockSpec((CHUNK, 128), lambda n: (0, 0))],
        out_shape=[jax.ShapeDtypeStruct((S, B_WIDTH), F32), jax.ShapeDtypeStruct((S, 128), F32), jax.ShapeDtypeStruct((S, 128), F32),
                   jax.ShapeDtypeStruct((CHUNK, 128), F32)],
        compiler_params=_params(("arbitrary",)),
    )(sinks, qn, kn, kn, proj, proj, dy, *after)


def _c_block(q, k, v, gq, gk):
    qn = q * lax.rsqrt(jnp.mean(q * q, axis=-1, keepdims=True) + EPS) * gq
    kn = k * lax.rsqrt(jnp.mean(k * k, axis=-1, keepdims=True) + EPS) * gk
    s = lax.dot_general(qn.astype(BF), kn.astype(BF), (((1,), (1,)), ((), ())), preferred_element_type=F32) * (C_HEAD_DIM ** -0.5)
    p = jax.nn.softmax(s, axis=-1)
    return jnp.dot(p.astype(BF), v.astype(BF), preferred_element_type=F32)


def _c_specs(S, M, tq):
    q_col = (2 * A_WIDTH + B_WIDTH + 2 * B_KV_WIDTH) // 128
    return [pl.BlockSpec((tq, 128), lambda h, i: (i, q_col + h)), pl.BlockSpec((M, 128), lambda h, i: (0, h)),
            pl.BlockSpec((M, 128), lambda h, i: (0, C_HEADS + h)), pl.BlockSpec((1, 128), lambda h, i: (0, 0)),
            pl.BlockSpec((1, 128), lambda h, i: (0, 0))]


def _c_fwd(proj, kv, gq, gk):
    S, M = proj.shape[0], kv.shape[0]
    tq = _pick(S, (512,))

    def body(q_ref, k_ref, v_ref, gq_ref, gk_ref, y_ref):
        y_ref[...] = _c_block(q_ref[...], k_ref[...], v_ref[...], gq_ref[...], gk_ref[...]).astype(BF)

    return pl.pallas_call(
        body, name="c_fwd", grid=(C_HEADS, S // tq), in_specs=_c_specs(S, M, tq),
        out_specs=pl.BlockSpec((tq, 128), lambda h, i: (i, h)),
        out_shape=jax.ShapeDtypeStruct((S, C_WIDTH), BF), compiler_params=_params(("parallel", "parallel")),
    )(proj, kv, kv, gq, gk)


def _c_bwd(proj, kv, gq, gk, dy):
    S, M = proj.shape[0], kv.shape[0]
    tq = _pick(S, (512,))

    def body(q_ref, k_ref, v_ref, gq_ref, gk_ref, dy_ref, dq_ref, dk_ref, dv_ref, dgq_ref, dgk_ref):
        i = pl.program_id(1)
        _, vjp = jax.vjp(_c_block, q_ref[...], k_ref[...], v_ref[...], gq_ref[...], gk_ref[...])
        dq, dk, dv, dgq, dgk = vjp(dy_ref[...])
        dq_ref[...] = dq.astype(BF)

        @pl.when(i == 0)
        def _():
            dk_ref[...] = dk
            dv_ref[...] = dv
            dgq_ref[...] = dgq
            dgk_ref[...] = dgk

        @pl.when(i > 0)
        def _():
            dk_ref[...] += dk
            dv_ref[...] += dv
            dgq_ref[...] += dgq
            dgk_ref[...] += dgk

    return pl.pallas_call(
        body, name="c_bwd", grid=(C_HEADS, S // tq),
        in_specs=_c_specs(S, M, tq) + [pl.BlockSpec((tq, 128), lambda h, i: (i, h))],
        out_specs=[pl.BlockSpec((tq, 128), lambda h, i: (i, h)), pl.BlockSpec((M, 128), lambda h, i: (0, h)),
                   pl.BlockSpec((M, 128), lambda h, i: (0, h)), pl.BlockSpec((None, 1, 128), lambda h, i: (h, 0, 0)),
                   pl.BlockSpec((None, 1, 128), lambda h, i: (h, 0, 0))],
        out_shape=[jax.ShapeDtypeStruct((S, C_WIDTH), BF), jax.ShapeDtypeStruct((M, C_WIDTH), F32), jax.ShapeDtypeStruct((M, C_WIDTH), F32),
                   jax.ShapeDtypeStruct((C_HEADS, 1, 128), F32), jax.ShapeDtypeStruct((C_HEADS, 1, 128), F32)],
        compiler_params=_params(("parallel", "arbitrary")),
    )(proj, kv, kv, gq, gk, dy)


def _merge_specs(S, D, tr, tc):
    off = GATE_OFF // tc
    nd = D // tc
    gates = [pl.BlockSpec((tr, tc), functools.partial(lambda b, i, j: (i, off + b * nd + j), b)) for b in range(3)]
    zs = [pl.BlockSpec((tr, tc), lambda i, j: (i, j)) for _ in range(3)]
    return gates + zs


def _merge_fwd(proj, za, zb, zc):
    S, D = za.shape
    tr, tc = _pick(S, (512,)), _pick(D, (256,))

    def body(ga_ref, gb_ref, gc_ref, za_ref, zb_ref, zc_ref, m_ref):
        acc = jax.nn.sigmoid(ga_ref[...]) * za_ref[...].astype(F32)
        acc = acc + jax.nn.sigmoid(gb_ref[...]) * zb_ref[...].astype(F32)
        acc = acc + jax.nn.sigmoid(gc_ref[...]) * zc_ref[...].astype(F32)
        m_ref[...] = acc.astype(BF)

    return pl.pallas_call(
        body, name="merge_fwd", grid=(S // tr, D // tc), in_specs=_merge_specs(S, D, tr, tc),
        out_specs=pl.BlockSpec((tr, tc), lambda i, j: (i, j)), out_shape=jax.ShapeDtypeStruct((S, D), BF),
        compiler_params=_params(("parallel", "parallel")),
    )(proj, proj, proj, za, zb, zc)


def _merge_bwd(proj, za, zb, zc, dm, after=()):
    S, D = za.shape
    tr, tc = _pick(S, (512,)), _pick(D, (256,))
    nd = D // tc

    def body(ga_ref, gb_ref, gc_ref, za_ref, zb_ref, zc_ref, dm_ref, dza_ref, dzb_ref, dzc_ref, dga_ref, dgb_ref, dgc_ref):
        dmv = dm_ref[...]
        for g_ref, z_ref, dz_ref, dg_ref in ((ga_ref, za_ref, dza_ref, dga_ref), (gb_ref, zb_ref, dzb_ref, dgb_ref),
                                             (gc_ref, zc_ref, dzc_ref, dgc_ref)):
            sg = jax.nn.sigmoid(g_ref[...])
            dz_ref[...] = (sg * dmv).astype(BF)
            dg_ref[...] = (dmv * z_ref[...].astype(F32) * sg * (1.0 - sg)).astype(BF)

    tile = pl.BlockSpec((tr, tc), lambda i, j: (i, j))
    return pl.pallas_call(
        _hide(body, 7, len(after)), name="merge_bwd", grid=(S // tr, D // tc),
        in_specs=_merge_specs(S, D, tr, tc) + [tile] + _hidden_specs(after),
        out_specs=[tile, tile, tile, tile, tile, tile],
        out_shape=[jax.ShapeDtypeStruct((S, D), BF)] * 6,
        compiler_params=_params(("parallel", "parallel")),
    )(proj, proj, proj, za, zb, zc, dm, *after)


def _shift_down(u, k):
    t = lax.broadcasted_iota(jnp.int32, u.shape, 0)
    return jnp.where(t >= k, pltpu.roll(u, k, 0), 0.0)


def _shift_up(u, k):
    n = u.shape[0]
    t = lax.broadcasted_iota(jnp.int32, u.shape, 0)
    return jnp.where(t < n - k, pltpu.roll(u, n - k, 0), 0.0)


def _conv3(u, w, b):
    return u * w[2:3] + _shift_down(u, 1) * w[1:2] + _shift_down(u, 2) * w[0:1] + b


def _ffn_specs(S, F, tc, c):
    per = c // tc

    def w_spec(half):
        return pl.BlockSpec((None, 3, tc), lambda j: (half * (N_DEV // 2) + j // per, 0, j % per))

    return [pl.BlockSpec((2, S, tc), lambda j: (0, 0, j)), w_spec(0), w_spec(1), pl.BlockSpec((2, 1, tc), lambda j: (0, 0, j))]


def _ffn_tile(F, c):
    tc = 128
    if c % tc or F % tc:
        raise ValueError(f"ffn tile {tc} does not divide {c}, {F}")
    return tc


def _ffn_act_fwd(up3, cws, cb3):
    _, S, F = up3.shape
    c = cws.shape[2]
    tc = _ffn_tile(F, c)

    def body(u_ref, wa_ref, wb_ref, b_ref, o_ref):
        ca = _conv3(u_ref[0], wa_ref[...], b_ref[0])
        cb = _conv3(u_ref[1], wb_ref[...], b_ref[1])
        o_ref[...] = (ca * jax.nn.sigmoid(ca) * cb).astype(BF)

    return pl.pallas_call(
        body, name="ffn_act_fwd", grid=(F // tc,), in_specs=_ffn_specs(S, F, tc, c),
        out_specs=pl.BlockSpec((S, tc), lambda j: (0, j)), out_shape=jax.ShapeDtypeStruct((S, F), BF),
        compiler_params=_params(("parallel",)),
    )(up3, cws, cws, cb3)


def _ffn_act_bwd(up3, cws, cb3, dact, after=()):
    _, S, F = up3.shape
    c = cws.shape[2]
    tc = _ffn_tile(F, c)

    def body(u_ref, wa_ref, wb_ref, b_ref, da_ref, du_ref, dw_ref, db_ref):
        w_refs = (wa_ref, wb_ref)
        ca = _conv3(u_ref[0], wa_ref[...], b_ref[0])
        cb = _conv3(u_ref[1], wb_ref[...], b_ref[1])
        sg = jax.nn.sigmoid(ca)
        dav = da_ref[...]
        dcs = (dav * cb * sg * (1.0 + ca * (1.0 - sg)), dav * ca * sg)
        for part in range(2):
            dc, w, u = dcs[part], w_refs[part][...], u_ref[part]
            du_ref[part] = (dc * w[2:3] + _shift_up(dc, 1) * w[1:2] + _shift_up(dc, 2) * w[0:1]).astype(BF)
            dw_ref[part, 2:3, :] = jnp.sum(dc * u, axis=0, keepdims=True)
            dw_ref[part, 1:2, :] = jnp.sum(dc * _shift_down(u, 1), axis=0, keepdims=True)
            dw_ref[part, 0:1, :] = jnp.sum(dc * _shift_down(u, 2), axis=0, keepdims=True)
            db_ref[part] = jnp.sum(dc, axis=0, keepdims=True)

    return pl.pallas_call(
        _hide(body, 5, len(after)), name="ffn_act_bwd", grid=(F // tc,),
        in_specs=_ffn_specs(S, F, tc, c) + [pl.BlockSpec((S, tc), lambda j: (0, j))] + _hidden_specs(after),
        out_specs=[pl.BlockSpec((2, S, tc), lambda j: (0, 0, j)), pl.BlockSpec((2, 3, tc), lambda j: (0, 0, j)),
                   pl.BlockSpec((2, 1, tc), lambda j: (0, 0, j))],
        out_shape=[jax.ShapeDtypeStruct((2, S, F), BF), jax.ShapeDtypeStruct((2, 3, F), F32), jax.ShapeDtypeStruct((2, 1, F), F32)],
        compiler_params=_params(("parallel",)),
    )(up3, cws, cws, cb3, dact, *after)


def _loss(y, target):
    S, D = y.shape
    tr = _pick(S, (256,))

    def body(y_ref, t_ref, dy_ref, dyb_ref, l_ref):
        i = pl.program_id(0)
        e = y_ref[...] - t_ref[...]
        dy = e * (1.0 / D)
        dy_ref[...] = dy
        dyb_ref[...] = dy.astype(BF)
        part = jnp.sum(jnp.sum(e * e, axis=-1, keepdims=True), axis=0, keepdims=True) * (0.5 / D)

        @pl.when(i == 0)
        def _():
            l_ref[...] = jnp.zeros_like(l_ref)

        l_ref[...] += part

    row = pl.BlockSpec((tr, D), lambda i: (i, 0))
    return pl.pallas_call(
        body, name="loss", grid=(S // tr,), in_specs=[row, row],
        out_specs=[row, row, pl.BlockSpec((8, 128), lambda i: (0, 0))],
        out_shape=[jax.ShapeDtypeStruct((S, D), F32), jax.ShapeDtypeStruct((S, D), BF), jax.ShapeDtypeStruct((8, 128), F32)],
        compiler_params=_params(("arbitrary",)),
    )(y, target)


ANY = pl.BlockSpec(memory_space=pl.ANY)


def _allgather(shards, name):
    n = len(shards)

    def body(*refs):
        ins, outs = refs[:n], refs[n:2 * n]
        send_sems, recv_sems, local_sems = refs[2 * n:]
        x, y, c = lax.axis_index("x"), lax.axis_index("y"), lax.axis_index("c")
        me, sibling = (x, y, c), (x, y, 1 - c)
        chips = [(1 - x, y), (x, 1 - y), (1 - x, 1 - y)]

        def blk(w, px, py, pc):
            return outs[w].at[4 * px + 2 * py + pc]

        def copy(w, k, block, to, src=None):
            return pltpu.make_async_remote_copy(
                src_ref=blk(w, *block) if src is None else src, dst_ref=blk(w, *block),
                send_sem=send_sems.at[w, k], recv_sem=recv_sems.at[w, k], device_id=to, device_id_type=MESH)

        started = []
        mine = []
        for w in range(n):
            mine.append(pltpu.make_async_copy(ins[w], blk(w, *me), local_sems.at[w]))
            mine[-1].start()
            first = [copy(w, 0, me, sibling, src=ins[w])]
            first += [copy(w, 1 + j, me, (*chip, c), src=ins[w]) for j, chip in enumerate(chips)]
            for cp in first:
                cp.start()
            started += first
        for w in range(n):
            for j, chip in enumerate(chips):
                copy(w, 1 + j, (*chip, c), me).wait_recv()
                fwd = copy(w, 4 + j, (*chip, c), sibling)
                fwd.start()
                started.append(fwd)
        for w in range(n):
            copy(w, 0, sibling, me).wait_recv()
            for j, chip in enumerate(chips):
                copy(w, 4 + j, (*chip, 1 - c), me).wait_recv()
        for cp in started:
            cp.wait_send()
        for cp in mine:
            cp.wait()

    whole = pl.BlockSpec(memory_space=pltpu.VMEM)
    outs = pl.pallas_call(
        body, name=name, in_specs=[whole] * n, out_specs=[whole] * n,
        out_shape=[jax.ShapeDtypeStruct((N_DEV,) + s.shape, s.dtype) for s in shards],
        scratch_shapes=[pltpu.SemaphoreType.DMA((n, 7)), pltpu.SemaphoreType.DMA((n, 7)), pltpu.SemaphoreType.DMA((n,))],
    )(*shards)
    return list(outs)


def _allgather_seq(shards, name, collective_id, after=()):
    n = len(shards)
    n_after = len(after)

    def body(*refs):
        ins, outs = refs[:n], refs[n + n_after:2 * n + n_after]
        send_sems, recv_sems, local_sems = refs[2 * n + n_after:]
        x, y, c = lax.axis_index("x"), lax.axis_index("y"), lax.axis_index("c")
        me, sibling = (x, y, c), (x, y, 1 - c)
        chips = [(1 - x, y), (x, 1 - y), (1 - x, 1 - y)]
        barrier = pltpu.get_barrier_semaphore()
        for peer in [sibling] + [(*chip, c) for chip in chips]:
            pl.semaphore_signal(barrier, inc=1, device_id=peer, device_id_type=MESH)
        pl.semaphore_wait(barrier, 4)

        def blk(w, px, py, pc):
            return outs[w].at[4 * px + 2 * py + pc]

        def copy(w, k, block, to, src=None):
            return pltpu.make_async_remote_copy(
                src_ref=blk(w, *block) if src is None else src, dst_ref=blk(w, *block),
                send_sem=send_sems.at[7 * w + k], recv_sem=recv_sems.at[7 * w + k], device_id=to, device_id_type=MESH)

        started = []
        mine = []
        for w in range(n):
            mine.append(pltpu.make_async_copy(ins[w], blk(w, *me), local_sems.at[w]))
            mine[-1].start()
            first = [copy(w, 0, me, sibling, src=ins[w])]
            first += [copy(w, 1 + j, me, (*chip, c), src=ins[w]) for j, chip in enumerate(chips)]
            for cp in first:
                cp.start()
            started += first
        for w in range(n):
            for j, chip in enumerate(chips):
                copy(w, 1 + j, (*chip, c), me).wait_recv()
                fwd = copy(w, 4 + j, (*chip, c), sibling)
                fwd.start()
                started.append(fwd)
        for w in range(n):
            copy(w, 0, sibling, me).wait_recv()
            for j, chip in enumerate(chips):
                copy(w, 4 + j, (*chip, 1 - c), me).wait_recv()
        for cp in started:
            cp.wait_send()
        for cp in mine:
            cp.wait()

    outs = pl.kernel(
        body, name=name, out_type=[jax.ShapeDtypeStruct((N_DEV,) + s.shape, s.dtype) for s in shards],
        mesh=plsc.ScalarSubcoreMesh(axis_name="seq", num_cores=1),
        scratch_types=[pltpu.SemaphoreType.DMA((7 * n,)), pltpu.SemaphoreType.DMA((7 * n,)), pltpu.SemaphoreType.DMA((n,))],
        compiler_params=pltpu.CompilerParams(collective_id=collective_id),
    )(*shards, *after)
    return list(outs)


def _sibling_exchange(grads, name):
    n = len(grads)

    def body(*refs):
        ins, outs = refs[:n], refs[n:2 * n]
        send_sems, recv_sems = refs[2 * n:]
        x, y, c = lax.axis_index("x"), lax.axis_index("y"), lax.axis_index("c")
        copies = [pltpu.make_async_remote_copy(
            src_ref=ins[w].at[:, 1 - c], dst_ref=outs[w], send_sem=send_sems.at[w], recv_sem=recv_sems.at[w],
            device_id=(x, y, 1 - c), device_id_type=MESH) for w in range(n)]
        for cp in copies:
            cp.start()
        for cp in copies:
            cp.wait()

    outs = pl.pallas_call(
        body, name=name, in_specs=[ANY] * n, out_specs=[ANY] * n,
        out_shape=[jax.ShapeDtypeStruct((g.shape[0],) + g.shape[2:], g.dtype) for g in grads],
        scratch_shapes=[pltpu.SemaphoreType.DMA((n,)), pltpu.SemaphoreType.DMA((n,))],
    )(*grads)
    return list(outs)


def _chip_exchange(sums, name, collective_id):
    n = len(sums)

    def body(*refs):
        ins, outs = refs[:n], refs[n:2 * n]
        send_sems, recv_sems = refs[2 * n:]
        x, y, c = lax.axis_index("x"), lax.axis_index("y"), lax.axis_index("c")
        chips = [(1 - x, y), (x, 1 - y), (1 - x, 1 - y)]
        barrier = pltpu.get_barrier_semaphore()
        for px, py in chips:
            pl.semaphore_signal(barrier, inc=1, device_id=(px, py, c), device_id_type=MESH)
        pl.semaphore_wait(barrier, 3)
        copies = []
        for w in range(n):
            for k, (px, py) in enumerate(chips):
                copies.append(pltpu.make_async_remote_copy(
                    src_ref=ins[w].at[2 * px + py], dst_ref=outs[w].at[k], send_sem=send_sems.at[3 * w + k],
                    recv_sem=recv_sems.at[3 * w + k], device_id=(px, py, c), device_id_type=MESH))
        for cp in copies:
            cp.start()
        for cp in copies:
            cp.wait()

    outs = pl.kernel(
        body, name=name, out_type=[jax.ShapeDtypeStruct((3,) + s.shape[1:], s.dtype) for s in sums],
        mesh=plsc.ScalarSubcoreMesh(axis_name="seq", num_cores=1),
        scratch_types=[pltpu.SemaphoreType.DMA((3 * n,)), pltpu.SemaphoreType.DMA((3 * n,))],
        compiler_params=pltpu.CompilerParams(collective_id=collective_id),
    )(*sums)
    return list(outs)


def _row_tile(r, c, elems=256 * 1024):
    want = max(8, elems // c)
    for t in range(min(want, r) // 8 * 8, 0, -8):
        if r % t == 0:
            return t
    return r


def _pair_add(g4, recv, core, name, after=()):
    _, _, r, c = g4.shape
    tr = _row_tile(r, c, 1024 * 1024)

    def body(core_ref, a_ref, b_ref, o_ref):
        o_ref[...] = (a_ref[...].astype(F32) + b_ref[...].astype(F32)).astype(BF)

    return pl.pallas_call(
        _hide(body, 3, len(after)), name=name,
        grid_spec=pltpu.PrefetchScalarGridSpec(
            num_scalar_prefetch=1, grid=(4, r // tr),
            in_specs=[pl.BlockSpec((None, None, tr, c), lambda p, i, s: (p, s[0], i, 0)),
                      pl.BlockSpec((None, tr, c), lambda p, i, s: (p, i, 0))] + _hidden_specs(after),
            out_specs=pl.BlockSpec((None, tr, c), lambda p, i, s: (p, i, 0))),
        out_shape=jax.ShapeDtypeStruct((4, r, c), BF), compiler_params=_params(("parallel", "parallel")),
    )(core, g4, recv, *after)


def _adam_math(w, g, m, v):
    m = ADAM_B1 * m + (1.0 - ADAM_B1) * g
    v = ADAM_B2 * v + (1.0 - ADAM_B2) * (g * g)
    m_hat = m / (1.0 - ADAM_B1 ** ADAM_STEP)
    v_hat = v / (1.0 - ADAM_B2 ** ADAM_STEP)
    delta = -ADAM_LR * (m_hat / (jnp.sqrt(v_hat) + ADAM_EPS) + ADAM_WD * w)
    return delta, m, v


def _adamw_big(sums, recv, chip, w, m, v, name, after=()):
    r, c = w.shape
    tr = _row_tile(r, c, 256 * 1024)

    def body(chip_ref, s_ref, r_ref, w_ref, m_ref, v_ref, g_out, d_out, m_out, v_out):
        g = s_ref[...].astype(F32) + r_ref[0].astype(F32)
        g = g + r_ref[1].astype(F32)
        g = g + r_ref[2].astype(F32)
        delta, mn, vn = _adam_math(w_ref[...], g, m_ref[...], v_ref[...])
        g_out[...] = g
        d_out[...] = delta
        m_out[...] = mn
        v_out[...] = vn

    row = pl.BlockSpec((tr, c), lambda i, s: (i, 0))
    return pl.pallas_call(
        _hide(body, 6, len(after)), name=name,
        grid_spec=pltpu.PrefetchScalarGridSpec(
            num_scalar_prefetch=1, grid=(r // tr,),
            in_specs=[pl.BlockSpec((None, tr, c), lambda i, s: (s[0], i, 0)), pl.BlockSpec((3, tr, c), lambda i, s: (0, i, 0)),
                      row, row, row] + _hidden_specs(after),
            out_specs=[row, row, row, row]),
        out_shape=[jax.ShapeDtypeStruct((r, c), F32)] * 4, compiler_params=_params(("parallel",)),
    )(chip, sums, recv, w, m, v, *after)


def _adamw_small(parts, ws, ms, vs, extra_parts, name):
    n, ne = len(ws), len(extra_parts)

    def total(p_ref):
        g = p_ref[0]
        for d in range(1, N_DEV):
            g = g + p_ref[d]
        return g

    def body(*refs):
        p_refs, w_refs, m_refs, v_refs = refs[:n], refs[n:2 * n], refs[2 * n:3 * n], refs[3 * n:4 * n]
        e_refs = refs[4 * n:4 * n + ne]
        outs = refs[4 * n + ne:]
        for i in range(n):
            g = total(p_refs[i])
            delta, mn, vn = _adam_math(w_refs[i][...], g, m_refs[i][...], v_refs[i][...])
            outs[4 * i][...] = g
            outs[4 * i + 1][...] = delta
            outs[4 * i + 2][...] = mn
            outs[4 * i + 3][...] = vn
        for i in range(ne):
            outs[4 * n + i][...] = total(e_refs[i])

    out_shape = []
    for w in ws:
        out_shape += [jax.ShapeDtypeStruct(w.shape, F32)] * 4
    out_shape += [jax.ShapeDtypeStruct(e.shape[1:], F32) for e in extra_parts]
    res = pl.pallas_call(body, name=name, out_shape=out_shape,
                         compiler_params=pltpu.CompilerParams(vmem_limit_bytes=VMEM_LIMIT))(*parts, *ws, *ms, *vs, *extra_parts)
    return [res[4 * i:4 * i + 4] for i in range(n)], list(res[4 * n:])


def _adamw_plain(g, w, m, v, name):
    def body(g_ref, w_ref, m_ref, v_ref, d_out, m_out, v_out):
        delta, mn, vn = _adam_math(w_ref[...], g_ref[...], m_ref[...], v_ref[...])
        d_out[...] = delta
        m_out[...] = mn
        v_out[...] = vn

    return pl.pallas_call(body, name=name, out_shape=[jax.ShapeDtypeStruct(w.shape, F32)] * 3)(g, w, m, v)


def kernel(x, mem, positions, g_mix, w_in, g_a_v, w_spatial, b_spatial, g_b_q, g_b_k, sinks, g_mem, w_mem_kv, g_c_q, g_c_k, w_branch_a, w_branch_b, w_branch_c, w_out, g_ffn, w_up, conv_w, conv_b, w_down, loss_target, m_g_mix, m_w_in, m_g_a_v, m_w_spatial, m_b_spatial, m_g_b_q, m_g_b_k, m_sinks, m_g_mem, m_w_mem_kv, m_g_c_q, m_g_c_k, m_w_branch_a, m_w_branch_b, m_w_branch_c, m_w_out, m_g_ffn, m_w_up, m_conv_w, m_conv_b, m_w_down, v_g_mix, v_w_in, v_g_a_v, v_w_spatial, v_b_spatial, v_g_b_q, v_g_b_k, v_sinks, v_g_mem, v_w_mem_kv, v_g_c_q, v_g_c_k, v_w_branch_a, v_w_branch_b, v_w_branch_c, v_w_out, v_g_ffn, v_w_up, v_conv_w, v_conv_b, v_w_down):
    S, D = x.shape[1], x.shape[2]
    M = mem.shape[1]
    F = w_down.shape[1] * N_DEV
    in_cols = w_in.shape[2] * N_DEV
    ax, ay, ac = lax.axis_index("x"), lax.axis_index("y"), lax.axis_index("c")
    core = jnp.reshape(ac, (1,)).astype(jnp.int32)
    chip = jnp.reshape(2 * ax + ay, (1,)).astype(jnp.int32)
    me = 4 * ax + 2 * ay + ac

    x2, mem2, tgt2 = x[0], mem[0], loss_target[0]

    big = dict(w_in=w_in[0].T, w_mem_kv=w_mem_kv[0], w_branch_a=w_branch_a[0], w_branch_b=w_branch_b[0],
               w_branch_c=w_branch_c[0], w_out=w_out[0], w_up=w_up[0], w_down=w_down[0])
    names = list(big)
    cast = {k: big[k].astype(BF) for k in names}
    W = {}
    cb3 = conv_b.reshape(2, 1, F)
    W["w_in"], = _allgather_seq([cast["w_in"]], "ag_seq0", 0)
    w_in_t = W["w_in"].reshape(in_cols, D)
    grp1 = ["w_mem_kv", "w_branch_a", "w_branch_b", "w_branch_c", "w_out"]
    res1 = _allgather_seq([cast[k] for k in grp1] + [conv_w[0]], "ag_seq1", 1, after=(_token((w_in_t,), "tok_w_in"),))
    W.update(zip(grp1, res1))
    cw3 = res1[-1]
    w_kv_f = W["w_mem_kv"].reshape(D, 2 * C_WIDTH)
    w_out_f = W["w_out"].reshape(D, D)

    half = ROPE_DIM // 2
    inv = ROPE_THETA ** (-jnp.arange(half, dtype=F32) / half)
    ang = positions[0].astype(F32)[:, None] * inv
    cos, sin = jnp.cos(ang), jnp.sin(ang)
    one, zero = jnp.ones((S, B_HEAD_DIM - ROPE_DIM), F32), jnp.zeros((S, B_HEAD_DIM - ROPE_DIM), F32)
    z8 = jnp.zeros((S, half), F32)
    ct = jnp.tile(jnp.concatenate([cos, cos, one], axis=1), (1, 2))
    sa = jnp.tile(jnp.concatenate([-sin, z8, zero], axis=1), (1, 2))
    sb = jnp.tile(jnp.concatenate([z8, sin, zero], axis=1), (1, 2))
    gq2, gk2 = jnp.tile(g_b_q, (1, 2)), jnp.tile(g_b_k, (1, 2))
    b_t = b_spatial[0].T

    h, rstd1 = _rms_fwd(x2, g_mix, "rms1_fwd")
    proj = _mm(h, w_in_t, "nt", F32, "mm_proj", tn=1280)
    y_a = _a_fwd(proj, g_a_v, w_spatial[0], b_t)
    qn, kn = _b_pre(proj, gq2, gk2, ct, sa, sb)
    W["w_up"], = _allgather_seq([cast["w_up"]], "ag_seq2", 2, after=(_token((W["w_out"], qn), "tok_group1"),))
    y_b = _b_attn_fwd(qn, kn, proj, sinks)
    mem_h, rstd_m = _rms_fwd(mem2, g_mem, "rmsmem_fwd")
    kv = _mm(mem_h, w_kv_f, "nn", F32, "mm_kv", after=(y_b,))
    y_c = _c_fwd(proj, kv, g_c_q, g_c_k)
    z_a = _mm(y_a, W["w_branch_a"], "nn", BF, "mm_za", b_stack=True, after=(y_b,))
    z_b = _mm(y_b, W["w_branch_b"], "nn", BF, "mm_zb", b_stack=True)
    z_c = _mm(y_c, W["w_branch_c"], "nn", BF, "mm_zc", b_stack=True)
    merged = _merge_fwd(proj, z_a, z_b, z_c)
    x1 = _mm(merged, w_out_f, "nn", F32, "mm_x1", resid=x2)
    h2, rstd2 = _rms_fwd(x1, g_ffn, "rms2_fwd")
    W["w_down"], = _allgather_seq([cast["w_down"]], "ag_seq3", 3, after=(W["w_up"], h2))
    w_down_f = W["w_down"].reshape(F, D)
    up3 = _mm(h2, W["w_up"], "nn", F32, "mm_up", b_stack=True, out_parts=2)
    act = _ffn_act_fwd(up3, cw3, cb3)
    y = _mm(act, w_down_f, "nn", F32, "mm_y", resid=x1, tk=1408)
    dy, dy_b, loss_acc = _loss(y, tgt2)
    loss = lax.psum(loss_acc[0, 0], ("x", "y", "c"))

    reduced = {}

    def reduce_group(gi, keys, stacked, add_after):
        g4 = [g.reshape(4, 2, g.shape[1], g.shape[2]) for g in stacked]
        from_sibling = _sibling_exchange(g4, f"rs_sib{gi}")
        sums = [_pair_add(a, b, core, "rs_add_" + k, after=add_after) for k, a, b in zip(keys, g4, from_sibling)]
        from_chips = _chip_exchange(sums, f"rs_chip{gi}", 4 + gi)
        reduced.update(zip(keys, zip(sums, from_chips)))
        return tuple(sums)

    d_act = _mm(dy_b, w_down_f, "nt", F32, "mm_dact", tn=1408)
    g_down = _mm(act, dy_b, "tn", BF, "mm_gdown", tm=1408)
    d_up3, d_cw3, d_cb3 = _ffn_act_bwd(up3, cw3, cb3, d_act, after=(g_down,))
    sums0 = reduce_group(0, ["w_down"], [g_down.reshape(N_DEV, F // N_DEV, D)], (d_up3,))
    d_h2 = _mm(d_up3, W["w_up"], "nt", F32, "mm_dh2", a_parts=2, b_stack=True, after=sums0)
    g_up = _mm(h2, d_up3, "tn", BF, "mm_gup", b_parts=2, out_stack=True)
    dx1, dx1_b, d_g_ffn = _rms_bwd(x1, rstd2, g_ffn, d_h2, dy, "rms2_bwd", after=(g_up,))
    d_merged = _mm(dx1_b, w_out_f, "nt", F32, "mm_dmerged")
    g_out = _mm(merged, dx1_b, "tn", BF, "mm_gout")
    sums1 = reduce_group(1, ["w_up"], [g_up], (g_out,))
    dz_a, dz_b, dz_c, dga, dgb, dgc = _merge_bwd(proj, z_a, z_b, z_c, d_merged, after=sums1)
    dy_a = _mm(dz_a, W["w_branch_a"], "nt", F32, "mm_dya", b_stack=True)
    dy_b_ = _mm(dz_b, W["w_branch_b"], "nt", F32, "mm_dyb", b_stack=True)
    dy_c = _mm(dz_c, W["w_branch_c"], "nt", F32, "mm_dyc", b_stack=True)
    g_ba = _mm(y_a, dz_a, "tn", BF, "mm_gba", out_stack=True)
    g_bb = _mm(y_b, dz_b, "tn", BF, "mm_gbb", out_stack=True)
    g_bc = _mm(y_c, dz_c, "tn", BF, "mm_gbc", out_stack=True)
    d_uv, d_g_a_v, d_w_s, d_b_t = _a_bwd(proj, g_a_v, w_spatial[0], b_t, dy_a, after=(g_ba, g_bb, g_bc))
    sums2 = reduce_group(2, ["w_out", "w_branch_a", "w_branch_b", "w_branch_c"],
                         [g_out.reshape(N_DEV, D // N_DEV, D), g_ba, g_bb, g_bc], (d_uv,))
    dqn, dkn, dv_b, dsink_rows = _b_attn_bwd(qn, kn, proj, sinks, dy_b_, after=sums2)
    d_qkv, d_gq2, d_gk2 = _b_pre_bwd(proj, gq2, gk2, ct, sa, sb, dqn, dkn, dv_b)
    dq_c, dk_c, dv_c, d_gcq, d_gck = _c_bwd(proj, kv, g_c_q, g_c_k, dy_c)
    dkv_b = jnp.concatenate([dk_c, dv_c], axis=1).astype(BF)
    d_memh = _mm(dkv_b, w_kv_f, "nt", F32, "mm_dmemh")
    g_kv = _mm(mem_h, dkv_b, "tn", BF, "mm_gkv")
    _, _, d_g_mem = _rms_bwd(mem2, rstd_m, g_mem, d_memh, None, "rmsmem_bwd")
    dproj = jnp.concatenate([d_uv, d_qkv, dq_c, dga, dgb, dgc], axis=1)
    g_in = _mm(dproj, h, "tn", BF, "mm_gin", tm=1280)
    d_h = _mm(dproj, w_in_t, "nn", F32, "mm_dh", tk=1792, after=(g_in, g_kv))
    sums3 = reduce_group(3, ["w_in", "w_mem_kv"],
                         [g_in.reshape(N_DEV, in_cols // N_DEV, D), g_kv.reshape(N_DEV, D // N_DEV, 2 * C_WIDTH)], (d_h,))
    grad_x, _, d_g_mix = _rms_bwd(x2, rstd1, g_mix, d_h, dx1, "rms1_bwd", after=sums3)

    small_names =["g_mix", "g_a_v", "w_spatial", "b_spatial", "g_b_q", "g_b_k", "sinks", "g_mem", "g_c_q", "g_c_k", "g_ffn", "conv_b"]
    small_w = dict(g_mix=g_mix, g_a_v=g_a_v, w_spatial=w_spatial, b_spatial=b_spatial, g_b_q=g_b_q, g_b_k=g_b_k, sinks=sinks,
                   g_mem=g_mem, g_c_q=g_c_q, g_c_k=g_c_k, g_ffn=g_ffn, conv_b=conv_b)
    small_m = dict(g_mix=m_g_mix, g_a_v=m_g_a_v, w_spatial=m_w_spatial, b_spatial=m_b_spatial, g_b_q=m_g_b_q, g_b_k=m_g_b_k,
                   sinks=m_sinks, g_mem=m_g_mem, g_c_q=m_g_c_q, g_c_k=m_g_c_k, g_ffn=m_g_ffn, conv_b=m_conv_b)
    small_v = dict(g_mix=v_g_mix, g_a_v=v_g_a_v, w_spatial=v_w_spatial, b_spatial=v_b_spatial, g_b_q=v_g_b_q, g_b_k=v_g_b_k,
                   sinks=v_sinks, g_mem=v_g_mem, g_c_q=v_g_c_q, g_c_k=v_g_c_k, g_ffn=v_g_ffn, conv_b=v_conv_b)
    small_g = dict(
        g_mix=d_g_mix, g_a_v=d_g_a_v, w_spatial=d_w_s, b_spatial=d_b_t.T,
        g_b_q=d_gq2.reshape(2, B_HEAD_DIM).sum(0), g_b_k=d_gk2.reshape(2, B_HEAD_DIM).sum(0),
        sinks=dsink_rows.sum(0)[:B_HEADS], g_mem=d_g_mem, g_c_q=d_gcq.sum(0), g_c_k=d_gck.sum(0), g_ffn=d_g_ffn,
        conv_b=d_cb3)
    partial = [small_g[k].reshape(small_w[k].shape) for k in small_names] + [d_cw3]
    parts = _allgather(partial, "ag_small")
    small_res, (g_cw3,) = _adamw_small(parts[:-1], [small_w[k] for k in small_names], [small_m[k] for k in small_names],
                                       [small_v[k] for k in small_names], parts[-1:], "adamw_small")
    small_out = dict(zip(small_names, small_res))
    c_cw = 2 * F // N_DEV
    g_cw = lax.dynamic_slice(g_cw3, (me // (N_DEV // 2), 0, (me % (N_DEV // 2)) * c_cw), (1, 3, c_cw))[0]
    cw_res = _adamw_plain(g_cw, conv_w[0], m_conv_w[0], v_conv_w[0], "adamw_conv_w")
    big_out = {"conv_w": [g_cw[None]] + [a[None] for a in cw_res]}

    moments = dict(w_in=(m_w_in, v_w_in), w_mem_kv=(m_w_mem_kv, v_w_mem_kv), w_branch_a=(m_w_branch_a, v_w_branch_a),
                   w_branch_b=(m_w_branch_b, v_w_branch_b), w_branch_c=(m_w_branch_c, v_w_branch_c), w_out=(m_w_out, v_w_out),
                   w_up=(m_w_up, v_w_up), w_down=(m_w_down, v_w_down))
    token = (grad_x, small_res[0][0])
    for k in ["w_down", "w_up", "w_out", "w_branch_a", "w_branch_b", "w_branch_c", "w_mem_kv", "w_in"]:
        s, r = reduced[k]
        mk, vk = moments[k][0][0], moments[k][1][0]
        if k == "w_in":
            res = _adamw_big(s, r, chip, big[k], mk.T, vk.T, "adamw_" + k, after=token)
            big_out[k] = [a.T[None] for a in res]
        else:
            res = _adamw_big(s, r, chip, big[k], mk, vk, "adamw_" + k, after=token)
            big_out[k] = [a[None] for a in res]
        token = (res[0],)

    order = ["g_mix", "w_in", "g_a_v", "w_spatial", "b_spatial", "g_b_q", "g_b_k", "sinks", "g_mem", "w_mem_kv", "g_c_q", "g_c_k",
             "w_branch_a", "w_branch_b", "w_branch_c", "w_out", "g_ffn", "w_up", "conv_w", "conv_b", "w_down"]
    res = {**small_out, **big_out}
    outs = [loss, grad_x[None]]
    for field in range(4):
        outs += [res[k][field] for k in order]
    return tuple(outs)
```

```python
import functools

import jax
import jax.numpy as jnp
from jax import lax
from jax.experimental import pallas as pl
from jax.experimental.pallas import tpu as pltpu
from jax.experimental.pallas import tpu_sc as plsc

F32 = jnp.float32
BF = jnp.bfloat16
EPS = 1e-6
NEG = -1e30

N_DEV = 8
CHUNK = 128
A_GROUPS = 4
A_WIDTH = 512
B_HEADS = 16
B_KV_HEADS = 2
B_HEAD_DIM = 64
B_WIDTH = 1024
B_KV_WIDTH = 128
ROPE_DIM = 16
ROPE_THETA = 500000.0
C_HEADS = 4
C_HEAD_DIM = 128
C_WIDTH = 512
GATE_OFF = 2 * A_WIDTH + B_WIDTH + 2 * B_KV_WIDTH + C_WIDTH

ADAM_LR = 0.001
ADAM_B1 = 0.9
ADAM_B2 = 0.999
ADAM_EPS = 1e-08
ADAM_WD = 0.01
ADAM_STEP = 10

VMEM_LIMIT = 48 * 1024 * 1024
MESH = pl.DeviceIdType.MESH


def _pick(n, prefs):
    for p in prefs:
        if p <= n and n % p == 0:
            return p
    return n


def _params(sem):
    return pltpu.CompilerParams(dimension_semantics=sem, vmem_limit_bytes=VMEM_LIMIT)


def _hide(body, n_seen, n_hidden):
    if not n_hidden:
        return body

    def wrapped(*refs):
        return body(*refs[:n_seen], *refs[n_seen + n_hidden:])

    return wrapped


def _hidden_specs(after):
    return [pl.BlockSpec(memory_space=pl.ANY) for _ in after]


def _token(xs, name):
    def body(*refs):
        refs[-1][...] = jnp.zeros_like(refs[-1])

    return pl.pallas_call(body, name=name, in_specs=_hidden_specs(xs), out_shape=jax.ShapeDtypeStruct((8, 128), F32))(*xs)


def _mm(a, b, mode, out_dtype, name, *, resid=None, b_stack=False, a_parts=0, b_parts=0, out_parts=0,
        out_stack=False, tm=1024, tn=1024, tk=2048, after=(), exchange=()):
    if mode == "nn":
        M = a.shape[-2]
        K = a.shape[-1] * max(a_parts, 1)
        N = b.shape[-1] * (N_DEV if b_stack else 1)
        dims = (((1,), (0,)), ((), ()))
    elif mode == "nt":
        M = a.shape[-2]
        K = a.shape[-1] * max(a_parts, 1)
        N = b.shape[-2]
        dims = (((1,), (1,)), ((), ()))
    else:
        K = a.shape[-2]
        M = a.shape[-1]
        N = b.shape[-1] * max(b_parts, 1)
        dims = (((0,), (0,)), ((), ()))
    if b_stack and mode == "nn":
        tn = b.shape[-1]
    if b_stack and mode == "nt":
        tk = b.shape[-1]
    if out_stack:
        tn = N // N_DEV
    tm, tn, tk = _pick(M, (tm,)), _pick(N, (tn,)), _pick(K, (tk,))
    if M % tm or N % tn or K % tk:
        raise ValueError(f"{name}: tiles {tm},{tn},{tk} do not divide {M},{N},{K}")
    nm, nn, nk = M // tm, N // tn, K // tk

    def parts_idx(t, ntile, parts):
        per = ntile // parts
        return t // per, t % per

    if mode in ("nn", "nt"):
        if a_parts:
            a_spec = pl.BlockSpec((None, tm, tk), lambda m, n, k: (parts_idx(k, nk, a_parts)[0], m, parts_idx(k, nk, a_parts)[1]))
        else:
            a_spec = pl.BlockSpec((tm, tk), lambda m, n, k: (m, k))
    else:
        a_spec = pl.BlockSpec((tk, tm), lambda m, n, k: (k, m))
    if mode == "nn":
        if b_stack:
            b_spec = pl.BlockSpec((None, tk, tn), lambda m, n, k: (n, k, 0))
        else:
            b_spec = pl.BlockSpec((tk, tn), lambda m, n, k: (k, n))
    elif mode == "nt":
        if b_stack:
            b_spec = pl.BlockSpec((None, tn, tk), lambda m, n, k: (k, n, 0))
        else:
            b_spec = pl.BlockSpec((tn, tk), lambda m, n, k: (n, k))
    else:
        if b_parts:
            b_spec = pl.BlockSpec((None, tk, tn), lambda m, n, k: (parts_idx(n, nn, b_parts)[0], k, parts_idx(n, nn, b_parts)[1]))
        else:
            b_spec = pl.BlockSpec((tk, tn), lambda m, n, k: (k, n))
    if out_stack:
        out_shape = jax.ShapeDtypeStruct((N_DEV, M, tn), out_dtype)
        o_spec = pl.BlockSpec((None, tm, tn), lambda m, n, k: (n, m, 0))
    elif out_parts:
        out_shape = jax.ShapeDtypeStruct((out_parts, M, N // out_parts), out_dtype)
        o_spec = pl.BlockSpec((None, tm, tn), lambda m, n, k: (parts_idx(n, nn, out_parts)[0], m, parts_idx(n, nn, out_parts)[1]))
    else:
        out_shape = jax.ShapeDtypeStruct((M, N), out_dtype)
        o_spec = pl.BlockSpec((tm, tn), lambda m, n, k: (m, n))
    has_resid = resid is not None

    n_ex = len(exchange)
    n_in = 2 + has_resid + len(after)

    def body(*refs):
        a_ref, b_ref = refs[:2]
        r_ref = refs[2] if has_resid else None
        ex_in = refs[n_in:n_in + n_ex]
        o_ref = refs[n_in + n_ex]
        ex_out = refs[n_in + n_ex + 1:n_in + 2 * n_ex + 1]
        scratch = refs[n_in + 2 * n_ex + 1:]
        m_i, n_i, k = pl.program_id(0), pl.program_id(1), pl.program_id(2)

        def pushes():
            send_sems, recv_sems = scratch[-2:]
            x, y, c = lax.axis_index("x"), lax.axis_index("y"), lax.axis_index("c")
            return [pltpu.make_async_remote_copy(
                src_ref=ex_in[w].at[:, 1 - c], dst_ref=ex_out[w], send_sem=send_sems.at[w], recv_sem=recv_sems.at[w],
                device_id=(x, y, 1 - c), device_id_type=MESH) for w in range(n_ex)]

        if n_ex:
            @pl.when((m_i == 0) & (n_i == 0) & (k == 0))
            def _():
                for cp in pushes():
                    cp.start()

        if nk == 1:
            res = lax.dot_general(a_ref[...], b_ref[...], dims, preferred_element_type=F32)
            if has_resid:
                res = res + r_ref[...]
            o_ref[...] = res.astype(o_ref.dtype)
        else:
            acc = scratch[0]

            @pl.when(k == 0)
            def _():
                acc[...] = jnp.zeros_like(acc)

            acc[...] += lax.dot_general(a_ref[...], b_ref[...], dims, preferred_element_type=F32)

            @pl.when(k == nk - 1)
            def _():
                res = acc[...]
                if has_resid:
                    res = res + r_ref[...]
                o_ref[...] = res.astype(o_ref.dtype)

        if n_ex:
            @pl.when((m_i == nm - 1) & (n_i == nn - 1) & (k == nk - 1))
            def _():
                for cp in pushes():
                    cp.wait()

    in_specs = [a_spec, b_spec]
    args = [a, b]
    if has_resid:
        in_specs.append(pl.BlockSpec((tm, tn), lambda m, n, k: (m, n)))
        args.append(resid)
    in_specs += _hidden_specs(after) + _hidden_specs(exchange)
    args += list(after) + list(exchange)
    scratch_shapes = [pltpu.VMEM((tm, tn), F32)] if nk > 1 else []
    if not n_ex:
        return pl.pallas_call(
            body, name=name, grid=(nm, nn, nk), in_specs=in_specs, out_specs=o_spec, out_shape=out_shape,
            scratch_shapes=scratch_shapes, compiler_params=_params(("parallel", "parallel", "arbitrary")),
        )(*args)
    res = pl.pallas_call(
        body, name=name, grid=(nm, nn, nk), in_specs=in_specs, out_specs=[o_spec] + _hidden_specs(exchange),
        out_shape=[out_shape] + [jax.ShapeDtypeStruct((g.shape[0],) + g.shape[2:], g.dtype) for g in exchange],
        scratch_shapes=scratch_shapes + [pltpu.SemaphoreType.DMA((n_ex,)), pltpu.SemaphoreType.DMA((n_ex,))],
        compiler_params=_params(("arbitrary", "arbitrary", "arbitrary")),
    )(*args)
    return res[0], list(res[1:])


def _rms_fwd(x, g, name):
    R, D = x.shape
    tr = _pick(R, (256,))

    def body(x_ref, g_ref, h_ref, r_ref):
        xv = x_ref[...]
        r = lax.rsqrt(jnp.mean(xv * xv, axis=-1, keepdims=True) + EPS)
        h_ref[...] = (xv * r * g_ref[...]).astype(BF)
        r_ref[...] = r

    return pl.pallas_call(
        body, name=name, grid=(R // tr,),
        in_specs=[pl.BlockSpec((tr, D), lambda i: (i, 0)), pl.BlockSpec((1, D), lambda i: (0, 0))],
        out_specs=[pl.BlockSpec((tr, D), lambda i: (i, 0)), pl.BlockSpec((tr, 1), lambda i: (i, 0))],
        out_shape=[jax.ShapeDtypeStruct((R, D), BF), jax.ShapeDtypeStruct((R, 1), F32)],
        compiler_params=_params(("parallel",)),
    )(x, g)


def _rms_bwd(x, r, g, dh, dres, name, after=()):
    R, D = x.shape
    tr = _pick(R, (256,))
    has_res = dres is not None

    def body(*refs):
        if has_res:
            x_ref, r_ref, g_ref, dh_ref, dres_ref, dx_ref, dxb_ref, dg_ref = refs
        else:
            x_ref, r_ref, g_ref, dh_ref, dx_ref, dxb_ref, dg_ref = refs
        i = pl.program_id(0)
        xv, rv, dhv = x_ref[...], r_ref[...], dh_ref[...]
        gy = dhv * g_ref[...]
        c = jnp.sum(xv * gy, axis=-1, keepdims=True)
        dx = rv * gy - xv * (rv * rv * rv) * (c * (1.0 / D))
        if has_res:
            dx = dx + dres_ref[...]
        dx_ref[...] = dx
        dxb_ref[...] = dx.astype(BF)
        part = jnp.sum(dhv * xv * rv, axis=0, keepdims=True)

        @pl.when(i == 0)
        def _():
            dg_ref[...] = part

        @pl.when(i > 0)
        def _():
            dg_ref[...] += part

    row = pl.BlockSpec((tr, D), lambda i: (i, 0))
    in_specs = [row, pl.BlockSpec((tr, 1), lambda i: (i, 0)), pl.BlockSpec((1, D), lambda i: (0, 0)), row]
    args = [x, r, g, dh]
    if has_res:
        in_specs.append(row)
        args.append(dres)
    return pl.pallas_call(
        _hide(body, len(args), len(after)), name=name, grid=(R // tr,), in_specs=in_specs + _hidden_specs(after),
        out_specs=[row, row, pl.BlockSpec((1, D), lambda i: (0, 0))],
        out_shape=[jax.ShapeDtypeStruct((R, D), F32), jax.ShapeDtypeStruct((R, D), BF), jax.ShapeDtypeStruct((1, D), F32)],
        compiler_params=_params(("arbitrary",)),
    )(*args, *after)


def _a_chunk(us, vs, gvs, ws, bs):
    r_i = lax.broadcasted_iota(jnp.int32, (CHUNK, CHUNK), 0)
    c_i = lax.broadcasted_iota(jnp.int32, (CHUNK, CHUNK), 1)
    causal = r_i >= c_i
    vg = [jax.nn.gelu(v) for v in vs]
    ss = sum(jnp.sum(v * v, axis=-1, keepdims=True) for v in vg)
    r = lax.rsqrt(ss * (1.0 / A_WIDTH) + EPS)
    ys = []
    for g in range(A_GROUPS):
        vn = vg[g] * r * gvs[g]
        w = jnp.where(causal, ws[g], 0.0)
        s = jnp.dot(w.astype(BF), vn.astype(BF), preferred_element_type=F32) + bs[g]
        ys.append(jax.nn.gelu(us[g]) * s)
    return ys


def _a_split(u_ref, v_ref, g_ref, w_ref, b_ref):
    sl = [slice(g * 128, (g + 1) * 128) for g in range(A_GROUPS)]
    return ([u_ref[:, s] for s in sl], [v_ref[:, s] for s in sl], [g_ref[:, s] for s in sl],
            [w_ref[g] for g in range(A_GROUPS)], [b_ref[:, g:g + 1] for g in range(A_GROUPS)])


def _a_specs(S):
    return [pl.BlockSpec((CHUNK, A_WIDTH), lambda n: (n, 0)), pl.BlockSpec((CHUNK, A_WIDTH), lambda n: (n, 1)),
            pl.BlockSpec((1, A_WIDTH), lambda n: (0, 0)), pl.BlockSpec((A_GROUPS, CHUNK, CHUNK), lambda n: (0, 0, 0)),
            pl.BlockSpec((CHUNK, A_GROUPS), lambda n: (0, 0))]


def _a_fwd(proj, g_v, w_s, b_t):
    S = proj.shape[0]

    def body(u_ref, v_ref, g_ref, w_ref, b_ref, y_ref):
        ys = _a_chunk(*_a_split(u_ref, v_ref, g_ref, w_ref, b_ref))
        for g in range(A_GROUPS):
            y_ref[:, g * 128:(g + 1) * 128] = ys[g].astype(BF)

    return pl.pallas_call(
        body, name="a_fwd", grid=(S // CHUNK,), in_specs=_a_specs(S),
        out_specs=pl.BlockSpec((CHUNK, A_WIDTH), lambda n: (n, 0)),
        out_shape=jax.ShapeDtypeStruct((S, A_WIDTH), BF), compiler_params=_params(("parallel",)),
    )(proj, proj, g_v, w_s, b_t)


def _a_bwd(proj, g_v, w_s, b_t, dy, after=()):
    S = proj.shape[0]

    def body(u_ref, v_ref, g_ref, w_ref, b_ref, dy_ref, duv_ref, dg_ref, dw_ref, db_ref):
        n = pl.program_id(0)
        dys = [dy_ref[:, g * 128:(g + 1) * 128] for g in range(A_GROUPS)]
        _, vjp = jax.vjp(_a_chunk, *_a_split(u_ref, v_ref, g_ref, w_ref, b_ref))
        dus, dvs, dgs, dws, dbs = vjp(dys)

        @pl.when(n == 0)
        def _():
            dg_ref[...] = jnp.zeros_like(dg_ref)
            dw_ref[...] = jnp.zeros_like(dw_ref)
            db_ref[...] = jnp.zeros_like(db_ref)

        for g in range(A_GROUPS):
            duv_ref[:, g * 128:(g + 1) * 128] = dus[g].astype(BF)
            duv_ref[:, A_WIDTH + g * 128:A_WIDTH + (g + 1) * 128] = dvs[g].astype(BF)
            dg_ref[:, g * 128:(g + 1) * 128] += dgs[g]
            dw_ref[g] += dws[g]
            db_ref[:, g:g + 1] += dbs[g]

    return pl.pallas_call(
        _hide(body, 6, len(after)), name="a_bwd", grid=(S // CHUNK,),
        in_specs=_a_specs(S) + [pl.BlockSpec((CHUNK, A_WIDTH), lambda n: (n, 0))] + _hidden_specs(after),
        out_specs=[pl.BlockSpec((CHUNK, 2 * A_WIDTH), lambda n: (n, 0)), pl.BlockSpec((1, A_WIDTH), lambda n: (0, 0)),
                   pl.BlockSpec((A_GROUPS, CHUNK, CHUNK), lambda n: (0, 0, 0)), pl.BlockSpec((CHUNK, A_GROUPS), lambda n: (0, 0))],
        out_shape=[jax.ShapeDtypeStruct((S, 2 * A_WIDTH), BF), jax.ShapeDtypeStruct((1, A_WIDTH), F32),
                   jax.ShapeDtypeStruct((A_GROUPS, CHUNK, CHUNK), F32), jax.ShapeDtypeStruct((CHUNK, A_GROUPS), F32)],
        compiler_params=_params(("arbitrary",)),
    )(proj, proj, g_v, w_s, b_t, dy, *after)


def _half_mask(shape, which):
    lane = lax.broadcasted_iota(jnp.int32, shape, len(shape) - 1)
    return (lane >= 64) == (which == 1)


def _pair_norm_rope(x, g, ct, sa, sb):
    lo = _half_mask(x.shape, 0)
    x2 = x * x
    ss_lo = jnp.sum(jnp.where(lo, x2, 0.0), axis=-1, keepdims=True)
    ss_hi = jnp.sum(jnp.where(lo, 0.0, x2), axis=-1, keepdims=True)
    r = jnp.where(lo, lax.rsqrt(ss_lo * (1.0 / B_HEAD_DIM) + EPS), lax.rsqrt(ss_hi * (1.0 / B_HEAD_DIM) + EPS))
    xr = x * r
    xn = xr * g
    out = xn * ct + pltpu.roll(xn, 120, 1) * sa + pltpu.roll(xn, 8, 1) * sb
    return out, xr, r


def _pair_norm_rope_bwd(x, g, ct, sa, sb, dout):
    lo = _half_mask(x.shape, 0)
    _, xr, r = _pair_norm_rope(x, g, ct, sa, sb)
    dxn = dout * ct + pltpu.roll(dout * sa, 8, 1) + pltpu.roll(dout * sb, 120, 1)
    gy = dxn * g
    t = xr * gy
    c_lo = jnp.sum(jnp.where(lo, t, 0.0), axis=-1, keepdims=True)
    c_hi = jnp.sum(jnp.where(lo, 0.0, t), axis=-1, keepdims=True)
    c = jnp.where(lo, c_lo, c_hi)
    dx = r * (gy - xr * c * (1.0 / B_HEAD_DIM))
    dg = jnp.sum(dxn * xr, axis=0, keepdims=True)
    return dx, dg


def _b_pre(proj, gq2, gk2, ct, sa, sb):
    S = proj.shape[0]
    tr = _pick(S, (256,))
    n_pair = B_WIDTH // 128

    def body(q_ref, k_ref, gq_ref, gk_ref, ct_ref, sa_ref, sb_ref, qn_ref, kn_ref):
        ct_v, sa_v, sb_v = ct_ref[...], sa_ref[...], sb_ref[...]
        for p in range(n_pair):
            o, _, _ = _pair_norm_rope(q_ref[:, p * 128:(p + 1) * 128], gq_ref[...], ct_v, sa_v, sb_v)
            qn_ref[:, p * 128:(p + 1) * 128] = o.astype(BF)
        o, _, _ = _pair_norm_rope(k_ref[...], gk_ref[...], ct_v, sa_v, sb_v)
        kn_ref[...] = o.astype(BF)

    tab = pl.BlockSpec((tr, 128), lambda i: (i, 0))
    gsp = pl.BlockSpec((1, 128), lambda i: (0, 0))
    return pl.pallas_call(
        body, name="b_pre", grid=(S // tr,),
        in_specs=[pl.BlockSpec((tr, B_WIDTH), lambda i: (i, 1)), pl.BlockSpec((tr, 128), lambda i: (i, 2 * B_WIDTH // 128)),
                  gsp, gsp, tab, tab, tab],
        out_specs=[pl.BlockSpec((tr, B_WIDTH), lambda i: (i, 0)), tab],
        out_shape=[jax.ShapeDtypeStruct((S, B_WIDTH), BF), jax.ShapeDtypeStruct((S, 128), BF)],
        compiler_params=_params(("parallel",)),
    )(proj, proj, gq2, gk2, ct, sa, sb)


def _b_pre_bwd(proj, gq2, gk2, ct, sa, sb, dqn, dkn, dv):
    S = proj.shape[0]
    tr = _pick(S, (256,))
    n_pair = B_WIDTH // 128

    def body(q_ref, k_ref, gq_ref, gk_ref, ct_ref, sa_ref, sb_ref, dqn_ref, dkn_ref, dv_ref, dqkv_ref, dgq_ref, dgk_ref):
        i = pl.program_id(0)
        ct_v, sa_v, sb_v = ct_ref[...], sa_ref[...], sb_ref[...]
        dgq = jnp.zeros((1, 128), F32)
        for p in range(n_pair):
            sl = slice(p * 128, (p + 1) * 128)
            dx, dg = _pair_norm_rope_bwd(q_ref[:, sl], gq_ref[...], ct_v, sa_v, sb_v, dqn_ref[:, sl])
            dqkv_ref[:, sl] = dx.astype(BF)
            dgq = dgq + dg
        dx, dgk = _pair_norm_rope_bwd(k_ref[...], gk_ref[...], ct_v, sa_v, sb_v, dkn_ref[...])
        dqkv_ref[:, B_WIDTH:B_WIDTH + 128] = dx.astype(BF)
        dqkv_ref[:, B_WIDTH + 128:B_WIDTH + 256] = dv_ref[...].astype(BF)

        @pl.when(i == 0)
        def _():
            dgq_ref[...] = dgq
            dgk_ref[...] = dgk

        @pl.when(i > 0)
        def _():
            dgq_ref[...] += dgq
            dgk_ref[...] += dgk

    tab = pl.BlockSpec((tr, 128), lambda i: (i, 0))
    gsp = pl.BlockSpec((1, 128), lambda i: (0, 0))
    return pl.pallas_call(
        body, name="b_pre_bwd", grid=(S // tr,),
        in_specs=[pl.BlockSpec((tr, B_WIDTH), lambda i: (i, 1)), pl.BlockSpec((tr, 128), lambda i: (i, 2 * B_WIDTH // 128)),
                  gsp, gsp, tab, tab, tab, pl.BlockSpec((tr, B_WIDTH), lambda i: (i, 0)), tab, tab],
        out_specs=[pl.BlockSpec((tr, B_WIDTH + 256), lambda i: (i, 0)), gsp, gsp],
        out_shape=[jax.ShapeDtypeStruct((S, B_WIDTH + 256), BF), jax.ShapeDtypeStruct((1, 128), F32), jax.ShapeDtypeStruct((1, 128), F32)],
        compiler_params=_params(("arbitrary",)),
    )(proj, proj, gq2, gk2, ct, sa, sb, dqn, dkn, dv)


def _b_dup(x2, g):
    d = jnp.where(_half_mask(x2.shape, g), x2, 0.0)
    return (d + pltpu.roll(d, 64, 1)).astype(BF)


PAIRS_PER_GROUP = B_HEADS // B_KV_HEADS // 2
GROUP_ROWS = PAIRS_PER_GROUP * CHUNK


def _b_valid(n):
    row = lax.broadcasted_iota(jnp.int32, (GROUP_ROWS, 2 * CHUNK), 0) & (CHUNK - 1)
    col = lax.broadcasted_iota(jnp.int32, (GROUP_ROWS, 2 * CHUNK), 1)
    rel = row + CHUNK - col
    return (rel >= 0) & (rel < CHUNK) & ((col >= CHUNK) | (n > 0))


def _b_blocks(x2, g):
    xd = _b_dup(x2, g)
    lo = _half_mask(xd.shape, 0)
    zero = jnp.zeros_like(xd)
    return jnp.concatenate([jnp.where(lo, xd, zero), jnp.where(lo, zero, xd)], axis=0)


def _b_sink_col(s_ref, g, hf):
    rb = lax.broadcasted_iota(jnp.int32, (GROUP_ROWS, 1), 0) // CHUNK
    col = jnp.zeros((GROUP_ROWS, 1), F32)
    for pp in range(PAIRS_PER_GROUP):
        col = jnp.where(rb == pp, s_ref[0, 2 * (g * PAIRS_PER_GROUP + pp) + hf], col)
    return col


def _b_probs(qs, kblk, valid, sinks):
    s = lax.dot_general(qs, kblk, (((1,), (1,)), ((), ())), preferred_element_type=F32) * (B_HEAD_DIM ** -0.5)
    out = []
    for hf in range(2):
        sh = jnp.where(valid, s[:, hf * 2 * CHUNK:(hf + 1) * 2 * CHUNK], NEG)
        m = jnp.maximum(jnp.max(sh, axis=-1, keepdims=True), sinks[hf])
        e = jnp.exp(sh - m)
        es = jnp.exp(sinks[hf] - m)
        inv = 1.0 / (jnp.sum(e, axis=-1, keepdims=True) + es)
        out.append((e * inv, es * inv))
    return out


def _b_fold(acc, g):
    lo = _half_mask((2 * CHUNK, 128), 0)
    t = jnp.where(lo, acc[:2 * CHUNK], 0.0) + jnp.where(lo, 0.0, acc[2 * CHUNK:])
    return jnp.where(_half_mask((2 * CHUNK, 128), g), t + pltpu.roll(t, 64, 1), 0.0)


def _b_kv_specs(S):
    prev = lambda n: (jnp.maximum(n - 1, 0), 0)
    cur = lambda n: (n, 0)
    v_col = (2 * B_WIDTH + B_KV_WIDTH) // 128
    return [pl.BlockSpec((CHUNK, 128), prev), pl.BlockSpec((CHUNK, 128), cur),
            pl.BlockSpec((CHUNK, 128), lambda n: (jnp.maximum(n - 1, 0), v_col)), pl.BlockSpec((CHUNK, 128), lambda n: (n, v_col))]


def _b_attn_fwd(qn, kn, proj, sinks):
    S = qn.shape[0]

    def body(s_ref, q_ref, kp_ref, kc_ref, vp_ref, vc_ref, y_ref):
        n = pl.program_id(0)
        valid = _b_valid(n)
        k2 = jnp.concatenate([kp_ref[...], kc_ref[...]], axis=0).astype(F32)
        v2 = jnp.concatenate([vp_ref[...], vc_ref[...]], axis=0)
        for g in range(B_KV_HEADS):
            pairs = [g * PAIRS_PER_GROUP + pp for pp in range(PAIRS_PER_GROUP)]
            qs = jnp.concatenate([q_ref[:, p * 128:(p + 1) * 128] for p in pairs], axis=0)
            probs = _b_probs(qs, _b_blocks(k2, g), valid, [_b_sink_col(s_ref, g, hf) for hf in range(2)])
            pcat = jnp.concatenate([probs[0][0].astype(BF), probs[1][0].astype(BF)], axis=1)
            o = jnp.dot(pcat, _b_blocks(v2, g), preferred_element_type=F32)
            for pp, p in enumerate(pairs):
                y_ref[:, p * 128:(p + 1) * 128] = o[pp * CHUNK:(pp + 1) * CHUNK].astype(BF)

    return pl.pallas_call(
        body, name="b_attn_fwd", grid=(S // CHUNK,),
        in_specs=[pl.BlockSpec(memory_space=pltpu.SMEM), pl.BlockSpec((CHUNK, B_WIDTH), lambda n: (n, 0))] + _b_kv_specs(S),
        out_specs=pl.BlockSpec((CHUNK, B_WIDTH), lambda n: (n, 0)),
        out_shape=jax.ShapeDtypeStruct((S, B_WIDTH), BF), compiler_params=_params(("arbitrary",)),
    )(sinks, qn, kn, kn, proj, proj)


def _b_attn_bwd(qn, kn, proj, sinks, dy, after=()):
    S = qn.shape[0]

    def body(s_ref, q_ref, kp_ref, kc_ref, vp_ref, vc_ref, dy_ref, dq_ref, dk_ref, dv_ref, ds_ref):
        n = pl.program_id(0)

        @pl.when(n == 0)
        def _():
            dk_ref[...] = jnp.zeros_like(dk_ref)
            dv_ref[...] = jnp.zeros_like(dv_ref)
            ds_ref[...] = jnp.zeros_like(ds_ref)

        valid = _b_valid(n)
        k2 = jnp.concatenate([kp_ref[...], kc_ref[...]], axis=0).astype(F32)
        v2 = jnp.concatenate([vp_ref[...], vc_ref[...]], axis=0)
        lane = lax.broadcasted_iota(jnp.int32, (CHUNK, 128), 1)
        dk2 = jnp.zeros((2 * CHUNK, 128), F32)
        dv2 = jnp.zeros((2 * CHUNK, 128), F32)
        dsink = jnp.zeros((CHUNK, 128), F32)
        scale = B_HEAD_DIM ** -0.5
        nt = (((1,), (1,)), ((), ()))
        tn = (((0,), (0,)), ((), ()))
        for g in range(B_KV_HEADS):
            pairs = [g * PAIRS_PER_GROUP + pp for pp in range(PAIRS_PER_GROUP)]
            qs = jnp.concatenate([q_ref[:, p * 128:(p + 1) * 128] for p in pairs], axis=0)
            do = jnp.concatenate([dy_ref[:, p * 128:(p + 1) * 128] for p in pairs], axis=0)
            do_b = do.astype(BF)
            kblk, vblk = _b_blocks(k2, g), _b_blocks(v2, g)
            probs = _b_probs(qs, kblk, valid, [_b_sink_col(s_ref, g, hf) for hf in range(2)])
            pcat = jnp.concatenate([probs[0][0].astype(BF), probs[1][0].astype(BF)], axis=1)
            o = jnp.dot(pcat, vblk, preferred_element_type=F32)
            dp = lax.dot_general(do_b, vblk, nt, preferred_element_type=F32)
            prod = do * o
            ds_halves = []
            for hf in range(2):
                pr, ps = probs[hf]
                delta = jnp.sum(jnp.where(_half_mask(prod.shape, hf), prod, 0.0), axis=-1, keepdims=True)
                ds_halves.append((pr * (dp[:, hf * 2 * CHUNK:(hf + 1) * 2 * CHUNK] - delta) * scale).astype(BF))
                t = -ps * delta
                for pp, p in enumerate(pairs):
                    dsink = dsink + jnp.where(lane == 2 * p + hf, t[pp * CHUNK:(pp + 1) * CHUNK], 0.0)
            dsc = jnp.concatenate(ds_halves, axis=1)
            dq = jnp.dot(dsc, kblk, preferred_element_type=F32)
            for pp, p in enumerate(pairs):
                dq_ref[:, p * 128:(p + 1) * 128] = dq[pp * CHUNK:(pp + 1) * CHUNK]
            dk2 = dk2 + _b_fold(lax.dot_general(dsc, qs, tn, preferred_element_type=F32), g)
            dv2 = dv2 + _b_fold(lax.dot_general(pcat, do_b, tn, preferred_element_type=F32), g)
        ds_ref[...] += dsink
        cur = pl.ds(pl.multiple_of(n * CHUNK, CHUNK), CHUNK)
        dk_ref[cur, :] += dk2[CHUNK:]
        dv_ref[cur, :] += dv2[CHUNK:]

        @pl.when(n > 0)
        def _():
            prv = pl.ds(pl.multiple_of((n - 1) * CHUNK, CHUNK), CHUNK)
            dk_ref[prv, :] += dk2[:CHUNK]
            dv_ref[prv, :] += dv2[:CHUNK]

    full = pl.BlockSpec((S, 128), lambda n: (0, 0))
    return pl.pallas_call(
        _hide(body, 7, len(after)), name="b_attn_bwd", grid=(S // CHUNK,),
        in_specs=[pl.BlockSpec(memory_space=pltpu.SMEM), pl.BlockSpec((CHUNK, B_WIDTH), lambda n: (n, 0))] + _b_kv_specs(S)
        + [pl.BlockSpec((CHUNK, B_WIDTH), lambda n: (n, 0))] + _hidden_specs(after),
        out_specs=[pl.BlockSpec((CHUNK, B_WIDTH), lambda n: (n, 0)), full, full, pl.BlockSpec((CHUNK, 128), lambda n: (0, 0))],
        out_shape=[jax.ShapeDtypeStruct((S, B_WIDTH), F32), jax.ShapeDtypeStruct((S, 128), F32), jax.ShapeDtypeStruct((S, 128), F32),
                   jax.ShapeDtypeStruct((CHUNK, 128), F32)],
        compiler_params=_params(("arbitrary",)),
    )(sinks, qn, kn, kn, proj, proj, dy, *after)


def _c_block(q, k, v, gq, gk):
    qn = q * lax.rsqrt(jnp.mean(q * q, axis=-1, keepdims=True) + EPS) * gq
    kn = k * lax.rsqrt(jnp.mean(k * k, axis=-1, keepdims=True) + EPS) * gk
    s = lax.dot_general(qn.astype(BF), kn.astype(BF), (((1,), (1,)), ((), ())), preferred_element_type=F32) * (C_HEAD_DIM ** -0.5)
    p = jax.nn.softmax(s, axis=-1)
    return jnp.dot(p.astype(BF), v.astype(BF), preferred_element_type=F32)


def _c_specs(S, M, tq):
    q_col = (2 * A_WIDTH + B_WIDTH + 2 * B_KV_WIDTH) // 128
    return [pl.BlockSpec((tq, 128), lambda h, i: (i, q_col + h)), pl.BlockSpec((M, 128), lambda h, i: (0, h)),
            pl.BlockSpec((M, 128), lambda h, i: (0, C_HEADS + h)), pl.BlockSpec((1, 128), lambda h, i: (0, 0)),
            pl.BlockSpec((1, 128), lambda h, i: (0, 0))]


def _c_fwd(proj, kv, gq, gk):
    S, M = proj.shape[0], kv.shape[0]
    tq = _pick(S, (512,))

    def body(q_ref, k_ref, v_ref, gq_ref, gk_ref, y_ref):
        y_ref[...] = _c_block(q_ref[...], k_ref[...], v_ref[...], gq_ref[...], gk_ref[...]).astype(BF)

    return pl.pallas_call(
        body, name="c_fwd", grid=(C_HEADS, S // tq), in_specs=_c_specs(S, M, tq),
        out_specs=pl.BlockSpec((tq, 128), lambda h, i: (i, h)),
        out_shape=jax.ShapeDtypeStruct((S, C_WIDTH), BF), compiler_params=_params(("parallel", "parallel")),
    )(proj, kv, kv, gq, gk)


def _c_bwd(proj, kv, gq, gk, dy):
    S, M = proj.shape[0], kv.shape[0]
    tq = _pick(S, (512,))

    def body(q_ref, k_ref, v_ref, gq_ref, gk_ref, dy_ref, dq_ref, dk_ref, dv_ref, dgq_ref, dgk_ref):
        i = pl.program_id(1)
        _, vjp = jax.vjp(_c_block, q_ref[...], k_ref[...], v_ref[...], gq_ref[...], gk_ref[...])
        dq, dk, dv, dgq, dgk = vjp(dy_ref[...])
        dq_ref[...] = dq.astype(BF)

        @pl.when(i == 0)
        def _():
            dk_ref[...] = dk
            dv_ref[...] = dv
            dgq_ref[...] = dgq
            dgk_ref[...] = dgk

        @pl.when(i > 0)
        def _():
            dk_ref[...] += dk
            dv_ref[...] += dv
            dgq_ref[...] += dgq
            dgk_ref[...] += dgk

    return pl.pallas_call(
        body, name="c_bwd", grid=(C_HEADS, S // tq),
        in_specs=_c_specs(S, M, tq) + [pl.BlockSpec((tq, 128), lambda h, i: (i, h))],
        out_specs=[pl.BlockSpec((tq, 128), lambda h, i: (i, h)), pl.BlockSpec((M, 128), lambda h, i: (0, h)),
                   pl.BlockSpec((M, 128), lambda h, i: (0, h)), pl.BlockSpec((None, 1, 128), lambda h, i: (h, 0, 0)),
                   pl.BlockSpec((None, 1, 128), lambda h, i: (h, 0, 0))],
        out_shape=[jax.ShapeDtypeStruct((S, C_WIDTH), BF), jax.ShapeDtypeStruct((M, C_WIDTH), F32), jax.ShapeDtypeStruct((M, C_WIDTH), F32),
                   jax.ShapeDtypeStruct((C_HEADS, 1, 128), F32), jax.ShapeDtypeStruct((C_HEADS, 1, 128), F32)],
        compiler_params=_params(("parallel", "arbitrary")),
    )(proj, kv, kv, gq, gk, dy)


def _merge_specs(S, D, tr, tc):
    off = GATE_OFF // tc
    nd = D // tc
    gates = [pl.BlockSpec((tr, tc), functools.partial(lambda b, i, j: (i, off + b * nd + j), b)) for b in range(3)]
    zs = [pl.BlockSpec((tr, tc), lambda i, j: (i, j)) for _ in range(3)]
    return gates + zs


def _merge_fwd(proj, za, zb, zc):
    S, D = za.shape
    tr, tc = _pick(S, (512,)), _pick(D, (256,))

    def body(ga_ref, gb_ref, gc_ref, za_ref, zb_ref, zc_ref, m_ref):
        acc = jax.nn.sigmoid(ga_ref[...]) * za_ref[...].astype(F32)
        acc = acc + jax.nn.sigmoid(gb_ref[...]) * zb_ref[...].astype(F32)
        acc = acc + jax.nn.sigmoid(gc_ref[...]) * zc_ref[...].astype(F32)
        m_ref[...] = acc.astype(BF)

    return pl.pallas_call(
        body, name="merge_fwd", grid=(S // tr, D // tc), in_specs=_merge_specs(S, D, tr, tc),
        out_specs=pl.BlockSpec((tr, tc), lambda i, j: (i, j)), out_shape=jax.ShapeDtypeStruct((S, D), BF),
        compiler_params=_params(("parallel", "parallel")),
    )(proj, proj, proj, za, zb, zc)


def _merge_bwd(proj, za, zb, zc, dm, after=()):
    S, D = za.shape
    tr, tc = _pick(S, (512,)), _pick(D, (256,))
    nd = D // tc

    def body(ga_ref, gb_ref, gc_ref, za_ref, zb_ref, zc_ref, dm_ref, dza_ref, dzb_ref, dzc_ref, dga_ref, dgb_ref, dgc_ref):
        dmv = dm_ref[...]
        for g_ref, z_ref, dz_ref, dg_ref in ((ga_ref, za_ref, dza_ref, dga_ref), (gb_ref, zb_ref, dzb_ref, dgb_ref),
                                             (gc_ref, zc_ref, dzc_ref, dgc_ref)):
            sg = jax.nn.sigmoid(g_ref[...])
            dz_ref[...] = (sg * dmv).astype(BF)
            dg_ref[...] = (dmv * z_ref[...].astype(F32) * sg * (1.0 - sg)).astype(BF)

    tile = pl.BlockSpec((tr, tc), lambda i, j: (i, j))
    return pl.pallas_call(
        _hide(body, 7, len(after)), name="merge_bwd", grid=(S // tr, D // tc),
        in_specs=_merge_specs(S, D, tr, tc) + [tile] + _hidden_specs(after),
        out_specs=[tile, tile, tile, tile, tile, tile],
        out_shape=[jax.ShapeDtypeStruct((S, D), BF)] * 6,
        compiler_params=_params(("parallel", "parallel")),
    )(proj, proj, proj, za, zb, zc, dm, *after)


PAD = 8


def _stage_shift_down(us_ref, u_ref):
    S = u_ref.shape[1]
    us_ref[:, 0:PAD, :] = jnp.zeros((2, PAD, us_ref.shape[2]), F32)
    us_ref[:, PAD:S + PAD, :] = u_ref[...]


def _conv3(us_ref, part, S, w, b):
    return (us_ref[part, PAD:S + PAD, :] * w[2:3] + us_ref[part, PAD - 1:S + PAD - 1, :] * w[1:2]
            + us_ref[part, PAD - 2:S + PAD - 2, :] * w[0:1] + b)


def _ffn_specs(S, F, tc, c):
    per = c // tc

    def w_spec(half):
        return pl.BlockSpec((None, 3, tc), lambda j: (half * (N_DEV // 2) + j // per, 0, j % per))

    return [pl.BlockSpec((2, S, tc), lambda j: (0, 0, j)), w_spec(0), w_spec(1), pl.BlockSpec((2, 1, tc), lambda j: (0, 0, j))]


def _ffn_tile(F, c):
    tc = 128
    if c % tc or F % tc:
        raise ValueError(f"ffn tile {tc} does not divide {c}, {F}")
    return tc


def _ffn_act_fwd(up3, cws, cb3):
    _, S, F = up3.shape
    c = cws.shape[2]
    tc = _ffn_tile(F, c)

    def body(u_ref, wa_ref, wb_ref, b_ref, o_ref, us_ref):
        _stage_shift_down(us_ref, u_ref)
        ca = _conv3(us_ref, 0, S, wa_ref[...], b_ref[0])
        cb = _conv3(us_ref, 1, S, wb_ref[...], b_ref[1])
        o_ref[...] = (ca * jax.nn.sigmoid(ca) * cb).astype(BF)

    return pl.pallas_call(
        body, name="ffn_act_fwd", grid=(F // tc,), in_specs=_ffn_specs(S, F, tc, c),
        out_specs=pl.BlockSpec((S, tc), lambda j: (0, j)), out_shape=jax.ShapeDtypeStruct((S, F), BF),
        scratch_shapes=[pltpu.VMEM((2, S + PAD, tc), F32)],
        compiler_params=_params(("parallel",)),
    )(up3, cws, cws, cb3)


def _ffn_act_bwd(up3, cws, cb3, dact, after=()):
    _, S, F = up3.shape
    c = cws.shape[2]
    tc = _ffn_tile(F, c)

    def body(u_ref, wa_ref, wb_ref, b_ref, da_ref, du_ref, dw_ref, db_ref, us_ref, dcs_ref):
        w_refs = (wa_ref, wb_ref)
        _stage_shift_down(us_ref, u_ref)
        ca = _conv3(us_ref, 0, S, wa_ref[...], b_ref[0])
        cb = _conv3(us_ref, 1, S, wb_ref[...], b_ref[1])
        sg = jax.nn.sigmoid(ca)
        dav = da_ref[...]
        dcs_ref[0, 0:S, :] = dav * cb * sg * (1.0 + ca * (1.0 - sg))
        dcs_ref[1, 0:S, :] = dav * ca * sg
        dcs_ref[:, S:S + PAD, :] = jnp.zeros((2, PAD, tc), F32)
        for part in range(2):
            w, u = w_refs[part][...], u_ref[part]
            dc, dc1, dc2 = dcs_ref[part, 0:S, :], dcs_ref[part, 1:S + 1, :], dcs_ref[part, 2:S + 2, :]
            du_ref[part] = (dc * w[2:3] + dc1 * w[1:2] + dc2 * w[0:1]).astype(BF)
            dw_ref[part, 2:3, :] = jnp.sum(dc * u, axis=0, keepdims=True)
            dw_ref[part, 1:2, :] = jnp.sum(dc1 * u, axis=0, keepdims=True)
            dw_ref[part, 0:1, :] = jnp.sum(dc2 * u, axis=0, keepdims=True)
            db_ref[part] = jnp.sum(dc, axis=0, keepdims=True)

    return pl.pallas_call(
        _hide(body, 5, len(after)), name="ffn_act_bwd", grid=(F // tc,),
        in_specs=_ffn_specs(S, F, tc, c) + [pl.BlockSpec((S, tc), lambda j: (0, j))] + _hidden_specs(after),
        out_specs=[pl.BlockSpec((2, S, tc), lambda j: (0, 0, j)), pl.BlockSpec((2, 3, tc), lambda j: (0, 0, j)),
                   pl.BlockSpec((2, 1, tc), lambda j: (0, 0, j))],
        out_shape=[jax.ShapeDtypeStruct((2, S, F), BF), jax.ShapeDtypeStruct((2, 3, F), F32), jax.ShapeDtypeStruct((2, 1, F), F32)],
        scratch_shapes=[pltpu.VMEM((2, S + PAD, tc), F32), pltpu.VMEM((2, S + PAD, tc), F32)],
        compiler_params=_params(("parallel",)),
    )(up3, cws, cws, cb3, dact, *after)


def _loss(y, target):
    S, D = y.shape
    tr = _pick(S, (256,))

    def body(y_ref, t_ref, dy_ref, dyb_ref, l_ref):
        i = pl.program_id(0)
        e = y_ref[...] - t_ref[...]
        dy = e * (1.0 / D)
        dy_ref[...] = dy
        dyb_ref[...] = dy.astype(BF)
        part = jnp.sum(jnp.sum(e * e, axis=-1, keepdims=True), axis=0, keepdims=True) * (0.5 / D)

        @pl.when(i == 0)
        def _():
            l_ref[...] = jnp.zeros_like(l_ref)

        l_ref[...] += part

    row = pl.BlockSpec((tr, D), lambda i: (i, 0))
    return pl.pallas_call(
        body, name="loss", grid=(S // tr,), in_specs=[row, row],
        out_specs=[row, row, pl.BlockSpec((8, 128), lambda i: (0, 0))],
        out_shape=[jax.ShapeDtypeStruct((S, D), F32), jax.ShapeDtypeStruct((S, D), BF), jax.ShapeDtypeStruct((8, 128), F32)],
        compiler_params=_params(("arbitrary",)),
    )(y, target)


ANY = pl.BlockSpec(memory_space=pl.ANY)


def _allgather(shards, name):
    n = len(shards)

    def body(*refs):
        ins, outs = refs[:n], refs[n:2 * n]
        send_sems, recv_sems, local_sems = refs[2 * n:]
        x, y, c = lax.axis_index("x"), lax.axis_index("y"), lax.axis_index("c")
        me, sibling = (x, y, c), (x, y, 1 - c)
        chips = [(1 - x, y), (x, 1 - y), (1 - x, 1 - y)]

        def blk(w, px, py, pc):
            return outs[w].at[4 * px + 2 * py + pc]

        def copy(w, k, block, to, src=None):
            return pltpu.make_async_remote_copy(
                src_ref=blk(w, *block) if src is None else src, dst_ref=blk(w, *block),
                send_sem=send_sems.at[w, k], recv_sem=recv_sems.at[w, k], device_id=to, device_id_type=MESH)

        started = []
        mine = []
        for w in range(n):
            mine.append(pltpu.make_async_copy(ins[w], blk(w, *me), local_sems.at[w]))
            mine[-1].start()
            first = [copy(w, 0, me, sibling, src=ins[w])]
            first += [copy(w, 1 + j, me, (*chip, c), src=ins[w]) for j, chip in enumerate(chips)]
            for cp in first:
                cp.start()
            started += first
        for w in range(n):
            for j, chip in enumerate(chips):
                copy(w, 1 + j, (*chip, c), me).wait_recv()
                fwd = copy(w, 4 + j, (*chip, c), sibling)
                fwd.start()
                started.append(fwd)
        for w in range(n):
            copy(w, 0, sibling, me).wait_recv()
            for j, chip in enumerate(chips):
                copy(w, 4 + j, (*chip, 1 - c), me).wait_recv()
        for cp in started:
            cp.wait_send()
        for cp in mine:
            cp.wait()

    whole = pl.BlockSpec(memory_space=pltpu.VMEM)
    outs = pl.pallas_call(
        body, name=name, in_specs=[whole] * n, out_specs=[whole] * n,
        out_shape=[jax.ShapeDtypeStruct((N_DEV,) + s.shape, s.dtype) for s in shards],
        scratch_shapes=[pltpu.SemaphoreType.DMA((n, 7)), pltpu.SemaphoreType.DMA((n, 7)), pltpu.SemaphoreType.DMA((n,))],
    )(*shards)
    return list(outs)


def _allgather_seq(shards, name, collective_id, after=()):
    n = len(shards)
    n_after = len(after)

    def body(*refs):
        ins, outs = refs[:n], refs[n + n_after:2 * n + n_after]
        send_sems, recv_sems, local_sems = refs[2 * n + n_after:]
        x, y, c = lax.axis_index("x"), lax.axis_index("y"), lax.axis_index("c")
        me, sibling = (x, y, c), (x, y, 1 - c)
        chips = [(1 - x, y), (x, 1 - y), (1 - x, 1 - y)]
        barrier = pltpu.get_barrier_semaphore()
        for peer in [sibling] + [(*chip, c) for chip in chips]:
            pl.semaphore_signal(barrier, inc=1, device_id=peer, device_id_type=MESH)
        pl.semaphore_wait(barrier, 4)

        def blk(w, px, py, pc):
            return outs[w].at[4 * px + 2 * py + pc]

        def copy(w, k, block, to, src=None):
            return pltpu.make_async_remote_copy(
                src_ref=blk(w, *block) if src is None else src, dst_ref=blk(w, *block),
                send_sem=send_sems.at[7 * w + k], recv_sem=recv_sems.at[7 * w + k], device_id=to, device_id_type=MESH)

        started = []
        mine = []
        for w in range(n):
            mine.append(pltpu.make_async_copy(ins[w], blk(w, *me), local_sems.at[w]))
            mine[-1].start()
            first = [copy(w, 0, me, sibling, src=ins[w])]
            first += [copy(w, 1 + j, me, (*chip, c), src=ins[w]) for j, chip in enumerate(chips)]
            for cp in first:
                cp.start()
            started += first
        for w in range(n):
            for j, chip in enumerate(chips):
                copy(w, 1 + j, (*chip, c), me).wait_recv()
                fwd = copy(w, 4 + j, (*chip, c), sibling)
                fwd.start()
                started.append(fwd)
        for w in range(n):
            copy(w, 0, sibling, me).wait_recv()
            for j, chip in enumerate(chips):
                copy(w, 4 + j, (*chip, 1 - c), me).wait_recv()
        for cp in started:
            cp.wait_send()
        for cp in mine:
            cp.wait()

    outs = pl.kernel(
        body, name=name, out_type=[jax.ShapeDtypeStruct((N_DEV,) + s.shape, s.dtype) for s in shards],
        mesh=plsc.ScalarSubcoreMesh(axis_name="seq", num_cores=1),
        scratch_types=[pltpu.SemaphoreType.DMA((7 * n,)), pltpu.SemaphoreType.DMA((7 * n,)), pltpu.SemaphoreType.DMA((n,))],
        compiler_params=pltpu.CompilerParams(collective_id=collective_id),
    )(*shards, *after)
    return list(outs)


def _chip_exchange(sums, name, collective_id):
    n = len(sums)

    def body(*refs):
        ins, outs = refs[:n], refs[n:2 * n]
        send_sems, recv_sems = refs[2 * n:]
        x, y, c = lax.axis_index("x"), lax.axis_index("y"), lax.axis_index("c")
        chips = [(1 - x, y), (x, 1 - y), (1 - x, 1 - y)]
        barrier = pltpu.get_barrier_semaphore()
        for px, py in chips:
            pl.semaphore_signal(barrier, inc=1, device_id=(px, py, c), device_id_type=MESH)
        pl.semaphore_wait(barrier, 3)
        copies = []
        for w in range(n):
            for k, (px, py) in enumerate(chips):
                copies.append(pltpu.make_async_remote_copy(
                    src_ref=ins[w].at[2 * px + py], dst_ref=outs[w].at[k], send_sem=send_sems.at[3 * w + k],
                    recv_sem=recv_sems.at[3 * w + k], device_id=(px, py, c), device_id_type=MESH))
        for cp in copies:
            cp.start()
        for cp in copies:
            cp.wait()

    outs = pl.kernel(
        body, name=name, out_type=[jax.ShapeDtypeStruct((3,) + s.shape[1:], s.dtype) for s in sums],
        mesh=plsc.ScalarSubcoreMesh(axis_name="seq", num_cores=1),
        scratch_types=[pltpu.SemaphoreType.DMA((3 * n,)), pltpu.SemaphoreType.DMA((3 * n,))],
        compiler_params=pltpu.CompilerParams(collective_id=collective_id),
    )(*sums)
    return list(outs)


def _row_tile(r, c, elems=256 * 1024):
    want = max(8, elems // c)
    for t in range(min(want, r) // 8 * 8, 0, -8):
        if r % t == 0:
            return t
    return r


def _pair_add(g4, recv, core, name, after=()):
    _, _, r, c = g4.shape
    tr = _row_tile(r, c, 1024 * 1024)

    def body(core_ref, a_ref, b_ref, o_ref):
        o_ref[...] = (a_ref[...].astype(F32) + b_ref[...].astype(F32)).astype(BF)

    return pl.pallas_call(
        _hide(body, 3, len(after)), name=name,
        grid_spec=pltpu.PrefetchScalarGridSpec(
            num_scalar_prefetch=1, grid=(4, r // tr),
            in_specs=[pl.BlockSpec((None, None, tr, c), lambda p, i, s: (p, s[0], i, 0)),
                      pl.BlockSpec((None, tr, c), lambda p, i, s: (p, i, 0))] + _hidden_specs(after),
            out_specs=pl.BlockSpec((None, tr, c), lambda p, i, s: (p, i, 0))),
        out_shape=jax.ShapeDtypeStruct((4, r, c), BF), compiler_params=_params(("parallel", "parallel")),
    )(core, g4, recv, *after)


def _adam_math(w, g, m, v):
    m = ADAM_B1 * m + (1.0 - ADAM_B1) * g
    v = ADAM_B2 * v + (1.0 - ADAM_B2) * (g * g)
    m_hat = m / (1.0 - ADAM_B1 ** ADAM_STEP)
    v_hat = v / (1.0 - ADAM_B2 ** ADAM_STEP)
    delta = -ADAM_LR * (m_hat / (jnp.sqrt(v_hat) + ADAM_EPS) + ADAM_WD * w)
    return delta, m, v


def _adamw_big(sums, recv, chip, w, m, v, name, after=()):
    r, c = w.shape
    tr = _row_tile(r, c, 256 * 1024)

    def body(chip_ref, s_ref, r_ref, w_ref, m_ref, v_ref, g_out, d_out, m_out, v_out):
        g = s_ref[...].astype(F32) + r_ref[0].astype(F32)
        g = g + r_ref[1].astype(F32)
        g = g + r_ref[2].astype(F32)
        delta, mn, vn = _adam_math(w_ref[...], g, m_ref[...], v_ref[...])
        g_out[...] = g
        d_out[...] = delta
        m_out[...] = mn
        v_out[...] = vn

    row = pl.BlockSpec((tr, c), lambda i, s: (i, 0))
    return pl.pallas_call(
        _hide(body, 6, len(after)), name=name,
        grid_spec=pltpu.PrefetchScalarGridSpec(
            num_scalar_prefetch=1, grid=(r // tr,),
            in_specs=[pl.BlockSpec((None, tr, c), lambda i, s: (s[0], i, 0)), pl.BlockSpec((3, tr, c), lambda i, s: (0, i, 0)),
                      row, row, row] + _hidden_specs(after),
            out_specs=[row, row, row, row]),
        out_shape=[jax.ShapeDtypeStruct((r, c), F32)] * 4, compiler_params=_params(("parallel",)),
    )(chip, sums, recv, w, m, v, *after)


def _adamw_small(parts, ws, ms, vs, extra_parts, name):
    n, ne = len(ws), len(extra_parts)

    def total(p_ref):
        g = p_ref[0]
        for d in range(1, N_DEV):
            g = g + p_ref[d]
        return g

    def body(*refs):
        p_refs, w_refs, m_refs, v_refs = refs[:n], refs[n:2 * n], refs[2 * n:3 * n], refs[3 * n:4 * n]
        e_refs = refs[4 * n:4 * n + ne]
        outs = refs[4 * n + ne:]
        for i in range(n):
            g = total(p_refs[i])
            delta, mn, vn = _adam_math(w_refs[i][...], g, m_refs[i][...], v_refs[i][...])
            outs[4 * i][...] = g
            outs[4 * i + 1][...] = delta
            outs[4 * i + 2][...] = mn
            outs[4 * i + 3][...] = vn
        for i in range(ne):
            outs[4 * n + i][...] = total(e_refs[i])

    out_shape = []
    for w in ws:
        out_shape += [jax.ShapeDtypeStruct(w.shape, F32)] * 4
    out_shape += [jax.ShapeDtypeStruct(e.shape[1:], F32) for e in extra_parts]
    res = pl.pallas_call(body, name=name, out_shape=out_shape,
                         compiler_params=pltpu.CompilerParams(vmem_limit_bytes=VMEM_LIMIT))(*parts, *ws, *ms, *vs, *extra_parts)
    return [res[4 * i:4 * i + 4] for i in range(n)], list(res[4 * n:])


def _adamw_plain(g, w, m, v, name):
    def body(g_ref, w_ref, m_ref, v_ref, d_out, m_out, v_out):
        delta, mn, vn = _adam_math(w_ref[...], g_ref[...], m_ref[...], v_ref[...])
        d_out[...] = delta
        m_out[...] = mn
        v_out[...] = vn

    return pl.pallas_call(body, name=name, out_shape=[jax.ShapeDtypeStruct(w.shape, F32)] * 3)(g, w, m, v)


def kernel(x, mem, positions, g_mix, w_in, g_a_v, w_spatial, b_spatial, g_b_q, g_b_k, sinks, g_mem, w_mem_kv, g_c_q, g_c_k, w_branch_a, w_branch_b, w_branch_c, w_out, g_ffn, w_up, conv_w, conv_b, w_down, loss_target, m_g_mix, m_w_in, m_g_a_v, m_w_spatial, m_b_spatial, m_g_b_q, m_g_b_k, m_sinks, m_g_mem, m_w_mem_kv, m_g_c_q, m_g_c_k, m_w_branch_a, m_w_branch_b, m_w_branch_c, m_w_out, m_g_ffn, m_w_up, m_conv_w, m_conv_b, m_w_down, v_g_mix, v_w_in, v_g_a_v, v_w_spatial, v_b_spatial, v_g_b_q, v_g_b_k, v_sinks, v_g_mem, v_w_mem_kv, v_g_c_q, v_g_c_k, v_w_branch_a, v_w_branch_b, v_w_branch_c, v_w_out, v_g_ffn, v_w_up, v_conv_w, v_conv_b, v_w_down):
    S, D = x.shape[1], x.shape[2]
    M = mem.shape[1]
    F = w_down.shape[1] * N_DEV
    in_cols = w_in.shape[2] * N_DEV
    ax, ay, ac = lax.axis_index("x"), lax.axis_index("y"), lax.axis_index("c")
    core = jnp.reshape(ac, (1,)).astype(jnp.int32)
    chip = jnp.reshape(2 * ax + ay, (1,)).astype(jnp.int32)
    me = 4 * ax + 2 * ay + ac

    x2, mem2, tgt2 = x[0], mem[0], loss_target[0]

    big = dict(w_in=w_in[0].T, w_mem_kv=w_mem_kv[0], w_branch_a=w_branch_a[0], w_branch_b=w_branch_b[0],
               w_branch_c=w_branch_c[0], w_out=w_out[0], w_up=w_up[0], w_down=w_down[0])
    names = list(big)
    cast = {k: big[k].astype(BF) for k in names}
    W = {}
    cb3 = conv_b.reshape(2, 1, F)
    W["w_in"], = _allgather_seq([cast["w_in"]], "ag_seq0", 0)
    w_in_t = W["w_in"].reshape(in_cols, D)
    grp1 = ["w_mem_kv", "w_branch_a", "w_branch_b", "w_branch_c", "w_out"]
    res1 = _allgather_seq([cast[k] for k in grp1] + [conv_w[0]], "ag_seq1", 1, after=(_token((w_in_t,), "tok_w_in"),))
    W.update(zip(grp1, res1))
    cw3 = res1[-1]
    w_kv_f = W["w_mem_kv"].reshape(D, 2 * C_WIDTH)
    w_out_f = W["w_out"].reshape(D, D)

    half = ROPE_DIM // 2
    inv = ROPE_THETA ** (-jnp.arange(half, dtype=F32) / half)
    ang = positions[0].astype(F32)[:, None] * inv
    cos, sin = jnp.cos(ang), jnp.sin(ang)
    one, zero = jnp.ones((S, B_HEAD_DIM - ROPE_DIM), F32), jnp.zeros((S, B_HEAD_DIM - ROPE_DIM), F32)
    z8 = jnp.zeros((S, half), F32)
    ct = jnp.tile(jnp.concatenate([cos, cos, one], axis=1), (1, 2))
    sa = jnp.tile(jnp.concatenate([-sin, z8, zero], axis=1), (1, 2))
    sb = jnp.tile(jnp.concatenate([z8, sin, zero], axis=1), (1, 2))
    gq2, gk2 = jnp.tile(g_b_q, (1, 2)), jnp.tile(g_b_k, (1, 2))
    b_t = b_spatial[0].T

    h, rstd1 = _rms_fwd(x2, g_mix, "rms1_fwd")
    proj = _mm(h, w_in_t, "nt", F32, "mm_proj", tn=1280)
    y_a = _a_fwd(proj, g_a_v, w_spatial[0], b_t)
    qn, kn = _b_pre(proj, gq2, gk2, ct, sa, sb)
    W["w_up"], = _allgather_seq([cast["w_up"]], "ag_seq2", 2, after=(_token((W["w_out"], qn), "tok_group1"),))
    y_b = _b_attn_fwd(qn, kn, proj, sinks)
    mem_h, rstd_m = _rms_fwd(mem2, g_mem, "rmsmem_fwd")
    kv = _mm(mem_h, w_kv_f, "nn", F32, "mm_kv", after=(y_b,))
    y_c = _c_fwd(proj, kv, g_c_q, g_c_k)
    z_a = _mm(y_a, W["w_branch_a"], "nn", BF, "mm_za", b_stack=True, after=(y_b,))
    z_b = _mm(y_b, W["w_branch_b"], "nn", BF, "mm_zb", b_stack=True)
    z_c = _mm(y_c, W["w_branch_c"], "nn", BF, "mm_zc", b_stack=True)
    merged = _merge_fwd(proj, z_a, z_b, z_c)
    x1 = _mm(merged, w_out_f, "nn", F32, "mm_x1", resid=x2)
    h2, rstd2 = _rms_fwd(x1, g_ffn, "rms2_fwd")
    W["w_down"], = _allgather_seq([cast["w_down"]], "ag_seq3", 3, after=(W["w_up"], h2))
    w_down_f = W["w_down"].reshape(F, D)
    up3 = _mm(h2, W["w_up"], "nn", F32, "mm_up", b_stack=True, out_parts=2)
    act = _ffn_act_fwd(up3, cw3, cb3)
    y = _mm(act, w_down_f, "nn", F32, "mm_y", resid=x1, tk=1408)
    dy, dy_b, loss_acc = _loss(y, tgt2)
    loss = lax.psum(loss_acc[0, 0], ("x", "y", "c"))

    reduced = {}

    def as4(g):
        return g.reshape(4, 2, g.shape[1], g.shape[2])

    def finish_group(gi, keys, g4, from_sibling):
        sums = [_pair_add(a, b, core, "rs_add_" + k) for k, a, b in zip(keys, g4, from_sibling)]
        from_chips = _chip_exchange(sums, f"rs_chip{gi}", 4 + gi)
        reduced.update(zip(keys, zip(sums, from_chips)))
        return tuple(sums)

    d_act = _mm(dy_b, w_down_f, "nt", F32, "mm_dact", tn=1408)
    g_down = _mm(act, dy_b, "tn", BF, "mm_gdown", tm=1408)
    d_up3, d_cw3, d_cb3 = _ffn_act_bwd(up3, cw3, cb3, d_act, after=(g_down,))
    grp0 = [as4(g_down.reshape(N_DEV, F // N_DEV, D))]
    g_up, sib0 = _mm(h2, d_up3, "tn", BF, "mm_gup", b_parts=2, out_stack=True, exchange=grp0)
    sums0 = finish_group(0, ["w_down"], grp0, sib0)
    grp1 = [as4(g_up)]
    d_h2, sib1 = _mm(d_up3, W["w_up"], "nt", F32, "mm_dh2", a_parts=2, b_stack=True, after=sums0, exchange=grp1)
    sums1 = finish_group(1, ["w_up"], grp1, sib1)
    dx1, dx1_b, d_g_ffn = _rms_bwd(x1, rstd2, g_ffn, d_h2, dy, "rms2_bwd", after=sums1)
    d_merged = _mm(dx1_b, w_out_f, "nt", F32, "mm_dmerged")
    g_out = _mm(merged, dx1_b, "tn", BF, "mm_gout")
    dz_a, dz_b, dz_c, dga, dgb, dgc = _merge_bwd(proj, z_a, z_b, z_c, d_merged, after=(g_out,))
    g_ba = _mm(y_a, dz_a, "tn", BF, "mm_gba", out_stack=True)
    g_bb = _mm(y_b, dz_b, "tn", BF, "mm_gbb", out_stack=True)
    g_bc = _mm(y_c, dz_c, "tn", BF, "mm_gbc", out_stack=True)
    grp2 = [as4(g_out.reshape(N_DEV, D // N_DEV, D)), as4(g_ba), as4(g_bb), as4(g_bc)]
    dy_a, sib2 = _mm(dz_a, W["w_branch_a"], "nt", F32, "mm_dya", b_stack=True, exchange=grp2)
    sums2 = finish_group(2, ["w_out", "w_branch_a", "w_branch_b", "w_branch_c"], grp2, sib2)
    dy_b_ = _mm(dz_b, W["w_branch_b"], "nt", F32, "mm_dyb", b_stack=True, after=sums2)
    dy_c = _mm(dz_c, W["w_branch_c"], "nt", F32, "mm_dyc", b_stack=True)
    d_uv, d_g_a_v, d_w_s, d_b_t = _a_bwd(proj, g_a_v, w_spatial[0], b_t, dy_a)
    dqn, dkn, dv_b, dsink_rows = _b_attn_bwd(qn, kn, proj, sinks, dy_b_)
    d_qkv, d_gq2, d_gk2 = _b_pre_bwd(proj, gq2, gk2, ct, sa, sb, dqn, dkn, dv_b)
    dq_c, dk_c, dv_c, d_gcq, d_gck = _c_bwd(proj, kv, g_c_q, g_c_k, dy_c)
    dkv_b = jnp.concatenate([dk_c, dv_c], axis=1).astype(BF)
    d_memh = _mm(dkv_b, w_kv_f, "nt", F32, "mm_dmemh")
    g_kv = _mm(mem_h, dkv_b, "tn", BF, "mm_gkv")
    _, _, d_g_mem = _rms_bwd(mem2, rstd_m, g_mem, d_memh, None, "rmsmem_bwd")
    dproj = jnp.concatenate([d_uv, d_qkv, dq_c, dga, dgb, dgc], axis=1)
    g_in = _mm(dproj, h, "tn", BF, "mm_gin", tm=1280)
    grp3 = [as4(g_in.reshape(N_DEV, in_cols // N_DEV, D)), as4(g_kv.reshape(N_DEV, D // N_DEV, 2 * C_WIDTH))]
    d_h, sib3 = _mm(dproj, w_in_t, "nn", F32, "mm_dh", tk=1792, exchange=grp3)
    sums3 = finish_group(3, ["w_in", "w_mem_kv"], grp3, sib3)
    grad_x, _, d_g_mix = _rms_bwd(x2, rstd1, g_mix, d_h, dx1, "rms1_bwd", after=sums3)

    small_names =["g_mix", "g_a_v", "w_spatial", "b_spatial", "g_b_q", "g_b_k", "sinks", "g_mem", "g_c_q", "g_c_k", "g_ffn", "conv_b"]
    small_w = dict(g_mix=g_mix, g_a_v=g_a_v, w_spatial=w_spatial, b_spatial=b_spatial, g_b_q=g_b_q, g_b_k=g_b_k, sinks=sinks,
                   g_mem=g_mem, g_c_q=g_c_q, g_c_k=g_c_k, g_ffn=g_ffn, conv_b=conv_b)
    small_m = dict(g_mix=m_g_mix, g_a_v=m_g_a_v, w_spatial=m_w_spatial, b_spatial=m_b_spatial, g_b_q=m_g_b_q, g_b_k=m_g_b_k,
                   sinks=m_sinks, g_mem=m_g_mem, g_c_q=m_g_c_q, g_c_k=m_g_c_k, g_ffn=m_g_ffn, conv_b=m_conv_b)
    small_v = dict(g_mix=v_g_mix, g_a_v=v_g_a_v, w_spatial=v_w_spatial, b_spatial=v_b_spatial, g_b_q=v_g_b_q, g_b_k=v_g_b_k,
                   sinks=v_sinks, g_mem=v_g_mem, g_c_q=v_g_c_q, g_c_k=v_g_c_k, g_ffn=v_g_ffn, conv_b=v_conv_b)
    small_g = dict(
        g_mix=d_g_mix, g_a_v=d_g_a_v, w_spatial=d_w_s, b_spatial=d_b_t.T,
        g_b_q=d_gq2.reshape(2, B_HEAD_DIM).sum(0), g_b_k=d_gk2.reshape(2, B_HEAD_DIM).sum(0),
        sinks=dsink_rows.sum(0)[:B_HEADS], g_mem=d_g_mem, g_c_q=d_gcq.sum(0), g_c_k=d_gck.sum(0), g_ffn=d_g_ffn,
        conv_b=d_cb3)
    partial = [small_g[k].reshape(small_w[k].shape) for k in small_names] + [d_cw3]
    parts = _allgather(partial, "ag_small")
    small_res, (g_cw3,) = _adamw_small(parts[:-1], [small_w[k] for k in small_names], [small_m[k] for k in small_names],
                                       [small_v[k] for k in small_names], parts[-1:], "adamw_small")
    small_out = dict(zip(small_names, small_res))
    c_cw = 2 * F // N_DEV
    g_cw = lax.dynamic_slice(g_cw3, (me // (N_DEV // 2), 0, (me % (N_DEV // 2)) * c_cw), (1, 3, c_cw))[0]
    cw_res = _adamw_plain(g_cw, conv_w[0], m_conv_w[0], v_conv_w[0], "adamw_conv_w")
    big_out = {"conv_w": [g_cw[None]] + [a[None] for a in cw_res]}

    moments = dict(w_in=(m_w_in, v_w_in), w_mem_kv=(m_w_mem_kv, v_w_mem_kv), w_branch_a=(m_w_branch_a, v_w_branch_a),
                   w_branch_b=(m_w_branch_b, v_w_branch_b), w_branch_c=(m_w_branch_c, v_w_branch_c), w_out=(m_w_out, v_w_out),
                   w_up=(m_w_up, v_w_up), w_down=(m_w_down, v_w_down))
    token = (grad_x, small_res[0][0])
    for k in ["w_down", "w_up", "w_out", "w_branch_a", "w_branch_b", "w_branch_c", "w_mem_kv", "w_in"]:
        s, r = reduced[k]
        mk, vk = moments[k][0][0], moments[k][1][0]
        if k == "w_in":
            res = _adamw_big(s, r, chip, big[k], mk.T, vk.T, "adamw_" + k, after=token)
            big_out[k] = [a.T[None] for a in res]
        else:
            res = _adamw_big(s, r, chip, big[k], mk, vk, "adamw_" + k, after=token)
            big_out[k] = [a[None] for a in res]
        token = (res[0],)

    order = ["g_mix", "w_in", "g_a_v", "w_spatial", "b_spatial", "g_b_q", "g_b_k", "sinks", "g_mem", "w_mem_kv", "g_c_q", "g_c_k",
             "w_branch_a", "w_branch_b", "w_branch_c", "w_out", "g_ffn", "w_up", "conv_w", "conv_b", "w_down"]
    res = {**small_out, **big_out}
    outs = [loss, grad_x[None]]
    for field in range(4):
        outs += [res[k][field] for k in order]
    return tuple(outs)
```

```python
import functools

import jax
import jax.numpy as jnp
from jax import lax
from jax.experimental import pallas as pl
from jax.experimental.pallas import tpu as pltpu
from jax.experimental.pallas import tpu_sc as plsc

F32 = jnp.float32
BF = jnp.bfloat16
EPS = 1e-6
NEG = -1e30

N_DEV = 8
CHUNK = 128
A_GROUPS = 4
A_WIDTH = 512
B_HEADS = 16
B_KV_HEADS = 2
B_HEAD_DIM = 64
B_WIDTH = 1024
B_KV_WIDTH = 128
ROPE_DIM = 16
ROPE_THETA = 500000.0
C_HEADS = 4
C_HEAD_DIM = 128
C_WIDTH = 512
GATE_OFF = 2 * A_WIDTH + B_WIDTH + 2 * B_KV_WIDTH + C_WIDTH

ADAM_LR = 0.001
ADAM_B1 = 0.9
ADAM_B2 = 0.999
ADAM_EPS = 1e-08
ADAM_WD = 0.01
ADAM_STEP = 10

VMEM_LIMIT = 48 * 1024 * 1024
MESH = pl.DeviceIdType.MESH


def _pick(n, prefs):
    for p in prefs:
        if p <= n and n % p == 0:
            return p
    return n


def _params(sem):
    return pltpu.CompilerParams(dimension_semantics=sem, vmem_limit_bytes=VMEM_LIMIT)


def _hide(body, n_seen, n_hidden):
    if not n_hidden:
        return body

    def wrapped(*refs):
        return body(*refs[:n_seen], *refs[n_seen + n_hidden:])

    return wrapped


def _hidden_specs(after):
    return [pl.BlockSpec(memory_space=pl.ANY) for _ in after]


def _token(xs, name):
    def body(*refs):
        refs[-1][...] = jnp.zeros_like(refs[-1])

    return pl.pallas_call(body, name=name, in_specs=_hidden_specs(xs), out_shape=jax.ShapeDtypeStruct((8, 128), F32))(*xs)


def _mm(a, b, mode, out_dtype, name, *, resid=None, b_stack=False, a_parts=0, b_parts=0, out_parts=0,
        out_stack=False, tm=1024, tn=1024, tk=2048, after=(), exchange=()):
    if mode == "nn":
        M = a.shape[-2]
        K = a.shape[-1] * max(a_parts, 1)
        N = b.shape[-1] * (N_DEV if b_stack else 1)
        dims = (((1,), (0,)), ((), ()))
    elif mode == "nt":
        M = a.shape[-2]
        K = a.shape[-1] * max(a_parts, 1)
        N = b.shape[-2]
        dims = (((1,), (1,)), ((), ()))
    else:
        K = a.shape[-2]
        M = a.shape[-1]
        N = b.shape[-1] * max(b_parts, 1)
        dims = (((0,), (0,)), ((), ()))
    if b_stack and mode == "nn":
        tn = b.shape[-1]
    if b_stack and mode == "nt":
        tk = b.shape[-1]
    if out_stack:
        tn = N // N_DEV
    tm, tn, tk = _pick(M, (tm,)), _pick(N, (tn,)), _pick(K, (tk,))
    if M % tm or N % tn or K % tk:
        raise ValueError(f"{name}: tiles {tm},{tn},{tk} do not divide {M},{N},{K}")
    nm, nn, nk = M // tm, N // tn, K // tk

    def parts_idx(t, ntile, parts):
        per = ntile // parts
        return t // per, t % per

    if mode in ("nn", "nt"):
        if a_parts:
            a_spec = pl.BlockSpec((None, tm, tk), lambda m, n, k: (parts_idx(k, nk, a_parts)[0], m, parts_idx(k, nk, a_parts)[1]))
        else:
            a_spec = pl.BlockSpec((tm, tk), lambda m, n, k: (m, k))
    else:
        a_spec = pl.BlockSpec((tk, tm), lambda m, n, k: (k, m))
    if mode == "nn":
        if b_stack:
            b_spec = pl.BlockSpec((None, tk, tn), lambda m, n, k: (n, k, 0))
        else:
            b_spec = pl.BlockSpec((tk, tn), lambda m, n, k: (k, n))
    elif mode == "nt":
        if b_stack:
            b_spec = pl.BlockSpec((None, tn, tk), lambda m, n, k: (k, n, 0))
        else:
            b_spec = pl.BlockSpec((tn, tk), lambda m, n, k: (n, k))
    else:
        if b_parts:
            b_spec = pl.BlockSpec((None, tk, tn), lambda m, n, k: (parts_idx(n, nn, b_parts)[0], k, parts_idx(n, nn, b_parts)[1]))
        else:
            b_spec = pl.BlockSpec((tk, tn), lambda m, n, k: (k, n))
    if out_stack:
        out_shape = jax.ShapeDtypeStruct((N_DEV, M, tn), out_dtype)
        o_spec = pl.BlockSpec((None, tm, tn), lambda m, n, k: (n, m, 0))
    elif out_parts:
        out_shape = jax.ShapeDtypeStruct((out_parts, M, N // out_parts), out_dtype)
        o_spec = pl.BlockSpec((None, tm, tn), lambda m, n, k: (parts_idx(n, nn, out_parts)[0], m, parts_idx(n, nn, out_parts)[1]))
    else:
        out_shape = jax.ShapeDtypeStruct((M, N), out_dtype)
        o_spec = pl.BlockSpec((tm, tn), lambda m, n, k: (m, n))
    has_resid = resid is not None

    n_ex = len(exchange)
    n_in = 2 + has_resid + len(after)

    def body(*refs):
        a_ref, b_ref = refs[:2]
        r_ref = refs[2] if has_resid else None
        ex_in = refs[n_in:n_in + n_ex]
        o_ref = refs[n_in + n_ex]
        ex_out = refs[n_in + n_ex + 1:n_in + 2 * n_ex + 1]
        scratch = refs[n_in + 2 * n_ex + 1:]
        m_i, n_i, k = pl.program_id(0), pl.program_id(1), pl.program_id(2)

        def pushes():
            send_sems, recv_sems = scratch[-2:]
            x, y, c = lax.axis_index("x"), lax.axis_index("y"), lax.axis_index("c")
            return [pltpu.make_async_remote_copy(
                src_ref=ex_in[w].at[:, 1 - c], dst_ref=ex_out[w], send_sem=send_sems.at[w], recv_sem=recv_sems.at[w],
                device_id=(x, y, 1 - c), device_id_type=MESH) for w in range(n_ex)]

        if n_ex:
            @pl.when((m_i == 0) & (n_i == 0) & (k == 0))
            def _():
                for cp in pushes():
                    cp.start()

        if nk == 1:
            res = lax.dot_general(a_ref[...], b_ref[...], dims, preferred_element_type=F32)
            if has_resid:
                res = res + r_ref[...]
            o_ref[...] = res.astype(o_ref.dtype)
        else:
            acc = scratch[0]

            @pl.when(k == 0)
            def _():
                acc[...] = jnp.zeros_like(acc)

            acc[...] += lax.dot_general(a_ref[...], b_ref[...], dims, preferred_element_type=F32)

            @pl.when(k == nk - 1)
            def _():
                res = acc[...]
                if has_resid:
                    res = res + r_ref[...]
                o_ref[...] = res.astype(o_ref.dtype)

        if n_ex:
            @pl.when((m_i == nm - 1) & (n_i == nn - 1) & (k == nk - 1))
            def _():
                for cp in pushes():
                    cp.wait()

    in_specs = [a_spec, b_spec]
    args = [a, b]
    if has_resid:
        in_specs.append(pl.BlockSpec((tm, tn), lambda m, n, k: (m, n)))
        args.append(resid)
    in_specs += _hidden_specs(after) + _hidden_specs(exchange)
    args += list(after) + list(exchange)
    scratch_shapes = [pltpu.VMEM((tm, tn), F32)] if nk > 1 else []
    if not n_ex:
        return pl.pallas_call(
            body, name=name, grid=(nm, nn, nk), in_specs=in_specs, out_specs=o_spec, out_shape=out_shape,
            scratch_shapes=scratch_shapes, compiler_params=_params(("parallel", "parallel", "arbitrary")),
        )(*args)
    res = pl.pallas_call(
        body, name=name, grid=(nm, nn, nk), in_specs=in_specs, out_specs=[o_spec] + _hidden_specs(exchange),
        out_shape=[out_shape] + [jax.ShapeDtypeStruct((g.shape[0],) + g.shape[2:], g.dtype) for g in exchange],
        scratch_shapes=scratch_shapes + [pltpu.SemaphoreType.DMA((n_ex,)), pltpu.SemaphoreType.DMA((n_ex,))],
        compiler_params=_params(("arbitrary", "arbitrary", "arbitrary")),
    )(*args)
    return res[0], list(res[1:])


def _rms_fwd(x, g, name):
    R, D = x.shape
    tr = _pick(R, (256,))

    def body(x_ref, g_ref, h_ref, r_ref):
        xv = x_ref[...]
        r = lax.rsqrt(jnp.mean(xv * xv, axis=-1, keepdims=True) + EPS)
        h_ref[...] = (xv * r * g_ref[...]).astype(BF)
        r_ref[...] = r

    return pl.pallas_call(
        body, name=name, grid=(R // tr,),
        in_specs=[pl.BlockSpec((tr, D), lambda i: (i, 0)), pl.BlockSpec((1, D), lambda i: (0, 0))],
        out_specs=[pl.BlockSpec((tr, D), lambda i: (i, 0)), pl.BlockSpec((tr, 1), lambda i: (i, 0))],
        out_shape=[jax.ShapeDtypeStruct((R, D), BF), jax.ShapeDtypeStruct((R, 1), F32)],
        compiler_params=_params(("parallel",)),
    )(x, g)


def _rms_bwd(x, r, g, dh, dres, name, after=()):
    R, D = x.shape
    tr = _pick(R, (256,))
    has_res = dres is not None

    def body(*refs):
        if has_res:
            x_ref, r_ref, g_ref, dh_ref, dres_ref, dx_ref, dxb_ref, dg_ref = refs
        else:
            x_ref, r_ref, g_ref, dh_ref, dx_ref, dxb_ref, dg_ref = refs
        i = pl.program_id(0)
        xv, rv, dhv = x_ref[...], r_ref[...], dh_ref[...]
        gy = dhv * g_ref[...]
        c = jnp.sum(xv * gy, axis=-1, keepdims=True)
        dx = rv * gy - xv * (rv * rv * rv) * (c * (1.0 / D))
        if has_res:
            dx = dx + dres_ref[...]
        dx_ref[...] = dx
        dxb_ref[...] = dx.astype(BF)
        part = jnp.sum(dhv * xv * rv, axis=0, keepdims=True)

        @pl.when(i == 0)
        def _():
            dg_ref[...] = part

        @pl.when(i > 0)
        def _():
            dg_ref[...] += part

    row = pl.BlockSpec((tr, D), lambda i: (i, 0))
    in_specs = [row, pl.BlockSpec((tr, 1), lambda i: (i, 0)), pl.BlockSpec((1, D), lambda i: (0, 0)), row]
    args = [x, r, g, dh]
    if has_res:
        in_specs.append(row)
        args.append(dres)
    return pl.pallas_call(
        _hide(body, len(args), len(after)), name=name, grid=(R // tr,), in_specs=in_specs + _hidden_specs(after),
        out_specs=[row, row, pl.BlockSpec((1, D), lambda i: (0, 0))],
        out_shape=[jax.ShapeDtypeStruct((R, D), F32), jax.ShapeDtypeStruct((R, D), BF), jax.ShapeDtypeStruct((1, D), F32)],
        compiler_params=_params(("arbitrary",)),
    )(*args, *after)


def _a_chunk(us, vs, gvs, ws, bs):
    r_i = lax.broadcasted_iota(jnp.int32, (CHUNK, CHUNK), 0)
    c_i = lax.broadcasted_iota(jnp.int32, (CHUNK, CHUNK), 1)
    causal = r_i >= c_i
    vg = [jax.nn.gelu(v) for v in vs]
    ss = sum(jnp.sum(v * v, axis=-1, keepdims=True) for v in vg)
    r = lax.rsqrt(ss * (1.0 / A_WIDTH) + EPS)
    ys = []
    for g in range(A_GROUPS):
        vn = vg[g] * r * gvs[g]
        w = jnp.where(causal, ws[g], 0.0)
        s = jnp.dot(w.astype(BF), vn.astype(BF), preferred_element_type=F32) + bs[g]
        ys.append(jax.nn.gelu(us[g]) * s)
    return ys


def _a_split(u_ref, v_ref, g_ref, w_ref, b_ref):
    sl = [slice(g * 128, (g + 1) * 128) for g in range(A_GROUPS)]
    return ([u_ref[:, s] for s in sl], [v_ref[:, s] for s in sl], [g_ref[:, s] for s in sl],
            [w_ref[g] for g in range(A_GROUPS)], [b_ref[:, g:g + 1] for g in range(A_GROUPS)])


def _a_specs(S):
    return [pl.BlockSpec((CHUNK, A_WIDTH), lambda n: (n, 0)), pl.BlockSpec((CHUNK, A_WIDTH), lambda n: (n, 1)),
            pl.BlockSpec((1, A_WIDTH), lambda n: (0, 0)), pl.BlockSpec((A_GROUPS, CHUNK, CHUNK), lambda n: (0, 0, 0)),
            pl.BlockSpec((CHUNK, A_GROUPS), lambda n: (0, 0))]


def _a_fwd(proj, g_v, w_s, b_t):
    S = proj.shape[0]

    def body(u_ref, v_ref, g_ref, w_ref, b_ref, y_ref):
        ys = _a_chunk(*_a_split(u_ref, v_ref, g_ref, w_ref, b_ref))
        for g in range(A_GROUPS):
            y_ref[:, g * 128:(g + 1) * 128] = ys[g].astype(BF)

    return pl.pallas_call(
        body, name="a_fwd", grid=(S // CHUNK,), in_specs=_a_specs(S),
        out_specs=pl.BlockSpec((CHUNK, A_WIDTH), lambda n: (n, 0)),
        out_shape=jax.ShapeDtypeStruct((S, A_WIDTH), BF), compiler_params=_params(("parallel",)),
    )(proj, proj, g_v, w_s, b_t)


def _a_bwd(proj, g_v, w_s, b_t, dy, after=()):
    S = proj.shape[0]

    def body(u_ref, v_ref, g_ref, w_ref, b_ref, dy_ref, duv_ref, dg_ref, dw_ref, db_ref):
        n = pl.program_id(0)
        dys = [dy_ref[:, g * 128:(g + 1) * 128] for g in range(A_GROUPS)]
        _, vjp = jax.vjp(_a_chunk, *_a_split(u_ref, v_ref, g_ref, w_ref, b_ref))
        dus, dvs, dgs, dws, dbs = vjp(dys)

        @pl.when(n == 0)
        def _():
            dg_ref[...] = jnp.zeros_like(dg_ref)
            dw_ref[...] = jnp.zeros_like(dw_ref)
            db_ref[...] = jnp.zeros_like(db_ref)

        for g in range(A_GROUPS):
            duv_ref[:, g * 128:(g + 1) * 128] = dus[g].astype(BF)
            duv_ref[:, A_WIDTH + g * 128:A_WIDTH + (g + 1) * 128] = dvs[g].astype(BF)
            dg_ref[:, g * 128:(g + 1) * 128] += dgs[g]
            dw_ref[g] += dws[g]
            db_ref[:, g:g + 1] += dbs[g]

    return pl.pallas_call(
        _hide(body, 6, len(after)), name="a_bwd", grid=(S // CHUNK,),
        in_specs=_a_specs(S) + [pl.BlockSpec((CHUNK, A_WIDTH), lambda n: (n, 0))] + _hidden_specs(after),
        out_specs=[pl.BlockSpec((CHUNK, 2 * A_WIDTH), lambda n: (n, 0)), pl.BlockSpec((1, A_WIDTH), lambda n: (0, 0)),
                   pl.BlockSpec((A_GROUPS, CHUNK, CHUNK), lambda n: (0, 0, 0)), pl.BlockSpec((CHUNK, A_GROUPS), lambda n: (0, 0))],
        out_shape=[jax.ShapeDtypeStruct((S, 2 * A_WIDTH), BF), jax.ShapeDtypeStruct((1, A_WIDTH), F32),
                   jax.ShapeDtypeStruct((A_GROUPS, CHUNK, CHUNK), F32), jax.ShapeDtypeStruct((CHUNK, A_GROUPS), F32)],
        compiler_params=_params(("arbitrary",)),
    )(proj, proj, g_v, w_s, b_t, dy, *after)


def _half_mask(shape, which):
    lane = lax.broadcasted_iota(jnp.int32, shape, len(shape) - 1)
    return (lane >= 64) == (which == 1)


def _pair_norm_rope(x, g, ct, sa, sb):
    lo = _half_mask(x.shape, 0)
    x2 = x * x
    ss_lo = jnp.sum(jnp.where(lo, x2, 0.0), axis=-1, keepdims=True)
    ss_hi = jnp.sum(jnp.where(lo, 0.0, x2), axis=-1, keepdims=True)
    r = jnp.where(lo, lax.rsqrt(ss_lo * (1.0 / B_HEAD_DIM) + EPS), lax.rsqrt(ss_hi * (1.0 / B_HEAD_DIM) + EPS))
    xr = x * r
    xn = xr * g
    out = xn * ct + pltpu.roll(xn, 120, 1) * sa + pltpu.roll(xn, 8, 1) * sb
    return out, xr, r


def _pair_norm_rope_bwd(x, g, ct, sa, sb, dout):
    lo = _half_mask(x.shape, 0)
    _, xr, r = _pair_norm_rope(x, g, ct, sa, sb)
    dxn = dout * ct + pltpu.roll(dout * sa, 8, 1) + pltpu.roll(dout * sb, 120, 1)
    gy = dxn * g
    t = xr * gy
    c_lo = jnp.sum(jnp.where(lo, t, 0.0), axis=-1, keepdims=True)
    c_hi = jnp.sum(jnp.where(lo, 0.0, t), axis=-1, keepdims=True)
    c = jnp.where(lo, c_lo, c_hi)
    dx = r * (gy - xr * c * (1.0 / B_HEAD_DIM))
    dg = jnp.sum(dxn * xr, axis=0, keepdims=True)
    return dx, dg


def _b_pre(proj, gq2, gk2, ct, sa, sb):
    S = proj.shape[0]
    tr = _pick(S, (256,))
    n_pair = B_WIDTH // 128

    def body(q_ref, k_ref, gq_ref, gk_ref, ct_ref, sa_ref, sb_ref, qn_ref, kn_ref):
        ct_v, sa_v, sb_v = ct_ref[...], sa_ref[...], sb_ref[...]
        for p in range(n_pair):
            o, _, _ = _pair_norm_rope(q_ref[:, p * 128:(p + 1) * 128], gq_ref[...], ct_v, sa_v, sb_v)
            qn_ref[:, p * 128:(p + 1) * 128] = o.astype(BF)
        o, _, _ = _pair_norm_rope(k_ref[...], gk_ref[...], ct_v, sa_v, sb_v)
        kn_ref[...] = o.astype(BF)

    tab = pl.BlockSpec((tr, 128), lambda i: (i, 0))
    gsp = pl.BlockSpec((1, 128), lambda i: (0, 0))
    return pl.pallas_call(
        body, name="b_pre", grid=(S // tr,),
        in_specs=[pl.BlockSpec((tr, B_WIDTH), lambda i: (i, 1)), pl.BlockSpec((tr, 128), lambda i: (i, 2 * B_WIDTH // 128)),
                  gsp, gsp, tab, tab, tab],
        out_specs=[pl.BlockSpec((tr, B_WIDTH), lambda i: (i, 0)), tab],
        out_shape=[jax.ShapeDtypeStruct((S, B_WIDTH), BF), jax.ShapeDtypeStruct((S, 128), BF)],
        compiler_params=_params(("parallel",)),
    )(proj, proj, gq2, gk2, ct, sa, sb)


def _b_pre_bwd(proj, gq2, gk2, ct, sa, sb, dqn, dkn, dv):
    S = proj.shape[0]
    tr = _pick(S, (256,))
    n_pair = B_WIDTH // 128

    def body(q_ref, k_ref, gq_ref, gk_ref, ct_ref, sa_ref, sb_ref, dqn_ref, dkn_ref, dv_ref, dqkv_ref, dgq_ref, dgk_ref):
        i = pl.program_id(0)
        ct_v, sa_v, sb_v = ct_ref[...], sa_ref[...], sb_ref[...]
        dgq = jnp.zeros((1, 128), F32)
        for p in range(n_pair):
            sl = slice(p * 128, (p + 1) * 128)
            dx, dg = _pair_norm_rope_bwd(q_ref[:, sl], gq_ref[...], ct_v, sa_v, sb_v, dqn_ref[:, sl])
            dqkv_ref[:, sl] = dx.astype(BF)
            dgq = dgq + dg
        dx, dgk = _pair_norm_rope_bwd(k_ref[...], gk_ref[...], ct_v, sa_v, sb_v, dkn_ref[...])
        dqkv_ref[:, B_WIDTH:B_WIDTH + 128] = dx.astype(BF)
        dqkv_ref[:, B_WIDTH + 128:B_WIDTH + 256] = dv_ref[...].astype(BF)

        @pl.when(i == 0)
        def _():
            dgq_ref[...] = dgq
            dgk_ref[...] = dgk

        @pl.when(i > 0)
        def _():
            dgq_ref[...] += dgq
            dgk_ref[...] += dgk

    tab = pl.BlockSpec((tr, 128), lambda i: (i, 0))
    gsp = pl.BlockSpec((1, 128), lambda i: (0, 0))
    return pl.pallas_call(
        body, name="b_pre_bwd", grid=(S // tr,),
        in_specs=[pl.BlockSpec((tr, B_WIDTH), lambda i: (i, 1)), pl.BlockSpec((tr, 128), lambda i: (i, 2 * B_WIDTH // 128)),
                  gsp, gsp, tab, tab, tab, pl.BlockSpec((tr, B_WIDTH), lambda i: (i, 0)), tab, tab],
        out_specs=[pl.BlockSpec((tr, B_WIDTH + 256), lambda i: (i, 0)), gsp, gsp],
        out_shape=[jax.ShapeDtypeStruct((S, B_WIDTH + 256), BF), jax.ShapeDtypeStruct((1, 128), F32), jax.ShapeDtypeStruct((1, 128), F32)],
        compiler_params=_params(("arbitrary",)),
    )(proj, proj, gq2, gk2, ct, sa, sb, dqn, dkn, dv)


def _b_dup(x2, g):
    d = jnp.where(_half_mask(x2.shape, g), x2, 0.0)
    return (d + pltpu.roll(d, 64, 1)).astype(BF)


PAIRS_PER_GROUP = B_HEADS // B_KV_HEADS // 2
GROUP_ROWS = PAIRS_PER_GROUP * CHUNK


def _b_valid(n):
    row = lax.broadcasted_iota(jnp.int32, (GROUP_ROWS, 2 * CHUNK), 0) & (CHUNK - 1)
    col = lax.broadcasted_iota(jnp.int32, (GROUP_ROWS, 2 * CHUNK), 1)
    rel = row + CHUNK - col
    return (rel >= 0) & (rel < CHUNK) & ((col >= CHUNK) | (n > 0))


def _b_blocks(x2, g):
    xd = _b_dup(x2, g)
    lo = _half_mask(xd.shape, 0)
    zero = jnp.zeros_like(xd)
    return jnp.concatenate([jnp.where(lo, xd, zero), jnp.where(lo, zero, xd)], axis=0)


def _b_sink_col(s_ref, g, hf):
    rb = lax.broadcasted_iota(jnp.int32, (GROUP_ROWS, 1), 0) // CHUNK
    col = jnp.zeros((GROUP_ROWS, 1), F32)
    for pp in range(PAIRS_PER_GROUP):
        col = jnp.where(rb == pp, s_ref[0, 2 * (g * PAIRS_PER_GROUP + pp) + hf], col)
    return col


def _b_probs(qs, kblk, valid, sinks):
    s = lax.dot_general(qs, kblk, (((1,), (1,)), ((), ())), preferred_element_type=F32) * (B_HEAD_DIM ** -0.5)
    out = []
    for hf in range(2):
        sh = jnp.where(valid, s[:, hf * 2 * CHUNK:(hf + 1) * 2 * CHUNK], NEG)
        m = jnp.maximum(jnp.max(sh, axis=-1, keepdims=True), sinks[hf])
        e = jnp.exp(sh - m)
        es = jnp.exp(sinks[hf] - m)
        inv = 1.0 / (jnp.sum(e, axis=-1, keepdims=True) + es)
        out.append((e * inv, es * inv))
    return out


def _b_fold(acc, g):
    lo = _half_mask((2 * CHUNK, 128), 0)
    t = jnp.where(lo, acc[:2 * CHUNK], 0.0) + jnp.where(lo, 0.0, acc[2 * CHUNK:])
    return jnp.where(_half_mask((2 * CHUNK, 128), g), t + pltpu.roll(t, 64, 1), 0.0)


def _b_kv_specs(S):
    prev = lambda n: (jnp.maximum(n - 1, 0), 0)
    cur = lambda n: (n, 0)
    v_col = (2 * B_WIDTH + B_KV_WIDTH) // 128
    return [pl.BlockSpec((CHUNK, 128), prev), pl.BlockSpec((CHUNK, 128), cur),
            pl.BlockSpec((CHUNK, 128), lambda n: (jnp.maximum(n - 1, 0), v_col)), pl.BlockSpec((CHUNK, 128), lambda n: (n, v_col))]


def _b_attn_fwd(qn, kn, proj, sinks):
    S = qn.shape[0]

    def body(s_ref, q_ref, kp_ref, kc_ref, vp_ref, vc_ref, y_ref):
        n = pl.program_id(0)
        valid = _b_valid(n)
        k2 = jnp.concatenate([kp_ref[...], kc_ref[...]], axis=0).astype(F32)
        v2 = jnp.concatenate([vp_ref[...], vc_ref[...]], axis=0)
        for g in range(B_KV_HEADS):
            pairs = [g * PAIRS_PER_GROUP + pp for pp in range(PAIRS_PER_GROUP)]
            qs = jnp.concatenate([q_ref[:, p * 128:(p + 1) * 128] for p in pairs], axis=0)
            probs = _b_probs(qs, _b_blocks(k2, g), valid, [_b_sink_col(s_ref, g, hf) for hf in range(2)])
            pcat = jnp.concatenate([probs[0][0].astype(BF), probs[1][0].astype(BF)], axis=1)
            o = jnp.dot(pcat, _b_blocks(v2, g), preferred_element_type=F32)
            for pp, p in enumerate(pairs):
                y_ref[:, p * 128:(p + 1) * 128] = o[pp * CHUNK:(pp + 1) * CHUNK].astype(BF)

    return pl.pallas_call(
        body, name="b_attn_fwd", grid=(S // CHUNK,),
        in_specs=[pl.BlockSpec(memory_space=pltpu.SMEM), pl.BlockSpec((CHUNK, B_WIDTH), lambda n: (n, 0))] + _b_kv_specs(S),
        out_specs=pl.BlockSpec((CHUNK, B_WIDTH), lambda n: (n, 0)),
        out_shape=jax.ShapeDtypeStruct((S, B_WIDTH), BF), compiler_params=_params(("arbitrary",)),
    )(sinks, qn, kn, kn, proj, proj)


def _b_attn_bwd(qn, kn, proj, sinks, dy, after=()):
    S = qn.shape[0]

    def body(s_ref, q_ref, kp_ref, kc_ref, vp_ref, vc_ref, dy_ref, dq_ref, dk_ref, dv_ref, ds_ref):
        n = pl.program_id(0)

        @pl.when(n == 0)
        def _():
            dk_ref[...] = jnp.zeros_like(dk_ref)
            dv_ref[...] = jnp.zeros_like(dv_ref)
            ds_ref[...] = jnp.zeros_like(ds_ref)

        valid = _b_valid(n)
        k2 = jnp.concatenate([kp_ref[...], kc_ref[...]], axis=0).astype(F32)
        v2 = jnp.concatenate([vp_ref[...], vc_ref[...]], axis=0)
        lane = lax.broadcasted_iota(jnp.int32, (CHUNK, 128), 1)
        dk2 = jnp.zeros((2 * CHUNK, 128), F32)
        dv2 = jnp.zeros((2 * CHUNK, 128), F32)
        dsink = jnp.zeros((CHUNK, 128), F32)
        scale = B_HEAD_DIM ** -0.5
        nt = (((1,), (1,)), ((), ()))
        tn = (((0,), (0,)), ((), ()))
        for g in range(B_KV_HEADS):
            pairs = [g * PAIRS_PER_GROUP + pp for pp in range(PAIRS_PER_GROUP)]
            qs = jnp.concatenate([q_ref[:, p * 128:(p + 1) * 128] for p in pairs], axis=0)
            do = jnp.concatenate([dy_ref[:, p * 128:(p + 1) * 128] for p in pairs], axis=0)
            do_b = do.astype(BF)
            kblk, vblk = _b_blocks(k2, g), _b_blocks(v2, g)
            probs = _b_probs(qs, kblk, valid, [_b_sink_col(s_ref, g, hf) for hf in range(2)])
            pcat = jnp.concatenate([probs[0][0].astype(BF), probs[1][0].astype(BF)], axis=1)
            o = jnp.dot(pcat, vblk, preferred_element_type=F32)
            dp = lax.dot_general(do_b, vblk, nt, preferred_element_type=F32)
            prod = do * o
            ds_halves = []
            for hf in range(2):
                pr, ps = probs[hf]
                delta = jnp.sum(jnp.where(_half_mask(prod.shape, hf), prod, 0.0), axis=-1, keepdims=True)
                ds_halves.append((pr * (dp[:, hf * 2 * CHUNK:(hf + 1) * 2 * CHUNK] - delta) * scale).astype(BF))
                t = -ps * delta
                for pp, p in enumerate(pairs):
                    dsink = dsink + jnp.where(lane == 2 * p + hf, t[pp * CHUNK:(pp + 1) * CHUNK], 0.0)
            dsc = jnp.concatenate(ds_halves, axis=1)
            dq = jnp.dot(dsc, kblk, preferred_element_type=F32)
            for pp, p in enumerate(pairs):
                dq_ref[:, p * 128:(p + 1) * 128] = dq[pp * CHUNK:(pp + 1) * CHUNK]
            dk2 = dk2 + _b_fold(lax.dot_general(dsc, qs, tn, preferred_element_type=F32), g)
            dv2 = dv2 + _b_fold(lax.dot_general(pcat, do_b, tn, preferred_element_type=F32), g)
        ds_ref[...] += dsink
        cur = pl.ds(pl.multiple_of(n * CHUNK, CHUNK), CHUNK)
        dk_ref[cur, :] += dk2[CHUNK:]
        dv_ref[cur, :] += dv2[CHUNK:]

        @pl.when(n > 0)
        def _():
            prv = pl.ds(pl.multiple_of((n - 1) * CHUNK, CHUNK), CHUNK)
            dk_ref[prv, :] += dk2[:CHUNK]
            dv_ref[prv, :] += dv2[:CHUNK]

    full = pl.BlockSpec((S, 128), lambda n: (0, 0))
    return pl.pallas_call(
        _hide(body, 7, len(after)), name="b_attn_bwd", grid=(S // CHUNK,),
        in_specs=[pl.BlockSpec(memory_space=pltpu.SMEM), pl.BlockSpec((CHUNK, B_WIDTH), lambda n: (n, 0))] + _b_kv_specs(S)
        + [pl.BlockSpec((CHUNK, B_WIDTH), lambda n: (n, 0))] + _hidden_specs(after),
        out_specs=[pl.BlockSpec((CHUNK, B_WIDTH), lambda n: (n, 0)), full, full, pl.BlockSpec((CHUNK, 128), lambda n: (0, 0))],
        out_shape=[jax.ShapeDtypeStruct((S, B_WIDTH), F32), jax.ShapeDtypeStruct((S, 128), F32), jax.ShapeDtypeStruct((S, 128), F32),
                   jax.ShapeDtypeStruct((CHUNK, 128), F32)],
        compiler_params=_params(("arbitrary",)),
    )(sinks, qn, kn, kn, proj, proj, dy, *after)


def _c_block(q, k, v, gq, gk):
    qn = q * lax.rsqrt(jnp.mean(q * q, axis=-1, keepdims=True) + EPS) * gq
    kn = k * lax.rsqrt(jnp.mean(k * k, axis=-1, keepdims=True) + EPS) * gk
    s = lax.dot_general(qn.astype(BF), kn.astype(BF), (((1,), (1,)), ((), ())), preferred_element_type=F32) * (C_HEAD_DIM ** -0.5)
    p = jax.nn.softmax(s, axis=-1)
    return jnp.dot(p.astype(BF), v.astype(BF), preferred_element_type=F32)


def _c_specs(S, M, tq):
    q_col = (2 * A_WIDTH + B_WIDTH + 2 * B_KV_WIDTH) // 128
    return [pl.BlockSpec((tq, 128), lambda h, i: (i, q_col + h)), pl.BlockSpec((M, 128), lambda h, i: (0, h)),
            pl.BlockSpec((M, 128), lambda h, i: (0, C_HEADS + h)), pl.BlockSpec((1, 128), lambda h, i: (0, 0)),
            pl.BlockSpec((1, 128), lambda h, i: (0, 0))]


def _c_fwd(proj, kv, gq, gk):
    S, M = proj.shape[0], kv.shape[0]
    tq = _pick(S, (512,))

    def body(q_ref, k_ref, v_ref, gq_ref, gk_ref, y_ref):
        y_ref[...] = _c_block(q_ref[...], k_ref[...], v_ref[...], gq_ref[...], gk_ref[...]).astype(BF)

    return pl.pallas_call(
        body, name="c_fwd", grid=(C_HEADS, S // tq), in_specs=_c_specs(S, M, tq),
        out_specs=pl.BlockSpec((tq, 128), lambda h, i: (i, h)),
        out_shape=jax.ShapeDtypeStruct((S, C_WIDTH), BF), compiler_params=_params(("parallel", "parallel")),
    )(proj, kv, kv, gq, gk)


def _c_bwd(proj, kv, gq, gk, dy):
    S, M = proj.shape[0], kv.shape[0]
    tq = _pick(S, (512,))

    def body(q_ref, k_ref, v_ref, gq_ref, gk_ref, dy_ref, dq_ref, dk_ref, dv_ref, dgq_ref, dgk_ref):
        i = pl.program_id(1)
        _, vjp = jax.vjp(_c_block, q_ref[...], k_ref[...], v_ref[...], gq_ref[...], gk_ref[...])
        dq, dk, dv, dgq, dgk = vjp(dy_ref[...])
        dq_ref[...] = dq.astype(BF)

        @pl.when(i == 0)
        def _():
            dk_ref[...] = dk
            dv_ref[...] = dv
            dgq_ref[...] = dgq
            dgk_ref[...] = dgk

        @pl.when(i > 0)
        def _():
            dk_ref[...] += dk
            dv_ref[...] += dv
            dgq_ref[...] += dgq
            dgk_ref[...] += dgk

    return pl.pallas_call(
        body, name="c_bwd", grid=(C_HEADS, S // tq),
        in_specs=_c_specs(S, M, tq) + [pl.BlockSpec((tq, 128), lambda h, i: (i, h))],
        out_specs=[pl.BlockSpec((tq, 128), lambda h, i: (i, h)), pl.BlockSpec((M, 128), lambda h, i: (0, h)),
                   pl.BlockSpec((M, 128), lambda h, i: (0, h)), pl.BlockSpec((None, 1, 128), lambda h, i: (h, 0, 0)),
                   pl.BlockSpec((None, 1, 128), lambda h, i: (h, 0, 0))],
        out_shape=[jax.ShapeDtypeStruct((S, C_WIDTH), BF), jax.ShapeDtypeStruct((M, C_WIDTH), F32), jax.ShapeDtypeStruct((M, C_WIDTH), F32),
                   jax.ShapeDtypeStruct((C_HEADS, 1, 128), F32), jax.ShapeDtypeStruct((C_HEADS, 1, 128), F32)],
        compiler_params=_params(("parallel", "arbitrary")),
    )(proj, kv, kv, gq, gk, dy)


def _merge_specs(S, D, tr, tc):
    off = GATE_OFF // tc
    nd = D // tc
    gates = [pl.BlockSpec((tr, tc), functools.partial(lambda b, i, j: (i, off + b * nd + j), b)) for b in range(3)]
    zs = [pl.BlockSpec((tr, tc), lambda i, j: (i, j)) for _ in range(3)]
    return gates + zs


def _merge_fwd(proj, za, zb, zc):
    S, D = za.shape
    tr, tc = _pick(S, (512,)), _pick(D, (256,))

    def body(ga_ref, gb_ref, gc_ref, za_ref, zb_ref, zc_ref, m_ref):
        acc = jax.nn.sigmoid(ga_ref[...]) * za_ref[...].astype(F32)
        acc = acc + jax.nn.sigmoid(gb_ref[...]) * zb_ref[...].astype(F32)
        acc = acc + jax.nn.sigmoid(gc_ref[...]) * zc_ref[...].astype(F32)
        m_ref[...] = acc.astype(BF)

    return pl.pallas_call(
        body, name="merge_fwd", grid=(S // tr, D // tc), in_specs=_merge_specs(S, D, tr, tc),
        out_specs=pl.BlockSpec((tr, tc), lambda i, j: (i, j)), out_shape=jax.ShapeDtypeStruct((S, D), BF),
        compiler_params=_params(("parallel", "parallel")),
    )(proj, proj, proj, za, zb, zc)


def _merge_bwd(proj, za, zb, zc, dm, after=()):
    S, D = za.shape
    tr, tc = _pick(S, (512,)), _pick(D, (256,))
    nd = D // tc

    def body(ga_ref, gb_ref, gc_ref, za_ref, zb_ref, zc_ref, dm_ref, dza_ref, dzb_ref, dzc_ref, dga_ref, dgb_ref, dgc_ref):
        dmv = dm_ref[...]
        for g_ref, z_ref, dz_ref, dg_ref in ((ga_ref, za_ref, dza_ref, dga_ref), (gb_ref, zb_ref, dzb_ref, dgb_ref),
                                             (gc_ref, zc_ref, dzc_ref, dgc_ref)):
            sg = jax.nn.sigmoid(g_ref[...])
            dz_ref[...] = (sg * dmv).astype(BF)
            dg_ref[...] = (dmv * z_ref[...].astype(F32) * sg * (1.0 - sg)).astype(BF)

    tile = pl.BlockSpec((tr, tc), lambda i, j: (i, j))
    return pl.pallas_call(
        _hide(body, 7, len(after)), name="merge_bwd", grid=(S // tr, D // tc),
        in_specs=_merge_specs(S, D, tr, tc) + [tile] + _hidden_specs(after),
        out_specs=[tile, tile, tile, tile, tile, tile],
        out_shape=[jax.ShapeDtypeStruct((S, D), BF)] * 6,
        compiler_params=_params(("parallel", "parallel")),
    )(proj, proj, proj, za, zb, zc, dm, *after)


PAD = 8


def _stage_shift_down(us_ref, u_ref):
    S = u_ref.shape[1]
    us_ref[:, 0:PAD, :] = jnp.zeros((2, PAD, us_ref.shape[2]), F32)
    us_ref[:, PAD:S + PAD, :] = u_ref[...]


ROWS = 32


def _conv3(us_ref, part, r0, w, b):
    return (us_ref[part, pl.ds(r0 + PAD, ROWS), :] * w[2:3] + us_ref[part, pl.ds(r0 + PAD - 1, ROWS), :] * w[1:2]
            + us_ref[part, pl.ds(r0 + PAD - 2, ROWS), :] * w[0:1] + b)


def _ffn_specs(S, F, tc, c):
    per = c // tc

    def w_spec(half):
        return pl.BlockSpec((None, 3, tc), lambda j: (half * (N_DEV // 2) + j // per, 0, j % per))

    return [pl.BlockSpec((2, S, tc), lambda j: (0, 0, j)), w_spec(0), w_spec(1), pl.BlockSpec((2, 1, tc), lambda j: (0, 0, j))]


def _ffn_tile(F, c):
    tc = 128
    if c % tc or F % tc:
        raise ValueError(f"ffn tile {tc} does not divide {c}, {F}")
    return tc


def _ffn_act_fwd(up3, cws, cb3):
    _, S, F = up3.shape
    c = cws.shape[2]
    tc = _ffn_tile(F, c)

    def body(u_ref, wa_ref, wb_ref, b_ref, o_ref, us_ref):
        _stage_shift_down(us_ref, u_ref)
        wa, wb, ba, bb = wa_ref[...], wb_ref[...], b_ref[0], b_ref[1]

        def step(i, carry):
            r0 = pl.multiple_of(i * ROWS, ROWS)
            ca = _conv3(us_ref, 0, r0, wa, ba)
            cb = _conv3(us_ref, 1, r0, wb, bb)
            o_ref[pl.ds(r0, ROWS), :] = (ca * jax.nn.sigmoid(ca) * cb).astype(BF)
            return carry

        lax.fori_loop(0, S // ROWS, step, 0, unroll=4)

    return pl.pallas_call(
        body, name="ffn_act_fwd", grid=(F // tc,), in_specs=_ffn_specs(S, F, tc, c),
        out_specs=pl.BlockSpec((S, tc), lambda j: (0, j)), out_shape=jax.ShapeDtypeStruct((S, F), BF),
        scratch_shapes=[pltpu.VMEM((2, S + PAD, tc), F32)],
        compiler_params=_params(("parallel",)),
    )(up3, cws, cws, cb3)


def _ffn_act_bwd(up3, cws, cb3, dact, after=()):
    _, S, F = up3.shape
    c = cws.shape[2]
    tc = _ffn_tile(F, c)

    def body(u_ref, wa_ref, wb_ref, b_ref, da_ref, du_ref, dw_ref, db_ref, us_ref, dcs_ref):
        _stage_shift_down(us_ref, u_ref)
        ws = (wa_ref[...], wb_ref[...])
        ba, bb = b_ref[0], b_ref[1]
        dcs_ref[:, S:S + PAD, :] = jnp.zeros((2, PAD, tc), F32)

        def conv_grads(i, carry):
            r0 = pl.multiple_of(i * ROWS, ROWS)
            ca = _conv3(us_ref, 0, r0, ws[0], ba)
            cb = _conv3(us_ref, 1, r0, ws[1], bb)
            sg = jax.nn.sigmoid(ca)
            dav = da_ref[pl.ds(r0, ROWS), :]
            dcs_ref[0, pl.ds(r0, ROWS), :] = dav * cb * sg * (1.0 + ca * (1.0 - sg))
            dcs_ref[1, pl.ds(r0, ROWS), :] = dav * ca * sg
            return carry

        lax.fori_loop(0, S // ROWS, conv_grads, 0, unroll=4)

        def fold(v):
            return jnp.sum(v.reshape(ROWS // 8, 8, tc), axis=0)

        def input_grads(i, acc):
            r0 = pl.multiple_of(i * ROWS, ROWS)
            new = []
            for part in range(2):
                w = ws[part]
                dc = dcs_ref[part, pl.ds(r0, ROWS), :]
                dc1 = dcs_ref[part, pl.ds(r0 + 1, ROWS), :]
                dc2 = dcs_ref[part, pl.ds(r0 + 2, ROWS), :]
                u = u_ref[part, pl.ds(r0, ROWS), :]
                du_ref[part, pl.ds(r0, ROWS), :] = (dc * w[2:3] + dc1 * w[1:2] + dc2 * w[0:1]).astype(BF)
                sums = (fold(dc2 * u), fold(dc1 * u), fold(dc * u), fold(dc))
                new += [a + s for a, s in zip(acc[4 * part:4 * part + 4], sums)]
            return tuple(new)

        acc = lax.fori_loop(0, S // ROWS, input_grads, tuple(jnp.zeros((8, tc), F32) for _ in range(8)), unroll=4)
        for part in range(2):
            for j in range(3):
                dw_ref[part, j:j + 1, :] = jnp.sum(acc[4 * part + j], axis=0, keepdims=True)
            db_ref[part] = jnp.sum(acc[4 * part + 3], axis=0, keepdims=True)

    return pl.pallas_call(
        _hide(body, 5, len(after)), name="ffn_act_bwd", grid=(F // tc,),
        in_specs=_ffn_specs(S, F, tc, c) + [pl.BlockSpec((S, tc), lambda j: (0, j))] + _hidden_specs(after),
        out_specs=[pl.BlockSpec((2, S, tc), lambda j: (0, 0, j)), pl.BlockSpec((2, 3, tc), lambda j: (0, 0, j)),
                   pl.BlockSpec((2, 1, tc), lambda j: (0, 0, j))],
        out_shape=[jax.ShapeDtypeStruct((2, S, F), BF), jax.ShapeDtypeStruct((2, 3, F), F32), jax.ShapeDtypeStruct((2, 1, F), F32)],
        scratch_shapes=[pltpu.VMEM((2, S + PAD, tc), F32), pltpu.VMEM((2, S + PAD, tc), F32)],
        compiler_params=_params(("parallel",)),
    )(up3, cws, cws, cb3, dact, *after)


def _loss(y, target):
    S, D = y.shape
    tr = _pick(S, (256,))

    def body(y_ref, t_ref, dy_ref, dyb_ref, l_ref):
        i = pl.program_id(0)
        e = y_ref[...] - t_ref[...]
        dy = e * (1.0 / D)
        dy_ref[...] = dy
        dyb_ref[...] = dy.astype(BF)
        part = jnp.sum(jnp.sum(e * e, axis=-1, keepdims=True), axis=0, keepdims=True) * (0.5 / D)

        @pl.when(i == 0)
        def _():
            l_ref[...] = jnp.zeros_like(l_ref)

        l_ref[...] += part

    row = pl.BlockSpec((tr, D), lambda i: (i, 0))
    return pl.pallas_call(
        body, name="loss", grid=(S // tr,), in_specs=[row, row],
        out_specs=[row, row, pl.BlockSpec((8, 128), lambda i: (0, 0))],
        out_shape=[jax.ShapeDtypeStruct((S, D), F32), jax.ShapeDtypeStruct((S, D), BF), jax.ShapeDtypeStruct((8, 128), F32)],
        compiler_params=_params(("arbitrary",)),
    )(y, target)


ANY = pl.BlockSpec(memory_space=pl.ANY)


def _allgather(shards, name):
    n = len(shards)

    def body(*refs):
        ins, outs = refs[:n], refs[n:2 * n]
        send_sems, recv_sems, local_sems = refs[2 * n:]
        x, y, c = lax.axis_index("x"), lax.axis_index("y"), lax.axis_index("c")
        me, sibling = (x, y, c), (x, y, 1 - c)
        chips = [(1 - x, y), (x, 1 - y), (1 - x, 1 - y)]

        def blk(w, px, py, pc):
            return outs[w].at[4 * px + 2 * py + pc]

        def copy(w, k, block, to, src=None):
            return pltpu.make_async_remote_copy(
                src_ref=blk(w, *block) if src is None else src, dst_ref=blk(w, *block),
                send_sem=send_sems.at[w, k], recv_sem=recv_sems.at[w, k], device_id=to, device_id_type=MESH)

        started = []
        mine = []
        for w in range(n):
            mine.append(pltpu.make_async_copy(ins[w], blk(w, *me), local_sems.at[w]))
            mine[-1].start()
            first = [copy(w, 0, me, sibling, src=ins[w])]
            first += [copy(w, 1 + j, me, (*chip, c), src=ins[w]) for j, chip in enumerate(chips)]
            for cp in first:
                cp.start()
            started += first
        for w in range(n):
            for j, chip in enumerate(chips):
                copy(w, 1 + j, (*chip, c), me).wait_recv()
                fwd = copy(w, 4 + j, (*chip, c), sibling)
                fwd.start()
                started.append(fwd)
        for w in range(n):
            copy(w, 0, sibling, me).wait_recv()
            for j, chip in enumerate(chips):
                copy(w, 4 + j, (*chip, 1 - c), me).wait_recv()
        for cp in started:
            cp.wait_send()
        for cp in mine:
            cp.wait()

    whole = pl.BlockSpec(memory_space=pltpu.VMEM)
    outs = pl.pallas_call(
        body, name=name, in_specs=[whole] * n, out_specs=[whole] * n,
        out_shape=[jax.ShapeDtypeStruct((N_DEV,) + s.shape, s.dtype) for s in shards],
        scratch_shapes=[pltpu.SemaphoreType.DMA((n, 7)), pltpu.SemaphoreType.DMA((n, 7)), pltpu.SemaphoreType.DMA((n,))],
    )(*shards)
    return list(outs)


def _allgather_seq(shards, name, collective_id, after=()):
    n = len(shards)
    n_after = len(after)

    def body(*refs):
        ins, outs = refs[:n], refs[n + n_after:2 * n + n_after]
        send_sems, recv_sems, local_sems = refs[2 * n + n_after:]
        x, y, c = lax.axis_index("x"), lax.axis_index("y"), lax.axis_index("c")
        me, sibling = (x, y, c), (x, y, 1 - c)
        chips = [(1 - x, y), (x, 1 - y), (1 - x, 1 - y)]
        barrier = pltpu.get_barrier_semaphore()
        for peer in [sibling] + [(*chip, c) for chip in chips]:
            pl.semaphore_signal(barrier, inc=1, device_id=peer, device_id_type=MESH)
        pl.semaphore_wait(barrier, 4)

        def blk(w, px, py, pc):
            return outs[w].at[4 * px + 2 * py + pc]

        def copy(w, k, block, to, src=None):
            return pltpu.make_async_remote_copy(
                src_ref=blk(w, *block) if src is None else src, dst_ref=blk(w, *block),
                send_sem=send_sems.at[7 * w + k], recv_sem=recv_sems.at[7 * w + k], device_id=to, device_id_type=MESH)

        started = []
        mine = []
        for w in range(n):
            mine.append(pltpu.make_async_copy(ins[w], blk(w, *me), local_sems.at[w]))
            mine[-1].start()
            first = [copy(w, 0, me, sibling, src=ins[w])]
            first += [copy(w, 1 + j, me, (*chip, c), src=ins[w]) for j, chip in enumerate(chips)]
            for cp in first:
                cp.start()
            started += first
        for w in range(n):
            for j, chip in enumerate(chips):
                copy(w, 1 + j, (*chip, c), me).wait_recv()
                fwd = copy(w, 4 + j, (*chip, c), sibling)
                fwd.start()
                started.append(fwd)
        for w in range(n):
            copy(w, 0, sibling, me).wait_recv()
            for j, chip in enumerate(chips):
                copy(w, 4 + j, (*chip, 1 - c), me).wait_recv()
        for cp in started:
            cp.wait_send()
        for cp in mine:
            cp.wait()

    outs = pl.kernel(
        body, name=name, out_type=[jax.ShapeDtypeStruct((N_DEV,) + s.shape, s.dtype) for s in shards],
        mesh=plsc.ScalarSubcoreMesh(axis_name="seq", num_cores=1),
        scratch_types=[pltpu.SemaphoreType.DMA((7 * n,)), pltpu.SemaphoreType.DMA((7 * n,)), pltpu.SemaphoreType.DMA((n,))],
        compiler_params=pltpu.CompilerParams(collective_id=collective_id),
    )(*shards, *after)
    return list(outs)


def _chip_exchange(sums, name, collective_id):
    n = len(sums)

    def body(*refs):
        ins, outs = refs[:n], refs[n:2 * n]
        send_sems, recv_sems = refs[2 * n:]
        x, y, c = lax.axis_index("x"), lax.axis_index("y"), lax.axis_index("c")
        chips = [(1 - x, y), (x, 1 - y), (1 - x, 1 - y)]
        barrier = pltpu.get_barrier_semaphore()
        for px, py in chips:
            pl.semaphore_signal(barrier, inc=1, device_id=(px, py, c), device_id_type=MESH)
        pl.semaphore_wait(barrier, 3)
        copies = []
        for w in range(n):
            for k, (px, py) in enumerate(chips):
                copies.append(pltpu.make_async_remote_copy(
                    src_ref=ins[w].at[2 * px + py], dst_ref=outs[w].at[k], send_sem=send_sems.at[3 * w + k],
                    recv_sem=recv_sems.at[3 * w + k], device_id=(px, py, c), device_id_type=MESH))
        for cp in copies:
            cp.start()
        for cp in copies:
            cp.wait()

    outs = pl.kernel(
        body, name=name, out_type=[jax.ShapeDtypeStruct((3,) + s.shape[1:], s.dtype) for s in sums],
        mesh=plsc.ScalarSubcoreMesh(axis_name="seq", num_cores=1),
        scratch_types=[pltpu.SemaphoreType.DMA((3 * n,)), pltpu.SemaphoreType.DMA((3 * n,))],
        compiler_params=pltpu.CompilerParams(collective_id=collective_id),
    )(*sums)
    return list(outs)


def _row_tile(r, c, elems=256 * 1024):
    want = max(8, elems // c)
    for t in range(min(want, r) // 8 * 8, 0, -8):
        if r % t == 0:
            return t
    return r


def _pair_add(g4, recv, core, name, after=()):
    _, _, r, c = g4.shape
    tr = _row_tile(r, c, 1024 * 1024)

    def body(core_ref, a_ref, b_ref, o_ref):
        o_ref[...] = (a_ref[...].astype(F32) + b_ref[...].astype(F32)).astype(BF)

    return pl.pallas_call(
        _hide(body, 3, len(after)), name=name,
        grid_spec=pltpu.PrefetchScalarGridSpec(
            num_scalar_prefetch=1, grid=(4, r // tr),
            in_specs=[pl.BlockSpec((None, None, tr, c), lambda p, i, s: (p, s[0], i, 0)),
                      pl.BlockSpec((None, tr, c), lambda p, i, s: (p, i, 0))] + _hidden_specs(after),
            out_specs=pl.BlockSpec((None, tr, c), lambda p, i, s: (p, i, 0))),
        out_shape=jax.ShapeDtypeStruct((4, r, c), BF), compiler_params=_params(("parallel", "parallel")),
    )(core, g4, recv, *after)


def _adam_math(w, g, m, v):
    m = ADAM_B1 * m + (1.0 - ADAM_B1) * g
    v = ADAM_B2 * v + (1.0 - ADAM_B2) * (g * g)
    m_hat = m / (1.0 - ADAM_B1 ** ADAM_STEP)
    v_hat = v / (1.0 - ADAM_B2 ** ADAM_STEP)
    delta = -ADAM_LR * (m_hat / (jnp.sqrt(v_hat) + ADAM_EPS) + ADAM_WD * w)
    return delta, m, v


def _adamw_big(sums, recv, chip, w, m, v, name, after=()):
    r, c = w.shape
    tr = _row_tile(r, c, 256 * 1024)

    def body(chip_ref, s_ref, r_ref, w_ref, m_ref, v_ref, g_out, d_out, m_out, v_out):
        g = s_ref[...].astype(F32) + r_ref[0].astype(F32)
        g = g + r_ref[1].astype(F32)
        g = g + r_ref[2].astype(F32)
        delta, mn, vn = _adam_math(w_ref[...], g, m_ref[...], v_ref[...])
        g_out[...] = g
        d_out[...] = delta
        m_out[...] = mn
        v_out[...] = vn

    row = pl.BlockSpec((tr, c), lambda i, s: (i, 0))
    return pl.pallas_call(
        _hide(body, 6, len(after)), name=name,
        grid_spec=pltpu.PrefetchScalarGridSpec(
            num_scalar_prefetch=1, grid=(r // tr,),
            in_specs=[pl.BlockSpec((None, tr, c), lambda i, s: (s[0], i, 0)), pl.BlockSpec((3, tr, c), lambda i, s: (0, i, 0)),
                      row, row, row] + _hidden_specs(after),
            out_specs=[row, row, row, row]),
        out_shape=[jax.ShapeDtypeStruct((r, c), F32)] * 4, compiler_params=_params(("parallel",)),
    )(chip, sums, recv, w, m, v, *after)


def _adamw_small(parts, ws, ms, vs, extra_parts, name):
    n, ne = len(ws), len(extra_parts)

    def total(p_ref):
        g = p_ref[0]
        for d in range(1, N_DEV):
            g = g + p_ref[d]
        return g

    def body(*refs):
        p_refs, w_refs, m_refs, v_refs = refs[:n], refs[n:2 * n], refs[2 * n:3 * n], refs[3 * n:4 * n]
        e_refs = refs[4 * n:4 * n + ne]
        outs = refs[4 * n + ne:]
        for i in range(n):
            g = total(p_refs[i])
            delta, mn, vn = _adam_math(w_refs[i][...], g, m_refs[i][...], v_refs[i][...])
            outs[4 * i][...] = g
            outs[4 * i + 1][...] = delta
            outs[4 * i + 2][...] = mn
            outs[4 * i + 3][...] = vn
        for i in range(ne):
            outs[4 * n + i][...] = total(e_refs[i])

    out_shape = []
    for w in ws:
        out_shape += [jax.ShapeDtypeStruct(w.shape, F32)] * 4
    out_shape += [jax.ShapeDtypeStruct(e.shape[1:], F32) for e in extra_parts]
    res = pl.pallas_call(body, name=name, out_shape=out_shape,
                         compiler_params=pltpu.CompilerParams(vmem_limit_bytes=VMEM_LIMIT))(*parts, *ws, *ms, *vs, *extra_parts)
    return [res[4 * i:4 * i + 4] for i in range(n)], list(res[4 * n:])


def _adamw_plain(g, w, m, v, name):
    def body(g_ref, w_ref, m_ref, v_ref, d_out, m_out, v_out):
        delta, mn, vn = _adam_math(w_ref[...], g_ref[...], m_ref[...], v_ref[...])
        d_out[...] = delta
        m_out[...] = mn
        v_out[...] = vn

    return pl.pallas_call(body, name=name, out_shape=[jax.ShapeDtypeStruct(w.shape, F32)] * 3)(g, w, m, v)


def kernel(x, mem, positions, g_mix, w_in, g_a_v, w_spatial, b_spatial, g_b_q, g_b_k, sinks, g_mem, w_mem_kv, g_c_q, g_c_k, w_branch_a, w_branch_b, w_branch_c, w_out, g_ffn, w_up, conv_w, conv_b, w_down, loss_target, m_g_mix, m_w_in, m_g_a_v, m_w_spatial, m_b_spatial, m_g_b_q, m_g_b_k, m_sinks, m_g_mem, m_w_mem_kv, m_g_c_q, m_g_c_k, m_w_branch_a, m_w_branch_b, m_w_branch_c, m_w_out, m_g_ffn, m_w_up, m_conv_w, m_conv_b, m_w_down, v_g_mix, v_w_in, v_g_a_v, v_w_spatial, v_b_spatial, v_g_b_q, v_g_b_k, v_sinks, v_g_mem, v_w_mem_kv, v_g_c_q, v_g_c_k, v_w_branch_a, v_w_branch_b, v_w_branch_c, v_w_out, v_g_ffn, v_w_up, v_conv_w, v_conv_b, v_w_down):
    S, D = x.shape[1], x.shape[2]
    M = mem.shape[1]
    F = w_down.shape[1] * N_DEV
    in_cols = w_in.shape[2] * N_DEV
    ax, ay, ac = lax.axis_index("x"), lax.axis_index("y"), lax.axis_index("c")
    core = jnp.reshape(ac, (1,)).astype(jnp.int32)
    chip = jnp.reshape(2 * ax + ay, (1,)).astype(jnp.int32)
    me = 4 * ax + 2 * ay + ac

    x2, mem2, tgt2 = x[0], mem[0], loss_target[0]

    big = dict(w_in=w_in[0].T, w_mem_kv=w_mem_kv[0], w_branch_a=w_branch_a[0], w_branch_b=w_branch_b[0],
               w_branch_c=w_branch_c[0], w_out=w_out[0], w_up=w_up[0], w_down=w_down[0])
    names = list(big)
    cast = {k: big[k].astype(BF) for k in names}
    W = {}
    cb3 = conv_b.reshape(2, 1, F)
    W["w_in"], = _allgather_seq([cast["w_in"]], "ag_seq0", 0)
    w_in_t = W["w_in"].reshape(in_cols, D)
    grp1 = ["w_mem_kv", "w_branch_a", "w_branch_b", "w_branch_c", "w_out"]
    res1 = _allgather_seq([cast[k] for k in grp1] + [conv_w[0]], "ag_seq1", 1, after=(_token((w_in_t,), "tok_w_in"),))
    W.update(zip(grp1, res1))
    cw3 = res1[-1]
    w_kv_f = W["w_mem_kv"].reshape(D, 2 * C_WIDTH)
    w_out_f = W["w_out"].reshape(D, D)

    half = ROPE_DIM // 2
    inv = ROPE_THETA ** (-jnp.arange(half, dtype=F32) / half)
    ang = positions[0].astype(F32)[:, None] * inv
    cos, sin = jnp.cos(ang), jnp.sin(ang)
    one, zero = jnp.ones((S, B_HEAD_DIM - ROPE_DIM), F32), jnp.zeros((S, B_HEAD_DIM - ROPE_DIM), F32)
    z8 = jnp.zeros((S, half), F32)
    ct = jnp.tile(jnp.concatenate([cos, cos, one], axis=1), (1, 2))
    sa = jnp.tile(jnp.concatenate([-sin, z8, zero], axis=1), (1, 2))
    sb = jnp.tile(jnp.concatenate([z8, sin, zero], axis=1), (1, 2))
    gq2, gk2 = jnp.tile(g_b_q, (1, 2)), jnp.tile(g_b_k, (1, 2))
    b_t = b_spatial[0].T

    h, rstd1 = _rms_fwd(x2, g_mix, "rms1_fwd")
    proj = _mm(h, w_in_t, "nt", F32, "mm_proj", tn=1280)
    y_a = _a_fwd(proj, g_a_v, w_spatial[0], b_t)
    qn, kn = _b_pre(proj, gq2, gk2, ct, sa, sb)
    W["w_up"], = _allgather_seq([cast["w_up"]], "ag_seq2", 2, after=(_token((W["w_out"], qn), "tok_group1"),))
    y_b = _b_attn_fwd(qn, kn, proj, sinks)
    mem_h, rstd_m = _rms_fwd(mem2, g_mem, "rmsmem_fwd")
    kv = _mm(mem_h, w_kv_f, "nn", F32, "mm_kv", after=(y_b,))
    y_c = _c_fwd(proj, kv, g_c_q, g_c_k)
    z_a = _mm(y_a, W["w_branch_a"], "nn", BF, "mm_za", b_stack=True, after=(y_b,))
    z_b = _mm(y_b, W["w_branch_b"], "nn", BF, "mm_zb", b_stack=True)
    z_c = _mm(y_c, W["w_branch_c"], "nn", BF, "mm_zc", b_stack=True)
    merged = _merge_fwd(proj, z_a, z_b, z_c)
    x1 = _mm(merged, w_out_f, "nn", F32, "mm_x1", resid=x2)
    h2, rstd2 = _rms_fwd(x1, g_ffn, "rms2_fwd")
    W["w_down"], = _allgather_seq([cast["w_down"]], "ag_seq3", 3, after=(W["w_up"], h2))
    w_down_f = W["w_down"].reshape(F, D)
    up3 = _mm(h2, W["w_up"], "nn", F32, "mm_up", b_stack=True, out_parts=2)
    act = _ffn_act_fwd(up3, cw3, cb3)
    y = _mm(act, w_down_f, "nn", F32, "mm_y", resid=x1, tk=1408)
    dy, dy_b, loss_acc = _loss(y, tgt2)
    loss = lax.psum(loss_acc[0, 0], ("x", "y", "c"))

    reduced = {}

    def as4(g):
        return g.reshape(4, 2, g.shape[1], g.shape[2])

    def finish_group(gi, keys, g4, from_sibling):
        sums = [_pair_add(a, b, core, "rs_add_" + k) for k, a, b in zip(keys, g4, from_sibling)]
        from_chips = _chip_exchange(sums, f"rs_chip{gi}", 4 + gi)
        reduced.update(zip(keys, zip(sums, from_chips)))
        return tuple(sums)

    d_act = _mm(dy_b, w_down_f, "nt", F32, "mm_dact", tn=1408)
    g_down = _mm(act, dy_b, "tn", BF, "mm_gdown", tm=1408)
    d_up3, d_cw3, d_cb3 = _ffn_act_bwd(up3, cw3, cb3, d_act, after=(g_down,))
    grp0 = [as4(g_down.reshape(N_DEV, F // N_DEV, D))]
    g_up, sib0 = _mm(h2, d_up3, "tn", BF, "mm_gup", b_parts=2, out_stack=True, exchange=grp0)
    sums0 = finish_group(0, ["w_down"], grp0, sib0)
    grp1 = [as4(g_up)]
    d_h2, sib1 = _mm(d_up3, W["w_up"], "nt", F32, "mm_dh2", a_parts=2, b_stack=True, after=sums0, exchange=grp1)
    sums1 = finish_group(1, ["w_up"], grp1, sib1)
    dx1, dx1_b, d_g_ffn = _rms_bwd(x1, rstd2, g_ffn, d_h2, dy, "rms2_bwd", after=sums1)
    d_merged = _mm(dx1_b, w_out_f, "nt", F32, "mm_dmerged")
    g_out = _mm(merged, dx1_b, "tn", BF, "mm_gout")
    dz_a, dz_b, dz_c, dga, dgb, dgc = _merge_bwd(proj, z_a, z_b, z_c, d_merged, after=(g_out,))
    g_ba = _mm(y_a, dz_a, "tn", BF, "mm_gba", out_stack=True)
    g_bb = _mm(y_b, dz_b, "tn", BF, "mm_gbb", out_stack=True)
    g_bc = _mm(y_c, dz_c, "tn", BF, "mm_gbc", out_stack=True)
    grp2 = [as4(g_out.reshape(N_DEV, D // N_DEV, D)), as4(g_ba), as4(g_bb), as4(g_bc)]
    dy_a, sib2 = _mm(dz_a, W["w_branch_a"], "nt", F32, "mm_dya", b_stack=True, exchange=grp2)
    sums2 = finish_group(2, ["w_out", "w_branch_a", "w_branch_b", "w_branch_c"], grp2, sib2)
    dy_b_ = _mm(dz_b, W["w_branch_b"], "nt", F32, "mm_dyb", b_stack=True, after=sums2)
    dy_c = _mm(dz_c, W["w_branch_c"], "nt", F32, "mm_dyc", b_stack=True)
    d_uv, d_g_a_v, d_w_s, d_b_t = _a_bwd(proj, g_a_v, w_spatial[0], b_t, dy_a)
    dqn, dkn, dv_b, dsink_rows = _b_attn_bwd(qn, kn, proj, sinks, dy_b_)
    d_qkv, d_gq2, d_gk2 = _b_pre_bwd(proj, gq2, gk2, ct, sa, sb, dqn, dkn, dv_b)
    dq_c, dk_c, dv_c, d_gcq, d_gck = _c_bwd(proj, kv, g_c_q, g_c_k, dy_c)
    dkv_b = jnp.concatenate([dk_c, dv_c], axis=1).astype(BF)
    d_memh = _mm(dkv_b, w_kv_f, "nt", F32, "mm_dmemh")
    g_kv = _mm(mem_h, dkv_b, "tn", BF, "mm_gkv")
    _, _, d_g_mem = _rms_bwd(mem2, rstd_m, g_mem, d_memh, None, "rmsmem_bwd")
    dproj = jnp.concatenate([d_uv, d_qkv, dq_c, dga, dgb, dgc], axis=1)
    g_in = _mm(dproj, h, "tn", BF, "mm_gin", tm=1280)
    grp3 = [as4(g_in.reshape(N_DEV, in_cols // N_DEV, D)), as4(g_kv.reshape(N_DEV, D // N_DEV, 2 * C_WIDTH))]
    d_h, sib3 = _mm(dproj, w_in_t, "nn", F32, "mm_dh", tk=1792, exchange=grp3)
    sums3 = finish_group(3, ["w_in", "w_mem_kv"], grp3, sib3)
    grad_x, _, d_g_mix = _rms_bwd(x2, rstd1, g_mix, d_h, dx1, "rms1_bwd", after=sums3)

    small_names =["g_mix", "g_a_v", "w_spatial", "b_spatial", "g_b_q", "g_b_k", "sinks", "g_mem", "g_c_q", "g_c_k", "g_ffn", "conv_b"]
    small_w = dict(g_mix=g_mix, g_a_v=g_a_v, w_spatial=w_spatial, b_spatial=b_spatial, g_b_q=g_b_q, g_b_k=g_b_k, sinks=sinks,
                   g_mem=g_mem, g_c_q=g_c_q, g_c_k=g_c_k, g_ffn=g_ffn, conv_b=conv_b)
    small_m = dict(g_mix=m_g_mix, g_a_v=m_g_a_v, w_spatial=m_w_spatial, b_spatial=m_b_spatial, g_b_q=m_g_b_q, g_b_k=m_g_b_k,
                   sinks=m_sinks, g_mem=m_g_mem, g_c_q=m_g_c_q, g_c_k=m_g_c_k, g_ffn=m_g_ffn, conv_b=m_conv_b)
    small_v = dict(g_mix=v_g_mix, g_a_v=v_g_a_v, w_spatial=v_w_spatial, b_spatial=v_b_spatial, g_b_q=v_g_b_q, g_b_k=v_g_b_k,
                   sinks=v_sinks, g_mem=v_g_mem, g_c_q=v_g_c_q, g_c_k=v_g_c_k, g_ffn=v_g_ffn, conv_b=v_conv_b)
    small_g = dict(
        g_mix=d_g_mix, g_a_v=d_g_a_v, w_spatial=d_w_s, b_spatial=d_b_t.T,
        g_b_q=d_gq2.reshape(2, B_HEAD_DIM).sum(0), g_b_k=d_gk2.reshape(2, B_HEAD_DIM).sum(0),
        sinks=dsink_rows.sum(0)[:B_HEADS], g_mem=d_g_mem, g_c_q=d_gcq.sum(0), g_c_k=d_gck.sum(0), g_ffn=d_g_ffn,
        conv_b=d_cb3)
    partial = [small_g[k].reshape(small_w[k].shape) for k in small_names] + [d_cw3]
    parts = _allgather(partial, "ag_small")
    small_res, (g_cw3,) = _adamw_small(parts[:-1], [small_w[k] for k in small_names], [small_m[k] for k in small_names],
                                       [small_v[k] for k in small_names], parts[-1:], "adamw_small")
    small_out = dict(zip(small_names, small_res))
    c_cw = 2 * F // N_DEV
    g_cw = lax.dynamic_slice(g_cw3, (me // (N_DEV // 2), 0, (me % (N_DEV // 2)) * c_cw), (1, 3, c_cw))[0]
    cw_res = _adamw_plain(g_cw, conv_w[0], m_conv_w[0], v_conv_w[0], "adamw_conv_w")
    big_out = {"conv_w": [g_cw[None]] + [a[None] for a in cw_res]}

    moments = dict(w_in=(m_w_in, v_w_in), w_mem_kv=(m_w_mem_kv, v_w_mem_kv), w_branch_a=(m_w_branch_a, v_w_branch_a),
                   w_branch_b=(m_w_branch_b, v_w_branch_b), w_branch_c=(m_w_branch_c, v_w_branch_c), w_out=(m_w_out, v_w_out),
                   w_up=(m_w_up, v_w_up), w_down=(m_w_down, v_w_down))
    token = (grad_x, small_res[0][0])
    for k in ["w_down", "w_up", "w_out", "w_branch_a", "w_branch_b", "w_branch_c", "w_mem_kv", "w_in"]:
        s, r = reduced[k]
        mk, vk = moments[k][0][0], moments[k][1][0]
        if k == "w_in":
            res = _adamw_big(s, r, chip, big[k], mk.T, vk.T, "adamw_" + k, after=token)
            big_out[k] = [a.T[None] for a in res]
        else:
            res = _adamw_big(s, r, chip, big[k], mk, vk, "adamw_" + k, after=token)
            big_out[k] = [a[None] for a in res]
        token = (res[0],)

    order = ["g_mix", "w_in", "g_a_v", "w_spatial", "b_spatial", "g_b_q", "g_b_k", "sinks", "g_mem", "w_mem_kv", "g_c_q", "g_c_k",
             "w_branch_a", "w_branch_b", "w_branch_c", "w_out", "g_ffn", "w_up", "conv_w", "conv_b", "w_down"]
    res = {**small_out, **big_out}
    outs = [loss, grad_x[None]]
    for field in range(4):
        outs += [res[k][field] for k in order]
    return tuple(outs)
```

```python
import functools

import jax
import jax.numpy as jnp
from jax import lax
from jax.experimental import pallas as pl
from jax.experimental.pallas import tpu as pltpu
from jax.experimental.pallas import tpu_sc as plsc

F32 = jnp.float32
BF = jnp.bfloat16
EPS = 1e-6
NEG = -1e30

N_DEV = 8
CHUNK = 128
A_GROUPS = 4
A_WIDTH = 512
B_HEADS = 16
B_KV_HEADS = 2
B_HEAD_DIM = 64
B_WIDTH = 1024
B_KV_WIDTH = 128
ROPE_DIM = 16
ROPE_THETA = 500000.0
C_HEADS = 4
C_HEAD_DIM = 128
C_WIDTH = 512
GATE_OFF = 2 * A_WIDTH + B_WIDTH + 2 * B_KV_WIDTH + C_WIDTH

ADAM_LR = 0.001
ADAM_B1 = 0.9
ADAM_B2 = 0.999
ADAM_EPS = 1e-08
ADAM_WD = 0.01
ADAM_STEP = 10

VMEM_LIMIT = 48 * 1024 * 1024
MESH = pl.DeviceIdType.MESH


def _pick(n, prefs):
    for p in prefs:
        if p <= n and n % p == 0:
            return p
    return n


def _params(sem):
    return pltpu.CompilerParams(dimension_semantics=sem, vmem_limit_bytes=VMEM_LIMIT)


def _hide(body, n_seen, n_hidden):
    if not n_hidden:
        return body

    def wrapped(*refs):
        return body(*refs[:n_seen], *refs[n_seen + n_hidden:])

    return wrapped


def _hidden_specs(after):
    return [pl.BlockSpec(memory_space=pl.ANY) for _ in after]


def _token(xs, name):
    def body(*refs):
        refs[-1][...] = jnp.zeros_like(refs[-1])

    return pl.pallas_call(body, name=name, in_specs=_hidden_specs(xs), out_shape=jax.ShapeDtypeStruct((8, 128), F32))(*xs)


def _mm(a, b, mode, out_dtype, name, *, resid=None, b_stack=False, a_parts=0, b_parts=0, out_parts=0,
        out_stack=False, tm=1024, tn=1024, tk=2048, after=(), exchange=()):
    if mode == "nn":
        M = a.shape[-2]
        K = a.shape[-1] * max(a_parts, 1)
        N = b.shape[-1] * (N_DEV if b_stack else 1)
        dims = (((1,), (0,)), ((), ()))
    elif mode == "nt":
        M = a.shape[-2]
        K = a.shape[-1] * max(a_parts, 1)
        N = b.shape[-2]
        dims = (((1,), (1,)), ((), ()))
    else:
        K = a.shape[-2]
        M = a.shape[-1]
        N = b.shape[-1] * max(b_parts, 1)
        dims = (((0,), (0,)), ((), ()))
    if b_stack and mode == "nn":
        tn = b.shape[-1]
    if b_stack and mode == "nt":
        tk = b.shape[-1]
    if out_stack:
        tn = N // N_DEV
    tm, tn, tk = _pick(M, (tm,)), _pick(N, (tn,)), _pick(K, (tk,))
    if M % tm or N % tn or K % tk:
        raise ValueError(f"{name}: tiles {tm},{tn},{tk} do not divide {M},{N},{K}")
    nm, nn, nk = M // tm, N // tn, K // tk

    def parts_idx(t, ntile, parts):
        per = ntile // parts
        return t // per, t % per

    if mode in ("nn", "nt"):
        if a_parts:
            a_spec = pl.BlockSpec((None, tm, tk), lambda m, n, k: (parts_idx(k, nk, a_parts)[0], m, parts_idx(k, nk, a_parts)[1]))
        else:
            a_spec = pl.BlockSpec((tm, tk), lambda m, n, k: (m, k))
    else:
        a_spec = pl.BlockSpec((tk, tm), lambda m, n, k: (k, m))
    if mode == "nn":
        if b_stack:
            b_spec = pl.BlockSpec((None, tk, tn), lambda m, n, k: (n, k, 0))
        else:
            b_spec = pl.BlockSpec((tk, tn), lambda m, n, k: (k, n))
    elif mode == "nt":
        if b_stack:
            b_spec = pl.BlockSpec((None, tn, tk), lambda m, n, k: (k, n, 0))
        else:
            b_spec = pl.BlockSpec((tn, tk), lambda m, n, k: (n, k))
    else:
        if b_parts:
            b_spec = pl.BlockSpec((None, tk, tn), lambda m, n, k: (parts_idx(n, nn, b_parts)[0], k, parts_idx(n, nn, b_parts)[1]))
        else:
            b_spec = pl.BlockSpec((tk, tn), lambda m, n, k: (k, n))
    if out_stack:
        out_shape = jax.ShapeDtypeStruct((N_DEV, M, tn), out_dtype)
        o_spec = pl.BlockSpec((None, tm, tn), lambda m, n, k: (n, m, 0))
    elif out_parts:
        out_shape = jax.ShapeDtypeStruct((out_parts, M, N // out_parts), out_dtype)
        o_spec = pl.BlockSpec((None, tm, tn), lambda m, n, k: (parts_idx(n, nn, out_parts)[0], m, parts_idx(n, nn, out_parts)[1]))
    else:
        out_shape = jax.ShapeDtypeStruct((M, N), out_dtype)
        o_spec = pl.BlockSpec((tm, tn), lambda m, n, k: (m, n))
    has_resid = resid is not None

    n_ex = len(exchange)
    n_in = 2 + has_resid + len(after)

    def body(*refs):
        a_ref, b_ref = refs[:2]
        r_ref = refs[2] if has_resid else None
        ex_in = refs[n_in:n_in + n_ex]
        o_ref = refs[n_in + n_ex]
        ex_out = refs[n_in + n_ex + 1:n_in + 2 * n_ex + 1]
        scratch = refs[n_in + 2 * n_ex + 1:]
        m_i, n_i, k = pl.program_id(0), pl.program_id(1), pl.program_id(2)

        def pushes():
            send_sems, recv_sems = scratch[-2:]
            x, y, c = lax.axis_index("x"), lax.axis_index("y"), lax.axis_index("c")
            return [pltpu.make_async_remote_copy(
                src_ref=ex_in[w].at[:, 1 - c], dst_ref=ex_out[w], send_sem=send_sems.at[w], recv_sem=recv_sems.at[w],
                device_id=(x, y, 1 - c), device_id_type=MESH) for w in range(n_ex)]

        if n_ex:
            @pl.when((m_i == 0) & (n_i == 0) & (k == 0))
            def _():
                for cp in pushes():
                    cp.start()

        if nk == 1:
            res = lax.dot_general(a_ref[...], b_ref[...], dims, preferred_element_type=F32)
            if has_resid:
                res = res + r_ref[...]
            o_ref[...] = res.astype(o_ref.dtype)
        else:
            acc = scratch[0]

            @pl.when(k == 0)
            def _():
                acc[...] = jnp.zeros_like(acc)

            acc[...] += lax.dot_general(a_ref[...], b_ref[...], dims, preferred_element_type=F32)

            @pl.when(k == nk - 1)
            def _():
                res = acc[...]
                if has_resid:
                    res = res + r_ref[...]
                o_ref[...] = res.astype(o_ref.dtype)

        if n_ex:
            @pl.when((m_i == nm - 1) & (n_i == nn - 1) & (k == nk - 1))
            def _():
                for cp in pushes():
                    cp.wait()

    in_specs = [a_spec, b_spec]
    args = [a, b]
    if has_resid:
        in_specs.append(pl.BlockSpec((tm, tn), lambda m, n, k: (m, n)))
        args.append(resid)
    in_specs += _hidden_specs(after) + _hidden_specs(exchange)
    args += list(after) + list(exchange)
    scratch_shapes = [pltpu.VMEM((tm, tn), F32)] if nk > 1 else []
    if not n_ex:
        return pl.pallas_call(
            body, name=name, grid=(nm, nn, nk), in_specs=in_specs, out_specs=o_spec, out_shape=out_shape,
            scratch_shapes=scratch_shapes, compiler_params=_params(("parallel", "parallel", "arbitrary")),
        )(*args)
    res = pl.pallas_call(
        body, name=name, grid=(nm, nn, nk), in_specs=in_specs, out_specs=[o_spec] + _hidden_specs(exchange),
        out_shape=[out_shape] + [jax.ShapeDtypeStruct((g.shape[0],) + g.shape[2:], g.dtype) for g in exchange],
        scratch_shapes=scratch_shapes + [pltpu.SemaphoreType.DMA((n_ex,)), pltpu.SemaphoreType.DMA((n_ex,))],
        compiler_params=_params(("arbitrary", "arbitrary", "arbitrary")),
    )(*args)
    return res[0], list(res[1:])


def _rms_fwd(x, g, name):
    R, D = x.shape
    tr = _pick(R, (256,))

    def body(x_ref, g_ref, h_ref, r_ref):
        xv = x_ref[...]
        r = lax.rsqrt(jnp.mean(xv * xv, axis=-1, keepdims=True) + EPS)
        h_ref[...] = (xv * r * g_ref[...]).astype(BF)
        r_ref[...] = r

    return pl.pallas_call(
        body, name=name, grid=(R // tr,),
        in_specs=[pl.BlockSpec((tr, D), lambda i: (i, 0)), pl.BlockSpec((1, D), lambda i: (0, 0))],
        out_specs=[pl.BlockSpec((tr, D), lambda i: (i, 0)), pl.BlockSpec((tr, 1), lambda i: (i, 0))],
        out_shape=[jax.ShapeDtypeStruct((R, D), BF), jax.ShapeDtypeStruct((R, 1), F32)],
        compiler_params=_params(("parallel",)),
    )(x, g)


def _rms_bwd(x, r, g, dh, dres, name, after=()):
    R, D = x.shape
    tr = _pick(R, (256,))
    has_res = dres is not None

    def body(*refs):
        if has_res:
            x_ref, r_ref, g_ref, dh_ref, dres_ref, dx_ref, dxb_ref, dg_ref = refs
        else:
            x_ref, r_ref, g_ref, dh_ref, dx_ref, dxb_ref, dg_ref = refs
        i = pl.program_id(0)
        xv, rv, dhv = x_ref[...], r_ref[...], dh_ref[...]
        gy = dhv * g_ref[...]
        c = jnp.sum(xv * gy, axis=-1, keepdims=True)
        dx = rv * gy - xv * (rv * rv * rv) * (c * (1.0 / D))
        if has_res:
            dx = dx + dres_ref[...]
        dx_ref[...] = dx
        dxb_ref[...] = dx.astype(BF)
        part = jnp.sum(dhv * xv * rv, axis=0, keepdims=True)

        @pl.when(i == 0)
        def _():
            dg_ref[...] = part

        @pl.when(i > 0)
        def _():
            dg_ref[...] += part

    row = pl.BlockSpec((tr, D), lambda i: (i, 0))
    in_specs = [row, pl.BlockSpec((tr, 1), lambda i: (i, 0)), pl.BlockSpec((1, D), lambda i: (0, 0)), row]
    args = [x, r, g, dh]
    if has_res:
        in_specs.append(row)
        args.append(dres)
    return pl.pallas_call(
        _hide(body, len(args), len(after)), name=name, grid=(R // tr,), in_specs=in_specs + _hidden_specs(after),
        out_specs=[row, row, pl.BlockSpec((1, D), lambda i: (0, 0))],
        out_shape=[jax.ShapeDtypeStruct((R, D), F32), jax.ShapeDtypeStruct((R, D), BF), jax.ShapeDtypeStruct((1, D), F32)],
        compiler_params=_params(("arbitrary",)),
    )(*args, *after)


def _a_chunk(us, vs, gvs, ws, bs):
    r_i = lax.broadcasted_iota(jnp.int32, (CHUNK, CHUNK), 0)
    c_i = lax.broadcasted_iota(jnp.int32, (CHUNK, CHUNK), 1)
    causal = r_i >= c_i
    vg = [jax.nn.gelu(v) for v in vs]
    ss = sum(jnp.sum(v * v, axis=-1, keepdims=True) for v in vg)
    r = lax.rsqrt(ss * (1.0 / A_WIDTH) + EPS)
    ys = []
    for g in range(A_GROUPS):
        vn = vg[g] * r * gvs[g]
        w = jnp.where(causal, ws[g], 0.0)
        s = jnp.dot(w.astype(BF), vn.astype(BF), preferred_element_type=F32) + bs[g]
        ys.append(jax.nn.gelu(us[g]) * s)
    return ys


def _a_split(u_ref, v_ref, g_ref, w_ref, b_ref):
    sl = [slice(g * 128, (g + 1) * 128) for g in range(A_GROUPS)]
    return ([u_ref[:, s] for s in sl], [v_ref[:, s] for s in sl], [g_ref[:, s] for s in sl],
            [w_ref[g] for g in range(A_GROUPS)], [b_ref[:, g:g + 1] for g in range(A_GROUPS)])


def _a_specs(S):
    return [pl.BlockSpec((CHUNK, A_WIDTH), lambda n: (n, 0)), pl.BlockSpec((CHUNK, A_WIDTH), lambda n: (n, 1)),
            pl.BlockSpec((1, A_WIDTH), lambda n: (0, 0)), pl.BlockSpec((A_GROUPS, CHUNK, CHUNK), lambda n: (0, 0, 0)),
            pl.BlockSpec((CHUNK, A_GROUPS), lambda n: (0, 0))]


def _a_fwd(proj, g_v, w_s, b_t):
    S = proj.shape[0]

    def body(u_ref, v_ref, g_ref, w_ref, b_ref, y_ref):
        ys = _a_chunk(*_a_split(u_ref, v_ref, g_ref, w_ref, b_ref))
        for g in range(A_GROUPS):
            y_ref[:, g * 128:(g + 1) * 128] = ys[g].astype(BF)

    return pl.pallas_call(
        body, name="a_fwd", grid=(S // CHUNK,), in_specs=_a_specs(S),
        out_specs=pl.BlockSpec((CHUNK, A_WIDTH), lambda n: (n, 0)),
        out_shape=jax.ShapeDtypeStruct((S, A_WIDTH), BF), compiler_params=_params(("parallel",)),
    )(proj, proj, g_v, w_s, b_t)


def _a_bwd(proj, g_v, w_s, b_t, dy, after=()):
    S = proj.shape[0]

    def body(u_ref, v_ref, g_ref, w_ref, b_ref, dy_ref, duv_ref, dg_ref, dw_ref, db_ref):
        n = pl.program_id(0)
        dys = [dy_ref[:, g * 128:(g + 1) * 128] for g in range(A_GROUPS)]
        _, vjp = jax.vjp(_a_chunk, *_a_split(u_ref, v_ref, g_ref, w_ref, b_ref))
        dus, dvs, dgs, dws, dbs = vjp(dys)

        @pl.when(n == 0)
        def _():
            dg_ref[...] = jnp.zeros_like(dg_ref)
            dw_ref[...] = jnp.zeros_like(dw_ref)
            db_ref[...] = jnp.zeros_like(db_ref)

        for g in range(A_GROUPS):
            duv_ref[:, g * 128:(g + 1) * 128] = dus[g].astype(BF)
            duv_ref[:, A_WIDTH + g * 128:A_WIDTH + (g + 1) * 128] = dvs[g].astype(BF)
            dg_ref[:, g * 128:(g + 1) * 128] += dgs[g]
            dw_ref[g] += dws[g]
            db_ref[:, g:g + 1] += dbs[g]

    return pl.pallas_call(
        _hide(body, 6, len(after)), name="a_bwd", grid=(S // CHUNK,),
        in_specs=_a_specs(S) + [pl.BlockSpec((CHUNK, A_WIDTH), lambda n: (n, 0))] + _hidden_specs(after),
        out_specs=[pl.BlockSpec((CHUNK, 2 * A_WIDTH), lambda n: (n, 0)), pl.BlockSpec((1, A_WIDTH), lambda n: (0, 0)),
                   pl.BlockSpec((A_GROUPS, CHUNK, CHUNK), lambda n: (0, 0, 0)), pl.BlockSpec((CHUNK, A_GROUPS), lambda n: (0, 0))],
        out_shape=[jax.ShapeDtypeStruct((S, 2 * A_WIDTH), BF), jax.ShapeDtypeStruct((1, A_WIDTH), F32),
                   jax.ShapeDtypeStruct((A_GROUPS, CHUNK, CHUNK), F32), jax.ShapeDtypeStruct((CHUNK, A_GROUPS), F32)],
        compiler_params=_params(("arbitrary",)),
    )(proj, proj, g_v, w_s, b_t, dy, *after)


def _half_mask(shape, which):
    lane = lax.broadcasted_iota(jnp.int32, shape, len(shape) - 1)
    return (lane >= 64) == (which == 1)


def _pair_norm_rope(x, g, ct, sa, sb):
    lo = _half_mask(x.shape, 0)
    x2 = x * x
    ss_lo = jnp.sum(jnp.where(lo, x2, 0.0), axis=-1, keepdims=True)
    ss_hi = jnp.sum(jnp.where(lo, 0.0, x2), axis=-1, keepdims=True)
    r = jnp.where(lo, lax.rsqrt(ss_lo * (1.0 / B_HEAD_DIM) + EPS), lax.rsqrt(ss_hi * (1.0 / B_HEAD_DIM) + EPS))
    xr = x * r
    xn = xr * g
    out = xn * ct + pltpu.roll(xn, 120, 1) * sa + pltpu.roll(xn, 8, 1) * sb
    return out, xr, r


def _pair_norm_rope_bwd(x, g, ct, sa, sb, dout):
    lo = _half_mask(x.shape, 0)
    _, xr, r = _pair_norm_rope(x, g, ct, sa, sb)
    dxn = dout * ct + pltpu.roll(dout * sa, 8, 1) + pltpu.roll(dout * sb, 120, 1)
    gy = dxn * g
    t = xr * gy
    c_lo = jnp.sum(jnp.where(lo, t, 0.0), axis=-1, keepdims=True)
    c_hi = jnp.sum(jnp.where(lo, 0.0, t), axis=-1, keepdims=True)
    c = jnp.where(lo, c_lo, c_hi)
    dx = r * (gy - xr * c * (1.0 / B_HEAD_DIM))
    dg = jnp.sum(dxn * xr, axis=0, keepdims=True)
    return dx, dg


def _b_pre(proj, gq2, gk2, ct, sa, sb):
    S = proj.shape[0]
    tr = _pick(S, (256,))
    n_pair = B_WIDTH // 128

    def body(q_ref, k_ref, gq_ref, gk_ref, ct_ref, sa_ref, sb_ref, qn_ref, kn_ref):
        ct_v, sa_v, sb_v = ct_ref[...], sa_ref[...], sb_ref[...]
        for p in range(n_pair):
            o, _, _ = _pair_norm_rope(q_ref[:, p * 128:(p + 1) * 128], gq_ref[...], ct_v, sa_v, sb_v)
            qn_ref[:, p * 128:(p + 1) * 128] = o.astype(BF)
        o, _, _ = _pair_norm_rope(k_ref[...], gk_ref[...], ct_v, sa_v, sb_v)
        kn_ref[...] = o.astype(BF)

    tab = pl.BlockSpec((tr, 128), lambda i: (i, 0))
    gsp = pl.BlockSpec((1, 128), lambda i: (0, 0))
    return pl.pallas_call(
        body, name="b_pre", grid=(S // tr,),
        in_specs=[pl.BlockSpec((tr, B_WIDTH), lambda i: (i, 1)), pl.BlockSpec((tr, 128), lambda i: (i, 2 * B_WIDTH // 128)),
                  gsp, gsp, tab, tab, tab],
        out_specs=[pl.BlockSpec((tr, B_WIDTH), lambda i: (i, 0)), tab],
        out_shape=[jax.ShapeDtypeStruct((S, B_WIDTH), BF), jax.ShapeDtypeStruct((S, 128), BF)],
        compiler_params=_params(("parallel",)),
    )(proj, proj, gq2, gk2, ct, sa, sb)


def _b_pre_bwd(proj, gq2, gk2, ct, sa, sb, dqn, dkn, dv):
    S = proj.shape[0]
    tr = _pick(S, (256,))
    n_pair = B_WIDTH // 128

    def body(q_ref, k_ref, gq_ref, gk_ref, ct_ref, sa_ref, sb_ref, dqn_ref, dkn_ref, dv_ref, dqkv_ref, dgq_ref, dgk_ref):
        i = pl.program_id(0)
        ct_v, sa_v, sb_v = ct_ref[...], sa_ref[...], sb_ref[...]
        dgq = jnp.zeros((1, 128), F32)
        for p in range(n_pair):
            sl = slice(p * 128, (p + 1) * 128)
            dx, dg = _pair_norm_rope_bwd(q_ref[:, sl], gq_ref[...], ct_v, sa_v, sb_v, dqn_ref[:, sl])
            dqkv_ref[:, sl] = dx.astype(BF)
            dgq = dgq + dg
        dx, dgk = _pair_norm_rope_bwd(k_ref[...], gk_ref[...], ct_v, sa_v, sb_v, dkn_ref[...])
        dqkv_ref[:, B_WIDTH:B_WIDTH + 128] = dx.astype(BF)
        dqkv_ref[:, B_WIDTH + 128:B_WIDTH + 256] = dv_ref[...].astype(BF)

        @pl.when(i == 0)
        def _():
            dgq_ref[...] = dgq
            dgk_ref[...] = dgk

        @pl.when(i > 0)
        def _():
            dgq_ref[...] += dgq
            dgk_ref[...] += dgk

    tab = pl.BlockSpec((tr, 128), lambda i: (i, 0))
    gsp = pl.BlockSpec((1, 128), lambda i: (0, 0))
    return pl.pallas_call(
        body, name="b_pre_bwd", grid=(S // tr,),
        in_specs=[pl.BlockSpec((tr, B_WIDTH), lambda i: (i, 1)), pl.BlockSpec((tr, 128), lambda i: (i, 2 * B_WIDTH // 128)),
                  gsp, gsp, tab, tab, tab, pl.BlockSpec((tr, B_WIDTH), lambda i: (i, 0)), tab, tab],
        out_specs=[pl.BlockSpec((tr, B_WIDTH + 256), lambda i: (i, 0)), gsp, gsp],
        out_shape=[jax.ShapeDtypeStruct((S, B_WIDTH + 256), BF), jax.ShapeDtypeStruct((1, 128), F32), jax.ShapeDtypeStruct((1, 128), F32)],
        compiler_params=_params(("arbitrary",)),
    )(proj, proj, gq2, gk2, ct, sa, sb, dqn, dkn, dv)


def _b_dup(x2, g):
    d = jnp.where(_half_mask(x2.shape, g), x2, 0.0)
    return (d + pltpu.roll(d, 64, 1)).astype(BF)


PAIRS_PER_GROUP = B_HEADS // B_KV_HEADS // 2
GROUP_ROWS = PAIRS_PER_GROUP * CHUNK


def _b_valid(n):
    row = lax.broadcasted_iota(jnp.int32, (GROUP_ROWS, 2 * CHUNK), 0) & (CHUNK - 1)
    col = lax.broadcasted_iota(jnp.int32, (GROUP_ROWS, 2 * CHUNK), 1)
    rel = row + CHUNK - col
    return (rel >= 0) & (rel < CHUNK) & ((col >= CHUNK) | (n > 0))


def _b_blocks(x2, g):
    xd = _b_dup(x2, g)
    lo = _half_mask(xd.shape, 0)
    zero = jnp.zeros_like(xd)
    return jnp.concatenate([jnp.where(lo, xd, zero), jnp.where(lo, zero, xd)], axis=0)


def _b_sink_col(s_ref, g, hf):
    rb = lax.broadcasted_iota(jnp.int32, (GROUP_ROWS, 1), 0) // CHUNK
    col = jnp.zeros((GROUP_ROWS, 1), F32)
    for pp in range(PAIRS_PER_GROUP):
        col = jnp.where(rb == pp, s_ref[0, 2 * (g * PAIRS_PER_GROUP + pp) + hf], col)
    return col


def _b_probs(qs, kblk, valid, sinks):
    s = lax.dot_general(qs, kblk, (((1,), (1,)), ((), ())), preferred_element_type=F32) * (B_HEAD_DIM ** -0.5)
    out = []
    for hf in range(2):
        sh = jnp.where(valid, s[:, hf * 2 * CHUNK:(hf + 1) * 2 * CHUNK], NEG)
        m = jnp.maximum(jnp.max(sh, axis=-1, keepdims=True), sinks[hf])
        e = jnp.exp(sh - m)
        es = jnp.exp(sinks[hf] - m)
        inv = 1.0 / (jnp.sum(e, axis=-1, keepdims=True) + es)
        out.append((e * inv, es * inv))
    return out


def _b_fold(acc, g):
    lo = _half_mask((2 * CHUNK, 128), 0)
    t = jnp.where(lo, acc[:2 * CHUNK], 0.0) + jnp.where(lo, 0.0, acc[2 * CHUNK:])
    return jnp.where(_half_mask((2 * CHUNK, 128), g), t + pltpu.roll(t, 64, 1), 0.0)


def _b_kv_specs(S):
    prev = lambda n: (jnp.maximum(n - 1, 0), 0)
    cur = lambda n: (n, 0)
    v_col = (2 * B_WIDTH + B_KV_WIDTH) // 128
    return [pl.BlockSpec((CHUNK, 128), prev), pl.BlockSpec((CHUNK, 128), cur),
            pl.BlockSpec((CHUNK, 128), lambda n: (jnp.maximum(n - 1, 0), v_col)), pl.BlockSpec((CHUNK, 128), lambda n: (n, v_col))]


def _b_attn_fwd(qn, kn, proj, sinks):
    S = qn.shape[0]

    def body(s_ref, q_ref, kp_ref, kc_ref, vp_ref, vc_ref, y_ref):
        n = pl.program_id(0)
        valid = _b_valid(n)
        k2 = jnp.concatenate([kp_ref[...], kc_ref[...]], axis=0).astype(F32)
        v2 = jnp.concatenate([vp_ref[...], vc_ref[...]], axis=0)
        for g in range(B_KV_HEADS):
            pairs = [g * PAIRS_PER_GROUP + pp for pp in range(PAIRS_PER_GROUP)]
            qs = jnp.concatenate([q_ref[:, p * 128:(p + 1) * 128] for p in pairs], axis=0)
            probs = _b_probs(qs, _b_blocks(k2, g), valid, [_b_sink_col(s_ref, g, hf) for hf in range(2)])
            pcat = jnp.concatenate([probs[0][0].astype(BF), probs[1][0].astype(BF)], axis=1)
            o = jnp.dot(pcat, _b_blocks(v2, g), preferred_element_type=F32)
            for pp, p in enumerate(pairs):
                y_ref[:, p * 128:(p + 1) * 128] = o[pp * CHUNK:(pp + 1) * CHUNK].astype(BF)

    return pl.pallas_call(
        body, name="b_attn_fwd", grid=(S // CHUNK,),
        in_specs=[pl.BlockSpec(memory_space=pltpu.SMEM), pl.BlockSpec((CHUNK, B_WIDTH), lambda n: (n, 0))] + _b_kv_specs(S),
        out_specs=pl.BlockSpec((CHUNK, B_WIDTH), lambda n: (n, 0)),
        out_shape=jax.ShapeDtypeStruct((S, B_WIDTH), BF), compiler_params=_params(("arbitrary",)),
    )(sinks, qn, kn, kn, proj, proj)


def _b_attn_bwd(qn, kn, proj, sinks, dy, after=()):
    S = qn.shape[0]

    def body(s_ref, q_ref, kp_ref, kc_ref, vp_ref, vc_ref, dy_ref, dq_ref, dk_ref, dv_ref, ds_ref):
        n = pl.program_id(0)

        @pl.when(n == 0)
        def _():
            dk_ref[...] = jnp.zeros_like(dk_ref)
            dv_ref[...] = jnp.zeros_like(dv_ref)
            ds_ref[...] = jnp.zeros_like(ds_ref)

        valid = _b_valid(n)
        k2 = jnp.concatenate([kp_ref[...], kc_ref[...]], axis=0).astype(F32)
        v2 = jnp.concatenate([vp_ref[...], vc_ref[...]], axis=0)
        lane = lax.broadcasted_iota(jnp.int32, (CHUNK, 128), 1)
        dk2 = jnp.zeros((2 * CHUNK, 128), F32)
        dv2 = jnp.zeros((2 * CHUNK, 128), F32)
        dsink = jnp.zeros((CHUNK, 128), F32)
        scale = B_HEAD_DIM ** -0.5
        nt = (((1,), (1,)), ((), ()))
        tn = (((0,), (0,)), ((), ()))
        for g in range(B_KV_HEADS):
            pairs = [g * PAIRS_PER_GROUP + pp for pp in range(PAIRS_PER_GROUP)]
            qs = jnp.concatenate([q_ref[:, p * 128:(p + 1) * 128] for p in pairs], axis=0)
            do = jnp.concatenate([dy_ref[:, p * 128:(p + 1) * 128] for p in pairs], axis=0)
            do_b = do.astype(BF)
            kblk, vblk = _b_blocks(k2, g), _b_blocks(v2, g)
            probs = _b_probs(qs, kblk, valid, [_b_sink_col(s_ref, g, hf) for hf in range(2)])
            pcat = jnp.concatenate([probs[0][0].astype(BF), probs[1][0].astype(BF)], axis=1)
            o = jnp.dot(pcat, vblk, preferred_element_type=F32)
            dp = lax.dot_general(do_b, vblk, nt, preferred_element_type=F32)
            prod = do * o
            ds_halves = []
            for hf in range(2):
                pr, ps = probs[hf]
                delta = jnp.sum(jnp.where(_half_mask(prod.shape, hf), prod, 0.0), axis=-1, keepdims=True)
                ds_halves.append((pr * (dp[:, hf * 2 * CHUNK:(hf + 1) * 2 * CHUNK] - delta) * scale).astype(BF))
                t = -ps * delta
                for pp, p in enumerate(pairs):
                    dsink = dsink + jnp.where(lane == 2 * p + hf, t[pp * CHUNK:(pp + 1) * CHUNK], 0.0)
            dsc = jnp.concatenate(ds_halves, axis=1)
            dq = jnp.dot(dsc, kblk, preferred_element_type=F32)
            for pp, p in enumerate(pairs):
                dq_ref[:, p * 128:(p + 1) * 128] = dq[pp * CHUNK:(pp + 1) * CHUNK]
            dk2 = dk2 + _b_fold(lax.dot_general(dsc, qs, tn, preferred_element_type=F32), g)
            dv2 = dv2 + _b_fold(lax.dot_general(pcat, do_b, tn, preferred_element_type=F32), g)
        ds_ref[...] += dsink
        cur = pl.ds(pl.multiple_of(n * CHUNK, CHUNK), CHUNK)
        dk_ref[cur, :] += dk2[CHUNK:]
        dv_ref[cur, :] += dv2[CHUNK:]

        @pl.when(n > 0)
        def _():
            prv = pl.ds(pl.multiple_of((n - 1) * CHUNK, CHUNK), CHUNK)
            dk_ref[prv, :] += dk2[:CHUNK]
            dv_ref[prv, :] += dv2[:CHUNK]

    full = pl.BlockSpec((S, 128), lambda n: (0, 0))
    return pl.pallas_call(
        _hide(body, 7, len(after)), name="b_attn_bwd", grid=(S // CHUNK,),
        in_specs=[pl.BlockSpec(memory_space=pltpu.SMEM), pl.BlockSpec((CHUNK, B_WIDTH), lambda n: (n, 0))] + _b_kv_specs(S)
        + [pl.BlockSpec((CHUNK, B_WIDTH), lambda n: (n, 0))] + _hidden_specs(after),
        out_specs=[pl.BlockSpec((CHUNK, B_WIDTH), lambda n: (n, 0)), full, full, pl.BlockSpec((CHUNK, 128), lambda n: (0, 0))],
        out_shape=[jax.ShapeDtypeStruct((S, B_WIDTH), F32), jax.ShapeDtypeStruct((S, 128), F32), jax.ShapeDtypeStruct((S, 128), F32),
                   jax.ShapeDtypeStruct((CHUNK, 128), F32)],
        compiler_params=_params(("arbitrary",)),
    )(sinks, qn, kn, kn, proj, proj, dy, *after)


def _c_block(q, k, v, gq, gk):
    qn = q * lax.rsqrt(jnp.mean(q * q, axis=-1, keepdims=True) + EPS) * gq
    kn = k * lax.rsqrt(jnp.mean(k * k, axis=-1, keepdims=True) + EPS) * gk
    s = lax.dot_general(qn.astype(BF), kn.astype(BF), (((1,), (1,)), ((), ())), preferred_element_type=F32) * (C_HEAD_DIM ** -0.5)
    p = jax.nn.softmax(s, axis=-1)
    return jnp.dot(p.astype(BF), v.astype(BF), preferred_element_type=F32)


def _c_specs(S, M, tq):
    q_col = (2 * A_WIDTH + B_WIDTH + 2 * B_KV_WIDTH) // 128
    return [pl.BlockSpec((tq, 128), lambda h, i: (i, q_col + h)), pl.BlockSpec((M, 128), lambda h, i: (0, h)),
            pl.BlockSpec((M, 128), lambda h, i: (0, C_HEADS + h)), pl.BlockSpec((1, 128), lambda h, i: (0, 0)),
            pl.BlockSpec((1, 128), lambda h, i: (0, 0))]


def _c_fwd(proj, kv, gq, gk):
    S, M = proj.shape[0], kv.shape[0]
    tq = _pick(S, (512,))

    def body(q_ref, k_ref, v_ref, gq_ref, gk_ref, y_ref):
        y_ref[...] = _c_block(q_ref[...], k_ref[...], v_ref[...], gq_ref[...], gk_ref[...]).astype(BF)

    return pl.pallas_call(
        body, name="c_fwd", grid=(C_HEADS, S // tq), in_specs=_c_specs(S, M, tq),
        out_specs=pl.BlockSpec((tq, 128), lambda h, i: (i, h)),
        out_shape=jax.ShapeDtypeStruct((S, C_WIDTH), BF), compiler_params=_params(("parallel", "parallel")),
    )(proj, kv, kv, gq, gk)


def _c_bwd(proj, kv, gq, gk, dy):
    S, M = proj.shape[0], kv.shape[0]
    tq = _pick(S, (512,))

    def body(q_ref, k_ref, v_ref, gq_ref, gk_ref, dy_ref, dq_ref, dk_ref, dv_ref, dgq_ref, dgk_ref):
        i = pl.program_id(1)
        _, vjp = jax.vjp(_c_block, q_ref[...], k_ref[...], v_ref[...], gq_ref[...], gk_ref[...])
        dq, dk, dv, dgq, dgk = vjp(dy_ref[...])
        dq_ref[...] = dq.astype(BF)

        @pl.when(i == 0)
        def _():
            dk_ref[...] = dk
            dv_ref[...] = dv
            dgq_ref[...] = dgq
            dgk_ref[...] = dgk

        @pl.when(i > 0)
        def _():
            dk_ref[...] += dk
            dv_ref[...] += dv
            dgq_ref[...] += dgq
            dgk_ref[...] += dgk

    return pl.pallas_call(
        body, name="c_bwd", grid=(C_HEADS, S // tq),
        in_specs=_c_specs(S, M, tq) + [pl.BlockSpec((tq, 128), lambda h, i: (i, h))],
        out_specs=[pl.BlockSpec((tq, 128), lambda h, i: (i, h)), pl.BlockSpec((M, 128), lambda h, i: (0, h)),
                   pl.BlockSpec((M, 128), lambda h, i: (0, h)), pl.BlockSpec((None, 1, 128), lambda h, i: (h, 0, 0)),
                   pl.BlockSpec((None, 1, 128), lambda h, i: (h, 0, 0))],
        out_shape=[jax.ShapeDtypeStruct((S, C_WIDTH), BF), jax.ShapeDtypeStruct((M, C_WIDTH), F32), jax.ShapeDtypeStruct((M, C_WIDTH), F32),
                   jax.ShapeDtypeStruct((C_HEADS, 1, 128), F32), jax.ShapeDtypeStruct((C_HEADS, 1, 128), F32)],
        compiler_params=_params(("parallel", "arbitrary")),
    )(proj, kv, kv, gq, gk, dy)


def _merge_specs(S, D, tr, tc):
    off = GATE_OFF // tc
    nd = D // tc
    gates = [pl.BlockSpec((tr, tc), functools.partial(lambda b, i, j: (i, off + b * nd + j), b)) for b in range(3)]
    zs = [pl.BlockSpec((tr, tc), lambda i, j: (i, j)) for _ in range(3)]
    return gates + zs


def _merge_fwd(proj, za, zb, zc):
    S, D = za.shape
    tr, tc = _pick(S, (512,)), _pick(D, (256,))

    def body(ga_ref, gb_ref, gc_ref, za_ref, zb_ref, zc_ref, m_ref):
        acc = jax.nn.sigmoid(ga_ref[...]) * za_ref[...].astype(F32)
        acc = acc + jax.nn.sigmoid(gb_ref[...]) * zb_ref[...].astype(F32)
        acc = acc + jax.nn.sigmoid(gc_ref[...]) * zc_ref[...].astype(F32)
        m_ref[...] = acc.astype(BF)

    return pl.pallas_call(
        body, name="merge_fwd", grid=(S // tr, D // tc), in_specs=_merge_specs(S, D, tr, tc),
        out_specs=pl.BlockSpec((tr, tc), lambda i, j: (i, j)), out_shape=jax.ShapeDtypeStruct((S, D), BF),
        compiler_params=_params(("parallel", "parallel")),
    )(proj, proj, proj, za, zb, zc)


def _merge_bwd(proj, za, zb, zc, dm, after=()):
    S, D = za.shape
    tr, tc = _pick(S, (512,)), _pick(D, (256,))
    nd = D // tc

    def body(ga_ref, gb_ref, gc_ref, za_ref, zb_ref, zc_ref, dm_ref, dza_ref, dzb_ref, dzc_ref, dga_ref, dgb_ref, dgc_ref):
        dmv = dm_ref[...]
        for g_ref, z_ref, dz_ref, dg_ref in ((ga_ref, za_ref, dza_ref, dga_ref), (gb_ref, zb_ref, dzb_ref, dgb_ref),
                                             (gc_ref, zc_ref, dzc_ref, dgc_ref)):
            sg = jax.nn.sigmoid(g_ref[...])
            dz_ref[...] = (sg * dmv).astype(BF)
            dg_ref[...] = (dmv * z_ref[...].astype(F32) * sg * (1.0 - sg)).astype(BF)

    tile = pl.BlockSpec((tr, tc), lambda i, j: (i, j))
    return pl.pallas_call(
        _hide(body, 7, len(after)), name="merge_bwd", grid=(S // tr, D // tc),
        in_specs=_merge_specs(S, D, tr, tc) + [tile] + _hidden_specs(after),
        out_specs=[tile, tile, tile, tile, tile, tile],
        out_shape=[jax.ShapeDtypeStruct((S, D), BF)] * 6,
        compiler_params=_params(("parallel", "parallel")),
    )(proj, proj, proj, za, zb, zc, dm, *after)


PAD = 8


def _stage_shift_down(us_ref, u_ref):
    S = u_ref.shape[1]
    us_ref[:, 0:PAD, :] = jnp.zeros((2, PAD, us_ref.shape[2]), F32)
    us_ref[:, PAD:S + PAD, :] = u_ref[...]


ROWS = 32


def _conv3(us_ref, part, r0, w, b):
    return (us_ref[part, pl.ds(r0 + PAD, ROWS), :] * w[2:3] + us_ref[part, pl.ds(r0 + PAD - 1, ROWS), :] * w[1:2]
            + us_ref[part, pl.ds(r0 + PAD - 2, ROWS), :] * w[0:1] + b)


def _ffn_specs(S, F, tc, c):
    per = c // tc

    def w_spec(half):
        return pl.BlockSpec((None, 3, tc), lambda j: (half * (N_DEV // 2) + j // per, 0, j % per))

    return [pl.BlockSpec((2, S, tc), lambda j: (0, 0, j)), w_spec(0), w_spec(1), pl.BlockSpec((2, 1, tc), lambda j: (0, 0, j))]


def _ffn_tile(F, c):
    tc = 128
    if c % tc or F % tc:
        raise ValueError(f"ffn tile {tc} does not divide {c}, {F}")
    return tc


def _ffn_act_fwd(up3, cws, cb3):
    _, S, F = up3.shape
    c = cws.shape[2]
    tc = _ffn_tile(F, c)

    def body(u_ref, wa_ref, wb_ref, b_ref, o_ref, us_ref):
        _stage_shift_down(us_ref, u_ref)
        wa, wb, ba, bb = wa_ref[...], wb_ref[...], b_ref[0], b_ref[1]

        def step(i, carry):
            r0 = pl.multiple_of(i * ROWS, ROWS)
            ca = _conv3(us_ref, 0, r0, wa, ba)
            cb = _conv3(us_ref, 1, r0, wb, bb)
            o_ref[pl.ds(r0, ROWS), :] = (ca * jax.nn.sigmoid(ca) * cb).astype(BF)
            return carry

        lax.fori_loop(0, S // ROWS, step, 0, unroll=4)

    return pl.pallas_call(
        body, name="ffn_act_fwd", grid=(F // tc,), in_specs=_ffn_specs(S, F, tc, c),
        out_specs=pl.BlockSpec((S, tc), lambda j: (0, j)), out_shape=jax.ShapeDtypeStruct((S, F), BF),
        scratch_shapes=[pltpu.VMEM((2, S + PAD, tc), F32)],
        compiler_params=_params(("parallel",)),
    )(up3, cws, cws, cb3)


def _ffn_act_bwd(up3, cws, cb3, dact, after=()):
    _, S, F = up3.shape
    c = cws.shape[2]
    tc = _ffn_tile(F, c)

    def body(u_ref, wa_ref, wb_ref, b_ref, da_ref, du_ref, dw_ref, db_ref, us_ref, dcs_ref):
        _stage_shift_down(us_ref, u_ref)
        ws = (wa_ref[...], wb_ref[...])
        ba, bb = b_ref[0], b_ref[1]
        dcs_ref[:, S:S + PAD, :] = jnp.zeros((2, PAD, tc), F32)

        def conv_grads(i, carry):
            r0 = pl.multiple_of(i * ROWS, ROWS)
            ca = _conv3(us_ref, 0, r0, ws[0], ba)
            cb = _conv3(us_ref, 1, r0, ws[1], bb)
            sg = jax.nn.sigmoid(ca)
            dav = da_ref[pl.ds(r0, ROWS), :].astype(F32)
            dcs_ref[0, pl.ds(r0, ROWS), :] = dav * cb * sg * (1.0 + ca * (1.0 - sg))
            dcs_ref[1, pl.ds(r0, ROWS), :] = dav * ca * sg
            return carry

        lax.fori_loop(0, S // ROWS, conv_grads, 0, unroll=4)

        def fold(v):
            return jnp.sum(v.reshape(ROWS // 8, 8, tc), axis=0)

        def input_grads(i, acc):
            r0 = pl.multiple_of(i * ROWS, ROWS)
            new = []
            for part in range(2):
                w = ws[part]
                dc = dcs_ref[part, pl.ds(r0, ROWS), :]
                dc1 = dcs_ref[part, pl.ds(r0 + 1, ROWS), :]
                dc2 = dcs_ref[part, pl.ds(r0 + 2, ROWS), :]
                u = u_ref[part, pl.ds(r0, ROWS), :]
                du_ref[part, pl.ds(r0, ROWS), :] = (dc * w[2:3] + dc1 * w[1:2] + dc2 * w[0:1]).astype(BF)
                sums = (fold(dc2 * u), fold(dc1 * u), fold(dc * u), fold(dc))
                new += [a + s for a, s in zip(acc[4 * part:4 * part + 4], sums)]
            return tuple(new)

        acc = lax.fori_loop(0, S // ROWS, input_grads, tuple(jnp.zeros((8, tc), F32) for _ in range(8)), unroll=4)
        for part in range(2):
            for j in range(3):
                dw_ref[part, j:j + 1, :] = jnp.sum(acc[4 * part + j], axis=0, keepdims=True)
            db_ref[part] = jnp.sum(acc[4 * part + 3], axis=0, keepdims=True)

    return pl.pallas_call(
        _hide(body, 5, len(after)), name="ffn_act_bwd", grid=(F // tc,),
        in_specs=_ffn_specs(S, F, tc, c) + [pl.BlockSpec((S, tc), lambda j: (0, j))] + _hidden_specs(after),
        out_specs=[pl.BlockSpec((2, S, tc), lambda j: (0, 0, j)), pl.BlockSpec((2, 3, tc), lambda j: (0, 0, j)),
                   pl.BlockSpec((2, 1, tc), lambda j: (0, 0, j))],
        out_shape=[jax.ShapeDtypeStruct((2, S, F), BF), jax.ShapeDtypeStruct((2, 3, F), F32), jax.ShapeDtypeStruct((2, 1, F), F32)],
        scratch_shapes=[pltpu.VMEM((2, S + PAD, tc), F32), pltpu.VMEM((2, S + PAD, tc), F32)],
        compiler_params=_params(("parallel",)),
    )(up3, cws, cws, cb3, dact, *after)


def _loss(y, target):
    S, D = y.shape
    tr = _pick(S, (256,))

    def body(y_ref, t_ref, dy_ref, dyb_ref, l_ref):
        i = pl.program_id(0)
        e = y_ref[...] - t_ref[...]
        dy = e * (1.0 / D)
        dy_ref[...] = dy
        dyb_ref[...] = dy.astype(BF)
        part = jnp.sum(jnp.sum(e * e, axis=-1, keepdims=True), axis=0, keepdims=True) * (0.5 / D)

        @pl.when(i == 0)
        def _():
            l_ref[...] = jnp.zeros_like(l_ref)

        l_ref[...] += part

    row = pl.BlockSpec((tr, D), lambda i: (i, 0))
    return pl.pallas_call(
        body, name="loss", grid=(S // tr,), in_specs=[row, row],
        out_specs=[row, row, pl.BlockSpec((8, 128), lambda i: (0, 0))],
        out_shape=[jax.ShapeDtypeStruct((S, D), F32), jax.ShapeDtypeStruct((S, D), BF), jax.ShapeDtypeStruct((8, 128), F32)],
        compiler_params=_params(("arbitrary",)),
    )(y, target)


ANY = pl.BlockSpec(memory_space=pl.ANY)


def _allgather(shards, name):
    n = len(shards)

    def body(*refs):
        ins, outs = refs[:n], refs[n:2 * n]
        send_sems, recv_sems, local_sems = refs[2 * n:]
        x, y, c = lax.axis_index("x"), lax.axis_index("y"), lax.axis_index("c")
        me, sibling = (x, y, c), (x, y, 1 - c)
        chips = [(1 - x, y), (x, 1 - y), (1 - x, 1 - y)]

        def blk(w, px, py, pc):
            return outs[w].at[4 * px + 2 * py + pc]

        def copy(w, k, block, to, src=None):
            return pltpu.make_async_remote_copy(
                src_ref=blk(w, *block) if src is None else src, dst_ref=blk(w, *block),
                send_sem=send_sems.at[w, k], recv_sem=recv_sems.at[w, k], device_id=to, device_id_type=MESH)

        started = []
        mine = []
        for w in range(n):
            mine.append(pltpu.make_async_copy(ins[w], blk(w, *me), local_sems.at[w]))
            mine[-1].start()
            first = [copy(w, 0, me, sibling, src=ins[w])]
            first += [copy(w, 1 + j, me, (*chip, c), src=ins[w]) for j, chip in enumerate(chips)]
            for cp in first:
                cp.start()
            started += first
        for w in range(n):
            for j, chip in enumerate(chips):
                copy(w, 1 + j, (*chip, c), me).wait_recv()
                fwd = copy(w, 4 + j, (*chip, c), sibling)
                fwd.start()
                started.append(fwd)
        for w in range(n):
            copy(w, 0, sibling, me).wait_recv()
            for j, chip in enumerate(chips):
                copy(w, 4 + j, (*chip, 1 - c), me).wait_recv()
        for cp in started:
            cp.wait_send()
        for cp in mine:
            cp.wait()

    whole = pl.BlockSpec(memory_space=pltpu.VMEM)
    outs = pl.pallas_call(
        body, name=name, in_specs=[whole] * n, out_specs=[whole] * n,
        out_shape=[jax.ShapeDtypeStruct((N_DEV,) + s.shape, s.dtype) for s in shards],
        scratch_shapes=[pltpu.SemaphoreType.DMA((n, 7)), pltpu.SemaphoreType.DMA((n, 7)), pltpu.SemaphoreType.DMA((n,))],
    )(*shards)
    return list(outs)


def _allgather_seq(shards, name, collective_id, after=()):
    n = len(shards)
    n_after = len(after)

    def body(*refs):
        ins, outs = refs[:n], refs[n + n_after:2 * n + n_after]
        send_sems, recv_sems, local_sems = refs[2 * n + n_after:]
        x, y, c = lax.axis_index("x"), lax.axis_index("y"), lax.axis_index("c")
        me, sibling = (x, y, c), (x, y, 1 - c)
        chips = [(1 - x, y), (x, 1 - y), (1 - x, 1 - y)]
        barrier = pltpu.get_barrier_semaphore()
        for peer in [sibling] + [(*chip, c) for chip in chips]:
            pl.semaphore_signal(barrier, inc=1, device_id=peer, device_id_type=MESH)
        pl.semaphore_wait(barrier, 4)

        def blk(w, px, py, pc):
            return outs[w].at[4 * px + 2 * py + pc]

        def copy(w, k, block, to, src=None):
            return pltpu.make_async_remote_copy(
                src_ref=blk(w, *block) if src is None else src, dst_ref=blk(w, *block),
                send_sem=send_sems.at[7 * w + k], recv_sem=recv_sems.at[7 * w + k], device_id=to, device_id_type=MESH)

        started = []
        mine = []
        for w in range(n):
            mine.append(pltpu.make_async_copy(ins[w], blk(w, *me), local_sems.at[w]))
            mine[-1].start()
            first = [copy(w, 0, me, sibling, src=ins[w])]
            first += [copy(w, 1 + j, me, (*chip, c), src=ins[w]) for j, chip in enumerate(chips)]
            for cp in first:
                cp.start()
            started += first
        for w in range(n):
            for j, chip in enumerate(chips):
                copy(w, 1 + j, (*chip, c), me).wait_recv()
                fwd = copy(w, 4 + j, (*chip, c), sibling)
                fwd.start()
                started.append(fwd)
        for w in range(n):
            copy(w, 0, sibling, me).wait_recv()
            for j, chip in enumerate(chips):
                copy(w, 4 + j, (*chip, 1 - c), me).wait_recv()
        for cp in started:
            cp.wait_send()
        for cp in mine:
            cp.wait()

    outs = pl.kernel(
        body, name=name, out_type=[jax.ShapeDtypeStruct((N_DEV,) + s.shape, s.dtype) for s in shards],
        mesh=plsc.ScalarSubcoreMesh(axis_name="seq", num_cores=1),
        scratch_types=[pltpu.SemaphoreType.DMA((7 * n,)), pltpu.SemaphoreType.DMA((7 * n,)), pltpu.SemaphoreType.DMA((n,))],
        compiler_params=pltpu.CompilerParams(collective_id=collective_id),
    )(*shards, *after)
    return list(outs)


def _chip_exchange(sums, name, collective_id):
    n = len(sums)

    def body(*refs):
        ins, outs = refs[:n], refs[n:2 * n]
        send_sems, recv_sems = refs[2 * n:]
        x, y, c = lax.axis_index("x"), lax.axis_index("y"), lax.axis_index("c")
        chips = [(1 - x, y), (x, 1 - y), (1 - x, 1 - y)]
        barrier = pltpu.get_barrier_semaphore()
        for px, py in chips:
            pl.semaphore_signal(barrier, inc=1, device_id=(px, py, c), device_id_type=MESH)
        pl.semaphore_wait(barrier, 3)
        copies = []
        for w in range(n):
            for k, (px, py) in enumerate(chips):
                copies.append(pltpu.make_async_remote_copy(
                    src_ref=ins[w].at[2 * px + py], dst_ref=outs[w].at[k], send_sem=send_sems.at[3 * w + k],
                    recv_sem=recv_sems.at[3 * w + k], device_id=(px, py, c), device_id_type=MESH))
        for cp in copies:
            cp.start()
        for cp in copies:
            cp.wait()

    outs = pl.kernel(
        body, name=name, out_type=[jax.ShapeDtypeStruct((3,) + s.shape[1:], s.dtype) for s in sums],
        mesh=plsc.ScalarSubcoreMesh(axis_name="seq", num_cores=1),
        scratch_types=[pltpu.SemaphoreType.DMA((3 * n,)), pltpu.SemaphoreType.DMA((3 * n,))],
        compiler_params=pltpu.CompilerParams(collective_id=collective_id),
    )(*sums)
    return list(outs)


def _row_tile(r, c, elems=256 * 1024):
    want = max(8, elems // c)
    for t in range(min(want, r) // 8 * 8, 0, -8):
        if r % t == 0:
            return t
    return r


def _pair_add(g4, recv, core, name, after=()):
    _, _, r, c = g4.shape
    tr = _row_tile(r, c, 1024 * 1024)

    def body(core_ref, a_ref, b_ref, o_ref):
        o_ref[...] = (a_ref[...].astype(F32) + b_ref[...].astype(F32)).astype(BF)

    return pl.pallas_call(
        _hide(body, 3, len(after)), name=name,
        grid_spec=pltpu.PrefetchScalarGridSpec(
            num_scalar_prefetch=1, grid=(4, r // tr),
            in_specs=[pl.BlockSpec((None, None, tr, c), lambda p, i, s: (p, s[0], i, 0)),
                      pl.BlockSpec((None, tr, c), lambda p, i, s: (p, i, 0))] + _hidden_specs(after),
            out_specs=pl.BlockSpec((None, tr, c), lambda p, i, s: (p, i, 0))),
        out_shape=jax.ShapeDtypeStruct((4, r, c), BF), compiler_params=_params(("parallel", "parallel")),
    )(core, g4, recv, *after)


def _adam_math(w, g, m, v):
    m = ADAM_B1 * m + (1.0 - ADAM_B1) * g
    v = ADAM_B2 * v + (1.0 - ADAM_B2) * (g * g)
    m_hat = m / (1.0 - ADAM_B1 ** ADAM_STEP)
    v_hat = v / (1.0 - ADAM_B2 ** ADAM_STEP)
    delta = -ADAM_LR * (m_hat / (jnp.sqrt(v_hat) + ADAM_EPS) + ADAM_WD * w)
    return delta, m, v


def _adamw_big(sums, recv, chip, w, m, v, name, after=()):
    r, c = w.shape
    tr = _row_tile(r, c, 512 * 1024)

    def body(chip_ref, s_ref, r_ref, w_ref, m_ref, v_ref, g_out, d_out, m_out, v_out):
        g = s_ref[...].astype(F32) + r_ref[0].astype(F32)
        g = g + r_ref[1].astype(F32)
        g = g + r_ref[2].astype(F32)
        delta, mn, vn = _adam_math(w_ref[...], g, m_ref[...], v_ref[...])
        g_out[...] = g
        d_out[...] = delta
        m_out[...] = mn
        v_out[...] = vn

    row = pl.BlockSpec((tr, c), lambda i, s: (i, 0))
    return pl.pallas_call(
        _hide(body, 6, len(after)), name=name,
        grid_spec=pltpu.PrefetchScalarGridSpec(
            num_scalar_prefetch=1, grid=(r // tr,),
            in_specs=[pl.BlockSpec((None, tr, c), lambda i, s: (s[0], i, 0)), pl.BlockSpec((3, tr, c), lambda i, s: (0, i, 0)),
                      row, row, row] + _hidden_specs(after),
            out_specs=[row, row, row, row]),
        out_shape=[jax.ShapeDtypeStruct((r, c), F32)] * 4, compiler_params=_params(("parallel",)),
    )(chip, sums, recv, w, m, v, *after)


def _adamw_small(parts, ws, ms, vs, extra_parts, name):
    n, ne = len(ws), len(extra_parts)

    def total(p_ref):
        g = p_ref[0]
        for d in range(1, N_DEV):
            g = g + p_ref[d]
        return g

    def body(*refs):
        p_refs, w_refs, m_refs, v_refs = refs[:n], refs[n:2 * n], refs[2 * n:3 * n], refs[3 * n:4 * n]
        e_refs = refs[4 * n:4 * n + ne]
        outs = refs[4 * n + ne:]
        for i in range(n):
            g = total(p_refs[i])
            delta, mn, vn = _adam_math(w_refs[i][...], g, m_refs[i][...], v_refs[i][...])
            outs[4 * i][...] = g
            outs[4 * i + 1][...] = delta
            outs[4 * i + 2][...] = mn
            outs[4 * i + 3][...] = vn
        for i in range(ne):
            outs[4 * n + i][...] = total(e_refs[i])

    out_shape = []
    for w in ws:
        out_shape += [jax.ShapeDtypeStruct(w.shape, F32)] * 4
    out_shape += [jax.ShapeDtypeStruct(e.shape[1:], F32) for e in extra_parts]
    res = pl.pallas_call(body, name=name, out_shape=out_shape,
                         compiler_params=pltpu.CompilerParams(vmem_limit_bytes=VMEM_LIMIT))(*parts, *ws, *ms, *vs, *extra_parts)
    return [res[4 * i:4 * i + 4] for i in range(n)], list(res[4 * n:])


def _adamw_plain(g, w, m, v, name):
    def body(g_ref, w_ref, m_ref, v_ref, d_out, m_out, v_out):
        delta, mn, vn = _adam_math(w_ref[...], g_ref[...], m_ref[...], v_ref[...])
        d_out[...] = delta
        m_out[...] = mn
        v_out[...] = vn

    return pl.pallas_call(body, name=name, out_shape=[jax.ShapeDtypeStruct(w.shape, F32)] * 3)(g, w, m, v)


def kernel(x, mem, positions, g_mix, w_in, g_a_v, w_spatial, b_spatial, g_b_q, g_b_k, sinks, g_mem, w_mem_kv, g_c_q, g_c_k, w_branch_a, w_branch_b, w_branch_c, w_out, g_ffn, w_up, conv_w, conv_b, w_down, loss_target, m_g_mix, m_w_in, m_g_a_v, m_w_spatial, m_b_spatial, m_g_b_q, m_g_b_k, m_sinks, m_g_mem, m_w_mem_kv, m_g_c_q, m_g_c_k, m_w_branch_a, m_w_branch_b, m_w_branch_c, m_w_out, m_g_ffn, m_w_up, m_conv_w, m_conv_b, m_w_down, v_g_mix, v_w_in, v_g_a_v, v_w_spatial, v_b_spatial, v_g_b_q, v_g_b_k, v_sinks, v_g_mem, v_w_mem_kv, v_g_c_q, v_g_c_k, v_w_branch_a, v_w_branch_b, v_w_branch_c, v_w_out, v_g_ffn, v_w_up, v_conv_w, v_conv_b, v_w_down):
    S, D = x.shape[1], x.shape[2]
    M = mem.shape[1]
    F = w_down.shape[1] * N_DEV
    in_cols = w_in.shape[2] * N_DEV
    ax, ay, ac = lax.axis_index("x"), lax.axis_index("y"), lax.axis_index("c")
    core = jnp.reshape(ac, (1,)).astype(jnp.int32)
    chip = jnp.reshape(2 * ax + ay, (1,)).astype(jnp.int32)
    me = 4 * ax + 2 * ay + ac

    x2, mem2, tgt2 = x[0], mem[0], loss_target[0]

    big = dict(w_in=w_in[0].T, w_mem_kv=w_mem_kv[0], w_branch_a=w_branch_a[0], w_branch_b=w_branch_b[0],
               w_branch_c=w_branch_c[0], w_out=w_out[0], w_up=w_up[0], w_down=w_down[0])
    names = list(big)
    cast = {k: big[k].astype(BF) for k in names}
    W = {}
    cb3 = conv_b.reshape(2, 1, F)
    W["w_in"], = _allgather_seq([cast["w_in"]], "ag_seq0", 0)
    w_in_t = W["w_in"].reshape(in_cols, D)
    grp1 = ["w_mem_kv", "w_branch_a", "w_branch_b", "w_branch_c", "w_out"]
    res1 = _allgather_seq([cast[k] for k in grp1] + [conv_w[0]], "ag_seq1", 1, after=(_token((w_in_t,), "tok_w_in"),))
    W.update(zip(grp1, res1))
    cw3 = res1[-1]
    w_kv_f = W["w_mem_kv"].reshape(D, 2 * C_WIDTH)
    w_out_f = W["w_out"].reshape(D, D)

    half = ROPE_DIM // 2
    inv = ROPE_THETA ** (-jnp.arange(half, dtype=F32) / half)
    ang = positions[0].astype(F32)[:, None] * inv
    cos, sin = jnp.cos(ang), jnp.sin(ang)
    one, zero = jnp.ones((S, B_HEAD_DIM - ROPE_DIM), F32), jnp.zeros((S, B_HEAD_DIM - ROPE_DIM), F32)
    z8 = jnp.zeros((S, half), F32)
    ct = jnp.tile(jnp.concatenate([cos, cos, one], axis=1), (1, 2))
    sa = jnp.tile(jnp.concatenate([-sin, z8, zero], axis=1), (1, 2))
    sb = jnp.tile(jnp.concatenate([z8, sin, zero], axis=1), (1, 2))
    gq2, gk2 = jnp.tile(g_b_q, (1, 2)), jnp.tile(g_b_k, (1, 2))
    b_t = b_spatial[0].T

    h, rstd1 = _rms_fwd(x2, g_mix, "rms1_fwd")
    proj = _mm(h, w_in_t, "nt", F32, "mm_proj", tn=1280)
    y_a = _a_fwd(proj, g_a_v, w_spatial[0], b_t)
    qn, kn = _b_pre(proj, gq2, gk2, ct, sa, sb)
    W["w_up"], = _allgather_seq([cast["w_up"]], "ag_seq2", 2, after=(_token((W["w_out"], qn), "tok_group1"),))
    y_b = _b_attn_fwd(qn, kn, proj, sinks)
    mem_h, rstd_m = _rms_fwd(mem2, g_mem, "rmsmem_fwd")
    kv = _mm(mem_h, w_kv_f, "nn", F32, "mm_kv", after=(y_b,))
    y_c = _c_fwd(proj, kv, g_c_q, g_c_k)
    z_a = _mm(y_a, W["w_branch_a"], "nn", BF, "mm_za", b_stack=True, after=(y_b,))
    z_b = _mm(y_b, W["w_branch_b"], "nn", BF, "mm_zb", b_stack=True)
    z_c = _mm(y_c, W["w_branch_c"], "nn", BF, "mm_zc", b_stack=True)
    merged = _merge_fwd(proj, z_a, z_b, z_c)
    x1 = _mm(merged, w_out_f, "nn", F32, "mm_x1", resid=x2)
    h2, rstd2 = _rms_fwd(x1, g_ffn, "rms2_fwd")
    W["w_down"], = _allgather_seq([cast["w_down"]], "ag_seq3", 3, after=(W["w_up"], h2))
    w_down_f = W["w_down"].reshape(F, D)
    up3 = _mm(h2, W["w_up"], "nn", F32, "mm_up", b_stack=True, out_parts=2)
    act = _ffn_act_fwd(up3, cw3, cb3)
    y = _mm(act, w_down_f, "nn", F32, "mm_y", resid=x1, tk=1408)
    dy, dy_b, loss_acc = _loss(y, tgt2)
    loss = lax.psum(loss_acc[0, 0], ("x", "y", "c"))

    reduced = {}

    def as4(g):
        return g.reshape(4, 2, g.shape[1], g.shape[2])

    def finish_group(gi, keys, g4, from_sibling):
        sums = [_pair_add(a, b, core, "rs_add_" + k) for k, a, b in zip(keys, g4, from_sibling)]
        from_chips = _chip_exchange(sums, f"rs_chip{gi}", 4 + gi)
        reduced.update(zip(keys, zip(sums, from_chips)))
        return tuple(sums)

    d_act = _mm(dy_b, w_down_f, "nt", BF, "mm_dact", tn=1408)
    g_down = _mm(act, dy_b, "tn", BF, "mm_gdown", tm=1408)
    d_up3, d_cw3, d_cb3 = _ffn_act_bwd(up3, cw3, cb3, d_act, after=(g_down,))
    grp0 = [as4(g_down.reshape(N_DEV, F // N_DEV, D))]
    g_up, sib0 = _mm(h2, d_up3, "tn", BF, "mm_gup", b_parts=2, out_stack=True, exchange=grp0)
    sums0 = finish_group(0, ["w_down"], grp0, sib0)
    grp1 = [as4(g_up)]
    d_h2, sib1 = _mm(d_up3, W["w_up"], "nt", F32, "mm_dh2", a_parts=2, b_stack=True, tm=2048, after=sums0, exchange=grp1)
    sums1 = finish_group(1, ["w_up"], grp1, sib1)
    dx1, dx1_b, d_g_ffn = _rms_bwd(x1, rstd2, g_ffn, d_h2, dy, "rms2_bwd", after=sums1)
    d_merged = _mm(dx1_b, w_out_f, "nt", F32, "mm_dmerged")
    g_out = _mm(merged, dx1_b, "tn", BF, "mm_gout")
    dz_a, dz_b, dz_c, dga, dgb, dgc = _merge_bwd(proj, z_a, z_b, z_c, d_merged, after=(g_out,))
    g_ba = _mm(y_a, dz_a, "tn", BF, "mm_gba", out_stack=True)
    g_bb = _mm(y_b, dz_b, "tn", BF, "mm_gbb", out_stack=True)
    g_bc = _mm(y_c, dz_c, "tn", BF, "mm_gbc", out_stack=True)
    grp2 = [as4(g_out.reshape(N_DEV, D // N_DEV, D)), as4(g_ba), as4(g_bb), as4(g_bc)]
    dy_a, sib2 = _mm(dz_a, W["w_branch_a"], "nt", F32, "mm_dya", b_stack=True, exchange=grp2)
    sums2 = finish_group(2, ["w_out", "w_branch_a", "w_branch_b", "w_branch_c"], grp2, sib2)
    dy_b_ = _mm(dz_b, W["w_branch_b"], "nt", F32, "mm_dyb", b_stack=True, after=sums2)
    dy_c = _mm(dz_c, W["w_branch_c"], "nt", F32, "mm_dyc", b_stack=True)
    d_uv, d_g_a_v, d_w_s, d_b_t = _a_bwd(proj, g_a_v, w_spatial[0], b_t, dy_a)
    dqn, dkn, dv_b, dsink_rows = _b_attn_bwd(qn, kn, proj, sinks, dy_b_)
    d_qkv, d_gq2, d_gk2 = _b_pre_bwd(proj, gq2, gk2, ct, sa, sb, dqn, dkn, dv_b)
    dq_c, dk_c, dv_c, d_gcq, d_gck = _c_bwd(proj, kv, g_c_q, g_c_k, dy_c)
    dkv_b = jnp.concatenate([dk_c, dv_c], axis=1).astype(BF)
    d_memh = _mm(dkv_b, w_kv_f, "nt", F32, "mm_dmemh")
    g_kv = _mm(mem_h, dkv_b, "tn", BF, "mm_gkv")
    _, _, d_g_mem = _rms_bwd(mem2, rstd_m, g_mem, d_memh, None, "rmsmem_bwd")
    dproj = jnp.concatenate([d_uv, d_qkv, dq_c, dga, dgb, dgc], axis=1)
    g_in = _mm(dproj, h, "tn", BF, "mm_gin", tm=1280)
    grp3 = [as4(g_in.reshape(N_DEV, in_cols // N_DEV, D)), as4(g_kv.reshape(N_DEV, D // N_DEV, 2 * C_WIDTH))]
    d_h, sib3 = _mm(dproj, w_in_t, "nn", F32, "mm_dh", tm=2048, tk=1280, exchange=grp3)
    sums3 = finish_group(3, ["w_in", "w_mem_kv"], grp3, sib3)
    grad_x, _, d_g_mix = _rms_bwd(x2, rstd1, g_mix, d_h, dx1, "rms1_bwd", after=sums3)

    small_names =["g_mix", "g_a_v", "w_spatial", "b_spatial", "g_b_q", "g_b_k", "sinks", "g_mem", "g_c_q", "g_c_k", "g_ffn", "conv_b"]
    small_w = dict(g_mix=g_mix, g_a_v=g_a_v, w_spatial=w_spatial, b_spatial=b_spatial, g_b_q=g_b_q, g_b_k=g_b_k, sinks=sinks,
                   g_mem=g_mem, g_c_q=g_c_q, g_c_k=g_c_k, g_ffn=g_ffn, conv_b=conv_b)
    small_m = dict(g_mix=m_g_mix, g_a_v=m_g_a_v, w_spatial=m_w_spatial, b_spatial=m_b_spatial, g_b_q=m_g_b_q, g_b_k=m_g_b_k,
                   sinks=m_sinks, g_mem=m_g_mem, g_c_q=m_g_c_q, g_c_k=m_g_c_k, g_ffn=m_g_ffn, conv_b=m_conv_b)
    small_v = dict(g_mix=v_g_mix, g_a_v=v_g_a_v, w_spatial=v_w_spatial, b_spatial=v_b_spatial, g_b_q=v_g_b_q, g_b_k=v_g_b_k,
                   sinks=v_sinks, g_mem=v_g_mem, g_c_q=v_g_c_q, g_c_k=v_g_c_k, g_ffn=v_g_ffn, conv_b=v_conv_b)
    small_g = dict(
        g_mix=d_g_mix, g_a_v=d_g_a_v, w_spatial=d_w_s, b_spatial=d_b_t.T,
        g_b_q=d_gq2.reshape(2, B_HEAD_DIM).sum(0), g_b_k=d_gk2.reshape(2, B_HEAD_DIM).sum(0),
        sinks=dsink_rows.sum(0)[:B_HEADS], g_mem=d_g_mem, g_c_q=d_gcq.sum(0), g_c_k=d_gck.sum(0), g_ffn=d_g_ffn,
        conv_b=d_cb3)
    partial = [small_g[k].reshape(small_w[k].shape) for k in small_names] + [d_cw3]
    parts = _allgather(partial, "ag_small")
    small_res, (g_cw3,) = _adamw_small(parts[:-1], [small_w[k] for k in small_names], [small_m[k] for k in small_names],
                                       [small_v[k] for k in small_names], parts[-1:], "adamw_small")
    small_out = dict(zip(small_names, small_res))
    c_cw = 2 * F // N_DEV
    g_cw = lax.dynamic_slice(g_cw3, (me // (N_DEV // 2), 0, (me % (N_DEV // 2)) * c_cw), (1, 3, c_cw))[0]
    cw_res = _adamw_plain(g_cw, conv_w[0], m_conv_w[0], v_conv_w[0], "adamw_conv_w")
    big_out = {"conv_w": [g_cw[None]] + [a[None] for a in cw_res]}

    moments = dict(w_in=(m_w_in, v_w_in), w_mem_kv=(m_w_mem_kv, v_w_mem_kv), w_branch_a=(m_w_branch_a, v_w_branch_a),
                   w_branch_b=(m_w_branch_b, v_w_branch_b), w_branch_c=(m_w_branch_c, v_w_branch_c), w_out=(m_w_out, v_w_out),
                   w_up=(m_w_up, v_w_up), w_down=(m_w_down, v_w_down))
    token = (grad_x, small_res[0][0])
    for k in ["w_down", "w_up", "w_out", "w_branch_a", "w_branch_b", "w_branch_c", "w_mem_kv", "w_in"]:
        s, r = reduced[k]
        mk, vk = moments[k][0][0], moments[k][1][0]
        if k == "w_in":
            res = _adamw_big(s, r, chip, big[k], mk.T, vk.T, "adamw_" + k, after=token)
            big_out[k] = [a.T[None] for a in res]
        else:
            res = _adamw_big(s, r, chip, big[k], mk, vk, "adamw_" + k, after=token)
            big_out[k] = [a[None] for a in res]
        token = (res[0],)

    order = ["g_mix", "w_in", "g_a_v", "w_spatial", "b_spatial", "g_b_q", "g_b_k", "sinks", "g_mem", "w_mem_kv", "g_c_q", "g_c_k",
             "w_branch_a", "w_branch_b", "w_branch_c", "w_out", "g_ffn", "w_up", "conv_w", "conv_b", "w_down"]
    res = {**small_out, **big_out}
    outs = [loss, grad_x[None]]
    for field in range(4):
        outs += [res[k][field] for k in order]
    return tuple(outs)
```

```python
import functools

import jax
import jax.numpy as jnp
from jax import lax
from jax.experimental import pallas as pl
from jax.experimental.pallas import tpu as pltpu
from jax.experimental.pallas import tpu_sc as plsc

F32 = jnp.float32
BF = jnp.bfloat16
EPS = 1e-6
NEG = -1e30

N_DEV = 8
CHUNK = 128
A_GROUPS = 4
A_WIDTH = 512
B_HEADS = 16
B_KV_HEADS = 2
B_HEAD_DIM = 64
B_WIDTH = 1024
B_KV_WIDTH = 128
ROPE_DIM = 16
ROPE_THETA = 500000.0
C_HEADS = 4
C_HEAD_DIM = 128
C_WIDTH = 512
GATE_OFF = 2 * A_WIDTH + B_WIDTH + 2 * B_KV_WIDTH + C_WIDTH

ADAM_LR = 0.001
ADAM_B1 = 0.9
ADAM_B2 = 0.999
ADAM_EPS = 1e-08
ADAM_WD = 0.01
ADAM_STEP = 10

VMEM_LIMIT = 48 * 1024 * 1024
MESH = pl.DeviceIdType.MESH


def _pick(n, prefs):
    for p in prefs:
        if p <= n and n % p == 0:
            return p
    return n


def _params(sem):
    return pltpu.CompilerParams(dimension_semantics=sem, vmem_limit_bytes=VMEM_LIMIT)


def _hide(body, n_seen, n_hidden):
    if not n_hidden:
        return body

    def wrapped(*refs):
        return body(*refs[:n_seen], *refs[n_seen + n_hidden:])

    return wrapped


def _hidden_specs(after):
    return [pl.BlockSpec(memory_space=pl.ANY) for _ in after]


def _token(xs, name):
    def body(*refs):
        refs[-1][...] = jnp.zeros_like(refs[-1])

    return pl.pallas_call(body, name=name, in_specs=_hidden_specs(xs), out_shape=jax.ShapeDtypeStruct((8, 128), F32))(*xs)


def _mm(a, b, mode, out_dtype, name, *, resid=None, b_stack=False, a_parts=0, b_parts=0, out_parts=0,
        out_stack=False, tm=1024, tn=1024, tk=2048, after=(), exchange=()):
    if mode == "nn":
        M = a.shape[-2]
        K = a.shape[-1] * max(a_parts, 1)
        N = b.shape[-1] * (N_DEV if b_stack else 1)
        dims = (((1,), (0,)), ((), ()))
    elif mode == "nt":
        M = a.shape[-2]
        K = a.shape[-1] * max(a_parts, 1)
        N = b.shape[-2]
        dims = (((1,), (1,)), ((), ()))
    else:
        K = a.shape[-2]
        M = a.shape[-1]
        N = b.shape[-1] * max(b_parts, 1)
        dims = (((0,), (0,)), ((), ()))
    if b_stack and mode == "nn":
        tn = b.shape[-1]
    if b_stack and mode == "nt":
        tk = b.shape[-1]
    if out_stack:
        tn = N // N_DEV
    tm, tn, tk = _pick(M, (tm,)), _pick(N, (tn,)), _pick(K, (tk,))
    if M % tm or N % tn or K % tk:
        raise ValueError(f"{name}: tiles {tm},{tn},{tk} do not divide {M},{N},{K}")
    nm, nn, nk = M // tm, N // tn, K // tk

    def parts_idx(t, ntile, parts):
        per = ntile // parts
        return t // per, t % per

    if mode in ("nn", "nt"):
        if a_parts:
            a_spec = pl.BlockSpec((None, tm, tk), lambda m, n, k: (parts_idx(k, nk, a_parts)[0], m, parts_idx(k, nk, a_parts)[1]))
        else:
            a_spec = pl.BlockSpec((tm, tk), lambda m, n, k: (m, k))
    else:
        a_spec = pl.BlockSpec((tk, tm), lambda m, n, k: (k, m))
    if mode == "nn":
        if b_stack:
            b_spec = pl.BlockSpec((None, tk, tn), lambda m, n, k: (n, k, 0))
        else:
            b_spec = pl.BlockSpec((tk, tn), lambda m, n, k: (k, n))
    elif mode == "nt":
        if b_stack:
            b_spec = pl.BlockSpec((None, tn, tk), lambda m, n, k: (k, n, 0))
        else:
            b_spec = pl.BlockSpec((tn, tk), lambda m, n, k: (n, k))
    else:
        if b_parts:
            b_spec = pl.BlockSpec((None, tk, tn), lambda m, n, k: (parts_idx(n, nn, b_parts)[0], k, parts_idx(n, nn, b_parts)[1]))
        else:
            b_spec = pl.BlockSpec((tk, tn), lambda m, n, k: (k, n))
    if out_stack:
        out_shape = jax.ShapeDtypeStruct((N_DEV, M, tn), out_dtype)
        o_spec = pl.BlockSpec((None, tm, tn), lambda m, n, k: (n, m, 0))
    elif out_parts:
        out_shape = jax.ShapeDtypeStruct((out_parts, M, N // out_parts), out_dtype)
        o_spec = pl.BlockSpec((None, tm, tn), lambda m, n, k: (parts_idx(n, nn, out_parts)[0], m, parts_idx(n, nn, out_parts)[1]))
    else:
        out_shape = jax.ShapeDtypeStruct((M, N), out_dtype)
        o_spec = pl.BlockSpec((tm, tn), lambda m, n, k: (m, n))
    has_resid = resid is not None

    n_ex = len(exchange)
    n_in = 2 + has_resid + len(after)

    def body(*refs):
        a_ref, b_ref = refs[:2]
        r_ref = refs[2] if has_resid else None
        ex_in = refs[n_in:n_in + n_ex]
        o_ref = refs[n_in + n_ex]
        ex_out = refs[n_in + n_ex + 1:n_in + 2 * n_ex + 1]
        scratch = refs[n_in + 2 * n_ex + 1:]
        m_i, n_i, k = pl.program_id(0), pl.program_id(1), pl.program_id(2)

        def pushes():
            send_sems, recv_sems = scratch[-2:]
            x, y, c = lax.axis_index("x"), lax.axis_index("y"), lax.axis_index("c")
            return [pltpu.make_async_remote_copy(
                src_ref=ex_in[w].at[:, 1 - c], dst_ref=ex_out[w], send_sem=send_sems.at[w], recv_sem=recv_sems.at[w],
                device_id=(x, y, 1 - c), device_id_type=MESH) for w in range(n_ex)]

        if n_ex:
            @pl.when((m_i == 0) & (n_i == 0) & (k == 0))
            def _():
                for cp in pushes():
                    cp.start()

        if nk == 1:
            res = lax.dot_general(a_ref[...], b_ref[...], dims, preferred_element_type=F32)
            if has_resid:
                res = res + r_ref[...]
            o_ref[...] = res.astype(o_ref.dtype)
        else:
            acc = scratch[0]

            @pl.when(k == 0)
            def _():
                acc[...] = jnp.zeros_like(acc)

            acc[...] += lax.dot_general(a_ref[...], b_ref[...], dims, preferred_element_type=F32)

            @pl.when(k == nk - 1)
            def _():
                res = acc[...]
                if has_resid:
                    res = res + r_ref[...]
                o_ref[...] = res.astype(o_ref.dtype)

        if n_ex:
            @pl.when((m_i == nm - 1) & (n_i == nn - 1) & (k == nk - 1))
            def _():
                for cp in pushes():
                    cp.wait()

    in_specs = [a_spec, b_spec]
    args = [a, b]
    if has_resid:
        in_specs.append(pl.BlockSpec((tm, tn), lambda m, n, k: (m, n)))
        args.append(resid)
    in_specs += _hidden_specs(after) + _hidden_specs(exchange)
    args += list(after) + list(exchange)
    scratch_shapes = [pltpu.VMEM((tm, tn), F32)] if nk > 1 else []
    if not n_ex:
        return pl.pallas_call(
            body, name=name, grid=(nm, nn, nk), in_specs=in_specs, out_specs=o_spec, out_shape=out_shape,
            scratch_shapes=scratch_shapes, compiler_params=_params(("parallel", "parallel", "arbitrary")),
        )(*args)
    res = pl.pallas_call(
        body, name=name, grid=(nm, nn, nk), in_specs=in_specs, out_specs=[o_spec] + _hidden_specs(exchange),
        out_shape=[out_shape] + [jax.ShapeDtypeStruct((g.shape[0],) + g.shape[2:], g.dtype) for g in exchange],
        scratch_shapes=scratch_shapes + [pltpu.SemaphoreType.DMA((n_ex,)), pltpu.SemaphoreType.DMA((n_ex,))],
        compiler_params=_params(("arbitrary", "arbitrary", "arbitrary")),
    )(*args)
    return res[0], list(res[1:])


def _rms_fwd(x, g, name):
    R, D = x.shape
    tr = _pick(R, (256,))

    def body(x_ref, g_ref, h_ref, r_ref):
        xv = x_ref[...]
        r = lax.rsqrt(jnp.mean(xv * xv, axis=-1, keepdims=True) + EPS)
        h_ref[...] = (xv * r * g_ref[...]).astype(BF)
        r_ref[...] = r

    return pl.pallas_call(
        body, name=name, grid=(R // tr,),
        in_specs=[pl.BlockSpec((tr, D), lambda i: (i, 0)), pl.BlockSpec((1, D), lambda i: (0, 0))],
        out_specs=[pl.BlockSpec((tr, D), lambda i: (i, 0)), pl.BlockSpec((tr, 1), lambda i: (i, 0))],
        out_shape=[jax.ShapeDtypeStruct((R, D), BF), jax.ShapeDtypeStruct((R, 1), F32)],
        compiler_params=_params(("parallel",)),
    )(x, g)


def _rms_bwd(x, r, g, dh, dres, name, after=()):
    R, D = x.shape
    tr = _pick(R, (256,))
    has_res = dres is not None

    def body(*refs):
        if has_res:
            x_ref, r_ref, g_ref, dh_ref, dres_ref, dx_ref, dxb_ref, dg_ref = refs
        else:
            x_ref, r_ref, g_ref, dh_ref, dx_ref, dxb_ref, dg_ref = refs
        i = pl.program_id(0)
        xv, rv, dhv = x_ref[...], r_ref[...], dh_ref[...]
        gy = dhv * g_ref[...]
        c = jnp.sum(xv * gy, axis=-1, keepdims=True)
        dx = rv * gy - xv * (rv * rv * rv) * (c * (1.0 / D))
        if has_res:
            dx = dx + dres_ref[...]
        dx_ref[...] = dx
        dxb_ref[...] = dx.astype(BF)
        part = jnp.sum(dhv * xv * rv, axis=0, keepdims=True)

        @pl.when(i == 0)
        def _():
            dg_ref[...] = part

        @pl.when(i > 0)
        def _():
            dg_ref[...] += part

    row = pl.BlockSpec((tr, D), lambda i: (i, 0))
    in_specs = [row, pl.BlockSpec((tr, 1), lambda i: (i, 0)), pl.BlockSpec((1, D), lambda i: (0, 0)), row]
    args = [x, r, g, dh]
    if has_res:
        in_specs.append(row)
        args.append(dres)
    return pl.pallas_call(
        _hide(body, len(args), len(after)), name=name, grid=(R // tr,), in_specs=in_specs + _hidden_specs(after),
        out_specs=[row, row, pl.BlockSpec((1, D), lambda i: (0, 0))],
        out_shape=[jax.ShapeDtypeStruct((R, D), F32), jax.ShapeDtypeStruct((R, D), BF), jax.ShapeDtypeStruct((1, D), F32)],
        compiler_params=_params(("arbitrary",)),
    )(*args, *after)


def _a_chunk(us, vs, gvs, ws, bs):
    r_i = lax.broadcasted_iota(jnp.int32, (CHUNK, CHUNK), 0)
    c_i = lax.broadcasted_iota(jnp.int32, (CHUNK, CHUNK), 1)
    causal = r_i >= c_i
    vg = [jax.nn.gelu(v) for v in vs]
    ss = sum(jnp.sum(v * v, axis=-1, keepdims=True) for v in vg)
    r = lax.rsqrt(ss * (1.0 / A_WIDTH) + EPS)
    ys = []
    for g in range(A_GROUPS):
        vn = vg[g] * r * gvs[g]
        w = jnp.where(causal, ws[g], 0.0)
        s = jnp.dot(w.astype(BF), vn.astype(BF), preferred_element_type=F32) + bs[g]
        ys.append(jax.nn.gelu(us[g]) * s)
    return ys


def _a_split(u_ref, v_ref, g_ref, w_ref, b_ref):
    sl = [slice(g * 128, (g + 1) * 128) for g in range(A_GROUPS)]
    return ([u_ref[:, s] for s in sl], [v_ref[:, s] for s in sl], [g_ref[:, s] for s in sl],
            [w_ref[g] for g in range(A_GROUPS)], [b_ref[:, g:g + 1] for g in range(A_GROUPS)])


def _a_specs(S):
    return [pl.BlockSpec((CHUNK, A_WIDTH), lambda n: (n, 0)), pl.BlockSpec((CHUNK, A_WIDTH), lambda n: (n, 1)),
            pl.BlockSpec((1, A_WIDTH), lambda n: (0, 0)), pl.BlockSpec((A_GROUPS, CHUNK, CHUNK), lambda n: (0, 0, 0)),
            pl.BlockSpec((CHUNK, A_GROUPS), lambda n: (0, 0))]


def _a_fwd(proj, g_v, w_s, b_t):
    S = proj.shape[0]

    def body(u_ref, v_ref, g_ref, w_ref, b_ref, y_ref):
        ys = _a_chunk(*_a_split(u_ref, v_ref, g_ref, w_ref, b_ref))
        for g in range(A_GROUPS):
            y_ref[:, g * 128:(g + 1) * 128] = ys[g].astype(BF)

    return pl.pallas_call(
        body, name="a_fwd", grid=(S // CHUNK,), in_specs=_a_specs(S),
        out_specs=pl.BlockSpec((CHUNK, A_WIDTH), lambda n: (n, 0)),
        out_shape=jax.ShapeDtypeStruct((S, A_WIDTH), BF), compiler_params=_params(("parallel",)),
    )(proj, proj, g_v, w_s, b_t)


def _a_bwd(proj, g_v, w_s, b_t, dy, after=()):
    S = proj.shape[0]

    def body(u_ref, v_ref, g_ref, w_ref, b_ref, dy_ref, duv_ref, dg_ref, dw_ref, db_ref):
        n = pl.program_id(0)
        dys = [dy_ref[:, g * 128:(g + 1) * 128] for g in range(A_GROUPS)]
        _, vjp = jax.vjp(_a_chunk, *_a_split(u_ref, v_ref, g_ref, w_ref, b_ref))
        dus, dvs, dgs, dws, dbs = vjp(dys)

        @pl.when(n == 0)
        def _():
            dg_ref[...] = jnp.zeros_like(dg_ref)
            dw_ref[...] = jnp.zeros_like(dw_ref)
            db_ref[...] = jnp.zeros_like(db_ref)

        for g in range(A_GROUPS):
            duv_ref[:, g * 128:(g + 1) * 128] = dus[g].astype(BF)
            duv_ref[:, A_WIDTH + g * 128:A_WIDTH + (g + 1) * 128] = dvs[g].astype(BF)
            dg_ref[:, g * 128:(g + 1) * 128] += dgs[g]
            dw_ref[g] += dws[g]
            db_ref[:, g:g + 1] += dbs[g]

    return pl.pallas_call(
        _hide(body, 6, len(after)), name="a_bwd", grid=(S // CHUNK,),
        in_specs=_a_specs(S) + [pl.BlockSpec((CHUNK, A_WIDTH), lambda n: (n, 0))] + _hidden_specs(after),
        out_specs=[pl.BlockSpec((CHUNK, 2 * A_WIDTH), lambda n: (n, 0)), pl.BlockSpec((1, A_WIDTH), lambda n: (0, 0)),
                   pl.BlockSpec((A_GROUPS, CHUNK, CHUNK), lambda n: (0, 0, 0)), pl.BlockSpec((CHUNK, A_GROUPS), lambda n: (0, 0))],
        out_shape=[jax.ShapeDtypeStruct((S, 2 * A_WIDTH), BF), jax.ShapeDtypeStruct((1, A_WIDTH), F32),
                   jax.ShapeDtypeStruct((A_GROUPS, CHUNK, CHUNK), F32), jax.ShapeDtypeStruct((CHUNK, A_GROUPS), F32)],
        compiler_params=_params(("arbitrary",)),
    )(proj, proj, g_v, w_s, b_t, dy, *after)


def _half_mask(shape, which):
    lane = lax.broadcasted_iota(jnp.int32, shape, len(shape) - 1)
    return (lane >= 64) == (which == 1)


def _pair_norm_rope(x, g, ct, sa, sb):
    lo = _half_mask(x.shape, 0)
    x2 = x * x
    ss_lo = jnp.sum(jnp.where(lo, x2, 0.0), axis=-1, keepdims=True)
    ss_hi = jnp.sum(jnp.where(lo, 0.0, x2), axis=-1, keepdims=True)
    r = jnp.where(lo, lax.rsqrt(ss_lo * (1.0 / B_HEAD_DIM) + EPS), lax.rsqrt(ss_hi * (1.0 / B_HEAD_DIM) + EPS))
    xr = x * r
    xn = xr * g
    out = xn * ct + pltpu.roll(xn, 120, 1) * sa + pltpu.roll(xn, 8, 1) * sb
    return out, xr, r


def _pair_norm_rope_bwd(x, g, ct, sa, sb, dout):
    lo = _half_mask(x.shape, 0)
    _, xr, r = _pair_norm_rope(x, g, ct, sa, sb)
    dxn = dout * ct + pltpu.roll(dout * sa, 8, 1) + pltpu.roll(dout * sb, 120, 1)
    gy = dxn * g
    t = xr * gy
    c_lo = jnp.sum(jnp.where(lo, t, 0.0), axis=-1, keepdims=True)
    c_hi = jnp.sum(jnp.where(lo, 0.0, t), axis=-1, keepdims=True)
    c = jnp.where(lo, c_lo, c_hi)
    dx = r * (gy - xr * c * (1.0 / B_HEAD_DIM))
    dg = jnp.sum(dxn * xr, axis=0, keepdims=True)
    return dx, dg


def _b_pre(proj, gq2, gk2, ct, sa, sb):
    S = proj.shape[0]
    tr = _pick(S, (256,))
    n_pair = B_WIDTH // 128

    def body(q_ref, k_ref, gq_ref, gk_ref, ct_ref, sa_ref, sb_ref, qn_ref, kn_ref):
        ct_v, sa_v, sb_v = ct_ref[...], sa_ref[...], sb_ref[...]
        for p in range(n_pair):
            o, _, _ = _pair_norm_rope(q_ref[:, p * 128:(p + 1) * 128], gq_ref[...], ct_v, sa_v, sb_v)
            qn_ref[:, p * 128:(p + 1) * 128] = o.astype(BF)
        o, _, _ = _pair_norm_rope(k_ref[...], gk_ref[...], ct_v, sa_v, sb_v)
        kn_ref[...] = o.astype(BF)

    tab = pl.BlockSpec((tr, 128), lambda i: (i, 0))
    gsp = pl.BlockSpec((1, 128), lambda i: (0, 0))
    return pl.pallas_call(
        body, name="b_pre", grid=(S // tr,),
        in_specs=[pl.BlockSpec((tr, B_WIDTH), lambda i: (i, 1)), pl.BlockSpec((tr, 128), lambda i: (i, 2 * B_WIDTH // 128)),
                  gsp, gsp, tab, tab, tab],
        out_specs=[pl.BlockSpec((tr, B_WIDTH), lambda i: (i, 0)), tab],
        out_shape=[jax.ShapeDtypeStruct((S, B_WIDTH), BF), jax.ShapeDtypeStruct((S, 128), BF)],
        compiler_params=_params(("parallel",)),
    )(proj, proj, gq2, gk2, ct, sa, sb)


def _b_pre_bwd(proj, gq2, gk2, ct, sa, sb, dqn, dkn, dv):
    S = proj.shape[0]
    tr = _pick(S, (256,))
    n_pair = B_WIDTH // 128

    def body(q_ref, k_ref, gq_ref, gk_ref, ct_ref, sa_ref, sb_ref, dqn_ref, dkn_ref, dv_ref, dqkv_ref, dgq_ref, dgk_ref):
        i = pl.program_id(0)
        ct_v, sa_v, sb_v = ct_ref[...], sa_ref[...], sb_ref[...]
        dgq = jnp.zeros((1, 128), F32)
        for p in range(n_pair):
            sl = slice(p * 128, (p + 1) * 128)
            dx, dg = _pair_norm_rope_bwd(q_ref[:, sl], gq_ref[...], ct_v, sa_v, sb_v, dqn_ref[:, sl])
            dqkv_ref[:, sl] = dx.astype(BF)
            dgq = dgq + dg
        dx, dgk = _pair_norm_rope_bwd(k_ref[...], gk_ref[...], ct_v, sa_v, sb_v, dkn_ref[...])
        dqkv_ref[:, B_WIDTH:B_WIDTH + 128] = dx.astype(BF)
        dqkv_ref[:, B_WIDTH + 128:B_WIDTH + 256] = dv_ref[...].astype(BF)

        @pl.when(i == 0)
        def _():
            dgq_ref[...] = dgq
            dgk_ref[...] = dgk

        @pl.when(i > 0)
        def _():
            dgq_ref[...] += dgq
            dgk_ref[...] += dgk

    tab = pl.BlockSpec((tr, 128), lambda i: (i, 0))
    gsp = pl.BlockSpec((1, 128), lambda i: (0, 0))
    return pl.pallas_call(
        body, name="b_pre_bwd", grid=(S // tr,),
        in_specs=[pl.BlockSpec((tr, B_WIDTH), lambda i: (i, 1)), pl.BlockSpec((tr, 128), lambda i: (i, 2 * B_WIDTH // 128)),
                  gsp, gsp, tab, tab, tab, pl.BlockSpec((tr, B_WIDTH), lambda i: (i, 0)), tab, tab],
        out_specs=[pl.BlockSpec((tr, B_WIDTH + 256), lambda i: (i, 0)), gsp, gsp],
        out_shape=[jax.ShapeDtypeStruct((S, B_WIDTH + 256), BF), jax.ShapeDtypeStruct((1, 128), F32), jax.ShapeDtypeStruct((1, 128), F32)],
        compiler_params=_params(("arbitrary",)),
    )(proj, proj, gq2, gk2, ct, sa, sb, dqn, dkn, dv)


def _b_dup(x2, g):
    d = jnp.where(_half_mask(x2.shape, g), x2, 0.0)
    return (d + pltpu.roll(d, 64, 1)).astype(BF)


PAIRS_PER_GROUP = B_HEADS // B_KV_HEADS // 2
GROUP_ROWS = PAIRS_PER_GROUP * CHUNK


def _b_valid(n):
    row = lax.broadcasted_iota(jnp.int32, (GROUP_ROWS, 2 * CHUNK), 0) & (CHUNK - 1)
    col = lax.broadcasted_iota(jnp.int32, (GROUP_ROWS, 2 * CHUNK), 1)
    rel = row + CHUNK - col
    return (rel >= 0) & (rel < CHUNK) & ((col >= CHUNK) | (n > 0))


def _b_blocks(x2, g):
    xd = _b_dup(x2, g)
    lo = _half_mask(xd.shape, 0)
    zero = jnp.zeros_like(xd)
    return jnp.concatenate([jnp.where(lo, xd, zero), jnp.where(lo, zero, xd)], axis=0)


def _b_sink_col(s_ref, g, hf):
    rb = lax.broadcasted_iota(jnp.int32, (GROUP_ROWS, 1), 0) // CHUNK
    col = jnp.zeros((GROUP_ROWS, 1), F32)
    for pp in range(PAIRS_PER_GROUP):
        col = jnp.where(rb == pp, s_ref[0, 2 * (g * PAIRS_PER_GROUP + pp) + hf], col)
    return col


def _b_probs(qs, kblk, valid, sinks):
    s = lax.dot_general(qs, kblk, (((1,), (1,)), ((), ())), preferred_element_type=F32) * (B_HEAD_DIM ** -0.5)
    out = []
    for hf in range(2):
        sh = jnp.where(valid, s[:, hf * 2 * CHUNK:(hf + 1) * 2 * CHUNK], NEG)
        m = jnp.maximum(jnp.max(sh, axis=-1, keepdims=True), sinks[hf])
        e = jnp.exp(sh - m)
        es = jnp.exp(sinks[hf] - m)
        inv = 1.0 / (jnp.sum(e, axis=-1, keepdims=True) + es)
        out.append((e * inv, es * inv))
    return out


def _b_fold(acc, g):
    lo = _half_mask((2 * CHUNK, 128), 0)
    t = jnp.where(lo, acc[:2 * CHUNK], 0.0) + jnp.where(lo, 0.0, acc[2 * CHUNK:])
    return jnp.where(_half_mask((2 * CHUNK, 128), g), t + pltpu.roll(t, 64, 1), 0.0)


def _b_kv_specs(S):
    prev = lambda n: (jnp.maximum(n - 1, 0), 0)
    cur = lambda n: (n, 0)
    v_col = (2 * B_WIDTH + B_KV_WIDTH) // 128
    return [pl.BlockSpec((CHUNK, 128), prev), pl.BlockSpec((CHUNK, 128), cur),
            pl.BlockSpec((CHUNK, 128), lambda n: (jnp.maximum(n - 1, 0), v_col)), pl.BlockSpec((CHUNK, 128), lambda n: (n, v_col))]


def _b_attn_fwd(qn, kn, proj, sinks):
    S = qn.shape[0]

    def body(s_ref, q_ref, kp_ref, kc_ref, vp_ref, vc_ref, y_ref):
        n = pl.program_id(0)
        valid = _b_valid(n)
        k2 = jnp.concatenate([kp_ref[...], kc_ref[...]], axis=0).astype(F32)
        v2 = jnp.concatenate([vp_ref[...], vc_ref[...]], axis=0)
        for g in range(B_KV_HEADS):
            pairs = [g * PAIRS_PER_GROUP + pp for pp in range(PAIRS_PER_GROUP)]
            qs = jnp.concatenate([q_ref[:, p * 128:(p + 1) * 128] for p in pairs], axis=0)
            probs = _b_probs(qs, _b_blocks(k2, g), valid, [_b_sink_col(s_ref, g, hf) for hf in range(2)])
            pcat = jnp.concatenate([probs[0][0].astype(BF), probs[1][0].astype(BF)], axis=1)
            o = jnp.dot(pcat, _b_blocks(v2, g), preferred_element_type=F32)
            for pp, p in enumerate(pairs):
                y_ref[:, p * 128:(p + 1) * 128] = o[pp * CHUNK:(pp + 1) * CHUNK].astype(BF)

    return pl.pallas_call(
        body, name="b_attn_fwd", grid=(S // CHUNK,),
        in_specs=[pl.BlockSpec(memory_space=pltpu.SMEM), pl.BlockSpec((CHUNK, B_WIDTH), lambda n: (n, 0))] + _b_kv_specs(S),
        out_specs=pl.BlockSpec((CHUNK, B_WIDTH), lambda n: (n, 0)),
        out_shape=jax.ShapeDtypeStruct((S, B_WIDTH), BF), compiler_params=_params(("arbitrary",)),
    )(sinks, qn, kn, kn, proj, proj)


def _b_attn_bwd(qn, kn, proj, sinks, dy, after=()):
    S = qn.shape[0]

    def body(s_ref, q_ref, kp_ref, kc_ref, vp_ref, vc_ref, dy_ref, dq_ref, dk_ref, dv_ref, ds_ref):
        n = pl.program_id(0)

        @pl.when(n == 0)
        def _():
            dk_ref[...] = jnp.zeros_like(dk_ref)
            dv_ref[...] = jnp.zeros_like(dv_ref)
            ds_ref[...] = jnp.zeros_like(ds_ref)

        valid = _b_valid(n)
        k2 = jnp.concatenate([kp_ref[...], kc_ref[...]], axis=0).astype(F32)
        v2 = jnp.concatenate([vp_ref[...], vc_ref[...]], axis=0)
        lane = lax.broadcasted_iota(jnp.int32, (CHUNK, 128), 1)
        dk2 = jnp.zeros((2 * CHUNK, 128), F32)
        dv2 = jnp.zeros((2 * CHUNK, 128), F32)
        dsink = jnp.zeros((CHUNK, 128), F32)
        scale = B_HEAD_DIM ** -0.5
        nt = (((1,), (1,)), ((), ()))
        tn = (((0,), (0,)), ((), ()))
        for g in range(B_KV_HEADS):
            pairs = [g * PAIRS_PER_GROUP + pp for pp in range(PAIRS_PER_GROUP)]
            qs = jnp.concatenate([q_ref[:, p * 128:(p + 1) * 128] for p in pairs], axis=0)
            do = jnp.concatenate([dy_ref[:, p * 128:(p + 1) * 128] for p in pairs], axis=0)
            do_b = do.astype(BF)
            kblk, vblk = _b_blocks(k2, g), _b_blocks(v2, g)
            probs = _b_probs(qs, kblk, valid, [_b_sink_col(s_ref, g, hf) for hf in range(2)])
            pcat = jnp.concatenate([probs[0][0].astype(BF), probs[1][0].astype(BF)], axis=1)
            o = jnp.dot(pcat, vblk, preferred_element_type=F32)
            dp = lax.dot_general(do_b, vblk, nt, preferred_element_type=F32)
            prod = do * o
            ds_halves = []
            for hf in range(2):
                pr, ps = probs[hf]
                delta = jnp.sum(jnp.where(_half_mask(prod.shape, hf), prod, 0.0), axis=-1, keepdims=True)
                ds_halves.append((pr * (dp[:, hf * 2 * CHUNK:(hf + 1) * 2 * CHUNK] - delta) * scale).astype(BF))
                t = -ps * delta
                for pp, p in enumerate(pairs):
                    dsink = dsink + jnp.where(lane == 2 * p + hf, t[pp * CHUNK:(pp + 1) * CHUNK], 0.0)
            dsc = jnp.concatenate(ds_halves, axis=1)
            dq = jnp.dot(dsc, kblk, preferred_element_type=F32)
            for pp, p in enumerate(pairs):
                dq_ref[:, p * 128:(p + 1) * 128] = dq[pp * CHUNK:(pp + 1) * CHUNK]
            dk2 = dk2 + _b_fold(lax.dot_general(dsc, qs, tn, preferred_element_type=F32), g)
            dv2 = dv2 + _b_fold(lax.dot_general(pcat, do_b, tn, preferred_element_type=F32), g)
        ds_ref[...] += dsink
        cur = pl.ds(pl.multiple_of(n * CHUNK, CHUNK), CHUNK)
        dk_ref[cur, :] += dk2[CHUNK:]
        dv_ref[cur, :] += dv2[CHUNK:]

        @pl.when(n > 0)
        def _():
            prv = pl.ds(pl.multiple_of((n - 1) * CHUNK, CHUNK), CHUNK)
            dk_ref[prv, :] += dk2[:CHUNK]
            dv_ref[prv, :] += dv2[:CHUNK]

    full = pl.BlockSpec((S, 128), lambda n: (0, 0))
    return pl.pallas_call(
        _hide(body, 7, len(after)), name="b_attn_bwd", grid=(S // CHUNK,),
        in_specs=[pl.BlockSpec(memory_space=pltpu.SMEM), pl.BlockSpec((CHUNK, B_WIDTH), lambda n: (n, 0))] + _b_kv_specs(S)
        + [pl.BlockSpec((CHUNK, B_WIDTH), lambda n: (n, 0))] + _hidden_specs(after),
        out_specs=[pl.BlockSpec((CHUNK, B_WIDTH), lambda n: (n, 0)), full, full, pl.BlockSpec((CHUNK, 128), lambda n: (0, 0))],
        out_shape=[jax.ShapeDtypeStruct((S, B_WIDTH), F32), jax.ShapeDtypeStruct((S, 128), F32), jax.ShapeDtypeStruct((S, 128), F32),
                   jax.ShapeDtypeStruct((CHUNK, 128), F32)],
        compiler_params=_params(("arbitrary",)),
    )(sinks, qn, kn, kn, proj, proj, dy, *after)


def _c_block(q, k, v, gq, gk):
    qn = q * lax.rsqrt(jnp.mean(q * q, axis=-1, keepdims=True) + EPS) * gq
    kn = k * lax.rsqrt(jnp.mean(k * k, axis=-1, keepdims=True) + EPS) * gk
    s = lax.dot_general(qn.astype(BF), kn.astype(BF), (((1,), (1,)), ((), ())), preferred_element_type=F32) * (C_HEAD_DIM ** -0.5)
    p = jax.nn.softmax(s, axis=-1)
    return jnp.dot(p.astype(BF), v.astype(BF), preferred_element_type=F32)


def _c_specs(S, M, tq):
    q_col = (2 * A_WIDTH + B_WIDTH + 2 * B_KV_WIDTH) // 128
    return [pl.BlockSpec((tq, 128), lambda h, i: (i, q_col + h)), pl.BlockSpec((M, 128), lambda h, i: (0, h)),
            pl.BlockSpec((M, 128), lambda h, i: (0, C_HEADS + h)), pl.BlockSpec((1, 128), lambda h, i: (0, 0)),
            pl.BlockSpec((1, 128), lambda h, i: (0, 0))]


def _c_fwd(proj, kv, gq, gk):
    S, M = proj.shape[0], kv.shape[0]
    tq = _pick(S, (512,))

    def body(q_ref, k_ref, v_ref, gq_ref, gk_ref, y_ref):
        y_ref[...] = _c_block(q_ref[...], k_ref[...], v_ref[...], gq_ref[...], gk_ref[...]).astype(BF)

    return pl.pallas_call(
        body, name="c_fwd", grid=(C_HEADS, S // tq), in_specs=_c_specs(S, M, tq),
        out_specs=pl.BlockSpec((tq, 128), lambda h, i: (i, h)),
        out_shape=jax.ShapeDtypeStruct((S, C_WIDTH), BF), compiler_params=_params(("parallel", "parallel")),
    )(proj, kv, kv, gq, gk)


def _c_bwd(proj, kv, gq, gk, dy):
    S, M = proj.shape[0], kv.shape[0]
    tq = _pick(S, (512,))

    def body(q_ref, k_ref, v_ref, gq_ref, gk_ref, dy_ref, dq_ref, dk_ref, dv_ref, dgq_ref, dgk_ref):
        i = pl.program_id(1)
        _, vjp = jax.vjp(_c_block, q_ref[...], k_ref[...], v_ref[...], gq_ref[...], gk_ref[...])
        dq, dk, dv, dgq, dgk = vjp(dy_ref[...])
        dq_ref[...] = dq.astype(BF)

        @pl.when(i == 0)
        def _():
            dk_ref[...] = dk
            dv_ref[...] = dv
            dgq_ref[...] = dgq
            dgk_ref[...] = dgk

        @pl.when(i > 0)
        def _():
            dk_ref[...] += dk
            dv_ref[...] += dv
            dgq_ref[...] += dgq
            dgk_ref[...] += dgk

    return pl.pallas_call(
        body, name="c_bwd", grid=(C_HEADS, S // tq),
        in_specs=_c_specs(S, M, tq) + [pl.BlockSpec((tq, 128), lambda h, i: (i, h))],
        out_specs=[pl.BlockSpec((tq, 128), lambda h, i: (i, h)), pl.BlockSpec((M, 128), lambda h, i: (0, h)),
                   pl.BlockSpec((M, 128), lambda h, i: (0, h)), pl.BlockSpec((None, 1, 128), lambda h, i: (h, 0, 0)),
                   pl.BlockSpec((None, 1, 128), lambda h, i: (h, 0, 0))],
        out_shape=[jax.ShapeDtypeStruct((S, C_WIDTH), BF), jax.ShapeDtypeStruct((M, C_WIDTH), F32), jax.ShapeDtypeStruct((M, C_WIDTH), F32),
                   jax.ShapeDtypeStruct((C_HEADS, 1, 128), F32), jax.ShapeDtypeStruct((C_HEADS, 1, 128), F32)],
        compiler_params=_params(("parallel", "arbitrary")),
    )(proj, kv, kv, gq, gk, dy)


def _merge_specs(S, D, tr, tc):
    off = GATE_OFF // tc
    nd = D // tc
    gates = [pl.BlockSpec((tr, tc), functools.partial(lambda b, i, j: (i, off + b * nd + j), b)) for b in range(3)]
    zs = [pl.BlockSpec((tr, tc), lambda i, j: (i, j)) for _ in range(3)]
    return gates + zs


def _merge_fwd(proj, za, zb, zc):
    S, D = za.shape
    tr, tc = _pick(S, (512,)), _pick(D, (256,))

    def body(ga_ref, gb_ref, gc_ref, za_ref, zb_ref, zc_ref, m_ref):
        acc = jax.nn.sigmoid(ga_ref[...]) * za_ref[...].astype(F32)
        acc = acc + jax.nn.sigmoid(gb_ref[...]) * zb_ref[...].astype(F32)
        acc = acc + jax.nn.sigmoid(gc_ref[...]) * zc_ref[...].astype(F32)
        m_ref[...] = acc.astype(BF)

    return pl.pallas_call(
        body, name="merge_fwd", grid=(S // tr, D // tc), in_specs=_merge_specs(S, D, tr, tc),
        out_specs=pl.BlockSpec((tr, tc), lambda i, j: (i, j)), out_shape=jax.ShapeDtypeStruct((S, D), BF),
        compiler_params=_params(("parallel", "parallel")),
    )(proj, proj, proj, za, zb, zc)


def _merge_bwd(proj, za, zb, zc, dm, after=()):
    S, D = za.shape
    tr, tc = _pick(S, (512,)), _pick(D, (256,))
    nd = D // tc

    def body(ga_ref, gb_ref, gc_ref, za_ref, zb_ref, zc_ref, dm_ref, dza_ref, dzb_ref, dzc_ref, dga_ref, dgb_ref, dgc_ref):
        dmv = dm_ref[...]
        for g_ref, z_ref, dz_ref, dg_ref in ((ga_ref, za_ref, dza_ref, dga_ref), (gb_ref, zb_ref, dzb_ref, dgb_ref),
                                             (gc_ref, zc_ref, dzc_ref, dgc_ref)):
            sg = jax.nn.sigmoid(g_ref[...])
            dz_ref[...] = (sg * dmv).astype(BF)
            dg_ref[...] = (dmv * z_ref[...].astype(F32) * sg * (1.0 - sg)).astype(BF)

    tile = pl.BlockSpec((tr, tc), lambda i, j: (i, j))
    return pl.pallas_call(
        _hide(body, 7, len(after)), name="merge_bwd", grid=(S // tr, D // tc),
        in_specs=_merge_specs(S, D, tr, tc) + [tile] + _hidden_specs(after),
        out_specs=[tile, tile, tile, tile, tile, tile],
        out_shape=[jax.ShapeDtypeStruct((S, D), BF)] * 6,
        compiler_params=_params(("parallel", "parallel")),
    )(proj, proj, proj, za, zb, zc, dm, *after)


PAD = 8


def _stage_shift_down(us_ref, u_ref):
    S = u_ref.shape[1]
    us_ref[:, 0:PAD, :] = jnp.zeros((2, PAD, us_ref.shape[2]), F32)
    us_ref[:, PAD:S + PAD, :] = u_ref[...]


ROWS = 32


def _conv3(us_ref, part, r0, w, b):
    return (us_ref[part, pl.ds(r0 + PAD, ROWS), :] * w[2:3] + us_ref[part, pl.ds(r0 + PAD - 1, ROWS), :] * w[1:2]
            + us_ref[part, pl.ds(r0 + PAD - 2, ROWS), :] * w[0:1] + b)


def _ffn_specs(S, F, tc, c):
    per = c // tc

    def w_spec(half):
        return pl.BlockSpec((None, 3, tc), lambda j: (half * (N_DEV // 2) + j // per, 0, j % per))

    return [pl.BlockSpec((2, S, tc), lambda j: (0, 0, j)), w_spec(0), w_spec(1), pl.BlockSpec((2, 1, tc), lambda j: (0, 0, j))]


def _ffn_tile(F, c):
    tc = 128
    if c % tc or F % tc:
        raise ValueError(f"ffn tile {tc} does not divide {c}, {F}")
    return tc


def _ffn_act_fwd(up3, cws, cb3):
    _, S, F = up3.shape
    c = cws.shape[2]
    tc = _ffn_tile(F, c)

    def body(u_ref, wa_ref, wb_ref, b_ref, o_ref, us_ref):
        _stage_shift_down(us_ref, u_ref)
        wa, wb, ba, bb = wa_ref[...], wb_ref[...], b_ref[0], b_ref[1]

        def step(i, carry):
            r0 = pl.multiple_of(i * ROWS, ROWS)
            ca = _conv3(us_ref, 0, r0, wa, ba)
            cb = _conv3(us_ref, 1, r0, wb, bb)
            o_ref[pl.ds(r0, ROWS), :] = (ca * jax.nn.sigmoid(ca) * cb).astype(BF)
            return carry

        lax.fori_loop(0, S // ROWS, step, 0, unroll=4)

    return pl.pallas_call(
        body, name="ffn_act_fwd", grid=(F // tc,), in_specs=_ffn_specs(S, F, tc, c),
        out_specs=pl.BlockSpec((S, tc), lambda j: (0, j)), out_shape=jax.ShapeDtypeStruct((S, F), BF),
        scratch_shapes=[pltpu.VMEM((2, S + PAD, tc), F32)],
        compiler_params=_params(("parallel",)),
    )(up3, cws, cws, cb3)


def _ffn_act_bwd(up3, cws, cb3, dact, after=()):
    _, S, F = up3.shape
    c = cws.shape[2]
    tc = _ffn_tile(F, c)

    def body(u_ref, wa_ref, wb_ref, b_ref, da_ref, du_ref, dw_ref, db_ref, us_ref, dcs_ref):
        _stage_shift_down(us_ref, u_ref)
        ws = (wa_ref[...], wb_ref[...])
        ba, bb = b_ref[0], b_ref[1]
        dcs_ref[:, S:S + PAD, :] = jnp.zeros((2, PAD, tc), F32)

        def conv_grads(i, carry):
            r0 = pl.multiple_of(i * ROWS, ROWS)
            ca = _conv3(us_ref, 0, r0, ws[0], ba)
            cb = _conv3(us_ref, 1, r0, ws[1], bb)
            sg = jax.nn.sigmoid(ca)
            dav = da_ref[pl.ds(r0, ROWS), :].astype(F32)
            dcs_ref[0, pl.ds(r0, ROWS), :] = dav * cb * sg * (1.0 + ca * (1.0 - sg))
            dcs_ref[1, pl.ds(r0, ROWS), :] = dav * ca * sg
            return carry

        lax.fori_loop(0, S // ROWS, conv_grads, 0, unroll=4)

        def fold(v):
            return jnp.sum(v.reshape(ROWS // 8, 8, tc), axis=0)

        def input_grads(i, acc):
            r0 = pl.multiple_of(i * ROWS, ROWS)
            new = []
            for part in range(2):
                w = ws[part]
                dc = dcs_ref[part, pl.ds(r0, ROWS), :]
                dc1 = dcs_ref[part, pl.ds(r0 + 1, ROWS), :]
                dc2 = dcs_ref[part, pl.ds(r0 + 2, ROWS), :]
                u = u_ref[part, pl.ds(r0, ROWS), :]
                du_ref[part, pl.ds(r0, ROWS), :] = (dc * w[2:3] + dc1 * w[1:2] + dc2 * w[0:1]).astype(BF)
                sums = (fold(dc2 * u), fold(dc1 * u), fold(dc * u), fold(dc))
                new += [a + s for a, s in zip(acc[4 * part:4 * part + 4], sums)]
            return tuple(new)

        acc = lax.fori_loop(0, S // ROWS, input_grads, tuple(jnp.zeros((8, tc), F32) for _ in range(8)), unroll=4)
        for part in range(2):
            for j in range(3):
                dw_ref[part, j:j + 1, :] = jnp.sum(acc[4 * part + j], axis=0, keepdims=True)
            db_ref[part] = jnp.sum(acc[4 * part + 3], axis=0, keepdims=True)

    return pl.pallas_call(
        _hide(body, 5, len(after)), name="ffn_act_bwd", grid=(F // tc,),
        in_specs=_ffn_specs(S, F, tc, c) + [pl.BlockSpec((S, tc), lambda j: (0, j))] + _hidden_specs(after),
        out_specs=[pl.BlockSpec((2, S, tc), lambda j: (0, 0, j)), pl.BlockSpec((2, 3, tc), lambda j: (0, 0, j)),
                   pl.BlockSpec((2, 1, tc), lambda j: (0, 0, j))],
        out_shape=[jax.ShapeDtypeStruct((2, S, F), BF), jax.ShapeDtypeStruct((2, 3, F), F32), jax.ShapeDtypeStruct((2, 1, F), F32)],
        scratch_shapes=[pltpu.VMEM((2, S + PAD, tc), F32), pltpu.VMEM((2, S + PAD, tc), F32)],
        compiler_params=_params(("parallel",)),
    )(up3, cws, cws, cb3, dact, *after)


def _loss(y, target):
    S, D = y.shape
    tr = _pick(S, (256,))

    def body(y_ref, t_ref, dy_ref, dyb_ref, l_ref):
        i = pl.program_id(0)
        e = y_ref[...] - t_ref[...]
        dy = e * (1.0 / D)
        dy_ref[...] = dy
        dyb_ref[...] = dy.astype(BF)
        part = jnp.sum(jnp.sum(e * e, axis=-1, keepdims=True), axis=0, keepdims=True) * (0.5 / D)

        @pl.when(i == 0)
        def _():
            l_ref[...] = jnp.zeros_like(l_ref)

        l_ref[...] += part

    row = pl.BlockSpec((tr, D), lambda i: (i, 0))
    return pl.pallas_call(
        body, name="loss", grid=(S // tr,), in_specs=[row, row],
        out_specs=[row, row, pl.BlockSpec((8, 128), lambda i: (0, 0))],
        out_shape=[jax.ShapeDtypeStruct((S, D), F32), jax.ShapeDtypeStruct((S, D), BF), jax.ShapeDtypeStruct((8, 128), F32)],
        compiler_params=_params(("arbitrary",)),
    )(y, target)


ANY = pl.BlockSpec(memory_space=pl.ANY)


def _allgather(shards, name):
    n = len(shards)

    def body(*refs):
        ins, outs = refs[:n], refs[n:2 * n]
        send_sems, recv_sems, local_sems = refs[2 * n:]
        x, y, c = lax.axis_index("x"), lax.axis_index("y"), lax.axis_index("c")
        me, sibling = (x, y, c), (x, y, 1 - c)
        chips = [(1 - x, y), (x, 1 - y), (1 - x, 1 - y)]

        def blk(w, px, py, pc):
            return outs[w].at[4 * px + 2 * py + pc]

        def copy(w, k, block, to, src=None):
            return pltpu.make_async_remote_copy(
                src_ref=blk(w, *block) if src is None else src, dst_ref=blk(w, *block),
                send_sem=send_sems.at[w, k], recv_sem=recv_sems.at[w, k], device_id=to, device_id_type=MESH)

        started = []
        mine = []
        for w in range(n):
            mine.append(pltpu.make_async_copy(ins[w], blk(w, *me), local_sems.at[w]))
            mine[-1].start()
            first = [copy(w, 0, me, sibling, src=ins[w])]
            first += [copy(w, 1 + j, me, (*chip, c), src=ins[w]) for j, chip in enumerate(chips)]
            for cp in first:
                cp.start()
            started += first
        for w in range(n):
            for j, chip in enumerate(chips):
                copy(w, 1 + j, (*chip, c), me).wait_recv()
                fwd = copy(w, 4 + j, (*chip, c), sibling)
                fwd.start()
                started.append(fwd)
        for w in range(n):
            copy(w, 0, sibling, me).wait_recv()
            for j, chip in enumerate(chips):
                copy(w, 4 + j, (*chip, 1 - c), me).wait_recv()
        for cp in started:
            cp.wait_send()
        for cp in mine:
            cp.wait()

    whole = pl.BlockSpec(memory_space=pltpu.VMEM)
    outs = pl.pallas_call(
        body, name=name, in_specs=[whole] * n, out_specs=[whole] * n,
        out_shape=[jax.ShapeDtypeStruct((N_DEV,) + s.shape, s.dtype) for s in shards],
        scratch_shapes=[pltpu.SemaphoreType.DMA((n, 7)), pltpu.SemaphoreType.DMA((n, 7)), pltpu.SemaphoreType.DMA((n,))],
    )(*shards)
    return list(outs)


def _allgather_seq(shards, name, collective_id, after=()):
    n = len(shards)
    n_after = len(after)

    halves = [s.shape[0] % 32 == 0 for s in shards]
    n_sem = 8
    to_diagonal = not all(halves)

    def body(*refs):
        ins, outs = refs[:n], refs[n + n_after:2 * n + n_after]
        send_sems, recv_sems, local_sems = refs[2 * n + n_after:]
        x, y, c = lax.axis_index("x"), lax.axis_index("y"), lax.axis_index("c")
        me, sibling = (x, y, c), (x, y, 1 - c)
        x_nb, y_nb, diag = (1 - x, y, c), (x, 1 - y, c), (1 - x, 1 - y, c)
        peers = [sibling, x_nb, y_nb] + ([diag] if to_diagonal else [])
        barrier = pltpu.get_barrier_semaphore()
        for peer in peers:
            pl.semaphore_signal(barrier, inc=1, device_id=peer, device_id_type=MESH)
        pl.semaphore_wait(barrier, len(peers))

        def blk(w, dev, rows=None):
            ref = outs[w].at[4 * dev[0] + 2 * dev[1] + dev[2]]
            return ref if rows is None else ref.at[rows]

        def copy(w, k, block, to, src=None, rows=None):
            return pltpu.make_async_remote_copy(
                src_ref=blk(w, block, rows) if src is None else src, dst_ref=blk(w, block, rows),
                send_sem=send_sems.at[n_sem * w + k], recv_sem=recv_sems.at[n_sem * w + k], device_id=to, device_id_type=MESH)

        def top(w):
            return pl.ds(0, shards[w].shape[0] // 2)

        def bottom(w):
            return pl.ds(shards[w].shape[0] // 2, shards[w].shape[0] // 2)

        started = []
        mine = []
        for w in range(n):
            mine.append(pltpu.make_async_copy(ins[w], blk(w, me), local_sems.at[w]))
            mine[-1].start()
            first = [copy(w, 0, me, sibling, src=ins[w]), copy(w, 1, me, x_nb, src=ins[w]), copy(w, 2, me, y_nb, src=ins[w])]
            if not halves[w]:
                first.append(copy(w, 3, me, diag, src=ins[w]))
            for cp in first:
                cp.start()
            started += first
        for w in range(n):
            copy(w, 1, x_nb, me).wait_recv()
            onward = [copy(w, 5, x_nb, sibling)] + ([copy(w, 3, x_nb, y_nb, rows=top(w))] if halves[w] else [])
            copy(w, 2, y_nb, me).wait_recv()
            onward += [copy(w, 6, y_nb, sibling)] + ([copy(w, 4, y_nb, x_nb, rows=bottom(w))] if halves[w] else [])
            for cp in onward:
                cp.start()
            started += onward
        for w in range(n):
            if halves[w]:
                copy(w, 3, diag, me, rows=top(w)).wait_recv()
                copy(w, 4, diag, me, rows=bottom(w)).wait_recv()
            else:
                copy(w, 3, diag, me).wait_recv()
            fwd = copy(w, 7, diag, sibling)
            fwd.start()
            started.append(fwd)
        for w in range(n):
            for k, dev in ((0, sibling), (5, (1 - x, y, 1 - c)), (6, (x, 1 - y, 1 - c)), (7, (1 - x, 1 - y, 1 - c))):
                copy(w, k, dev, me).wait_recv()
        for cp in started:
            cp.wait_send()
        for cp in mine:
            cp.wait()

    outs = pl.kernel(
        body, name=name, out_type=[jax.ShapeDtypeStruct((N_DEV,) + s.shape, s.dtype) for s in shards],
        mesh=plsc.ScalarSubcoreMesh(axis_name="seq", num_cores=1),
        scratch_types=[pltpu.SemaphoreType.DMA((n_sem * n,)), pltpu.SemaphoreType.DMA((n_sem * n,)), pltpu.SemaphoreType.DMA((n,))],
        compiler_params=pltpu.CompilerParams(collective_id=collective_id),
    )(*shards, *after)
    return list(outs)


def _chip_exchange(sums, name, collective_id):
    n = len(sums)

    def body(*refs):
        ins, outs = refs[:n], refs[n:2 * n]
        send_sems, recv_sems = refs[2 * n:]
        x, y, c = lax.axis_index("x"), lax.axis_index("y"), lax.axis_index("c")
        chips = [(1 - x, y), (x, 1 - y), (1 - x, 1 - y)]
        barrier = pltpu.get_barrier_semaphore()
        for px, py in chips:
            pl.semaphore_signal(barrier, inc=1, device_id=(px, py, c), device_id_type=MESH)
        pl.semaphore_wait(barrier, 3)
        copies = []
        for w in range(n):
            for k, (px, py) in enumerate(chips):
                copies.append(pltpu.make_async_remote_copy(
                    src_ref=ins[w].at[2 * px + py], dst_ref=outs[w].at[k], send_sem=send_sems.at[3 * w + k],
                    recv_sem=recv_sems.at[3 * w + k], device_id=(px, py, c), device_id_type=MESH))
        for cp in copies:
            cp.start()
        for cp in copies:
            cp.wait()

    outs = pl.kernel(
        body, name=name, out_type=[jax.ShapeDtypeStruct((3,) + s.shape[1:], s.dtype) for s in sums],
        mesh=plsc.ScalarSubcoreMesh(axis_name="seq", num_cores=1),
        scratch_types=[pltpu.SemaphoreType.DMA((3 * n,)), pltpu.SemaphoreType.DMA((3 * n,))],
        compiler_params=pltpu.CompilerParams(collective_id=collective_id),
    )(*sums)
    return list(outs)


def _row_tile(r, c, elems=256 * 1024):
    want = max(8, elems // c)
    for t in range(min(want, r) // 8 * 8, 0, -8):
        if r % t == 0:
            return t
    return r


def _pair_add(g4, recv, core, name, after=()):
    _, _, r, c = g4.shape
    tr = _row_tile(r, c, 1024 * 1024)

    def body(core_ref, a_ref, b_ref, o_ref):
        o_ref[...] = (a_ref[...].astype(F32) + b_ref[...].astype(F32)).astype(BF)

    return pl.pallas_call(
        _hide(body, 3, len(after)), name=name,
        grid_spec=pltpu.PrefetchScalarGridSpec(
            num_scalar_prefetch=1, grid=(4, r // tr),
            in_specs=[pl.BlockSpec((None, None, tr, c), lambda p, i, s: (p, s[0], i, 0)),
                      pl.BlockSpec((None, tr, c), lambda p, i, s: (p, i, 0))] + _hidden_specs(after),
            out_specs=pl.BlockSpec((None, tr, c), lambda p, i, s: (p, i, 0))),
        out_shape=jax.ShapeDtypeStruct((4, r, c), BF), compiler_params=_params(("parallel", "parallel")),
    )(core, g4, recv, *after)


def _adam_math(w, g, m, v):
    m = ADAM_B1 * m + (1.0 - ADAM_B1) * g
    v = ADAM_B2 * v + (1.0 - ADAM_B2) * (g * g)
    m_hat = m / (1.0 - ADAM_B1 ** ADAM_STEP)
    v_hat = v / (1.0 - ADAM_B2 ** ADAM_STEP)
    delta = -ADAM_LR * (m_hat / (jnp.sqrt(v_hat) + ADAM_EPS) + ADAM_WD * w)
    return delta, m, v


def _adamw_big(sums, recv, chip, w, m, v, name, after=()):
    r, c = w.shape
    tr = _row_tile(r, c, 512 * 1024)

    def body(chip_ref, s_ref, r_ref, w_ref, m_ref, v_ref, g_out, d_out, m_out, v_out):
        g = s_ref[...].astype(F32) + r_ref[0].astype(F32)
        g = g + r_ref[1].astype(F32)
        g = g + r_ref[2].astype(F32)
        delta, mn, vn = _adam_math(w_ref[...], g, m_ref[...], v_ref[...])
        g_out[...] = g
        d_out[...] = delta
        m_out[...] = mn
        v_out[...] = vn

    row = pl.BlockSpec((tr, c), lambda i, s: (i, 0))
    return pl.pallas_call(
        _hide(body, 6, len(after)), name=name,
        grid_spec=pltpu.PrefetchScalarGridSpec(
            num_scalar_prefetch=1, grid=(r // tr,),
            in_specs=[pl.BlockSpec((None, tr, c), lambda i, s: (s[0], i, 0)), pl.BlockSpec((3, tr, c), lambda i, s: (0, i, 0)),
                      row, row, row] + _hidden_specs(after),
            out_specs=[row, row, row, row]),
        out_shape=[jax.ShapeDtypeStruct((r, c), F32)] * 4, compiler_params=_params(("parallel",)),
    )(chip, sums, recv, w, m, v, *after)


def _adamw_small(parts, ws, ms, vs, extra_parts, name):
    n, ne = len(ws), len(extra_parts)

    def total(p_ref):
        g = p_ref[0]
        for d in range(1, N_DEV):
            g = g + p_ref[d]
        return g

    def body(*refs):
        p_refs, w_refs, m_refs, v_refs = refs[:n], refs[n:2 * n], refs[2 * n:3 * n], refs[3 * n:4 * n]
        e_refs = refs[4 * n:4 * n + ne]
        outs = refs[4 * n + ne:]
        for i in range(n):
            g = total(p_refs[i])
            delta, mn, vn = _adam_math(w_refs[i][...], g, m_refs[i][...], v_refs[i][...])
            outs[4 * i][...] = g
            outs[4 * i + 1][...] = delta
            outs[4 * i + 2][...] = mn
            outs[4 * i + 3][...] = vn
        for i in range(ne):
            outs[4 * n + i][...] = total(e_refs[i])

    out_shape = []
    for w in ws:
        out_shape += [jax.ShapeDtypeStruct(w.shape, F32)] * 4
    out_shape += [jax.ShapeDtypeStruct(e.shape[1:], F32) for e in extra_parts]
    res = pl.pallas_call(body, name=name, out_shape=out_shape,
                         compiler_params=pltpu.CompilerParams(vmem_limit_bytes=VMEM_LIMIT))(*parts, *ws, *ms, *vs, *extra_parts)
    return [res[4 * i:4 * i + 4] for i in range(n)], list(res[4 * n:])


def _adamw_plain(g, w, m, v, name):
    def body(g_ref, w_ref, m_ref, v_ref, d_out, m_out, v_out):
        delta, mn, vn = _adam_math(w_ref[...], g_ref[...], m_ref[...], v_ref[...])
        d_out[...] = delta
        m_out[...] = mn
        v_out[...] = vn

    return pl.pallas_call(body, name=name, out_shape=[jax.ShapeDtypeStruct(w.shape, F32)] * 3)(g, w, m, v)


def kernel(x, mem, positions, g_mix, w_in, g_a_v, w_spatial, b_spatial, g_b_q, g_b_k, sinks, g_mem, w_mem_kv, g_c_q, g_c_k, w_branch_a, w_branch_b, w_branch_c, w_out, g_ffn, w_up, conv_w, conv_b, w_down, loss_target, m_g_mix, m_w_in, m_g_a_v, m_w_spatial, m_b_spatial, m_g_b_q, m_g_b_k, m_sinks, m_g_mem, m_w_mem_kv, m_g_c_q, m_g_c_k, m_w_branch_a, m_w_branch_b, m_w_branch_c, m_w_out, m_g_ffn, m_w_up, m_conv_w, m_conv_b, m_w_down, v_g_mix, v_w_in, v_g_a_v, v_w_spatial, v_b_spatial, v_g_b_q, v_g_b_k, v_sinks, v_g_mem, v_w_mem_kv, v_g_c_q, v_g_c_k, v_w_branch_a, v_w_branch_b, v_w_branch_c, v_w_out, v_g_ffn, v_w_up, v_conv_w, v_conv_b, v_w_down):
    S, D = x.shape[1], x.shape[2]
    M = mem.shape[1]
    F = w_down.shape[1] * N_DEV
    in_cols = w_in.shape[2] * N_DEV
    ax, ay, ac = lax.axis_index("x"), lax.axis_index("y"), lax.axis_index("c")
    core = jnp.reshape(ac, (1,)).astype(jnp.int32)
    chip = jnp.reshape(2 * ax + ay, (1,)).astype(jnp.int32)
    me = 4 * ax + 2 * ay + ac

    x2, mem2, tgt2 = x[0], mem[0], loss_target[0]

    big = dict(w_in=w_in[0].T, w_mem_kv=w_mem_kv[0], w_branch_a=w_branch_a[0], w_branch_b=w_branch_b[0],
               w_branch_c=w_branch_c[0], w_out=w_out[0], w_up=w_up[0], w_down=w_down[0])
    names = list(big)
    cast = {k: big[k].astype(BF) for k in names}
    W = {}
    cb3 = conv_b.reshape(2, 1, F)
    W["w_in"], = _allgather_seq([cast["w_in"]], "ag_seq0", 0)
    w_in_t = W["w_in"].reshape(in_cols, D)
    grp1 = ["w_mem_kv", "w_branch_a", "w_branch_b", "w_branch_c", "w_out"]
    res1 = _allgather_seq([cast[k] for k in grp1] + [conv_w[0]], "ag_seq1", 1, after=(_token((w_in_t,), "tok_w_in"),))
    W.update(zip(grp1, res1))
    cw3 = res1[-1]
    w_kv_f = W["w_mem_kv"].reshape(D, 2 * C_WIDTH)
    w_out_f = W["w_out"].reshape(D, D)

    half = ROPE_DIM // 2
    inv = ROPE_THETA ** (-jnp.arange(half, dtype=F32) / half)
    ang = positions[0].astype(F32)[:, None] * inv
    cos, sin = jnp.cos(ang), jnp.sin(ang)
    one, zero = jnp.ones((S, B_HEAD_DIM - ROPE_DIM), F32), jnp.zeros((S, B_HEAD_DIM - ROPE_DIM), F32)
    z8 = jnp.zeros((S, half), F32)
    ct = jnp.tile(jnp.concatenate([cos, cos, one], axis=1), (1, 2))
    sa = jnp.tile(jnp.concatenate([-sin, z8, zero], axis=1), (1, 2))
    sb = jnp.tile(jnp.concatenate([z8, sin, zero], axis=1), (1, 2))
    gq2, gk2 = jnp.tile(g_b_q, (1, 2)), jnp.tile(g_b_k, (1, 2))
    b_t = b_spatial[0].T

    h, rstd1 = _rms_fwd(x2, g_mix, "rms1_fwd")
    proj = _mm(h, w_in_t, "nt", F32, "mm_proj", tn=1280)
    y_a = _a_fwd(proj, g_a_v, w_spatial[0], b_t)
    qn, kn = _b_pre(proj, gq2, gk2, ct, sa, sb)
    W["w_up"], = _allgather_seq([cast["w_up"]], "ag_seq2", 2, after=(_token((W["w_out"], qn), "tok_group1"),))
    y_b = _b_attn_fwd(qn, kn, proj, sinks)
    mem_h, rstd_m = _rms_fwd(mem2, g_mem, "rmsmem_fwd")
    kv = _mm(mem_h, w_kv_f, "nn", F32, "mm_kv", after=(y_b,))
    y_c = _c_fwd(proj, kv, g_c_q, g_c_k)
    z_a = _mm(y_a, W["w_branch_a"], "nn", BF, "mm_za", b_stack=True, after=(y_b,))
    z_b = _mm(y_b, W["w_branch_b"], "nn", BF, "mm_zb", b_stack=True)
    z_c = _mm(y_c, W["w_branch_c"], "nn", BF, "mm_zc", b_stack=True)
    merged = _merge_fwd(proj, z_a, z_b, z_c)
    x1 = _mm(merged, w_out_f, "nn", F32, "mm_x1", resid=x2)
    h2, rstd2 = _rms_fwd(x1, g_ffn, "rms2_fwd")
    W["w_down"], = _allgather_seq([cast["w_down"]], "ag_seq3", 3, after=(W["w_up"], h2))
    w_down_f = W["w_down"].reshape(F, D)
    up3 = _mm(h2, W["w_up"], "nn", F32, "mm_up", b_stack=True, out_parts=2)
    act = _ffn_act_fwd(up3, cw3, cb3)
    y = _mm(act, w_down_f, "nn", F32, "mm_y", resid=x1, tk=1408)
    dy, dy_b, loss_acc = _loss(y, tgt2)
    loss = lax.psum(loss_acc[0, 0], ("x", "y", "c"))

    reduced = {}

    def as4(g):
        return g.reshape(4, 2, g.shape[1], g.shape[2])

    def finish_group(gi, keys, g4, from_sibling):
        sums = [_pair_add(a, b, core, "rs_add_" + k) for k, a, b in zip(keys, g4, from_sibling)]
        from_chips = _chip_exchange(sums, f"rs_chip{gi}", 4 + gi)
        reduced.update(zip(keys, zip(sums, from_chips)))
        return tuple(sums)

    d_act = _mm(dy_b, w_down_f, "nt", BF, "mm_dact", tn=1408)
    g_down = _mm(act, dy_b, "tn", BF, "mm_gdown", tm=1408)
    d_up3, d_cw3, d_cb3 = _ffn_act_bwd(up3, cw3, cb3, d_act, after=(g_down,))
    grp0 = [as4(g_down.reshape(N_DEV, F // N_DEV, D))]
    g_up, sib0 = _mm(h2, d_up3, "tn", BF, "mm_gup", b_parts=2, out_stack=True, exchange=grp0)
    sums0 = finish_group(0, ["w_down"], grp0, sib0)
    grp1 = [as4(g_up)]
    d_h2, sib1 = _mm(d_up3, W["w_up"], "nt", F32, "mm_dh2", a_parts=2, b_stack=True, tm=2048, after=sums0, exchange=grp1)
    sums1 = finish_group(1, ["w_up"], grp1, sib1)
    dx1, dx1_b, d_g_ffn = _rms_bwd(x1, rstd2, g_ffn, d_h2, dy, "rms2_bwd", after=sums1)
    d_merged = _mm(dx1_b, w_out_f, "nt", F32, "mm_dmerged")
    g_out = _mm(merged, dx1_b, "tn", BF, "mm_gout")
    dz_a, dz_b, dz_c, dga, dgb, dgc = _merge_bwd(proj, z_a, z_b, z_c, d_merged, after=(g_out,))
    g_ba = _mm(y_a, dz_a, "tn", BF, "mm_gba", out_stack=True)
    g_bb = _mm(y_b, dz_b, "tn", BF, "mm_gbb", out_stack=True)
    g_bc = _mm(y_c, dz_c, "tn", BF, "mm_gbc", out_stack=True)
    grp2 = [as4(g_out.reshape(N_DEV, D // N_DEV, D)), as4(g_ba), as4(g_bb), as4(g_bc)]
    dy_a, sib2 = _mm(dz_a, W["w_branch_a"], "nt", F32, "mm_dya", b_stack=True, exchange=grp2)
    sums2 = finish_group(2, ["w_out", "w_branch_a", "w_branch_b", "w_branch_c"], grp2, sib2)
    dy_b_ = _mm(dz_b, W["w_branch_b"], "nt", F32, "mm_dyb", b_stack=True, after=sums2)
    dy_c = _mm(dz_c, W["w_branch_c"], "nt", F32, "mm_dyc", b_stack=True)
    d_uv, d_g_a_v, d_w_s, d_b_t = _a_bwd(proj, g_a_v, w_spatial[0], b_t, dy_a)
    dqn, dkn, dv_b, dsink_rows = _b_attn_bwd(qn, kn, proj, sinks, dy_b_)
    d_qkv, d_gq2, d_gk2 = _b_pre_bwd(proj, gq2, gk2, ct, sa, sb, dqn, dkn, dv_b)
    dq_c, dk_c, dv_c, d_gcq, d_gck = _c_bwd(proj, kv, g_c_q, g_c_k, dy_c)
    dkv_b = jnp.concatenate([dk_c, dv_c], axis=1).astype(BF)
    d_memh = _mm(dkv_b, w_kv_f, "nt", F32, "mm_dmemh")
    g_kv = _mm(mem_h, dkv_b, "tn", BF, "mm_gkv")
    _, _, d_g_mem = _rms_bwd(mem2, rstd_m, g_mem, d_memh, None, "rmsmem_bwd")
    dproj = jnp.concatenate([d_uv, d_qkv, dq_c, dga, dgb, dgc], axis=1)
    g_in = _mm(dproj, h, "tn", BF, "mm_gin", tm=1280)
    grp3 = [as4(g_in.reshape(N_DEV, in_cols // N_DEV, D)), as4(g_kv.reshape(N_DEV, D // N_DEV, 2 * C_WIDTH))]
    d_h, sib3 = _mm(dproj, w_in_t, "nn", F32, "mm_dh", tm=2048, tk=1280, exchange=grp3)
    sums3 = finish_group(3, ["w_in", "w_mem_kv"], grp3, sib3)
    grad_x, _, d_g_mix = _rms_bwd(x2, rstd1, g_mix, d_h, dx1, "rms1_bwd", after=sums3)

    small_names =["g_mix", "g_a_v", "w_spatial", "b_spatial", "g_b_q", "g_b_k", "sinks", "g_mem", "g_c_q", "g_c_k", "g_ffn", "conv_b"]
    small_w = dict(g_mix=g_mix, g_a_v=g_a_v, w_spatial=w_spatial, b_spatial=b_spatial, g_b_q=g_b_q, g_b_k=g_b_k, sinks=sinks,
                   g_mem=g_mem, g_c_q=g_c_q, g_c_k=g_c_k, g_ffn=g_ffn, conv_b=conv_b)
    small_m = dict(g_mix=m_g_mix, g_a_v=m_g_a_v, w_spatial=m_w_spatial, b_spatial=m_b_spatial, g_b_q=m_g_b_q, g_b_k=m_g_b_k,
                   sinks=m_sinks, g_mem=m_g_mem, g_c_q=m_g_c_q, g_c_k=m_g_c_k, g_ffn=m_g_ffn, conv_b=m_conv_b)
    small_v = dict(g_mix=v_g_mix, g_a_v=v_g_a_v, w_spatial=v_w_spatial, b_spatial=v_b_spatial, g_b_q=v_g_b_q, g_b_k=v_g_b_k,
                   sinks=v_sinks, g_mem=v_g_mem, g_c_q=v_g_c_q, g_c_k=v_g_c_k, g_ffn=v_g_ffn, conv_b=v_conv_b)
    small_g = dict(
        g_mix=d_g_mix, g_a_v=d_g_a_v, w_spatial=d_w_s, b_spatial=d_b_t.T,
        g_b_q=d_gq2.reshape(2, B_HEAD_DIM).sum(0), g_b_k=d_gk2.reshape(2, B_HEAD_DIM).sum(0),
        sinks=dsink_rows.sum(0)[:B_HEADS], g_mem=d_g_mem, g_c_q=d_gcq.sum(0), g_c_k=d_gck.sum(0), g_ffn=d_g_ffn,
        conv_b=d_cb3)
    partial = [small_g[k].reshape(small_w[k].shape) for k in small_names] + [d_cw3]
    parts = _allgather(partial, "ag_small")
    small_res, (g_cw3,) = _adamw_small(parts[:-1], [small_w[k] for k in small_names], [small_m[k] for k in small_names],
                                       [small_v[k] for k in small_names], parts[-1:], "adamw_small")
    small_out = dict(zip(small_names, small_res))
    c_cw = 2 * F // N_DEV
    g_cw = lax.dynamic_slice(g_cw3, (me // (N_DEV // 2), 0, (me % (N_DEV // 2)) * c_cw), (1, 3, c_cw))[0]
    cw_res = _adamw_plain(g_cw, conv_w[0], m_conv_w[0], v_conv_w[0], "adamw_conv_w")
    big_out = {"conv_w": [g_cw[None]] + [a[None] for a in cw_res]}

    moments = dict(w_in=(m_w_in, v_w_in), w_mem_kv=(m_w_mem_kv, v_w_mem_kv), w_branch_a=(m_w_branch_a, v_w_branch_a),
                   w_branch_b=(m_w_branch_b, v_w_branch_b), w_branch_c=(m_w_branch_c, v_w_branch_c), w_out=(m_w_out, v_w_out),
                   w_up=(m_w_up, v_w_up), w_down=(m_w_down, v_w_down))
    token = (grad_x, small_res[0][0])
    for k in ["w_down", "w_up", "w_out", "w_branch_a", "w_branch_b", "w_branch_c", "w_mem_kv", "w_in"]:
        s, r = reduced[k]
        mk, vk = moments[k][0][0], moments[k][1][0]
        if k == "w_in":
            res = _adamw_big(s, r, chip, big[k], mk.T, vk.T, "adamw_" + k, after=token)
            big_out[k] = [a.T[None] for a in res]
        else:
            res = _adamw_big(s, r, chip, big[k], mk, vk, "adamw_" + k, after=token)
            big_out[k] = [a[None] for a in res]
        token = (res[0],)

    order = ["g_mix", "w_in", "g_a_v", "w_spatial", "b_spatial", "g_b_q", "g_b_k", "sinks", "g_mem", "w_mem_kv", "g_c_q", "g_c_k",
             "w_branch_a", "w_branch_b", "w_branch_c", "w_out", "g_ffn", "w_up", "conv_w", "conv_b", "w_down"]
    res = {**small_out, **big_out}
    outs = [loss, grad_x[None]]
    for field in range(4):
        outs += [res[k][field] for k in order]
    return tuple(outs)
```

```python
import functools

import jax
import jax.numpy as jnp
from jax import lax
from jax.experimental import pallas as pl
from jax.experimental.pallas import tpu as pltpu
from jax.experimental.pallas import tpu_sc as plsc

F32 = jnp.float32
BF = jnp.bfloat16
EPS = 1e-6
NEG = -1e30

N_DEV = 8
CHUNK = 128
A_GROUPS = 4
A_WIDTH = 512
B_HEADS = 16
B_KV_HEADS = 2
B_HEAD_DIM = 64
B_WIDTH = 1024
B_KV_WIDTH = 128
ROPE_DIM = 16
ROPE_THETA = 500000.0
C_HEADS = 4
C_HEAD_DIM = 128
C_WIDTH = 512
GATE_OFF = 2 * A_WIDTH + B_WIDTH + 2 * B_KV_WIDTH + C_WIDTH

ADAM_LR = 0.001
ADAM_B1 = 0.9
ADAM_B2 = 0.999
ADAM_EPS = 1e-08
ADAM_WD = 0.01
ADAM_STEP = 10

VMEM_LIMIT = 48 * 1024 * 1024
MESH = pl.DeviceIdType.MESH


def _pick(n, prefs):
    for p in prefs:
        if p <= n and n % p == 0:
            return p
    return n


def _params(sem):
    return pltpu.CompilerParams(dimension_semantics=sem, vmem_limit_bytes=VMEM_LIMIT)


def _hide(body, n_seen, n_hidden):
    if not n_hidden:
        return body

    def wrapped(*refs):
        return body(*refs[:n_seen], *refs[n_seen + n_hidden:])

    return wrapped


def _hidden_specs(after):
    return [pl.BlockSpec(memory_space=pl.ANY) for _ in after]


def _token(xs, name):
    def body(*refs):
        refs[-1][...] = jnp.zeros_like(refs[-1])

    return pl.pallas_call(body, name=name, in_specs=_hidden_specs(xs), out_shape=jax.ShapeDtypeStruct((8, 128), F32))(*xs)


def _mm(a, b, mode, out_dtype, name, *, resid=None, b_stack=False, a_parts=0, b_parts=0, out_parts=0,
        out_stack=False, tm=1024, tn=1024, tk=2048, after=(), exchange=()):
    if mode == "nn":
        M = a.shape[-2]
        K = a.shape[-1] * max(a_parts, 1)
        N = b.shape[-1] * (N_DEV if b_stack else 1)
        dims = (((1,), (0,)), ((), ()))
    elif mode == "nt":
        M = a.shape[-2]
        K = a.shape[-1] * max(a_parts, 1)
        N = b.shape[-2]
        dims = (((1,), (1,)), ((), ()))
    else:
        K = a.shape[-2]
        M = a.shape[-1]
        N = b.shape[-1] * max(b_parts, 1)
        dims = (((0,), (0,)), ((), ()))
    if b_stack and mode == "nn":
        tn = b.shape[-1]
    if b_stack and mode == "nt":
        tk = b.shape[-1]
    if out_stack:
        tn = N // N_DEV
    tm, tn, tk = _pick(M, (tm,)), _pick(N, (tn,)), _pick(K, (tk,))
    if M % tm or N % tn or K % tk:
        raise ValueError(f"{name}: tiles {tm},{tn},{tk} do not divide {M},{N},{K}")
    nm, nn, nk = M // tm, N // tn, K // tk

    def parts_idx(t, ntile, parts):
        per = ntile // parts
        return t // per, t % per

    if mode in ("nn", "nt"):
        if a_parts:
            a_spec = pl.BlockSpec((None, tm, tk), lambda m, n, k: (parts_idx(k, nk, a_parts)[0], m, parts_idx(k, nk, a_parts)[1]))
        else:
            a_spec = pl.BlockSpec((tm, tk), lambda m, n, k: (m, k))
    else:
        a_spec = pl.BlockSpec((tk, tm), lambda m, n, k: (k, m))
    if mode == "nn":
        if b_stack:
            b_spec = pl.BlockSpec((None, tk, tn), lambda m, n, k: (n, k, 0))
        else:
            b_spec = pl.BlockSpec((tk, tn), lambda m, n, k: (k, n))
    elif mode == "nt":
        if b_stack:
            b_spec = pl.BlockSpec((None, tn, tk), lambda m, n, k: (k, n, 0))
        else:
            b_spec = pl.BlockSpec((tn, tk), lambda m, n, k: (n, k))
    else:
        if b_parts:
            b_spec = pl.BlockSpec((None, tk, tn), lambda m, n, k: (parts_idx(n, nn, b_parts)[0], k, parts_idx(n, nn, b_parts)[1]))
        else:
            b_spec = pl.BlockSpec((tk, tn), lambda m, n, k: (k, n))
    if out_stack:
        out_shape = jax.ShapeDtypeStruct((N_DEV, M, tn), out_dtype)
        o_spec = pl.BlockSpec((None, tm, tn), lambda m, n, k: (n, m, 0))
    elif out_parts:
        out_shape = jax.ShapeDtypeStruct((out_parts, M, N // out_parts), out_dtype)
        o_spec = pl.BlockSpec((None, tm, tn), lambda m, n, k: (parts_idx(n, nn, out_parts)[0], m, parts_idx(n, nn, out_parts)[1]))
    else:
        out_shape = jax.ShapeDtypeStruct((M, N), out_dtype)
        o_spec = pl.BlockSpec((tm, tn), lambda m, n, k: (m, n))
    has_resid = resid is not None

    n_ex = len(exchange)
    n_in = 2 + has_resid + len(after)

    def body(*refs):
        a_ref, b_ref = refs[:2]
        r_ref = refs[2] if has_resid else None
        ex_in = refs[n_in:n_in + n_ex]
        o_ref = refs[n_in + n_ex]
        ex_out = refs[n_in + n_ex + 1:n_in + 2 * n_ex + 1]
        scratch = refs[n_in + 2 * n_ex + 1:]
        m_i, n_i, k = pl.program_id(0), pl.program_id(1), pl.program_id(2)

        def pushes():
            send_sems, recv_sems = scratch[-2:]
            x, y, c = lax.axis_index("x"), lax.axis_index("y"), lax.axis_index("c")
            return [pltpu.make_async_remote_copy(
                src_ref=ex_in[w].at[:, 1 - c], dst_ref=ex_out[w], send_sem=send_sems.at[w], recv_sem=recv_sems.at[w],
                device_id=(x, y, 1 - c), device_id_type=MESH) for w in range(n_ex)]

        if n_ex:
            @pl.when((m_i == 0) & (n_i == 0) & (k == 0))
            def _():
                for cp in pushes():
                    cp.start()

        if nk == 1:
            res = lax.dot_general(a_ref[...], b_ref[...], dims, preferred_element_type=F32)
            if has_resid:
                res = res + r_ref[...]
            o_ref[...] = res.astype(o_ref.dtype)
        else:
            acc = scratch[0]

            @pl.when(k == 0)
            def _():
                acc[...] = jnp.zeros_like(acc)

            acc[...] += lax.dot_general(a_ref[...], b_ref[...], dims, preferred_element_type=F32)

            @pl.when(k == nk - 1)
            def _():
                res = acc[...]
                if has_resid:
                    res = res + r_ref[...]
                o_ref[...] = res.astype(o_ref.dtype)

        if n_ex:
            @pl.when((m_i == nm - 1) & (n_i == nn - 1) & (k == nk - 1))
            def _():
                for cp in pushes():
                    cp.wait()

    in_specs = [a_spec, b_spec]
    args = [a, b]
    if has_resid:
        in_specs.append(pl.BlockSpec((tm, tn), lambda m, n, k: (m, n)))
        args.append(resid)
    in_specs += _hidden_specs(after) + _hidden_specs(exchange)
    args += list(after) + list(exchange)
    scratch_shapes = [pltpu.VMEM((tm, tn), F32)] if nk > 1 else []
    if not n_ex:
        return pl.pallas_call(
            body, name=name, grid=(nm, nn, nk), in_specs=in_specs, out_specs=o_spec, out_shape=out_shape,
            scratch_shapes=scratch_shapes, compiler_params=_params(("parallel", "parallel", "arbitrary")),
        )(*args)
    res = pl.pallas_call(
        body, name=name, grid=(nm, nn, nk), in_specs=in_specs, out_specs=[o_spec] + _hidden_specs(exchange),
        out_shape=[out_shape] + [jax.ShapeDtypeStruct((g.shape[0],) + g.shape[2:], g.dtype) for g in exchange],
        scratch_shapes=scratch_shapes + [pltpu.SemaphoreType.DMA((n_ex,)), pltpu.SemaphoreType.DMA((n_ex,))],
        compiler_params=_params(("arbitrary", "arbitrary", "arbitrary")),
    )(*args)
    return res[0], list(res[1:])


def _rms_fwd(x, g, name):
    R, D = x.shape
    tr = _pick(R, (256,))

    def body(x_ref, g_ref, h_ref, r_ref):
        xv = x_ref[...]
        r = lax.rsqrt(jnp.mean(xv * xv, axis=-1, keepdims=True) + EPS)
        h_ref[...] = (xv * r * g_ref[...]).astype(BF)
        r_ref[...] = r

    return pl.pallas_call(
        body, name=name, grid=(R // tr,),
        in_specs=[pl.BlockSpec((tr, D), lambda i: (i, 0)), pl.BlockSpec((1, D), lambda i: (0, 0))],
        out_specs=[pl.BlockSpec((tr, D), lambda i: (i, 0)), pl.BlockSpec((tr, 1), lambda i: (i, 0))],
        out_shape=[jax.ShapeDtypeStruct((R, D), BF), jax.ShapeDtypeStruct((R, 1), F32)],
        compiler_params=_params(("parallel",)),
    )(x, g)


def _rms_bwd(x, r, g, dh, dres, name, after=()):
    R, D = x.shape
    tr = _pick(R, (256,))
    has_res = dres is not None

    def body(*refs):
        if has_res:
            x_ref, r_ref, g_ref, dh_ref, dres_ref, dx_ref, dxb_ref, dg_ref = refs
        else:
            x_ref, r_ref, g_ref, dh_ref, dx_ref, dxb_ref, dg_ref = refs
        i = pl.program_id(0)
        xv, rv, dhv = x_ref[...], r_ref[...], dh_ref[...]
        gy = dhv * g_ref[...]
        c = jnp.sum(xv * gy, axis=-1, keepdims=True)
        dx = rv * gy - xv * (rv * rv * rv) * (c * (1.0 / D))
        if has_res:
            dx = dx + dres_ref[...]
        dx_ref[...] = dx
        dxb_ref[...] = dx.astype(BF)
        part = jnp.sum(dhv * xv * rv, axis=0, keepdims=True)

        @pl.when(i == 0)
        def _():
            dg_ref[...] = part

        @pl.when(i > 0)
        def _():
            dg_ref[...] += part

    row = pl.BlockSpec((tr, D), lambda i: (i, 0))
    in_specs = [row, pl.BlockSpec((tr, 1), lambda i: (i, 0)), pl.BlockSpec((1, D), lambda i: (0, 0)), row]
    args = [x, r, g, dh]
    if has_res:
        in_specs.append(row)
        args.append(dres)
    return pl.pallas_call(
        _hide(body, len(args), len(after)), name=name, grid=(R // tr,), in_specs=in_specs + _hidden_specs(after),
        out_specs=[row, row, pl.BlockSpec((1, D), lambda i: (0, 0))],
        out_shape=[jax.ShapeDtypeStruct((R, D), F32), jax.ShapeDtypeStruct((R, D), BF), jax.ShapeDtypeStruct((1, D), F32)],
        compiler_params=_params(("arbitrary",)),
    )(*args, *after)


def _a_chunk(us, vs, gvs, ws, bs):
    r_i = lax.broadcasted_iota(jnp.int32, (CHUNK, CHUNK), 0)
    c_i = lax.broadcasted_iota(jnp.int32, (CHUNK, CHUNK), 1)
    causal = r_i >= c_i
    vg = [jax.nn.gelu(v) for v in vs]
    ss = sum(jnp.sum(v * v, axis=-1, keepdims=True) for v in vg)
    r = lax.rsqrt(ss * (1.0 / A_WIDTH) + EPS)
    ys = []
    for g in range(A_GROUPS):
        vn = vg[g] * r * gvs[g]
        w = jnp.where(causal, ws[g], 0.0)
        s = jnp.dot(w.astype(BF), vn.astype(BF), preferred_element_type=F32) + bs[g]
        ys.append(jax.nn.gelu(us[g]) * s)
    return ys


def _a_split(u_ref, v_ref, g_ref, w_ref, b_ref):
    sl = [slice(g * 128, (g + 1) * 128) for g in range(A_GROUPS)]
    return ([u_ref[:, s] for s in sl], [v_ref[:, s] for s in sl], [g_ref[:, s] for s in sl],
            [w_ref[g] for g in range(A_GROUPS)], [b_ref[:, g:g + 1] for g in range(A_GROUPS)])


def _a_specs(S):
    return [pl.BlockSpec((CHUNK, A_WIDTH), lambda n: (n, 0)), pl.BlockSpec((CHUNK, A_WIDTH), lambda n: (n, 1)),
            pl.BlockSpec((1, A_WIDTH), lambda n: (0, 0)), pl.BlockSpec((A_GROUPS, CHUNK, CHUNK), lambda n: (0, 0, 0)),
            pl.BlockSpec((CHUNK, A_GROUPS), lambda n: (0, 0))]


def _a_fwd(proj, g_v, w_s, b_t):
    S = proj.shape[0]

    def body(u_ref, v_ref, g_ref, w_ref, b_ref, y_ref):
        ys = _a_chunk(*_a_split(u_ref, v_ref, g_ref, w_ref, b_ref))
        for g in range(A_GROUPS):
            y_ref[:, g * 128:(g + 1) * 128] = ys[g].astype(BF)

    return pl.pallas_call(
        body, name="a_fwd", grid=(S // CHUNK,), in_specs=_a_specs(S),
        out_specs=pl.BlockSpec((CHUNK, A_WIDTH), lambda n: (n, 0)),
        out_shape=jax.ShapeDtypeStruct((S, A_WIDTH), BF), compiler_params=_params(("parallel",)),
    )(proj, proj, g_v, w_s, b_t)


def _a_bwd(proj, g_v, w_s, b_t, dy, after=()):
    S = proj.shape[0]

    def body(u_ref, v_ref, g_ref, w_ref, b_ref, dy_ref, duv_ref, dg_ref, dw_ref, db_ref):
        n = pl.program_id(0)
        dys = [dy_ref[:, g * 128:(g + 1) * 128] for g in range(A_GROUPS)]
        _, vjp = jax.vjp(_a_chunk, *_a_split(u_ref, v_ref, g_ref, w_ref, b_ref))
        dus, dvs, dgs, dws, dbs = vjp(dys)

        @pl.when(n == 0)
        def _():
            dg_ref[...] = jnp.zeros_like(dg_ref)
            dw_ref[...] = jnp.zeros_like(dw_ref)
            db_ref[...] = jnp.zeros_like(db_ref)

        for g in range(A_GROUPS):
            duv_ref[:, g * 128:(g + 1) * 128] = dus[g].astype(BF)
            duv_ref[:, A_WIDTH + g * 128:A_WIDTH + (g + 1) * 128] = dvs[g].astype(BF)
            dg_ref[:, g * 128:(g + 1) * 128] += dgs[g]
            dw_ref[g] += dws[g]
            db_ref[:, g:g + 1] += dbs[g]

    return pl.pallas_call(
        _hide(body, 6, len(after)), name="a_bwd", grid=(S // CHUNK,),
        in_specs=_a_specs(S) + [pl.BlockSpec((CHUNK, A_WIDTH), lambda n: (n, 0))] + _hidden_specs(after),
        out_specs=[pl.BlockSpec((CHUNK, 2 * A_WIDTH), lambda n: (n, 0)), pl.BlockSpec((1, A_WIDTH), lambda n: (0, 0)),
                   pl.BlockSpec((A_GROUPS, CHUNK, CHUNK), lambda n: (0, 0, 0)), pl.BlockSpec((CHUNK, A_GROUPS), lambda n: (0, 0))],
        out_shape=[jax.ShapeDtypeStruct((S, 2 * A_WIDTH), BF), jax.ShapeDtypeStruct((1, A_WIDTH), F32),
                   jax.ShapeDtypeStruct((A_GROUPS, CHUNK, CHUNK), F32), jax.ShapeDtypeStruct((CHUNK, A_GROUPS), F32)],
        compiler_params=_params(("arbitrary",)),
    )(proj, proj, g_v, w_s, b_t, dy, *after)


def _half_mask(shape, which):
    lane = lax.broadcasted_iota(jnp.int32, shape, len(shape) - 1)
    return (lane >= 64) == (which == 1)


def _pair_norm_rope(x, g, ct, sa, sb):
    lo = _half_mask(x.shape, 0)
    x2 = x * x
    ss_lo = jnp.sum(jnp.where(lo, x2, 0.0), axis=-1, keepdims=True)
    ss_hi = jnp.sum(jnp.where(lo, 0.0, x2), axis=-1, keepdims=True)
    r = jnp.where(lo, lax.rsqrt(ss_lo * (1.0 / B_HEAD_DIM) + EPS), lax.rsqrt(ss_hi * (1.0 / B_HEAD_DIM) + EPS))
    xr = x * r
    xn = xr * g
    out = xn * ct + pltpu.roll(xn, 120, 1) * sa + pltpu.roll(xn, 8, 1) * sb
    return out, xr, r


def _pair_norm_rope_bwd(x, g, ct, sa, sb, dout):
    lo = _half_mask(x.shape, 0)
    _, xr, r = _pair_norm_rope(x, g, ct, sa, sb)
    dxn = dout * ct + pltpu.roll(dout * sa, 8, 1) + pltpu.roll(dout * sb, 120, 1)
    gy = dxn * g
    t = xr * gy
    c_lo = jnp.sum(jnp.where(lo, t, 0.0), axis=-1, keepdims=True)
    c_hi = jnp.sum(jnp.where(lo, 0.0, t), axis=-1, keepdims=True)
    c = jnp.where(lo, c_lo, c_hi)
    dx = r * (gy - xr * c * (1.0 / B_HEAD_DIM))
    dg = jnp.sum(dxn * xr, axis=0, keepdims=True)
    return dx, dg


def _b_pre(proj, gq2, gk2, ct, sa, sb):
    S = proj.shape[0]
    tr = _pick(S, (256,))
    n_pair = B_WIDTH // 128

    def body(q_ref, k_ref, gq_ref, gk_ref, ct_ref, sa_ref, sb_ref, qn_ref, kn_ref):
        ct_v, sa_v, sb_v = ct_ref[...], sa_ref[...], sb_ref[...]
        for p in range(n_pair):
            o, _, _ = _pair_norm_rope(q_ref[:, p * 128:(p + 1) * 128], gq_ref[...], ct_v, sa_v, sb_v)
            qn_ref[:, p * 128:(p + 1) * 128] = o.astype(BF)
        o, _, _ = _pair_norm_rope(k_ref[...], gk_ref[...], ct_v, sa_v, sb_v)
        kn_ref[...] = o.astype(BF)

    tab = pl.BlockSpec((tr, 128), lambda i: (i, 0))
    gsp = pl.BlockSpec((1, 128), lambda i: (0, 0))
    return pl.pallas_call(
        body, name="b_pre", grid=(S // tr,),
        in_specs=[pl.BlockSpec((tr, B_WIDTH), lambda i: (i, 1)), pl.BlockSpec((tr, 128), lambda i: (i, 2 * B_WIDTH // 128)),
                  gsp, gsp, tab, tab, tab],
        out_specs=[pl.BlockSpec((tr, B_WIDTH), lambda i: (i, 0)), tab],
        out_shape=[jax.ShapeDtypeStruct((S, B_WIDTH), BF), jax.ShapeDtypeStruct((S, 128), BF)],
        compiler_params=_params(("parallel",)),
    )(proj, proj, gq2, gk2, ct, sa, sb)


def _b_pre_bwd(proj, gq2, gk2, ct, sa, sb, dqn, dkn, dv):
    S = proj.shape[0]
    tr = _pick(S, (256,))
    n_pair = B_WIDTH // 128

    def body(q_ref, k_ref, gq_ref, gk_ref, ct_ref, sa_ref, sb_ref, dqn_ref, dkn_ref, dv_ref, dqkv_ref, dgq_ref, dgk_ref):
        i = pl.program_id(0)
        ct_v, sa_v, sb_v = ct_ref[...], sa_ref[...], sb_ref[...]
        dgq = jnp.zeros((1, 128), F32)
        for p in range(n_pair):
            sl = slice(p * 128, (p + 1) * 128)
            dx, dg = _pair_norm_rope_bwd(q_ref[:, sl], gq_ref[...], ct_v, sa_v, sb_v, dqn_ref[:, sl])
            dqkv_ref[:, sl] = dx.astype(BF)
            dgq = dgq + dg
        dx, dgk = _pair_norm_rope_bwd(k_ref[...], gk_ref[...], ct_v, sa_v, sb_v, dkn_ref[...])
        dqkv_ref[:, B_WIDTH:B_WIDTH + 128] = dx.astype(BF)
        dqkv_ref[:, B_WIDTH + 128:B_WIDTH + 256] = dv_ref[...].astype(BF)

        @pl.when(i == 0)
        def _():
            dgq_ref[...] = dgq
            dgk_ref[...] = dgk

        @pl.when(i > 0)
        def _():
            dgq_ref[...] += dgq
            dgk_ref[...] += dgk

    tab = pl.BlockSpec((tr, 128), lambda i: (i, 0))
    gsp = pl.BlockSpec((1, 128), lambda i: (0, 0))
    return pl.pallas_call(
        body, name="b_pre_bwd", grid=(S // tr,),
        in_specs=[pl.BlockSpec((tr, B_WIDTH), lambda i: (i, 1)), pl.BlockSpec((tr, 128), lambda i: (i, 2 * B_WIDTH // 128)),
                  gsp, gsp, tab, tab, tab, pl.BlockSpec((tr, B_WIDTH), lambda i: (i, 0)), tab, tab],
        out_specs=[pl.BlockSpec((tr, B_WIDTH + 256), lambda i: (i, 0)), gsp, gsp],
        out_shape=[jax.ShapeDtypeStruct((S, B_WIDTH + 256), BF), jax.ShapeDtypeStruct((1, 128), F32), jax.ShapeDtypeStruct((1, 128), F32)],
        compiler_params=_params(("arbitrary",)),
    )(proj, proj, gq2, gk2, ct, sa, sb, dqn, dkn, dv)


def _b_dup(x2, g):
    d = jnp.where(_half_mask(x2.shape, g), x2, 0.0)
    return (d + pltpu.roll(d, 64, 1)).astype(BF)


PAIRS_PER_GROUP = B_HEADS // B_KV_HEADS // 2
GROUP_ROWS = PAIRS_PER_GROUP * CHUNK


def _b_valid(n):
    row = lax.broadcasted_iota(jnp.int32, (GROUP_ROWS, 2 * CHUNK), 0) & (CHUNK - 1)
    col = lax.broadcasted_iota(jnp.int32, (GROUP_ROWS, 2 * CHUNK), 1)
    rel = row + CHUNK - col
    return (rel >= 0) & (rel < CHUNK) & ((col >= CHUNK) | (n > 0))


def _b_blocks(x2, g):
    xd = _b_dup(x2, g)
    lo = _half_mask(xd.shape, 0)
    zero = jnp.zeros_like(xd)
    return jnp.concatenate([jnp.where(lo, xd, zero), jnp.where(lo, zero, xd)], axis=0)


def _b_sink_col(s_ref, g, hf):
    rb = lax.broadcasted_iota(jnp.int32, (GROUP_ROWS, 1), 0) // CHUNK
    col = jnp.zeros((GROUP_ROWS, 1), F32)
    for pp in range(PAIRS_PER_GROUP):
        col = jnp.where(rb == pp, s_ref[0, 2 * (g * PAIRS_PER_GROUP + pp) + hf], col)
    return col


def _b_probs(qs, kblk, valid, sinks):
    s = lax.dot_general(qs, kblk, (((1,), (1,)), ((), ())), preferred_element_type=F32) * (B_HEAD_DIM ** -0.5)
    out = []
    for hf in range(2):
        sh = jnp.where(valid, s[:, hf * 2 * CHUNK:(hf + 1) * 2 * CHUNK], NEG)
        m = jnp.maximum(jnp.max(sh, axis=-1, keepdims=True), sinks[hf])
        e = jnp.exp(sh - m)
        es = jnp.exp(sinks[hf] - m)
        inv = 1.0 / (jnp.sum(e, axis=-1, keepdims=True) + es)
        out.append((e * inv, es * inv))
    return out


def _b_fold(acc, g):
    lo = _half_mask((2 * CHUNK, 128), 0)
    t = jnp.where(lo, acc[:2 * CHUNK], 0.0) + jnp.where(lo, 0.0, acc[2 * CHUNK:])
    return jnp.where(_half_mask((2 * CHUNK, 128), g), t + pltpu.roll(t, 64, 1), 0.0)


def _b_kv_specs(S):
    prev = lambda n: (jnp.maximum(n - 1, 0), 0)
    cur = lambda n: (n, 0)
    v_col = (2 * B_WIDTH + B_KV_WIDTH) // 128
    return [pl.BlockSpec((CHUNK, 128), prev), pl.BlockSpec((CHUNK, 128), cur),
            pl.BlockSpec((CHUNK, 128), lambda n: (jnp.maximum(n - 1, 0), v_col)), pl.BlockSpec((CHUNK, 128), lambda n: (n, v_col))]


def _b_attn_fwd(qn, kn, proj, sinks):
    S = qn.shape[0]

    def body(s_ref, q_ref, kp_ref, kc_ref, vp_ref, vc_ref, y_ref):
        n = pl.program_id(0)
        valid = _b_valid(n)
        k2 = jnp.concatenate([kp_ref[...], kc_ref[...]], axis=0).astype(F32)
        v2 = jnp.concatenate([vp_ref[...], vc_ref[...]], axis=0)
        for g in range(B_KV_HEADS):
            pairs = [g * PAIRS_PER_GROUP + pp for pp in range(PAIRS_PER_GROUP)]
            qs = jnp.concatenate([q_ref[:, p * 128:(p + 1) * 128] for p in pairs], axis=0)
            probs = _b_probs(qs, _b_blocks(k2, g), valid, [_b_sink_col(s_ref, g, hf) for hf in range(2)])
            pcat = jnp.concatenate([probs[0][0].astype(BF), probs[1][0].astype(BF)], axis=1)
            o = jnp.dot(pcat, _b_blocks(v2, g), preferred_element_type=F32)
            for pp, p in enumerate(pairs):
                y_ref[:, p * 128:(p + 1) * 128] = o[pp * CHUNK:(pp + 1) * CHUNK].astype(BF)

    return pl.pallas_call(
        body, name="b_attn_fwd", grid=(S // CHUNK,),
        in_specs=[pl.BlockSpec(memory_space=pltpu.SMEM), pl.BlockSpec((CHUNK, B_WIDTH), lambda n: (n, 0))] + _b_kv_specs(S),
        out_specs=pl.BlockSpec((CHUNK, B_WIDTH), lambda n: (n, 0)),
        out_shape=jax.ShapeDtypeStruct((S, B_WIDTH), BF), compiler_params=_params(("arbitrary",)),
    )(sinks, qn, kn, kn, proj, proj)


def _b_attn_bwd(qn, kn, proj, sinks, dy, after=()):
    S = qn.shape[0]

    def body(s_ref, q_ref, kp_ref, kc_ref, vp_ref, vc_ref, dy_ref, dq_ref, dk_ref, dv_ref, ds_ref):
        n = pl.program_id(0)

        @pl.when(n == 0)
        def _():
            dk_ref[...] = jnp.zeros_like(dk_ref)
            dv_ref[...] = jnp.zeros_like(dv_ref)
            ds_ref[...] = jnp.zeros_like(ds_ref)

        valid = _b_valid(n)
        k2 = jnp.concatenate([kp_ref[...], kc_ref[...]], axis=0).astype(F32)
        v2 = jnp.concatenate([vp_ref[...], vc_ref[...]], axis=0)
        lane = lax.broadcasted_iota(jnp.int32, (CHUNK, 128), 1)
        dk2 = jnp.zeros((2 * CHUNK, 128), F32)
        dv2 = jnp.zeros((2 * CHUNK, 128), F32)
        dsink = jnp.zeros((CHUNK, 128), F32)
        scale = B_HEAD_DIM ** -0.5
        nt = (((1,), (1,)), ((), ()))
        tn = (((0,), (0,)), ((), ()))
        for g in range(B_KV_HEADS):
            pairs = [g * PAIRS_PER_GROUP + pp for pp in range(PAIRS_PER_GROUP)]
            qs = jnp.concatenate([q_ref[:, p * 128:(p + 1) * 128] for p in pairs], axis=0)
            do = jnp.concatenate([dy_ref[:, p * 128:(p + 1) * 128] for p in pairs], axis=0)
            do_b = do.astype(BF)
            kblk, vblk = _b_blocks(k2, g), _b_blocks(v2, g)
            probs = _b_probs(qs, kblk, valid, [_b_sink_col(s_ref, g, hf) for hf in range(2)])
            pcat = jnp.concatenate([probs[0][0].astype(BF), probs[1][0].astype(BF)], axis=1)
            o = jnp.dot(pcat, vblk, preferred_element_type=F32)
            dp = lax.dot_general(do_b, vblk, nt, preferred_element_type=F32)
            prod = do * o
            ds_halves = []
            for hf in range(2):
                pr, ps = probs[hf]
                delta = jnp.sum(jnp.where(_half_mask(prod.shape, hf), prod, 0.0), axis=-1, keepdims=True)
                ds_halves.append((pr * (dp[:, hf * 2 * CHUNK:(hf + 1) * 2 * CHUNK] - delta) * scale).astype(BF))
                t = -ps * delta
                for pp, p in enumerate(pairs):
                    dsink = dsink + jnp.where(lane == 2 * p + hf, t[pp * CHUNK:(pp + 1) * CHUNK], 0.0)
            dsc = jnp.concatenate(ds_halves, axis=1)
            dq = jnp.dot(dsc, kblk, preferred_element_type=F32)
            for pp, p in enumerate(pairs):
                dq_ref[:, p * 128:(p + 1) * 128] = dq[pp * CHUNK:(pp + 1) * CHUNK]
            dk2 = dk2 + _b_fold(lax.dot_general(dsc, qs, tn, preferred_element_type=F32), g)
            dv2 = dv2 + _b_fold(lax.dot_general(pcat, do_b, tn, preferred_element_type=F32), g)
        ds_ref[...] += dsink
        cur = pl.ds(pl.multiple_of(n * CHUNK, CHUNK), CHUNK)
        dk_ref[cur, :] += dk2[CHUNK:]
        dv_ref[cur, :] += dv2[CHUNK:]

        @pl.when(n > 0)
        def _():
            prv = pl.ds(pl.multiple_of((n - 1) * CHUNK, CHUNK), CHUNK)
            dk_ref[prv, :] += dk2[:CHUNK]
            dv_ref[prv, :] += dv2[:CHUNK]

    full = pl.BlockSpec((S, 128), lambda n: (0, 0))
    return pl.pallas_call(
        _hide(body, 7, len(after)), name="b_attn_bwd", grid=(S // CHUNK,),
        in_specs=[pl.BlockSpec(memory_space=pltpu.SMEM), pl.BlockSpec((CHUNK, B_WIDTH), lambda n: (n, 0))] + _b_kv_specs(S)
        + [pl.BlockSpec((CHUNK, B_WIDTH), lambda n: (n, 0))] + _hidden_specs(after),
        out_specs=[pl.BlockSpec((CHUNK, B_WIDTH), lambda n: (n, 0)), full, full, pl.BlockSpec((CHUNK, 128), lambda n: (0, 0))],
        out_shape=[jax.ShapeDtypeStruct((S, B_WIDTH), F32), jax.ShapeDtypeStruct((S, 128), F32), jax.ShapeDtypeStruct((S, 128), F32),
                   jax.ShapeDtypeStruct((CHUNK, 128), F32)],
        compiler_params=_params(("arbitrary",)),
    )(sinks, qn, kn, kn, proj, proj, dy, *after)


def _c_block(q, k, v, gq, gk):
    qn = q * lax.rsqrt(jnp.mean(q * q, axis=-1, keepdims=True) + EPS) * gq
    kn = k * lax.rsqrt(jnp.mean(k * k, axis=-1, keepdims=True) + EPS) * gk
    s = lax.dot_general(qn.astype(BF), kn.astype(BF), (((1,), (1,)), ((), ())), preferred_element_type=F32) * (C_HEAD_DIM ** -0.5)
    p = jax.nn.softmax(s, axis=-1)
    return jnp.dot(p.astype(BF), v.astype(BF), preferred_element_type=F32)


def _c_specs(S, M, tq):
    q_col = (2 * A_WIDTH + B_WIDTH + 2 * B_KV_WIDTH) // 128
    return [pl.BlockSpec((tq, 128), lambda h, i: (i, q_col + h)), pl.BlockSpec((M, 128), lambda h, i: (0, h)),
            pl.BlockSpec((M, 128), lambda h, i: (0, C_HEADS + h)), pl.BlockSpec((1, 128), lambda h, i: (0, 0)),
            pl.BlockSpec((1, 128), lambda h, i: (0, 0))]


def _c_fwd(proj, kv, gq, gk):
    S, M = proj.shape[0], kv.shape[0]
    tq = _pick(S, (512,))

    def body(q_ref, k_ref, v_ref, gq_ref, gk_ref, y_ref):
        y_ref[...] = _c_block(q_ref[...], k_ref[...], v_ref[...], gq_ref[...], gk_ref[...]).astype(BF)

    return pl.pallas_call(
        body, name="c_fwd", grid=(C_HEADS, S // tq), in_specs=_c_specs(S, M, tq),
        out_specs=pl.BlockSpec((tq, 128), lambda h, i: (i, h)),
        out_shape=jax.ShapeDtypeStruct((S, C_WIDTH), BF), compiler_params=_params(("parallel", "parallel")),
    )(proj, kv, kv, gq, gk)


def _c_bwd(proj, kv, gq, gk, dy):
    S, M = proj.shape[0], kv.shape[0]
    tq = _pick(S, (512,))

    def body(q_ref, k_ref, v_ref, gq_ref, gk_ref, dy_ref, dq_ref, dk_ref, dv_ref, dgq_ref, dgk_ref):
        i = pl.program_id(1)
        _, vjp = jax.vjp(_c_block, q_ref[...], k_ref[...], v_ref[...], gq_ref[...], gk_ref[...])
        dq, dk, dv, dgq, dgk = vjp(dy_ref[...])
        dq_ref[...] = dq.astype(BF)

        @pl.when(i == 0)
        def _():
            dk_ref[...] = dk
            dv_ref[...] = dv
            dgq_ref[...] = dgq
            dgk_ref[...] = dgk

        @pl.when(i > 0)
        def _():
            dk_ref[...] += dk
            dv_ref[...] += dv
            dgq_ref[...] += dgq
            dgk_ref[...] += dgk

    return pl.pallas_call(
        body, name="c_bwd", grid=(C_HEADS, S // tq),
        in_specs=_c_specs(S, M, tq) + [pl.BlockSpec((tq, 128), lambda h, i: (i, h))],
        out_specs=[pl.BlockSpec((tq, 128), lambda h, i: (i, h)), pl.BlockSpec((M, 128), lambda h, i: (0, h)),
                   pl.BlockSpec((M, 128), lambda h, i: (0, h)), pl.BlockSpec((None, 1, 128), lambda h, i: (h, 0, 0)),
                   pl.BlockSpec((None, 1, 128), lambda h, i: (h, 0, 0))],
        out_shape=[jax.ShapeDtypeStruct((S, C_WIDTH), BF), jax.ShapeDtypeStruct((M, C_WIDTH), F32), jax.ShapeDtypeStruct((M, C_WIDTH), F32),
                   jax.ShapeDtypeStruct((C_HEADS, 1, 128), F32), jax.ShapeDtypeStruct((C_HEADS, 1, 128), F32)],
        compiler_params=_params(("parallel", "arbitrary")),
    )(proj, kv, kv, gq, gk, dy)


def _merge_specs(S, D, tm, tn, ks):
    off = GATE_OFF // tn
    nd = D // tn
    gates = [pl.BlockSpec((tm, tn), functools.partial(lambda b, m, n: (m, off + b * nd + n), b)) for b in range(3)]
    ys = [pl.BlockSpec((tm, k), lambda m, n: (m, 0)) for k in ks]
    ws = [pl.BlockSpec((None, k, tn), lambda m, n: (n, 0, 0)) for k in ks]
    return gates, ys, ws


def _merge_fwd(proj, ys, ws):
    S = proj.shape[0]
    tn = ws[0].shape[2]
    D = N_DEV * tn
    ks = [w.shape[1] for w in ws]
    tm = _pick(S, (1024,))
    gates, y_specs, w_specs = _merge_specs(S, D, tm, tn, ks)

    def body(ga_ref, gb_ref, gc_ref, ya_ref, yb_ref, yc_ref, wa_ref, wb_ref, wc_ref, m_ref, za_ref, zb_ref, zc_ref):
        acc = None
        for g_ref, y_ref, w_ref, z_ref in ((ga_ref, ya_ref, wa_ref, za_ref), (gb_ref, yb_ref, wb_ref, zb_ref),
                                           (gc_ref, yc_ref, wc_ref, zc_ref)):
            z = jnp.dot(y_ref[...], w_ref[...], preferred_element_type=F32)
            z_ref[...] = z.astype(BF)
            t = jax.nn.sigmoid(g_ref[...]) * z
            acc = t if acc is None else acc + t
        m_ref[...] = acc.astype(BF)

    tile = pl.BlockSpec((tm, tn), lambda m, n: (m, n))
    return pl.pallas_call(
        body, name="merge_fwd", grid=(S // tm, D // tn), in_specs=gates + y_specs + w_specs,
        out_specs=[tile, tile, tile, tile], out_shape=[jax.ShapeDtypeStruct((S, D), BF)] * 4,
        compiler_params=_params(("parallel", "parallel")),
    )(proj, proj, proj, *ys, *ws)


def _merge_bwd(proj, zs, dm, ws, after=()):
    S = proj.shape[0]
    tn = ws[0].shape[2]
    D = N_DEV * tn
    ks = [w.shape[1] for w in ws]
    tm = _pick(S, (512,))
    gates, _, w_specs = _merge_specs(S, D, tm, tn, ks)
    nt = (((1,), (1,)), ((), ()))

    def body(ga_ref, gb_ref, gc_ref, za_ref, zb_ref, zc_ref, dm_ref, wa_ref, wb_ref, wc_ref,
             dza_ref, dzb_ref, dzc_ref, dga_ref, dgb_ref, dgc_ref, dya_ref, dyb_ref, dyc_ref):
        n = pl.program_id(1)
        dmv = dm_ref[...]
        for g_ref, z_ref, w_ref, dz_ref, dg_ref, dy_ref in (
                (ga_ref, za_ref, wa_ref, dza_ref, dga_ref, dya_ref), (gb_ref, zb_ref, wb_ref, dzb_ref, dgb_ref, dyb_ref),
                (gc_ref, zc_ref, wc_ref, dzc_ref, dgc_ref, dyc_ref)):
            sg = jax.nn.sigmoid(g_ref[...])
            dz = (sg * dmv).astype(BF)
            dz_ref[...] = dz
            dg_ref[...] = (dmv * z_ref[...].astype(F32) * sg * (1.0 - sg)).astype(BF)
            part = lax.dot_general(dz, w_ref[...], nt, preferred_element_type=F32)

            @pl.when(n == 0)
            def _():
                dy_ref[...] = part

            @pl.when(n > 0)
            def _():
                dy_ref[...] += part

    tile = pl.BlockSpec((tm, tn), lambda m, n: (m, n))
    dys = [pl.BlockSpec((tm, k), lambda m, n: (m, 0)) for k in ks]
    return pl.pallas_call(
        _hide(body, 10, len(after)), name="merge_bwd", grid=(S // tm, D // tn),
        in_specs=gates + [tile, tile, tile, tile] + w_specs + _hidden_specs(after),
        out_specs=[tile] * 6 + dys,
        out_shape=[jax.ShapeDtypeStruct((S, D), BF)] * 6 + [jax.ShapeDtypeStruct((S, k), F32) for k in ks],
        compiler_params=_params(("parallel", "arbitrary")),
    )(proj, proj, proj, *zs, dm, *ws, *after)


PAD = 8


def _stage_shift_down(us_ref, u_ref):
    S = u_ref.shape[1]
    us_ref[:, 0:PAD, :] = jnp.zeros((2, PAD, us_ref.shape[2]), F32)
    us_ref[:, PAD:S + PAD, :] = u_ref[...].astype(F32)


ROWS = 32


def _conv3(us_ref, part, r0, w, b):
    return (us_ref[part, pl.ds(r0 + PAD, ROWS), :] * w[2:3] + us_ref[part, pl.ds(r0 + PAD - 1, ROWS), :] * w[1:2]
            + us_ref[part, pl.ds(r0 + PAD - 2, ROWS), :] * w[0:1] + b)


def _ffn_specs(S, F, tc, c):
    per = c // tc

    def w_spec(half):
        return pl.BlockSpec((None, 3, tc), lambda j: (half * (N_DEV // 2) + j // per, 0, j % per))

    return [pl.BlockSpec((2, S, tc), lambda j: (0, 0, j)), w_spec(0), w_spec(1), pl.BlockSpec((2, 1, tc), lambda j: (0, 0, j))]


def _ffn_tile(F, c):
    tc = 128
    if c % tc or F % tc:
        raise ValueError(f"ffn tile {tc} does not divide {c}, {F}")
    return tc


def _ffn_act_fwd(up3, cws, cb3):
    _, S, F = up3.shape
    c = cws.shape[2]
    tc = _ffn_tile(F, c)

    def body(u_ref, wa_ref, wb_ref, b_ref, o_ref, us_ref):
        _stage_shift_down(us_ref, u_ref)
        wa, wb, ba, bb = wa_ref[...], wb_ref[...], b_ref[0], b_ref[1]

        def step(i, carry):
            r0 = pl.multiple_of(i * ROWS, ROWS)
            ca = _conv3(us_ref, 0, r0, wa, ba)
            cb = _conv3(us_ref, 1, r0, wb, bb)
            o_ref[pl.ds(r0, ROWS), :] = (ca * jax.nn.sigmoid(ca) * cb).astype(BF)
            return carry

        lax.fori_loop(0, S // ROWS, step, 0, unroll=4)

    return pl.pallas_call(
        body, name="ffn_act_fwd", grid=(F // tc,), in_specs=_ffn_specs(S, F, tc, c),
        out_specs=pl.BlockSpec((S, tc), lambda j: (0, j)), out_shape=jax.ShapeDtypeStruct((S, F), BF),
        scratch_shapes=[pltpu.VMEM((2, S + PAD, tc), F32)],
        compiler_params=_params(("parallel",)),
    )(up3, cws, cws, cb3)


def _ffn_act_bwd(up3, cws, cb3, dact, after=()):
    _, S, F = up3.shape
    c = cws.shape[2]
    tc = _ffn_tile(F, c)

    def body(u_ref, wa_ref, wb_ref, b_ref, da_ref, du_ref, dw_ref, db_ref, us_ref, dcs_ref):
        _stage_shift_down(us_ref, u_ref)
        ws = (wa_ref[...], wb_ref[...])
        ba, bb = b_ref[0], b_ref[1]
        dcs_ref[:, S:S + PAD, :] = jnp.zeros((2, PAD, tc), F32)

        def conv_grads(i, carry):
            r0 = pl.multiple_of(i * ROWS, ROWS)
            ca = _conv3(us_ref, 0, r0, ws[0], ba)
            cb = _conv3(us_ref, 1, r0, ws[1], bb)
            sg = jax.nn.sigmoid(ca)
            dav = da_ref[pl.ds(r0, ROWS), :].astype(F32)
            dcs_ref[0, pl.ds(r0, ROWS), :] = dav * cb * sg * (1.0 + ca * (1.0 - sg))
            dcs_ref[1, pl.ds(r0, ROWS), :] = dav * ca * sg
            return carry

        lax.fori_loop(0, S // ROWS, conv_grads, 0, unroll=4)

        def fold(v):
            return jnp.sum(v.reshape(ROWS // 8, 8, tc), axis=0)

        def input_grads(i, acc):
            r0 = pl.multiple_of(i * ROWS, ROWS)
            new = []
            for part in range(2):
                w = ws[part]
                dc = dcs_ref[part, pl.ds(r0, ROWS), :]
                dc1 = dcs_ref[part, pl.ds(r0 + 1, ROWS), :]
                dc2 = dcs_ref[part, pl.ds(r0 + 2, ROWS), :]
                u = us_ref[part, pl.ds(r0 + PAD, ROWS), :]
                du_ref[part, pl.ds(r0, ROWS), :] = (dc * w[2:3] + dc1 * w[1:2] + dc2 * w[0:1]).astype(BF)
                sums = (fold(dc2 * u), fold(dc1 * u), fold(dc * u), fold(dc))
                new += [a + s for a, s in zip(acc[4 * part:4 * part + 4], sums)]
            return tuple(new)

        acc = lax.fori_loop(0, S // ROWS, input_grads, tuple(jnp.zeros((8, tc), F32) for _ in range(8)), unroll=4)
        for part in range(2):
            for j in range(3):
                dw_ref[part, j:j + 1, :] = jnp.sum(acc[4 * part + j], axis=0, keepdims=True)
            db_ref[part] = jnp.sum(acc[4 * part + 3], axis=0, keepdims=True)

    return pl.pallas_call(
        _hide(body, 5, len(after)), name="ffn_act_bwd", grid=(F // tc,),
        in_specs=_ffn_specs(S, F, tc, c) + [pl.BlockSpec((S, tc), lambda j: (0, j))] + _hidden_specs(after),
        out_specs=[pl.BlockSpec((2, S, tc), lambda j: (0, 0, j)), pl.BlockSpec((2, 3, tc), lambda j: (0, 0, j)),
                   pl.BlockSpec((2, 1, tc), lambda j: (0, 0, j))],
        out_shape=[jax.ShapeDtypeStruct((2, S, F), BF), jax.ShapeDtypeStruct((2, 3, F), F32), jax.ShapeDtypeStruct((2, 1, F), F32)],
        scratch_shapes=[pltpu.VMEM((2, S + PAD, tc), F32), pltpu.VMEM((2, S + PAD, tc), F32)],
        compiler_params=_params(("parallel",)),
    )(up3, cws, cws, cb3, dact, *after)


def _loss(y, target):
    S, D = y.shape
    tr = _pick(S, (256,))

    def body(y_ref, t_ref, dy_ref, dyb_ref, l_ref):
        i = pl.program_id(0)
        e = y_ref[...] - t_ref[...]
        dy = e * (1.0 / D)
        dy_ref[...] = dy
        dyb_ref[...] = dy.astype(BF)
        part = jnp.sum(jnp.sum(e * e, axis=-1, keepdims=True), axis=0, keepdims=True) * (0.5 / D)

        @pl.when(i == 0)
        def _():
            l_ref[...] = jnp.zeros_like(l_ref)

        l_ref[...] += part

    row = pl.BlockSpec((tr, D), lambda i: (i, 0))
    return pl.pallas_call(
        body, name="loss", grid=(S // tr,), in_specs=[row, row],
        out_specs=[row, row, pl.BlockSpec((8, 128), lambda i: (0, 0))],
        out_shape=[jax.ShapeDtypeStruct((S, D), F32), jax.ShapeDtypeStruct((S, D), BF), jax.ShapeDtypeStruct((8, 128), F32)],
        compiler_params=_params(("arbitrary",)),
    )(y, target)


ANY = pl.BlockSpec(memory_space=pl.ANY)


def _allgather(shards, name):
    n = len(shards)

    def body(*refs):
        ins, outs = refs[:n], refs[n:2 * n]
        send_sems, recv_sems, local_sems = refs[2 * n:]
        x, y, c = lax.axis_index("x"), lax.axis_index("y"), lax.axis_index("c")
        me, sibling = (x, y, c), (x, y, 1 - c)
        chips = [(1 - x, y), (x, 1 - y), (1 - x, 1 - y)]

        def blk(w, px, py, pc):
            return outs[w].at[4 * px + 2 * py + pc]

        def copy(w, k, block, to, src=None):
            return pltpu.make_async_remote_copy(
                src_ref=blk(w, *block) if src is None else src, dst_ref=blk(w, *block),
                send_sem=send_sems.at[w, k], recv_sem=recv_sems.at[w, k], device_id=to, device_id_type=MESH)

        started = []
        mine = []
        for w in range(n):
            mine.append(pltpu.make_async_copy(ins[w], blk(w, *me), local_sems.at[w]))
            mine[-1].start()
            first = [copy(w, 0, me, sibling, src=ins[w])]
            first += [copy(w, 1 + j, me, (*chip, c), src=ins[w]) for j, chip in enumerate(chips)]
            for cp in first:
                cp.start()
            started += first
        for w in range(n):
            for j, chip in enumerate(chips):
                copy(w, 1 + j, (*chip, c), me).wait_recv()
                fwd = copy(w, 4 + j, (*chip, c), sibling)
                fwd.start()
                started.append(fwd)
        for w in range(n):
            copy(w, 0, sibling, me).wait_recv()
            for j, chip in enumerate(chips):
                copy(w, 4 + j, (*chip, 1 - c), me).wait_recv()
        for cp in started:
            cp.wait_send()
        for cp in mine:
            cp.wait()

    whole = pl.BlockSpec(memory_space=pltpu.VMEM)
    outs = pl.pallas_call(
        body, name=name, in_specs=[whole] * n, out_specs=[whole] * n,
        out_shape=[jax.ShapeDtypeStruct((N_DEV,) + s.shape, s.dtype) for s in shards],
        scratch_shapes=[pltpu.SemaphoreType.DMA((n, 7)), pltpu.SemaphoreType.DMA((n, 7)), pltpu.SemaphoreType.DMA((n,))],
    )(*shards)
    return list(outs)


def _allgather_seq(shards, name, collective_id, after=()):
    n = len(shards)
    n_after = len(after)

    halves = [s.shape[0] % 32 == 0 for s in shards]
    n_sem = 8
    to_diagonal = not all(halves)

    def body(*refs):
        ins, outs = refs[:n], refs[n + n_after:2 * n + n_after]
        send_sems, recv_sems, local_sems = refs[2 * n + n_after:]
        x, y, c = lax.axis_index("x"), lax.axis_index("y"), lax.axis_index("c")
        me, sibling = (x, y, c), (x, y, 1 - c)
        x_nb, y_nb, diag = (1 - x, y, c), (x, 1 - y, c), (1 - x, 1 - y, c)
        peers = [sibling, x_nb, y_nb] + ([diag] if to_diagonal else [])
        barrier = pltpu.get_barrier_semaphore()
        for peer in peers:
            pl.semaphore_signal(barrier, inc=1, device_id=peer, device_id_type=MESH)
        pl.semaphore_wait(barrier, len(peers))

        def blk(w, dev, rows=None):
            ref = outs[w].at[4 * dev[0] + 2 * dev[1] + dev[2]]
            return ref if rows is None else ref.at[rows]

        def copy(w, k, block, to, src=None, rows=None):
            return pltpu.make_async_remote_copy(
                src_ref=blk(w, block, rows) if src is None else src, dst_ref=blk(w, block, rows),
                send_sem=send_sems.at[n_sem * w + k], recv_sem=recv_sems.at[n_sem * w + k], device_id=to, device_id_type=MESH)

        def top(w):
            return pl.ds(0, shards[w].shape[0] // 2)

        def bottom(w):
            return pl.ds(shards[w].shape[0] // 2, shards[w].shape[0] // 2)

        started = []
        mine = []
        for w in range(n):
            mine.append(pltpu.make_async_copy(ins[w], blk(w, me), local_sems.at[w]))
            mine[-1].start()
            first = [copy(w, 0, me, sibling, src=ins[w]), copy(w, 1, me, x_nb, src=ins[w]), copy(w, 2, me, y_nb, src=ins[w])]
            if not halves[w]:
                first.append(copy(w, 3, me, diag, src=ins[w]))
            for cp in first:
                cp.start()
            started += first
        for w in range(n):
            copy(w, 1, x_nb, me).wait_recv()
            onward = [copy(w, 5, x_nb, sibling)] + ([copy(w, 3, x_nb, y_nb, rows=top(w))] if halves[w] else [])
            copy(w, 2, y_nb, me).wait_recv()
            onward += [copy(w, 6, y_nb, sibling)] + ([copy(w, 4, y_nb, x_nb, rows=bottom(w))] if halves[w] else [])
            for cp in onward:
                cp.start()
            started += onward
        for w in range(n):
            if halves[w]:
                copy(w, 3, diag, me, rows=top(w)).wait_recv()
                copy(w, 4, diag, me, rows=bottom(w)).wait_recv()
            else:
                copy(w, 3, diag, me).wait_recv()
            fwd = copy(w, 7, diag, sibling)
            fwd.start()
            started.append(fwd)
        for w in range(n):
            for k, dev in ((0, sibling), (5, (1 - x, y, 1 - c)), (6, (x, 1 - y, 1 - c)), (7, (1 - x, 1 - y, 1 - c))):
                copy(w, k, dev, me).wait_recv()
        for cp in started:
            cp.wait_send()
        for cp in mine:
            cp.wait()

    outs = pl.kernel(
        body, name=name, out_type=[jax.ShapeDtypeStruct((N_DEV,) + s.shape, s.dtype) for s in shards],
        mesh=plsc.ScalarSubcoreMesh(axis_name="seq", num_cores=1),
        scratch_types=[pltpu.SemaphoreType.DMA((n_sem * n,)), pltpu.SemaphoreType.DMA((n_sem * n,)), pltpu.SemaphoreType.DMA((n,))],
        compiler_params=pltpu.CompilerParams(collective_id=collective_id),
    )(*shards, *after)
    return list(outs)


def _chip_exchange(sums, name, collective_id):
    n = len(sums)

    def body(*refs):
        ins, outs = refs[:n], refs[n:2 * n]
        send_sems, recv_sems = refs[2 * n:]
        x, y, c = lax.axis_index("x"), lax.axis_index("y"), lax.axis_index("c")
        chips = [(1 - x, y), (x, 1 - y), (1 - x, 1 - y)]
        barrier = pltpu.get_barrier_semaphore()
        for px, py in chips:
            pl.semaphore_signal(barrier, inc=1, device_id=(px, py, c), device_id_type=MESH)
        pl.semaphore_wait(barrier, 3)
        copies = []
        for w in range(n):
            for k, (px, py) in enumerate(chips):
                copies.append(pltpu.make_async_remote_copy(
                    src_ref=ins[w].at[2 * px + py], dst_ref=outs[w].at[k], send_sem=send_sems.at[3 * w + k],
                    recv_sem=recv_sems.at[3 * w + k], device_id=(px, py, c), device_id_type=MESH))
        for cp in copies:
            cp.start()
        for cp in copies:
            cp.wait()

    outs = pl.kernel(
        body, name=name, out_type=[jax.ShapeDtypeStruct((3,) + s.shape[1:], s.dtype) for s in sums],
        mesh=plsc.ScalarSubcoreMesh(axis_name="seq", num_cores=1),
        scratch_types=[pltpu.SemaphoreType.DMA((3 * n,)), pltpu.SemaphoreType.DMA((3 * n,))],
        compiler_params=pltpu.CompilerParams(collective_id=collective_id),
    )(*sums)
    return list(outs)


def _row_tile(r, c, elems=256 * 1024):
    want = max(8, elems // c)
    for t in range(min(want, r) // 8 * 8, 0, -8):
        if r % t == 0:
            return t
    return r


def _pair_add(g4, recv, core, name, after=()):
    _, _, r, c = g4.shape
    tr = _row_tile(r, c, 1024 * 1024)

    def body(core_ref, a_ref, b_ref, o_ref):
        o_ref[...] = (a_ref[...].astype(F32) + b_ref[...].astype(F32)).astype(BF)

    return pl.pallas_call(
        _hide(body, 3, len(after)), name=name,
        grid_spec=pltpu.PrefetchScalarGridSpec(
            num_scalar_prefetch=1, grid=(4, r // tr),
            in_specs=[pl.BlockSpec((None, None, tr, c), lambda p, i, s: (p, s[0], i, 0)),
                      pl.BlockSpec((None, tr, c), lambda p, i, s: (p, i, 0))] + _hidden_specs(after),
            out_specs=pl.BlockSpec((None, tr, c), lambda p, i, s: (p, i, 0))),
        out_shape=jax.ShapeDtypeStruct((4, r, c), BF), compiler_params=_params(("parallel", "parallel")),
    )(core, g4, recv, *after)


def _adam_math(w, g, m, v):
    m = ADAM_B1 * m + (1.0 - ADAM_B1) * g
    v = ADAM_B2 * v + (1.0 - ADAM_B2) * (g * g)
    m_hat = m / (1.0 - ADAM_B1 ** ADAM_STEP)
    v_hat = v / (1.0 - ADAM_B2 ** ADAM_STEP)
    delta = -ADAM_LR * (m_hat / (jnp.sqrt(v_hat) + ADAM_EPS) + ADAM_WD * w)
    return delta, m, v


def _adamw_big(sums, recv, chip, w, m, v, name, after=()):
    r, c = w.shape
    tr = _row_tile(r, c, 512 * 1024)

    def body(chip_ref, s_ref, r_ref, w_ref, m_ref, v_ref, g_out, d_out, m_out, v_out):
        g = s_ref[...].astype(F32) + r_ref[0].astype(F32)
        g = g + r_ref[1].astype(F32)
        g = g + r_ref[2].astype(F32)
        delta, mn, vn = _adam_math(w_ref[...], g, m_ref[...], v_ref[...])
        g_out[...] = g
        d_out[...] = delta
        m_out[...] = mn
        v_out[...] = vn

    row = pl.BlockSpec((tr, c), lambda i, s: (i, 0))
    return pl.pallas_call(
        _hide(body, 6, len(after)), name=name,
        grid_spec=pltpu.PrefetchScalarGridSpec(
            num_scalar_prefetch=1, grid=(r // tr,),
            in_specs=[pl.BlockSpec((None, tr, c), lambda i, s: (s[0], i, 0)), pl.BlockSpec((3, tr, c), lambda i, s: (0, i, 0)),
                      row, row, row] + _hidden_specs(after),
            out_specs=[row, row, row, row]),
        out_shape=[jax.ShapeDtypeStruct((r, c), F32)] * 4, compiler_params=_params(("parallel",)),
    )(chip, sums, recv, w, m, v, *after)


def _adamw_small(parts, ws, ms, vs, extra_parts, name):
    n, ne = len(ws), len(extra_parts)

    def total(p_ref):
        g = p_ref[0]
        for d in range(1, N_DEV):
            g = g + p_ref[d]
        return g

    def body(*refs):
        p_refs, w_refs, m_refs, v_refs = refs[:n], refs[n:2 * n], refs[2 * n:3 * n], refs[3 * n:4 * n]
        e_refs = refs[4 * n:4 * n + ne]
        outs = refs[4 * n + ne:]
        for i in range(n):
            g = total(p_refs[i])
            delta, mn, vn = _adam_math(w_refs[i][...], g, m_refs[i][...], v_refs[i][...])
            outs[4 * i][...] = g
            outs[4 * i + 1][...] = delta
            outs[4 * i + 2][...] = mn
            outs[4 * i + 3][...] = vn
        for i in range(ne):
            outs[4 * n + i][...] = total(e_refs[i])

    out_shape = []
    for w in ws:
        out_shape += [jax.ShapeDtypeStruct(w.shape, F32)] * 4
    out_shape += [jax.ShapeDtypeStruct(e.shape[1:], F32) for e in extra_parts]
    res = pl.pallas_call(body, name=name, out_shape=out_shape,
                         compiler_params=pltpu.CompilerParams(vmem_limit_bytes=VMEM_LIMIT))(*parts, *ws, *ms, *vs, *extra_parts)
    return [res[4 * i:4 * i + 4] for i in range(n)], list(res[4 * n:])


def _adamw_plain(g, w, m, v, name):
    def body(g_ref, w_ref, m_ref, v_ref, d_out, m_out, v_out):
        delta, mn, vn = _adam_math(w_ref[...], g_ref[...], m_ref[...], v_ref[...])
        d_out[...] = delta
        m_out[...] = mn
        v_out[...] = vn

    return pl.pallas_call(body, name=name, out_shape=[jax.ShapeDtypeStruct(w.shape, F32)] * 3)(g, w, m, v)


def kernel(x, mem, positions, g_mix, w_in, g_a_v, w_spatial, b_spatial, g_b_q, g_b_k, sinks, g_mem, w_mem_kv, g_c_q, g_c_k, w_branch_a, w_branch_b, w_branch_c, w_out, g_ffn, w_up, conv_w, conv_b, w_down, loss_target, m_g_mix, m_w_in, m_g_a_v, m_w_spatial, m_b_spatial, m_g_b_q, m_g_b_k, m_sinks, m_g_mem, m_w_mem_kv, m_g_c_q, m_g_c_k, m_w_branch_a, m_w_branch_b, m_w_branch_c, m_w_out, m_g_ffn, m_w_up, m_conv_w, m_conv_b, m_w_down, v_g_mix, v_w_in, v_g_a_v, v_w_spatial, v_b_spatial, v_g_b_q, v_g_b_k, v_sinks, v_g_mem, v_w_mem_kv, v_g_c_q, v_g_c_k, v_w_branch_a, v_w_branch_b, v_w_branch_c, v_w_out, v_g_ffn, v_w_up, v_conv_w, v_conv_b, v_w_down):
    S, D = x.shape[1], x.shape[2]
    M = mem.shape[1]
    F = w_down.shape[1] * N_DEV
    in_cols = w_in.shape[2] * N_DEV
    ax, ay, ac = lax.axis_index("x"), lax.axis_index("y"), lax.axis_index("c")
    core = jnp.reshape(ac, (1,)).astype(jnp.int32)
    chip = jnp.reshape(2 * ax + ay, (1,)).astype(jnp.int32)
    me = 4 * ax + 2 * ay + ac

    x2, mem2, tgt2 = x[0], mem[0], loss_target[0]

    big = dict(w_in=w_in[0].T, w_mem_kv=w_mem_kv[0], w_branch_a=w_branch_a[0], w_branch_b=w_branch_b[0],
               w_branch_c=w_branch_c[0], w_out=w_out[0], w_up=w_up[0], w_down=w_down[0])
    names = list(big)
    cast = {k: big[k].astype(BF) for k in names}
    W = {}
    cb3 = conv_b.reshape(2, 1, F)
    W["w_in"], = _allgather_seq([cast["w_in"]], "ag_seq0", 0)
    w_in_t = W["w_in"].reshape(in_cols, D)
    grp1 = ["w_mem_kv", "w_branch_a", "w_branch_b", "w_branch_c", "w_out"]
    res1 = _allgather_seq([cast[k] for k in grp1] + [conv_w[0]], "ag_seq1", 1, after=(_token((w_in_t,), "tok_w_in"),))
    W.update(zip(grp1, res1))
    cw3 = res1[-1]
    w_kv_f = W["w_mem_kv"].reshape(D, 2 * C_WIDTH)
    w_out_f = W["w_out"].reshape(D, D)

    half = ROPE_DIM // 2
    inv = ROPE_THETA ** (-jnp.arange(half, dtype=F32) / half)
    ang = positions[0].astype(F32)[:, None] * inv
    cos, sin = jnp.cos(ang), jnp.sin(ang)
    one, zero = jnp.ones((S, B_HEAD_DIM - ROPE_DIM), F32), jnp.zeros((S, B_HEAD_DIM - ROPE_DIM), F32)
    z8 = jnp.zeros((S, half), F32)
    ct = jnp.tile(jnp.concatenate([cos, cos, one], axis=1), (1, 2))
    sa = jnp.tile(jnp.concatenate([-sin, z8, zero], axis=1), (1, 2))
    sb = jnp.tile(jnp.concatenate([z8, sin, zero], axis=1), (1, 2))
    gq2, gk2 = jnp.tile(g_b_q, (1, 2)), jnp.tile(g_b_k, (1, 2))
    b_t = b_spatial[0].T

    h, rstd1 = _rms_fwd(x2, g_mix, "rms1_fwd")
    proj = _mm(h, w_in_t, "nt", F32, "mm_proj", tn=1280)
    y_a = _a_fwd(proj, g_a_v, w_spatial[0], b_t)
    qn, kn = _b_pre(proj, gq2, gk2, ct, sa, sb)
    W["w_up"], = _allgather_seq([cast["w_up"]], "ag_seq2", 2, after=(_token((W["w_out"], qn), "tok_group1"),))
    y_b = _b_attn_fwd(qn, kn, proj, sinks)
    mem_h, rstd_m = _rms_fwd(mem2, g_mem, "rmsmem_fwd")
    kv = _mm(mem_h, w_kv_f, "nn", F32, "mm_kv", after=(y_b,))
    y_c = _c_fwd(proj, kv, g_c_q, g_c_k)
    w_branches = [W["w_branch_a"], W["w_branch_b"], W["w_branch_c"]]
    merged, z_a, z_b, z_c = _merge_fwd(proj, [y_a, y_b, y_c], w_branches)
    x1 = _mm(merged, w_out_f, "nn", F32, "mm_x1", resid=x2)
    h2, rstd2 = _rms_fwd(x1, g_ffn, "rms2_fwd")
    W["w_down"], = _allgather_seq([cast["w_down"]], "ag_seq3", 3, after=(W["w_up"], h2))
    w_down_f = W["w_down"].reshape(F, D)
    up3 = _mm(h2, W["w_up"], "nn", BF, "mm_up", b_stack=True, out_parts=2)
    act = _ffn_act_fwd(up3, cw3, cb3)
    y = _mm(act, w_down_f, "nn", F32, "mm_y", resid=x1, tk=1408)
    dy, dy_b, loss_acc = _loss(y, tgt2)
    loss = lax.psum(loss_acc[0, 0], ("x", "y", "c"))

    reduced = {}

    def as4(g):
        return g.reshape(4, 2, g.shape[1], g.shape[2])

    def finish_group(gi, keys, g4, from_sibling):
        sums = [_pair_add(a, b, core, "rs_add_" + k) for k, a, b in zip(keys, g4, from_sibling)]
        from_chips = _chip_exchange(sums, f"rs_chip{gi}", 4 + gi)
        reduced.update(zip(keys, zip(sums, from_chips)))
        return tuple(sums)

    d_act = _mm(dy_b, w_down_f, "nt", BF, "mm_dact", tn=1408)
    g_down = _mm(act, dy_b, "tn", BF, "mm_gdown", tm=1408)
    d_up3, d_cw3, d_cb3 = _ffn_act_bwd(up3, cw3, cb3, d_act, after=(g_down,))
    grp0 = [as4(g_down.reshape(N_DEV, F // N_DEV, D))]
    g_up, sib0 = _mm(h2, d_up3, "tn", BF, "mm_gup", b_parts=2, out_stack=True, exchange=grp0)
    sums0 = finish_group(0, ["w_down"], grp0, sib0)
    grp1 = [as4(g_up)]
    d_h2, sib1 = _mm(d_up3, W["w_up"], "nt", F32, "mm_dh2", a_parts=2, b_stack=True, tm=2048, after=sums0, exchange=grp1)
    sums1 = finish_group(1, ["w_up"], grp1, sib1)
    dx1, dx1_b, d_g_ffn = _rms_bwd(x1, rstd2, g_ffn, d_h2, dy, "rms2_bwd", after=sums1)
    g_out = _mm(merged, dx1_b, "tn", BF, "mm_gout")
    grp2 = [as4(g_out.reshape(N_DEV, D // N_DEV, D))]
    d_merged, sib2 = _mm(dx1_b, w_out_f, "nt", F32, "mm_dmerged", exchange=grp2)
    sums2 = finish_group(2, ["w_out"], grp2, sib2)
    dz_a, dz_b, dz_c, dga, dgb, dgc, dy_a, dy_b_, dy_c = _merge_bwd(proj, [z_a, z_b, z_c], d_merged, w_branches, after=sums2)
    g_ba = _mm(y_a, dz_a, "tn", BF, "mm_gba", out_stack=True)
    g_bb = _mm(y_b, dz_b, "tn", BF, "mm_gbb", out_stack=True)
    g_bc = _mm(y_c, dz_c, "tn", BF, "mm_gbc", out_stack=True)
    d_uv, d_g_a_v, d_w_s, d_b_t = _a_bwd(proj, g_a_v, w_spatial[0], b_t, dy_a, after=(g_ba, g_bb, g_bc))
    dqn, dkn, dv_b, dsink_rows = _b_attn_bwd(qn, kn, proj, sinks, dy_b_)
    d_qkv, d_gq2, d_gk2 = _b_pre_bwd(proj, gq2, gk2, ct, sa, sb, dqn, dkn, dv_b)
    dq_c, dk_c, dv_c, d_gcq, d_gck = _c_bwd(proj, kv, g_c_q, g_c_k, dy_c)
    dkv_b = jnp.concatenate([dk_c, dv_c], axis=1).astype(BF)
    d_memh = _mm(dkv_b, w_kv_f, "nt", F32, "mm_dmemh")
    g_kv = _mm(mem_h, dkv_b, "tn", BF, "mm_gkv")
    _, _, d_g_mem = _rms_bwd(mem2, rstd_m, g_mem, d_memh, None, "rmsmem_bwd")
    dproj = jnp.concatenate([d_uv, d_qkv, dq_c, dga, dgb, dgc], axis=1)
    grp3 = [as4(g_ba), as4(g_bb), as4(g_bc)]
    g_in, sib3 = _mm(dproj, h, "tn", BF, "mm_gin", tm=1280, exchange=grp3)
    sums3 = finish_group(3, ["w_branch_a", "w_branch_b", "w_branch_c"], grp3, sib3)
    grp4 = [as4(g_in.reshape(N_DEV, in_cols // N_DEV, D)), as4(g_kv.reshape(N_DEV, D // N_DEV, 2 * C_WIDTH))]
    d_h, sib4 = _mm(dproj, w_in_t, "nn", F32, "mm_dh", tm=2048, tk=1280, after=sums3, exchange=grp4)
    sums4 = finish_group(4, ["w_in", "w_mem_kv"], grp4, sib4)
    grad_x, _, d_g_mix = _rms_bwd(x2, rstd1, g_mix, d_h, dx1, "rms1_bwd", after=sums4)

    small_names =["g_mix", "g_a_v", "w_spatial", "b_spatial", "g_b_q", "g_b_k", "sinks", "g_mem", "g_c_q", "g_c_k", "g_ffn", "conv_b"]
    small_w = dict(g_mix=g_mix, g_a_v=g_a_v, w_spatial=w_spatial, b_spatial=b_spatial, g_b_q=g_b_q, g_b_k=g_b_k, sinks=sinks,
                   g_mem=g_mem, g_c_q=g_c_q, g_c_k=g_c_k, g_ffn=g_ffn, conv_b=conv_b)
    small_m = dict(g_mix=m_g_mix, g_a_v=m_g_a_v, w_spatial=m_w_spatial, b_spatial=m_b_spatial, g_b_q=m_g_b_q, g_b_k=m_g_b_k,
                   sinks=m_sinks, g_mem=m_g_mem, g_c_q=m_g_c_q, g_c_k=m_g_c_k, g_ffn=m_g_ffn, conv_b=m_conv_b)
    small_v = dict(g_mix=v_g_mix, g_a_v=v_g_a_v, w_spatial=v_w_spatial, b_spatial=v_b_spatial, g_b_q=v_g_b_q, g_b_k=v_g_b_k,
                   sinks=v_sinks, g_mem=v_g_mem, g_c_q=v_g_c_q, g_c_k=v_g_c_k, g_ffn=v_g_ffn, conv_b=v_conv_b)
    small_g = dict(
        g_mix=d_g_mix, g_a_v=d_g_a_v, w_spatial=d_w_s, b_spatial=d_b_t.T,
        g_b_q=d_gq2.reshape(2, B_HEAD_DIM).sum(0), g_b_k=d_gk2.reshape(2, B_HEAD_DIM).sum(0),
        sinks=dsink_rows.sum(0)[:B_HEADS], g_mem=d_g_mem, g_c_q=d_gcq.sum(0), g_c_k=d_gck.sum(0), g_ffn=d_g_ffn,
        conv_b=d_cb3)
    partial = [small_g[k].reshape(small_w[k].shape) for k in small_names] + [d_cw3]
    parts = _allgather(partial, "ag_small")
    small_res, (g_cw3,) = _adamw_small(parts[:-1], [small_w[k] for k in small_names], [small_m[k] for k in small_names],
                                       [small_v[k] for k in small_names], parts[-1:], "adamw_small")
    small_out = dict(zip(small_names, small_res))
    c_cw = 2 * F // N_DEV
    g_cw = lax.dynamic_slice(g_cw3, (me // (N_DEV // 2), 0, (me % (N_DEV // 2)) * c_cw), (1, 3, c_cw))[0]
    cw_res = _adamw_plain(g_cw, conv_w[0], m_conv_w[0], v_conv_w[0], "adamw_conv_w")
    big_out = {"conv_w": [g_cw[None]] + [a[None] for a in cw_res]}

    moments = dict(w_in=(m_w_in, v_w_in), w_mem_kv=(m_w_mem_kv, v_w_mem_kv), w_branch_a=(m_w_branch_a, v_w_branch_a),
                   w_branch_b=(m_w_branch_b, v_w_branch_b), w_branch_c=(m_w_branch_c, v_w_branch_c), w_out=(m_w_out, v_w_out),
                   w_up=(m_w_up, v_w_up), w_down=(m_w_down, v_w_down))
    token = (grad_x, small_res[0][0])
    for k in ["w_down", "w_up", "w_out", "w_branch_a", "w_branch_b", "w_branch_c", "w_mem_kv", "w_in"]:
        s, r = reduced[k]
        mk, vk = moments[k][0][0], moments[k][1][0]
        if k == "w_in":
            res = _adamw_big(s, r, chip, big[k], mk.T, vk.T, "adamw_" + k, after=token)
            big_out[k] = [a.T[None] for a in res]
        else:
            res = _adamw_big(s, r, chip, big[k], mk, vk, "adamw_" + k, after=token)
            big_out[k] = [a[None] for a in res]
        token = (res[0],)

    order = ["g_mix", "w_in", "g_a_v", "w_spatial", "b_spatial", "g_b_q", "g_b_k", "sinks", "g_mem", "w_mem_kv", "g_c_q", "g_c_k",
             "w_branch_a", "w_branch_b", "w_branch_c", "w_out", "g_ffn", "w_up", "conv_w", "conv_b", "w_down"]
    res = {**small_out, **big_out}
    outs = [loss, grad_x[None]]
    for field in range(4):
        outs += [res[k][field] for k in order]
    return tuple(outs)
```

```python
import functools

import jax
import jax.numpy as jnp
from jax import lax
from jax.experimental import pallas as pl
from jax.experimental.pallas import tpu as pltpu
from jax.experimental.pallas import tpu_sc as plsc

F32 = jnp.float32
BF = jnp.bfloat16
EPS = 1e-6
NEG = -1e30

N_DEV = 8
CHUNK = 128
A_GROUPS = 4
A_WIDTH = 512
B_HEADS = 16
B_KV_HEADS = 2
B_HEAD_DIM = 64
B_WIDTH = 1024
B_KV_WIDTH = 128
ROPE_DIM = 16
ROPE_THETA = 500000.0
C_HEADS = 4
C_HEAD_DIM = 128
C_WIDTH = 512
GATE_OFF = 2 * A_WIDTH + B_WIDTH + 2 * B_KV_WIDTH + C_WIDTH

ADAM_LR = 0.001
ADAM_B1 = 0.9
ADAM_B2 = 0.999
ADAM_EPS = 1e-08
ADAM_WD = 0.01
ADAM_STEP = 10

VMEM_LIMIT = 48 * 1024 * 1024
MESH = pl.DeviceIdType.MESH


def _pick(n, prefs):
    for p in prefs:
        if p <= n and n % p == 0:
            return p
    return n


def _params(sem):
    return pltpu.CompilerParams(dimension_semantics=sem, vmem_limit_bytes=VMEM_LIMIT)


def _hide(body, n_seen, n_hidden):
    if not n_hidden:
        return body

    def wrapped(*refs):
        return body(*refs[:n_seen], *refs[n_seen + n_hidden:])

    return wrapped


def _hidden_specs(after):
    return [pl.BlockSpec(memory_space=pl.ANY) for _ in after]


def _token(xs, name):
    def body(*refs):
        refs[-1][...] = jnp.zeros_like(refs[-1])

    return pl.pallas_call(body, name=name, in_specs=_hidden_specs(xs), out_shape=jax.ShapeDtypeStruct((8, 128), F32))(*xs)


def _mm(a, b, mode, out_dtype, name, *, resid=None, b_stack=False, a_parts=0, b_parts=0, out_parts=0,
        out_stack=False, tm=1024, tn=1024, tk=2048, after=(), exchange=(), cols=None, into=None):
    if mode == "nn":
        M = a.shape[-2]
        K = a.shape[-1] * max(a_parts, 1)
        N = b.shape[-1] * (N_DEV if b_stack else 1)
        dims = (((1,), (0,)), ((), ()))
    elif mode == "nt":
        M = a.shape[-2]
        K = a.shape[-1] * max(a_parts, 1)
        N = b.shape[-2]
        dims = (((1,), (1,)), ((), ()))
    else:
        K = a.shape[-2]
        M = a.shape[-1]
        N = b.shape[-1] * max(b_parts, 1)
        dims = (((0,), (0,)), ((), ()))
    if b_stack and mode == "nn":
        tn = b.shape[-1]
    if b_stack and mode == "nt":
        tk = b.shape[-1]
    if out_stack:
        tn = N // N_DEV
    tm, tn, tk = _pick(M, (tm,)), _pick(N, (tn,)), _pick(K, (tk,))
    if M % tm or N % tn or K % tk:
        raise ValueError(f"{name}: tiles {tm},{tn},{tk} do not divide {M},{N},{K}")
    nm, nn, nk = M // tm, N // tn, K // tk
    n_off = 0
    if cols is not None:
        if mode != "nn" or b_stack or out_stack or out_parts or resid is not None or nn % cols[1]:
            raise ValueError(f"{name}: cols= needs mode nn with plain operands")
        nn = nn // cols[1]
        n_off = cols[0] * nn

    def parts_idx(t, ntile, parts):
        per = ntile // parts
        return t // per, t % per

    if mode in ("nn", "nt"):
        if a_parts:
            a_spec = pl.BlockSpec((None, tm, tk), lambda m, n, k: (parts_idx(k, nk, a_parts)[0], m, parts_idx(k, nk, a_parts)[1]))
        else:
            a_spec = pl.BlockSpec((tm, tk), lambda m, n, k: (m, k))
    else:
        a_spec = pl.BlockSpec((tk, tm), lambda m, n, k: (k, m))
    if mode == "nn":
        if b_stack:
            b_spec = pl.BlockSpec((None, tk, tn), lambda m, n, k: (n, k, 0))
        else:
            b_spec = pl.BlockSpec((tk, tn), lambda m, n, k: (k, n + n_off))
    elif mode == "nt":
        if b_stack:
            b_spec = pl.BlockSpec((None, tn, tk), lambda m, n, k: (k, n, 0))
        else:
            b_spec = pl.BlockSpec((tn, tk), lambda m, n, k: (n, k))
    else:
        if b_parts:
            b_spec = pl.BlockSpec((None, tk, tn), lambda m, n, k: (parts_idx(n, nn, b_parts)[0], k, parts_idx(n, nn, b_parts)[1]))
        else:
            b_spec = pl.BlockSpec((tk, tn), lambda m, n, k: (k, n))
    if out_stack:
        out_shape = jax.ShapeDtypeStruct((N_DEV, M, tn), out_dtype)
        o_spec = pl.BlockSpec((None, tm, tn), lambda m, n, k: (n, m, 0))
    elif out_parts:
        out_shape = jax.ShapeDtypeStruct((out_parts, M, N // out_parts), out_dtype)
        o_spec = pl.BlockSpec((None, tm, tn), lambda m, n, k: (parts_idx(n, nn, out_parts)[0], m, parts_idx(n, nn, out_parts)[1]))
    else:
        out_shape = jax.ShapeDtypeStruct((M, N), out_dtype)
        o_spec = pl.BlockSpec((tm, tn), lambda m, n, k: (m, n + n_off))
    has_resid = resid is not None
    if into is not None:
        after = tuple(after) + (into,)

    n_ex = len(exchange)
    n_in = 2 + has_resid + len(after)

    def body(*refs):
        a_ref, b_ref = refs[:2]
        r_ref = refs[2] if has_resid else None
        ex_in = refs[n_in:n_in + n_ex]
        o_ref = refs[n_in + n_ex]
        ex_out = refs[n_in + n_ex + 1:n_in + 2 * n_ex + 1]
        scratch = refs[n_in + 2 * n_ex + 1:]
        m_i, n_i, k = pl.program_id(0), pl.program_id(1), pl.program_id(2)

        def pushes():
            send_sems, recv_sems = scratch[-2:]
            x, y, c = lax.axis_index("x"), lax.axis_index("y"), lax.axis_index("c")
            return [pltpu.make_async_remote_copy(
                src_ref=ex_in[w].at[:, 1 - c], dst_ref=ex_out[w], send_sem=send_sems.at[w], recv_sem=recv_sems.at[w],
                device_id=(x, y, 1 - c), device_id_type=MESH) for w in range(n_ex)]

        if n_ex:
            @pl.when((m_i == 0) & (n_i == 0) & (k == 0))
            def _():
                for cp in pushes():
                    cp.start()

        if nk == 1:
            res = lax.dot_general(a_ref[...], b_ref[...], dims, preferred_element_type=F32)
            if has_resid:
                res = res + r_ref[...]
            o_ref[...] = res.astype(o_ref.dtype)
        else:
            acc = scratch[0]

            @pl.when(k == 0)
            def _():
                acc[...] = jnp.zeros_like(acc)

            acc[...] += lax.dot_general(a_ref[...], b_ref[...], dims, preferred_element_type=F32)

            @pl.when(k == nk - 1)
            def _():
                res = acc[...]
                if has_resid:
                    res = res + r_ref[...]
                o_ref[...] = res.astype(o_ref.dtype)

        if n_ex:
            @pl.when((m_i == nm - 1) & (n_i == nn - 1) & (k == nk - 1))
            def _():
                for cp in pushes():
                    cp.wait()

    in_specs = [a_spec, b_spec]
    args = [a, b]
    if has_resid:
        in_specs.append(pl.BlockSpec((tm, tn), lambda m, n, k: (m, n)))
        args.append(resid)
    in_specs += _hidden_specs(after) + _hidden_specs(exchange)
    args += list(after) + list(exchange)
    scratch_shapes = [pltpu.VMEM((tm, tn), F32)] if nk > 1 else []
    aliases = {2 + has_resid + len(after) - 1: 0} if into is not None else {}
    if not n_ex:
        return pl.pallas_call(
            body, name=name, grid=(nm, nn, nk), in_specs=in_specs, out_specs=o_spec, out_shape=out_shape,
            scratch_shapes=scratch_shapes, input_output_aliases=aliases,
            compiler_params=_params(("parallel", "parallel", "arbitrary")),
        )(*args)
    res = pl.pallas_call(
        body, name=name, grid=(nm, nn, nk), in_specs=in_specs, out_specs=[o_spec] + _hidden_specs(exchange),
        out_shape=[out_shape] + [jax.ShapeDtypeStruct((g.shape[0],) + g.shape[2:], g.dtype) for g in exchange],
        scratch_shapes=scratch_shapes + [pltpu.SemaphoreType.DMA((n_ex,)), pltpu.SemaphoreType.DMA((n_ex,))],
        input_output_aliases=aliases, compiler_params=_params(("arbitrary", "arbitrary", "arbitrary")),
    )(*args)
    return res[0], list(res[1:])


def _rms_fwd(x, g, name):
    R, D = x.shape
    tr = _pick(R, (256,))

    def body(x_ref, g_ref, h_ref, r_ref):
        xv = x_ref[...]
        r = lax.rsqrt(jnp.mean(xv * xv, axis=-1, keepdims=True) + EPS)
        h_ref[...] = (xv * r * g_ref[...]).astype(BF)
        r_ref[...] = r

    return pl.pallas_call(
        body, name=name, grid=(R // tr,),
        in_specs=[pl.BlockSpec((tr, D), lambda i: (i, 0)), pl.BlockSpec((1, D), lambda i: (0, 0))],
        out_specs=[pl.BlockSpec((tr, D), lambda i: (i, 0)), pl.BlockSpec((tr, 1), lambda i: (i, 0))],
        out_shape=[jax.ShapeDtypeStruct((R, D), BF), jax.ShapeDtypeStruct((R, 1), F32)],
        compiler_params=_params(("parallel",)),
    )(x, g)


def _rms_bwd(x, r, g, dh, dres, name, after=()):
    R, D = x.shape
    tr = _pick(R, (256,))
    has_res = dres is not None

    def body(*refs):
        if has_res:
            x_ref, r_ref, g_ref, dh_ref, dres_ref, dx_ref, dxb_ref, dg_ref = refs
        else:
            x_ref, r_ref, g_ref, dh_ref, dx_ref, dxb_ref, dg_ref = refs
        i = pl.program_id(0)
        xv, rv, dhv = x_ref[...], r_ref[...], dh_ref[...]
        gy = dhv * g_ref[...]
        c = jnp.sum(xv * gy, axis=-1, keepdims=True)
        dx = rv * gy - xv * (rv * rv * rv) * (c * (1.0 / D))
        if has_res:
            dx = dx + dres_ref[...]
        dx_ref[...] = dx
        dxb_ref[...] = dx.astype(BF)
        part = jnp.sum(dhv * xv * rv, axis=0, keepdims=True)

        @pl.when(i == 0)
        def _():
            dg_ref[...] = part

        @pl.when(i > 0)
        def _():
            dg_ref[...] += part

    row = pl.BlockSpec((tr, D), lambda i: (i, 0))
    in_specs = [row, pl.BlockSpec((tr, 1), lambda i: (i, 0)), pl.BlockSpec((1, D), lambda i: (0, 0)), row]
    args = [x, r, g, dh]
    if has_res:
        in_specs.append(row)
        args.append(dres)
    return pl.pallas_call(
        _hide(body, len(args), len(after)), name=name, grid=(R // tr,), in_specs=in_specs + _hidden_specs(after),
        out_specs=[row, row, pl.BlockSpec((1, D), lambda i: (0, 0))],
        out_shape=[jax.ShapeDtypeStruct((R, D), F32), jax.ShapeDtypeStruct((R, D), BF), jax.ShapeDtypeStruct((1, D), F32)],
        compiler_params=_params(("arbitrary",)),
    )(*args, *after)


def _a_chunk(us, vs, gvs, ws, bs):
    r_i = lax.broadcasted_iota(jnp.int32, (CHUNK, CHUNK), 0)
    c_i = lax.broadcasted_iota(jnp.int32, (CHUNK, CHUNK), 1)
    causal = r_i >= c_i
    vg = [jax.nn.gelu(v) for v in vs]
    ss = sum(jnp.sum(v * v, axis=-1, keepdims=True) for v in vg)
    r = lax.rsqrt(ss * (1.0 / A_WIDTH) + EPS)
    ys = []
    for g in range(A_GROUPS):
        vn = vg[g] * r * gvs[g]
        w = jnp.where(causal, ws[g], 0.0)
        s = jnp.dot(w.astype(BF), vn.astype(BF), preferred_element_type=F32) + bs[g]
        ys.append(jax.nn.gelu(us[g]) * s)
    return ys


def _a_split(u_ref, v_ref, g_ref, w_ref, b_ref):
    sl = [slice(g * 128, (g + 1) * 128) for g in range(A_GROUPS)]
    return ([u_ref[:, s] for s in sl], [v_ref[:, s] for s in sl], [g_ref[:, s] for s in sl],
            [w_ref[g] for g in range(A_GROUPS)], [b_ref[:, g:g + 1] for g in range(A_GROUPS)])


def _a_specs(S):
    return [pl.BlockSpec((CHUNK, A_WIDTH), lambda n: (n, 0)), pl.BlockSpec((CHUNK, A_WIDTH), lambda n: (n, 1)),
            pl.BlockSpec((1, A_WIDTH), lambda n: (0, 0)), pl.BlockSpec((A_GROUPS, CHUNK, CHUNK), lambda n: (0, 0, 0)),
            pl.BlockSpec((CHUNK, A_GROUPS), lambda n: (0, 0))]


def _a_fwd(proj, g_v, w_s, b_t):
    S = proj.shape[0]

    def body(u_ref, v_ref, g_ref, w_ref, b_ref, y_ref):
        ys = _a_chunk(*_a_split(u_ref, v_ref, g_ref, w_ref, b_ref))
        for g in range(A_GROUPS):
            y_ref[:, g * 128:(g + 1) * 128] = ys[g].astype(BF)

    return pl.pallas_call(
        body, name="a_fwd", grid=(S // CHUNK,), in_specs=_a_specs(S),
        out_specs=pl.BlockSpec((CHUNK, A_WIDTH), lambda n: (n, 0)),
        out_shape=jax.ShapeDtypeStruct((S, A_WIDTH), BF), compiler_params=_params(("parallel",)),
    )(proj, proj, g_v, w_s, b_t)


def _a_bwd(proj, g_v, w_s, b_t, dy, after=()):
    S = proj.shape[0]

    def body(u_ref, v_ref, g_ref, w_ref, b_ref, dy_ref, duv_ref, dg_ref, dw_ref, db_ref):
        n = pl.program_id(0)
        dys = [dy_ref[:, g * 128:(g + 1) * 128] for g in range(A_GROUPS)]
        _, vjp = jax.vjp(_a_chunk, *_a_split(u_ref, v_ref, g_ref, w_ref, b_ref))
        dus, dvs, dgs, dws, dbs = vjp(dys)

        @pl.when(n == 0)
        def _():
            dg_ref[...] = jnp.zeros_like(dg_ref)
            dw_ref[...] = jnp.zeros_like(dw_ref)
            db_ref[...] = jnp.zeros_like(db_ref)

        for g in range(A_GROUPS):
            duv_ref[:, g * 128:(g + 1) * 128] = dus[g].astype(BF)
            duv_ref[:, A_WIDTH + g * 128:A_WIDTH + (g + 1) * 128] = dvs[g].astype(BF)
            dg_ref[:, g * 128:(g + 1) * 128] += dgs[g]
            dw_ref[g] += dws[g]
            db_ref[:, g:g + 1] += dbs[g]

    return pl.pallas_call(
        _hide(body, 6, len(after)), name="a_bwd", grid=(S // CHUNK,),
        in_specs=_a_specs(S) + [pl.BlockSpec((CHUNK, A_WIDTH), lambda n: (n, 0))] + _hidden_specs(after),
        out_specs=[pl.BlockSpec((CHUNK, 2 * A_WIDTH), lambda n: (n, 0)), pl.BlockSpec((1, A_WIDTH), lambda n: (0, 0)),
                   pl.BlockSpec((A_GROUPS, CHUNK, CHUNK), lambda n: (0, 0, 0)), pl.BlockSpec((CHUNK, A_GROUPS), lambda n: (0, 0))],
        out_shape=[jax.ShapeDtypeStruct((S, 2 * A_WIDTH), BF), jax.ShapeDtypeStruct((1, A_WIDTH), F32),
                   jax.ShapeDtypeStruct((A_GROUPS, CHUNK, CHUNK), F32), jax.ShapeDtypeStruct((CHUNK, A_GROUPS), F32)],
        compiler_params=_params(("arbitrary",)),
    )(proj, proj, g_v, w_s, b_t, dy, *after)


def _half_mask(shape, which):
    lane = lax.broadcasted_iota(jnp.int32, shape, len(shape) - 1)
    return (lane >= 64) == (which == 1)


def _pair_norm_rope(x, g, ct, sa, sb):
    lo = _half_mask(x.shape, 0)
    x2 = x * x
    ss_lo = jnp.sum(jnp.where(lo, x2, 0.0), axis=-1, keepdims=True)
    ss_hi = jnp.sum(jnp.where(lo, 0.0, x2), axis=-1, keepdims=True)
    r = jnp.where(lo, lax.rsqrt(ss_lo * (1.0 / B_HEAD_DIM) + EPS), lax.rsqrt(ss_hi * (1.0 / B_HEAD_DIM) + EPS))
    xr = x * r
    xn = xr * g
    out = xn * ct + pltpu.roll(xn, 120, 1) * sa + pltpu.roll(xn, 8, 1) * sb
    return out, xr, r


def _pair_norm_rope_bwd(x, g, ct, sa, sb, dout):
    lo = _half_mask(x.shape, 0)
    _, xr, r = _pair_norm_rope(x, g, ct, sa, sb)
    dxn = dout * ct + pltpu.roll(dout * sa, 8, 1) + pltpu.roll(dout * sb, 120, 1)
    gy = dxn * g
    t = xr * gy
    c_lo = jnp.sum(jnp.where(lo, t, 0.0), axis=-1, keepdims=True)
    c_hi = jnp.sum(jnp.where(lo, 0.0, t), axis=-1, keepdims=True)
    c = jnp.where(lo, c_lo, c_hi)
    dx = r * (gy - xr * c * (1.0 / B_HEAD_DIM))
    dg = jnp.sum(dxn * xr, axis=0, keepdims=True)
    return dx, dg


def _b_pre(proj, gq2, gk2, ct, sa, sb):
    S = proj.shape[0]
    tr = _pick(S, (256,))
    n_pair = B_WIDTH // 128

    def body(q_ref, k_ref, gq_ref, gk_ref, ct_ref, sa_ref, sb_ref, qn_ref, kn_ref):
        ct_v, sa_v, sb_v = ct_ref[...], sa_ref[...], sb_ref[...]
        for p in range(n_pair):
            o, _, _ = _pair_norm_rope(q_ref[:, p * 128:(p + 1) * 128], gq_ref[...], ct_v, sa_v, sb_v)
            qn_ref[:, p * 128:(p + 1) * 128] = o.astype(BF)
        o, _, _ = _pair_norm_rope(k_ref[...], gk_ref[...], ct_v, sa_v, sb_v)
        kn_ref[...] = o.astype(BF)

    tab = pl.BlockSpec((tr, 128), lambda i: (i, 0))
    gsp = pl.BlockSpec((1, 128), lambda i: (0, 0))
    return pl.pallas_call(
        body, name="b_pre", grid=(S // tr,),
        in_specs=[pl.BlockSpec((tr, B_WIDTH), lambda i: (i, 1)), pl.BlockSpec((tr, 128), lambda i: (i, 2 * B_WIDTH // 128)),
                  gsp, gsp, tab, tab, tab],
        out_specs=[pl.BlockSpec((tr, B_WIDTH), lambda i: (i, 0)), tab],
        out_shape=[jax.ShapeDtypeStruct((S, B_WIDTH), BF), jax.ShapeDtypeStruct((S, 128), BF)],
        compiler_params=_params(("parallel",)),
    )(proj, proj, gq2, gk2, ct, sa, sb)


def _b_pre_bwd(proj, gq2, gk2, ct, sa, sb, dqn, dkn, dv):
    S = proj.shape[0]
    tr = _pick(S, (256,))
    n_pair = B_WIDTH // 128

    def body(q_ref, k_ref, gq_ref, gk_ref, ct_ref, sa_ref, sb_ref, dqn_ref, dkn_ref, dv_ref, dqkv_ref, dgq_ref, dgk_ref):
        i = pl.program_id(0)
        ct_v, sa_v, sb_v = ct_ref[...], sa_ref[...], sb_ref[...]
        dgq = jnp.zeros((1, 128), F32)
        for p in range(n_pair):
            sl = slice(p * 128, (p + 1) * 128)
            dx, dg = _pair_norm_rope_bwd(q_ref[:, sl], gq_ref[...], ct_v, sa_v, sb_v, dqn_ref[:, sl])
            dqkv_ref[:, sl] = dx.astype(BF)
            dgq = dgq + dg
        dx, dgk = _pair_norm_rope_bwd(k_ref[...], gk_ref[...], ct_v, sa_v, sb_v, dkn_ref[...])
        dqkv_ref[:, B_WIDTH:B_WIDTH + 128] = dx.astype(BF)
        dqkv_ref[:, B_WIDTH + 128:B_WIDTH + 256] = dv_ref[...].astype(BF)

        @pl.when(i == 0)
        def _():
            dgq_ref[...] = dgq
            dgk_ref[...] = dgk

        @pl.when(i > 0)
        def _():
            dgq_ref[...] += dgq
            dgk_ref[...] += dgk

    tab = pl.BlockSpec((tr, 128), lambda i: (i, 0))
    gsp = pl.BlockSpec((1, 128), lambda i: (0, 0))
    return pl.pallas_call(
        body, name="b_pre_bwd", grid=(S // tr,),
        in_specs=[pl.BlockSpec((tr, B_WIDTH), lambda i: (i, 1)), pl.BlockSpec((tr, 128), lambda i: (i, 2 * B_WIDTH // 128)),
                  gsp, gsp, tab, tab, tab, pl.BlockSpec((tr, B_WIDTH), lambda i: (i, 0)), tab, tab],
        out_specs=[pl.BlockSpec((tr, B_WIDTH + 256), lambda i: (i, 0)), gsp, gsp],
        out_shape=[jax.ShapeDtypeStruct((S, B_WIDTH + 256), BF), jax.ShapeDtypeStruct((1, 128), F32), jax.ShapeDtypeStruct((1, 128), F32)],
        compiler_params=_params(("arbitrary",)),
    )(proj, proj, gq2, gk2, ct, sa, sb, dqn, dkn, dv)


def _b_dup(x2, g):
    d = jnp.where(_half_mask(x2.shape, g), x2, 0.0)
    return (d + pltpu.roll(d, 64, 1)).astype(BF)


PAIRS_PER_GROUP = B_HEADS // B_KV_HEADS // 2
GROUP_ROWS = PAIRS_PER_GROUP * CHUNK


def _b_valid(n):
    row = lax.broadcasted_iota(jnp.int32, (GROUP_ROWS, 2 * CHUNK), 0) & (CHUNK - 1)
    col = lax.broadcasted_iota(jnp.int32, (GROUP_ROWS, 2 * CHUNK), 1)
    rel = row + CHUNK - col
    return (rel >= 0) & (rel < CHUNK) & ((col >= CHUNK) | (n > 0))


def _b_blocks(x2, g):
    xd = _b_dup(x2, g)
    lo = _half_mask(xd.shape, 0)
    zero = jnp.zeros_like(xd)
    return jnp.concatenate([jnp.where(lo, xd, zero), jnp.where(lo, zero, xd)], axis=0)


def _b_sink_col(s_ref, g, hf):
    rb = lax.broadcasted_iota(jnp.int32, (GROUP_ROWS, 1), 0) // CHUNK
    col = jnp.zeros((GROUP_ROWS, 1), F32)
    for pp in range(PAIRS_PER_GROUP):
        col = jnp.where(rb == pp, s_ref[0, 2 * (g * PAIRS_PER_GROUP + pp) + hf], col)
    return col


def _b_probs(qs, kblk, valid, sinks):
    s = lax.dot_general(qs, kblk, (((1,), (1,)), ((), ())), preferred_element_type=F32) * (B_HEAD_DIM ** -0.5)
    out = []
    for hf in range(2):
        sh = jnp.where(valid, s[:, hf * 2 * CHUNK:(hf + 1) * 2 * CHUNK], NEG)
        m = jnp.maximum(jnp.max(sh, axis=-1, keepdims=True), sinks[hf])
        e = jnp.exp(sh - m)
        es = jnp.exp(sinks[hf] - m)
        inv = 1.0 / (jnp.sum(e, axis=-1, keepdims=True) + es)
        out.append((e * inv, es * inv))
    return out


def _b_fold(acc, g):
    lo = _half_mask((2 * CHUNK, 128), 0)
    t = jnp.where(lo, acc[:2 * CHUNK], 0.0) + jnp.where(lo, 0.0, acc[2 * CHUNK:])
    return jnp.where(_half_mask((2 * CHUNK, 128), g), t + pltpu.roll(t, 64, 1), 0.0)


def _b_kv_specs(S):
    prev = lambda n: (jnp.maximum(n - 1, 0), 0)
    cur = lambda n: (n, 0)
    v_col = (2 * B_WIDTH + B_KV_WIDTH) // 128
    return [pl.BlockSpec((CHUNK, 128), prev), pl.BlockSpec((CHUNK, 128), cur),
            pl.BlockSpec((CHUNK, 128), lambda n: (jnp.maximum(n - 1, 0), v_col)), pl.BlockSpec((CHUNK, 128), lambda n: (n, v_col))]


def _b_attn_fwd(qn, kn, proj, sinks):
    S = qn.shape[0]

    def body(s_ref, q_ref, kp_ref, kc_ref, vp_ref, vc_ref, y_ref):
        n = pl.program_id(0)
        valid = _b_valid(n)
        k2 = jnp.concatenate([kp_ref[...], kc_ref[...]], axis=0).astype(F32)
        v2 = jnp.concatenate([vp_ref[...], vc_ref[...]], axis=0)
        for g in range(B_KV_HEADS):
            pairs = [g * PAIRS_PER_GROUP + pp for pp in range(PAIRS_PER_GROUP)]
            qs = jnp.concatenate([q_ref[:, p * 128:(p + 1) * 128] for p in pairs], axis=0)
            probs = _b_probs(qs, _b_blocks(k2, g), valid, [_b_sink_col(s_ref, g, hf) for hf in range(2)])
            pcat = jnp.concatenate([probs[0][0].astype(BF), probs[1][0].astype(BF)], axis=1)
            o = jnp.dot(pcat, _b_blocks(v2, g), preferred_element_type=F32)
            for pp, p in enumerate(pairs):
                y_ref[:, p * 128:(p + 1) * 128] = o[pp * CHUNK:(pp + 1) * CHUNK].astype(BF)

    return pl.pallas_call(
        body, name="b_attn_fwd", grid=(S // CHUNK,),
        in_specs=[pl.BlockSpec(memory_space=pltpu.SMEM), pl.BlockSpec((CHUNK, B_WIDTH), lambda n: (n, 0))] + _b_kv_specs(S),
        out_specs=pl.BlockSpec((CHUNK, B_WIDTH), lambda n: (n, 0)),
        out_shape=jax.ShapeDtypeStruct((S, B_WIDTH), BF), compiler_params=_params(("arbitrary",)),
    )(sinks, qn, kn, kn, proj, proj)


def _b_attn_bwd(qn, kn, proj, sinks, dy, after=()):
    S = qn.shape[0]

    def body(s_ref, q_ref, kp_ref, kc_ref, vp_ref, vc_ref, dy_ref, dq_ref, dk_ref, dv_ref, ds_ref):
        n = pl.program_id(0)

        @pl.when(n == 0)
        def _():
            dk_ref[...] = jnp.zeros_like(dk_ref)
            dv_ref[...] = jnp.zeros_like(dv_ref)
            ds_ref[...] = jnp.zeros_like(ds_ref)

        valid = _b_valid(n)
        k2 = jnp.concatenate([kp_ref[...], kc_ref[...]], axis=0).astype(F32)
        v2 = jnp.concatenate([vp_ref[...], vc_ref[...]], axis=0)
        lane = lax.broadcasted_iota(jnp.int32, (CHUNK, 128), 1)
        dk2 = jnp.zeros((2 * CHUNK, 128), F32)
        dv2 = jnp.zeros((2 * CHUNK, 128), F32)
        dsink = jnp.zeros((CHUNK, 128), F32)
        scale = B_HEAD_DIM ** -0.5
        nt = (((1,), (1,)), ((), ()))
        tn = (((0,), (0,)), ((), ()))
        for g in range(B_KV_HEADS):
            pairs = [g * PAIRS_PER_GROUP + pp for pp in range(PAIRS_PER_GROUP)]
            qs = jnp.concatenate([q_ref[:, p * 128:(p + 1) * 128] for p in pairs], axis=0)
            do = jnp.concatenate([dy_ref[:, p * 128:(p + 1) * 128] for p in pairs], axis=0)
            do_b = do.astype(BF)
            kblk, vblk = _b_blocks(k2, g), _b_blocks(v2, g)
            probs = _b_probs(qs, kblk, valid, [_b_sink_col(s_ref, g, hf) for hf in range(2)])
            pcat = jnp.concatenate([probs[0][0].astype(BF), probs[1][0].astype(BF)], axis=1)
            o = jnp.dot(pcat, vblk, preferred_element_type=F32)
            dp = lax.dot_general(do_b, vblk, nt, preferred_element_type=F32)
            prod = do * o
            ds_halves = []
            for hf in range(2):
                pr, ps = probs[hf]
                delta = jnp.sum(jnp.where(_half_mask(prod.shape, hf), prod, 0.0), axis=-1, keepdims=True)
                ds_halves.append((pr * (dp[:, hf * 2 * CHUNK:(hf + 1) * 2 * CHUNK] - delta) * scale).astype(BF))
                t = -ps * delta
                for pp, p in enumerate(pairs):
                    dsink = dsink + jnp.where(lane == 2 * p + hf, t[pp * CHUNK:(pp + 1) * CHUNK], 0.0)
            dsc = jnp.concatenate(ds_halves, axis=1)
            dq = jnp.dot(dsc, kblk, preferred_element_type=F32)
            for pp, p in enumerate(pairs):
                dq_ref[:, p * 128:(p + 1) * 128] = dq[pp * CHUNK:(pp + 1) * CHUNK]
            dk2 = dk2 + _b_fold(lax.dot_general(dsc, qs, tn, preferred_element_type=F32), g)
            dv2 = dv2 + _b_fold(lax.dot_general(pcat, do_b, tn, preferred_element_type=F32), g)
        ds_ref[...] += dsink
        cur = pl.ds(pl.multiple_of(n * CHUNK, CHUNK), CHUNK)
        dk_ref[cur, :] += dk2[CHUNK:]
        dv_ref[cur, :] += dv2[CHUNK:]

        @pl.when(n > 0)
        def _():
            prv = pl.ds(pl.multiple_of((n - 1) * CHUNK, CHUNK), CHUNK)
            dk_ref[prv, :] += dk2[:CHUNK]
            dv_ref[prv, :] += dv2[:CHUNK]

    full = pl.BlockSpec((S, 128), lambda n: (0, 0))
    return pl.pallas_call(
        _hide(body, 7, len(after)), name="b_attn_bwd", grid=(S // CHUNK,),
        in_specs=[pl.BlockSpec(memory_space=pltpu.SMEM), pl.BlockSpec((CHUNK, B_WIDTH), lambda n: (n, 0))] + _b_kv_specs(S)
        + [pl.BlockSpec((CHUNK, B_WIDTH), lambda n: (n, 0))] + _hidden_specs(after),
        out_specs=[pl.BlockSpec((CHUNK, B_WIDTH), lambda n: (n, 0)), full, full, pl.BlockSpec((CHUNK, 128), lambda n: (0, 0))],
        out_shape=[jax.ShapeDtypeStruct((S, B_WIDTH), F32), jax.ShapeDtypeStruct((S, 128), F32), jax.ShapeDtypeStruct((S, 128), F32),
                   jax.ShapeDtypeStruct((CHUNK, 128), F32)],
        compiler_params=_params(("arbitrary",)),
    )(sinks, qn, kn, kn, proj, proj, dy, *after)


def _c_block(q, k, v, gq, gk):
    qn = q * lax.rsqrt(jnp.mean(q * q, axis=-1, keepdims=True) + EPS) * gq
    kn = k * lax.rsqrt(jnp.mean(k * k, axis=-1, keepdims=True) + EPS) * gk
    s = lax.dot_general(qn.astype(BF), kn.astype(BF), (((1,), (1,)), ((), ())), preferred_element_type=F32) * (C_HEAD_DIM ** -0.5)
    p = jax.nn.softmax(s, axis=-1)
    return jnp.dot(p.astype(BF), v.astype(BF), preferred_element_type=F32)


def _c_specs(S, M, tq):
    q_col = (2 * A_WIDTH + B_WIDTH + 2 * B_KV_WIDTH) // 128
    return [pl.BlockSpec((tq, 128), lambda h, i: (i, q_col + h)), pl.BlockSpec((M, 128), lambda h, i: (0, h)),
            pl.BlockSpec((M, 128), lambda h, i: (0, C_HEADS + h)), pl.BlockSpec((1, 128), lambda h, i: (0, 0)),
            pl.BlockSpec((1, 128), lambda h, i: (0, 0))]


def _c_fwd(proj, kv, gq, gk):
    S, M = proj.shape[0], kv.shape[0]
    tq = _pick(S, (512,))

    def body(q_ref, k_ref, v_ref, gq_ref, gk_ref, y_ref):
        y_ref[...] = _c_block(q_ref[...], k_ref[...], v_ref[...], gq_ref[...], gk_ref[...]).astype(BF)

    return pl.pallas_call(
        body, name="c_fwd", grid=(C_HEADS, S // tq), in_specs=_c_specs(S, M, tq),
        out_specs=pl.BlockSpec((tq, 128), lambda h, i: (i, h)),
        out_shape=jax.ShapeDtypeStruct((S, C_WIDTH), BF), compiler_params=_params(("parallel", "parallel")),
    )(proj, kv, kv, gq, gk)


def _c_bwd(proj, kv, gq, gk, dy):
    S, M = proj.shape[0], kv.shape[0]
    tq = _pick(S, (512,))

    def body(q_ref, k_ref, v_ref, gq_ref, gk_ref, dy_ref, dq_ref, dk_ref, dv_ref, dgq_ref, dgk_ref):
        i = pl.program_id(1)
        _, vjp = jax.vjp(_c_block, q_ref[...], k_ref[...], v_ref[...], gq_ref[...], gk_ref[...])
        dq, dk, dv, dgq, dgk = vjp(dy_ref[...])
        dq_ref[...] = dq.astype(BF)

        @pl.when(i == 0)
        def _():
            dk_ref[...] = dk
            dv_ref[...] = dv
            dgq_ref[...] = dgq
            dgk_ref[...] = dgk

        @pl.when(i > 0)
        def _():
            dk_ref[...] += dk
            dv_ref[...] += dv
            dgq_ref[...] += dgq
            dgk_ref[...] += dgk

    return pl.pallas_call(
        body, name="c_bwd", grid=(C_HEADS, S // tq),
        in_specs=_c_specs(S, M, tq) + [pl.BlockSpec((tq, 128), lambda h, i: (i, h))],
        out_specs=[pl.BlockSpec((tq, 128), lambda h, i: (i, h)), pl.BlockSpec((M, 128), lambda h, i: (0, h)),
                   pl.BlockSpec((M, 128), lambda h, i: (0, h)), pl.BlockSpec((None, 1, 128), lambda h, i: (h, 0, 0)),
                   pl.BlockSpec((None, 1, 128), lambda h, i: (h, 0, 0))],
        out_shape=[jax.ShapeDtypeStruct((S, C_WIDTH), BF), jax.ShapeDtypeStruct((M, C_WIDTH), F32), jax.ShapeDtypeStruct((M, C_WIDTH), F32),
                   jax.ShapeDtypeStruct((C_HEADS, 1, 128), F32), jax.ShapeDtypeStruct((C_HEADS, 1, 128), F32)],
        compiler_params=_params(("parallel", "arbitrary")),
    )(proj, kv, kv, gq, gk, dy)


def _merge_specs(S, D, tm, tn, ks):
    off = GATE_OFF // tn
    nd = D // tn
    gates = [pl.BlockSpec((tm, tn), functools.partial(lambda b, m, n: (m, off + b * nd + n), b)) for b in range(3)]
    ys = [pl.BlockSpec((tm, k), lambda m, n: (m, 0)) for k in ks]
    ws = [pl.BlockSpec((None, k, tn), lambda m, n: (n, 0, 0)) for k in ks]
    return gates, ys, ws


def _merge_fwd(proj, ys, ws):
    S = proj.shape[0]
    tn = ws[0].shape[2]
    D = N_DEV * tn
    ks = [w.shape[1] for w in ws]
    tm = _pick(S, (1024,))
    gates, y_specs, w_specs = _merge_specs(S, D, tm, tn, ks)

    def body(ga_ref, gb_ref, gc_ref, ya_ref, yb_ref, yc_ref, wa_ref, wb_ref, wc_ref, m_ref, za_ref, zb_ref, zc_ref):
        acc = None
        for g_ref, y_ref, w_ref, z_ref in ((ga_ref, ya_ref, wa_ref, za_ref), (gb_ref, yb_ref, wb_ref, zb_ref),
                                           (gc_ref, yc_ref, wc_ref, zc_ref)):
            z = jnp.dot(y_ref[...], w_ref[...], preferred_element_type=F32)
            z_ref[...] = z.astype(BF)
            t = jax.nn.sigmoid(g_ref[...]) * z
            acc = t if acc is None else acc + t
        m_ref[...] = acc.astype(BF)

    tile = pl.BlockSpec((tm, tn), lambda m, n: (m, n))
    return pl.pallas_call(
        body, name="merge_fwd", grid=(S // tm, D // tn), in_specs=gates + y_specs + w_specs,
        out_specs=[tile, tile, tile, tile], out_shape=[jax.ShapeDtypeStruct((S, D), BF)] * 4,
        compiler_params=_params(("parallel", "parallel")),
    )(proj, proj, proj, *ys, *ws)


def _merge_bwd(proj, zs, dm, ws, after=()):
    S = proj.shape[0]
    tn = ws[0].shape[2]
    D = N_DEV * tn
    ks = [w.shape[1] for w in ws]
    tm = _pick(S, (512,))
    gates, _, w_specs = _merge_specs(S, D, tm, tn, ks)
    nt = (((1,), (1,)), ((), ()))

    def body(ga_ref, gb_ref, gc_ref, za_ref, zb_ref, zc_ref, dm_ref, wa_ref, wb_ref, wc_ref,
             dza_ref, dzb_ref, dzc_ref, dga_ref, dgb_ref, dgc_ref, dya_ref, dyb_ref, dyc_ref):
        n = pl.program_id(1)
        dmv = dm_ref[...]
        for g_ref, z_ref, w_ref, dz_ref, dg_ref, dy_ref in (
                (ga_ref, za_ref, wa_ref, dza_ref, dga_ref, dya_ref), (gb_ref, zb_ref, wb_ref, dzb_ref, dgb_ref, dyb_ref),
                (gc_ref, zc_ref, wc_ref, dzc_ref, dgc_ref, dyc_ref)):
            sg = jax.nn.sigmoid(g_ref[...])
            dz = (sg * dmv).astype(BF)
            dz_ref[...] = dz
            dg_ref[...] = (dmv * z_ref[...].astype(F32) * sg * (1.0 - sg)).astype(BF)
            part = lax.dot_general(dz, w_ref[...], nt, preferred_element_type=F32)

            @pl.when(n == 0)
            def _():
                dy_ref[...] = part

            @pl.when(n > 0)
            def _():
                dy_ref[...] += part

    tile = pl.BlockSpec((tm, tn), lambda m, n: (m, n))
    dys = [pl.BlockSpec((tm, k), lambda m, n: (m, 0)) for k in ks]
    return pl.pallas_call(
        _hide(body, 10, len(after)), name="merge_bwd", grid=(S // tm, D // tn),
        in_specs=gates + [tile, tile, tile, tile] + w_specs + _hidden_specs(after),
        out_specs=[tile] * 6 + dys,
        out_shape=[jax.ShapeDtypeStruct((S, D), BF)] * 6 + [jax.ShapeDtypeStruct((S, k), F32) for k in ks],
        compiler_params=_params(("parallel", "arbitrary")),
    )(proj, proj, proj, *zs, dm, *ws, *after)


PAD = 8


def _stage_shift_down(us_ref, u_ref):
    S = u_ref.shape[1]
    us_ref[:, 0:PAD, :] = jnp.zeros((2, PAD, us_ref.shape[2]), F32)
    us_ref[:, PAD:S + PAD, :] = u_ref[...].astype(F32)


ROWS = 32


def _conv3(us_ref, part, r0, w, b):
    return (us_ref[part, pl.ds(r0 + PAD, ROWS), :] * w[2:3] + us_ref[part, pl.ds(r0 + PAD - 1, ROWS), :] * w[1:2]
            + us_ref[part, pl.ds(r0 + PAD - 2, ROWS), :] * w[0:1] + b)


def _ffn_specs(S, F, tc, c):
    per = c // tc

    def w_spec(half):
        return pl.BlockSpec((None, 3, tc), lambda j: (half * (N_DEV // 2) + j // per, 0, j % per))

    return [pl.BlockSpec((2, S, tc), lambda j: (0, 0, j)), w_spec(0), w_spec(1), pl.BlockSpec((2, 1, tc), lambda j: (0, 0, j))]


def _ffn_tile(F, c):
    tc = 128
    if c % tc or F % tc:
        raise ValueError(f"ffn tile {tc} does not divide {c}, {F}")
    return tc


def _ffn_act_fwd(up3, cws, cb3):
    _, S, F = up3.shape
    c = cws.shape[2]
    tc = _ffn_tile(F, c)

    def body(u_ref, wa_ref, wb_ref, b_ref, o_ref, us_ref):
        _stage_shift_down(us_ref, u_ref)
        wa, wb, ba, bb = wa_ref[...], wb_ref[...], b_ref[0], b_ref[1]

        def step(i, carry):
            r0 = pl.multiple_of(i * ROWS, ROWS)
            ca = _conv3(us_ref, 0, r0, wa, ba)
            cb = _conv3(us_ref, 1, r0, wb, bb)
            o_ref[pl.ds(r0, ROWS), :] = (ca * jax.nn.sigmoid(ca) * cb).astype(BF)
            return carry

        lax.fori_loop(0, S // ROWS, step, 0, unroll=4)

    return pl.pallas_call(
        body, name="ffn_act_fwd", grid=(F // tc,), in_specs=_ffn_specs(S, F, tc, c),
        out_specs=pl.BlockSpec((S, tc), lambda j: (0, j)), out_shape=jax.ShapeDtypeStruct((S, F), BF),
        scratch_shapes=[pltpu.VMEM((2, S + PAD, tc), F32)],
        compiler_params=_params(("parallel",)),
    )(up3, cws, cws, cb3)


def _ffn_act_bwd(up3, cws, cb3, dact, after=()):
    _, S, F = up3.shape
    c = cws.shape[2]
    tc = _ffn_tile(F, c)

    def body(u_ref, wa_ref, wb_ref, b_ref, da_ref, du_ref, dw_ref, db_ref, us_ref, dcs_ref):
        _stage_shift_down(us_ref, u_ref)
        ws = (wa_ref[...], wb_ref[...])
        ba, bb = b_ref[0], b_ref[1]
        dcs_ref[:, S:S + PAD, :] = jnp.zeros((2, PAD, tc), F32)

        def conv_grads(i, carry):
            r0 = pl.multiple_of(i * ROWS, ROWS)
            ca = _conv3(us_ref, 0, r0, ws[0], ba)
            cb = _conv3(us_ref, 1, r0, ws[1], bb)
            sg = jax.nn.sigmoid(ca)
            dav = da_ref[pl.ds(r0, ROWS), :].astype(F32)
            dcs_ref[0, pl.ds(r0, ROWS), :] = dav * cb * sg * (1.0 + ca * (1.0 - sg))
            dcs_ref[1, pl.ds(r0, ROWS), :] = dav * ca * sg
            return carry

        lax.fori_loop(0, S // ROWS, conv_grads, 0, unroll=4)

        def fold(v):
            return jnp.sum(v.reshape(ROWS // 8, 8, tc), axis=0)

        def input_grads(i, acc):
            r0 = pl.multiple_of(i * ROWS, ROWS)
            new = []
            for part in range(2):
                w = ws[part]
                dc = dcs_ref[part, pl.ds(r0, ROWS), :]
                dc1 = dcs_ref[part, pl.ds(r0 + 1, ROWS), :]
                dc2 = dcs_ref[part, pl.ds(r0 + 2, ROWS), :]
                u = us_ref[part, pl.ds(r0 + PAD, ROWS), :]
                du_ref[part, pl.ds(r0, ROWS), :] = (dc * w[2:3] + dc1 * w[1:2] + dc2 * w[0:1]).astype(BF)
                sums = (fold(dc2 * u), fold(dc1 * u), fold(dc * u), fold(dc))
                new += [a + s for a, s in zip(acc[4 * part:4 * part + 4], sums)]
            return tuple(new)

        acc = lax.fori_loop(0, S // ROWS, input_grads, tuple(jnp.zeros((8, tc), F32) for _ in range(8)), unroll=4)
        for part in range(2):
            for j in range(3):
                dw_ref[part, j:j + 1, :] = jnp.sum(acc[4 * part + j], axis=0, keepdims=True)
            db_ref[part] = jnp.sum(acc[4 * part + 3], axis=0, keepdims=True)

    return pl.pallas_call(
        _hide(body, 5, len(after)), name="ffn_act_bwd", grid=(F // tc,),
        in_specs=_ffn_specs(S, F, tc, c) + [pl.BlockSpec((S, tc), lambda j: (0, j))] + _hidden_specs(after),
        out_specs=[pl.BlockSpec((2, S, tc), lambda j: (0, 0, j)), pl.BlockSpec((2, 3, tc), lambda j: (0, 0, j)),
                   pl.BlockSpec((2, 1, tc), lambda j: (0, 0, j))],
        out_shape=[jax.ShapeDtypeStruct((2, S, F), BF), jax.ShapeDtypeStruct((2, 3, F), F32), jax.ShapeDtypeStruct((2, 1, F), F32)],
        scratch_shapes=[pltpu.VMEM((2, S + PAD, tc), F32), pltpu.VMEM((2, S + PAD, tc), F32)],
        compiler_params=_params(("parallel",)),
    )(up3, cws, cws, cb3, dact, *after)


def _loss(y, target):
    S, D = y.shape
    tr = _pick(S, (256,))

    def body(y_ref, t_ref, dy_ref, dyb_ref, l_ref):
        i = pl.program_id(0)
        e = y_ref[...] - t_ref[...]
        dy = e * (1.0 / D)
        dy_ref[...] = dy
        dyb_ref[...] = dy.astype(BF)
        part = jnp.sum(jnp.sum(e * e, axis=-1, keepdims=True), axis=0, keepdims=True) * (0.5 / D)

        @pl.when(i == 0)
        def _():
            l_ref[...] = jnp.zeros_like(l_ref)

        l_ref[...] += part

    row = pl.BlockSpec((tr, D), lambda i: (i, 0))
    return pl.pallas_call(
        body, name="loss", grid=(S // tr,), in_specs=[row, row],
        out_specs=[row, row, pl.BlockSpec((8, 128), lambda i: (0, 0))],
        out_shape=[jax.ShapeDtypeStruct((S, D), F32), jax.ShapeDtypeStruct((S, D), BF), jax.ShapeDtypeStruct((8, 128), F32)],
        compiler_params=_params(("arbitrary",)),
    )(y, target)


ANY = pl.BlockSpec(memory_space=pl.ANY)


def _allgather(shards, name):
    n = len(shards)

    def body(*refs):
        ins, outs = refs[:n], refs[n:2 * n]
        send_sems, recv_sems, local_sems = refs[2 * n:]
        x, y, c = lax.axis_index("x"), lax.axis_index("y"), lax.axis_index("c")
        me, sibling = (x, y, c), (x, y, 1 - c)
        chips = [(1 - x, y), (x, 1 - y), (1 - x, 1 - y)]

        def blk(w, px, py, pc):
            return outs[w].at[4 * px + 2 * py + pc]

        def copy(w, k, block, to, src=None):
            return pltpu.make_async_remote_copy(
                src_ref=blk(w, *block) if src is None else src, dst_ref=blk(w, *block),
                send_sem=send_sems.at[w, k], recv_sem=recv_sems.at[w, k], device_id=to, device_id_type=MESH)

        started = []
        mine = []
        for w in range(n):
            mine.append(pltpu.make_async_copy(ins[w], blk(w, *me), local_sems.at[w]))
            mine[-1].start()
            first = [copy(w, 0, me, sibling, src=ins[w])]
            first += [copy(w, 1 + j, me, (*chip, c), src=ins[w]) for j, chip in enumerate(chips)]
            for cp in first:
                cp.start()
            started += first
        for w in range(n):
            for j, chip in enumerate(chips):
                copy(w, 1 + j, (*chip, c), me).wait_recv()
                fwd = copy(w, 4 + j, (*chip, c), sibling)
                fwd.start()
                started.append(fwd)
        for w in range(n):
            copy(w, 0, sibling, me).wait_recv()
            for j, chip in enumerate(chips):
                copy(w, 4 + j, (*chip, 1 - c), me).wait_recv()
        for cp in started:
            cp.wait_send()
        for cp in mine:
            cp.wait()

    whole = pl.BlockSpec(memory_space=pltpu.VMEM)
    outs = pl.pallas_call(
        body, name=name, in_specs=[whole] * n, out_specs=[whole] * n,
        out_shape=[jax.ShapeDtypeStruct((N_DEV,) + s.shape, s.dtype) for s in shards],
        scratch_shapes=[pltpu.SemaphoreType.DMA((n, 7)), pltpu.SemaphoreType.DMA((n, 7)), pltpu.SemaphoreType.DMA((n,))],
    )(*shards)
    return list(outs)


def _allgather_seq(shards, name, collective_id, after=()):
    n = len(shards)
    n_after = len(after)

    halves = [s.shape[0] % 32 == 0 for s in shards]
    n_sem = 8
    to_diagonal = not all(halves)

    def body(*refs):
        ins, outs = refs[:n], refs[n + n_after:2 * n + n_after]
        send_sems, recv_sems, local_sems = refs[2 * n + n_after:]
        x, y, c = lax.axis_index("x"), lax.axis_index("y"), lax.axis_index("c")
        me, sibling = (x, y, c), (x, y, 1 - c)
        x_nb, y_nb, diag = (1 - x, y, c), (x, 1 - y, c), (1 - x, 1 - y, c)
        peers = [sibling, x_nb, y_nb] + ([diag] if to_diagonal else [])
        barrier = pltpu.get_barrier_semaphore()
        for peer in peers:
            pl.semaphore_signal(barrier, inc=1, device_id=peer, device_id_type=MESH)
        pl.semaphore_wait(barrier, len(peers))

        def blk(w, dev, rows=None):
            ref = outs[w].at[4 * dev[0] + 2 * dev[1] + dev[2]]
            return ref if rows is None else ref.at[rows]

        def copy(w, k, block, to, src=None, rows=None):
            return pltpu.make_async_remote_copy(
                src_ref=blk(w, block, rows) if src is None else src, dst_ref=blk(w, block, rows),
                send_sem=send_sems.at[n_sem * w + k], recv_sem=recv_sems.at[n_sem * w + k], device_id=to, device_id_type=MESH)

        def top(w):
            return pl.ds(0, shards[w].shape[0] // 2)

        def bottom(w):
            return pl.ds(shards[w].shape[0] // 2, shards[w].shape[0] // 2)

        started = []
        mine = []
        for w in range(n):
            mine.append(pltpu.make_async_copy(ins[w], blk(w, me), local_sems.at[w]))
            mine[-1].start()
            first = [copy(w, 0, me, sibling, src=ins[w]), copy(w, 1, me, x_nb, src=ins[w]), copy(w, 2, me, y_nb, src=ins[w])]
            if not halves[w]:
                first.append(copy(w, 3, me, diag, src=ins[w]))
            for cp in first:
                cp.start()
            started += first
        for w in range(n):
            copy(w, 1, x_nb, me).wait_recv()
            onward = [copy(w, 5, x_nb, sibling)] + ([copy(w, 3, x_nb, y_nb, rows=top(w))] if halves[w] else [])
            copy(w, 2, y_nb, me).wait_recv()
            onward += [copy(w, 6, y_nb, sibling)] + ([copy(w, 4, y_nb, x_nb, rows=bottom(w))] if halves[w] else [])
            for cp in onward:
                cp.start()
            started += onward
        for w in range(n):
            if halves[w]:
                copy(w, 3, diag, me, rows=top(w)).wait_recv()
                copy(w, 4, diag, me, rows=bottom(w)).wait_recv()
            else:
                copy(w, 3, diag, me).wait_recv()
            fwd = copy(w, 7, diag, sibling)
            fwd.start()
            started.append(fwd)
        for w in range(n):
            for k, dev in ((0, sibling), (5, (1 - x, y, 1 - c)), (6, (x, 1 - y, 1 - c)), (7, (1 - x, 1 - y, 1 - c))):
                copy(w, k, dev, me).wait_recv()
        for cp in started:
            cp.wait_send()
        for cp in mine:
            cp.wait()

    outs = pl.kernel(
        body, name=name, out_type=[jax.ShapeDtypeStruct((N_DEV,) + s.shape, s.dtype) for s in shards],
        mesh=plsc.ScalarSubcoreMesh(axis_name="seq", num_cores=1),
        scratch_types=[pltpu.SemaphoreType.DMA((n_sem * n,)), pltpu.SemaphoreType.DMA((n_sem * n,)), pltpu.SemaphoreType.DMA((n,))],
        compiler_params=pltpu.CompilerParams(collective_id=collective_id),
    )(*shards, *after)
    return list(outs)


def _chip_exchange(sums, name, collective_id):
    n = len(sums)

    def body(*refs):
        ins, outs = refs[:n], refs[n:2 * n]
        send_sems, recv_sems = refs[2 * n:]
        x, y, c = lax.axis_index("x"), lax.axis_index("y"), lax.axis_index("c")
        chips = [(1 - x, y), (x, 1 - y), (1 - x, 1 - y)]
        barrier = pltpu.get_barrier_semaphore()
        for px, py in chips:
            pl.semaphore_signal(barrier, inc=1, device_id=(px, py, c), device_id_type=MESH)
        pl.semaphore_wait(barrier, 3)
        copies = []
        for w in range(n):
            for k, (px, py) in enumerate(chips):
                copies.append(pltpu.make_async_remote_copy(
                    src_ref=ins[w].at[2 * px + py], dst_ref=outs[w].at[k], send_sem=send_sems.at[3 * w + k],
                    recv_sem=recv_sems.at[3 * w + k], device_id=(px, py, c), device_id_type=MESH))
        for cp in copies:
            cp.start()
        for cp in copies:
            cp.wait()

    outs = pl.kernel(
        body, name=name, out_type=[jax.ShapeDtypeStruct((3,) + s.shape[1:], s.dtype) for s in sums],
        mesh=plsc.ScalarSubcoreMesh(axis_name="seq", num_cores=1),
        scratch_types=[pltpu.SemaphoreType.DMA((3 * n,)), pltpu.SemaphoreType.DMA((3 * n,))],
        compiler_params=pltpu.CompilerParams(collective_id=collective_id),
    )(*sums)
    return list(outs)


def _row_tile(r, c, elems=256 * 1024):
    want = max(8, elems // c)
    for t in range(min(want, r) // 8 * 8, 0, -8):
        if r % t == 0:
            return t
    return r


def _pair_add(g4, recv, core, name, after=()):
    _, _, r, c = g4.shape
    tr = _row_tile(r, c, 1024 * 1024)

    def body(core_ref, a_ref, b_ref, o_ref):
        o_ref[...] = (a_ref[...].astype(F32) + b_ref[...].astype(F32)).astype(BF)

    return pl.pallas_call(
        _hide(body, 3, len(after)), name=name,
        grid_spec=pltpu.PrefetchScalarGridSpec(
            num_scalar_prefetch=1, grid=(4, r // tr),
            in_specs=[pl.BlockSpec((None, None, tr, c), lambda p, i, s: (p, s[0], i, 0)),
                      pl.BlockSpec((None, tr, c), lambda p, i, s: (p, i, 0))] + _hidden_specs(after),
            out_specs=pl.BlockSpec((None, tr, c), lambda p, i, s: (p, i, 0))),
        out_shape=jax.ShapeDtypeStruct((4, r, c), BF), compiler_params=_params(("parallel", "parallel")),
    )(core, g4, recv, *after)


def _adam_math(w, g, m, v):
    m = ADAM_B1 * m + (1.0 - ADAM_B1) * g
    v = ADAM_B2 * v + (1.0 - ADAM_B2) * (g * g)
    m_hat = m / (1.0 - ADAM_B1 ** ADAM_STEP)
    v_hat = v / (1.0 - ADAM_B2 ** ADAM_STEP)
    delta = -ADAM_LR * (m_hat / (jnp.sqrt(v_hat) + ADAM_EPS) + ADAM_WD * w)
    return delta, m, v


def _adamw_big(sums, recv, chip, w, m, v, name, after=()):
    r, c = w.shape
    tr = _row_tile(r, c, 512 * 1024)

    def body(chip_ref, s_ref, r_ref, w_ref, m_ref, v_ref, g_out, d_out, m_out, v_out):
        g = s_ref[...].astype(F32) + r_ref[0].astype(F32)
        g = g + r_ref[1].astype(F32)
        g = g + r_ref[2].astype(F32)
        delta, mn, vn = _adam_math(w_ref[...], g, m_ref[...], v_ref[...])
        g_out[...] = g
        d_out[...] = delta
        m_out[...] = mn
        v_out[...] = vn

    row = pl.BlockSpec((tr, c), lambda i, s: (i, 0))
    return pl.pallas_call(
        _hide(body, 6, len(after)), name=name,
        grid_spec=pltpu.PrefetchScalarGridSpec(
            num_scalar_prefetch=1, grid=(r // tr,),
            in_specs=[pl.BlockSpec((None, tr, c), lambda i, s: (s[0], i, 0)), pl.BlockSpec((3, tr, c), lambda i, s: (0, i, 0)),
                      row, row, row] + _hidden_specs(after),
            out_specs=[row, row, row, row]),
        out_shape=[jax.ShapeDtypeStruct((r, c), F32)] * 4, compiler_params=_params(("parallel",)),
    )(chip, sums, recv, w, m, v, *after)


def _adamw_small(parts, ws, ms, vs, extra_parts, name):
    n, ne = len(ws), len(extra_parts)

    def total(p_ref):
        g = p_ref[0]
        for d in range(1, N_DEV):
            g = g + p_ref[d]
        return g

    def body(*refs):
        p_refs, w_refs, m_refs, v_refs = refs[:n], refs[n:2 * n], refs[2 * n:3 * n], refs[3 * n:4 * n]
        e_refs = refs[4 * n:4 * n + ne]
        outs = refs[4 * n + ne:]
        for i in range(n):
            g = total(p_refs[i])
            delta, mn, vn = _adam_math(w_refs[i][...], g, m_refs[i][...], v_refs[i][...])
            outs[4 * i][...] = g
            outs[4 * i + 1][...] = delta
            outs[4 * i + 2][...] = mn
            outs[4 * i + 3][...] = vn
        for i in range(ne):
            outs[4 * n + i][...] = total(e_refs[i])

    out_shape = []
    for w in ws:
        out_shape += [jax.ShapeDtypeStruct(w.shape, F32)] * 4
    out_shape += [jax.ShapeDtypeStruct(e.shape[1:], F32) for e in extra_parts]
    res = pl.pallas_call(body, name=name, out_shape=out_shape,
                         compiler_params=pltpu.CompilerParams(vmem_limit_bytes=VMEM_LIMIT))(*parts, *ws, *ms, *vs, *extra_parts)
    return [res[4 * i:4 * i + 4] for i in range(n)], list(res[4 * n:])


def _adamw_plain(g, w, m, v, name):
    def body(g_ref, w_ref, m_ref, v_ref, d_out, m_out, v_out):
        delta, mn, vn = _adam_math(w_ref[...], g_ref[...], m_ref[...], v_ref[...])
        d_out[...] = delta
        m_out[...] = mn
        v_out[...] = vn

    return pl.pallas_call(body, name=name, out_shape=[jax.ShapeDtypeStruct(w.shape, F32)] * 3)(g, w, m, v)


def kernel(x, mem, positions, g_mix, w_in, g_a_v, w_spatial, b_spatial, g_b_q, g_b_k, sinks, g_mem, w_mem_kv, g_c_q, g_c_k, w_branch_a, w_branch_b, w_branch_c, w_out, g_ffn, w_up, conv_w, conv_b, w_down, loss_target, m_g_mix, m_w_in, m_g_a_v, m_w_spatial, m_b_spatial, m_g_b_q, m_g_b_k, m_sinks, m_g_mem, m_w_mem_kv, m_g_c_q, m_g_c_k, m_w_branch_a, m_w_branch_b, m_w_branch_c, m_w_out, m_g_ffn, m_w_up, m_conv_w, m_conv_b, m_w_down, v_g_mix, v_w_in, v_g_a_v, v_w_spatial, v_b_spatial, v_g_b_q, v_g_b_k, v_sinks, v_g_mem, v_w_mem_kv, v_g_c_q, v_g_c_k, v_w_branch_a, v_w_branch_b, v_w_branch_c, v_w_out, v_g_ffn, v_w_up, v_conv_w, v_conv_b, v_w_down):
    S, D = x.shape[1], x.shape[2]
    M = mem.shape[1]
    F = w_down.shape[1] * N_DEV
    in_cols = w_in.shape[2] * N_DEV
    ax, ay, ac = lax.axis_index("x"), lax.axis_index("y"), lax.axis_index("c")
    core = jnp.reshape(ac, (1,)).astype(jnp.int32)
    chip = jnp.reshape(2 * ax + ay, (1,)).astype(jnp.int32)
    me = 4 * ax + 2 * ay + ac

    x2, mem2, tgt2 = x[0], mem[0], loss_target[0]

    big = dict(w_in=w_in[0].T, w_mem_kv=w_mem_kv[0], w_branch_a=w_branch_a[0], w_branch_b=w_branch_b[0],
               w_branch_c=w_branch_c[0], w_out=w_out[0], w_up=w_up[0], w_down=w_down[0])
    names = list(big)
    cast = {k: big[k].astype(BF) for k in names}
    W = {}
    cb3 = conv_b.reshape(2, 1, F)
    W["w_in"], = _allgather_seq([cast["w_in"]], "ag_seq0", 0)
    w_in_t = W["w_in"].reshape(in_cols, D)
    grp1 = ["w_mem_kv", "w_branch_a", "w_branch_b", "w_branch_c", "w_out"]
    res1 = _allgather_seq([cast[k] for k in grp1] + [conv_w[0]], "ag_seq1", 1, after=(_token((w_in_t,), "tok_w_in"),))
    W.update(zip(grp1, res1))
    cw3 = res1[-1]
    w_kv_f = W["w_mem_kv"].reshape(D, 2 * C_WIDTH)
    w_out_f = W["w_out"].reshape(D, D)

    half = ROPE_DIM // 2
    inv = ROPE_THETA ** (-jnp.arange(half, dtype=F32) / half)
    ang = positions[0].astype(F32)[:, None] * inv
    cos, sin = jnp.cos(ang), jnp.sin(ang)
    one, zero = jnp.ones((S, B_HEAD_DIM - ROPE_DIM), F32), jnp.zeros((S, B_HEAD_DIM - ROPE_DIM), F32)
    z8 = jnp.zeros((S, half), F32)
    ct = jnp.tile(jnp.concatenate([cos, cos, one], axis=1), (1, 2))
    sa = jnp.tile(jnp.concatenate([-sin, z8, zero], axis=1), (1, 2))
    sb = jnp.tile(jnp.concatenate([z8, sin, zero], axis=1), (1, 2))
    gq2, gk2 = jnp.tile(g_b_q, (1, 2)), jnp.tile(g_b_k, (1, 2))
    b_t = b_spatial[0].T

    h, rstd1 = _rms_fwd(x2, g_mix, "rms1_fwd")
    proj = _mm(h, w_in_t, "nt", F32, "mm_proj", tn=1280)
    y_a = _a_fwd(proj, g_a_v, w_spatial[0], b_t)
    qn, kn = _b_pre(proj, gq2, gk2, ct, sa, sb)
    W["w_up"], = _allgather_seq([cast["w_up"]], "ag_seq2", 2, after=(_token((W["w_out"], qn), "tok_group1"),))
    y_b = _b_attn_fwd(qn, kn, proj, sinks)
    mem_h, rstd_m = _rms_fwd(mem2, g_mem, "rmsmem_fwd")
    kv = _mm(mem_h, w_kv_f, "nn", F32, "mm_kv", after=(y_b,))
    y_c = _c_fwd(proj, kv, g_c_q, g_c_k)
    w_branches = [W["w_branch_a"], W["w_branch_b"], W["w_branch_c"]]
    merged, z_a, z_b, z_c = _merge_fwd(proj, [y_a, y_b, y_c], w_branches)
    x1 = _mm(merged, w_out_f, "nn", F32, "mm_x1", resid=x2)
    h2, rstd2 = _rms_fwd(x1, g_ffn, "rms2_fwd")
    W["w_down"], = _allgather_seq([cast["w_down"]], "ag_seq3", 3, after=(W["w_up"], h2))
    w_down_f = W["w_down"].reshape(F, D)
    up3 = _mm(h2, W["w_up"], "nn", BF, "mm_up", b_stack=True, out_parts=2)
    act = _ffn_act_fwd(up3, cw3, cb3)
    y = _mm(act, w_down_f, "nn", F32, "mm_y", resid=x1, tk=1408)
    dy, dy_b, loss_acc = _loss(y, tgt2)
    loss = lax.psum(loss_acc[0, 0], ("x", "y", "c"))

    reduced = {}

    def as4(g):
        return g.reshape(4, 2, g.shape[1], g.shape[2])

    def finish_group(gi, keys, g4, from_sibling):
        sums = [_pair_add(a, b, core, "rs_add_" + k) for k, a, b in zip(keys, g4, from_sibling)]
        from_chips = _chip_exchange(sums, f"rs_chip{gi}", 4 + gi)
        reduced.update(zip(keys, zip(sums, from_chips)))
        return tuple(sums)

    d_act = _mm(dy_b, w_down_f, "nt", BF, "mm_dact", tn=1408)
    g_down = _mm(act, dy_b, "tn", BF, "mm_gdown", tm=1408)
    d_up3, d_cw3, d_cb3 = _ffn_act_bwd(up3, cw3, cb3, d_act, after=(g_down,))
    grp0 = [as4(g_down.reshape(N_DEV, F // N_DEV, D))]
    g_up, sib0 = _mm(h2, d_up3, "tn", BF, "mm_gup", b_parts=2, out_stack=True, exchange=grp0)
    sums0 = finish_group(0, ["w_down"], grp0, sib0)
    grp1 = [as4(g_up)]
    d_h2, sib1 = _mm(d_up3, W["w_up"], "nt", F32, "mm_dh2", a_parts=2, b_stack=True, tm=2048, after=sums0, exchange=grp1)
    sums1 = finish_group(1, ["w_up"], grp1, sib1)
    dx1, dx1_b, d_g_ffn = _rms_bwd(x1, rstd2, g_ffn, d_h2, dy, "rms2_bwd", after=sums1)
    g_out = _mm(merged, dx1_b, "tn", BF, "mm_gout")
    grp2 = [as4(g_out.reshape(N_DEV, D // N_DEV, D))]
    d_merged, sib2 = _mm(dx1_b, w_out_f, "nt", F32, "mm_dmerged", exchange=grp2)
    sums2 = finish_group(2, ["w_out"], grp2, sib2)
    dz_a, dz_b, dz_c, dga, dgb, dgc, dy_a, dy_b_, dy_c = _merge_bwd(proj, [z_a, z_b, z_c], d_merged, w_branches, after=sums2)
    g_ba = _mm(y_a, dz_a, "tn", BF, "mm_gba", out_stack=True)
    g_bb = _mm(y_b, dz_b, "tn", BF, "mm_gbb", out_stack=True)
    g_bc = _mm(y_c, dz_c, "tn", BF, "mm_gbc", out_stack=True)
    d_uv, d_g_a_v, d_w_s, d_b_t = _a_bwd(proj, g_a_v, w_spatial[0], b_t, dy_a, after=(g_ba, g_bb, g_bc))
    dqn, dkn, dv_b, dsink_rows = _b_attn_bwd(qn, kn, proj, sinks, dy_b_)
    d_qkv, d_gq2, d_gk2 = _b_pre_bwd(proj, gq2, gk2, ct, sa, sb, dqn, dkn, dv_b)
    dq_c, dk_c, dv_c, d_gcq, d_gck = _c_bwd(proj, kv, g_c_q, g_c_k, dy_c)
    dkv_b = jnp.concatenate([dk_c, dv_c], axis=1).astype(BF)
    d_memh = _mm(dkv_b, w_kv_f, "nt", F32, "mm_dmemh")
    g_kv = _mm(mem_h, dkv_b, "tn", BF, "mm_gkv")
    _, _, d_g_mem = _rms_bwd(mem2, rstd_m, g_mem, d_memh, None, "rmsmem_bwd")
    dproj = jnp.concatenate([d_uv, d_qkv, dq_c, dga, dgb, dgc], axis=1)
    grp3 = [as4(g_ba), as4(g_bb), as4(g_bc)]
    g_in, sib3 = _mm(dproj, h, "tn", BF, "mm_gin", tm=1280, exchange=grp3)
    sums3 = finish_group(3, ["w_branch_a", "w_branch_b", "w_branch_c"], grp3, sib3)
    grp4 = [as4(g_in.reshape(N_DEV, in_cols // N_DEV, D)), as4(g_kv.reshape(N_DEV, D // N_DEV, 2 * C_WIDTH))]
    d_h_half, sib4 = _mm(dproj, w_in_t, "nn", F32, "mm_dh_a", tm=2048, tn=D // 2, tk=1280, after=sums3, exchange=grp4, cols=(0, 2))
    sums4 = finish_group(4, ["w_in", "w_mem_kv"], grp4, sib4)
    d_h = _mm(dproj, w_in_t, "nn", F32, "mm_dh_b", tm=2048, tn=D // 2, tk=1280, after=sums4, cols=(1, 2), into=d_h_half)
    grad_x, _, d_g_mix = _rms_bwd(x2, rstd1, g_mix, d_h, dx1, "rms1_bwd", after=sums4)

    small_names =["g_mix", "g_a_v", "w_spatial", "b_spatial", "g_b_q", "g_b_k", "sinks", "g_mem", "g_c_q", "g_c_k", "g_ffn", "conv_b"]
    small_w = dict(g_mix=g_mix, g_a_v=g_a_v, w_spatial=w_spatial, b_spatial=b_spatial, g_b_q=g_b_q, g_b_k=g_b_k, sinks=sinks,
                   g_mem=g_mem, g_c_q=g_c_q, g_c_k=g_c_k, g_ffn=g_ffn, conv_b=conv_b)
    small_m = dict(g_mix=m_g_mix, g_a_v=m_g_a_v, w_spatial=m_w_spatial, b_spatial=m_b_spatial, g_b_q=m_g_b_q, g_b_k=m_g_b_k,
                   sinks=m_sinks, g_mem=m_g_mem, g_c_q=m_g_c_q, g_c_k=m_g_c_k, g_ffn=m_g_ffn, conv_b=m_conv_b)
    small_v = dict(g_mix=v_g_mix, g_a_v=v_g_a_v, w_spatial=v_w_spatial, b_spatial=v_b_spatial, g_b_q=v_g_b_q, g_b_k=v_g_b_k,
                   sinks=v_sinks, g_mem=v_g_mem, g_c_q=v_g_c_q, g_c_k=v_g_c_k, g_ffn=v_g_ffn, conv_b=v_conv_b)
    small_g = dict(
        g_mix=d_g_mix, g_a_v=d_g_a_v, w_spatial=d_w_s, b_spatial=d_b_t.T,
        g_b_q=d_gq2.reshape(2, B_HEAD_DIM).sum(0), g_b_k=d_gk2.reshape(2, B_HEAD_DIM).sum(0),
        sinks=dsink_rows.sum(0)[:B_HEADS], g_mem=d_g_mem, g_c_q=d_gcq.sum(0), g_c_k=d_gck.sum(0), g_ffn=d_g_ffn,
        conv_b=d_cb3)
    partial = [small_g[k].reshape(small_w[k].shape) for k in small_names] + [d_cw3]
    parts = _allgather(partial, "ag_small")
    small_res, (g_cw3,) = _adamw_small(parts[:-1], [small_w[k] for k in small_names], [small_m[k] for k in small_names],
                                       [small_v[k] for k in small_names], parts[-1:], "adamw_small")
    small_out = dict(zip(small_names, small_res))
    c_cw = 2 * F // N_DEV
    g_cw = lax.dynamic_slice(g_cw3, (me // (N_DEV // 2), 0, (me % (N_DEV // 2)) * c_cw), (1, 3, c_cw))[0]
    cw_res = _adamw_plain(g_cw, conv_w[0], m_conv_w[0], v_conv_w[0], "adamw_conv_w")
    big_out = {"conv_w": [g_cw[None]] + [a[None] for a in cw_res]}

    moments = dict(w_in=(m_w_in, v_w_in), w_mem_kv=(m_w_mem_kv, v_w_mem_kv), w_branch_a=(m_w_branch_a, v_w_branch_a),
                   w_branch_b=(m_w_branch_b, v_w_branch_b), w_branch_c=(m_w_branch_c, v_w_branch_c), w_out=(m_w_out, v_w_out),
                   w_up=(m_w_up, v_w_up), w_down=(m_w_down, v_w_down))
    token = (grad_x, small_res[0][0])
    for k in ["w_down", "w_up", "w_out", "w_branch_a", "w_branch_b", "w_branch_c", "w_mem_kv", "w_in"]:
        s, r = reduced[k]
        mk, vk = moments[k][0][0], moments[k][1][0]
        if k == "w_in":
            res = _adamw_big(s, r, chip, big[k], mk.T, vk.T, "adamw_" + k, after=token)
            big_out[k] = [a.T[None] for a in res]
        else:
            res = _adamw_big(s, r, chip, big[k], mk, vk, "adamw_" + k, after=token)
            big_out[k] = [a[None] for a in res]
        token = (res[0],)

    order = ["g_mix", "w_in", "g_a_v", "w_spatial", "b_spatial", "g_b_q", "g_b_k", "sinks", "g_mem", "w_mem_kv", "g_c_q", "g_c_k",
             "w_branch_a", "w_branch_b", "w_branch_c", "w_out", "g_ffn", "w_up", "conv_w", "conv_b", "w_down"]
    res = {**small_out, **big_out}
    outs = [loss, grad_x[None]]
    for field in range(4):
        outs += [res[k][field] for k in order]
    return tuple(outs)
```

```python
import functools

import jax
import jax.numpy as jnp
from jax import lax
from jax.experimental import pallas as pl
from jax.experimental.pallas import tpu as pltpu
from jax.experimental.pallas import tpu_sc as plsc

F32 = jnp.float32
BF = jnp.bfloat16
EPS = 1e-6
NEG = -1e30

N_DEV = 8
CHUNK = 128
A_GROUPS = 4
A_WIDTH = 512
B_HEADS = 16
B_KV_HEADS = 2
B_HEAD_DIM = 64
B_WIDTH = 1024
B_KV_WIDTH = 128
ROPE_DIM = 16
ROPE_THETA = 500000.0
C_HEADS = 4
C_HEAD_DIM = 128
C_WIDTH = 512
GATE_OFF = 2 * A_WIDTH + B_WIDTH + 2 * B_KV_WIDTH + C_WIDTH

ADAM_LR = 0.001
ADAM_B1 = 0.9
ADAM_B2 = 0.999
ADAM_EPS = 1e-08
ADAM_WD = 0.01
ADAM_STEP = 10

VMEM_LIMIT = 48 * 1024 * 1024
MESH = pl.DeviceIdType.MESH


def _pick(n, prefs):
    for p in prefs:
        if p <= n and n % p == 0:
            return p
    return n


def _params(sem):
    return pltpu.CompilerParams(dimension_semantics=sem, vmem_limit_bytes=VMEM_LIMIT)


def _hide(body, n_seen, n_hidden):
    if not n_hidden:
        return body

    def wrapped(*refs):
        return body(*refs[:n_seen], *refs[n_seen + n_hidden:])

    return wrapped


def _hidden_specs(after):
    return [pl.BlockSpec(memory_space=pl.ANY) for _ in after]


def _token(xs, name):
    def body(*refs):
        refs[-1][...] = jnp.zeros_like(refs[-1])

    return pl.pallas_call(body, name=name, in_specs=_hidden_specs(xs), out_shape=jax.ShapeDtypeStruct((8, 128), F32))(*xs)


def _mm(a, b, mode, out_dtype, name, *, resid=None, b_stack=False, a_parts=0, b_parts=0, out_parts=0,
        out_stack=False, tm=1024, tn=1024, tk=2048, after=(), exchange=()):
    if mode == "nn":
        M = a.shape[-2]
        K = a.shape[-1] * max(a_parts, 1)
        N = b.shape[-1] * (N_DEV if b_stack else 1)
        dims = (((1,), (0,)), ((), ()))
    elif mode == "nt":
        M = a.shape[-2]
        K = a.shape[-1] * max(a_parts, 1)
        N = b.shape[-2]
        dims = (((1,), (1,)), ((), ()))
    else:
        K = a.shape[-2]
        M = a.shape[-1]
        N = b.shape[-1] * max(b_parts, 1)
        dims = (((0,), (0,)), ((), ()))
    if b_stack and mode == "nn":
        tn = b.shape[-1]
    if b_stack and mode == "nt":
        tk = b.shape[-1]
    if out_stack:
        tn = N // N_DEV
    tm, tn, tk = _pick(M, (tm,)), _pick(N, (tn,)), _pick(K, (tk,))
    if M % tm or N % tn or K % tk:
        raise ValueError(f"{name}: tiles {tm},{tn},{tk} do not divide {M},{N},{K}")
    nm, nn, nk = M // tm, N // tn, K // tk

    def parts_idx(t, ntile, parts):
        per = ntile // parts
        return t // per, t % per

    if mode in ("nn", "nt"):
        if a_parts:
            a_spec = pl.BlockSpec((None, tm, tk), lambda m, n, k: (parts_idx(k, nk, a_parts)[0], m, parts_idx(k, nk, a_parts)[1]))
        else:
            a_spec = pl.BlockSpec((tm, tk), lambda m, n, k: (m, k))
    else:
        a_spec = pl.BlockSpec((tk, tm), lambda m, n, k: (k, m))
    if mode == "nn":
        if b_stack:
            b_spec = pl.BlockSpec((None, tk, tn), lambda m, n, k: (n, k, 0))
        else:
            b_spec = pl.BlockSpec((tk, tn), lambda m, n, k: (k, n))
    elif mode == "nt":
        if b_stack:
            b_spec = pl.BlockSpec((None, tn, tk), lambda m, n, k: (k, n, 0))
        else:
            b_spec = pl.BlockSpec((tn, tk), lambda m, n, k: (n, k))
    else:
        if b_parts:
            b_spec = pl.BlockSpec((None, tk, tn), lambda m, n, k: (parts_idx(n, nn, b_parts)[0], k, parts_idx(n, nn, b_parts)[1]))
        else:
            b_spec = pl.BlockSpec((tk, tn), lambda m, n, k: (k, n))
    if out_stack:
        out_shape = jax.ShapeDtypeStruct((N_DEV, M, tn), out_dtype)
        o_spec = pl.BlockSpec((None, tm, tn), lambda m, n, k: (n, m, 0))
    elif out_parts:
        out_shape = jax.ShapeDtypeStruct((out_parts, M, N // out_parts), out_dtype)
        o_spec = pl.BlockSpec((None, tm, tn), lambda m, n, k: (parts_idx(n, nn, out_parts)[0], m, parts_idx(n, nn, out_parts)[1]))
    else:
        out_shape = jax.ShapeDtypeStruct((M, N), out_dtype)
        o_spec = pl.BlockSpec((tm, tn), lambda m, n, k: (m, n))
    has_resid = resid is not None

    n_ex = len(exchange)
    n_in = 2 + has_resid + len(after)

    def body(*refs):
        a_ref, b_ref = refs[:2]
        r_ref = refs[2] if has_resid else None
        ex_in = refs[n_in:n_in + n_ex]
        o_ref = refs[n_in + n_ex]
        ex_out = refs[n_in + n_ex + 1:n_in + 2 * n_ex + 1]
        scratch = refs[n_in + 2 * n_ex + 1:]
        m_i, n_i, k = pl.program_id(0), pl.program_id(1), pl.program_id(2)

        def pushes():
            send_sems, recv_sems = scratch[-2:]
            x, y, c = lax.axis_index("x"), lax.axis_index("y"), lax.axis_index("c")
            return [pltpu.make_async_remote_copy(
                src_ref=ex_in[w].at[:, 1 - c], dst_ref=ex_out[w], send_sem=send_sems.at[w], recv_sem=recv_sems.at[w],
                device_id=(x, y, 1 - c), device_id_type=MESH) for w in range(n_ex)]

        if n_ex:
            @pl.when((m_i == 0) & (n_i == 0) & (k == 0))
            def _():
                for cp in pushes():
                    cp.start()

        if nk == 1:
            res = lax.dot_general(a_ref[...], b_ref[...], dims, preferred_element_type=F32)
            if has_resid:
                res = res + r_ref[...]
            o_ref[...] = res.astype(o_ref.dtype)
        else:
            acc = scratch[0]

            @pl.when(k == 0)
            def _():
                acc[...] = jnp.zeros_like(acc)

            acc[...] += lax.dot_general(a_ref[...], b_ref[...], dims, preferred_element_type=F32)

            @pl.when(k == nk - 1)
            def _():
                res = acc[...]
                if has_resid:
                    res = res + r_ref[...]
                o_ref[...] = res.astype(o_ref.dtype)

        if n_ex:
            @pl.when((m_i == nm - 1) & (n_i == nn - 1) & (k == nk - 1))
            def _():
                for cp in pushes():
                    cp.wait()

    in_specs = [a_spec, b_spec]
    args = [a, b]
    if has_resid:
        in_specs.append(pl.BlockSpec((tm, tn), lambda m, n, k: (m, n)))
        args.append(resid)
    in_specs += _hidden_specs(after) + _hidden_specs(exchange)
    args += list(after) + list(exchange)
    scratch_shapes = [pltpu.VMEM((tm, tn), F32)] if nk > 1 else []
    if not n_ex:
        return pl.pallas_call(
            body, name=name, grid=(nm, nn, nk), in_specs=in_specs, out_specs=o_spec, out_shape=out_shape,
            scratch_shapes=scratch_shapes, compiler_params=_params(("parallel", "parallel", "arbitrary")),
        )(*args)
    res = pl.pallas_call(
        body, name=name, grid=(nm, nn, nk), in_specs=in_specs, out_specs=[o_spec] + _hidden_specs(exchange),
        out_shape=[out_shape] + [jax.ShapeDtypeStruct((g.shape[0],) + g.shape[2:], g.dtype) for g in exchange],
        scratch_shapes=scratch_shapes + [pltpu.SemaphoreType.DMA((n_ex,)), pltpu.SemaphoreType.DMA((n_ex,))],
        compiler_params=_params(("arbitrary", "arbitrary", "arbitrary")),
    )(*args)
    return res[0], list(res[1:])


def _rms_fwd(x, g, name):
    R, D = x.shape
    tr = _pick(R, (256,))

    def body(x_ref, g_ref, h_ref, r_ref):
        xv = x_ref[...]
        r = lax.rsqrt(jnp.mean(xv * xv, axis=-1, keepdims=True) + EPS)
        h_ref[...] = (xv * r * g_ref[...]).astype(BF)
        r_ref[...] = r

    return pl.pallas_call(
        body, name=name, grid=(R // tr,),
        in_specs=[pl.BlockSpec((tr, D), lambda i: (i, 0)), pl.BlockSpec((1, D), lambda i: (0, 0))],
        out_specs=[pl.BlockSpec((tr, D), lambda i: (i, 0)), pl.BlockSpec((tr, 1), lambda i: (i, 0))],
        out_shape=[jax.ShapeDtypeStruct((R, D), BF), jax.ShapeDtypeStruct((R, 1), F32)],
        compiler_params=_params(("parallel",)),
    )(x, g)


def _rms_bwd(x, r, g, dh, dres, name, after=()):
    R, D = x.shape
    tr = _pick(R, (256,))
    has_res = dres is not None

    def body(*refs):
        if has_res:
            x_ref, r_ref, g_ref, dh_ref, dres_ref, dx_ref, dxb_ref, dg_ref = refs
        else:
            x_ref, r_ref, g_ref, dh_ref, dx_ref, dxb_ref, dg_ref = refs
        i = pl.program_id(0)
        xv, rv, dhv = x_ref[...], r_ref[...], dh_ref[...]
        gy = dhv * g_ref[...]
        c = jnp.sum(xv * gy, axis=-1, keepdims=True)
        dx = rv * gy - xv * (rv * rv * rv) * (c * (1.0 / D))
        if has_res:
            dx = dx + dres_ref[...]
        dx_ref[...] = dx
        dxb_ref[...] = dx.astype(BF)
        part = jnp.sum(dhv * xv * rv, axis=0, keepdims=True)

        @pl.when(i == 0)
        def _():
            dg_ref[...] = part

        @pl.when(i > 0)
        def _():
            dg_ref[...] += part

    row = pl.BlockSpec((tr, D), lambda i: (i, 0))
    in_specs = [row, pl.BlockSpec((tr, 1), lambda i: (i, 0)), pl.BlockSpec((1, D), lambda i: (0, 0)), row]
    args = [x, r, g, dh]
    if has_res:
        in_specs.append(row)
        args.append(dres)
    return pl.pallas_call(
        _hide(body, len(args), len(after)), name=name, grid=(R // tr,), in_specs=in_specs + _hidden_specs(after),
        out_specs=[row, row, pl.BlockSpec((1, D), lambda i: (0, 0))],
        out_shape=[jax.ShapeDtypeStruct((R, D), F32), jax.ShapeDtypeStruct((R, D), BF), jax.ShapeDtypeStruct((1, D), F32)],
        compiler_params=_params(("arbitrary",)),
    )(*args, *after)


def _a_chunk(us, vs, gvs, ws, bs):
    r_i = lax.broadcasted_iota(jnp.int32, (CHUNK, CHUNK), 0)
    c_i = lax.broadcasted_iota(jnp.int32, (CHUNK, CHUNK), 1)
    causal = r_i >= c_i
    vg = [jax.nn.gelu(v) for v in vs]
    ss = sum(jnp.sum(v * v, axis=-1, keepdims=True) for v in vg)
    r = lax.rsqrt(ss * (1.0 / A_WIDTH) + EPS)
    ys = []
    for g in range(A_GROUPS):
        vn = vg[g] * r * gvs[g]
        w = jnp.where(causal, ws[g], 0.0)
        s = jnp.dot(w.astype(BF), vn.astype(BF), preferred_element_type=F32) + bs[g]
        ys.append(jax.nn.gelu(us[g]) * s)
    return ys


def _a_split(u_ref, v_ref, g_ref, w_ref, b_ref):
    sl = [slice(g * 128, (g + 1) * 128) for g in range(A_GROUPS)]
    return ([u_ref[:, s] for s in sl], [v_ref[:, s] for s in sl], [g_ref[:, s] for s in sl],
            [w_ref[g] for g in range(A_GROUPS)], [b_ref[:, g:g + 1] for g in range(A_GROUPS)])


def _a_specs(S):
    return [pl.BlockSpec((CHUNK, A_WIDTH), lambda n: (n, 0)), pl.BlockSpec((CHUNK, A_WIDTH), lambda n: (n, 1)),
            pl.BlockSpec((1, A_WIDTH), lambda n: (0, 0)), pl.BlockSpec((A_GROUPS, CHUNK, CHUNK), lambda n: (0, 0, 0)),
            pl.BlockSpec((CHUNK, A_GROUPS), lambda n: (0, 0))]


def _a_fwd(proj, g_v, w_s, b_t):
    S = proj.shape[0]

    def body(u_ref, v_ref, g_ref, w_ref, b_ref, y_ref):
        ys = _a_chunk(*_a_split(u_ref, v_ref, g_ref, w_ref, b_ref))
        for g in range(A_GROUPS):
            y_ref[:, g * 128:(g + 1) * 128] = ys[g].astype(BF)

    return pl.pallas_call(
        body, name="a_fwd", grid=(S // CHUNK,), in_specs=_a_specs(S),
        out_specs=pl.BlockSpec((CHUNK, A_WIDTH), lambda n: (n, 0)),
        out_shape=jax.ShapeDtypeStruct((S, A_WIDTH), BF), compiler_params=_params(("parallel",)),
    )(proj, proj, g_v, w_s, b_t)


def _a_bwd(proj, g_v, w_s, b_t, dy, after=()):
    S = proj.shape[0]

    def body(u_ref, v_ref, g_ref, w_ref, b_ref, dy_ref, duv_ref, dg_ref, dw_ref, db_ref):
        n = pl.program_id(0)
        dys = [dy_ref[:, g * 128:(g + 1) * 128] for g in range(A_GROUPS)]
        _, vjp = jax.vjp(_a_chunk, *_a_split(u_ref, v_ref, g_ref, w_ref, b_ref))
        dus, dvs, dgs, dws, dbs = vjp(dys)

        @pl.when(n == 0)
        def _():
            dg_ref[...] = jnp.zeros_like(dg_ref)
            dw_ref[...] = jnp.zeros_like(dw_ref)
            db_ref[...] = jnp.zeros_like(db_ref)

        for g in range(A_GROUPS):
            duv_ref[:, g * 128:(g + 1) * 128] = dus[g].astype(BF)
            duv_ref[:, A_WIDTH + g * 128:A_WIDTH + (g + 1) * 128] = dvs[g].astype(BF)
            dg_ref[:, g * 128:(g + 1) * 128] += dgs[g]
            dw_ref[g] += dws[g]
            db_ref[:, g:g + 1] += dbs[g]

    return pl.pallas_call(
        _hide(body, 6, len(after)), name="a_bwd", grid=(S // CHUNK,),
        in_specs=_a_specs(S) + [pl.BlockSpec((CHUNK, A_WIDTH), lambda n: (n, 0))] + _hidden_specs(after),
        out_specs=[pl.BlockSpec((CHUNK, 2 * A_WIDTH), lambda n: (n, 0)), pl.BlockSpec((1, A_WIDTH), lambda n: (0, 0)),
                   pl.BlockSpec((A_GROUPS, CHUNK, CHUNK), lambda n: (0, 0, 0)), pl.BlockSpec((CHUNK, A_GROUPS), lambda n: (0, 0))],
        out_shape=[jax.ShapeDtypeStruct((S, 2 * A_WIDTH), BF), jax.ShapeDtypeStruct((1, A_WIDTH), F32),
                   jax.ShapeDtypeStruct((A_GROUPS, CHUNK, CHUNK), F32), jax.ShapeDtypeStruct((CHUNK, A_GROUPS), F32)],
        compiler_params=_params(("arbitrary",)),
    )(proj, proj, g_v, w_s, b_t, dy, *after)


def _half_mask(shape, which):
    lane = lax.broadcasted_iota(jnp.int32, shape, len(shape) - 1)
    return (lane >= 64) == (which == 1)


def _pair_norm_rope(x, g, ct, sa, sb):
    lo = _half_mask(x.shape, 0)
    x2 = x * x
    ss_lo = jnp.sum(jnp.where(lo, x2, 0.0), axis=-1, keepdims=True)
    ss_hi = jnp.sum(jnp.where(lo, 0.0, x2), axis=-1, keepdims=True)
    r = jnp.where(lo, lax.rsqrt(ss_lo * (1.0 / B_HEAD_DIM) + EPS), lax.rsqrt(ss_hi * (1.0 / B_HEAD_DIM) + EPS))
    xr = x * r
    xn = xr * g
    out = xn * ct + pltpu.roll(xn, 120, 1) * sa + pltpu.roll(xn, 8, 1) * sb
    return out, xr, r


def _pair_norm_rope_bwd(x, g, ct, sa, sb, dout):
    lo = _half_mask(x.shape, 0)
    _, xr, r = _pair_norm_rope(x, g, ct, sa, sb)
    dxn = dout * ct + pltpu.roll(dout * sa, 8, 1) + pltpu.roll(dout * sb, 120, 1)
    gy = dxn * g
    t = xr * gy
    c_lo = jnp.sum(jnp.where(lo, t, 0.0), axis=-1, keepdims=True)
    c_hi = jnp.sum(jnp.where(lo, 0.0, t), axis=-1, keepdims=True)
    c = jnp.where(lo, c_lo, c_hi)
    dx = r * (gy - xr * c * (1.0 / B_HEAD_DIM))
    dg = jnp.sum(dxn * xr, axis=0, keepdims=True)
    return dx, dg


def _b_pre(proj, gq2, gk2, ct, sa, sb):
    S = proj.shape[0]
    tr = _pick(S, (256,))
    n_pair = B_WIDTH // 128

    def body(q_ref, k_ref, gq_ref, gk_ref, ct_ref, sa_ref, sb_ref, qn_ref, kn_ref):
        ct_v, sa_v, sb_v = ct_ref[...], sa_ref[...], sb_ref[...]
        for p in range(n_pair):
            o, _, _ = _pair_norm_rope(q_ref[:, p * 128:(p + 1) * 128], gq_ref[...], ct_v, sa_v, sb_v)
            qn_ref[:, p * 128:(p + 1) * 128] = o.astype(BF)
        o, _, _ = _pair_norm_rope(k_ref[...], gk_ref[...], ct_v, sa_v, sb_v)
        kn_ref[...] = o.astype(BF)

    tab = pl.BlockSpec((tr, 128), lambda i: (i, 0))
    gsp = pl.BlockSpec((1, 128), lambda i: (0, 0))
    return pl.pallas_call(
        body, name="b_pre", grid=(S // tr,),
        in_specs=[pl.BlockSpec((tr, B_WIDTH), lambda i: (i, 1)), pl.BlockSpec((tr, 128), lambda i: (i, 2 * B_WIDTH // 128)),
                  gsp, gsp, tab, tab, tab],
        out_specs=[pl.BlockSpec((tr, B_WIDTH), lambda i: (i, 0)), tab],
        out_shape=[jax.ShapeDtypeStruct((S, B_WIDTH), BF), jax.ShapeDtypeStruct((S, 128), BF)],
        compiler_params=_params(("parallel",)),
    )(proj, proj, gq2, gk2, ct, sa, sb)


def _b_pre_bwd(proj, gq2, gk2, ct, sa, sb, dqn, dkn, dv):
    S = proj.shape[0]
    tr = _pick(S, (256,))
    n_pair = B_WIDTH // 128

    def body(q_ref, k_ref, gq_ref, gk_ref, ct_ref, sa_ref, sb_ref, dqn_ref, dkn_ref, dv_ref, dqkv_ref, dgq_ref, dgk_ref):
        i = pl.program_id(0)
        ct_v, sa_v, sb_v = ct_ref[...], sa_ref[...], sb_ref[...]
        dgq = jnp.zeros((1, 128), F32)
        for p in range(n_pair):
            sl = slice(p * 128, (p + 1) * 128)
            dx, dg = _pair_norm_rope_bwd(q_ref[:, sl], gq_ref[...], ct_v, sa_v, sb_v, dqn_ref[:, sl])
            dqkv_ref[:, sl] = dx.astype(BF)
            dgq = dgq + dg
        dx, dgk = _pair_norm_rope_bwd(k_ref[...], gk_ref[...], ct_v, sa_v, sb_v, dkn_ref[...])
        dqkv_ref[:, B_WIDTH:B_WIDTH + 128] = dx.astype(BF)
        dqkv_ref[:, B_WIDTH + 128:B_WIDTH + 256] = dv_ref[...].astype(BF)

        @pl.when(i == 0)
        def _():
            dgq_ref[...] = dgq
            dgk_ref[...] = dgk

        @pl.when(i > 0)
        def _():
            dgq_ref[...] += dgq
            dgk_ref[...] += dgk

    tab = pl.BlockSpec((tr, 128), lambda i: (i, 0))
    gsp = pl.BlockSpec((1, 128), lambda i: (0, 0))
    return pl.pallas_call(
        body, name="b_pre_bwd", grid=(S // tr,),
        in_specs=[pl.BlockSpec((tr, B_WIDTH), lambda i: (i, 1)), pl.BlockSpec((tr, 128), lambda i: (i, 2 * B_WIDTH // 128)),
                  gsp, gsp, tab, tab, tab, pl.BlockSpec((tr, B_WIDTH), lambda i: (i, 0)), tab, tab],
        out_specs=[pl.BlockSpec((tr, B_WIDTH + 256), lambda i: (i, 0)), gsp, gsp],
        out_shape=[jax.ShapeDtypeStruct((S, B_WIDTH + 256), BF), jax.ShapeDtypeStruct((1, 128), F32), jax.ShapeDtypeStruct((1, 128), F32)],
        compiler_params=_params(("arbitrary",)),
    )(proj, proj, gq2, gk2, ct, sa, sb, dqn, dkn, dv)


def _b_dup(x2, g):
    d = jnp.where(_half_mask(x2.shape, g), x2, 0.0)
    return (d + pltpu.roll(d, 64, 1)).astype(BF)


PAIRS_PER_GROUP = B_HEADS // B_KV_HEADS // 2
GROUP_ROWS = PAIRS_PER_GROUP * CHUNK


def _b_valid(n):
    row = lax.broadcasted_iota(jnp.int32, (GROUP_ROWS, 2 * CHUNK), 0) & (CHUNK - 1)
    col = lax.broadcasted_iota(jnp.int32, (GROUP_ROWS, 2 * CHUNK), 1)
    rel = row + CHUNK - col
    return (rel >= 0) & (rel < CHUNK) & ((col >= CHUNK) | (n > 0))


def _b_blocks(x2, g):
    xd = _b_dup(x2, g)
    lo = _half_mask(xd.shape, 0)
    zero = jnp.zeros_like(xd)
    return jnp.concatenate([jnp.where(lo, xd, zero), jnp.where(lo, zero, xd)], axis=0)


def _b_sink_col(s_ref, g, hf):
    rb = lax.broadcasted_iota(jnp.int32, (GROUP_ROWS, 1), 0) // CHUNK
    col = jnp.zeros((GROUP_ROWS, 1), F32)
    for pp in range(PAIRS_PER_GROUP):
        col = jnp.where(rb == pp, s_ref[0, 2 * (g * PAIRS_PER_GROUP + pp) + hf], col)
    return col


def _b_probs(qs, kblk, valid, sinks):
    s = lax.dot_general(qs, kblk, (((1,), (1,)), ((), ())), preferred_element_type=F32) * (B_HEAD_DIM ** -0.5)
    out = []
    for hf in range(2):
        sh = jnp.where(valid, s[:, hf * 2 * CHUNK:(hf + 1) * 2 * CHUNK], NEG)
        m = jnp.maximum(jnp.max(sh, axis=-1, keepdims=True), sinks[hf])
        e = jnp.exp(sh - m)
        es = jnp.exp(sinks[hf] - m)
        inv = 1.0 / (jnp.sum(e, axis=-1, keepdims=True) + es)
        out.append((e * inv, es * inv))
    return out


def _b_fold(acc, g):
    lo = _half_mask((2 * CHUNK, 128), 0)
    t = jnp.where(lo, acc[:2 * CHUNK], 0.0) + jnp.where(lo, 0.0, acc[2 * CHUNK:])
    return jnp.where(_half_mask((2 * CHUNK, 128), g), t + pltpu.roll(t, 64, 1), 0.0)


def _b_kv_specs(S):
    prev = lambda n: (jnp.maximum(n - 1, 0), 0)
    cur = lambda n: (n, 0)
    v_col = (2 * B_WIDTH + B_KV_WIDTH) // 128
    return [pl.BlockSpec((CHUNK, 128), prev), pl.BlockSpec((CHUNK, 128), cur),
            pl.BlockSpec((CHUNK, 128), lambda n: (jnp.maximum(n - 1, 0), v_col)), pl.BlockSpec((CHUNK, 128), lambda n: (n, v_col))]


def _b_attn_fwd(qn, kn, proj, sinks):
    S = qn.shape[0]

    def body(s_ref, q_ref, kp_ref, kc_ref, vp_ref, vc_ref, y_ref):
        n = pl.program_id(0)
        valid = _b_valid(n)
        k2 = jnp.concatenate([kp_ref[...], kc_ref[...]], axis=0).astype(F32)
        v2 = jnp.concatenate([vp_ref[...], vc_ref[...]], axis=0)
        for g in range(B_KV_HEADS):
            pairs = [g * PAIRS_PER_GROUP + pp for pp in range(PAIRS_PER_GROUP)]
            qs = jnp.concatenate([q_ref[:, p * 128:(p + 1) * 128] for p in pairs], axis=0)
            probs = _b_probs(qs, _b_blocks(k2, g), valid, [_b_sink_col(s_ref, g, hf) for hf in range(2)])
            pcat = jnp.concatenate([probs[0][0].astype(BF), probs[1][0].astype(BF)], axis=1)
            o = jnp.dot(pcat, _b_blocks(v2, g), preferred_element_type=F32)
            for pp, p in enumerate(pairs):
                y_ref[:, p * 128:(p + 1) * 128] = o[pp * CHUNK:(pp + 1) * CHUNK].astype(BF)

    return pl.pallas_call(
        body, name="b_attn_fwd", grid=(S // CHUNK,),
        in_specs=[pl.BlockSpec(memory_space=pltpu.SMEM), pl.BlockSpec((CHUNK, B_WIDTH), lambda n: (n, 0))] + _b_kv_specs(S),
        out_specs=pl.BlockSpec((CHUNK, B_WIDTH), lambda n: (n, 0)),
        out_shape=jax.ShapeDtypeStruct((S, B_WIDTH), BF), compiler_params=_params(("arbitrary",)),
    )(sinks, qn, kn, kn, proj, proj)


def _b_attn_bwd(qn, kn, proj, sinks, dy, after=()):
    S = qn.shape[0]

    def body(s_ref, q_ref, kp_ref, kc_ref, vp_ref, vc_ref, dy_ref, dq_ref, dk_ref, dv_ref, ds_ref):
        n = pl.program_id(0)

        @pl.when(n == 0)
        def _():
            dk_ref[...] = jnp.zeros_like(dk_ref)
            dv_ref[...] = jnp.zeros_like(dv_ref)
            ds_ref[...] = jnp.zeros_like(ds_ref)

        valid = _b_valid(n)
        k2 = jnp.concatenate([kp_ref[...], kc_ref[...]], axis=0).astype(F32)
        v2 = jnp.concatenate([vp_ref[...], vc_ref[...]], axis=0)
        lane = lax.broadcasted_iota(jnp.int32, (CHUNK, 128), 1)
        dk2 = jnp.zeros((2 * CHUNK, 128), F32)
        dv2 = jnp.zeros((2 * CHUNK, 128), F32)
        dsink = jnp.zeros((CHUNK, 128), F32)
        scale = B_HEAD_DIM ** -0.5
        nt = (((1,), (1,)), ((), ()))
        tn = (((0,), (0,)), ((), ()))
        for g in range(B_KV_HEADS):
            pairs = [g * PAIRS_PER_GROUP + pp for pp in range(PAIRS_PER_GROUP)]
            qs = jnp.concatenate([q_ref[:, p * 128:(p + 1) * 128] for p in pairs], axis=0)
            do = jnp.concatenate([dy_ref[:, p * 128:(p + 1) * 128] for p in pairs], axis=0)
            do_b = do.astype(BF)
            kblk, vblk = _b_blocks(k2, g), _b_blocks(v2, g)
            probs = _b_probs(qs, kblk, valid, [_b_sink_col(s_ref, g, hf) for hf in range(2)])
            pcat = jnp.concatenate([probs[0][0].astype(BF), probs[1][0].astype(BF)], axis=1)
            o = jnp.dot(pcat, vblk, preferred_element_type=F32)
            dp = lax.dot_general(do_b, vblk, nt, preferred_element_type=F32)
            prod = do * o
            ds_halves = []
            for hf in range(2):
                pr, ps = probs[hf]
                delta = jnp.sum(jnp.where(_half_mask(prod.shape, hf), prod, 0.0), axis=-1, keepdims=True)
                ds_halves.append((pr * (dp[:, hf * 2 * CHUNK:(hf + 1) * 2 * CHUNK] - delta) * scale).astype(BF))
                t = -ps * delta
                for pp, p in enumerate(pairs):
                    dsink = dsink + jnp.where(lane == 2 * p + hf, t[pp * CHUNK:(pp + 1) * CHUNK], 0.0)
            dsc = jnp.concatenate(ds_halves, axis=1)
            dq = jnp.dot(dsc, kblk, preferred_element_type=F32)
            for pp, p in enumerate(pairs):
                dq_ref[:, p * 128:(p + 1) * 128] = dq[pp * CHUNK:(pp + 1) * CHUNK]
            dk2 = dk2 + _b_fold(lax.dot_general(dsc, qs, tn, preferred_element_type=F32), g)
            dv2 = dv2 + _b_fold(lax.dot_general(pcat, do_b, tn, preferred_element_type=F32), g)
        ds_ref[...] += dsink
        cur = pl.ds(pl.multiple_of(n * CHUNK, CHUNK), CHUNK)
        dk_ref[cur, :] += dk2[CHUNK:]
        dv_ref[cur, :] += dv2[CHUNK:]

        @pl.when(n > 0)
        def _():
            prv = pl.ds(pl.multiple_of((n - 1) * CHUNK, CHUNK), CHUNK)
            dk_ref[prv, :] += dk2[:CHUNK]
            dv_ref[prv, :] += dv2[:CHUNK]

    full = pl.BlockSpec((S, 128), lambda n: (0, 0))
    return pl.pallas_call(
        _hide(body, 7, len(after)), name="b_attn_bwd", grid=(S // CHUNK,),
        in_specs=[pl.BlockSpec(memory_space=pltpu.SMEM), pl.BlockSpec((CHUNK, B_WIDTH), lambda n: (n, 0))] + _b_kv_specs(S)
        + [pl.BlockSpec((CHUNK, B_WIDTH), lambda n: (n, 0))] + _hidden_specs(after),
        out_specs=[pl.BlockSpec((CHUNK, B_WIDTH), lambda n: (n, 0)), full, full, pl.BlockSpec((CHUNK, 128), lambda n: (0, 0))],
        out_shape=[jax.ShapeDtypeStruct((S, B_WIDTH), F32), jax.ShapeDtypeStruct((S, 128), F32), jax.ShapeDtypeStruct((S, 128), F32),
                   jax.ShapeDtypeStruct((CHUNK, 128), F32)],
        compiler_params=_params(("arbitrary",)),
    )(sinks, qn, kn, kn, proj, proj, dy, *after)


def _c_block(q, k, v, gq, gk):
    qn = q * lax.rsqrt(jnp.mean(q * q, axis=-1, keepdims=True) + EPS) * gq
    kn = k * lax.rsqrt(jnp.mean(k * k, axis=-1, keepdims=True) + EPS) * gk
    s = lax.dot_general(qn.astype(BF), kn.astype(BF), (((1,), (1,)), ((), ())), preferred_element_type=F32) * (C_HEAD_DIM ** -0.5)
    p = jax.nn.softmax(s, axis=-1)
    return jnp.dot(p.astype(BF), v.astype(BF), preferred_element_type=F32)


def _c_specs(S, M, tq):
    q_col = (2 * A_WIDTH + B_WIDTH + 2 * B_KV_WIDTH) // 128
    return [pl.BlockSpec((tq, 128), lambda h, i: (i, q_col + h)), pl.BlockSpec((M, 128), lambda h, i: (0, h)),
            pl.BlockSpec((M, 128), lambda h, i: (0, C_HEADS + h)), pl.BlockSpec((1, 128), lambda h, i: (0, 0)),
            pl.BlockSpec((1, 128), lambda h, i: (0, 0))]


def _c_fwd(proj, kv, gq, gk):
    S, M = proj.shape[0], kv.shape[0]
    tq = _pick(S, (512,))

    def body(q_ref, k_ref, v_ref, gq_ref, gk_ref, y_ref):
        y_ref[...] = _c_block(q_ref[...], k_ref[...], v_ref[...], gq_ref[...], gk_ref[...]).astype(BF)

    return pl.pallas_call(
        body, name="c_fwd", grid=(C_HEADS, S // tq), in_specs=_c_specs(S, M, tq),
        out_specs=pl.BlockSpec((tq, 128), lambda h, i: (i, h)),
        out_shape=jax.ShapeDtypeStruct((S, C_WIDTH), BF), compiler_params=_params(("parallel", "parallel")),
    )(proj, kv, kv, gq, gk)


def _c_bwd(proj, kv, gq, gk, dy):
    S, M = proj.shape[0], kv.shape[0]
    tq = _pick(S, (512,))

    def body(q_ref, k_ref, v_ref, gq_ref, gk_ref, dy_ref, dq_ref, dk_ref, dv_ref, dgq_ref, dgk_ref):
        i = pl.program_id(1)
        _, vjp = jax.vjp(_c_block, q_ref[...], k_ref[...], v_ref[...], gq_ref[...], gk_ref[...])
        dq, dk, dv, dgq, dgk = vjp(dy_ref[...])
        dq_ref[...] = dq.astype(BF)

        @pl.when(i == 0)
        def _():
            dk_ref[...] = dk
            dv_ref[...] = dv
            dgq_ref[...] = dgq
            dgk_ref[...] = dgk

        @pl.when(i > 0)
        def _():
            dk_ref[...] += dk
            dv_ref[...] += dv
            dgq_ref[...] += dgq
            dgk_ref[...] += dgk

    return pl.pallas_call(
        body, name="c_bwd", grid=(C_HEADS, S // tq),
        in_specs=_c_specs(S, M, tq) + [pl.BlockSpec((tq, 128), lambda h, i: (i, h))],
        out_specs=[pl.BlockSpec((tq, 128), lambda h, i: (i, h)), pl.BlockSpec((M, 128), lambda h, i: (0, h)),
                   pl.BlockSpec((M, 128), lambda h, i: (0, h)), pl.BlockSpec((None, 1, 128), lambda h, i: (h, 0, 0)),
                   pl.BlockSpec((None, 1, 128), lambda h, i: (h, 0, 0))],
        out_shape=[jax.ShapeDtypeStruct((S, C_WIDTH), BF), jax.ShapeDtypeStruct((M, C_WIDTH), F32), jax.ShapeDtypeStruct((M, C_WIDTH), F32),
                   jax.ShapeDtypeStruct((C_HEADS, 1, 128), F32), jax.ShapeDtypeStruct((C_HEADS, 1, 128), F32)],
        compiler_params=_params(("parallel", "arbitrary")),
    )(proj, kv, kv, gq, gk, dy)


def _merge_specs(S, D, tm, tn, ks):
    off = GATE_OFF // tn
    nd = D // tn
    gates = [pl.BlockSpec((tm, tn), functools.partial(lambda b, m, n: (m, off + b * nd + n), b)) for b in range(3)]
    ys = [pl.BlockSpec((tm, k), lambda m, n: (m, 0)) for k in ks]
    ws = [pl.BlockSpec((None, k, tn), lambda m, n: (n, 0, 0)) for k in ks]
    return gates, ys, ws


def _merge_fwd(proj, ys, ws):
    S = proj.shape[0]
    tn = ws[0].shape[2]
    D = N_DEV * tn
    ks = [w.shape[1] for w in ws]
    tm = _pick(S, (1024,))
    gates, y_specs, w_specs = _merge_specs(S, D, tm, tn, ks)

    def body(ga_ref, gb_ref, gc_ref, ya_ref, yb_ref, yc_ref, wa_ref, wb_ref, wc_ref, m_ref, za_ref, zb_ref, zc_ref):
        acc = None
        for g_ref, y_ref, w_ref, z_ref in ((ga_ref, ya_ref, wa_ref, za_ref), (gb_ref, yb_ref, wb_ref, zb_ref),
                                           (gc_ref, yc_ref, wc_ref, zc_ref)):
            z = jnp.dot(y_ref[...], w_ref[...], preferred_element_type=F32)
            z_ref[...] = z.astype(BF)
            t = jax.nn.sigmoid(g_ref[...]) * z
            acc = t if acc is None else acc + t
        m_ref[...] = acc.astype(BF)

    tile = pl.BlockSpec((tm, tn), lambda m, n: (m, n))
    return pl.pallas_call(
        body, name="merge_fwd", grid=(S // tm, D // tn), in_specs=gates + y_specs + w_specs,
        out_specs=[tile, tile, tile, tile], out_shape=[jax.ShapeDtypeStruct((S, D), BF)] * 4,
        compiler_params=_params(("parallel", "parallel")),
    )(proj, proj, proj, *ys, *ws)


def _merge_bwd(proj, zs, dm, ws, after=()):
    S = proj.shape[0]
    tn = ws[0].shape[2]
    D = N_DEV * tn
    ks = [w.shape[1] for w in ws]
    tm = _pick(S, (512,))
    gates, _, w_specs = _merge_specs(S, D, tm, tn, ks)
    nt = (((1,), (1,)), ((), ()))

    def body(ga_ref, gb_ref, gc_ref, za_ref, zb_ref, zc_ref, dm_ref, wa_ref, wb_ref, wc_ref,
             dza_ref, dzb_ref, dzc_ref, dga_ref, dgb_ref, dgc_ref, dya_ref, dyb_ref, dyc_ref):
        n = pl.program_id(1)
        dmv = dm_ref[...]
        for g_ref, z_ref, w_ref, dz_ref, dg_ref, dy_ref in (
                (ga_ref, za_ref, wa_ref, dza_ref, dga_ref, dya_ref), (gb_ref, zb_ref, wb_ref, dzb_ref, dgb_ref, dyb_ref),
                (gc_ref, zc_ref, wc_ref, dzc_ref, dgc_ref, dyc_ref)):
            sg = jax.nn.sigmoid(g_ref[...])
            dz = (sg * dmv).astype(BF)
            dz_ref[...] = dz
            dg_ref[...] = (dmv * z_ref[...].astype(F32) * sg * (1.0 - sg)).astype(BF)
            part = lax.dot_general(dz, w_ref[...], nt, preferred_element_type=F32)

            @pl.when(n == 0)
            def _():
                dy_ref[...] = part

            @pl.when(n > 0)
            def _():
                dy_ref[...] += part

    tile = pl.BlockSpec((tm, tn), lambda m, n: (m, n))
    dys = [pl.BlockSpec((tm, k), lambda m, n: (m, 0)) for k in ks]
    return pl.pallas_call(
        _hide(body, 10, len(after)), name="merge_bwd", grid=(S // tm, D // tn),
        in_specs=gates + [tile, tile, tile, tile] + w_specs + _hidden_specs(after),
        out_specs=[tile] * 6 + dys,
        out_shape=[jax.ShapeDtypeStruct((S, D), BF)] * 6 + [jax.ShapeDtypeStruct((S, k), F32) for k in ks],
        compiler_params=_params(("parallel", "arbitrary")),
    )(proj, proj, proj, *zs, dm, *ws, *after)


PAD = 8


def _stage_shift_down(us_ref, u_ref):
    S = u_ref.shape[1]
    us_ref[:, 0:PAD, :] = jnp.zeros((2, PAD, us_ref.shape[2]), F32)
    us_ref[:, PAD:S + PAD, :] = u_ref[...].astype(F32)


ROWS = 32


def _conv3(us_ref, part, r0, w, b):
    return (us_ref[part, pl.ds(r0 + PAD, ROWS), :] * w[2:3] + us_ref[part, pl.ds(r0 + PAD - 1, ROWS), :] * w[1:2]
            + us_ref[part, pl.ds(r0 + PAD - 2, ROWS), :] * w[0:1] + b)


def _ffn_specs(S, F, tc, c):
    per = c // tc

    def w_spec(half):
        return pl.BlockSpec((None, 3, tc), lambda j: (half * (N_DEV // 2) + j // per, 0, j % per))

    return [pl.BlockSpec((2, S, tc), lambda j: (0, 0, j)), w_spec(0), w_spec(1), pl.BlockSpec((2, 1, tc), lambda j: (0, 0, j))]


def _ffn_tile(F, c):
    tc = 128
    if c % tc or F % tc:
        raise ValueError(f"ffn tile {tc} does not divide {c}, {F}")
    return tc


def _ffn_act_fwd(up3, cws, cb3):
    _, S, F = up3.shape
    c = cws.shape[2]
    tc = _ffn_tile(F, c)

    def body(u_ref, wa_ref, wb_ref, b_ref, o_ref, us_ref):
        _stage_shift_down(us_ref, u_ref)
        wa, wb, ba, bb = wa_ref[...], wb_ref[...], b_ref[0], b_ref[1]

        def step(i, carry):
            r0 = pl.multiple_of(i * ROWS, ROWS)
            ca = _conv3(us_ref, 0, r0, wa, ba)
            cb = _conv3(us_ref, 1, r0, wb, bb)
            o_ref[pl.ds(r0, ROWS), :] = (ca * jax.nn.sigmoid(ca) * cb).astype(BF)
            return carry

        lax.fori_loop(0, S // ROWS, step, 0, unroll=4)

    return pl.pallas_call(
        body, name="ffn_act_fwd", grid=(F // tc,), in_specs=_ffn_specs(S, F, tc, c),
        out_specs=pl.BlockSpec((S, tc), lambda j: (0, j)), out_shape=jax.ShapeDtypeStruct((S, F), BF),
        scratch_shapes=[pltpu.VMEM((2, S + PAD, tc), F32)],
        compiler_params=_params(("parallel",)),
    )(up3, cws, cws, cb3)


def _ffn_act_bwd(up3, cws, cb3, dact, after=()):
    _, S, F = up3.shape
    c = cws.shape[2]
    tc = _ffn_tile(F, c)

    def body(u_ref, wa_ref, wb_ref, b_ref, da_ref, du_ref, dw_ref, db_ref, us_ref, dcs_ref):
        _stage_shift_down(us_ref, u_ref)
        ws = (wa_ref[...], wb_ref[...])
        ba, bb = b_ref[0], b_ref[1]
        dcs_ref[:, S:S + PAD, :] = jnp.zeros((2, PAD, tc), F32)

        def conv_grads(i, carry):
            r0 = pl.multiple_of(i * ROWS, ROWS)
            ca = _conv3(us_ref, 0, r0, ws[0], ba)
            cb = _conv3(us_ref, 1, r0, ws[1], bb)
            sg = jax.nn.sigmoid(ca)
            dav = da_ref[pl.ds(r0, ROWS), :].astype(F32)
            dcs_ref[0, pl.ds(r0, ROWS), :] = dav * cb * sg * (1.0 + ca * (1.0 - sg))
            dcs_ref[1, pl.ds(r0, ROWS), :] = dav * ca * sg
            return carry

        lax.fori_loop(0, S // ROWS, conv_grads, 0, unroll=4)

        def fold(v):
            return jnp.sum(v.reshape(ROWS // 8, 8, tc), axis=0)

        def input_grads(i, acc):
            r0 = pl.multiple_of(i * ROWS, ROWS)
            new = []
            for part in range(2):
                w = ws[part]
                dc = dcs_ref[part, pl.ds(r0, ROWS), :]
                dc1 = dcs_ref[part, pl.ds(r0 + 1, ROWS), :]
                dc2 = dcs_ref[part, pl.ds(r0 + 2, ROWS), :]
                u = us_ref[part, pl.ds(r0 + PAD, ROWS), :]
                du_ref[part, pl.ds(r0, ROWS), :] = (dc * w[2:3] + dc1 * w[1:2] + dc2 * w[0:1]).astype(BF)
                sums = (fold(dc2 * u), fold(dc1 * u), fold(dc * u), fold(dc))
                new += [a + s for a, s in zip(acc[4 * part:4 * part + 4], sums)]
            return tuple(new)

        acc = lax.fori_loop(0, S // ROWS, input_grads, tuple(jnp.zeros((8, tc), F32) for _ in range(8)), unroll=4)
        for part in range(2):
            for j in range(3):
                dw_ref[part, j:j + 1, :] = jnp.sum(acc[4 * part + j], axis=0, keepdims=True)
            db_ref[part] = jnp.sum(acc[4 * part + 3], axis=0, keepdims=True)

    return pl.pallas_call(
        _hide(body, 5, len(after)), name="ffn_act_bwd", grid=(F // tc,),
        in_specs=_ffn_specs(S, F, tc, c) + [pl.BlockSpec((S, tc), lambda j: (0, j))] + _hidden_specs(after),
        out_specs=[pl.BlockSpec((2, S, tc), lambda j: (0, 0, j)), pl.BlockSpec((2, 3, tc), lambda j: (0, 0, j)),
                   pl.BlockSpec((2, 1, tc), lambda j: (0, 0, j))],
        out_shape=[jax.ShapeDtypeStruct((2, S, F), BF), jax.ShapeDtypeStruct((2, 3, F), F32), jax.ShapeDtypeStruct((2, 1, F), F32)],
        scratch_shapes=[pltpu.VMEM((2, S + PAD, tc), F32), pltpu.VMEM((2, S + PAD, tc), F32)],
        compiler_params=_params(("parallel",)),
    )(up3, cws, cws, cb3, dact, *after)


def _residual_rms(a, w, x, g, name, tm=512):
    S, K = a.shape
    D = w.shape[1]
    tm = _pick(S, (tm,))

    def body(a_ref, w_ref, x_ref, g_ref, x1_ref, h_ref, r_ref):
        x1 = jnp.dot(a_ref[...], w_ref[...], preferred_element_type=F32) + x_ref[...]
        r = lax.rsqrt(jnp.mean(x1 * x1, axis=-1, keepdims=True) + EPS)
        x1_ref[...] = x1
        h_ref[...] = (x1 * r * g_ref[...]).astype(BF)
        r_ref[...] = r

    row = pl.BlockSpec((tm, D), lambda i: (i, 0))
    return pl.pallas_call(
        body, name=name, grid=(S // tm,),
        in_specs=[pl.BlockSpec((tm, K), lambda i: (i, 0)), pl.BlockSpec((K, D), lambda i: (0, 0)), row, pl.BlockSpec((1, D), lambda i: (0, 0))],
        out_specs=[row, row, pl.BlockSpec((tm, 1), lambda i: (i, 0))],
        out_shape=[jax.ShapeDtypeStruct((S, D), F32), jax.ShapeDtypeStruct((S, D), BF), jax.ShapeDtypeStruct((S, 1), F32)],
        compiler_params=_params(("parallel",)),
    )(a, w, x, g)


def _out_loss(act, w_down, x1, target, tm=512, tn=1024, tk=1408):
    S, F = act.shape
    D = w_down.shape[1]
    tm, tn, tk = _pick(S, (tm,)), _pick(D, (tn,)), _pick(F, (tk,))
    nm, nn, nk = S // tm, D // tn, F // tk

    def body(a_ref, b_ref, x_ref, t_ref, dy_ref, dyb_ref, l_ref, acc):
        m, n, k = pl.program_id(0), pl.program_id(1), pl.program_id(2)

        @pl.when((m == 0) & (n == 0) & (k == 0))
        def _():
            l_ref[...] = jnp.zeros_like(l_ref)

        @pl.when(k == 0)
        def _():
            acc[...] = jnp.zeros_like(acc)

        acc[...] += jnp.dot(a_ref[...], b_ref[...], preferred_element_type=F32)

        @pl.when(k == nk - 1)
        def _():
            e = acc[...] + x_ref[...] - t_ref[...]
            dy = e * (1.0 / D)
            dy_ref[...] = dy
            dyb_ref[...] = dy.astype(BF)
            l_ref[...] += jnp.sum(jnp.sum(e * e, axis=-1, keepdims=True), axis=0, keepdims=True) * (0.5 / D)

    tile = pl.BlockSpec((tm, tn), lambda m, n, k: (m, n))
    return pl.pallas_call(
        body, name="mm_y_loss", grid=(nm, nn, nk),
        in_specs=[pl.BlockSpec((tm, tk), lambda m, n, k: (m, k)), pl.BlockSpec((tk, tn), lambda m, n, k: (k, n)), tile, tile],
        out_specs=[tile, tile, pl.BlockSpec((8, 128), lambda m, n, k: (0, 0))],
        out_shape=[jax.ShapeDtypeStruct((S, D), F32), jax.ShapeDtypeStruct((S, D), BF), jax.ShapeDtypeStruct((8, 128), F32)],
        scratch_shapes=[pltpu.VMEM((tm, tn), F32)],
        compiler_params=_params(("arbitrary", "arbitrary", "arbitrary")),
    )(act, w_down, x1, target)


ANY = pl.BlockSpec(memory_space=pl.ANY)


def _allgather(shards, name):
    n = len(shards)

    def body(*refs):
        ins, outs = refs[:n], refs[n:2 * n]
        send_sems, recv_sems, local_sems = refs[2 * n:]
        x, y, c = lax.axis_index("x"), lax.axis_index("y"), lax.axis_index("c")
        me, sibling = (x, y, c), (x, y, 1 - c)
        chips = [(1 - x, y), (x, 1 - y), (1 - x, 1 - y)]

        def blk(w, px, py, pc):
            return outs[w].at[4 * px + 2 * py + pc]

        def copy(w, k, block, to, src=None):
            return pltpu.make_async_remote_copy(
                src_ref=blk(w, *block) if src is None else src, dst_ref=blk(w, *block),
                send_sem=send_sems.at[w, k], recv_sem=recv_sems.at[w, k], device_id=to, device_id_type=MESH)

        started = []
        mine = []
        for w in range(n):
            mine.append(pltpu.make_async_copy(ins[w], blk(w, *me), local_sems.at[w]))
            mine[-1].start()
            first = [copy(w, 0, me, sibling, src=ins[w])]
            first += [copy(w, 1 + j, me, (*chip, c), src=ins[w]) for j, chip in enumerate(chips)]
            for cp in first:
                cp.start()
            started += first
        for w in range(n):
            for j, chip in enumerate(chips):
                copy(w, 1 + j, (*chip, c), me).wait_recv()
                fwd = copy(w, 4 + j, (*chip, c), sibling)
                fwd.start()
                started.append(fwd)
        for w in range(n):
            copy(w, 0, sibling, me).wait_recv()
            for j, chip in enumerate(chips):
                copy(w, 4 + j, (*chip, 1 - c), me).wait_recv()
        for cp in started:
            cp.wait_send()
        for cp in mine:
            cp.wait()

    whole = pl.BlockSpec(memory_space=pltpu.VMEM)
    outs = pl.pallas_call(
        body, name=name, in_specs=[whole] * n, out_specs=[whole] * n,
        out_shape=[jax.ShapeDtypeStruct((N_DEV,) + s.shape, s.dtype) for s in shards],
        scratch_shapes=[pltpu.SemaphoreType.DMA((n, 7)), pltpu.SemaphoreType.DMA((n, 7)), pltpu.SemaphoreType.DMA((n,))],
    )(*shards)
    return list(outs)


def _allgather_seq(shards, name, collective_id, after=()):
    n = len(shards)
    n_after = len(after)

    halves = [s.shape[0] % 32 == 0 for s in shards]
    n_sem = 8
    to_diagonal = not all(halves)

    def body(*refs):
        ins, outs = refs[:n], refs[n + n_after:2 * n + n_after]
        send_sems, recv_sems, local_sems = refs[2 * n + n_after:]
        x, y, c = lax.axis_index("x"), lax.axis_index("y"), lax.axis_index("c")
        me, sibling = (x, y, c), (x, y, 1 - c)
        x_nb, y_nb, diag = (1 - x, y, c), (x, 1 - y, c), (1 - x, 1 - y, c)
        peers = [sibling, x_nb, y_nb] + ([diag] if to_diagonal else [])
        barrier = pltpu.get_barrier_semaphore()
        for peer in peers:
            pl.semaphore_signal(barrier, inc=1, device_id=peer, device_id_type=MESH)
        pl.semaphore_wait(barrier, len(peers))

        def blk(w, dev, rows=None):
            ref = outs[w].at[4 * dev[0] + 2 * dev[1] + dev[2]]
            return ref if rows is None else ref.at[rows]

        def copy(w, k, block, to, src=None, rows=None):
            return pltpu.make_async_remote_copy(
                src_ref=blk(w, block, rows) if src is None else src, dst_ref=blk(w, block, rows),
                send_sem=send_sems.at[n_sem * w + k], recv_sem=recv_sems.at[n_sem * w + k], device_id=to, device_id_type=MESH)

        def top(w):
            return pl.ds(0, shards[w].shape[0] // 2)

        def bottom(w):
            return pl.ds(shards[w].shape[0] // 2, shards[w].shape[0] // 2)

        started = []
        mine = []
        for w in range(n):
            mine.append(pltpu.make_async_copy(ins[w], blk(w, me), local_sems.at[w]))
            mine[-1].start()
            first = [copy(w, 0, me, sibling, src=ins[w]), copy(w, 1, me, x_nb, src=ins[w]), copy(w, 2, me, y_nb, src=ins[w])]
            if not halves[w]:
                first.append(copy(w, 3, me, diag, src=ins[w]))
            for cp in first:
                cp.start()
            started += first
        for w in range(n):
            copy(w, 1, x_nb, me).wait_recv()
            onward = [copy(w, 5, x_nb, sibling)] + ([copy(w, 3, x_nb, y_nb, rows=top(w))] if halves[w] else [])
            copy(w, 2, y_nb, me).wait_recv()
            onward += [copy(w, 6, y_nb, sibling)] + ([copy(w, 4, y_nb, x_nb, rows=bottom(w))] if halves[w] else [])
            for cp in onward:
                cp.start()
            started += onward
        for w in range(n):
            if halves[w]:
                copy(w, 3, diag, me, rows=top(w)).wait_recv()
                copy(w, 4, diag, me, rows=bottom(w)).wait_recv()
            else:
                copy(w, 3, diag, me).wait_recv()
            fwd = copy(w, 7, diag, sibling)
            fwd.start()
            started.append(fwd)
        for w in range(n):
            for k, dev in ((0, sibling), (5, (1 - x, y, 1 - c)), (6, (x, 1 - y, 1 - c)), (7, (1 - x, 1 - y, 1 - c))):
                copy(w, k, dev, me).wait_recv()
        for cp in started:
            cp.wait_send()
        for cp in mine:
            cp.wait()

    outs = pl.kernel(
        body, name=name, out_type=[jax.ShapeDtypeStruct((N_DEV,) + s.shape, s.dtype) for s in shards],
        mesh=plsc.ScalarSubcoreMesh(axis_name="seq", num_cores=1),
        scratch_types=[pltpu.SemaphoreType.DMA((n_sem * n,)), pltpu.SemaphoreType.DMA((n_sem * n,)), pltpu.SemaphoreType.DMA((n,))],
        compiler_params=pltpu.CompilerParams(collective_id=collective_id),
    )(*shards, *after)
    return list(outs)


def _chip_exchange(sums, name, collective_id):
    n = len(sums)

    def body(*refs):
        ins, outs = refs[:n], refs[n:2 * n]
        send_sems, recv_sems = refs[2 * n:]
        x, y, c = lax.axis_index("x"), lax.axis_index("y"), lax.axis_index("c")
        chips = [(1 - x, y), (x, 1 - y), (1 - x, 1 - y)]
        barrier = pltpu.get_barrier_semaphore()
        for px, py in chips:
            pl.semaphore_signal(barrier, inc=1, device_id=(px, py, c), device_id_type=MESH)
        pl.semaphore_wait(barrier, 3)
        copies = []
        for w in range(n):
            for k, (px, py) in enumerate(chips):
                copies.append(pltpu.make_async_remote_copy(
                    src_ref=ins[w].at[2 * px + py], dst_ref=outs[w].at[k], send_sem=send_sems.at[3 * w + k],
                    recv_sem=recv_sems.at[3 * w + k], device_id=(px, py, c), device_id_type=MESH))
        for cp in copies:
            cp.start()
        for cp in copies:
            cp.wait()

    outs = pl.kernel(
        body, name=name, out_type=[jax.ShapeDtypeStruct((3,) + s.shape[1:], s.dtype) for s in sums],
        mesh=plsc.ScalarSubcoreMesh(axis_name="seq", num_cores=1),
        scratch_types=[pltpu.SemaphoreType.DMA((3 * n,)), pltpu.SemaphoreType.DMA((3 * n,))],
        compiler_params=pltpu.CompilerParams(collective_id=collective_id),
    )(*sums)
    return list(outs)


def _row_tile(r, c, elems=256 * 1024):
    want = max(8, elems // c)
    for t in range(min(want, r) // 8 * 8, 0, -8):
        if r % t == 0:
            return t
    return r


def _pair_add(g4, recv, core, name, after=()):
    _, _, r, c = g4.shape
    tr = _row_tile(r, c, 1024 * 1024)

    def body(core_ref, a_ref, b_ref, o_ref):
        o_ref[...] = (a_ref[...].astype(F32) + b_ref[...].astype(F32)).astype(BF)

    return pl.pallas_call(
        _hide(body, 3, len(after)), name=name,
        grid_spec=pltpu.PrefetchScalarGridSpec(
            num_scalar_prefetch=1, grid=(4, r // tr),
            in_specs=[pl.BlockSpec((None, None, tr, c), lambda p, i, s: (p, s[0], i, 0)),
                      pl.BlockSpec((None, tr, c), lambda p, i, s: (p, i, 0))] + _hidden_specs(after),
            out_specs=pl.BlockSpec((None, tr, c), lambda p, i, s: (p, i, 0))),
        out_shape=jax.ShapeDtypeStruct((4, r, c), BF), compiler_params=_params(("parallel", "parallel")),
    )(core, g4, recv, *after)


def _adam_math(w, g, m, v):
    m = ADAM_B1 * m + (1.0 - ADAM_B1) * g
    v = ADAM_B2 * v + (1.0 - ADAM_B2) * (g * g)
    m_hat = m / (1.0 - ADAM_B1 ** ADAM_STEP)
    v_hat = v / (1.0 - ADAM_B2 ** ADAM_STEP)
    delta = -ADAM_LR * (m_hat / (jnp.sqrt(v_hat) + ADAM_EPS) + ADAM_WD * w)
    return delta, m, v


def _adamw_big(sums, recv, chip, w, m, v, name, after=()):
    r, c = w.shape
    tr = _row_tile(r, c, 512 * 1024)

    def body(chip_ref, s_ref, r_ref, w_ref, m_ref, v_ref, g_out, d_out, m_out, v_out):
        g = s_ref[...].astype(F32) + r_ref[0].astype(F32)
        g = g + r_ref[1].astype(F32)
        g = g + r_ref[2].astype(F32)
        delta, mn, vn = _adam_math(w_ref[...], g, m_ref[...], v_ref[...])
        g_out[...] = g
        d_out[...] = delta
        m_out[...] = mn
        v_out[...] = vn

    row = pl.BlockSpec((tr, c), lambda i, s: (i, 0))
    return pl.pallas_call(
        _hide(body, 6, len(after)), name=name,
        grid_spec=pltpu.PrefetchScalarGridSpec(
            num_scalar_prefetch=1, grid=(r // tr,),
            in_specs=[pl.BlockSpec((None, tr, c), lambda i, s: (s[0], i, 0)), pl.BlockSpec((3, tr, c), lambda i, s: (0, i, 0)),
                      row, row, row] + _hidden_specs(after),
            out_specs=[row, row, row, row]),
        out_shape=[jax.ShapeDtypeStruct((r, c), F32)] * 4, compiler_params=_params(("parallel",)),
    )(chip, sums, recv, w, m, v, *after)


def _adamw_small(parts, ws, ms, vs, extra_parts, name):
    n, ne = len(ws), len(extra_parts)

    def total(p_ref):
        g = p_ref[0]
        for d in range(1, N_DEV):
            g = g + p_ref[d]
        return g

    def body(*refs):
        p_refs, w_refs, m_refs, v_refs = refs[:n], refs[n:2 * n], refs[2 * n:3 * n], refs[3 * n:4 * n]
        e_refs = refs[4 * n:4 * n + ne]
        outs = refs[4 * n + ne:]
        for i in range(n):
            g = total(p_refs[i])
            delta, mn, vn = _adam_math(w_refs[i][...], g, m_refs[i][...], v_refs[i][...])
            outs[4 * i][...] = g
            outs[4 * i + 1][...] = delta
            outs[4 * i + 2][...] = mn
            outs[4 * i + 3][...] = vn
        for i in range(ne):
            outs[4 * n + i][...] = total(e_refs[i])

    out_shape = []
    for w in ws:
        out_shape += [jax.ShapeDtypeStruct(w.shape, F32)] * 4
    out_shape += [jax.ShapeDtypeStruct(e.shape[1:], F32) for e in extra_parts]
    res = pl.pallas_call(body, name=name, out_shape=out_shape,
                         compiler_params=pltpu.CompilerParams(vmem_limit_bytes=VMEM_LIMIT))(*parts, *ws, *ms, *vs, *extra_parts)
    return [res[4 * i:4 * i + 4] for i in range(n)], list(res[4 * n:])


def _adamw_plain(g, w, m, v, name):
    def body(g_ref, w_ref, m_ref, v_ref, d_out, m_out, v_out):
        delta, mn, vn = _adam_math(w_ref[...], g_ref[...], m_ref[...], v_ref[...])
        d_out[...] = delta
        m_out[...] = mn
        v_out[...] = vn

    return pl.pallas_call(body, name=name, out_shape=[jax.ShapeDtypeStruct(w.shape, F32)] * 3)(g, w, m, v)


def kernel(x, mem, positions, g_mix, w_in, g_a_v, w_spatial, b_spatial, g_b_q, g_b_k, sinks, g_mem, w_mem_kv, g_c_q, g_c_k, w_branch_a, w_branch_b, w_branch_c, w_out, g_ffn, w_up, conv_w, conv_b, w_down, loss_target, m_g_mix, m_w_in, m_g_a_v, m_w_spatial, m_b_spatial, m_g_b_q, m_g_b_k, m_sinks, m_g_mem, m_w_mem_kv, m_g_c_q, m_g_c_k, m_w_branch_a, m_w_branch_b, m_w_branch_c, m_w_out, m_g_ffn, m_w_up, m_conv_w, m_conv_b, m_w_down, v_g_mix, v_w_in, v_g_a_v, v_w_spatial, v_b_spatial, v_g_b_q, v_g_b_k, v_sinks, v_g_mem, v_w_mem_kv, v_g_c_q, v_g_c_k, v_w_branch_a, v_w_branch_b, v_w_branch_c, v_w_out, v_g_ffn, v_w_up, v_conv_w, v_conv_b, v_w_down):
    S, D = x.shape[1], x.shape[2]
    M = mem.shape[1]
    F = w_down.shape[1] * N_DEV
    in_cols = w_in.shape[2] * N_DEV
    ax, ay, ac = lax.axis_index("x"), lax.axis_index("y"), lax.axis_index("c")
    core = jnp.reshape(ac, (1,)).astype(jnp.int32)
    chip = jnp.reshape(2 * ax + ay, (1,)).astype(jnp.int32)
    me = 4 * ax + 2 * ay + ac

    x2, mem2, tgt2 = x[0], mem[0], loss_target[0]

    big = dict(w_in=w_in[0].T, w_mem_kv=w_mem_kv[0], w_branch_a=w_branch_a[0], w_branch_b=w_branch_b[0],
               w_branch_c=w_branch_c[0], w_out=w_out[0], w_up=w_up[0], w_down=w_down[0])
    names = list(big)
    cast = {k: big[k].astype(BF) for k in names}
    W = {}
    cb3 = conv_b.reshape(2, 1, F)
    W["w_in"], = _allgather_seq([cast["w_in"]], "ag_seq0", 0)
    w_in_t = W["w_in"].reshape(in_cols, D)
    grp1 = ["w_mem_kv", "w_branch_a", "w_branch_b", "w_branch_c", "w_out"]
    res1 = _allgather_seq([cast[k] for k in grp1] + [conv_w[0]], "ag_seq1", 1, after=(_token((w_in_t,), "tok_w_in"),))
    W.update(zip(grp1, res1))
    cw3 = res1[-1]
    w_kv_f = W["w_mem_kv"].reshape(D, 2 * C_WIDTH)
    w_out_f = W["w_out"].reshape(D, D)

    half = ROPE_DIM // 2
    inv = ROPE_THETA ** (-jnp.arange(half, dtype=F32) / half)
    ang = positions[0].astype(F32)[:, None] * inv
    cos, sin = jnp.cos(ang), jnp.sin(ang)
    one, zero = jnp.ones((S, B_HEAD_DIM - ROPE_DIM), F32), jnp.zeros((S, B_HEAD_DIM - ROPE_DIM), F32)
    z8 = jnp.zeros((S, half), F32)
    ct = jnp.tile(jnp.concatenate([cos, cos, one], axis=1), (1, 2))
    sa = jnp.tile(jnp.concatenate([-sin, z8, zero], axis=1), (1, 2))
    sb = jnp.tile(jnp.concatenate([z8, sin, zero], axis=1), (1, 2))
    gq2, gk2 = jnp.tile(g_b_q, (1, 2)), jnp.tile(g_b_k, (1, 2))
    b_t = b_spatial[0].T

    h, rstd1 = _rms_fwd(x2, g_mix, "rms1_fwd")
    proj = _mm(h, w_in_t, "nt", F32, "mm_proj", tn=1280)
    y_a = _a_fwd(proj, g_a_v, w_spatial[0], b_t)
    qn, kn = _b_pre(proj, gq2, gk2, ct, sa, sb)
    W["w_up"], = _allgather_seq([cast["w_up"]], "ag_seq2", 2, after=(_token((W["w_out"], qn), "tok_group1"),))
    y_b = _b_attn_fwd(qn, kn, proj, sinks)
    mem_h, rstd_m = _rms_fwd(mem2, g_mem, "rmsmem_fwd")
    kv = _mm(mem_h, w_kv_f, "nn", F32, "mm_kv", after=(y_b,))
    y_c = _c_fwd(proj, kv, g_c_q, g_c_k)
    w_branches = [W["w_branch_a"], W["w_branch_b"], W["w_branch_c"]]
    merged, z_a, z_b, z_c = _merge_fwd(proj, [y_a, y_b, y_c], w_branches)
    x1, h2, rstd2 = _residual_rms(merged, w_out_f, x2, g_ffn, "mm_x1_rms2")
    W["w_down"], = _allgather_seq([cast["w_down"]], "ag_seq3", 3, after=(W["w_up"], h2))
    w_down_f = W["w_down"].reshape(F, D)
    up3 = _mm(h2, W["w_up"], "nn", BF, "mm_up", b_stack=True, out_parts=2)
    act = _ffn_act_fwd(up3, cw3, cb3)
    dy, dy_b, loss_acc = _out_loss(act, w_down_f, x1, tgt2)
    loss = lax.psum(loss_acc[0, 0], ("x", "y", "c"))

    reduced = {}

    def as4(g):
        return g.reshape(4, 2, g.shape[1], g.shape[2])

    def finish_group(gi, keys, g4, from_sibling):
        sums = [_pair_add(a, b, core, "rs_add_" + k) for k, a, b in zip(keys, g4, from_sibling)]
        from_chips = _chip_exchange(sums, f"rs_chip{gi}", 4 + gi)
        reduced.update(zip(keys, zip(sums, from_chips)))
        return tuple(sums)

    d_act = _mm(dy_b, w_down_f, "nt", BF, "mm_dact", tn=1408)
    g_down = _mm(act, dy_b, "tn", BF, "mm_gdown", tm=1408)
    d_up3, d_cw3, d_cb3 = _ffn_act_bwd(up3, cw3, cb3, d_act, after=(g_down,))
    grp0 = [as4(g_down.reshape(N_DEV, F // N_DEV, D))]
    g_up, sib0 = _mm(h2, d_up3, "tn", BF, "mm_gup", b_parts=2, out_stack=True, exchange=grp0)
    sums0 = finish_group(0, ["w_down"], grp0, sib0)
    grp1 = [as4(g_up)]
    d_h2, sib1 = _mm(d_up3, W["w_up"], "nt", F32, "mm_dh2", a_parts=2, b_stack=True, tm=2048, after=sums0, exchange=grp1)
    sums1 = finish_group(1, ["w_up"], grp1, sib1)
    dx1, dx1_b, d_g_ffn = _rms_bwd(x1, rstd2, g_ffn, d_h2, dy, "rms2_bwd", after=sums1)
    g_out = _mm(merged, dx1_b, "tn", BF, "mm_gout")
    grp2 = [as4(g_out.reshape(N_DEV, D // N_DEV, D))]
    d_merged, sib2 = _mm(dx1_b, w_out_f, "nt", F32, "mm_dmerged", exchange=grp2)
    sums2 = finish_group(2, ["w_out"], grp2, sib2)
    dz_a, dz_b, dz_c, dga, dgb, dgc, dy_a, dy_b_, dy_c = _merge_bwd(proj, [z_a, z_b, z_c], d_merged, w_branches, after=sums2)
    g_ba = _mm(y_a, dz_a, "tn", BF, "mm_gba", out_stack=True)
    g_bb = _mm(y_b, dz_b, "tn", BF, "mm_gbb", out_stack=True)
    g_bc = _mm(y_c, dz_c, "tn", BF, "mm_gbc", out_stack=True)
    d_uv, d_g_a_v, d_w_s, d_b_t = _a_bwd(proj, g_a_v, w_spatial[0], b_t, dy_a, after=(g_ba, g_bb, g_bc))
    dqn, dkn, dv_b, dsink_rows = _b_attn_bwd(qn, kn, proj, sinks, dy_b_)
    d_qkv, d_gq2, d_gk2 = _b_pre_bwd(proj, gq2, gk2, ct, sa, sb, dqn, dkn, dv_b)
    dq_c, dk_c, dv_c, d_gcq, d_gck = _c_bwd(proj, kv, g_c_q, g_c_k, dy_c)
    dkv_b = jnp.concatenate([dk_c, dv_c], axis=1).astype(BF)
    d_memh = _mm(dkv_b, w_kv_f, "nt", F32, "mm_dmemh")
    g_kv = _mm(mem_h, dkv_b, "tn", BF, "mm_gkv")
    _, _, d_g_mem = _rms_bwd(mem2, rstd_m, g_mem, d_memh, None, "rmsmem_bwd")
    dproj = jnp.concatenate([d_uv, d_qkv, dq_c, dga, dgb, dgc], axis=1)
    grp3 = [as4(g_ba), as4(g_bb), as4(g_bc)]
    g_in, sib3 = _mm(dproj, h, "tn", BF, "mm_gin", tm=1280, exchange=grp3)
    sums3 = finish_group(3, ["w_branch_a", "w_branch_b", "w_branch_c"], grp3, sib3)
    grp4 = [as4(g_in.reshape(N_DEV, in_cols // N_DEV, D)), as4(g_kv.reshape(N_DEV, D // N_DEV, 2 * C_WIDTH))]
    d_h, sib4 = _mm(dproj, w_in_t, "nn", F32, "mm_dh", tm=2048, tk=1280, after=sums3, exchange=grp4)
    sums4 = finish_group(4, ["w_in", "w_mem_kv"], grp4, sib4)
    grad_x, _, d_g_mix = _rms_bwd(x2, rstd1, g_mix, d_h, dx1, "rms1_bwd", after=sums4)

    small_names =["g_mix", "g_a_v", "w_spatial", "b_spatial", "g_b_q", "g_b_k", "sinks", "g_mem", "g_c_q", "g_c_k", "g_ffn", "conv_b"]
    small_w = dict(g_mix=g_mix, g_a_v=g_a_v, w_spatial=w_spatial, b_spatial=b_spatial, g_b_q=g_b_q, g_b_k=g_b_k, sinks=sinks,
                   g_mem=g_mem, g_c_q=g_c_q, g_c_k=g_c_k, g_ffn=g_ffn, conv_b=conv_b)
    small_m = dict(g_mix=m_g_mix, g_a_v=m_g_a_v, w_spatial=m_w_spatial, b_spatial=m_b_spatial, g_b_q=m_g_b_q, g_b_k=m_g_b_k,
                   sinks=m_sinks, g_mem=m_g_mem, g_c_q=m_g_c_q, g_c_k=m_g_c_k, g_ffn=m_g_ffn, conv_b=m_conv_b)
    small_v = dict(g_mix=v_g_mix, g_a_v=v_g_a_v, w_spatial=v_w_spatial, b_spatial=v_b_spatial, g_b_q=v_g_b_q, g_b_k=v_g_b_k,
                   sinks=v_sinks, g_mem=v_g_mem, g_c_q=v_g_c_q, g_c_k=v_g_c_k, g_ffn=v_g_ffn, conv_b=v_conv_b)
    small_g = dict(
        g_mix=d_g_mix, g_a_v=d_g_a_v, w_spatial=d_w_s, b_spatial=d_b_t.T,
        g_b_q=d_gq2.reshape(2, B_HEAD_DIM).sum(0), g_b_k=d_gk2.reshape(2, B_HEAD_DIM).sum(0),
        sinks=dsink_rows.sum(0)[:B_HEADS], g_mem=d_g_mem, g_c_q=d_gcq.sum(0), g_c_k=d_gck.sum(0), g_ffn=d_g_ffn,
        conv_b=d_cb3)
    partial = [small_g[k].reshape(small_w[k].shape) for k in small_names] + [d_cw3]
    parts = _allgather(partial, "ag_small")
    small_res, (g_cw3,) = _adamw_small(parts[:-1], [small_w[k] for k in small_names], [small_m[k] for k in small_names],
                                       [small_v[k] for k in small_names], parts[-1:], "adamw_small")
    small_out = dict(zip(small_names, small_res))
    c_cw = 2 * F // N_DEV
    g_cw = lax.dynamic_slice(g_cw3, (me // (N_DEV // 2), 0, (me % (N_DEV // 2)) * c_cw), (1, 3, c_cw))[0]
    cw_res = _adamw_plain(g_cw, conv_w[0], m_conv_w[0], v_conv_w[0], "adamw_conv_w")
    big_out = {"conv_w": [g_cw[None]] + [a[None] for a in cw_res]}

    moments = dict(w_in=(m_w_in, v_w_in), w_mem_kv=(m_w_mem_kv, v_w_mem_kv), w_branch_a=(m_w_branch_a, v_w_branch_a),
                   w_branch_b=(m_w_branch_b, v_w_branch_b), w_branch_c=(m_w_branch_c, v_w_branch_c), w_out=(m_w_out, v_w_out),
                   w_up=(m_w_up, v_w_up), w_down=(m_w_down, v_w_down))
    token = (grad_x, small_res[0][0])
    for k in ["w_down", "w_up", "w_out", "w_branch_a", "w_branch_b", "w_branch_c", "w_mem_kv", "w_in"]:
        s, r = reduced[k]
        mk, vk = moments[k][0][0], moments[k][1][0]
        if k == "w_in":
            res = _adamw_big(s, r, chip, big[k], mk.T, vk.T, "adamw_" + k, after=token)
            big_out[k] = [a.T[None] for a in res]
        else:
            res = _adamw_big(s, r, chip, big[k], mk, vk, "adamw_" + k, after=token)
            big_out[k] = [a[None] for a in res]
        token = (res[0],)

    order = ["g_mix", "w_in", "g_a_v", "w_spatial", "b_spatial", "g_b_q", "g_b_k", "sinks", "g_mem", "w_mem_kv", "g_c_q", "g_c_k",
             "w_branch_a", "w_branch_b", "w_branch_c", "w_out", "g_ffn", "w_up", "conv_w", "conv_b", "w_down"]
    res = {**small_out, **big_out}
    outs = [loss, grad_x[None]]
    for field in range(4):
        outs += [res[k][field] for k in order]
    return tuple(outs)
```

```python
import functools

import jax
import jax.numpy as jnp
from jax import lax
from jax.experimental import pallas as pl
from jax.experimental.pallas import tpu as pltpu
from jax.experimental.pallas import tpu_sc as plsc

F32 = jnp.float32
BF = jnp.bfloat16
EPS = 1e-6
NEG = -1e30

N_DEV = 8
CHUNK = 128
A_GROUPS = 4
A_WIDTH = 512
B_HEADS = 16
B_KV_HEADS = 2
B_HEAD_DIM = 64
B_WIDTH = 1024
B_KV_WIDTH = 128
ROPE_DIM = 16
ROPE_THETA = 500000.0
C_HEADS = 4
C_HEAD_DIM = 128
C_WIDTH = 512
GATE_OFF = 2 * A_WIDTH + B_WIDTH + 2 * B_KV_WIDTH + C_WIDTH

ADAM_LR = 0.001
ADAM_B1 = 0.9
ADAM_B2 = 0.999
ADAM_EPS = 1e-08
ADAM_WD = 0.01
ADAM_STEP = 10

VMEM_LIMIT = 48 * 1024 * 1024
MESH = pl.DeviceIdType.MESH


def _pick(n, prefs):
    for p in prefs:
        if p <= n and n % p == 0:
            return p
    return n


def _params(sem):
    return pltpu.CompilerParams(dimension_semantics=sem, vmem_limit_bytes=VMEM_LIMIT)


def _hide(body, n_seen, n_hidden):
    if not n_hidden:
        return body

    def wrapped(*refs):
        return body(*refs[:n_seen], *refs[n_seen + n_hidden:])

    return wrapped


def _hidden_specs(after):
    return [pl.BlockSpec(memory_space=pl.ANY) for _ in after]


def _token(xs, name):
    def body(*refs):
        refs[-1][...] = jnp.zeros_like(refs[-1])

    return pl.pallas_call(body, name=name, in_specs=_hidden_specs(xs), out_shape=jax.ShapeDtypeStruct((8, 128), F32))(*xs)


def _mm(a, b, mode, out_dtype, name, *, resid=None, b_stack=False, a_parts=0, b_parts=0, out_parts=0,
        out_stack=False, tm=1024, tn=1024, tk=2048, after=(), exchange=()):
    if mode == "nn":
        M = a.shape[-2]
        K = a.shape[-1] * max(a_parts, 1)
        N = b.shape[-1] * (N_DEV if b_stack else 1)
        dims = (((1,), (0,)), ((), ()))
    elif mode == "nt":
        M = a.shape[-2]
        K = a.shape[-1] * max(a_parts, 1)
        N = b.shape[-2]
        dims = (((1,), (1,)), ((), ()))
    else:
        K = a.shape[-2]
        M = a.shape[-1]
        N = b.shape[-1] * max(b_parts, 1)
        dims = (((0,), (0,)), ((), ()))
    if b_stack and mode == "nn":
        tn = b.shape[-1]
    if b_stack and mode == "nt":
        tk = b.shape[-1]
    if out_stack:
        tn = N // N_DEV
    tm, tn, tk = _pick(M, (tm,)), _pick(N, (tn,)), _pick(K, (tk,))
    if M % tm or N % tn or K % tk:
        raise ValueError(f"{name}: tiles {tm},{tn},{tk} do not divide {M},{N},{K}")
    nm, nn, nk = M // tm, N // tn, K // tk

    def parts_idx(t, ntile, parts):
        per = ntile // parts
        return t // per, t % per

    if mode in ("nn", "nt"):
        if a_parts:
            a_spec = pl.BlockSpec((None, tm, tk), lambda m, n, k: (parts_idx(k, nk, a_parts)[0], m, parts_idx(k, nk, a_parts)[1]))
        else:
            a_spec = pl.BlockSpec((tm, tk), lambda m, n, k: (m, k))
    else:
        a_spec = pl.BlockSpec((tk, tm), lambda m, n, k: (k, m))
    if mode == "nn":
        if b_stack:
            b_spec = pl.BlockSpec((None, tk, tn), lambda m, n, k: (n, k, 0))
        else:
            b_spec = pl.BlockSpec((tk, tn), lambda m, n, k: (k, n))
    elif mode == "nt":
        if b_stack:
            b_spec = pl.BlockSpec((None, tn, tk), lambda m, n, k: (k, n, 0))
        else:
            b_spec = pl.BlockSpec((tn, tk), lambda m, n, k: (n, k))
    else:
        if b_parts:
            b_spec = pl.BlockSpec((None, tk, tn), lambda m, n, k: (parts_idx(n, nn, b_parts)[0], k, parts_idx(n, nn, b_parts)[1]))
        else:
            b_spec = pl.BlockSpec((tk, tn), lambda m, n, k: (k, n))
    if out_stack:
        out_shape = jax.ShapeDtypeStruct((N_DEV, M, tn), out_dtype)
        o_spec = pl.BlockSpec((None, tm, tn), lambda m, n, k: (n, m, 0))
    elif out_parts:
        out_shape = jax.ShapeDtypeStruct((out_parts, M, N // out_parts), out_dtype)
        o_spec = pl.BlockSpec((None, tm, tn), lambda m, n, k: (parts_idx(n, nn, out_parts)[0], m, parts_idx(n, nn, out_parts)[1]))
    else:
        out_shape = jax.ShapeDtypeStruct((M, N), out_dtype)
        o_spec = pl.BlockSpec((tm, tn), lambda m, n, k: (m, n))
    has_resid = resid is not None

    n_ex = len(exchange)
    n_in = 2 + has_resid + len(after)

    def body(*refs):
        a_ref, b_ref = refs[:2]
        r_ref = refs[2] if has_resid else None
        ex_in = refs[n_in:n_in + n_ex]
        o_ref = refs[n_in + n_ex]
        ex_out = refs[n_in + n_ex + 1:n_in + 2 * n_ex + 1]
        scratch = refs[n_in + 2 * n_ex + 1:]
        m_i, n_i, k = pl.program_id(0), pl.program_id(1), pl.program_id(2)

        def pushes():
            send_sems, recv_sems = scratch[-2:]
            x, y, c = lax.axis_index("x"), lax.axis_index("y"), lax.axis_index("c")
            return [pltpu.make_async_remote_copy(
                src_ref=ex_in[w].at[:, 1 - c], dst_ref=ex_out[w], send_sem=send_sems.at[w], recv_sem=recv_sems.at[w],
                device_id=(x, y, 1 - c), device_id_type=MESH) for w in range(n_ex)]

        if n_ex:
            @pl.when((m_i == 0) & (n_i == 0) & (k == 0))
            def _():
                for cp in pushes():
                    cp.start()

        if nk == 1:
            res = lax.dot_general(a_ref[...], b_ref[...], dims, preferred_element_type=F32)
            if has_resid:
                res = res + r_ref[...]
            o_ref[...] = res.astype(o_ref.dtype)
        else:
            acc = scratch[0]

            @pl.when(k == 0)
            def _():
                acc[...] = jnp.zeros_like(acc)

            acc[...] += lax.dot_general(a_ref[...], b_ref[...], dims, preferred_element_type=F32)

            @pl.when(k == nk - 1)
            def _():
                res = acc[...]
                if has_resid:
                    res = res + r_ref[...]
                o_ref[...] = res.astype(o_ref.dtype)

        if n_ex:
            @pl.when((m_i == nm - 1) & (n_i == nn - 1) & (k == nk - 1))
            def _():
                for cp in pushes():
                    cp.wait()

    in_specs = [a_spec, b_spec]
    args = [a, b]
    if has_resid:
        in_specs.append(pl.BlockSpec((tm, tn), lambda m, n, k: (m, n)))
        args.append(resid)
    in_specs += _hidden_specs(after) + _hidden_specs(exchange)
    args += list(after) + list(exchange)
    scratch_shapes = [pltpu.VMEM((tm, tn), F32)] if nk > 1 else []
    if not n_ex:
        return pl.pallas_call(
            body, name=name, grid=(nm, nn, nk), in_specs=in_specs, out_specs=o_spec, out_shape=out_shape,
            scratch_shapes=scratch_shapes, compiler_params=_params(("parallel", "parallel", "arbitrary")),
        )(*args)
    res = pl.pallas_call(
        body, name=name, grid=(nm, nn, nk), in_specs=in_specs, out_specs=[o_spec] + _hidden_specs(exchange),
        out_shape=[out_shape] + [jax.ShapeDtypeStruct((g.shape[0],) + g.shape[2:], g.dtype) for g in exchange],
        scratch_shapes=scratch_shapes + [pltpu.SemaphoreType.DMA((n_ex,)), pltpu.SemaphoreType.DMA((n_ex,))],
        compiler_params=_params(("arbitrary", "arbitrary", "arbitrary")),
    )(*args)
    return res[0], list(res[1:])


def _rms_fwd(x, g, name):
    R, D = x.shape
    tr = _pick(R, (256,))

    def body(x_ref, g_ref, h_ref, r_ref):
        xv = x_ref[...]
        r = lax.rsqrt(jnp.mean(xv * xv, axis=-1, keepdims=True) + EPS)
        h_ref[...] = (xv * r * g_ref[...]).astype(BF)
        r_ref[...] = r

    return pl.pallas_call(
        body, name=name, grid=(R // tr,),
        in_specs=[pl.BlockSpec((tr, D), lambda i: (i, 0)), pl.BlockSpec((1, D), lambda i: (0, 0))],
        out_specs=[pl.BlockSpec((tr, D), lambda i: (i, 0)), pl.BlockSpec((tr, 1), lambda i: (i, 0))],
        out_shape=[jax.ShapeDtypeStruct((R, D), BF), jax.ShapeDtypeStruct((R, 1), F32)],
        compiler_params=_params(("parallel",)),
    )(x, g)


def _rms_bwd(x, r, g, dh, dres, name, after=()):
    R, D = x.shape
    tr = _pick(R, (256,))
    has_res = dres is not None

    def body(*refs):
        if has_res:
            x_ref, r_ref, g_ref, dh_ref, dres_ref, dx_ref, dxb_ref, dg_ref = refs
        else:
            x_ref, r_ref, g_ref, dh_ref, dx_ref, dxb_ref, dg_ref = refs
        i = pl.program_id(0)
        xv, rv, dhv = x_ref[...], r_ref[...], dh_ref[...]
        gy = dhv * g_ref[...]
        c = jnp.sum(xv * gy, axis=-1, keepdims=True)
        dx = rv * gy - xv * (rv * rv * rv) * (c * (1.0 / D))
        if has_res:
            dx = dx + dres_ref[...]
        dx_ref[...] = dx
        dxb_ref[...] = dx.astype(BF)
        part = jnp.sum(dhv * xv * rv, axis=0, keepdims=True)

        @pl.when(i == 0)
        def _():
            dg_ref[...] = part

        @pl.when(i > 0)
        def _():
            dg_ref[...] += part

    row = pl.BlockSpec((tr, D), lambda i: (i, 0))
    in_specs = [row, pl.BlockSpec((tr, 1), lambda i: (i, 0)), pl.BlockSpec((1, D), lambda i: (0, 0)), row]
    args = [x, r, g, dh]
    if has_res:
        in_specs.append(row)
        args.append(dres)
    return pl.pallas_call(
        _hide(body, len(args), len(after)), name=name, grid=(R // tr,), in_specs=in_specs + _hidden_specs(after),
        out_specs=[row, row, pl.BlockSpec((1, D), lambda i: (0, 0))],
        out_shape=[jax.ShapeDtypeStruct((R, D), F32), jax.ShapeDtypeStruct((R, D), BF), jax.ShapeDtypeStruct((1, D), F32)],
        compiler_params=_params(("arbitrary",)),
    )(*args, *after)


def _a_chunk(us, vs, gvs, ws, bs):
    r_i = lax.broadcasted_iota(jnp.int32, (CHUNK, CHUNK), 0)
    c_i = lax.broadcasted_iota(jnp.int32, (CHUNK, CHUNK), 1)
    causal = r_i >= c_i
    vg = [jax.nn.gelu(v) for v in vs]
    ss = sum(jnp.sum(v * v, axis=-1, keepdims=True) for v in vg)
    r = lax.rsqrt(ss * (1.0 / A_WIDTH) + EPS)
    ys = []
    for g in range(A_GROUPS):
        vn = vg[g] * r * gvs[g]
        w = jnp.where(causal, ws[g], 0.0)
        s = jnp.dot(w.astype(BF), vn.astype(BF), preferred_element_type=F32) + bs[g]
        ys.append(jax.nn.gelu(us[g]) * s)
    return ys


def _a_split(u_ref, v_ref, g_ref, w_ref, b_ref):
    sl = [slice(g * 128, (g + 1) * 128) for g in range(A_GROUPS)]
    return ([u_ref[:, s] for s in sl], [v_ref[:, s] for s in sl], [g_ref[:, s] for s in sl],
            [w_ref[g] for g in range(A_GROUPS)], [b_ref[:, g:g + 1] for g in range(A_GROUPS)])


def _a_specs(S):
    return [pl.BlockSpec((CHUNK, A_WIDTH), lambda n: (n, 0)), pl.BlockSpec((CHUNK, A_WIDTH), lambda n: (n, 1)),
            pl.BlockSpec((1, A_WIDTH), lambda n: (0, 0)), pl.BlockSpec((A_GROUPS, CHUNK, CHUNK), lambda n: (0, 0, 0)),
            pl.BlockSpec((CHUNK, A_GROUPS), lambda n: (0, 0))]


def _a_fwd(proj, g_v, w_s, b_t):
    S = proj.shape[0]

    def body(u_ref, v_ref, g_ref, w_ref, b_ref, y_ref):
        ys = _a_chunk(*_a_split(u_ref, v_ref, g_ref, w_ref, b_ref))
        for g in range(A_GROUPS):
            y_ref[:, g * 128:(g + 1) * 128] = ys[g].astype(BF)

    return pl.pallas_call(
        body, name="a_fwd", grid=(S // CHUNK,), in_specs=_a_specs(S),
        out_specs=pl.BlockSpec((CHUNK, A_WIDTH), lambda n: (n, 0)),
        out_shape=jax.ShapeDtypeStruct((S, A_WIDTH), BF), compiler_params=_params(("parallel",)),
    )(proj, proj, g_v, w_s, b_t)


def _a_bwd(proj, g_v, w_s, b_t, dy, after=()):
    S = proj.shape[0]

    def body(u_ref, v_ref, g_ref, w_ref, b_ref, dy_ref, duv_ref, dg_ref, dw_ref, db_ref):
        n = pl.program_id(0)
        dys = [dy_ref[:, g * 128:(g + 1) * 128] for g in range(A_GROUPS)]
        _, vjp = jax.vjp(_a_chunk, *_a_split(u_ref, v_ref, g_ref, w_ref, b_ref))
        dus, dvs, dgs, dws, dbs = vjp(dys)

        @pl.when(n == 0)
        def _():
            dg_ref[...] = jnp.zeros_like(dg_ref)
            dw_ref[...] = jnp.zeros_like(dw_ref)
            db_ref[...] = jnp.zeros_like(db_ref)

        for g in range(A_GROUPS):
            duv_ref[:, g * 128:(g + 1) * 128] = dus[g].astype(BF)
            duv_ref[:, A_WIDTH + g * 128:A_WIDTH + (g + 1) * 128] = dvs[g].astype(BF)
            dg_ref[:, g * 128:(g + 1) * 128] += dgs[g]
            dw_ref[g] += dws[g]
            db_ref[:, g:g + 1] += dbs[g]

    return pl.pallas_call(
        _hide(body, 6, len(after)), name="a_bwd", grid=(S // CHUNK,),
        in_specs=_a_specs(S) + [pl.BlockSpec((CHUNK, A_WIDTH), lambda n: (n, 0))] + _hidden_specs(after),
        out_specs=[pl.BlockSpec((CHUNK, 2 * A_WIDTH), lambda n: (n, 0)), pl.BlockSpec((1, A_WIDTH), lambda n: (0, 0)),
                   pl.BlockSpec((A_GROUPS, CHUNK, CHUNK), lambda n: (0, 0, 0)), pl.BlockSpec((CHUNK, A_GROUPS), lambda n: (0, 0))],
        out_shape=[jax.ShapeDtypeStruct((S, 2 * A_WIDTH), BF), jax.ShapeDtypeStruct((1, A_WIDTH), F32),
                   jax.ShapeDtypeStruct((A_GROUPS, CHUNK, CHUNK), F32), jax.ShapeDtypeStruct((CHUNK, A_GROUPS), F32)],
        compiler_params=_params(("arbitrary",)),
    )(proj, proj, g_v, w_s, b_t, dy, *after)


def _half_mask(shape, which):
    lane = lax.broadcasted_iota(jnp.int32, shape, len(shape) - 1)
    return (lane >= 64) == (which == 1)


def _pair_norm_rope(x, g, ct, sa, sb):
    lo = _half_mask(x.shape, 0)
    x2 = x * x
    ss_lo = jnp.sum(jnp.where(lo, x2, 0.0), axis=-1, keepdims=True)
    ss_hi = jnp.sum(jnp.where(lo, 0.0, x2), axis=-1, keepdims=True)
    r = jnp.where(lo, lax.rsqrt(ss_lo * (1.0 / B_HEAD_DIM) + EPS), lax.rsqrt(ss_hi * (1.0 / B_HEAD_DIM) + EPS))
    xr = x * r
    xn = xr * g
    out = xn * ct + pltpu.roll(xn, 120, 1) * sa + pltpu.roll(xn, 8, 1) * sb
    return out, xr, r


def _pair_norm_rope_bwd(x, g, ct, sa, sb, dout):
    lo = _half_mask(x.shape, 0)
    _, xr, r = _pair_norm_rope(x, g, ct, sa, sb)
    dxn = dout * ct + pltpu.roll(dout * sa, 8, 1) + pltpu.roll(dout * sb, 120, 1)
    gy = dxn * g
    t = xr * gy
    c_lo = jnp.sum(jnp.where(lo, t, 0.0), axis=-1, keepdims=True)
    c_hi = jnp.sum(jnp.where(lo, 0.0, t), axis=-1, keepdims=True)
    c = jnp.where(lo, c_lo, c_hi)
    dx = r * (gy - xr * c * (1.0 / B_HEAD_DIM))
    dg = jnp.sum(dxn * xr, axis=0, keepdims=True)
    return dx, dg


def _b_pre(proj, gq2, gk2, ct, sa, sb):
    S = proj.shape[0]
    tr = _pick(S, (256,))
    n_pair = B_WIDTH // 128

    def body(q_ref, k_ref, gq_ref, gk_ref, ct_ref, sa_ref, sb_ref, qn_ref, kn_ref):
        ct_v, sa_v, sb_v = ct_ref[...], sa_ref[...], sb_ref[...]
        for p in range(n_pair):
            o, _, _ = _pair_norm_rope(q_ref[:, p * 128:(p + 1) * 128], gq_ref[...], ct_v, sa_v, sb_v)
            qn_ref[:, p * 128:(p + 1) * 128] = o.astype(BF)
        o, _, _ = _pair_norm_rope(k_ref[...], gk_ref[...], ct_v, sa_v, sb_v)
        kn_ref[...] = o.astype(BF)

    tab = pl.BlockSpec((tr, 128), lambda i: (i, 0))
    gsp = pl.BlockSpec((1, 128), lambda i: (0, 0))
    return pl.pallas_call(
        body, name="b_pre", grid=(S // tr,),
        in_specs=[pl.BlockSpec((tr, B_WIDTH), lambda i: (i, 1)), pl.BlockSpec((tr, 128), lambda i: (i, 2 * B_WIDTH // 128)),
                  gsp, gsp, tab, tab, tab],
        out_specs=[pl.BlockSpec((tr, B_WIDTH), lambda i: (i, 0)), tab],
        out_shape=[jax.ShapeDtypeStruct((S, B_WIDTH), BF), jax.ShapeDtypeStruct((S, 128), BF)],
        compiler_params=_params(("parallel",)),
    )(proj, proj, gq2, gk2, ct, sa, sb)


def _b_pre_bwd(proj, gq2, gk2, ct, sa, sb, dqn, dkn, dv):
    S = proj.shape[0]
    tr = _pick(S, (256,))
    n_pair = B_WIDTH // 128

    def body(q_ref, k_ref, gq_ref, gk_ref, ct_ref, sa_ref, sb_ref, dqn_ref, dkn_ref, dv_ref, dqkv_ref, dgq_ref, dgk_ref):
        i = pl.program_id(0)
        ct_v, sa_v, sb_v = ct_ref[...], sa_ref[...], sb_ref[...]
        dgq = jnp.zeros((1, 128), F32)
        for p in range(n_pair):
            sl = slice(p * 128, (p + 1) * 128)
            dx, dg = _pair_norm_rope_bwd(q_ref[:, sl], gq_ref[...], ct_v, sa_v, sb_v, dqn_ref[:, sl])
            dqkv_ref[:, sl] = dx.astype(BF)
            dgq = dgq + dg
        dx, dgk = _pair_norm_rope_bwd(k_ref[...], gk_ref[...], ct_v, sa_v, sb_v, dkn_ref[...])
        dqkv_ref[:, B_WIDTH:B_WIDTH + 128] = dx.astype(BF)
        dqkv_ref[:, B_WIDTH + 128:B_WIDTH + 256] = dv_ref[...].astype(BF)

        @pl.when(i == 0)
        def _():
            dgq_ref[...] = dgq
            dgk_ref[...] = dgk

        @pl.when(i > 0)
        def _():
            dgq_ref[...] += dgq
            dgk_ref[...] += dgk

    tab = pl.BlockSpec((tr, 128), lambda i: (i, 0))
    gsp = pl.BlockSpec((1, 128), lambda i: (0, 0))
    return pl.pallas_call(
        body, name="b_pre_bwd", grid=(S // tr,),
        in_specs=[pl.BlockSpec((tr, B_WIDTH), lambda i: (i, 1)), pl.BlockSpec((tr, 128), lambda i: (i, 2 * B_WIDTH // 128)),
                  gsp, gsp, tab, tab, tab, pl.BlockSpec((tr, B_WIDTH), lambda i: (i, 0)), tab, tab],
        out_specs=[pl.BlockSpec((tr, B_WIDTH + 256), lambda i: (i, 0)), gsp, gsp],
        out_shape=[jax.ShapeDtypeStruct((S, B_WIDTH + 256), BF), jax.ShapeDtypeStruct((1, 128), F32), jax.ShapeDtypeStruct((1, 128), F32)],
        compiler_params=_params(("arbitrary",)),
    )(proj, proj, gq2, gk2, ct, sa, sb, dqn, dkn, dv)


def _b_dup(x2, g):
    d = jnp.where(_half_mask(x2.shape, g), x2, 0.0)
    return (d + pltpu.roll(d, 64, 1)).astype(BF)


PAIRS_PER_GROUP = B_HEADS // B_KV_HEADS // 2
GROUP_ROWS = PAIRS_PER_GROUP * CHUNK


def _b_valid(n):
    row = lax.broadcasted_iota(jnp.int32, (GROUP_ROWS, 2 * CHUNK), 0) & (CHUNK - 1)
    col = lax.broadcasted_iota(jnp.int32, (GROUP_ROWS, 2 * CHUNK), 1)
    rel = row + CHUNK - col
    return (rel >= 0) & (rel < CHUNK) & ((col >= CHUNK) | (n > 0))


def _b_blocks(x2, g):
    xd = _b_dup(x2, g)
    lo = _half_mask(xd.shape, 0)
    zero = jnp.zeros_like(xd)
    return jnp.concatenate([jnp.where(lo, xd, zero), jnp.where(lo, zero, xd)], axis=0)


def _b_sink_col(s_ref, g, hf):
    rb = lax.broadcasted_iota(jnp.int32, (GROUP_ROWS, 1), 0) // CHUNK
    col = jnp.zeros((GROUP_ROWS, 1), F32)
    for pp in range(PAIRS_PER_GROUP):
        col = jnp.where(rb == pp, s_ref[0, 2 * (g * PAIRS_PER_GROUP + pp) + hf], col)
    return col


def _b_probs(qs, kblk, valid, sinks):
    s = lax.dot_general(qs, kblk, (((1,), (1,)), ((), ())), preferred_element_type=F32) * (B_HEAD_DIM ** -0.5)
    out = []
    for hf in range(2):
        sh = jnp.where(valid, s[:, hf * 2 * CHUNK:(hf + 1) * 2 * CHUNK], NEG)
        m = jnp.maximum(jnp.max(sh, axis=-1, keepdims=True), sinks[hf])
        e = jnp.exp(sh - m)
        es = jnp.exp(sinks[hf] - m)
        inv = 1.0 / (jnp.sum(e, axis=-1, keepdims=True) + es)
        out.append((e * inv, es * inv))
    return out


def _b_fold(acc, g):
    lo = _half_mask((2 * CHUNK, 128), 0)
    t = jnp.where(lo, acc[:2 * CHUNK], 0.0) + jnp.where(lo, 0.0, acc[2 * CHUNK:])
    return jnp.where(_half_mask((2 * CHUNK, 128), g), t + pltpu.roll(t, 64, 1), 0.0)


def _b_kv_specs(S):
    prev = lambda n: (jnp.maximum(n - 1, 0), 0)
    cur = lambda n: (n, 0)
    v_col = (2 * B_WIDTH + B_KV_WIDTH) // 128
    return [pl.BlockSpec((CHUNK, 128), prev), pl.BlockSpec((CHUNK, 128), cur),
            pl.BlockSpec((CHUNK, 128), lambda n: (jnp.maximum(n - 1, 0), v_col)), pl.BlockSpec((CHUNK, 128), lambda n: (n, v_col))]


def _b_attn_fwd(qn, kn, proj, sinks):
    S = qn.shape[0]

    def body(s_ref, q_ref, kp_ref, kc_ref, vp_ref, vc_ref, y_ref):
        n = pl.program_id(0)
        valid = _b_valid(n)
        k2 = jnp.concatenate([kp_ref[...], kc_ref[...]], axis=0).astype(F32)
        v2 = jnp.concatenate([vp_ref[...], vc_ref[...]], axis=0)
        for g in range(B_KV_HEADS):
            pairs = [g * PAIRS_PER_GROUP + pp for pp in range(PAIRS_PER_GROUP)]
            qs = jnp.concatenate([q_ref[:, p * 128:(p + 1) * 128] for p in pairs], axis=0)
            probs = _b_probs(qs, _b_blocks(k2, g), valid, [_b_sink_col(s_ref, g, hf) for hf in range(2)])
            pcat = jnp.concatenate([probs[0][0].astype(BF), probs[1][0].astype(BF)], axis=1)
            o = jnp.dot(pcat, _b_blocks(v2, g), preferred_element_type=F32)
            for pp, p in enumerate(pairs):
                y_ref[:, p * 128:(p + 1) * 128] = o[pp * CHUNK:(pp + 1) * CHUNK].astype(BF)

    return pl.pallas_call(
        body, name="b_attn_fwd", grid=(S // CHUNK,),
        in_specs=[pl.BlockSpec(memory_space=pltpu.SMEM), pl.BlockSpec((CHUNK, B_WIDTH), lambda n: (n, 0))] + _b_kv_specs(S),
        out_specs=pl.BlockSpec((CHUNK, B_WIDTH), lambda n: (n, 0)),
        out_shape=jax.ShapeDtypeStruct((S, B_WIDTH), BF), compiler_params=_params(("arbitrary",)),
    )(sinks, qn, kn, kn, proj, proj)


def _b_attn_bwd(qn, kn, proj, sinks, dy, after=()):
    S = qn.shape[0]

    def body(s_ref, q_ref, kp_ref, kc_ref, vp_ref, vc_ref, dy_ref, dq_ref, dk_ref, dv_ref, ds_ref):
        n = pl.program_id(0)

        @pl.when(n == 0)
        def _():
            dk_ref[...] = jnp.zeros_like(dk_ref)
            dv_ref[...] = jnp.zeros_like(dv_ref)
            ds_ref[...] = jnp.zeros_like(ds_ref)

        valid = _b_valid(n)
        k2 = jnp.concatenate([kp_ref[...], kc_ref[...]], axis=0).astype(F32)
        v2 = jnp.concatenate([vp_ref[...], vc_ref[...]], axis=0)
        lane = lax.broadcasted_iota(jnp.int32, (CHUNK, 128), 1)
        dk2 = jnp.zeros((2 * CHUNK, 128), F32)
        dv2 = jnp.zeros((2 * CHUNK, 128), F32)
        dsink = jnp.zeros((CHUNK, 128), F32)
        scale = B_HEAD_DIM ** -0.5
        nt = (((1,), (1,)), ((), ()))
        tn = (((0,), (0,)), ((), ()))
        for g in range(B_KV_HEADS):
            pairs = [g * PAIRS_PER_GROUP + pp for pp in range(PAIRS_PER_GROUP)]
            qs = jnp.concatenate([q_ref[:, p * 128:(p + 1) * 128] for p in pairs], axis=0)
            do = jnp.concatenate([dy_ref[:, p * 128:(p + 1) * 128] for p in pairs], axis=0)
            do_b = do.astype(BF)
            kblk, vblk = _b_blocks(k2, g), _b_blocks(v2, g)
            probs = _b_probs(qs, kblk, valid, [_b_sink_col(s_ref, g, hf) for hf in range(2)])
            pcat = jnp.concatenate([probs[0][0].astype(BF), probs[1][0].astype(BF)], axis=1)
            o = jnp.dot(pcat, vblk, preferred_element_type=F32)
            dp = lax.dot_general(do_b, vblk, nt, preferred_element_type=F32)
            prod = do * o
            ds_halves = []
            for hf in range(2):
                pr, ps = probs[hf]
                delta = jnp.sum(jnp.where(_half_mask(prod.shape, hf), prod, 0.0), axis=-1, keepdims=True)
                ds_halves.append((pr * (dp[:, hf * 2 * CHUNK:(hf + 1) * 2 * CHUNK] - delta) * scale).astype(BF))
                t = -ps * delta
                for pp, p in enumerate(pairs):
                    dsink = dsink + jnp.where(lane == 2 * p + hf, t[pp * CHUNK:(pp + 1) * CHUNK], 0.0)
            dsc = jnp.concatenate(ds_halves, axis=1)
            dq = jnp.dot(dsc, kblk, preferred_element_type=F32)
            for pp, p in enumerate(pairs):
                dq_ref[:, p * 128:(p + 1) * 128] = dq[pp * CHUNK:(pp + 1) * CHUNK]
            dk2 = dk2 + _b_fold(lax.dot_general(dsc, qs, tn, preferred_element_type=F32), g)
            dv2 = dv2 + _b_fold(lax.dot_general(pcat, do_b, tn, preferred_element_type=F32), g)
        ds_ref[...] += dsink
        cur = pl.ds(pl.multiple_of(n * CHUNK, CHUNK), CHUNK)
        dk_ref[cur, :] += dk2[CHUNK:]
        dv_ref[cur, :] += dv2[CHUNK:]

        @pl.when(n > 0)
        def _():
            prv = pl.ds(pl.multiple_of((n - 1) * CHUNK, CHUNK), CHUNK)
            dk_ref[prv, :] += dk2[:CHUNK]
            dv_ref[prv, :] += dv2[:CHUNK]

    full = pl.BlockSpec((S, 128), lambda n: (0, 0))
    return pl.pallas_call(
        _hide(body, 7, len(after)), name="b_attn_bwd", grid=(S // CHUNK,),
        in_specs=[pl.BlockSpec(memory_space=pltpu.SMEM), pl.BlockSpec((CHUNK, B_WIDTH), lambda n: (n, 0))] + _b_kv_specs(S)
        + [pl.BlockSpec((CHUNK, B_WIDTH), lambda n: (n, 0))] + _hidden_specs(after),
        out_specs=[pl.BlockSpec((CHUNK, B_WIDTH), lambda n: (n, 0)), full, full, pl.BlockSpec((CHUNK, 128), lambda n: (0, 0))],
        out_shape=[jax.ShapeDtypeStruct((S, B_WIDTH), F32), jax.ShapeDtypeStruct((S, 128), F32), jax.ShapeDtypeStruct((S, 128), F32),
                   jax.ShapeDtypeStruct((CHUNK, 128), F32)],
        compiler_params=_params(("arbitrary",)),
    )(sinks, qn, kn, kn, proj, proj, dy, *after)


def _c_block(q, k, v, gq, gk):
    qn = q * lax.rsqrt(jnp.mean(q * q, axis=-1, keepdims=True) + EPS) * gq
    kn = k * lax.rsqrt(jnp.mean(k * k, axis=-1, keepdims=True) + EPS) * gk
    s = lax.dot_general(qn.astype(BF), kn.astype(BF), (((1,), (1,)), ((), ())), preferred_element_type=F32) * (C_HEAD_DIM ** -0.5)
    p = jax.nn.softmax(s, axis=-1)
    return jnp.dot(p.astype(BF), v.astype(BF), preferred_element_type=F32)


def _c_specs(S, M, tq):
    q_col = (2 * A_WIDTH + B_WIDTH + 2 * B_KV_WIDTH) // 128
    return [pl.BlockSpec((tq, 128), lambda h, i: (i, q_col + h)), pl.BlockSpec((M, 128), lambda h, i: (0, h)),
            pl.BlockSpec((M, 128), lambda h, i: (0, C_HEADS + h)), pl.BlockSpec((1, 128), lambda h, i: (0, 0)),
            pl.BlockSpec((1, 128), lambda h, i: (0, 0))]


def _c_fwd(proj, kv, gq, gk):
    S, M = proj.shape[0], kv.shape[0]
    tq = _pick(S, (512,))

    def body(q_ref, k_ref, v_ref, gq_ref, gk_ref, y_ref):
        y_ref[...] = _c_block(q_ref[...], k_ref[...], v_ref[...], gq_ref[...], gk_ref[...]).astype(BF)

    return pl.pallas_call(
        body, name="c_fwd", grid=(C_HEADS, S // tq), in_specs=_c_specs(S, M, tq),
        out_specs=pl.BlockSpec((tq, 128), lambda h, i: (i, h)),
        out_shape=jax.ShapeDtypeStruct((S, C_WIDTH), BF), compiler_params=_params(("parallel", "parallel")),
    )(proj, kv, kv, gq, gk)


def _c_bwd(proj, kv, gq, gk, dy):
    S, M = proj.shape[0], kv.shape[0]
    tq = _pick(S, (512,))

    def body(q_ref, k_ref, v_ref, gq_ref, gk_ref, dy_ref, dq_ref, dk_ref, dv_ref, dgq_ref, dgk_ref):
        i = pl.program_id(1)
        _, vjp = jax.vjp(_c_block, q_ref[...], k_ref[...], v_ref[...], gq_ref[...], gk_ref[...])
        dq, dk, dv, dgq, dgk = vjp(dy_ref[...])
        dq_ref[...] = dq.astype(BF)

        @pl.when(i == 0)
        def _():
            dk_ref[...] = dk
            dv_ref[...] = dv
            dgq_ref[...] = dgq
            dgk_ref[...] = dgk

        @pl.when(i > 0)
        def _():
            dk_ref[...] += dk
            dv_ref[...] += dv
            dgq_ref[...] += dgq
            dgk_ref[...] += dgk

    return pl.pallas_call(
        body, name="c_bwd", grid=(C_HEADS, S // tq),
        in_specs=_c_specs(S, M, tq) + [pl.BlockSpec((tq, 128), lambda h, i: (i, h))],
        out_specs=[pl.BlockSpec((tq, 128), lambda h, i: (i, h)), pl.BlockSpec((M, 128), lambda h, i: (0, h)),
                   pl.BlockSpec((M, 128), lambda h, i: (0, h)), pl.BlockSpec((None, 1, 128), lambda h, i: (h, 0, 0)),
                   pl.BlockSpec((None, 1, 128), lambda h, i: (h, 0, 0))],
        out_shape=[jax.ShapeDtypeStruct((S, C_WIDTH), BF), jax.ShapeDtypeStruct((M, C_WIDTH), F32), jax.ShapeDtypeStruct((M, C_WIDTH), F32),
                   jax.ShapeDtypeStruct((C_HEADS, 1, 128), F32), jax.ShapeDtypeStruct((C_HEADS, 1, 128), F32)],
        compiler_params=_params(("parallel", "arbitrary")),
    )(proj, kv, kv, gq, gk, dy)


def _merge_specs(S, D, tm, tn, ks):
    off = GATE_OFF // tn
    nd = D // tn
    gates = [pl.BlockSpec((tm, tn), functools.partial(lambda b, m, n: (m, off + b * nd + n), b)) for b in range(3)]
    ys = [pl.BlockSpec((tm, k), lambda m, n: (m, 0)) for k in ks]
    ws = [pl.BlockSpec((None, k, tn), lambda m, n: (n, 0, 0)) for k in ks]
    return gates, ys, ws


def _merge_fwd(proj, ys, ws):
    S = proj.shape[0]
    tn = ws[0].shape[2]
    D = N_DEV * tn
    ks = [w.shape[1] for w in ws]
    tm = _pick(S, (1024,))
    gates, y_specs, w_specs = _merge_specs(S, D, tm, tn, ks)

    def body(ga_ref, gb_ref, gc_ref, ya_ref, yb_ref, yc_ref, wa_ref, wb_ref, wc_ref, m_ref, za_ref, zb_ref, zc_ref):
        acc = None
        for g_ref, y_ref, w_ref, z_ref in ((ga_ref, ya_ref, wa_ref, za_ref), (gb_ref, yb_ref, wb_ref, zb_ref),
                                           (gc_ref, yc_ref, wc_ref, zc_ref)):
            z = jnp.dot(y_ref[...], w_ref[...], preferred_element_type=F32)
            z_ref[...] = z.astype(BF)
            t = jax.nn.sigmoid(g_ref[...]) * z
            acc = t if acc is None else acc + t
        m_ref[...] = acc.astype(BF)

    tile = pl.BlockSpec((tm, tn), lambda m, n: (m, n))
    return pl.pallas_call(
        body, name="merge_fwd", grid=(S // tm, D // tn), in_specs=gates + y_specs + w_specs,
        out_specs=[tile, tile, tile, tile], out_shape=[jax.ShapeDtypeStruct((S, D), BF)] * 4,
        compiler_params=_params(("parallel", "parallel")),
    )(proj, proj, proj, *ys, *ws)


def _merge_bwd(proj, zs, dm, ws, after=()):
    S = proj.shape[0]
    tn = ws[0].shape[2]
    D = N_DEV * tn
    ks = [w.shape[1] for w in ws]
    tm = _pick(S, (512,))
    gates, _, w_specs = _merge_specs(S, D, tm, tn, ks)
    nt = (((1,), (1,)), ((), ()))

    def body(ga_ref, gb_ref, gc_ref, za_ref, zb_ref, zc_ref, dm_ref, wa_ref, wb_ref, wc_ref,
             dza_ref, dzb_ref, dzc_ref, dga_ref, dgb_ref, dgc_ref, dya_ref, dyb_ref, dyc_ref):
        n = pl.program_id(1)
        dmv = dm_ref[...]
        for g_ref, z_ref, w_ref, dz_ref, dg_ref, dy_ref in (
                (ga_ref, za_ref, wa_ref, dza_ref, dga_ref, dya_ref), (gb_ref, zb_ref, wb_ref, dzb_ref, dgb_ref, dyb_ref),
                (gc_ref, zc_ref, wc_ref, dzc_ref, dgc_ref, dyc_ref)):
            sg = jax.nn.sigmoid(g_ref[...])
            dz = (sg * dmv).astype(BF)
            dz_ref[...] = dz
            dg_ref[...] = (dmv * z_ref[...].astype(F32) * sg * (1.0 - sg)).astype(BF)
            part = lax.dot_general(dz, w_ref[...], nt, preferred_element_type=F32)

            @pl.when(n == 0)
            def _():
                dy_ref[...] = part

            @pl.when(n > 0)
            def _():
                dy_ref[...] += part

    tile = pl.BlockSpec((tm, tn), lambda m, n: (m, n))
    dys = [pl.BlockSpec((tm, k), lambda m, n: (m, 0)) for k in ks]
    return pl.pallas_call(
        _hide(body, 10, len(after)), name="merge_bwd", grid=(S // tm, D // tn),
        in_specs=gates + [tile, tile, tile, tile] + w_specs + _hidden_specs(after),
        out_specs=[tile] * 6 + dys,
        out_shape=[jax.ShapeDtypeStruct((S, D), BF)] * 6 + [jax.ShapeDtypeStruct((S, k), F32) for k in ks],
        compiler_params=_params(("parallel", "arbitrary")),
    )(proj, proj, proj, *zs, dm, *ws, *after)


PAD = 8


def _stage_shift_down(us_ref, u_ref):
    S = u_ref.shape[1]
    us_ref[:, 0:PAD, :] = jnp.zeros((2, PAD, us_ref.shape[2]), F32)
    us_ref[:, PAD:S + PAD, :] = u_ref[...].astype(F32)


ROWS = 32


def _conv3(us_ref, part, r0, w, b):
    return (us_ref[part, pl.ds(r0 + PAD, ROWS), :] * w[2:3] + us_ref[part, pl.ds(r0 + PAD - 1, ROWS), :] * w[1:2]
            + us_ref[part, pl.ds(r0 + PAD - 2, ROWS), :] * w[0:1] + b)


def _ffn_specs(S, F, tc, c):
    per = c // tc

    def w_spec(half):
        return pl.BlockSpec((None, 3, tc), lambda j: (half * (N_DEV // 2) + j // per, 0, j % per))

    return [pl.BlockSpec((2, S, tc), lambda j: (0, 0, j)), w_spec(0), w_spec(1), pl.BlockSpec((2, 1, tc), lambda j: (0, 0, j))]


def _ffn_tile(F, c):
    tc = 128
    if c % tc or F % tc:
        raise ValueError(f"ffn tile {tc} does not divide {c}, {F}")
    return tc


def _ffn_act_fwd(up3, cws, cb3):
    _, S, F = up3.shape
    c = cws.shape[2]
    tc = _ffn_tile(F, c)

    def body(u_ref, wa_ref, wb_ref, b_ref, o_ref, us_ref):
        _stage_shift_down(us_ref, u_ref)
        wa, wb, ba, bb = wa_ref[...], wb_ref[...], b_ref[0], b_ref[1]

        def step(i, carry):
            r0 = pl.multiple_of(i * ROWS, ROWS)
            ca = _conv3(us_ref, 0, r0, wa, ba)
            cb = _conv3(us_ref, 1, r0, wb, bb)
            o_ref[pl.ds(r0, ROWS), :] = (ca * jax.nn.sigmoid(ca) * cb).astype(BF)
            return carry

        lax.fori_loop(0, S // ROWS, step, 0, unroll=4)

    return pl.pallas_call(
        body, name="ffn_act_fwd", grid=(F // tc,), in_specs=_ffn_specs(S, F, tc, c),
        out_specs=pl.BlockSpec((S, tc), lambda j: (0, j)), out_shape=jax.ShapeDtypeStruct((S, F), BF),
        scratch_shapes=[pltpu.VMEM((2, S + PAD, tc), F32)],
        compiler_params=_params(("parallel",)),
    )(up3, cws, cws, cb3)


def _ffn_act_bwd(up3, cws, cb3, dact, after=()):
    _, S, F = up3.shape
    c = cws.shape[2]
    tc = _ffn_tile(F, c)

    def body(u_ref, wa_ref, wb_ref, b_ref, da_ref, du_ref, dw_ref, db_ref, us_ref, dcs_ref):
        _stage_shift_down(us_ref, u_ref)
        ws = (wa_ref[...], wb_ref[...])
        ba, bb = b_ref[0], b_ref[1]
        dcs_ref[:, S:S + PAD, :] = jnp.zeros((2, PAD, tc), F32)

        def conv_grads(i, carry):
            r0 = pl.multiple_of(i * ROWS, ROWS)
            ca = _conv3(us_ref, 0, r0, ws[0], ba)
            cb = _conv3(us_ref, 1, r0, ws[1], bb)
            sg = jax.nn.sigmoid(ca)
            dav = da_ref[pl.ds(r0, ROWS), :].astype(F32)
            dcs_ref[0, pl.ds(r0, ROWS), :] = dav * cb * sg * (1.0 + ca * (1.0 - sg))
            dcs_ref[1, pl.ds(r0, ROWS), :] = dav * ca * sg
            return carry

        lax.fori_loop(0, S // ROWS, conv_grads, 0, unroll=4)

        def fold(v):
            return jnp.sum(v.reshape(ROWS // 8, 8, tc), axis=0)

        def input_grads(i, acc):
            r0 = pl.multiple_of(i * ROWS, ROWS)
            new = []
            for part in range(2):
                w = ws[part]
                dc = dcs_ref[part, pl.ds(r0, ROWS), :]
                dc1 = dcs_ref[part, pl.ds(r0 + 1, ROWS), :]
                dc2 = dcs_ref[part, pl.ds(r0 + 2, ROWS), :]
                u = us_ref[part, pl.ds(r0 + PAD, ROWS), :]
                du_ref[part, pl.ds(r0, ROWS), :] = (dc * w[2:3] + dc1 * w[1:2] + dc2 * w[0:1]).astype(BF)
                sums = (fold(dc2 * u), fold(dc1 * u), fold(dc * u), fold(dc))
                new += [a + s for a, s in zip(acc[4 * part:4 * part + 4], sums)]
            return tuple(new)

        acc = lax.fori_loop(0, S // ROWS, input_grads, tuple(jnp.zeros((8, tc), F32) for _ in range(8)), unroll=4)
        for part in range(2):
            for j in range(3):
                dw_ref[part, j:j + 1, :] = jnp.sum(acc[4 * part + j], axis=0, keepdims=True)
            db_ref[part] = jnp.sum(acc[4 * part + 3], axis=0, keepdims=True)

    return pl.pallas_call(
        _hide(body, 5, len(after)), name="ffn_act_bwd", grid=(F // tc,),
        in_specs=_ffn_specs(S, F, tc, c) + [pl.BlockSpec((S, tc), lambda j: (0, j))] + _hidden_specs(after),
        out_specs=[pl.BlockSpec((2, S, tc), lambda j: (0, 0, j)), pl.BlockSpec((2, 3, tc), lambda j: (0, 0, j)),
                   pl.BlockSpec((2, 1, tc), lambda j: (0, 0, j))],
        out_shape=[jax.ShapeDtypeStruct((2, S, F), BF), jax.ShapeDtypeStruct((2, 3, F), F32), jax.ShapeDtypeStruct((2, 1, F), F32)],
        scratch_shapes=[pltpu.VMEM((2, S + PAD, tc), F32), pltpu.VMEM((2, S + PAD, tc), F32)],
        compiler_params=_params(("parallel",)),
    )(up3, cws, cws, cb3, dact, *after)


def _residual_rms(a, w, x, g, name, tm=512):
    S, K = a.shape
    D = w.shape[1]
    tm = _pick(S, (tm,))

    def body(a_ref, w_ref, x_ref, g_ref, x1_ref, h_ref, r_ref):
        x1 = jnp.dot(a_ref[...], w_ref[...], preferred_element_type=F32) + x_ref[...]
        r = lax.rsqrt(jnp.mean(x1 * x1, axis=-1, keepdims=True) + EPS)
        x1_ref[...] = x1
        h_ref[...] = (x1 * r * g_ref[...]).astype(BF)
        r_ref[...] = r

    row = pl.BlockSpec((tm, D), lambda i: (i, 0))
    return pl.pallas_call(
        body, name=name, grid=(S // tm,),
        in_specs=[pl.BlockSpec((tm, K), lambda i: (i, 0)), pl.BlockSpec((K, D), lambda i: (0, 0)), row, pl.BlockSpec((1, D), lambda i: (0, 0))],
        out_specs=[row, row, pl.BlockSpec((tm, 1), lambda i: (i, 0))],
        out_shape=[jax.ShapeDtypeStruct((S, D), F32), jax.ShapeDtypeStruct((S, D), BF), jax.ShapeDtypeStruct((S, 1), F32)],
        compiler_params=_params(("parallel",)),
    )(a, w, x, g)


def _out_loss(act, w_down, x1, target, tm=512, tn=1024, tk=1408):
    S, F = act.shape
    D = w_down.shape[1]
    tm, tn, tk = _pick(S, (tm,)), _pick(D, (tn,)), _pick(F, (tk,))
    nm, nn, nk = S // tm, D // tn, F // tk

    def body(a_ref, b_ref, x_ref, t_ref, dy_ref, dyb_ref, l_ref, acc):
        m, n, k = pl.program_id(0), pl.program_id(1), pl.program_id(2)

        @pl.when((m == 0) & (n == 0) & (k == 0))
        def _():
            l_ref[...] = jnp.zeros_like(l_ref)

        @pl.when(k == 0)
        def _():
            acc[...] = jnp.zeros_like(acc)

        acc[...] += jnp.dot(a_ref[...], b_ref[...], preferred_element_type=F32)

        @pl.when(k == nk - 1)
        def _():
            e = acc[...] + x_ref[...] - t_ref[...]
            dy = e * (1.0 / D)
            dy_ref[...] = dy
            dyb_ref[...] = dy.astype(BF)
            l_ref[...] += jnp.sum(jnp.sum(e * e, axis=-1, keepdims=True), axis=0, keepdims=True) * (0.5 / D)

    tile = pl.BlockSpec((tm, tn), lambda m, n, k: (m, n))
    return pl.pallas_call(
        body, name="mm_y_loss", grid=(nm, nn, nk),
        in_specs=[pl.BlockSpec((tm, tk), lambda m, n, k: (m, k)), pl.BlockSpec((tk, tn), lambda m, n, k: (k, n)), tile, tile],
        out_specs=[tile, tile, pl.BlockSpec((8, 128), lambda m, n, k: (0, 0))],
        out_shape=[jax.ShapeDtypeStruct((S, D), F32), jax.ShapeDtypeStruct((S, D), BF), jax.ShapeDtypeStruct((8, 128), F32)],
        scratch_shapes=[pltpu.VMEM((tm, tn), F32)],
        compiler_params=_params(("arbitrary", "arbitrary", "arbitrary")),
    )(act, w_down, x1, target)


ANY = pl.BlockSpec(memory_space=pl.ANY)


def _allgather(shards, name):
    n = len(shards)

    def body(*refs):
        ins, outs = refs[:n], refs[n:2 * n]
        send_sems, recv_sems, local_sems = refs[2 * n:]
        x, y, c = lax.axis_index("x"), lax.axis_index("y"), lax.axis_index("c")
        me, sibling = (x, y, c), (x, y, 1 - c)
        chips = [(1 - x, y), (x, 1 - y), (1 - x, 1 - y)]

        def blk(w, px, py, pc):
            return outs[w].at[4 * px + 2 * py + pc]

        def copy(w, k, block, to, src=None):
            return pltpu.make_async_remote_copy(
                src_ref=blk(w, *block) if src is None else src, dst_ref=blk(w, *block),
                send_sem=send_sems.at[w, k], recv_sem=recv_sems.at[w, k], device_id=to, device_id_type=MESH)

        started = []
        mine = []
        for w in range(n):
            mine.append(pltpu.make_async_copy(ins[w], blk(w, *me), local_sems.at[w]))
            mine[-1].start()
            first = [copy(w, 0, me, sibling, src=ins[w])]
            first += [copy(w, 1 + j, me, (*chip, c), src=ins[w]) for j, chip in enumerate(chips)]
            for cp in first:
                cp.start()
            started += first
        for w in range(n):
            for j, chip in enumerate(chips):
                copy(w, 1 + j, (*chip, c), me).wait_recv()
                fwd = copy(w, 4 + j, (*chip, c), sibling)
                fwd.start()
                started.append(fwd)
        for w in range(n):
            copy(w, 0, sibling, me).wait_recv()
            for j, chip in enumerate(chips):
                copy(w, 4 + j, (*chip, 1 - c), me).wait_recv()
        for cp in started:
            cp.wait_send()
        for cp in mine:
            cp.wait()

    whole = pl.BlockSpec(memory_space=pltpu.VMEM)
    outs = pl.pallas_call(
        body, name=name, in_specs=[whole] * n, out_specs=[whole] * n,
        out_shape=[jax.ShapeDtypeStruct((N_DEV,) + s.shape, s.dtype) for s in shards],
        scratch_shapes=[pltpu.SemaphoreType.DMA((n, 7)), pltpu.SemaphoreType.DMA((n, 7)), pltpu.SemaphoreType.DMA((n,))],
    )(*shards)
    return list(outs)


def _allgather_seq(shards, name, collective_id, after=()):
    n = len(shards)
    n_after = len(after)

    halves = [s.shape[0] % 32 == 0 for s in shards]
    n_sem = 8
    to_diagonal = not all(halves)

    def body(*refs):
        ins, outs = refs[:n], refs[n + n_after:2 * n + n_after]
        send_sems, recv_sems, local_sems = refs[2 * n + n_after:]
        x, y, c = lax.axis_index("x"), lax.axis_index("y"), lax.axis_index("c")
        me, sibling = (x, y, c), (x, y, 1 - c)
        x_nb, y_nb, diag = (1 - x, y, c), (x, 1 - y, c), (1 - x, 1 - y, c)
        peers = [sibling, x_nb, y_nb] + ([diag] if to_diagonal else [])
        barrier = pltpu.get_barrier_semaphore()
        for peer in peers:
            pl.semaphore_signal(barrier, inc=1, device_id=peer, device_id_type=MESH)
        pl.semaphore_wait(barrier, len(peers))

        def blk(w, dev, rows=None):
            ref = outs[w].at[4 * dev[0] + 2 * dev[1] + dev[2]]
            return ref if rows is None else ref.at[rows]

        def copy(w, k, block, to, src=None, rows=None):
            return pltpu.make_async_remote_copy(
                src_ref=blk(w, block, rows) if src is None else src, dst_ref=blk(w, block, rows),
                send_sem=send_sems.at[n_sem * w + k], recv_sem=recv_sems.at[n_sem * w + k], device_id=to, device_id_type=MESH)

        def top(w):
            return pl.ds(0, shards[w].shape[0] // 2)

        def bottom(w):
            return pl.ds(shards[w].shape[0] // 2, shards[w].shape[0] // 2)

        started = []
        mine = []
        for w in range(n):
            mine.append(pltpu.make_async_copy(ins[w], blk(w, me), local_sems.at[w]))
            mine[-1].start()
            first = [copy(w, 0, me, sibling, src=ins[w]), copy(w, 1, me, x_nb, src=ins[w]), copy(w, 2, me, y_nb, src=ins[w])]
            if not halves[w]:
                first.append(copy(w, 3, me, diag, src=ins[w]))
            for cp in first:
                cp.start()
            started += first
        for w in range(n):
            copy(w, 1, x_nb, me).wait_recv()
            onward = [copy(w, 5, x_nb, sibling)] + ([copy(w, 3, x_nb, y_nb, rows=top(w))] if halves[w] else [])
            copy(w, 2, y_nb, me).wait_recv()
            onward += [copy(w, 6, y_nb, sibling)] + ([copy(w, 4, y_nb, x_nb, rows=bottom(w))] if halves[w] else [])
            for cp in onward:
                cp.start()
            started += onward
        for w in range(n):
            if halves[w]:
                copy(w, 3, diag, me, rows=top(w)).wait_recv()
                copy(w, 4, diag, me, rows=bottom(w)).wait_recv()
            else:
                copy(w, 3, diag, me).wait_recv()
            fwd = copy(w, 7, diag, sibling)
            fwd.start()
            started.append(fwd)
        for w in range(n):
            for k, dev in ((0, sibling), (5, (1 - x, y, 1 - c)), (6, (x, 1 - y, 1 - c)), (7, (1 - x, 1 - y, 1 - c))):
                copy(w, k, dev, me).wait_recv()
        for cp in started:
            cp.wait_send()
        for cp in mine:
            cp.wait()

    outs = pl.kernel(
        body, name=name, out_type=[jax.ShapeDtypeStruct((N_DEV,) + s.shape, s.dtype) for s in shards],
        mesh=plsc.ScalarSubcoreMesh(axis_name="seq", num_cores=1),
        scratch_types=[pltpu.SemaphoreType.DMA((n_sem * n,)), pltpu.SemaphoreType.DMA((n_sem * n,)), pltpu.SemaphoreType.DMA((n,))],
        compiler_params=pltpu.CompilerParams(collective_id=collective_id),
    )(*shards, *after)
    return list(outs)


def _chip_exchange(sums, name, collective_id):
    n = len(sums)

    def body(*refs):
        ins, outs = refs[:n], refs[n:2 * n]
        send_sems, recv_sems = refs[2 * n:]
        x, y, c = lax.axis_index("x"), lax.axis_index("y"), lax.axis_index("c")
        chips = [(1 - x, y), (x, 1 - y), (1 - x, 1 - y)]
        barrier = pltpu.get_barrier_semaphore()
        for px, py in chips:
            pl.semaphore_signal(barrier, inc=1, device_id=(px, py, c), device_id_type=MESH)
        pl.semaphore_wait(barrier, 3)
        copies = []
        for w in range(n):
            for k, (px, py) in enumerate(chips):
                copies.append(pltpu.make_async_remote_copy(
                    src_ref=ins[w].at[2 * px + py], dst_ref=outs[w].at[k], send_sem=send_sems.at[3 * w + k],
                    recv_sem=recv_sems.at[3 * w + k], device_id=(px, py, c), device_id_type=MESH))
        for cp in copies:
            cp.start()
        for cp in copies:
            cp.wait()

    outs = pl.kernel(
        body, name=name, out_type=[jax.ShapeDtypeStruct((3,) + s.shape[1:], s.dtype) for s in sums],
        mesh=plsc.ScalarSubcoreMesh(axis_name="seq", num_cores=1),
        scratch_types=[pltpu.SemaphoreType.DMA((3 * n,)), pltpu.SemaphoreType.DMA((3 * n,))],
        compiler_params=pltpu.CompilerParams(collective_id=collective_id),
    )(*sums)
    return list(outs)


def _row_tile(r, c, elems=256 * 1024):
    want = max(8, elems // c)
    for t in range(min(want, r) // 8 * 8, 0, -8):
        if r % t == 0:
            return t
    return r


def _pair_add(g4, recv, core, name, after=()):
    _, _, r, c = g4.shape
    tr = _row_tile(r, c, 1024 * 1024)

    def body(core_ref, a_ref, b_ref, o_ref):
        o_ref[...] = (a_ref[...].astype(F32) + b_ref[...].astype(F32)).astype(BF)

    return pl.pallas_call(
        _hide(body, 3, len(after)), name=name,
        grid_spec=pltpu.PrefetchScalarGridSpec(
            num_scalar_prefetch=1, grid=(4, r // tr),
            in_specs=[pl.BlockSpec((None, None, tr, c), lambda p, i, s: (p, s[0], i, 0)),
                      pl.BlockSpec((None, tr, c), lambda p, i, s: (p, i, 0))] + _hidden_specs(after),
            out_specs=pl.BlockSpec((None, tr, c), lambda p, i, s: (p, i, 0))),
        out_shape=jax.ShapeDtypeStruct((4, r, c), BF), compiler_params=_params(("parallel", "parallel")),
    )(core, g4, recv, *after)


def _adam_math(w, g, m, v):
    m = ADAM_B1 * m + (1.0 - ADAM_B1) * g
    v = ADAM_B2 * v + (1.0 - ADAM_B2) * (g * g)
    m_hat = m / (1.0 - ADAM_B1 ** ADAM_STEP)
    v_hat = v / (1.0 - ADAM_B2 ** ADAM_STEP)
    delta = -ADAM_LR * (m_hat / (jnp.sqrt(v_hat) + ADAM_EPS) + ADAM_WD * w)
    return delta, m, v


def _adamw_big(sums, recv, chip, w, m, v, name, after=()):
    r, c = w.shape
    tr = _row_tile(r, c, 512 * 1024)

    def body(chip_ref, s_ref, r_ref, w_ref, m_ref, v_ref, g_out, d_out, m_out, v_out):
        g = s_ref[...].astype(F32) + r_ref[0].astype(F32)
        g = g + r_ref[1].astype(F32)
        g = g + r_ref[2].astype(F32)
        delta, mn, vn = _adam_math(w_ref[...], g, m_ref[...], v_ref[...])
        g_out[...] = g
        d_out[...] = delta
        m_out[...] = mn
        v_out[...] = vn

    row = pl.BlockSpec((tr, c), lambda i, s: (i, 0))
    return pl.pallas_call(
        _hide(body, 6, len(after)), name=name,
        grid_spec=pltpu.PrefetchScalarGridSpec(
            num_scalar_prefetch=1, grid=(r // tr,),
            in_specs=[pl.BlockSpec((None, tr, c), lambda i, s: (s[0], i, 0)), pl.BlockSpec((3, tr, c), lambda i, s: (0, i, 0)),
                      row, row, row] + _hidden_specs(after),
            out_specs=[row, row, row, row]),
        out_shape=[jax.ShapeDtypeStruct((r, c), F32)] * 4, compiler_params=_params(("parallel",)),
    )(chip, sums, recv, w, m, v, *after)


def _adamw_small(parts, ws, ms, vs, extra_parts, name):
    n, ne = len(ws), len(extra_parts)

    def total(p_ref):
        g = p_ref[0]
        for d in range(1, N_DEV):
            g = g + p_ref[d]
        return g

    def body(*refs):
        p_refs, w_refs, m_refs, v_refs = refs[:n], refs[n:2 * n], refs[2 * n:3 * n], refs[3 * n:4 * n]
        e_refs = refs[4 * n:4 * n + ne]
        outs = refs[4 * n + ne:]
        for i in range(n):
            g = total(p_refs[i])
            delta, mn, vn = _adam_math(w_refs[i][...], g, m_refs[i][...], v_refs[i][...])
            outs[4 * i][...] = g
            outs[4 * i + 1][...] = delta
            outs[4 * i + 2][...] = mn
            outs[4 * i + 3][...] = vn
        for i in range(ne):
            outs[4 * n + i][...] = total(e_refs[i])

    out_shape = []
    for w in ws:
        out_shape += [jax.ShapeDtypeStruct(w.shape, F32)] * 4
    out_shape += [jax.ShapeDtypeStruct(e.shape[1:], F32) for e in extra_parts]
    res = pl.pallas_call(body, name=name, out_shape=out_shape,
                         compiler_params=pltpu.CompilerParams(vmem_limit_bytes=VMEM_LIMIT))(*parts, *ws, *ms, *vs, *extra_parts)
    return [res[4 * i:4 * i + 4] for i in range(n)], list(res[4 * n:])


def _adamw_plain(g, w, m, v, name):
    def body(g_ref, w_ref, m_ref, v_ref, d_out, m_out, v_out):
        delta, mn, vn = _adam_math(w_ref[...], g_ref[...], m_ref[...], v_ref[...])
        d_out[...] = delta
        m_out[...] = mn
        v_out[...] = vn

    return pl.pallas_call(body, name=name, out_shape=[jax.ShapeDtypeStruct(w.shape, F32)] * 3)(g, w, m, v)


def kernel(x, mem, positions, g_mix, w_in, g_a_v, w_spatial, b_spatial, g_b_q, g_b_k, sinks, g_mem, w_mem_kv, g_c_q, g_c_k, w_branch_a, w_branch_b, w_branch_c, w_out, g_ffn, w_up, conv_w, conv_b, w_down, loss_target, m_g_mix, m_w_in, m_g_a_v, m_w_spatial, m_b_spatial, m_g_b_q, m_g_b_k, m_sinks, m_g_mem, m_w_mem_kv, m_g_c_q, m_g_c_k, m_w_branch_a, m_w_branch_b, m_w_branch_c, m_w_out, m_g_ffn, m_w_up, m_conv_w, m_conv_b, m_w_down, v_g_mix, v_w_in, v_g_a_v, v_w_spatial, v_b_spatial, v_g_b_q, v_g_b_k, v_sinks, v_g_mem, v_w_mem_kv, v_g_c_q, v_g_c_k, v_w_branch_a, v_w_branch_b, v_w_branch_c, v_w_out, v_g_ffn, v_w_up, v_conv_w, v_conv_b, v_w_down):
    S, D = x.shape[1], x.shape[2]
    M = mem.shape[1]
    F = w_down.shape[1] * N_DEV
    in_cols = w_in.shape[2] * N_DEV
    ax, ay, ac = lax.axis_index("x"), lax.axis_index("y"), lax.axis_index("c")
    core = jnp.reshape(ac, (1,)).astype(jnp.int32)
    chip = jnp.reshape(2 * ax + ay, (1,)).astype(jnp.int32)
    me = 4 * ax + 2 * ay + ac

    x2, mem2, tgt2 = x[0], mem[0], loss_target[0]

    big = dict(w_in=w_in[0].T, w_mem_kv=w_mem_kv[0], w_branch_a=w_branch_a[0], w_branch_b=w_branch_b[0],
               w_branch_c=w_branch_c[0], w_out=w_out[0], w_up=w_up[0], w_down=w_down[0])
    names = list(big)
    cast = {k: big[k].astype(BF) for k in names}
    W = {}
    cb3 = conv_b.reshape(2, 1, F)
    W["w_in"], = _allgather_seq([cast["w_in"]], "ag_seq0", 0)
    w_in_t = W["w_in"].reshape(in_cols, D)
    grp1 = ["w_mem_kv", "w_branch_a", "w_branch_b", "w_branch_c", "w_out"]
    res1 = _allgather_seq([cast[k] for k in grp1] + [conv_w[0]], "ag_seq1", 1, after=(_token((w_in_t,), "tok_w_in"),))
    W.update(zip(grp1, res1))
    cw3 = res1[-1]
    w_kv_f = W["w_mem_kv"].reshape(D, 2 * C_WIDTH)
    w_out_f = W["w_out"].reshape(D, D)

    half = ROPE_DIM // 2
    inv = ROPE_THETA ** (-jnp.arange(half, dtype=F32) / half)
    ang = positions[0].astype(F32)[:, None] * inv
    cos, sin = jnp.cos(ang), jnp.sin(ang)
    one, zero = jnp.ones((S, B_HEAD_DIM - ROPE_DIM), F32), jnp.zeros((S, B_HEAD_DIM - ROPE_DIM), F32)
    z8 = jnp.zeros((S, half), F32)
    ct = jnp.tile(jnp.concatenate([cos, cos, one], axis=1), (1, 2))
    sa = jnp.tile(jnp.concatenate([-sin, z8, zero], axis=1), (1, 2))
    sb = jnp.tile(jnp.concatenate([z8, sin, zero], axis=1), (1, 2))
    gq2, gk2 = jnp.tile(g_b_q, (1, 2)), jnp.tile(g_b_k, (1, 2))
    b_t = b_spatial[0].T

    h, rstd1 = _rms_fwd(x2, g_mix, "rms1_fwd")
    proj = _mm(h, w_in_t, "nt", F32, "mm_proj", tn=1280)
    y_a = _a_fwd(proj, g_a_v, w_spatial[0], b_t)
    W["w_up"], = _allgather_seq([cast["w_up"]], "ag_seq2", 2, after=(_token((W["w_out"], proj), "tok_group1"),))
    qn, kn = _b_pre(proj, gq2, gk2, ct, sa, sb)
    y_b = _b_attn_fwd(qn, kn, proj, sinks)
    mem_h, rstd_m = _rms_fwd(mem2, g_mem, "rmsmem_fwd")
    kv = _mm(mem_h, w_kv_f, "nn", F32, "mm_kv", after=(y_b,))
    y_c = _c_fwd(proj, kv, g_c_q, g_c_k)
    w_branches = [W["w_branch_a"], W["w_branch_b"], W["w_branch_c"]]
    merged, z_a, z_b, z_c = _merge_fwd(proj, [y_a, y_b, y_c], w_branches)
    x1, h2, rstd2 = _residual_rms(merged, w_out_f, x2, g_ffn, "mm_x1_rms2")
    W["w_down"], = _allgather_seq([cast["w_down"]], "ag_seq3", 3, after=(W["w_up"], h2))
    w_down_f = W["w_down"].reshape(F, D)
    up3 = _mm(h2, W["w_up"], "nn", BF, "mm_up", b_stack=True, out_parts=2)
    act = _ffn_act_fwd(up3, cw3, cb3)
    dy, dy_b, loss_acc = _out_loss(act, w_down_f, x1, tgt2)

    reduced = {}

    def as4(g):
        return g.reshape(4, 2, g.shape[1], g.shape[2])

    def finish_group(gi, keys, g4, from_sibling):
        sums = [_pair_add(a, b, core, "rs_add_" + k) for k, a, b in zip(keys, g4, from_sibling)]
        from_chips = _chip_exchange(sums, f"rs_chip{gi}", 4 + gi)
        reduced.update(zip(keys, zip(sums, from_chips)))
        return tuple(sums)

    d_act = _mm(dy_b, w_down_f, "nt", BF, "mm_dact", tn=1408)
    g_down = _mm(act, dy_b, "tn", BF, "mm_gdown", tm=1408)
    d_up3, d_cw3, d_cb3 = _ffn_act_bwd(up3, cw3, cb3, d_act, after=(g_down,))
    grp0 = [as4(g_down.reshape(N_DEV, F // N_DEV, D))]
    g_up, sib0 = _mm(h2, d_up3, "tn", BF, "mm_gup", b_parts=2, out_stack=True, exchange=grp0)
    sums0 = finish_group(0, ["w_down"], grp0, sib0)
    grp1 = [as4(g_up)]
    d_h2, sib1 = _mm(d_up3, W["w_up"], "nt", F32, "mm_dh2", a_parts=2, b_stack=True, tm=2048, after=sums0, exchange=grp1)
    sums1 = finish_group(1, ["w_up"], grp1, sib1)
    dx1, dx1_b, d_g_ffn = _rms_bwd(x1, rstd2, g_ffn, d_h2, dy, "rms2_bwd", after=sums1)
    g_out = _mm(merged, dx1_b, "tn", BF, "mm_gout")
    grp2 = [as4(g_out.reshape(N_DEV, D // N_DEV, D))]
    d_merged, sib2 = _mm(dx1_b, w_out_f, "nt", F32, "mm_dmerged", exchange=grp2)
    sums2 = finish_group(2, ["w_out"], grp2, sib2)
    dz_a, dz_b, dz_c, dga, dgb, dgc, dy_a, dy_b_, dy_c = _merge_bwd(proj, [z_a, z_b, z_c], d_merged, w_branches, after=sums2)
    g_ba = _mm(y_a, dz_a, "tn", BF, "mm_gba", out_stack=True)
    g_bb = _mm(y_b, dz_b, "tn", BF, "mm_gbb", out_stack=True)
    g_bc = _mm(y_c, dz_c, "tn", BF, "mm_gbc", out_stack=True)
    d_uv, d_g_a_v, d_w_s, d_b_t = _a_bwd(proj, g_a_v, w_spatial[0], b_t, dy_a, after=(g_ba, g_bb, g_bc))
    dqn, dkn, dv_b, dsink_rows = _b_attn_bwd(qn, kn, proj, sinks, dy_b_)
    d_qkv, d_gq2, d_gk2 = _b_pre_bwd(proj, gq2, gk2, ct, sa, sb, dqn, dkn, dv_b)
    dq_c, dk_c, dv_c, d_gcq, d_gck = _c_bwd(proj, kv, g_c_q, g_c_k, dy_c)
    dkv_b = jnp.concatenate([dk_c, dv_c], axis=1).astype(BF)
    d_memh = _mm(dkv_b, w_kv_f, "nt", F32, "mm_dmemh")
    g_kv = _mm(mem_h, dkv_b, "tn", BF, "mm_gkv")
    _, _, d_g_mem = _rms_bwd(mem2, rstd_m, g_mem, d_memh, None, "rmsmem_bwd")
    dproj = jnp.concatenate([d_uv, d_qkv, dq_c, dga, dgb, dgc], axis=1)
    grp3 = [as4(g_ba), as4(g_bb), as4(g_bc)]
    g_in, sib3 = _mm(dproj, h, "tn", BF, "mm_gin", tm=1280, exchange=grp3)
    sums3 = finish_group(3, ["w_branch_a", "w_branch_b", "w_branch_c"], grp3, sib3)
    grp4 = [as4(g_in.reshape(N_DEV, in_cols // N_DEV, D)), as4(g_kv.reshape(N_DEV, D // N_DEV, 2 * C_WIDTH))]
    d_h, sib4 = _mm(dproj, w_in_t, "nn", F32, "mm_dh", tm=2048, tk=1280, after=sums3, exchange=grp4)
    sums4 = finish_group(4, ["w_in", "w_mem_kv"], grp4, sib4)
    grad_x, _, d_g_mix = _rms_bwd(x2, rstd1, g_mix, d_h, dx1, "rms1_bwd", after=sums4)

    small_names =["g_mix", "g_a_v", "w_spatial", "b_spatial", "g_b_q", "g_b_k", "sinks", "g_mem", "g_c_q", "g_c_k", "g_ffn", "conv_b"]
    small_w = dict(g_mix=g_mix, g_a_v=g_a_v, w_spatial=w_spatial, b_spatial=b_spatial, g_b_q=g_b_q, g_b_k=g_b_k, sinks=sinks,
                   g_mem=g_mem, g_c_q=g_c_q, g_c_k=g_c_k, g_ffn=g_ffn, conv_b=conv_b)
    small_m = dict(g_mix=m_g_mix, g_a_v=m_g_a_v, w_spatial=m_w_spatial, b_spatial=m_b_spatial, g_b_q=m_g_b_q, g_b_k=m_g_b_k,
                   sinks=m_sinks, g_mem=m_g_mem, g_c_q=m_g_c_q, g_c_k=m_g_c_k, g_ffn=m_g_ffn, conv_b=m_conv_b)
    small_v = dict(g_mix=v_g_mix, g_a_v=v_g_a_v, w_spatial=v_w_spatial, b_spatial=v_b_spatial, g_b_q=v_g_b_q, g_b_k=v_g_b_k,
                   sinks=v_sinks, g_mem=v_g_mem, g_c_q=v_g_c_q, g_c_k=v_g_c_k, g_ffn=v_g_ffn, conv_b=v_conv_b)
    small_g = dict(
        g_mix=d_g_mix, g_a_v=d_g_a_v, w_spatial=d_w_s, b_spatial=d_b_t.T,
        g_b_q=d_gq2.reshape(2, B_HEAD_DIM).sum(0), g_b_k=d_gk2.reshape(2, B_HEAD_DIM).sum(0),
        sinks=dsink_rows.sum(0)[:B_HEADS], g_mem=d_g_mem, g_c_q=d_gcq.sum(0), g_c_k=d_gck.sum(0), g_ffn=d_g_ffn,
        conv_b=d_cb3)
    partial = [small_g[k].reshape(small_w[k].shape) for k in small_names] + [d_cw3, loss_acc[0:1]]
    parts = _allgather(partial, "ag_small")
    n_small = len(small_names)
    small_res, (g_cw3, loss_row) = _adamw_small(parts[:n_small], [small_w[k] for k in small_names], [small_m[k] for k in small_names],
                                                [small_v[k] for k in small_names], parts[n_small:], "adamw_small")
    loss = loss_row[0, 0]
    small_out = dict(zip(small_names, small_res))
    c_cw = 2 * F // N_DEV
    g_cw = lax.dynamic_slice(g_cw3, (me // (N_DEV // 2), 0, (me % (N_DEV // 2)) * c_cw), (1, 3, c_cw))[0]
    cw_res = _adamw_plain(g_cw, conv_w[0], m_conv_w[0], v_conv_w[0], "adamw_conv_w")
    big_out = {"conv_w": [g_cw[None]] + [a[None] for a in cw_res]}

    moments = dict(w_in=(m_w_in, v_w_in), w_mem_kv=(m_w_mem_kv, v_w_mem_kv), w_branch_a=(m_w_branch_a, v_w_branch_a),
                   w_branch_b=(m_w_branch_b, v_w_branch_b), w_branch_c=(m_w_branch_c, v_w_branch_c), w_out=(m_w_out, v_w_out),
                   w_up=(m_w_up, v_w_up), w_down=(m_w_down, v_w_down))
    token = (grad_x, small_res[0][0])
    for k in ["w_down", "w_up", "w_out", "w_branch_a", "w_branch_b", "w_branch_c", "w_mem_kv", "w_in"]:
        s, r = reduced[k]
        mk, vk = moments[k][0][0], moments[k][1][0]
        if k == "w_in":
            res = _adamw_big(s, r, chip, big[k], mk.T, vk.T, "adamw_" + k, after=token)
            big_out[k] = [a.T[None] for a in res]
        else:
            res = _adamw_big(s, r, chip, big[k], mk, vk, "adamw_" + k, after=token)
            big_out[k] = [a[None] for a in res]
        token = (res[0],)

    order = ["g_mix", "w_in", "g_a_v", "w_spatial", "b_spatial", "g_b_q", "g_b_k", "sinks", "g_mem", "w_mem_kv", "g_c_q", "g_c_k",
             "w_branch_a", "w_branch_b", "w_branch_c", "w_out", "g_ffn", "w_up", "conv_w", "conv_b", "w_down"]
    res = {**small_out, **big_out}
    outs = [loss, grad_x[None]]
    for field in range(4):
        outs += [res[k][field] for k in order]
    return tuple(outs)
```

```python
import functools

import jax
import jax.numpy as jnp
from jax import lax
from jax.experimental import pallas as pl
from jax.experimental.pallas import tpu as pltpu
from jax.experimental.pallas import tpu_sc as plsc

F32 = jnp.float32
BF = jnp.bfloat16
EPS = 1e-6
NEG = -1e30

N_DEV = 8
CHUNK = 128
A_GROUPS = 4
A_WIDTH = 512
B_HEADS = 16
B_KV_HEADS = 2
B_HEAD_DIM = 64
B_WIDTH = 1024
B_KV_WIDTH = 128
ROPE_DIM = 16
ROPE_THETA = 500000.0
C_HEADS = 4
C_HEAD_DIM = 128
C_WIDTH = 512
GATE_OFF = 2 * A_WIDTH + B_WIDTH + 2 * B_KV_WIDTH + C_WIDTH

ADAM_LR = 0.001
ADAM_B1 = 0.9
ADAM_B2 = 0.999
ADAM_EPS = 1e-08
ADAM_WD = 0.01
ADAM_STEP = 10

VMEM_LIMIT = 48 * 1024 * 1024
MESH = pl.DeviceIdType.MESH


def _pick(n, prefs):
    for p in prefs:
        if p <= n and n % p == 0:
            return p
    return n


def _params(sem):
    return pltpu.CompilerParams(dimension_semantics=sem, vmem_limit_bytes=VMEM_LIMIT)


def _hide(body, n_seen, n_hidden):
    if not n_hidden:
        return body

    def wrapped(*refs):
        return body(*refs[:n_seen], *refs[n_seen + n_hidden:])

    return wrapped


def _hidden_specs(after):
    return [pl.BlockSpec(memory_space=pl.ANY) for _ in after]


def _token(xs, name):
    def body(*refs):
        refs[-1][...] = jnp.zeros_like(refs[-1])

    return pl.pallas_call(body, name=name, in_specs=_hidden_specs(xs), out_shape=jax.ShapeDtypeStruct((8, 128), F32))(*xs)


def _mm(a, b, mode, out_dtype, name, *, resid=None, b_stack=False, a_parts=0, b_parts=0, out_parts=0,
        out_stack=False, tm=1024, tn=1024, tk=2048, after=(), exchange=(), a_cols=None, out_cols=None, into=None):
    if mode == "nn":
        M = a.shape[-2]
        K = a.shape[-1] * max(a_parts, 1) // (a_cols[1] if a_cols else 1)
        N = b.shape[-1] * (N_DEV if b_stack else 1)
        dims = (((1,), (0,)), ((), ()))
    elif mode == "nt":
        M = a.shape[-2]
        K = a.shape[-1] * max(a_parts, 1) // (a_cols[1] if a_cols else 1)
        N = b.shape[-2]
        dims = (((1,), (1,)), ((), ()))
    else:
        K = a.shape[-2]
        M = a.shape[-1]
        N = b.shape[-1] * max(b_parts, 1)
        dims = (((0,), (0,)), ((), ()))
    if b_stack and mode == "nn":
        tn = b.shape[-1]
    if b_stack and mode == "nt":
        tk = b.shape[-1]
    if out_stack:
        tn = N // N_DEV
    tm, tn, tk = _pick(M, (tm,)), _pick(N, (tn,)), _pick(K, (tk,))
    if M % tm or N % tn or K % tk:
        raise ValueError(f"{name}: tiles {tm},{tn},{tk} do not divide {M},{N},{K}")
    nm, nn, nk = M // tm, N // tn, K // tk
    k_off = a_cols[0] * nk if a_cols else 0
    if out_cols is not None and (out_stack or out_parts or resid is not None):
        raise ValueError(f"{name}: out_cols= needs a plain output")
    n_off = out_cols[0] * nn if out_cols else 0
    n_wide = N * (out_cols[1] if out_cols else 1)

    def parts_idx(t, ntile, parts):
        per = ntile // parts
        return t // per, t % per

    if mode in ("nn", "nt"):
        if a_parts:
            a_spec = pl.BlockSpec((None, tm, tk), lambda m, n, k: (parts_idx(k, nk, a_parts)[0], m, parts_idx(k, nk, a_parts)[1]))
        else:
            a_spec = pl.BlockSpec((tm, tk), lambda m, n, k: (m, k + k_off))
    else:
        a_spec = pl.BlockSpec((tk, tm), lambda m, n, k: (k, m))
    if mode == "nn":
        if b_stack:
            b_spec = pl.BlockSpec((None, tk, tn), lambda m, n, k: (n, k, 0))
        else:
            b_spec = pl.BlockSpec((tk, tn), lambda m, n, k: (k, n))
    elif mode == "nt":
        if b_stack:
            b_spec = pl.BlockSpec((None, tn, tk), lambda m, n, k: (k, n, 0))
        else:
            b_spec = pl.BlockSpec((tn, tk), lambda m, n, k: (n, k))
    else:
        if b_parts:
            b_spec = pl.BlockSpec((None, tk, tn), lambda m, n, k: (parts_idx(n, nn, b_parts)[0], k, parts_idx(n, nn, b_parts)[1]))
        else:
            b_spec = pl.BlockSpec((tk, tn), lambda m, n, k: (k, n))
    if out_stack:
        out_shape = jax.ShapeDtypeStruct((N_DEV, M, tn), out_dtype)
        o_spec = pl.BlockSpec((None, tm, tn), lambda m, n, k: (n, m, 0))
    elif out_parts:
        out_shape = jax.ShapeDtypeStruct((out_parts, M, N // out_parts), out_dtype)
        o_spec = pl.BlockSpec((None, tm, tn), lambda m, n, k: (parts_idx(n, nn, out_parts)[0], m, parts_idx(n, nn, out_parts)[1]))
    else:
        out_shape = jax.ShapeDtypeStruct((M, n_wide), out_dtype)
        o_spec = pl.BlockSpec((tm, tn), lambda m, n, k: (m, n + n_off))
    has_resid = resid is not None
    if into is not None:
        after = tuple(after) + (into,)

    n_ex = len(exchange)
    n_in = 2 + has_resid + len(after)

    def body(*refs):
        a_ref, b_ref = refs[:2]
        r_ref = refs[2] if has_resid else None
        ex_in = refs[n_in:n_in + n_ex]
        o_ref = refs[n_in + n_ex]
        ex_out = refs[n_in + n_ex + 1:n_in + 2 * n_ex + 1]
        scratch = refs[n_in + 2 * n_ex + 1:]
        m_i, n_i, k = pl.program_id(0), pl.program_id(1), pl.program_id(2)

        def pushes():
            send_sems, recv_sems = scratch[-2:]
            x, y, c = lax.axis_index("x"), lax.axis_index("y"), lax.axis_index("c")
            return [pltpu.make_async_remote_copy(
                src_ref=ex_in[w].at[:, 1 - c], dst_ref=ex_out[w], send_sem=send_sems.at[w], recv_sem=recv_sems.at[w],
                device_id=(x, y, 1 - c), device_id_type=MESH) for w in range(n_ex)]

        if n_ex:
            @pl.when((m_i == 0) & (n_i == 0) & (k == 0))
            def _():
                for cp in pushes():
                    cp.start()

        if nk == 1:
            res = lax.dot_general(a_ref[...], b_ref[...], dims, preferred_element_type=F32)
            if has_resid:
                res = res + r_ref[...]
            o_ref[...] = res.astype(o_ref.dtype)
        else:
            acc = scratch[0]

            @pl.when(k == 0)
            def _():
                acc[...] = jnp.zeros_like(acc)

            acc[...] += lax.dot_general(a_ref[...], b_ref[...], dims, preferred_element_type=F32)

            @pl.when(k == nk - 1)
            def _():
                res = acc[...]
                if has_resid:
                    res = res + r_ref[...]
                o_ref[...] = res.astype(o_ref.dtype)

        if n_ex:
            @pl.when((m_i == nm - 1) & (n_i == nn - 1) & (k == nk - 1))
            def _():
                for cp in pushes():
                    cp.wait()

    in_specs = [a_spec, b_spec]
    args = [a, b]
    if has_resid:
        in_specs.append(pl.BlockSpec((tm, tn), lambda m, n, k: (m, n)))
        args.append(resid)
    in_specs += _hidden_specs(after) + _hidden_specs(exchange)
    args += list(after) + list(exchange)
    scratch_shapes = [pltpu.VMEM((tm, tn), F32)] if nk > 1 else []
    aliases = {2 + has_resid + len(after) - 1: 0} if into is not None else {}
    if not n_ex:
        return pl.pallas_call(
            body, name=name, grid=(nm, nn, nk), in_specs=in_specs, out_specs=o_spec, out_shape=out_shape,
            scratch_shapes=scratch_shapes, input_output_aliases=aliases,
            compiler_params=_params(("parallel", "parallel", "arbitrary")),
        )(*args)
    res = pl.pallas_call(
        body, name=name, grid=(nm, nn, nk), in_specs=in_specs, out_specs=[o_spec] + _hidden_specs(exchange),
        out_shape=[out_shape] + [jax.ShapeDtypeStruct((g.shape[0],) + g.shape[2:], g.dtype) for g in exchange],
        scratch_shapes=scratch_shapes + [pltpu.SemaphoreType.DMA((n_ex,)), pltpu.SemaphoreType.DMA((n_ex,))],
        input_output_aliases=aliases, compiler_params=_params(("arbitrary", "arbitrary", "arbitrary")),
    )(*args)
    return res[0], list(res[1:])


def _rms_fwd(x, g, name):
    R, D = x.shape
    tr = _pick(R, (256,))

    def body(x_ref, g_ref, h_ref, r_ref):
        xv = x_ref[...]
        r = lax.rsqrt(jnp.mean(xv * xv, axis=-1, keepdims=True) + EPS)
        h_ref[...] = (xv * r * g_ref[...]).astype(BF)
        r_ref[...] = r

    return pl.pallas_call(
        body, name=name, grid=(R // tr,),
        in_specs=[pl.BlockSpec((tr, D), lambda i: (i, 0)), pl.BlockSpec((1, D), lambda i: (0, 0))],
        out_specs=[pl.BlockSpec((tr, D), lambda i: (i, 0)), pl.BlockSpec((tr, 1), lambda i: (i, 0))],
        out_shape=[jax.ShapeDtypeStruct((R, D), BF), jax.ShapeDtypeStruct((R, 1), F32)],
        compiler_params=_params(("parallel",)),
    )(x, g)


def _rms_bwd(x, r, g, dh, dres, name, after=()):
    R, D = x.shape
    tr = _pick(R, (256,))
    has_res = dres is not None

    def body(*refs):
        if has_res:
            x_ref, r_ref, g_ref, dh_ref, dres_ref, dx_ref, dxb_ref, dg_ref = refs
        else:
            x_ref, r_ref, g_ref, dh_ref, dx_ref, dxb_ref, dg_ref = refs
        i = pl.program_id(0)
        xv, rv, dhv = x_ref[...], r_ref[...], dh_ref[...]
        gy = dhv * g_ref[...]
        c = jnp.sum(xv * gy, axis=-1, keepdims=True)
        dx = rv * gy - xv * (rv * rv * rv) * (c * (1.0 / D))
        if has_res:
            dx = dx + dres_ref[...]
        dx_ref[...] = dx
        dxb_ref[...] = dx.astype(BF)
        part = jnp.sum(dhv * xv * rv, axis=0, keepdims=True)

        @pl.when(i == 0)
        def _():
            dg_ref[...] = part

        @pl.when(i > 0)
        def _():
            dg_ref[...] += part

    row = pl.BlockSpec((tr, D), lambda i: (i, 0))
    in_specs = [row, pl.BlockSpec((tr, 1), lambda i: (i, 0)), pl.BlockSpec((1, D), lambda i: (0, 0)), row]
    args = [x, r, g, dh]
    if has_res:
        in_specs.append(row)
        args.append(dres)
    return pl.pallas_call(
        _hide(body, len(args), len(after)), name=name, grid=(R // tr,), in_specs=in_specs + _hidden_specs(after),
        out_specs=[row, row, pl.BlockSpec((1, D), lambda i: (0, 0))],
        out_shape=[jax.ShapeDtypeStruct((R, D), F32), jax.ShapeDtypeStruct((R, D), BF), jax.ShapeDtypeStruct((1, D), F32)],
        compiler_params=_params(("arbitrary",)),
    )(*args, *after)


def _a_chunk(us, vs, gvs, ws, bs):
    r_i = lax.broadcasted_iota(jnp.int32, (CHUNK, CHUNK), 0)
    c_i = lax.broadcasted_iota(jnp.int32, (CHUNK, CHUNK), 1)
    causal = r_i >= c_i
    vg = [jax.nn.gelu(v) for v in vs]
    ss = sum(jnp.sum(v * v, axis=-1, keepdims=True) for v in vg)
    r = lax.rsqrt(ss * (1.0 / A_WIDTH) + EPS)
    ys = []
    for g in range(A_GROUPS):
        vn = vg[g] * r * gvs[g]
        w = jnp.where(causal, ws[g], 0.0)
        s = jnp.dot(w.astype(BF), vn.astype(BF), preferred_element_type=F32) + bs[g]
        ys.append(jax.nn.gelu(us[g]) * s)
    return ys


def _a_split(u_ref, v_ref, g_ref, w_ref, b_ref):
    sl = [slice(g * 128, (g + 1) * 128) for g in range(A_GROUPS)]
    return ([u_ref[:, s] for s in sl], [v_ref[:, s] for s in sl], [g_ref[:, s] for s in sl],
            [w_ref[g] for g in range(A_GROUPS)], [b_ref[:, g:g + 1] for g in range(A_GROUPS)])


def _a_specs(S):
    return [pl.BlockSpec((CHUNK, A_WIDTH), lambda n: (n, 0)), pl.BlockSpec((CHUNK, A_WIDTH), lambda n: (n, 1)),
            pl.BlockSpec((1, A_WIDTH), lambda n: (0, 0)), pl.BlockSpec((A_GROUPS, CHUNK, CHUNK), lambda n: (0, 0, 0)),
            pl.BlockSpec((CHUNK, A_GROUPS), lambda n: (0, 0))]


def _a_fwd(proj, g_v, w_s, b_t):
    S = proj.shape[0]

    def body(u_ref, v_ref, g_ref, w_ref, b_ref, y_ref):
        ys = _a_chunk(*_a_split(u_ref, v_ref, g_ref, w_ref, b_ref))
        for g in range(A_GROUPS):
            y_ref[:, g * 128:(g + 1) * 128] = ys[g].astype(BF)

    return pl.pallas_call(
        body, name="a_fwd", grid=(S // CHUNK,), in_specs=_a_specs(S),
        out_specs=pl.BlockSpec((CHUNK, A_WIDTH), lambda n: (n, 0)),
        out_shape=jax.ShapeDtypeStruct((S, A_WIDTH), BF), compiler_params=_params(("parallel",)),
    )(proj, proj, g_v, w_s, b_t)


def _a_bwd(proj, g_v, w_s, b_t, dy, after=()):
    S = proj.shape[0]

    def body(u_ref, v_ref, g_ref, w_ref, b_ref, dy_ref, duv_ref, dg_ref, dw_ref, db_ref):
        n = pl.program_id(0)
        dys = [dy_ref[:, g * 128:(g + 1) * 128] for g in range(A_GROUPS)]
        _, vjp = jax.vjp(_a_chunk, *_a_split(u_ref, v_ref, g_ref, w_ref, b_ref))
        dus, dvs, dgs, dws, dbs = vjp(dys)

        @pl.when(n == 0)
        def _():
            dg_ref[...] = jnp.zeros_like(dg_ref)
            dw_ref[...] = jnp.zeros_like(dw_ref)
            db_ref[...] = jnp.zeros_like(db_ref)

        for g in range(A_GROUPS):
            duv_ref[:, g * 128:(g + 1) * 128] = dus[g].astype(BF)
            duv_ref[:, A_WIDTH + g * 128:A_WIDTH + (g + 1) * 128] = dvs[g].astype(BF)
            dg_ref[:, g * 128:(g + 1) * 128] += dgs[g]
            dw_ref[g] += dws[g]
            db_ref[:, g:g + 1] += dbs[g]

    return pl.pallas_call(
        _hide(body, 6, len(after)), name="a_bwd", grid=(S // CHUNK,),
        in_specs=_a_specs(S) + [pl.BlockSpec((CHUNK, A_WIDTH), lambda n: (n, 0))] + _hidden_specs(after),
        out_specs=[pl.BlockSpec((CHUNK, 2 * A_WIDTH), lambda n: (n, 0)), pl.BlockSpec((1, A_WIDTH), lambda n: (0, 0)),
                   pl.BlockSpec((A_GROUPS, CHUNK, CHUNK), lambda n: (0, 0, 0)), pl.BlockSpec((CHUNK, A_GROUPS), lambda n: (0, 0))],
        out_shape=[jax.ShapeDtypeStruct((S, 2 * A_WIDTH), BF), jax.ShapeDtypeStruct((1, A_WIDTH), F32),
                   jax.ShapeDtypeStruct((A_GROUPS, CHUNK, CHUNK), F32), jax.ShapeDtypeStruct((CHUNK, A_GROUPS), F32)],
        compiler_params=_params(("arbitrary",)),
    )(proj, proj, g_v, w_s, b_t, dy, *after)


def _half_mask(shape, which):
    lane = lax.broadcasted_iota(jnp.int32, shape, len(shape) - 1)
    return (lane >= 64) == (which == 1)


def _pair_norm_rope(x, g, ct, sa, sb):
    lo = _half_mask(x.shape, 0)
    x2 = x * x
    ss_lo = jnp.sum(jnp.where(lo, x2, 0.0), axis=-1, keepdims=True)
    ss_hi = jnp.sum(jnp.where(lo, 0.0, x2), axis=-1, keepdims=True)
    r = jnp.where(lo, lax.rsqrt(ss_lo * (1.0 / B_HEAD_DIM) + EPS), lax.rsqrt(ss_hi * (1.0 / B_HEAD_DIM) + EPS))
    xr = x * r
    xn = xr * g
    out = xn * ct + pltpu.roll(xn, 120, 1) * sa + pltpu.roll(xn, 8, 1) * sb
    return out, xr, r


def _pair_norm_rope_bwd(x, g, ct, sa, sb, dout):
    lo = _half_mask(x.shape, 0)
    _, xr, r = _pair_norm_rope(x, g, ct, sa, sb)
    dxn = dout * ct + pltpu.roll(dout * sa, 8, 1) + pltpu.roll(dout * sb, 120, 1)
    gy = dxn * g
    t = xr * gy
    c_lo = jnp.sum(jnp.where(lo, t, 0.0), axis=-1, keepdims=True)
    c_hi = jnp.sum(jnp.where(lo, 0.0, t), axis=-1, keepdims=True)
    c = jnp.where(lo, c_lo, c_hi)
    dx = r * (gy - xr * c * (1.0 / B_HEAD_DIM))
    dg = jnp.sum(dxn * xr, axis=0, keepdims=True)
    return dx, dg


def _b_pre(proj, gq2, gk2, ct, sa, sb):
    S = proj.shape[0]
    tr = _pick(S, (256,))
    n_pair = B_WIDTH // 128

    def body(q_ref, k_ref, gq_ref, gk_ref, ct_ref, sa_ref, sb_ref, qn_ref, kn_ref):
        ct_v, sa_v, sb_v = ct_ref[...], sa_ref[...], sb_ref[...]
        for p in range(n_pair):
            o, _, _ = _pair_norm_rope(q_ref[:, p * 128:(p + 1) * 128], gq_ref[...], ct_v, sa_v, sb_v)
            qn_ref[:, p * 128:(p + 1) * 128] = o.astype(BF)
        o, _, _ = _pair_norm_rope(k_ref[...], gk_ref[...], ct_v, sa_v, sb_v)
        kn_ref[...] = o.astype(BF)

    tab = pl.BlockSpec((tr, 128), lambda i: (i, 0))
    gsp = pl.BlockSpec((1, 128), lambda i: (0, 0))
    return pl.pallas_call(
        body, name="b_pre", grid=(S // tr,),
        in_specs=[pl.BlockSpec((tr, B_WIDTH), lambda i: (i, 1)), pl.BlockSpec((tr, 128), lambda i: (i, 2 * B_WIDTH // 128)),
                  gsp, gsp, tab, tab, tab],
        out_specs=[pl.BlockSpec((tr, B_WIDTH), lambda i: (i, 0)), tab],
        out_shape=[jax.ShapeDtypeStruct((S, B_WIDTH), BF), jax.ShapeDtypeStruct((S, 128), BF)],
        compiler_params=_params(("parallel",)),
    )(proj, proj, gq2, gk2, ct, sa, sb)


def _b_pre_bwd(proj, gq2, gk2, ct, sa, sb, dqn, dkn, dv):
    S = proj.shape[0]
    tr = _pick(S, (256,))
    n_pair = B_WIDTH // 128

    def body(q_ref, k_ref, gq_ref, gk_ref, ct_ref, sa_ref, sb_ref, dqn_ref, dkn_ref, dv_ref, dqkv_ref, dgq_ref, dgk_ref):
        i = pl.program_id(0)
        ct_v, sa_v, sb_v = ct_ref[...], sa_ref[...], sb_ref[...]
        dgq = jnp.zeros((1, 128), F32)
        for p in range(n_pair):
            sl = slice(p * 128, (p + 1) * 128)
            dx, dg = _pair_norm_rope_bwd(q_ref[:, sl], gq_ref[...], ct_v, sa_v, sb_v, dqn_ref[:, sl])
            dqkv_ref[:, sl] = dx.astype(BF)
            dgq = dgq + dg
        dx, dgk = _pair_norm_rope_bwd(k_ref[...], gk_ref[...], ct_v, sa_v, sb_v, dkn_ref[...])
        dqkv_ref[:, B_WIDTH:B_WIDTH + 128] = dx.astype(BF)
        dqkv_ref[:, B_WIDTH + 128:B_WIDTH + 256] = dv_ref[...].astype(BF)

        @pl.when(i == 0)
        def _():
            dgq_ref[...] = dgq
            dgk_ref[...] = dgk

        @pl.when(i > 0)
        def _():
            dgq_ref[...] += dgq
            dgk_ref[...] += dgk

    tab = pl.BlockSpec((tr, 128), lambda i: (i, 0))
    gsp = pl.BlockSpec((1, 128), lambda i: (0, 0))
    return pl.pallas_call(
        body, name="b_pre_bwd", grid=(S // tr,),
        in_specs=[pl.BlockSpec((tr, B_WIDTH), lambda i: (i, 1)), pl.BlockSpec((tr, 128), lambda i: (i, 2 * B_WIDTH // 128)),
                  gsp, gsp, tab, tab, tab, pl.BlockSpec((tr, B_WIDTH), lambda i: (i, 0)), tab, tab],
        out_specs=[pl.BlockSpec((tr, B_WIDTH + 256), lambda i: (i, 0)), gsp, gsp],
        out_shape=[jax.ShapeDtypeStruct((S, B_WIDTH + 256), BF), jax.ShapeDtypeStruct((1, 128), F32), jax.ShapeDtypeStruct((1, 128), F32)],
        compiler_params=_params(("arbitrary",)),
    )(proj, proj, gq2, gk2, ct, sa, sb, dqn, dkn, dv)


def _b_dup(x2, g):
    d = jnp.where(_half_mask(x2.shape, g), x2, 0.0)
    return (d + pltpu.roll(d, 64, 1)).astype(BF)


PAIRS_PER_GROUP = B_HEADS // B_KV_HEADS // 2
GROUP_ROWS = PAIRS_PER_GROUP * CHUNK


def _b_valid(n):
    row = lax.broadcasted_iota(jnp.int32, (GROUP_ROWS, 2 * CHUNK), 0) & (CHUNK - 1)
    col = lax.broadcasted_iota(jnp.int32, (GROUP_ROWS, 2 * CHUNK), 1)
    rel = row + CHUNK - col
    return (rel >= 0) & (rel < CHUNK) & ((col >= CHUNK) | (n > 0))


def _b_blocks(x2, g):
    xd = _b_dup(x2, g)
    lo = _half_mask(xd.shape, 0)
    zero = jnp.zeros_like(xd)
    return jnp.concatenate([jnp.where(lo, xd, zero), jnp.where(lo, zero, xd)], axis=0)


def _b_sink_col(s_ref, g, hf):
    rb = lax.broadcasted_iota(jnp.int32, (GROUP_ROWS, 1), 0) // CHUNK
    col = jnp.zeros((GROUP_ROWS, 1), F32)
    for pp in range(PAIRS_PER_GROUP):
        col = jnp.where(rb == pp, s_ref[0, 2 * (g * PAIRS_PER_GROUP + pp) + hf], col)
    return col


def _b_probs(qs, kblk, valid, sinks):
    s = lax.dot_general(qs, kblk, (((1,), (1,)), ((), ())), preferred_element_type=F32) * (B_HEAD_DIM ** -0.5)
    out = []
    for hf in range(2):
        sh = jnp.where(valid, s[:, hf * 2 * CHUNK:(hf + 1) * 2 * CHUNK], NEG)
        m = jnp.maximum(jnp.max(sh, axis=-1, keepdims=True), sinks[hf])
        e = jnp.exp(sh - m)
        es = jnp.exp(sinks[hf] - m)
        inv = 1.0 / (jnp.sum(e, axis=-1, keepdims=True) + es)
        out.append((e * inv, es * inv))
    return out


def _b_fold(acc, g):
    lo = _half_mask((2 * CHUNK, 128), 0)
    t = jnp.where(lo, acc[:2 * CHUNK], 0.0) + jnp.where(lo, 0.0, acc[2 * CHUNK:])
    return jnp.where(_half_mask((2 * CHUNK, 128), g), t + pltpu.roll(t, 64, 1), 0.0)


def _b_kv_specs(S):
    prev = lambda n: (jnp.maximum(n - 1, 0), 0)
    cur = lambda n: (n, 0)
    v_col = (2 * B_WIDTH + B_KV_WIDTH) // 128
    return [pl.BlockSpec((CHUNK, 128), prev), pl.BlockSpec((CHUNK, 128), cur),
            pl.BlockSpec((CHUNK, 128), lambda n: (jnp.maximum(n - 1, 0), v_col)), pl.BlockSpec((CHUNK, 128), lambda n: (n, v_col))]


def _b_attn_fwd(qn, kn, proj, sinks):
    S = qn.shape[0]

    def body(s_ref, q_ref, kp_ref, kc_ref, vp_ref, vc_ref, y_ref):
        n = pl.program_id(0)
        valid = _b_valid(n)
        k2 = jnp.concatenate([kp_ref[...], kc_ref[...]], axis=0).astype(F32)
        v2 = jnp.concatenate([vp_ref[...], vc_ref[...]], axis=0)
        for g in range(B_KV_HEADS):
            pairs = [g * PAIRS_PER_GROUP + pp for pp in range(PAIRS_PER_GROUP)]
            qs = jnp.concatenate([q_ref[:, p * 128:(p + 1) * 128] for p in pairs], axis=0)
            probs = _b_probs(qs, _b_blocks(k2, g), valid, [_b_sink_col(s_ref, g, hf) for hf in range(2)])
            pcat = jnp.concatenate([probs[0][0].astype(BF), probs[1][0].astype(BF)], axis=1)
            o = jnp.dot(pcat, _b_blocks(v2, g), preferred_element_type=F32)
            for pp, p in enumerate(pairs):
                y_ref[:, p * 128:(p + 1) * 128] = o[pp * CHUNK:(pp + 1) * CHUNK].astype(BF)

    return pl.pallas_call(
        body, name="b_attn_fwd", grid=(S // CHUNK,),
        in_specs=[pl.BlockSpec(memory_space=pltpu.SMEM), pl.BlockSpec((CHUNK, B_WIDTH), lambda n: (n, 0))] + _b_kv_specs(S),
        out_specs=pl.BlockSpec((CHUNK, B_WIDTH), lambda n: (n, 0)),
        out_shape=jax.ShapeDtypeStruct((S, B_WIDTH), BF), compiler_params=_params(("arbitrary",)),
    )(sinks, qn, kn, kn, proj, proj)


def _b_attn_bwd(qn, kn, proj, sinks, dy, after=()):
    S = qn.shape[0]

    def body(s_ref, q_ref, kp_ref, kc_ref, vp_ref, vc_ref, dy_ref, dq_ref, dk_ref, dv_ref, ds_ref):
        n = pl.program_id(0)

        @pl.when(n == 0)
        def _():
            dk_ref[...] = jnp.zeros_like(dk_ref)
            dv_ref[...] = jnp.zeros_like(dv_ref)
            ds_ref[...] = jnp.zeros_like(ds_ref)

        valid = _b_valid(n)
        k2 = jnp.concatenate([kp_ref[...], kc_ref[...]], axis=0).astype(F32)
        v2 = jnp.concatenate([vp_ref[...], vc_ref[...]], axis=0)
        lane = lax.broadcasted_iota(jnp.int32, (CHUNK, 128), 1)
        dk2 = jnp.zeros((2 * CHUNK, 128), F32)
        dv2 = jnp.zeros((2 * CHUNK, 128), F32)
        dsink = jnp.zeros((CHUNK, 128), F32)
        scale = B_HEAD_DIM ** -0.5
        nt = (((1,), (1,)), ((), ()))
        tn = (((0,), (0,)), ((), ()))
        for g in range(B_KV_HEADS):
            pairs = [g * PAIRS_PER_GROUP + pp for pp in range(PAIRS_PER_GROUP)]
            qs = jnp.concatenate([q_ref[:, p * 128:(p + 1) * 128] for p in pairs], axis=0)
            do = jnp.concatenate([dy_ref[:, p * 128:(p + 1) * 128] for p in pairs], axis=0)
            do_b = do.astype(BF)
            kblk, vblk = _b_blocks(k2, g), _b_blocks(v2, g)
            probs = _b_probs(qs, kblk, valid, [_b_sink_col(s_ref, g, hf) for hf in range(2)])
            pcat = jnp.concatenate([probs[0][0].astype(BF), probs[1][0].astype(BF)], axis=1)
            o = jnp.dot(pcat, vblk, preferred_element_type=F32)
            dp = lax.dot_general(do_b, vblk, nt, preferred_element_type=F32)
            prod = do * o
            ds_halves = []
            for hf in range(2):
                pr, ps = probs[hf]
                delta = jnp.sum(jnp.where(_half_mask(prod.shape, hf), prod, 0.0), axis=-1, keepdims=True)
                ds_halves.append((pr * (dp[:, hf * 2 * CHUNK:(hf + 1) * 2 * CHUNK] - delta) * scale).astype(BF))
                t = -ps * delta
                for pp, p in enumerate(pairs):
                    dsink = dsink + jnp.where(lane == 2 * p + hf, t[pp * CHUNK:(pp + 1) * CHUNK], 0.0)
            dsc = jnp.concatenate(ds_halves, axis=1)
            dq = jnp.dot(dsc, kblk, preferred_element_type=F32)
            for pp, p in enumerate(pairs):
                dq_ref[:, p * 128:(p + 1) * 128] = dq[pp * CHUNK:(pp + 1) * CHUNK]
            dk2 = dk2 + _b_fold(lax.dot_general(dsc, qs, tn, preferred_element_type=F32), g)
            dv2 = dv2 + _b_fold(lax.dot_general(pcat, do_b, tn, preferred_element_type=F32), g)
        ds_ref[...] += dsink
        cur = pl.ds(pl.multiple_of(n * CHUNK, CHUNK), CHUNK)
        dk_ref[cur, :] += dk2[CHUNK:]
        dv_ref[cur, :] += dv2[CHUNK:]

        @pl.when(n > 0)
        def _():
            prv = pl.ds(pl.multiple_of((n - 1) * CHUNK, CHUNK), CHUNK)
            dk_ref[prv, :] += dk2[:CHUNK]
            dv_ref[prv, :] += dv2[:CHUNK]

    full = pl.BlockSpec((S, 128), lambda n: (0, 0))
    return pl.pallas_call(
        _hide(body, 7, len(after)), name="b_attn_bwd", grid=(S // CHUNK,),
        in_specs=[pl.BlockSpec(memory_space=pltpu.SMEM), pl.BlockSpec((CHUNK, B_WIDTH), lambda n: (n, 0))] + _b_kv_specs(S)
        + [pl.BlockSpec((CHUNK, B_WIDTH), lambda n: (n, 0))] + _hidden_specs(after),
        out_specs=[pl.BlockSpec((CHUNK, B_WIDTH), lambda n: (n, 0)), full, full, pl.BlockSpec((CHUNK, 128), lambda n: (0, 0))],
        out_shape=[jax.ShapeDtypeStruct((S, B_WIDTH), F32), jax.ShapeDtypeStruct((S, 128), F32), jax.ShapeDtypeStruct((S, 128), F32),
                   jax.ShapeDtypeStruct((CHUNK, 128), F32)],
        compiler_params=_params(("arbitrary",)),
    )(sinks, qn, kn, kn, proj, proj, dy, *after)


def _c_block(q, k, v, gq, gk):
    qn = q * lax.rsqrt(jnp.mean(q * q, axis=-1, keepdims=True) + EPS) * gq
    kn = k * lax.rsqrt(jnp.mean(k * k, axis=-1, keepdims=True) + EPS) * gk
    s = lax.dot_general(qn.astype(BF), kn.astype(BF), (((1,), (1,)), ((), ())), preferred_element_type=F32) * (C_HEAD_DIM ** -0.5)
    p = jax.nn.softmax(s, axis=-1)
    return jnp.dot(p.astype(BF), v.astype(BF), preferred_element_type=F32)


def _c_specs(S, M, tq):
    q_col = (2 * A_WIDTH + B_WIDTH + 2 * B_KV_WIDTH) // 128
    return [pl.BlockSpec((tq, 128), lambda h, i: (i, q_col + h)), pl.BlockSpec((M, 128), lambda h, i: (0, h)),
            pl.BlockSpec((M, 128), lambda h, i: (0, C_HEADS + h)), pl.BlockSpec((1, 128), lambda h, i: (0, 0)),
            pl.BlockSpec((1, 128), lambda h, i: (0, 0))]


def _c_fwd(proj, kv, gq, gk):
    S, M = proj.shape[0], kv.shape[0]
    tq = _pick(S, (512,))

    def body(q_ref, k_ref, v_ref, gq_ref, gk_ref, y_ref):
        y_ref[...] = _c_block(q_ref[...], k_ref[...], v_ref[...], gq_ref[...], gk_ref[...]).astype(BF)

    return pl.pallas_call(
        body, name="c_fwd", grid=(C_HEADS, S // tq), in_specs=_c_specs(S, M, tq),
        out_specs=pl.BlockSpec((tq, 128), lambda h, i: (i, h)),
        out_shape=jax.ShapeDtypeStruct((S, C_WIDTH), BF), compiler_params=_params(("parallel", "parallel")),
    )(proj, kv, kv, gq, gk)


def _c_bwd(proj, kv, gq, gk, dy):
    S, M = proj.shape[0], kv.shape[0]
    tq = _pick(S, (512,))

    def body(q_ref, k_ref, v_ref, gq_ref, gk_ref, dy_ref, dq_ref, dk_ref, dv_ref, dgq_ref, dgk_ref):
        i = pl.program_id(1)
        _, vjp = jax.vjp(_c_block, q_ref[...], k_ref[...], v_ref[...], gq_ref[...], gk_ref[...])
        dq, dk, dv, dgq, dgk = vjp(dy_ref[...])
        dq_ref[...] = dq.astype(BF)

        @pl.when(i == 0)
        def _():
            dk_ref[...] = dk
            dv_ref[...] = dv
            dgq_ref[...] = dgq
            dgk_ref[...] = dgk

        @pl.when(i > 0)
        def _():
            dk_ref[...] += dk
            dv_ref[...] += dv
            dgq_ref[...] += dgq
            dgk_ref[...] += dgk

    return pl.pallas_call(
        body, name="c_bwd", grid=(C_HEADS, S // tq),
        in_specs=_c_specs(S, M, tq) + [pl.BlockSpec((tq, 128), lambda h, i: (i, h))],
        out_specs=[pl.BlockSpec((tq, 128), lambda h, i: (i, h)), pl.BlockSpec((M, 128), lambda h, i: (0, h)),
                   pl.BlockSpec((M, 128), lambda h, i: (0, h)), pl.BlockSpec((None, 1, 128), lambda h, i: (h, 0, 0)),
                   pl.BlockSpec((None, 1, 128), lambda h, i: (h, 0, 0))],
        out_shape=[jax.ShapeDtypeStruct((S, C_WIDTH), BF), jax.ShapeDtypeStruct((M, C_WIDTH), F32), jax.ShapeDtypeStruct((M, C_WIDTH), F32),
                   jax.ShapeDtypeStruct((C_HEADS, 1, 128), F32), jax.ShapeDtypeStruct((C_HEADS, 1, 128), F32)],
        compiler_params=_params(("parallel", "arbitrary")),
    )(proj, kv, kv, gq, gk, dy)


def _merge_specs(S, D, tm, tn, ks):
    off = GATE_OFF // tn
    nd = D // tn
    gates = [pl.BlockSpec((tm, tn), functools.partial(lambda b, m, n: (m, off + b * nd + n), b)) for b in range(3)]
    ys = [pl.BlockSpec((tm, k), lambda m, n: (m, 0)) for k in ks]
    ws = [pl.BlockSpec((None, k, tn), lambda m, n: (n, 0, 0)) for k in ks]
    return gates, ys, ws


def _merge_fwd(proj, ys, ws):
    S = proj.shape[0]
    tn = ws[0].shape[2]
    D = N_DEV * tn
    ks = [w.shape[1] for w in ws]
    tm = _pick(S, (1024,))
    gates, y_specs, w_specs = _merge_specs(S, D, tm, tn, ks)

    def body(ga_ref, gb_ref, gc_ref, ya_ref, yb_ref, yc_ref, wa_ref, wb_ref, wc_ref, m_ref, za_ref, zb_ref, zc_ref):
        acc = None
        for g_ref, y_ref, w_ref, z_ref in ((ga_ref, ya_ref, wa_ref, za_ref), (gb_ref, yb_ref, wb_ref, zb_ref),
                                           (gc_ref, yc_ref, wc_ref, zc_ref)):
            z = jnp.dot(y_ref[...], w_ref[...], preferred_element_type=F32)
            z_ref[...] = z.astype(BF)
            t = jax.nn.sigmoid(g_ref[...]) * z
            acc = t if acc is None else acc + t
        m_ref[...] = acc.astype(BF)

    tile = pl.BlockSpec((tm, tn), lambda m, n: (m, n))
    return pl.pallas_call(
        body, name="merge_fwd", grid=(S // tm, D // tn), in_specs=gates + y_specs + w_specs,
        out_specs=[tile, tile, tile, tile], out_shape=[jax.ShapeDtypeStruct((S, D), BF)] * 4,
        compiler_params=_params(("parallel", "parallel")),
    )(proj, proj, proj, *ys, *ws)


def _merge_bwd(proj, zs, dm, ws, after=()):
    S = proj.shape[0]
    tn = ws[0].shape[2]
    D = N_DEV * tn
    ks = [w.shape[1] for w in ws]
    tm = _pick(S, (1024,))
    gates, _, w_specs = _merge_specs(S, D, tm, tn, ks)
    nt = (((1,), (1,)), ((), ()))

    def body(ga_ref, gb_ref, gc_ref, za_ref, zb_ref, zc_ref, dm_ref, wa_ref, wb_ref, wc_ref,
             dza_ref, dzb_ref, dzc_ref, dga_ref, dgb_ref, dgc_ref, dya_ref, dyb_ref, dyc_ref):
        n = pl.program_id(1)
        dmv = dm_ref[...]
        for g_ref, z_ref, w_ref, dz_ref, dg_ref, dy_ref in (
                (ga_ref, za_ref, wa_ref, dza_ref, dga_ref, dya_ref), (gb_ref, zb_ref, wb_ref, dzb_ref, dgb_ref, dyb_ref),
                (gc_ref, zc_ref, wc_ref, dzc_ref, dgc_ref, dyc_ref)):
            sg = jax.nn.sigmoid(g_ref[...])
            dz = (sg * dmv).astype(BF)
            dz_ref[...] = dz
            dg_ref[...] = (dmv * z_ref[...].astype(F32) * sg * (1.0 - sg)).astype(BF)
            part = lax.dot_general(dz, w_ref[...], nt, preferred_element_type=F32)

            @pl.when(n == 0)
            def _():
                dy_ref[...] = part

            @pl.when(n > 0)
            def _():
                dy_ref[...] += part

    tile = pl.BlockSpec((tm, tn), lambda m, n: (m, n))
    dys = [pl.BlockSpec((tm, k), lambda m, n: (m, 0)) for k in ks]
    return pl.pallas_call(
        _hide(body, 10, len(after)), name="merge_bwd", grid=(S // tm, D // tn),
        in_specs=gates + [tile, tile, tile, tile] + w_specs + _hidden_specs(after),
        out_specs=[tile] * 6 + dys,
        out_shape=[jax.ShapeDtypeStruct((S, D), BF)] * 6 + [jax.ShapeDtypeStruct((S, k), F32) for k in ks],
        compiler_params=_params(("parallel", "arbitrary")),
    )(proj, proj, proj, *zs, dm, *ws, *after)


PAD = 8


def _stage_shift_down(us_ref, u_ref):
    S = u_ref.shape[1]
    us_ref[:, 0:PAD, :] = jnp.zeros((2, PAD, us_ref.shape[2]), F32)
    us_ref[:, PAD:S + PAD, :] = u_ref[...].astype(F32)


ROWS = 32


def _conv3(us_ref, part, r0, w, b):
    return (us_ref[part, pl.ds(r0 + PAD, ROWS), :] * w[2:3] + us_ref[part, pl.ds(r0 + PAD - 1, ROWS), :] * w[1:2]
            + us_ref[part, pl.ds(r0 + PAD - 2, ROWS), :] * w[0:1] + b)


def _ffn_specs(S, F, tc, c):
    per = c // tc

    def w_spec(half):
        return pl.BlockSpec((None, 3, tc), lambda j: (half * (N_DEV // 2) + j // per, 0, j % per))

    return [pl.BlockSpec((2, S, tc), lambda j: (0, 0, j)), w_spec(0), w_spec(1), pl.BlockSpec((2, 1, tc), lambda j: (0, 0, j))]


def _ffn_tile(F, c):
    tc = 128
    if c % tc or F % tc:
        raise ValueError(f"ffn tile {tc} does not divide {c}, {F}")
    return tc


def _ffn_act_fwd(up3, cws, cb3):
    _, S, F = up3.shape
    c = cws.shape[2]
    tc = _ffn_tile(F, c)

    def body(u_ref, wa_ref, wb_ref, b_ref, o_ref, us_ref):
        _stage_shift_down(us_ref, u_ref)
        wa, wb, ba, bb = wa_ref[...], wb_ref[...], b_ref[0], b_ref[1]

        def step(i, carry):
            r0 = pl.multiple_of(i * ROWS, ROWS)
            ca = _conv3(us_ref, 0, r0, wa, ba)
            cb = _conv3(us_ref, 1, r0, wb, bb)
            o_ref[pl.ds(r0, ROWS), :] = (ca * jax.nn.sigmoid(ca) * cb).astype(BF)
            return carry

        lax.fori_loop(0, S // ROWS, step, 0, unroll=4)

    return pl.pallas_call(
        body, name="ffn_act_fwd", grid=(F // tc,), in_specs=_ffn_specs(S, F, tc, c),
        out_specs=pl.BlockSpec((S, tc), lambda j: (0, j)), out_shape=jax.ShapeDtypeStruct((S, F), BF),
        scratch_shapes=[pltpu.VMEM((2, S + PAD, tc), F32)],
        compiler_params=_params(("parallel",)),
    )(up3, cws, cws, cb3)


def _ffn_act_bwd(up3, cws, cb3, dact, after=()):
    _, S, F = up3.shape
    c = cws.shape[2]
    tc = _ffn_tile(F, c)

    def body(u_ref, wa_ref, wb_ref, b_ref, da_ref, du_ref, dw_ref, db_ref, us_ref, dcs_ref):
        _stage_shift_down(us_ref, u_ref)
        ws = (wa_ref[...], wb_ref[...])
        ba, bb = b_ref[0], b_ref[1]
        dcs_ref[:, S:S + PAD, :] = jnp.zeros((2, PAD, tc), F32)

        def conv_grads(i, carry):
            r0 = pl.multiple_of(i * ROWS, ROWS)
            ca = _conv3(us_ref, 0, r0, ws[0], ba)
            cb = _conv3(us_ref, 1, r0, ws[1], bb)
            sg = jax.nn.sigmoid(ca)
            dav = da_ref[pl.ds(r0, ROWS), :].astype(F32)
            dcs_ref[0, pl.ds(r0, ROWS), :] = dav * cb * sg * (1.0 + ca * (1.0 - sg))
            dcs_ref[1, pl.ds(r0, ROWS), :] = dav * ca * sg
            return carry

        lax.fori_loop(0, S // ROWS, conv_grads, 0, unroll=4)

        def fold(v):
            return jnp.sum(v.reshape(ROWS // 8, 8, tc), axis=0)

        def input_grads(i, acc):
            r0 = pl.multiple_of(i * ROWS, ROWS)
            new = []
            for part in range(2):
                w = ws[part]
                dc = dcs_ref[part, pl.ds(r0, ROWS), :]
                dc1 = dcs_ref[part, pl.ds(r0 + 1, ROWS), :]
                dc2 = dcs_ref[part, pl.ds(r0 + 2, ROWS), :]
                u = us_ref[part, pl.ds(r0 + PAD, ROWS), :]
                du_ref[part, pl.ds(r0, ROWS), :] = (dc * w[2:3] + dc1 * w[1:2] + dc2 * w[0:1]).astype(BF)
                sums = (fold(dc2 * u), fold(dc1 * u), fold(dc * u), fold(dc))
                new += [a + s for a, s in zip(acc[4 * part:4 * part + 4], sums)]
            return tuple(new)

        acc = lax.fori_loop(0, S // ROWS, input_grads, tuple(jnp.zeros((8, tc), F32) for _ in range(8)), unroll=4)
        for part in range(2):
            for j in range(3):
                dw_ref[part, j:j + 1, :] = jnp.sum(acc[4 * part + j], axis=0, keepdims=True)
            db_ref[part] = jnp.sum(acc[4 * part + 3], axis=0, keepdims=True)

    return pl.pallas_call(
        _hide(body, 5, len(after)), name="ffn_act_bwd", grid=(F // tc,),
        in_specs=_ffn_specs(S, F, tc, c) + [pl.BlockSpec((S, tc), lambda j: (0, j))] + _hidden_specs(after),
        out_specs=[pl.BlockSpec((2, S, tc), lambda j: (0, 0, j)), pl.BlockSpec((2, 3, tc), lambda j: (0, 0, j)),
                   pl.BlockSpec((2, 1, tc), lambda j: (0, 0, j))],
        out_shape=[jax.ShapeDtypeStruct((2, S, F), BF), jax.ShapeDtypeStruct((2, 3, F), F32), jax.ShapeDtypeStruct((2, 1, F), F32)],
        scratch_shapes=[pltpu.VMEM((2, S + PAD, tc), F32), pltpu.VMEM((2, S + PAD, tc), F32)],
        compiler_params=_params(("parallel",)),
    )(up3, cws, cws, cb3, dact, *after)


def _residual_rms(a, w, x, g, name, tm=512):
    S, K = a.shape
    D = w.shape[1]
    tm = _pick(S, (tm,))

    def body(a_ref, w_ref, x_ref, g_ref, x1_ref, h_ref, r_ref):
        x1 = jnp.dot(a_ref[...], w_ref[...], preferred_element_type=F32) + x_ref[...]
        r = lax.rsqrt(jnp.mean(x1 * x1, axis=-1, keepdims=True) + EPS)
        x1_ref[...] = x1
        h_ref[...] = (x1 * r * g_ref[...]).astype(BF)
        r_ref[...] = r

    row = pl.BlockSpec((tm, D), lambda i: (i, 0))
    return pl.pallas_call(
        body, name=name, grid=(S // tm,),
        in_specs=[pl.BlockSpec((tm, K), lambda i: (i, 0)), pl.BlockSpec((K, D), lambda i: (0, 0)), row, pl.BlockSpec((1, D), lambda i: (0, 0))],
        out_specs=[row, row, pl.BlockSpec((tm, 1), lambda i: (i, 0))],
        out_shape=[jax.ShapeDtypeStruct((S, D), F32), jax.ShapeDtypeStruct((S, D), BF), jax.ShapeDtypeStruct((S, 1), F32)],
        compiler_params=_params(("parallel",)),
    )(a, w, x, g)


def _out_loss(act, w_down, x1, target, tm=512, tn=1024, tk=1408):
    S, F = act.shape
    D = w_down.shape[1]
    tm, tn, tk = _pick(S, (tm,)), _pick(D, (tn,)), _pick(F, (tk,))
    nm, nn, nk = S // tm, D // tn, F // tk

    def body(a_ref, b_ref, x_ref, t_ref, dy_ref, dyb_ref, l_ref, acc):
        m, n, k = pl.program_id(0), pl.program_id(1), pl.program_id(2)

        @pl.when((m == 0) & (n == 0) & (k == 0))
        def _():
            l_ref[...] = jnp.zeros_like(l_ref)

        @pl.when(k == 0)
        def _():
            acc[...] = jnp.zeros_like(acc)

        acc[...] += jnp.dot(a_ref[...], b_ref[...], preferred_element_type=F32)

        @pl.when(k == nk - 1)
        def _():
            e = acc[...] + x_ref[...] - t_ref[...]
            dy = e * (1.0 / D)
            dy_ref[...] = dy
            dyb_ref[...] = dy.astype(BF)
            l_ref[...] += jnp.sum(jnp.sum(e * e, axis=-1, keepdims=True), axis=0, keepdims=True) * (0.5 / D)

    tile = pl.BlockSpec((tm, tn), lambda m, n, k: (m, n))
    return pl.pallas_call(
        body, name="mm_y_loss", grid=(nm, nn, nk),
        in_specs=[pl.BlockSpec((tm, tk), lambda m, n, k: (m, k)), pl.BlockSpec((tk, tn), lambda m, n, k: (k, n)), tile, tile],
        out_specs=[tile, tile, pl.BlockSpec((8, 128), lambda m, n, k: (0, 0))],
        out_shape=[jax.ShapeDtypeStruct((S, D), F32), jax.ShapeDtypeStruct((S, D), BF), jax.ShapeDtypeStruct((8, 128), F32)],
        scratch_shapes=[pltpu.VMEM((tm, tn), F32)],
        compiler_params=_params(("arbitrary", "arbitrary", "arbitrary")),
    )(act, w_down, x1, target)


ANY = pl.BlockSpec(memory_space=pl.ANY)


def _allgather(shards, name):
    n = len(shards)

    def body(*refs):
        ins, outs = refs[:n], refs[n:2 * n]
        send_sems, recv_sems, local_sems = refs[2 * n:]
        x, y, c = lax.axis_index("x"), lax.axis_index("y"), lax.axis_index("c")
        me, sibling = (x, y, c), (x, y, 1 - c)
        chips = [(1 - x, y), (x, 1 - y), (1 - x, 1 - y)]

        def blk(w, px, py, pc):
            return outs[w].at[4 * px + 2 * py + pc]

        def copy(w, k, block, to, src=None):
            return pltpu.make_async_remote_copy(
                src_ref=blk(w, *block) if src is None else src, dst_ref=blk(w, *block),
                send_sem=send_sems.at[w, k], recv_sem=recv_sems.at[w, k], device_id=to, device_id_type=MESH)

        started = []
        mine = []
        for w in range(n):
            mine.append(pltpu.make_async_copy(ins[w], blk(w, *me), local_sems.at[w]))
            mine[-1].start()
            first = [copy(w, 0, me, sibling, src=ins[w])]
            first += [copy(w, 1 + j, me, (*chip, c), src=ins[w]) for j, chip in enumerate(chips)]
            for cp in first:
                cp.start()
            started += first
        for w in range(n):
            for j, chip in enumerate(chips):
                copy(w, 1 + j, (*chip, c), me).wait_recv()
                fwd = copy(w, 4 + j, (*chip, c), sibling)
                fwd.start()
                started.append(fwd)
        for w in range(n):
            copy(w, 0, sibling, me).wait_recv()
            for j, chip in enumerate(chips):
                copy(w, 4 + j, (*chip, 1 - c), me).wait_recv()
        for cp in started:
            cp.wait_send()
        for cp in mine:
            cp.wait()

    whole = pl.BlockSpec(memory_space=pltpu.VMEM)
    outs = pl.pallas_call(
        body, name=name, in_specs=[whole] * n, out_specs=[whole] * n,
        out_shape=[jax.ShapeDtypeStruct((N_DEV,) + s.shape, s.dtype) for s in shards],
        scratch_shapes=[pltpu.SemaphoreType.DMA((n, 7)), pltpu.SemaphoreType.DMA((n, 7)), pltpu.SemaphoreType.DMA((n,))],
    )(*shards)
    return list(outs)


def _allgather_seq(shards, name, collective_id, after=()):
    n = len(shards)
    n_after = len(after)

    halves = [s.shape[0] % 32 == 0 for s in shards]
    n_sem = 8
    to_diagonal = not all(halves)

    def body(*refs):
        ins, outs = refs[:n], refs[n + n_after:2 * n + n_after]
        send_sems, recv_sems, local_sems = refs[2 * n + n_after:]
        x, y, c = lax.axis_index("x"), lax.axis_index("y"), lax.axis_index("c")
        me, sibling = (x, y, c), (x, y, 1 - c)
        x_nb, y_nb, diag = (1 - x, y, c), (x, 1 - y, c), (1 - x, 1 - y, c)
        peers = [sibling, x_nb, y_nb] + ([diag] if to_diagonal else [])
        barrier = pltpu.get_barrier_semaphore()
        for peer in peers:
            pl.semaphore_signal(barrier, inc=1, device_id=peer, device_id_type=MESH)
        pl.semaphore_wait(barrier, len(peers))

        def blk(w, dev, rows=None):
            ref = outs[w].at[4 * dev[0] + 2 * dev[1] + dev[2]]
            return ref if rows is None else ref.at[rows]

        def copy(w, k, block, to, src=None, rows=None):
            return pltpu.make_async_remote_copy(
                src_ref=blk(w, block, rows) if src is None else src, dst_ref=blk(w, block, rows),
                send_sem=send_sems.at[n_sem * w + k], recv_sem=recv_sems.at[n_sem * w + k], device_id=to, device_id_type=MESH)

        def top(w):
            return pl.ds(0, shards[w].shape[0] // 2)

        def bottom(w):
            return pl.ds(shards[w].shape[0] // 2, shards[w].shape[0] // 2)

        started = []
        mine = []
        for w in range(n):
            mine.append(pltpu.make_async_copy(ins[w], blk(w, me), local_sems.at[w]))
            mine[-1].start()
            first = [copy(w, 0, me, sibling, src=ins[w]), copy(w, 1, me, x_nb, src=ins[w]), copy(w, 2, me, y_nb, src=ins[w])]
            if not halves[w]:
                first.append(copy(w, 3, me, diag, src=ins[w]))
            for cp in first:
                cp.start()
            started += first
        for w in range(n):
            copy(w, 1, x_nb, me).wait_recv()
            onward = [copy(w, 5, x_nb, sibling)] + ([copy(w, 3, x_nb, y_nb, rows=top(w))] if halves[w] else [])
            copy(w, 2, y_nb, me).wait_recv()
            onward += [copy(w, 6, y_nb, sibling)] + ([copy(w, 4, y_nb, x_nb, rows=bottom(w))] if halves[w] else [])
            for cp in onward:
                cp.start()
            started += onward
        for w in range(n):
            if halves[w]:
                copy(w, 3, diag, me, rows=top(w)).wait_recv()
                copy(w, 4, diag, me, rows=bottom(w)).wait_recv()
            else:
                copy(w, 3, diag, me).wait_recv()
            fwd = copy(w, 7, diag, sibling)
            fwd.start()
            started.append(fwd)
        for w in range(n):
            for k, dev in ((0, sibling), (5, (1 - x, y, 1 - c)), (6, (x, 1 - y, 1 - c)), (7, (1 - x, 1 - y, 1 - c))):
                copy(w, k, dev, me).wait_recv()
        for cp in started:
            cp.wait_send()
        for cp in mine:
            cp.wait()

    outs = pl.kernel(
        body, name=name, out_type=[jax.ShapeDtypeStruct((N_DEV,) + s.shape, s.dtype) for s in shards],
        mesh=plsc.ScalarSubcoreMesh(axis_name="seq", num_cores=1),
        scratch_types=[pltpu.SemaphoreType.DMA((n_sem * n,)), pltpu.SemaphoreType.DMA((n_sem * n,)), pltpu.SemaphoreType.DMA((n,))],
        compiler_params=pltpu.CompilerParams(collective_id=collective_id),
    )(*shards, *after)
    return list(outs)


def _chip_exchange(sums, name, collective_id):
    n = len(sums)

    def body(*refs):
        ins, outs = refs[:n], refs[n:2 * n]
        send_sems, recv_sems = refs[2 * n:]
        x, y, c = lax.axis_index("x"), lax.axis_index("y"), lax.axis_index("c")
        chips = [(1 - x, y), (x, 1 - y), (1 - x, 1 - y)]
        barrier = pltpu.get_barrier_semaphore()
        for px, py in chips:
            pl.semaphore_signal(barrier, inc=1, device_id=(px, py, c), device_id_type=MESH)
        pl.semaphore_wait(barrier, 3)
        copies = []
        for w in range(n):
            for k, (px, py) in enumerate(chips):
                copies.append(pltpu.make_async_remote_copy(
                    src_ref=ins[w].at[2 * px + py], dst_ref=outs[w].at[k], send_sem=send_sems.at[3 * w + k],
                    recv_sem=recv_sems.at[3 * w + k], device_id=(px, py, c), device_id_type=MESH))
        for cp in copies:
            cp.start()
        for cp in copies:
            cp.wait()

    outs = pl.kernel(
        body, name=name, out_type=[jax.ShapeDtypeStruct((3,) + s.shape[1:], s.dtype) for s in sums],
        mesh=plsc.ScalarSubcoreMesh(axis_name="seq", num_cores=1),
        scratch_types=[pltpu.SemaphoreType.DMA((3 * n,)), pltpu.SemaphoreType.DMA((3 * n,))],
        compiler_params=pltpu.CompilerParams(collective_id=collective_id),
    )(*sums)
    return list(outs)


def _row_tile(r, c, elems=256 * 1024):
    want = max(8, elems // c)
    for t in range(min(want, r) // 8 * 8, 0, -8):
        if r % t == 0:
            return t
    return r


def _pair_add(g4, recv, core, name, after=()):
    _, _, r, c = g4.shape
    tr = _row_tile(r, c, 1024 * 1024)

    def body(core_ref, a_ref, b_ref, o_ref):
        o_ref[...] = (a_ref[...].astype(F32) + b_ref[...].astype(F32)).astype(BF)

    return pl.pallas_call(
        _hide(body, 3, len(after)), name=name,
        grid_spec=pltpu.PrefetchScalarGridSpec(
            num_scalar_prefetch=1, grid=(4, r // tr),
            in_specs=[pl.BlockSpec((None, None, tr, c), lambda p, i, s: (p, s[0], i, 0)),
                      pl.BlockSpec((None, tr, c), lambda p, i, s: (p, i, 0))] + _hidden_specs(after),
            out_specs=pl.BlockSpec((None, tr, c), lambda p, i, s: (p, i, 0))),
        out_shape=jax.ShapeDtypeStruct((4, r, c), BF), compiler_params=_params(("parallel", "parallel")),
    )(core, g4, recv, *after)


def _adam_math(w, g, m, v):
    m = ADAM_B1 * m + (1.0 - ADAM_B1) * g
    v = ADAM_B2 * v + (1.0 - ADAM_B2) * (g * g)
    m_hat = m / (1.0 - ADAM_B1 ** ADAM_STEP)
    v_hat = v / (1.0 - ADAM_B2 ** ADAM_STEP)
    delta = -ADAM_LR * (m_hat / (jnp.sqrt(v_hat) + ADAM_EPS) + ADAM_WD * w)
    return delta, m, v


def _adamw_big(sums, recv, chip, w, m, v, name, after=()):
    r, c = w.shape
    tr = _row_tile(r, c, 512 * 1024)

    def body(chip_ref, s_ref, r_ref, w_ref, m_ref, v_ref, g_out, d_out, m_out, v_out):
        g = s_ref[...].astype(F32) + r_ref[0].astype(F32)
        g = g + r_ref[1].astype(F32)
        g = g + r_ref[2].astype(F32)
        delta, mn, vn = _adam_math(w_ref[...], g, m_ref[...], v_ref[...])
        g_out[...] = g
        d_out[...] = delta
        m_out[...] = mn
        v_out[...] = vn

    row = pl.BlockSpec((tr, c), lambda i, s: (i, 0))
    return pl.pallas_call(
        _hide(body, 6, len(after)), name=name,
        grid_spec=pltpu.PrefetchScalarGridSpec(
            num_scalar_prefetch=1, grid=(r // tr,),
            in_specs=[pl.BlockSpec((None, tr, c), lambda i, s: (s[0], i, 0)), pl.BlockSpec((3, tr, c), lambda i, s: (0, i, 0)),
                      row, row, row] + _hidden_specs(after),
            out_specs=[row, row, row, row]),
        out_shape=[jax.ShapeDtypeStruct((r, c), F32)] * 4, compiler_params=_params(("parallel",)),
    )(chip, sums, recv, w, m, v, *after)


def _adamw_small(parts, ws, ms, vs, extra_parts, name):
    n, ne = len(ws), len(extra_parts)

    def total(p_ref):
        g = p_ref[0]
        for d in range(1, N_DEV):
            g = g + p_ref[d]
        return g

    def body(*refs):
        p_refs, w_refs, m_refs, v_refs = refs[:n], refs[n:2 * n], refs[2 * n:3 * n], refs[3 * n:4 * n]
        e_refs = refs[4 * n:4 * n + ne]
        outs = refs[4 * n + ne:]
        for i in range(n):
            g = total(p_refs[i])
            delta, mn, vn = _adam_math(w_refs[i][...], g, m_refs[i][...], v_refs[i][...])
            outs[4 * i][...] = g
            outs[4 * i + 1][...] = delta
            outs[4 * i + 2][...] = mn
            outs[4 * i + 3][...] = vn
        for i in range(ne):
            outs[4 * n + i][...] = total(e_refs[i])

    out_shape = []
    for w in ws:
        out_shape += [jax.ShapeDtypeStruct(w.shape, F32)] * 4
    out_shape += [jax.ShapeDtypeStruct(e.shape[1:], F32) for e in extra_parts]
    res = pl.pallas_call(body, name=name, out_shape=out_shape,
                         compiler_params=pltpu.CompilerParams(vmem_limit_bytes=VMEM_LIMIT))(*parts, *ws, *ms, *vs, *extra_parts)
    return [res[4 * i:4 * i + 4] for i in range(n)], list(res[4 * n:])


def _adamw_plain(g, w, m, v, name):
    def body(g_ref, w_ref, m_ref, v_ref, d_out, m_out, v_out):
        delta, mn, vn = _adam_math(w_ref[...], g_ref[...], m_ref[...], v_ref[...])
        d_out[...] = delta
        m_out[...] = mn
        v_out[...] = vn

    return pl.pallas_call(body, name=name, out_shape=[jax.ShapeDtypeStruct(w.shape, F32)] * 3)(g, w, m, v)


def kernel(x, mem, positions, g_mix, w_in, g_a_v, w_spatial, b_spatial, g_b_q, g_b_k, sinks, g_mem, w_mem_kv, g_c_q, g_c_k, w_branch_a, w_branch_b, w_branch_c, w_out, g_ffn, w_up, conv_w, conv_b, w_down, loss_target, m_g_mix, m_w_in, m_g_a_v, m_w_spatial, m_b_spatial, m_g_b_q, m_g_b_k, m_sinks, m_g_mem, m_w_mem_kv, m_g_c_q, m_g_c_k, m_w_branch_a, m_w_branch_b, m_w_branch_c, m_w_out, m_g_ffn, m_w_up, m_conv_w, m_conv_b, m_w_down, v_g_mix, v_w_in, v_g_a_v, v_w_spatial, v_b_spatial, v_g_b_q, v_g_b_k, v_sinks, v_g_mem, v_w_mem_kv, v_g_c_q, v_g_c_k, v_w_branch_a, v_w_branch_b, v_w_branch_c, v_w_out, v_g_ffn, v_w_up, v_conv_w, v_conv_b, v_w_down):
    S, D = x.shape[1], x.shape[2]
    M = mem.shape[1]
    F = w_down.shape[1] * N_DEV
    in_cols = w_in.shape[2] * N_DEV
    ax, ay, ac = lax.axis_index("x"), lax.axis_index("y"), lax.axis_index("c")
    core = jnp.reshape(ac, (1,)).astype(jnp.int32)
    chip = jnp.reshape(2 * ax + ay, (1,)).astype(jnp.int32)
    me = 4 * ax + 2 * ay + ac

    x2, mem2, tgt2 = x[0], mem[0], loss_target[0]

    big = dict(w_in=w_in[0].T, w_mem_kv=w_mem_kv[0], w_branch_a=w_branch_a[0], w_branch_b=w_branch_b[0],
               w_branch_c=w_branch_c[0], w_out=w_out[0], w_up=w_up[0], w_down=w_down[0])
    names = list(big)
    cast = {k: big[k].astype(BF) for k in names}
    W = {}
    cb3 = conv_b.reshape(2, 1, F)
    d_half = D // 2
    w_lo, = _allgather_seq([cast["w_in"][:, :d_half]], "ag_seq0", 0)
    w_in_lo = w_lo.reshape(in_cols, d_half)
    w_hi, = _allgather_seq([cast["w_in"][:, d_half:]], "ag_seq0b", 9, after=(_token((w_in_lo,), "tok_w_in_lo"),))
    w_in_hi = w_hi.reshape(in_cols, d_half)
    grp1 = ["w_mem_kv", "w_branch_a", "w_branch_b", "w_branch_c", "w_out"]
    res1 = _allgather_seq([cast[k] for k in grp1] + [conv_w[0]], "ag_seq1", 1, after=(_token((w_in_hi,), "tok_w_in"),))
    W.update(zip(grp1, res1))
    cw3 = res1[-1]
    w_kv_f = W["w_mem_kv"].reshape(D, 2 * C_WIDTH)
    w_out_f = W["w_out"].reshape(D, D)

    half = ROPE_DIM // 2
    inv = ROPE_THETA ** (-jnp.arange(half, dtype=F32) / half)
    ang = positions[0].astype(F32)[:, None] * inv
    cos, sin = jnp.cos(ang), jnp.sin(ang)
    one, zero = jnp.ones((S, B_HEAD_DIM - ROPE_DIM), F32), jnp.zeros((S, B_HEAD_DIM - ROPE_DIM), F32)
    z8 = jnp.zeros((S, half), F32)
    ct = jnp.tile(jnp.concatenate([cos, cos, one], axis=1), (1, 2))
    sa = jnp.tile(jnp.concatenate([-sin, z8, zero], axis=1), (1, 2))
    sb = jnp.tile(jnp.concatenate([z8, sin, zero], axis=1), (1, 2))
    gq2, gk2 = jnp.tile(g_b_q, (1, 2)), jnp.tile(g_b_k, (1, 2))
    b_t = b_spatial[0].T

    h, rstd1 = _rms_fwd(x2, g_mix, "rms1_fwd")
    proj_lo = _mm(h, w_in_lo, "nt", F32, "mm_proj_a", tn=1280, a_cols=(0, 2))
    proj = _mm(h, w_in_hi, "nt", F32, "mm_proj_b", tn=1280, a_cols=(1, 2), resid=proj_lo)
    y_a = _a_fwd(proj, g_a_v, w_spatial[0], b_t)
    W["w_up"], = _allgather_seq([cast["w_up"]], "ag_seq2", 2, after=(_token((W["w_out"], proj), "tok_group1"),))
    qn, kn = _b_pre(proj, gq2, gk2, ct, sa, sb)
    y_b = _b_attn_fwd(qn, kn, proj, sinks)
    mem_h, rstd_m = _rms_fwd(mem2, g_mem, "rmsmem_fwd")
    kv = _mm(mem_h, w_kv_f, "nn", F32, "mm_kv", after=(y_b,))
    y_c = _c_fwd(proj, kv, g_c_q, g_c_k)
    w_branches = [W["w_branch_a"], W["w_branch_b"], W["w_branch_c"]]
    merged, z_a, z_b, z_c = _merge_fwd(proj, [y_a, y_b, y_c], w_branches)
    x1, h2, rstd2 = _residual_rms(merged, w_out_f, x2, g_ffn, "mm_x1_rms2")
    W["w_down"], = _allgather_seq([cast["w_down"]], "ag_seq3", 3, after=(W["w_up"], h2))
    w_down_f = W["w_down"].reshape(F, D)
    up3 = _mm(h2, W["w_up"], "nn", BF, "mm_up", b_stack=True, out_parts=2)
    act = _ffn_act_fwd(up3, cw3, cb3)
    dy, dy_b, loss_acc = _out_loss(act, w_down_f, x1, tgt2)

    reduced = {}

    def as4(g):
        return g.reshape(4, 2, g.shape[1], g.shape[2])

    def finish_group(gi, keys, g4, from_sibling):
        sums = [_pair_add(a, b, core, "rs_add_" + k) for k, a, b in zip(keys, g4, from_sibling)]
        from_chips = _chip_exchange(sums, f"rs_chip{gi}", 4 + gi)
        reduced.update(zip(keys, zip(sums, from_chips)))
        return tuple(sums)

    d_act = _mm(dy_b, w_down_f, "nt", BF, "mm_dact", tn=1408)
    g_down = _mm(act, dy_b, "tn", BF, "mm_gdown", tm=1408)
    d_up3, d_cw3, d_cb3 = _ffn_act_bwd(up3, cw3, cb3, d_act, after=(g_down,))
    grp0 = [as4(g_down.reshape(N_DEV, F // N_DEV, D))]
    g_up, sib0 = _mm(h2, d_up3, "tn", BF, "mm_gup", b_parts=2, out_stack=True, exchange=grp0)
    sums0 = finish_group(0, ["w_down"], grp0, sib0)
    grp1 = [as4(g_up)]
    d_h2, sib1 = _mm(d_up3, W["w_up"], "nt", F32, "mm_dh2", a_parts=2, b_stack=True, tm=2048, after=sums0, exchange=grp1)
    sums1 = finish_group(1, ["w_up"], grp1, sib1)
    dx1, dx1_b, d_g_ffn = _rms_bwd(x1, rstd2, g_ffn, d_h2, dy, "rms2_bwd", after=sums1)
    g_out = _mm(merged, dx1_b, "tn", BF, "mm_gout")
    grp2 = [as4(g_out.reshape(N_DEV, D // N_DEV, D))]
    d_merged, sib2 = _mm(dx1_b, w_out_f, "nt", F32, "mm_dmerged", exchange=grp2)
    sums2 = finish_group(2, ["w_out"], grp2, sib2)
    dz_a, dz_b, dz_c, dga, dgb, dgc, dy_a, dy_b_, dy_c = _merge_bwd(proj, [z_a, z_b, z_c], d_merged, w_branches, after=sums2)
    g_ba = _mm(y_a, dz_a, "tn", BF, "mm_gba", out_stack=True)
    g_bb = _mm(y_b, dz_b, "tn", BF, "mm_gbb", out_stack=True)
    g_bc = _mm(y_c, dz_c, "tn", BF, "mm_gbc", out_stack=True)
    d_uv, d_g_a_v, d_w_s, d_b_t = _a_bwd(proj, g_a_v, w_spatial[0], b_t, dy_a, after=(g_ba, g_bb, g_bc))
    dqn, dkn, dv_b, dsink_rows = _b_attn_bwd(qn, kn, proj, sinks, dy_b_)
    d_qkv, d_gq2, d_gk2 = _b_pre_bwd(proj, gq2, gk2, ct, sa, sb, dqn, dkn, dv_b)
    dq_c, dk_c, dv_c, d_gcq, d_gck = _c_bwd(proj, kv, g_c_q, g_c_k, dy_c)
    dkv_b = jnp.concatenate([dk_c, dv_c], axis=1).astype(BF)
    d_memh = _mm(dkv_b, w_kv_f, "nt", F32, "mm_dmemh")
    g_kv = _mm(mem_h, dkv_b, "tn", BF, "mm_gkv")
    _, _, d_g_mem = _rms_bwd(mem2, rstd_m, g_mem, d_memh, None, "rmsmem_bwd")
    dproj = jnp.concatenate([d_uv, d_qkv, dq_c, dga, dgb, dgc], axis=1)
    grp3 = [as4(g_ba), as4(g_bb), as4(g_bc)]
    g_in, sib3 = _mm(dproj, h, "tn", BF, "mm_gin", tm=1280, exchange=grp3)
    sums3 = finish_group(3, ["w_branch_a", "w_branch_b", "w_branch_c"], grp3, sib3)
    grp4 = [as4(g_in.reshape(N_DEV, in_cols // N_DEV, D)), as4(g_kv.reshape(N_DEV, D // N_DEV, 2 * C_WIDTH))]
    d_h_lo = _mm(dproj, w_in_lo, "nn", F32, "mm_dh_a", tm=2048, tk=1280, after=sums3, out_cols=(0, 2))
    d_h, sib4 = _mm(dproj, w_in_hi, "nn", F32, "mm_dh_b", tm=2048, tk=1280, exchange=grp4, out_cols=(1, 2), into=d_h_lo)
    sums4 = finish_group(4, ["w_in", "w_mem_kv"], grp4, sib4)
    grad_x, _, d_g_mix = _rms_bwd(x2, rstd1, g_mix, d_h, dx1, "rms1_bwd", after=sums4)

    small_names =["g_mix", "g_a_v", "w_spatial", "b_spatial", "g_b_q", "g_b_k", "sinks", "g_mem", "g_c_q", "g_c_k", "g_ffn", "conv_b"]
    small_w = dict(g_mix=g_mix, g_a_v=g_a_v, w_spatial=w_spatial, b_spatial=b_spatial, g_b_q=g_b_q, g_b_k=g_b_k, sinks=sinks,
                   g_mem=g_mem, g_c_q=g_c_q, g_c_k=g_c_k, g_ffn=g_ffn, conv_b=conv_b)
    small_m = dict(g_mix=m_g_mix, g_a_v=m_g_a_v, w_spatial=m_w_spatial, b_spatial=m_b_spatial, g_b_q=m_g_b_q, g_b_k=m_g_b_k,
                   sinks=m_sinks, g_mem=m_g_mem, g_c_q=m_g_c_q, g_c_k=m_g_c_k, g_ffn=m_g_ffn, conv_b=m_conv_b)
    small_v = dict(g_mix=v_g_mix, g_a_v=v_g_a_v, w_spatial=v_w_spatial, b_spatial=v_b_spatial, g_b_q=v_g_b_q, g_b_k=v_g_b_k,
                   sinks=v_sinks, g_mem=v_g_mem, g_c_q=v_g_c_q, g_c_k=v_g_c_k, g_ffn=v_g_ffn, conv_b=v_conv_b)
    small_g = dict(
        g_mix=d_g_mix, g_a_v=d_g_a_v, w_spatial=d_w_s, b_spatial=d_b_t.T,
        g_b_q=d_gq2.reshape(2, B_HEAD_DIM).sum(0), g_b_k=d_gk2.reshape(2, B_HEAD_DIM).sum(0),
        sinks=dsink_rows.sum(0)[:B_HEADS], g_mem=d_g_mem, g_c_q=d_gcq.sum(0), g_c_k=d_gck.sum(0), g_ffn=d_g_ffn,
        conv_b=d_cb3)
    partial = [small_g[k].reshape(small_w[k].shape) for k in small_names] + [d_cw3, loss_acc[0:1]]
    parts = _allgather(partial, "ag_small")
    n_small = len(small_names)
    small_res, (g_cw3, loss_row) = _adamw_small(parts[:n_small], [small_w[k] for k in small_names], [small_m[k] for k in small_names],
                                                [small_v[k] for k in small_names], parts[n_small:], "adamw_small")
    loss = loss_row[0, 0]
    small_out = dict(zip(small_names, small_res))
    c_cw = 2 * F // N_DEV
    g_cw = lax.dynamic_slice(g_cw3, (me // (N_DEV // 2), 0, (me % (N_DEV // 2)) * c_cw), (1, 3, c_cw))[0]
    cw_res = _adamw_plain(g_cw, conv_w[0], m_conv_w[0], v_conv_w[0], "adamw_conv_w")
    big_out = {"conv_w": [g_cw[None]] + [a[None] for a in cw_res]}

    moments = dict(w_in=(m_w_in, v_w_in), w_mem_kv=(m_w_mem_kv, v_w_mem_kv), w_branch_a=(m_w_branch_a, v_w_branch_a),
                   w_branch_b=(m_w_branch_b, v_w_branch_b), w_branch_c=(m_w_branch_c, v_w_branch_c), w_out=(m_w_out, v_w_out),
                   w_up=(m_w_up, v_w_up), w_down=(m_w_down, v_w_down))
    token = (grad_x, small_res[0][0])
    for k in ["w_down", "w_up", "w_out", "w_branch_a", "w_branch_b", "w_branch_c", "w_mem_kv", "w_in"]:
        s, r = reduced[k]
        mk, vk = moments[k][0][0], moments[k][1][0]
        if k == "w_in":
            res = _adamw_big(s, r, chip, big[k], mk.T, vk.T, "adamw_" + k, after=token)
            big_out[k] = [a.T[None] for a in res]
        else:
            res = _adamw_big(s, r, chip, big[k], mk, vk, "adamw_" + k, after=token)
            big_out[k] = [a[None] for a in res]
        token = (res[0],)

    order = ["g_mix", "w_in", "g_a_v", "w_spatial", "b_spatial", "g_b_q", "g_b_k", "sinks", "g_mem", "w_mem_kv", "g_c_q", "g_c_k",
             "w_branch_a", "w_branch_b", "w_branch_c", "w_out", "g_ffn", "w_up", "conv_w", "conv_b", "w_down"]
    res = {**small_out, **big_out}
    outs = [loss, grad_x[None]]
    for field in range(4):
        outs += [res[k][field] for k in order]
    return tuple(outs)
```

```python
import functools

import jax
import jax.numpy as jnp
from jax import lax
from jax.experimental import pallas as pl
from jax.experimental.pallas import tpu as pltpu
from jax.experimental.pallas import tpu_sc as plsc

F32 = jnp.float32
BF = jnp.bfloat16
EPS = 1e-6
NEG = -1e30

N_DEV = 8
CHUNK = 128
A_GROUPS = 4
A_WIDTH = 512
B_HEADS = 16
B_KV_HEADS = 2
B_HEAD_DIM = 64
B_WIDTH = 1024
B_KV_WIDTH = 128
ROPE_DIM = 16
ROPE_THETA = 500000.0
C_HEADS = 4
C_HEAD_DIM = 128
C_WIDTH = 512
GATE_OFF = 2 * A_WIDTH + B_WIDTH + 2 * B_KV_WIDTH + C_WIDTH

ADAM_LR = 0.001
ADAM_B1 = 0.9
ADAM_B2 = 0.999
ADAM_EPS = 1e-08
ADAM_WD = 0.01
ADAM_STEP = 10

VMEM_LIMIT = 48 * 1024 * 1024
MESH = pl.DeviceIdType.MESH


def _pick(n, prefs):
    for p in prefs:
        if p <= n and n % p == 0:
            return p
    return n


def _params(sem):
    return pltpu.CompilerParams(dimension_semantics=sem, vmem_limit_bytes=VMEM_LIMIT)


def _hide(body, n_seen, n_hidden):
    if not n_hidden:
        return body

    def wrapped(*refs):
        return body(*refs[:n_seen], *refs[n_seen + n_hidden:])

    return wrapped


def _hidden_specs(after):
    return [pl.BlockSpec(memory_space=pl.ANY) for _ in after]


def _token(xs, name):
    def body(*refs):
        refs[-1][...] = jnp.zeros_like(refs[-1])

    return pl.pallas_call(body, name=name, in_specs=_hidden_specs(xs), out_shape=jax.ShapeDtypeStruct((8, 128), F32))(*xs)


def _mm(a, b, mode, out_dtype, name, *, resid=None, b_stack=False, a_parts=0, b_parts=0, out_parts=0,
        out_stack=False, tm=1024, tn=1024, tk=2048, after=(), exchange=()):
    if mode == "nn":
        M = a.shape[-2]
        K = a.shape[-1] * max(a_parts, 1)
        N = b.shape[-1] * (N_DEV if b_stack else 1)
        dims = (((1,), (0,)), ((), ()))
    elif mode == "nt":
        M = a.shape[-2]
        K = a.shape[-1] * max(a_parts, 1)
        N = b.shape[-2]
        dims = (((1,), (1,)), ((), ()))
    else:
        K = a.shape[-2]
        M = a.shape[-1]
        N = b.shape[-1] * max(b_parts, 1)
        dims = (((0,), (0,)), ((), ()))
    if b_stack and mode == "nn":
        tn = b.shape[-1]
    if b_stack and mode == "nt":
        tk = b.shape[-1]
    if out_stack:
        tn = N // N_DEV
    tm, tn, tk = _pick(M, (tm,)), _pick(N, (tn,)), _pick(K, (tk,))
    if M % tm or N % tn or K % tk:
        raise ValueError(f"{name}: tiles {tm},{tn},{tk} do not divide {M},{N},{K}")
    nm, nn, nk = M // tm, N // tn, K // tk

    def parts_idx(t, ntile, parts):
        per = ntile // parts
        return t // per, t % per

    if mode in ("nn", "nt"):
        if a_parts:
            a_spec = pl.BlockSpec((None, tm, tk), lambda m, n, k: (parts_idx(k, nk, a_parts)[0], m, parts_idx(k, nk, a_parts)[1]))
        else:
            a_spec = pl.BlockSpec((tm, tk), lambda m, n, k: (m, k))
    else:
        a_spec = pl.BlockSpec((tk, tm), lambda m, n, k: (k, m))
    if mode == "nn":
        if b_stack:
            b_spec = pl.BlockSpec((None, tk, tn), lambda m, n, k: (n, k, 0))
        else:
            b_spec = pl.BlockSpec((tk, tn), lambda m, n, k: (k, n))
    elif mode == "nt":
        if b_stack:
            b_spec = pl.BlockSpec((None, tn, tk), lambda m, n, k: (k, n, 0))
        else:
            b_spec = pl.BlockSpec((tn, tk), lambda m, n, k: (n, k))
    else:
        if b_parts:
            b_spec = pl.BlockSpec((None, tk, tn), lambda m, n, k: (parts_idx(n, nn, b_parts)[0], k, parts_idx(n, nn, b_parts)[1]))
        else:
            b_spec = pl.BlockSpec((tk, tn), lambda m, n, k: (k, n))
    if out_stack:
        out_shape = jax.ShapeDtypeStruct((N_DEV, M, tn), out_dtype)
        o_spec = pl.BlockSpec((None, tm, tn), lambda m, n, k: (n, m, 0))
    elif out_parts:
        out_shape = jax.ShapeDtypeStruct((out_parts, M, N // out_parts), out_dtype)
        o_spec = pl.BlockSpec((None, tm, tn), lambda m, n, k: (parts_idx(n, nn, out_parts)[0], m, parts_idx(n, nn, out_parts)[1]))
    else:
        out_shape = jax.ShapeDtypeStruct((M, N), out_dtype)
        o_spec = pl.BlockSpec((tm, tn), lambda m, n, k: (m, n))
    has_resid = resid is not None

    n_ex = len(exchange)
    n_in = 2 + has_resid + len(after)

    def body(*refs):
        a_ref, b_ref = refs[:2]
        r_ref = refs[2] if has_resid else None
        ex_in = refs[n_in:n_in + n_ex]
        o_ref = refs[n_in + n_ex]
        ex_out = refs[n_in + n_ex + 1:n_in + 2 * n_ex + 1]
        scratch = refs[n_in + 2 * n_ex + 1:]
        m_i, n_i, k = pl.program_id(0), pl.program_id(1), pl.program_id(2)

        def pushes():
            send_sems, recv_sems = scratch[-2:]
            x, y, c = lax.axis_index("x"), lax.axis_index("y"), lax.axis_index("c")
            return [pltpu.make_async_remote_copy(
                src_ref=ex_in[w].at[:, 1 - c], dst_ref=ex_out[w], send_sem=send_sems.at[w], recv_sem=recv_sems.at[w],
                device_id=(x, y, 1 - c), device_id_type=MESH) for w in range(n_ex)]

        if n_ex:
            @pl.when((m_i == 0) & (n_i == 0) & (k == 0))
            def _():
                for cp in pushes():
                    cp.start()

        if nk == 1:
            res = lax.dot_general(a_ref[...], b_ref[...], dims, preferred_element_type=F32)
            if has_resid:
                res = res + r_ref[...]
            o_ref[...] = res.astype(o_ref.dtype)
        else:
            acc = scratch[0]

            @pl.when(k == 0)
            def _():
                acc[...] = jnp.zeros_like(acc)

            acc[...] += lax.dot_general(a_ref[...], b_ref[...], dims, preferred_element_type=F32)

            @pl.when(k == nk - 1)
            def _():
                res = acc[...]
                if has_resid:
                    res = res + r_ref[...]
                o_ref[...] = res.astype(o_ref.dtype)

        if n_ex:
            @pl.when((m_i == nm - 1) & (n_i == nn - 1) & (k == nk - 1))
            def _():
                for cp in pushes():
                    cp.wait()

    in_specs = [a_spec, b_spec]
    args = [a, b]
    if has_resid:
        in_specs.append(pl.BlockSpec((tm, tn), lambda m, n, k: (m, n)))
        args.append(resid)
    in_specs += _hidden_specs(after) + _hidden_specs(exchange)
    args += list(after) + list(exchange)
    scratch_shapes = [pltpu.VMEM((tm, tn), F32)] if nk > 1 else []
    if not n_ex:
        return pl.pallas_call(
            body, name=name, grid=(nm, nn, nk), in_specs=in_specs, out_specs=o_spec, out_shape=out_shape,
            scratch_shapes=scratch_shapes, compiler_params=_params(("parallel", "parallel", "arbitrary")),
        )(*args)
    res = pl.pallas_call(
        body, name=name, grid=(nm, nn, nk), in_specs=in_specs, out_specs=[o_spec] + _hidden_specs(exchange),
        out_shape=[out_shape] + [jax.ShapeDtypeStruct((g.shape[0],) + g.shape[2:], g.dtype) for g in exchange],
        scratch_shapes=scratch_shapes + [pltpu.SemaphoreType.DMA((n_ex,)), pltpu.SemaphoreType.DMA((n_ex,))],
        compiler_params=_params(("arbitrary", "arbitrary", "arbitrary")),
    )(*args)
    return res[0], list(res[1:])


def _rms_fwd(x, g, name):
    R, D = x.shape
    tr = _pick(R, (256,))

    def body(x_ref, g_ref, h_ref, r_ref):
        xv = x_ref[...]
        r = lax.rsqrt(jnp.mean(xv * xv, axis=-1, keepdims=True) + EPS)
        h_ref[...] = (xv * r * g_ref[...]).astype(BF)
        r_ref[...] = r

    return pl.pallas_call(
        body, name=name, grid=(R // tr,),
        in_specs=[pl.BlockSpec((tr, D), lambda i: (i, 0)), pl.BlockSpec((1, D), lambda i: (0, 0))],
        out_specs=[pl.BlockSpec((tr, D), lambda i: (i, 0)), pl.BlockSpec((tr, 1), lambda i: (i, 0))],
        out_shape=[jax.ShapeDtypeStruct((R, D), BF), jax.ShapeDtypeStruct((R, 1), F32)],
        compiler_params=_params(("parallel",)),
    )(x, g)


def _rms_bwd(x, r, g, dh, dres, name, after=()):
    R, D = x.shape
    tr = _pick(R, (256,))
    has_res = dres is not None

    def body(*refs):
        if has_res:
            x_ref, r_ref, g_ref, dh_ref, dres_ref, dx_ref, dxb_ref, dg_ref = refs
        else:
            x_ref, r_ref, g_ref, dh_ref, dx_ref, dxb_ref, dg_ref = refs
        i = pl.program_id(0)
        xv, rv, dhv = x_ref[...], r_ref[...], dh_ref[...]
        gy = dhv * g_ref[...]
        c = jnp.sum(xv * gy, axis=-1, keepdims=True)
        dx = rv * gy - xv * (rv * rv * rv) * (c * (1.0 / D))
        if has_res:
            dx = dx + dres_ref[...]
        dx_ref[...] = dx
        dxb_ref[...] = dx.astype(BF)
        part = jnp.sum(dhv * xv * rv, axis=0, keepdims=True)

        @pl.when(i == 0)
        def _():
            dg_ref[...] = part

        @pl.when(i > 0)
        def _():
            dg_ref[...] += part

    row = pl.BlockSpec((tr, D), lambda i: (i, 0))
    in_specs = [row, pl.BlockSpec((tr, 1), lambda i: (i, 0)), pl.BlockSpec((1, D), lambda i: (0, 0)), row]
    args = [x, r, g, dh]
    if has_res:
        in_specs.append(row)
        args.append(dres)
    return pl.pallas_call(
        _hide(body, len(args), len(after)), name=name, grid=(R // tr,), in_specs=in_specs + _hidden_specs(after),
        out_specs=[row, row, pl.BlockSpec((1, D), lambda i: (0, 0))],
        out_shape=[jax.ShapeDtypeStruct((R, D), F32), jax.ShapeDtypeStruct((R, D), BF), jax.ShapeDtypeStruct((1, D), F32)],
        compiler_params=_params(("arbitrary",)),
    )(*args, *after)


def _a_chunk(us, vs, gvs, ws, bs):
    r_i = lax.broadcasted_iota(jnp.int32, (CHUNK, CHUNK), 0)
    c_i = lax.broadcasted_iota(jnp.int32, (CHUNK, CHUNK), 1)
    causal = r_i >= c_i
    vg = [jax.nn.gelu(v) for v in vs]
    ss = sum(jnp.sum(v * v, axis=-1, keepdims=True) for v in vg)
    r = lax.rsqrt(ss * (1.0 / A_WIDTH) + EPS)
    ys = []
    for g in range(A_GROUPS):
        vn = vg[g] * r * gvs[g]
        w = jnp.where(causal, ws[g], 0.0)
        s = jnp.dot(w.astype(BF), vn.astype(BF), preferred_element_type=F32) + bs[g]
        ys.append(jax.nn.gelu(us[g]) * s)
    return ys


def _a_split(u_ref, v_ref, g_ref, w_ref, b_ref):
    sl = [slice(g * 128, (g + 1) * 128) for g in range(A_GROUPS)]
    return ([u_ref[:, s] for s in sl], [v_ref[:, s] for s in sl], [g_ref[:, s] for s in sl],
            [w_ref[g] for g in range(A_GROUPS)], [b_ref[:, g:g + 1] for g in range(A_GROUPS)])


def _a_specs(S):
    return [pl.BlockSpec((CHUNK, A_WIDTH), lambda n: (n, 0)), pl.BlockSpec((CHUNK, A_WIDTH), lambda n: (n, 1)),
            pl.BlockSpec((1, A_WIDTH), lambda n: (0, 0)), pl.BlockSpec((A_GROUPS, CHUNK, CHUNK), lambda n: (0, 0, 0)),
            pl.BlockSpec((CHUNK, A_GROUPS), lambda n: (0, 0))]


def _a_fwd(proj, g_v, w_s, b_t):
    S = proj.shape[0]

    def body(u_ref, v_ref, g_ref, w_ref, b_ref, y_ref):
        ys = _a_chunk(*_a_split(u_ref, v_ref, g_ref, w_ref, b_ref))
        for g in range(A_GROUPS):
            y_ref[:, g * 128:(g + 1) * 128] = ys[g].astype(BF)

    return pl.pallas_call(
        body, name="a_fwd", grid=(S // CHUNK,), in_specs=_a_specs(S),
        out_specs=pl.BlockSpec((CHUNK, A_WIDTH), lambda n: (n, 0)),
        out_shape=jax.ShapeDtypeStruct((S, A_WIDTH), BF), compiler_params=_params(("parallel",)),
    )(proj, proj, g_v, w_s, b_t)


def _a_bwd(proj, g_v, w_s, b_t, dy, after=()):
    S = proj.shape[0]

    def body(u_ref, v_ref, g_ref, w_ref, b_ref, dy_ref, duv_ref, dg_ref, dw_ref, db_ref):
        n = pl.program_id(0)
        dys = [dy_ref[:, g * 128:(g + 1) * 128] for g in range(A_GROUPS)]
        _, vjp = jax.vjp(_a_chunk, *_a_split(u_ref, v_ref, g_ref, w_ref, b_ref))
        dus, dvs, dgs, dws, dbs = vjp(dys)

        @pl.when(n == 0)
        def _():
            dg_ref[...] = jnp.zeros_like(dg_ref)
            dw_ref[...] = jnp.zeros_like(dw_ref)
            db_ref[...] = jnp.zeros_like(db_ref)

        for g in range(A_GROUPS):
            duv_ref[:, g * 128:(g + 1) * 128] = dus[g].astype(BF)
            duv_ref[:, A_WIDTH + g * 128:A_WIDTH + (g + 1) * 128] = dvs[g].astype(BF)
            dg_ref[:, g * 128:(g + 1) * 128] += dgs[g]
            dw_ref[g] += dws[g]
            db_ref[:, g:g + 1] += dbs[g]

    return pl.pallas_call(
        _hide(body, 6, len(after)), name="a_bwd", grid=(S // CHUNK,),
        in_specs=_a_specs(S) + [pl.BlockSpec((CHUNK, A_WIDTH), lambda n: (n, 0))] + _hidden_specs(after),
        out_specs=[pl.BlockSpec((CHUNK, 2 * A_WIDTH), lambda n: (n, 0)), pl.BlockSpec((1, A_WIDTH), lambda n: (0, 0)),
                   pl.BlockSpec((A_GROUPS, CHUNK, CHUNK), lambda n: (0, 0, 0)), pl.BlockSpec((CHUNK, A_GROUPS), lambda n: (0, 0))],
        out_shape=[jax.ShapeDtypeStruct((S, 2 * A_WIDTH), BF), jax.ShapeDtypeStruct((1, A_WIDTH), F32),
                   jax.ShapeDtypeStruct((A_GROUPS, CHUNK, CHUNK), F32), jax.ShapeDtypeStruct((CHUNK, A_GROUPS), F32)],
        compiler_params=_params(("arbitrary",)),
    )(proj, proj, g_v, w_s, b_t, dy, *after)


def _half_mask(shape, which):
    lane = lax.broadcasted_iota(jnp.int32, shape, len(shape) - 1)
    return (lane >= 64) == (which == 1)


def _pair_norm_rope(x, g, ct, sa, sb):
    lo = _half_mask(x.shape, 0)
    x2 = x * x
    ss_lo = jnp.sum(jnp.where(lo, x2, 0.0), axis=-1, keepdims=True)
    ss_hi = jnp.sum(jnp.where(lo, 0.0, x2), axis=-1, keepdims=True)
    r = jnp.where(lo, lax.rsqrt(ss_lo * (1.0 / B_HEAD_DIM) + EPS), lax.rsqrt(ss_hi * (1.0 / B_HEAD_DIM) + EPS))
    xr = x * r
    xn = xr * g
    out = xn * ct + pltpu.roll(xn, 120, 1) * sa + pltpu.roll(xn, 8, 1) * sb
    return out, xr, r


def _pair_norm_rope_bwd(x, g, ct, sa, sb, dout):
    lo = _half_mask(x.shape, 0)
    _, xr, r = _pair_norm_rope(x, g, ct, sa, sb)
    dxn = dout * ct + pltpu.roll(dout * sa, 8, 1) + pltpu.roll(dout * sb, 120, 1)
    gy = dxn * g
    t = xr * gy
    c_lo = jnp.sum(jnp.where(lo, t, 0.0), axis=-1, keepdims=True)
    c_hi = jnp.sum(jnp.where(lo, 0.0, t), axis=-1, keepdims=True)
    c = jnp.where(lo, c_lo, c_hi)
    dx = r * (gy - xr * c * (1.0 / B_HEAD_DIM))
    dg = jnp.sum(dxn * xr, axis=0, keepdims=True)
    return dx, dg


def _b_pre(proj, gq2, gk2, ct, sa, sb):
    S = proj.shape[0]
    tr = _pick(S, (256,))
    n_pair = B_WIDTH // 128

    def body(q_ref, k_ref, gq_ref, gk_ref, ct_ref, sa_ref, sb_ref, qn_ref, kn_ref):
        ct_v, sa_v, sb_v = ct_ref[...], sa_ref[...], sb_ref[...]
        for p in range(n_pair):
            o, _, _ = _pair_norm_rope(q_ref[:, p * 128:(p + 1) * 128], gq_ref[...], ct_v, sa_v, sb_v)
            qn_ref[:, p * 128:(p + 1) * 128] = o.astype(BF)
        o, _, _ = _pair_norm_rope(k_ref[...], gk_ref[...], ct_v, sa_v, sb_v)
        kn_ref[...] = o.astype(BF)

    tab = pl.BlockSpec((tr, 128), lambda i: (i, 0))
    gsp = pl.BlockSpec((1, 128), lambda i: (0, 0))
    return pl.pallas_call(
        body, name="b_pre", grid=(S // tr,),
        in_specs=[pl.BlockSpec((tr, B_WIDTH), lambda i: (i, 1)), pl.BlockSpec((tr, 128), lambda i: (i, 2 * B_WIDTH // 128)),
                  gsp, gsp, tab, tab, tab],
        out_specs=[pl.BlockSpec((tr, B_WIDTH), lambda i: (i, 0)), tab],
        out_shape=[jax.ShapeDtypeStruct((S, B_WIDTH), BF), jax.ShapeDtypeStruct((S, 128), BF)],
        compiler_params=_params(("parallel",)),
    )(proj, proj, gq2, gk2, ct, sa, sb)


def _b_pre_bwd(proj, gq2, gk2, ct, sa, sb, dqn, dkn, dv):
    S = proj.shape[0]
    tr = _pick(S, (256,))
    n_pair = B_WIDTH // 128

    def body(q_ref, k_ref, gq_ref, gk_ref, ct_ref, sa_ref, sb_ref, dqn_ref, dkn_ref, dv_ref, dqkv_ref, dgq_ref, dgk_ref):
        i = pl.program_id(0)
        ct_v, sa_v, sb_v = ct_ref[...], sa_ref[...], sb_ref[...]
        dgq = jnp.zeros((1, 128), F32)
        for p in range(n_pair):
            sl = slice(p * 128, (p + 1) * 128)
            dx, dg = _pair_norm_rope_bwd(q_ref[:, sl], gq_ref[...], ct_v, sa_v, sb_v, dqn_ref[:, sl])
            dqkv_ref[:, sl] = dx.astype(BF)
            dgq = dgq + dg
        dx, dgk = _pair_norm_rope_bwd(k_ref[...], gk_ref[...], ct_v, sa_v, sb_v, dkn_ref[...])
        dqkv_ref[:, B_WIDTH:B_WIDTH + 128] = dx.astype(BF)
        dqkv_ref[:, B_WIDTH + 128:B_WIDTH + 256] = dv_ref[...].astype(BF)

        @pl.when(i == 0)
        def _():
            dgq_ref[...] = dgq
            dgk_ref[...] = dgk

        @pl.when(i > 0)
        def _():
            dgq_ref[...] += dgq
            dgk_ref[...] += dgk

    tab = pl.BlockSpec((tr, 128), lambda i: (i, 0))
    gsp = pl.BlockSpec((1, 128), lambda i: (0, 0))
    return pl.pallas_call(
        body, name="b_pre_bwd", grid=(S // tr,),
        in_specs=[pl.BlockSpec((tr, B_WIDTH), lambda i: (i, 1)), pl.BlockSpec((tr, 128), lambda i: (i, 2 * B_WIDTH // 128)),
                  gsp, gsp, tab, tab, tab, pl.BlockSpec((tr, B_WIDTH), lambda i: (i, 0)), tab, tab],
        out_specs=[pl.BlockSpec((tr, B_WIDTH + 256), lambda i: (i, 0)), gsp, gsp],
        out_shape=[jax.ShapeDtypeStruct((S, B_WIDTH + 256), BF), jax.ShapeDtypeStruct((1, 128), F32), jax.ShapeDtypeStruct((1, 128), F32)],
        compiler_params=_params(("arbitrary",)),
    )(proj, proj, gq2, gk2, ct, sa, sb, dqn, dkn, dv)


def _b_dup(x2, g):
    d = jnp.where(_half_mask(x2.shape, g), x2, 0.0)
    return (d + pltpu.roll(d, 64, 1)).astype(BF)


PAIRS_PER_GROUP = B_HEADS // B_KV_HEADS // 2
GROUP_ROWS = PAIRS_PER_GROUP * CHUNK


def _b_valid(n):
    row = lax.broadcasted_iota(jnp.int32, (GROUP_ROWS, 2 * CHUNK), 0) & (CHUNK - 1)
    col = lax.broadcasted_iota(jnp.int32, (GROUP_ROWS, 2 * CHUNK), 1)
    rel = row + CHUNK - col
    return (rel >= 0) & (rel < CHUNK) & ((col >= CHUNK) | (n > 0))


def _b_blocks(x2, g):
    xd = _b_dup(x2, g)
    lo = _half_mask(xd.shape, 0)
    zero = jnp.zeros_like(xd)
    return jnp.concatenate([jnp.where(lo, xd, zero), jnp.where(lo, zero, xd)], axis=0)


def _b_sink_col(s_ref, g, hf):
    rb = lax.broadcasted_iota(jnp.int32, (GROUP_ROWS, 1), 0) // CHUNK
    col = jnp.zeros((GROUP_ROWS, 1), F32)
    for pp in range(PAIRS_PER_GROUP):
        col = jnp.where(rb == pp, s_ref[0, 2 * (g * PAIRS_PER_GROUP + pp) + hf], col)
    return col


def _b_probs(qs, kblk, valid, sinks):
    s = lax.dot_general(qs, kblk, (((1,), (1,)), ((), ())), preferred_element_type=F32) * (B_HEAD_DIM ** -0.5)
    out = []
    for hf in range(2):
        sh = jnp.where(valid, s[:, hf * 2 * CHUNK:(hf + 1) * 2 * CHUNK], NEG)
        m = jnp.maximum(jnp.max(sh, axis=-1, keepdims=True), sinks[hf])
        e = jnp.exp(sh - m)
        es = jnp.exp(sinks[hf] - m)
        inv = 1.0 / (jnp.sum(e, axis=-1, keepdims=True) + es)
        out.append((e * inv, es * inv))
    return out


def _b_fold(acc, g):
    lo = _half_mask((2 * CHUNK, 128), 0)
    t = jnp.where(lo, acc[:2 * CHUNK], 0.0) + jnp.where(lo, 0.0, acc[2 * CHUNK:])
    return jnp.where(_half_mask((2 * CHUNK, 128), g), t + pltpu.roll(t, 64, 1), 0.0)


def _b_kv_specs(S):
    prev = lambda n: (jnp.maximum(n - 1, 0), 0)
    cur = lambda n: (n, 0)
    v_col = (2 * B_WIDTH + B_KV_WIDTH) // 128
    return [pl.BlockSpec((CHUNK, 128), prev), pl.BlockSpec((CHUNK, 128), cur),
            pl.BlockSpec((CHUNK, 128), lambda n: (jnp.maximum(n - 1, 0), v_col)), pl.BlockSpec((CHUNK, 128), lambda n: (n, v_col))]


def _b_attn_fwd(qn, kn, proj, sinks):
    S = qn.shape[0]

    def body(s_ref, q_ref, kp_ref, kc_ref, vp_ref, vc_ref, y_ref):
        n = pl.program_id(0)
        valid = _b_valid(n)
        k2 = jnp.concatenate([kp_ref[...], kc_ref[...]], axis=0).astype(F32)
        v2 = jnp.concatenate([vp_ref[...], vc_ref[...]], axis=0)
        for g in range(B_KV_HEADS):
            pairs = [g * PAIRS_PER_GROUP + pp for pp in range(PAIRS_PER_GROUP)]
            qs = jnp.concatenate([q_ref[:, p * 128:(p + 1) * 128] for p in pairs], axis=0)
            probs = _b_probs(qs, _b_blocks(k2, g), valid, [_b_sink_col(s_ref, g, hf) for hf in range(2)])
            pcat = jnp.concatenate([probs[0][0].astype(BF), probs[1][0].astype(BF)], axis=1)
            o = jnp.dot(pcat, _b_blocks(v2, g), preferred_element_type=F32)
            for pp, p in enumerate(pairs):
                y_ref[:, p * 128:(p + 1) * 128] = o[pp * CHUNK:(pp + 1) * CHUNK].astype(BF)

    return pl.pallas_call(
        body, name="b_attn_fwd", grid=(S // CHUNK,),
        in_specs=[pl.BlockSpec(memory_space=pltpu.SMEM), pl.BlockSpec((CHUNK, B_WIDTH), lambda n: (n, 0))] + _b_kv_specs(S),
        out_specs=pl.BlockSpec((CHUNK, B_WIDTH), lambda n: (n, 0)),
        out_shape=jax.ShapeDtypeStruct((S, B_WIDTH), BF), compiler_params=_params(("arbitrary",)),
    )(sinks, qn, kn, kn, proj, proj)


def _b_attn_bwd(qn, kn, proj, sinks, dy, after=()):
    S = qn.shape[0]

    def body(s_ref, q_ref, kp_ref, kc_ref, vp_ref, vc_ref, dy_ref, dq_ref, dk_ref, dv_ref, ds_ref):
        n = pl.program_id(0)

        @pl.when(n == 0)
        def _():
            dk_ref[...] = jnp.zeros_like(dk_ref)
            dv_ref[...] = jnp.zeros_like(dv_ref)
            ds_ref[...] = jnp.zeros_like(ds_ref)

        valid = _b_valid(n)
        k2 = jnp.concatenate([kp_ref[...], kc_ref[...]], axis=0).astype(F32)
        v2 = jnp.concatenate([vp_ref[...], vc_ref[...]], axis=0)
        lane = lax.broadcasted_iota(jnp.int32, (CHUNK, 128), 1)
        dk2 = jnp.zeros((2 * CHUNK, 128), F32)
        dv2 = jnp.zeros((2 * CHUNK, 128), F32)
        dsink = jnp.zeros((CHUNK, 128), F32)
        scale = B_HEAD_DIM ** -0.5
        nt = (((1,), (1,)), ((), ()))
        tn = (((0,), (0,)), ((), ()))
        for g in range(B_KV_HEADS):
            pairs = [g * PAIRS_PER_GROUP + pp for pp in range(PAIRS_PER_GROUP)]
            qs = jnp.concatenate([q_ref[:, p * 128:(p + 1) * 128] for p in pairs], axis=0)
            do = jnp.concatenate([dy_ref[:, p * 128:(p + 1) * 128] for p in pairs], axis=0)
            do_b = do.astype(BF)
            kblk, vblk = _b_blocks(k2, g), _b_blocks(v2, g)
            probs = _b_probs(qs, kblk, valid, [_b_sink_col(s_ref, g, hf) for hf in range(2)])
            pcat = jnp.concatenate([probs[0][0].astype(BF), probs[1][0].astype(BF)], axis=1)
            o = jnp.dot(pcat, vblk, preferred_element_type=F32)
            dp = lax.dot_general(do_b, vblk, nt, preferred_element_type=F32)
            prod = do * o
            ds_halves = []
            for hf in range(2):
                pr, ps = probs[hf]
                delta = jnp.sum(jnp.where(_half_mask(prod.shape, hf), prod, 0.0), axis=-1, keepdims=True)
                ds_halves.append((pr * (dp[:, hf * 2 * CHUNK:(hf + 1) * 2 * CHUNK] - delta) * scale).astype(BF))
                t = -ps * delta
                for pp, p in enumerate(pairs):
                    dsink = dsink + jnp.where(lane == 2 * p + hf, t[pp * CHUNK:(pp + 1) * CHUNK], 0.0)
            dsc = jnp.concatenate(ds_halves, axis=1)
            dq = jnp.dot(dsc, kblk, preferred_element_type=F32)
            for pp, p in enumerate(pairs):
                dq_ref[:, p * 128:(p + 1) * 128] = dq[pp * CHUNK:(pp + 1) * CHUNK]
            dk2 = dk2 + _b_fold(lax.dot_general(dsc, qs, tn, preferred_element_type=F32), g)
            dv2 = dv2 + _b_fold(lax.dot_general(pcat, do_b, tn, preferred_element_type=F32), g)
        ds_ref[...] += dsink
        cur = pl.ds(pl.multiple_of(n * CHUNK, CHUNK), CHUNK)
        dk_ref[cur, :] += dk2[CHUNK:]
        dv_ref[cur, :] += dv2[CHUNK:]

        @pl.when(n > 0)
        def _():
            prv = pl.ds(pl.multiple_of((n - 1) * CHUNK, CHUNK), CHUNK)
            dk_ref[prv, :] += dk2[:CHUNK]
            dv_ref[prv, :] += dv2[:CHUNK]

    full = pl.BlockSpec((S, 128), lambda n: (0, 0))
    return pl.pallas_call(
        _hide(body, 7, len(after)), name="b_attn_bwd", grid=(S // CHUNK,),
        in_specs=[pl.BlockSpec(memory_space=pltpu.SMEM), pl.BlockSpec((CHUNK, B_WIDTH), lambda n: (n, 0))] + _b_kv_specs(S)
        + [pl.BlockSpec((CHUNK, B_WIDTH), lambda n: (n, 0))] + _hidden_specs(after),
        out_specs=[pl.BlockSpec((CHUNK, B_WIDTH), lambda n: (n, 0)), full, full, pl.BlockSpec((CHUNK, 128), lambda n: (0, 0))],
        out_shape=[jax.ShapeDtypeStruct((S, B_WIDTH), F32), jax.ShapeDtypeStruct((S, 128), F32), jax.ShapeDtypeStruct((S, 128), F32),
                   jax.ShapeDtypeStruct((CHUNK, 128), F32)],
        compiler_params=_params(("arbitrary",)),
    )(sinks, qn, kn, kn, proj, proj, dy, *after)


def _c_block(q, k, v, gq, gk):
    qn = q * lax.rsqrt(jnp.mean(q * q, axis=-1, keepdims=True) + EPS) * gq
    kn = k * lax.rsqrt(jnp.mean(k * k, axis=-1, keepdims=True) + EPS) * gk
    s = lax.dot_general(qn.astype(BF), kn.astype(BF), (((1,), (1,)), ((), ())), preferred_element_type=F32) * (C_HEAD_DIM ** -0.5)
    p = jax.nn.softmax(s, axis=-1)
    return jnp.dot(p.astype(BF), v.astype(BF), preferred_element_type=F32)


def _c_specs(S, M, tq):
    q_col = (2 * A_WIDTH + B_WIDTH + 2 * B_KV_WIDTH) // 128
    return [pl.BlockSpec((tq, 128), lambda h, i: (i, q_col + h)), pl.BlockSpec((M, 128), lambda h, i: (0, h)),
            pl.BlockSpec((M, 128), lambda h, i: (0, C_HEADS + h)), pl.BlockSpec((1, 128), lambda h, i: (0, 0)),
            pl.BlockSpec((1, 128), lambda h, i: (0, 0))]


def _c_fwd(proj, kv, gq, gk):
    S, M = proj.shape[0], kv.shape[0]
    tq = _pick(S, (512,))

    def body(q_ref, k_ref, v_ref, gq_ref, gk_ref, y_ref):
        y_ref[...] = _c_block(q_ref[...], k_ref[...], v_ref[...], gq_ref[...], gk_ref[...]).astype(BF)

    return pl.pallas_call(
        body, name="c_fwd", grid=(C_HEADS, S // tq), in_specs=_c_specs(S, M, tq),
        out_specs=pl.BlockSpec((tq, 128), lambda h, i: (i, h)),
        out_shape=jax.ShapeDtypeStruct((S, C_WIDTH), BF), compiler_params=_params(("parallel", "parallel")),
    )(proj, kv, kv, gq, gk)


def _c_bwd(proj, kv, gq, gk, dy):
    S, M = proj.shape[0], kv.shape[0]
    tq = _pick(S, (512,))

    def body(q_ref, k_ref, v_ref, gq_ref, gk_ref, dy_ref, dq_ref, dk_ref, dv_ref, dgq_ref, dgk_ref):
        i = pl.program_id(1)
        _, vjp = jax.vjp(_c_block, q_ref[...], k_ref[...], v_ref[...], gq_ref[...], gk_ref[...])
        dq, dk, dv, dgq, dgk = vjp(dy_ref[...])
        dq_ref[...] = dq.astype(BF)

        @pl.when(i == 0)
        def _():
            dk_ref[...] = dk
            dv_ref[...] = dv
            dgq_ref[...] = dgq
            dgk_ref[...] = dgk

        @pl.when(i > 0)
        def _():
            dk_ref[...] += dk
            dv_ref[...] += dv
            dgq_ref[...] += dgq
            dgk_ref[...] += dgk

    return pl.pallas_call(
        body, name="c_bwd", grid=(C_HEADS, S // tq),
        in_specs=_c_specs(S, M, tq) + [pl.BlockSpec((tq, 128), lambda h, i: (i, h))],
        out_specs=[pl.BlockSpec((tq, 128), lambda h, i: (i, h)), pl.BlockSpec((M, 128), lambda h, i: (0, h)),
                   pl.BlockSpec((M, 128), lambda h, i: (0, h)), pl.BlockSpec((None, 1, 128), lambda h, i: (h, 0, 0)),
                   pl.BlockSpec((None, 1, 128), lambda h, i: (h, 0, 0))],
        out_shape=[jax.ShapeDtypeStruct((S, C_WIDTH), BF), jax.ShapeDtypeStruct((M, C_WIDTH), F32), jax.ShapeDtypeStruct((M, C_WIDTH), F32),
                   jax.ShapeDtypeStruct((C_HEADS, 1, 128), F32), jax.ShapeDtypeStruct((C_HEADS, 1, 128), F32)],
        compiler_params=_params(("parallel", "arbitrary")),
    )(proj, kv, kv, gq, gk, dy)


def _merge_specs(S, D, tm, tn, ks):
    off = GATE_OFF // tn
    nd = D // tn
    gates = [pl.BlockSpec((tm, tn), functools.partial(lambda b, m, n: (m, off + b * nd + n), b)) for b in range(3)]
    ys = [pl.BlockSpec((tm, k), lambda m, n: (m, 0)) for k in ks]
    ws = [pl.BlockSpec((None, k, tn), lambda m, n: (n, 0, 0)) for k in ks]
    return gates, ys, ws


def _merge_fwd(proj, ys, ws):
    S = proj.shape[0]
    tn = ws[0].shape[2]
    D = N_DEV * tn
    ks = [w.shape[1] for w in ws]
    tm = _pick(S, (1024,))
    gates, y_specs, w_specs = _merge_specs(S, D, tm, tn, ks)

    def body(ga_ref, gb_ref, gc_ref, ya_ref, yb_ref, yc_ref, wa_ref, wb_ref, wc_ref, m_ref, za_ref, zb_ref, zc_ref):
        acc = None
        for g_ref, y_ref, w_ref, z_ref in ((ga_ref, ya_ref, wa_ref, za_ref), (gb_ref, yb_ref, wb_ref, zb_ref),
                                           (gc_ref, yc_ref, wc_ref, zc_ref)):
            z = jnp.dot(y_ref[...], w_ref[...], preferred_element_type=F32)
            z_ref[...] = z.astype(BF)
            t = jax.nn.sigmoid(g_ref[...]) * z
            acc = t if acc is None else acc + t
        m_ref[...] = acc.astype(BF)

    tile = pl.BlockSpec((tm, tn), lambda m, n: (m, n))
    return pl.pallas_call(
        body, name="merge_fwd", grid=(S // tm, D // tn), in_specs=gates + y_specs + w_specs,
        out_specs=[tile, tile, tile, tile], out_shape=[jax.ShapeDtypeStruct((S, D), BF)] * 4,
        compiler_params=_params(("parallel", "parallel")),
    )(proj, proj, proj, *ys, *ws)


def _merge_bwd(proj, zs, dm, ws, after=()):
    S = proj.shape[0]
    tn = ws[0].shape[2]
    D = N_DEV * tn
    ks = [w.shape[1] for w in ws]
    tm = _pick(S, (1024,))
    gates, _, w_specs = _merge_specs(S, D, tm, tn, ks)
    nt = (((1,), (1,)), ((), ()))

    def body(ga_ref, gb_ref, gc_ref, za_ref, zb_ref, zc_ref, dm_ref, wa_ref, wb_ref, wc_ref,
             dza_ref, dzb_ref, dzc_ref, dga_ref, dgb_ref, dgc_ref, dya_ref, dyb_ref, dyc_ref):
        n = pl.program_id(1)
        dmv = dm_ref[...]
        for g_ref, z_ref, w_ref, dz_ref, dg_ref, dy_ref in (
                (ga_ref, za_ref, wa_ref, dza_ref, dga_ref, dya_ref), (gb_ref, zb_ref, wb_ref, dzb_ref, dgb_ref, dyb_ref),
                (gc_ref, zc_ref, wc_ref, dzc_ref, dgc_ref, dyc_ref)):
            sg = jax.nn.sigmoid(g_ref[...])
            dz = (sg * dmv).astype(BF)
            dz_ref[...] = dz
            dg_ref[...] = (dmv * z_ref[...].astype(F32) * sg * (1.0 - sg)).astype(BF)
            part = lax.dot_general(dz, w_ref[...], nt, preferred_element_type=F32)

            @pl.when(n == 0)
            def _():
                dy_ref[...] = part

            @pl.when(n > 0)
            def _():
                dy_ref[...] += part

    tile = pl.BlockSpec((tm, tn), lambda m, n: (m, n))
    dys = [pl.BlockSpec((tm, k), lambda m, n: (m, 0)) for k in ks]
    return pl.pallas_call(
        _hide(body, 10, len(after)), name="merge_bwd", grid=(S // tm, D // tn),
        in_specs=gates + [tile, tile, tile, tile] + w_specs + _hidden_specs(after),
        out_specs=[tile] * 6 + dys,
        out_shape=[jax.ShapeDtypeStruct((S, D), BF)] * 6 + [jax.ShapeDtypeStruct((S, k), F32) for k in ks],
        compiler_params=_params(("parallel", "arbitrary")),
    )(proj, proj, proj, *zs, dm, *ws, *after)


PAD = 8


def _stage_shift_down(us_ref, u_ref):
    S = u_ref.shape[1]
    us_ref[:, 0:PAD, :] = jnp.zeros((2, PAD, us_ref.shape[2]), F32)
    us_ref[:, PAD:S + PAD, :] = u_ref[...].astype(F32)


ROWS = 32


def _conv3(us_ref, part, r0, w, b):
    return (us_ref[part, pl.ds(r0 + PAD, ROWS), :] * w[2:3] + us_ref[part, pl.ds(r0 + PAD - 1, ROWS), :] * w[1:2]
            + us_ref[part, pl.ds(r0 + PAD - 2, ROWS), :] * w[0:1] + b)


def _ffn_specs(S, F, tc, c):
    per = c // tc

    def w_spec(half):
        return pl.BlockSpec((None, 3, tc), lambda j: (half * (N_DEV // 2) + j // per, 0, j % per))

    return [pl.BlockSpec((2, S, tc), lambda j: (0, 0, j)), w_spec(0), w_spec(1), pl.BlockSpec((2, 1, tc), lambda j: (0, 0, j))]


def _ffn_tile(F, c):
    tc = 128
    if c % tc or F % tc:
        raise ValueError(f"ffn tile {tc} does not divide {c}, {F}")
    return tc


def _ffn_act_fwd(up3, cws, cb3):
    _, S, F = up3.shape
    c = cws.shape[2]
    tc = _ffn_tile(F, c)

    def body(u_ref, wa_ref, wb_ref, b_ref, o_ref, us_ref):
        _stage_shift_down(us_ref, u_ref)
        wa, wb, ba, bb = wa_ref[...], wb_ref[...], b_ref[0], b_ref[1]

        def step(i, carry):
            r0 = pl.multiple_of(i * ROWS, ROWS)
            ca = _conv3(us_ref, 0, r0, wa, ba)
            cb = _conv3(us_ref, 1, r0, wb, bb)
            o_ref[pl.ds(r0, ROWS), :] = (ca * jax.nn.sigmoid(ca) * cb).astype(BF)
            return carry

        lax.fori_loop(0, S // ROWS, step, 0, unroll=4)

    return pl.pallas_call(
        body, name="ffn_act_fwd", grid=(F // tc,), in_specs=_ffn_specs(S, F, tc, c),
        out_specs=pl.BlockSpec((S, tc), lambda j: (0, j)), out_shape=jax.ShapeDtypeStruct((S, F), BF),
        scratch_shapes=[pltpu.VMEM((2, S + PAD, tc), F32)],
        compiler_params=_params(("parallel",)),
    )(up3, cws, cws, cb3)


def _ffn_act_bwd(up3, cws, cb3, dact, after=()):
    _, S, F = up3.shape
    c = cws.shape[2]
    tc = _ffn_tile(F, c)

    def body(u_ref, wa_ref, wb_ref, b_ref, da_ref, du_ref, dw_ref, db_ref, us_ref, dcs_ref):
        _stage_shift_down(us_ref, u_ref)
        ws = (wa_ref[...], wb_ref[...])
        ba, bb = b_ref[0], b_ref[1]
        dcs_ref[:, S:S + PAD, :] = jnp.zeros((2, PAD, tc), F32)

        def conv_grads(i, carry):
            r0 = pl.multiple_of(i * ROWS, ROWS)
            ca = _conv3(us_ref, 0, r0, ws[0], ba)
            cb = _conv3(us_ref, 1, r0, ws[1], bb)
            sg = jax.nn.sigmoid(ca)
            dav = da_ref[pl.ds(r0, ROWS), :].astype(F32)
            dcs_ref[0, pl.ds(r0, ROWS), :] = dav * cb * sg * (1.0 + ca * (1.0 - sg))
            dcs_ref[1, pl.ds(r0, ROWS), :] = dav * ca * sg
            return carry

        lax.fori_loop(0, S // ROWS, conv_grads, 0, unroll=4)

        def fold(v):
            return jnp.sum(v.reshape(ROWS // 8, 8, tc), axis=0)

        def input_grads(i, acc):
            r0 = pl.multiple_of(i * ROWS, ROWS)
            new = []
            for part in range(2):
                w = ws[part]
                dc = dcs_ref[part, pl.ds(r0, ROWS), :]
                dc1 = dcs_ref[part, pl.ds(r0 + 1, ROWS), :]
                dc2 = dcs_ref[part, pl.ds(r0 + 2, ROWS), :]
                u = us_ref[part, pl.ds(r0 + PAD, ROWS), :]
                du_ref[part, pl.ds(r0, ROWS), :] = (dc * w[2:3] + dc1 * w[1:2] + dc2 * w[0:1]).astype(BF)
                sums = (fold(dc2 * u), fold(dc1 * u), fold(dc * u), fold(dc))
                new += [a + s for a, s in zip(acc[4 * part:4 * part + 4], sums)]
            return tuple(new)

        acc = lax.fori_loop(0, S // ROWS, input_grads, tuple(jnp.zeros((8, tc), F32) for _ in range(8)), unroll=4)
        for part in range(2):
            for j in range(3):
                dw_ref[part, j:j + 1, :] = jnp.sum(acc[4 * part + j], axis=0, keepdims=True)
            db_ref[part] = jnp.sum(acc[4 * part + 3], axis=0, keepdims=True)

    return pl.pallas_call(
        _hide(body, 5, len(after)), name="ffn_act_bwd", grid=(F // tc,),
        in_specs=_ffn_specs(S, F, tc, c) + [pl.BlockSpec((S, tc), lambda j: (0, j))] + _hidden_specs(after),
        out_specs=[pl.BlockSpec((2, S, tc), lambda j: (0, 0, j)), pl.BlockSpec((2, 3, tc), lambda j: (0, 0, j)),
                   pl.BlockSpec((2, 1, tc), lambda j: (0, 0, j))],
        out_shape=[jax.ShapeDtypeStruct((2, S, F), BF), jax.ShapeDtypeStruct((2, 3, F), F32), jax.ShapeDtypeStruct((2, 1, F), F32)],
        scratch_shapes=[pltpu.VMEM((2, S + PAD, tc), F32), pltpu.VMEM((2, S + PAD, tc), F32)],
        compiler_params=_params(("parallel",)),
    )(up3, cws, cws, cb3, dact, *after)


def _residual_rms(a, w, x, g, name, tm=512):
    S, K = a.shape
    D = w.shape[1]
    tm = _pick(S, (tm,))

    def body(a_ref, w_ref, x_ref, g_ref, x1_ref, h_ref, r_ref):
        x1 = jnp.dot(a_ref[...], w_ref[...], preferred_element_type=F32) + x_ref[...]
        r = lax.rsqrt(jnp.mean(x1 * x1, axis=-1, keepdims=True) + EPS)
        x1_ref[...] = x1
        h_ref[...] = (x1 * r * g_ref[...]).astype(BF)
        r_ref[...] = r

    row = pl.BlockSpec((tm, D), lambda i: (i, 0))
    return pl.pallas_call(
        body, name=name, grid=(S // tm,),
        in_specs=[pl.BlockSpec((tm, K), lambda i: (i, 0)), pl.BlockSpec((K, D), lambda i: (0, 0)), row, pl.BlockSpec((1, D), lambda i: (0, 0))],
        out_specs=[row, row, pl.BlockSpec((tm, 1), lambda i: (i, 0))],
        out_shape=[jax.ShapeDtypeStruct((S, D), F32), jax.ShapeDtypeStruct((S, D), BF), jax.ShapeDtypeStruct((S, 1), F32)],
        compiler_params=_params(("parallel",)),
    )(a, w, x, g)


def _out_loss(act, w_down, x1, target, tm=512, tn=1024, tk=1408):
    S, F = act.shape
    D = w_down.shape[1]
    tm, tn, tk = _pick(S, (tm,)), _pick(D, (tn,)), _pick(F, (tk,))
    nm, nn, nk = S // tm, D // tn, F // tk

    def body(a_ref, b_ref, x_ref, t_ref, dy_ref, dyb_ref, l_ref, acc):
        m, n, k = pl.program_id(0), pl.program_id(1), pl.program_id(2)

        @pl.when((m == 0) & (n == 0) & (k == 0))
        def _():
            l_ref[...] = jnp.zeros_like(l_ref)

        @pl.when(k == 0)
        def _():
            acc[...] = jnp.zeros_like(acc)

        acc[...] += jnp.dot(a_ref[...], b_ref[...], preferred_element_type=F32)

        @pl.when(k == nk - 1)
        def _():
            e = acc[...] + x_ref[...] - t_ref[...]
            dy = e * (1.0 / D)
            dy_ref[...] = dy
            dyb_ref[...] = dy.astype(BF)
            l_ref[...] += jnp.sum(jnp.sum(e * e, axis=-1, keepdims=True), axis=0, keepdims=True) * (0.5 / D)

    tile = pl.BlockSpec((tm, tn), lambda m, n, k: (m, n))
    return pl.pallas_call(
        body, name="mm_y_loss", grid=(nm, nn, nk),
        in_specs=[pl.BlockSpec((tm, tk), lambda m, n, k: (m, k)), pl.BlockSpec((tk, tn), lambda m, n, k: (k, n)), tile, tile],
        out_specs=[tile, tile, pl.BlockSpec((8, 128), lambda m, n, k: (0, 0))],
        out_shape=[jax.ShapeDtypeStruct((S, D), F32), jax.ShapeDtypeStruct((S, D), BF), jax.ShapeDtypeStruct((8, 128), F32)],
        scratch_shapes=[pltpu.VMEM((tm, tn), F32)],
        compiler_params=_params(("arbitrary", "arbitrary", "arbitrary")),
    )(act, w_down, x1, target)


def _allgather(shards, name):
    n = len(shards)

    def body(*refs):
        ins, outs = refs[:n], refs[n:2 * n]
        send_sems, recv_sems, local_sems = refs[2 * n:]
        x, y, c = lax.axis_index("x"), lax.axis_index("y"), lax.axis_index("c")
        me, sibling = (x, y, c), (x, y, 1 - c)
        chips = [(1 - x, y), (x, 1 - y), (1 - x, 1 - y)]

        def blk(w, px, py, pc):
            return outs[w].at[4 * px + 2 * py + pc]

        def copy(w, k, block, to, src=None):
            return pltpu.make_async_remote_copy(
                src_ref=blk(w, *block) if src is None else src, dst_ref=blk(w, *block),
                send_sem=send_sems.at[w, k], recv_sem=recv_sems.at[w, k], device_id=to, device_id_type=MESH)

        started = []
        mine = []
        for w in range(n):
            mine.append(pltpu.make_async_copy(ins[w], blk(w, *me), local_sems.at[w]))
            mine[-1].start()
            first = [copy(w, 0, me, sibling, src=ins[w])]
            first += [copy(w, 1 + j, me, (*chip, c), src=ins[w]) for j, chip in enumerate(chips)]
            for cp in first:
                cp.start()
            started += first
        for w in range(n):
            for j, chip in enumerate(chips):
                copy(w, 1 + j, (*chip, c), me).wait_recv()
                fwd = copy(w, 4 + j, (*chip, c), sibling)
                fwd.start()
                started.append(fwd)
        for w in range(n):
            copy(w, 0, sibling, me).wait_recv()
            for j, chip in enumerate(chips):
                copy(w, 4 + j, (*chip, 1 - c), me).wait_recv()
        for cp in started:
            cp.wait_send()
        for cp in mine:
            cp.wait()

    whole = pl.BlockSpec(memory_space=pltpu.VMEM)
    outs = pl.pallas_call(
        body, name=name, in_specs=[whole] * n, out_specs=[whole] * n,
        out_shape=[jax.ShapeDtypeStruct((N_DEV,) + s.shape, s.dtype) for s in shards],
        scratch_shapes=[pltpu.SemaphoreType.DMA((n, 7)), pltpu.SemaphoreType.DMA((n, 7)), pltpu.SemaphoreType.DMA((n,))],
    )(*shards)
    return list(outs)


def _allgather_seq(shards, name, collective_id, after=()):
    n = len(shards)
    n_after = len(after)

    halves = [s.shape[0] % 32 == 0 for s in shards]
    n_sem = 8
    to_diagonal = not all(halves)

    def body(*refs):
        ins, outs = refs[:n], refs[n + n_after:2 * n + n_after]
        send_sems, recv_sems, local_sems = refs[2 * n + n_after:]
        x, y, c = lax.axis_index("x"), lax.axis_index("y"), lax.axis_index("c")
        me, sibling = (x, y, c), (x, y, 1 - c)
        x_nb, y_nb, diag = (1 - x, y, c), (x, 1 - y, c), (1 - x, 1 - y, c)
        peers = [sibling, x_nb, y_nb] + ([diag] if to_diagonal else [])
        barrier = pltpu.get_barrier_semaphore()
        for peer in peers:
            pl.semaphore_signal(barrier, inc=1, device_id=peer, device_id_type=MESH)
        pl.semaphore_wait(barrier, len(peers))

        def blk(w, dev, rows=None):
            ref = outs[w].at[4 * dev[0] + 2 * dev[1] + dev[2]]
            return ref if rows is None else ref.at[rows]

        def copy(w, k, block, to, src=None, rows=None):
            return pltpu.make_async_remote_copy(
                src_ref=blk(w, block, rows) if src is None else src, dst_ref=blk(w, block, rows),
                send_sem=send_sems.at[n_sem * w + k], recv_sem=recv_sems.at[n_sem * w + k], device_id=to, device_id_type=MESH)

        def top(w):
            return pl.ds(0, shards[w].shape[0] // 2)

        def bottom(w):
            return pl.ds(shards[w].shape[0] // 2, shards[w].shape[0] // 2)

        started = []
        mine = []
        for w in range(n):
            mine.append(pltpu.make_async_copy(ins[w], blk(w, me), local_sems.at[w]))
            mine[-1].start()
            first = [copy(w, 0, me, sibling, src=ins[w]), copy(w, 1, me, x_nb, src=ins[w]), copy(w, 2, me, y_nb, src=ins[w])]
            if not halves[w]:
                first.append(copy(w, 3, me, diag, src=ins[w]))
            for cp in first:
                cp.start()
            started += first
        for w in range(n):
            copy(w, 1, x_nb, me).wait_recv()
            onward = [copy(w, 5, x_nb, sibling)] + ([copy(w, 3, x_nb, y_nb, rows=top(w))] if halves[w] else [])
            copy(w, 2, y_nb, me).wait_recv()
            onward += [copy(w, 6, y_nb, sibling)] + ([copy(w, 4, y_nb, x_nb, rows=bottom(w))] if halves[w] else [])
            for cp in onward:
                cp.start()
            started += onward
        for w in range(n):
            if halves[w]:
                copy(w, 3, diag, me, rows=top(w)).wait_recv()
                copy(w, 4, diag, me, rows=bottom(w)).wait_recv()
            else:
                copy(w, 3, diag, me).wait_recv()
            fwd = copy(w, 7, diag, sibling)
            fwd.start()
            started.append(fwd)
        for w in range(n):
            for k, dev in ((0, sibling), (5, (1 - x, y, 1 - c)), (6, (x, 1 - y, 1 - c)), (7, (1 - x, 1 - y, 1 - c))):
                copy(w, k, dev, me).wait_recv()
        for cp in started:
            cp.wait_send()
        for cp in mine:
            cp.wait()

    outs = pl.kernel(
        body, name=name, out_type=[jax.ShapeDtypeStruct((N_DEV,) + s.shape, s.dtype) for s in shards],
        mesh=plsc.ScalarSubcoreMesh(axis_name="seq", num_cores=1),
        scratch_types=[pltpu.SemaphoreType.DMA((n_sem * n,)), pltpu.SemaphoreType.DMA((n_sem * n,)), pltpu.SemaphoreType.DMA((n,))],
        compiler_params=pltpu.CompilerParams(collective_id=collective_id),
    )(*shards, *after)
    return list(outs)


def _chip_exchange(sums, name, collective_id):
    n = len(sums)

    def body(*refs):
        ins, outs = refs[:n], refs[n:2 * n]
        send_sems, recv_sems = refs[2 * n:]
        x, y, c = lax.axis_index("x"), lax.axis_index("y"), lax.axis_index("c")
        chips = [(1 - x, y), (x, 1 - y), (1 - x, 1 - y)]
        barrier = pltpu.get_barrier_semaphore()
        for px, py in chips:
            pl.semaphore_signal(barrier, inc=1, device_id=(px, py, c), device_id_type=MESH)
        pl.semaphore_wait(barrier, 3)
        copies = []
        for w in range(n):
            for k, (px, py) in enumerate(chips):
                copies.append(pltpu.make_async_remote_copy(
                    src_ref=ins[w].at[2 * px + py], dst_ref=outs[w].at[k], send_sem=send_sems.at[3 * w + k],
                    recv_sem=recv_sems.at[3 * w + k], device_id=(px, py, c), device_id_type=MESH))
        for cp in copies:
            cp.start()
        for cp in copies:
            cp.wait()

    outs = pl.kernel(
        body, name=name, out_type=[jax.ShapeDtypeStruct((3,) + s.shape[1:], s.dtype) for s in sums],
        mesh=plsc.ScalarSubcoreMesh(axis_name="seq", num_cores=1),
        scratch_types=[pltpu.SemaphoreType.DMA((3 * n,)), pltpu.SemaphoreType.DMA((3 * n,))],
        compiler_params=pltpu.CompilerParams(collective_id=collective_id),
    )(*sums)
    return list(outs)


def _row_tile(r, c, elems=256 * 1024):
    want = max(8, elems // c)
    for t in range(min(want, r) // 8 * 8, 0, -8):
        if r % t == 0:
            return t
    return r


def _pair_add(g4, recv, core, name, after=()):
    _, _, r, c = g4.shape
    tr = _row_tile(r, c, 1024 * 1024)

    def body(core_ref, a_ref, b_ref, o_ref):
        o_ref[...] = (a_ref[...].astype(F32) + b_ref[...].astype(F32)).astype(BF)

    return pl.pallas_call(
        _hide(body, 3, len(after)), name=name,
        grid_spec=pltpu.PrefetchScalarGridSpec(
            num_scalar_prefetch=1, grid=(4, r // tr),
            in_specs=[pl.BlockSpec((None, None, tr, c), lambda p, i, s: (p, s[0], i, 0)),
                      pl.BlockSpec((None, tr, c), lambda p, i, s: (p, i, 0))] + _hidden_specs(after),
            out_specs=pl.BlockSpec((None, tr, c), lambda p, i, s: (p, i, 0))),
        out_shape=jax.ShapeDtypeStruct((4, r, c), BF), compiler_params=_params(("parallel", "parallel")),
    )(core, g4, recv, *after)


def _adam_math(w, g, m, v):
    m = ADAM_B1 * m + (1.0 - ADAM_B1) * g
    v = ADAM_B2 * v + (1.0 - ADAM_B2) * (g * g)
    m_hat = m / (1.0 - ADAM_B1 ** ADAM_STEP)
    v_hat = v / (1.0 - ADAM_B2 ** ADAM_STEP)
    delta = -ADAM_LR * (m_hat / (jnp.sqrt(v_hat) + ADAM_EPS) + ADAM_WD * w)
    return delta, m, v


def _adamw_big(sums, recv, chip, w, m, v, name, after=()):
    r, c = w.shape
    tr = _row_tile(r, c, 512 * 1024)

    def body(chip_ref, s_ref, r_ref, w_ref, m_ref, v_ref, g_out, d_out, m_out, v_out):
        g = s_ref[...].astype(F32) + r_ref[0].astype(F32)
        g = g + r_ref[1].astype(F32)
        g = g + r_ref[2].astype(F32)
        delta, mn, vn = _adam_math(w_ref[...], g, m_ref[...], v_ref[...])
        g_out[...] = g
        d_out[...] = delta
        m_out[...] = mn
        v_out[...] = vn

    row = pl.BlockSpec((tr, c), lambda i, s: (i, 0))
    return pl.pallas_call(
        _hide(body, 6, len(after)), name=name,
        grid_spec=pltpu.PrefetchScalarGridSpec(
            num_scalar_prefetch=1, grid=(r // tr,),
            in_specs=[pl.BlockSpec((None, tr, c), lambda i, s: (s[0], i, 0)), pl.BlockSpec((3, tr, c), lambda i, s: (0, i, 0)),
                      row, row, row] + _hidden_specs(after),
            out_specs=[row, row, row, row]),
        out_shape=[jax.ShapeDtypeStruct((r, c), F32)] * 4, compiler_params=_params(("parallel",)),
    )(chip, sums, recv, w, m, v, *after)


def _adamw_small(parts, ws, ms, vs, extra_parts, name):
    n, ne = len(ws), len(extra_parts)

    def total(p_ref):
        g = p_ref[0]
        for d in range(1, N_DEV):
            g = g + p_ref[d]
        return g

    def body(*refs):
        p_refs, w_refs, m_refs, v_refs = refs[:n], refs[n:2 * n], refs[2 * n:3 * n], refs[3 * n:4 * n]
        e_refs = refs[4 * n:4 * n + ne]
        outs = refs[4 * n + ne:]
        for i in range(n):
            g = total(p_refs[i])
            delta, mn, vn = _adam_math(w_refs[i][...], g, m_refs[i][...], v_refs[i][...])
            outs[4 * i][...] = g
            outs[4 * i + 1][...] = delta
            outs[4 * i + 2][...] = mn
            outs[4 * i + 3][...] = vn
        for i in range(ne):
            outs[4 * n + i][...] = total(e_refs[i])

    out_shape = []
    for w in ws:
        out_shape += [jax.ShapeDtypeStruct(w.shape, F32)] * 4
    out_shape += [jax.ShapeDtypeStruct(e.shape[1:], F32) for e in extra_parts]
    res = pl.pallas_call(body, name=name, out_shape=out_shape,
                         compiler_params=pltpu.CompilerParams(vmem_limit_bytes=VMEM_LIMIT))(*parts, *ws, *ms, *vs, *extra_parts)
    return [res[4 * i:4 * i + 4] for i in range(n)], list(res[4 * n:])


def _adamw_plain(g, w, m, v, name):
    def body(g_ref, w_ref, m_ref, v_ref, d_out, m_out, v_out):
        delta, mn, vn = _adam_math(w_ref[...], g_ref[...], m_ref[...], v_ref[...])
        d_out[...] = delta
        m_out[...] = mn
        v_out[...] = vn

    return pl.pallas_call(body, name=name, out_shape=[jax.ShapeDtypeStruct(w.shape, F32)] * 3)(g, w, m, v)


def kernel(x, mem, positions, g_mix, w_in, g_a_v, w_spatial, b_spatial, g_b_q, g_b_k, sinks, g_mem, w_mem_kv, g_c_q, g_c_k, w_branch_a, w_branch_b, w_branch_c, w_out, g_ffn, w_up, conv_w, conv_b, w_down, loss_target, m_g_mix, m_w_in, m_g_a_v, m_w_spatial, m_b_spatial, m_g_b_q, m_g_b_k, m_sinks, m_g_mem, m_w_mem_kv, m_g_c_q, m_g_c_k, m_w_branch_a, m_w_branch_b, m_w_branch_c, m_w_out, m_g_ffn, m_w_up, m_conv_w, m_conv_b, m_w_down, v_g_mix, v_w_in, v_g_a_v, v_w_spatial, v_b_spatial, v_g_b_q, v_g_b_k, v_sinks, v_g_mem, v_w_mem_kv, v_g_c_q, v_g_c_k, v_w_branch_a, v_w_branch_b, v_w_branch_c, v_w_out, v_g_ffn, v_w_up, v_conv_w, v_conv_b, v_w_down):
    S, D = x.shape[1], x.shape[2]
    M = mem.shape[1]
    F = w_down.shape[1] * N_DEV
    in_cols = w_in.shape[2] * N_DEV
    ax, ay, ac = lax.axis_index("x"), lax.axis_index("y"), lax.axis_index("c")
    core = jnp.reshape(ac, (1,)).astype(jnp.int32)
    chip = jnp.reshape(2 * ax + ay, (1,)).astype(jnp.int32)
    me = 4 * ax + 2 * ay + ac

    x2, mem2, tgt2 = x[0], mem[0], loss_target[0]

    big = dict(w_in=w_in[0].T, w_mem_kv=w_mem_kv[0], w_branch_a=w_branch_a[0], w_branch_b=w_branch_b[0],
               w_branch_c=w_branch_c[0], w_out=w_out[0], w_up=w_up[0], w_down=w_down[0])
    names = list(big)
    cast = {k: big[k].astype(BF) for k in names}
    W = {}
    cb3 = conv_b.reshape(2, 1, F)
    W["w_in"], = _allgather_seq([cast["w_in"]], "ag_seq0", 0)
    w_in_t = W["w_in"].reshape(in_cols, D)
    grp1 = ["w_mem_kv", "w_branch_a", "w_branch_b", "w_branch_c", "w_out"]
    res1 = _allgather_seq([cast[k] for k in grp1] + [conv_w[0]], "ag_seq1", 1, after=(_token((w_in_t,), "tok_w_in"),))
    W.update(zip(grp1, res1))
    cw3 = res1[-1]
    w_kv_f = W["w_mem_kv"].reshape(D, 2 * C_WIDTH)
    w_out_f = W["w_out"].reshape(D, D)

    half = ROPE_DIM // 2
    inv = ROPE_THETA ** (-jnp.arange(half, dtype=F32) / half)
    ang = positions[0].astype(F32)[:, None] * inv
    cos, sin = jnp.cos(ang), jnp.sin(ang)
    one, zero = jnp.ones((S, B_HEAD_DIM - ROPE_DIM), F32), jnp.zeros((S, B_HEAD_DIM - ROPE_DIM), F32)
    z8 = jnp.zeros((S, half), F32)
    ct = jnp.tile(jnp.concatenate([cos, cos, one], axis=1), (1, 2))
    sa = jnp.tile(jnp.concatenate([-sin, z8, zero], axis=1), (1, 2))
    sb = jnp.tile(jnp.concatenate([z8, sin, zero], axis=1), (1, 2))
    gq2, gk2 = jnp.tile(g_b_q, (1, 2)), jnp.tile(g_b_k, (1, 2))
    b_t = b_spatial[0].T

    h, rstd1 = _rms_fwd(x2, g_mix, "rms1_fwd")
    proj = _mm(h, w_in_t, "nt", F32, "mm_proj", tn=1280)
    y_a = _a_fwd(proj, g_a_v, w_spatial[0], b_t)
    W["w_up"], = _allgather_seq([cast["w_up"]], "ag_seq2", 2, after=(_token((W["w_out"], proj), "tok_group1"),))
    qn, kn = _b_pre(proj, gq2, gk2, ct, sa, sb)
    y_b = _b_attn_fwd(qn, kn, proj, sinks)
    mem_h, rstd_m = _rms_fwd(mem2, g_mem, "rmsmem_fwd")
    kv = _mm(mem_h, w_kv_f, "nn", F32, "mm_kv", after=(y_b,))
    y_c = _c_fwd(proj, kv, g_c_q, g_c_k)
    w_branches = [W["w_branch_a"], W["w_branch_b"], W["w_branch_c"]]
    merged, z_a, z_b, z_c = _merge_fwd(proj, [y_a, y_b, y_c], w_branches)
    x1, h2, rstd2 = _residual_rms(merged, w_out_f, x2, g_ffn, "mm_x1_rms2")
    W["w_down"], = _allgather_seq([cast["w_down"]], "ag_seq3", 3, after=(W["w_up"], h2))
    w_down_f = W["w_down"].reshape(F, D)
    up3 = _mm(h2, W["w_up"], "nn", BF, "mm_up", b_stack=True, out_parts=2)
    act = _ffn_act_fwd(up3, cw3, cb3)
    dy, dy_b, loss_acc = _out_loss(act, w_down_f, x1, tgt2)

    reduced = {}

    def as4(g):
        return g.reshape(4, 2, g.shape[1], g.shape[2])

    def finish_group(gi, keys, g4, from_sibling):
        sums = [_pair_add(a, b, core, "rs_add_" + k) for k, a, b in zip(keys, g4, from_sibling)]
        from_chips = _chip_exchange(sums, f"rs_chip{gi}", 4 + gi)
        reduced.update(zip(keys, zip(sums, from_chips)))
        return tuple(sums)

    d_act = _mm(dy_b, w_down_f, "nt", BF, "mm_dact", tn=1408)
    g_down = _mm(act, dy_b, "tn", BF, "mm_gdown", tm=1408)
    d_up3, d_cw3, d_cb3 = _ffn_act_bwd(up3, cw3, cb3, d_act, after=(g_down,))
    grp0 = [as4(g_down.reshape(N_DEV, F // N_DEV, D))]
    g_up, sib0 = _mm(h2, d_up3, "tn", BF, "mm_gup", b_parts=2, out_stack=True, exchange=grp0)
    sums0 = finish_group(0, ["w_down"], grp0, sib0)
    grp1 = [as4(g_up)]
    d_h2, sib1 = _mm(d_up3, W["w_up"], "nt", F32, "mm_dh2", a_parts=2, b_stack=True, tm=2048, after=sums0, exchange=grp1)
    sums1 = finish_group(1, ["w_up"], grp1, sib1)
    dx1, dx1_b, d_g_ffn = _rms_bwd(x1, rstd2, g_ffn, d_h2, dy, "rms2_bwd", after=sums1)
    g_out = _mm(merged, dx1_b, "tn", BF, "mm_gout")
    grp2 = [as4(g_out.reshape(N_DEV, D // N_DEV, D))]
    d_merged, sib2 = _mm(dx1_b, w_out_f, "nt", F32, "mm_dmerged", exchange=grp2)
    sums2 = finish_group(2, ["w_out"], grp2, sib2)
    dz_a, dz_b, dz_c, dga, dgb, dgc, dy_a, dy_b_, dy_c = _merge_bwd(proj, [z_a, z_b, z_c], d_merged, w_branches, after=sums2)
    g_ba = _mm(y_a, dz_a, "tn", BF, "mm_gba", out_stack=True)
    g_bb = _mm(y_b, dz_b, "tn", BF, "mm_gbb", out_stack=True)
    g_bc = _mm(y_c, dz_c, "tn", BF, "mm_gbc", out_stack=True)
    d_uv, d_g_a_v, d_w_s, d_b_t = _a_bwd(proj, g_a_v, w_spatial[0], b_t, dy_a, after=(g_ba, g_bb, g_bc))
    dqn, dkn, dv_b, dsink_rows = _b_attn_bwd(qn, kn, proj, sinks, dy_b_)
    d_qkv, d_gq2, d_gk2 = _b_pre_bwd(proj, gq2, gk2, ct, sa, sb, dqn, dkn, dv_b)
    dq_c, dk_c, dv_c, d_gcq, d_gck = _c_bwd(proj, kv, g_c_q, g_c_k, dy_c)
    dkv_b = jnp.concatenate([dk_c, dv_c], axis=1).astype(BF)
    d_memh = _mm(dkv_b, w_kv_f, "nt", F32, "mm_dmemh")
    g_kv = _mm(mem_h, dkv_b, "tn", BF, "mm_gkv")
    _, _, d_g_mem = _rms_bwd(mem2, rstd_m, g_mem, d_memh, None, "rmsmem_bwd")
    dproj = jnp.concatenate([d_uv, d_qkv, dq_c, dga, dgb, dgc], axis=1)
    grp3 = [as4(g_ba), as4(g_bb), as4(g_bc)]
    g_in, sib3 = _mm(dproj, h, "tn", BF, "mm_gin", tm=1280, exchange=grp3)
    sums3 = finish_group(3, ["w_branch_a", "w_branch_b", "w_branch_c"], grp3, sib3)
    grp4 = [as4(g_in.reshape(N_DEV, in_cols // N_DEV, D)), as4(g_kv.reshape(N_DEV, D // N_DEV, 2 * C_WIDTH))]
    d_h, sib4 = _mm(dproj, w_in_t, "nn", F32, "mm_dh", tm=2048, tk=1280, after=sums3, exchange=grp4)
    sums4 = finish_group(4, ["w_in", "w_mem_kv"], grp4, sib4)
    grad_x, _, d_g_mix = _rms_bwd(x2, rstd1, g_mix, d_h, dx1, "rms1_bwd", after=sums4)

    small_names =["g_mix", "g_a_v", "w_spatial", "b_spatial", "g_b_q", "g_b_k", "sinks", "g_mem", "g_c_q", "g_c_k", "g_ffn", "conv_b"]
    small_w = dict(g_mix=g_mix, g_a_v=g_a_v, w_spatial=w_spatial, b_spatial=b_spatial, g_b_q=g_b_q, g_b_k=g_b_k, sinks=sinks,
                   g_mem=g_mem, g_c_q=g_c_q, g_c_k=g_c_k, g_ffn=g_ffn, conv_b=conv_b)
    small_m = dict(g_mix=m_g_mix, g_a_v=m_g_a_v, w_spatial=m_w_spatial, b_spatial=m_b_spatial, g_b_q=m_g_b_q, g_b_k=m_g_b_k,
                   sinks=m_sinks, g_mem=m_g_mem, g_c_q=m_g_c_q, g_c_k=m_g_c_k, g_ffn=m_g_ffn, conv_b=m_conv_b)
    small_v = dict(g_mix=v_g_mix, g_a_v=v_g_a_v, w_spatial=v_w_spatial, b_spatial=v_b_spatial, g_b_q=v_g_b_q, g_b_k=v_g_b_k,
                   sinks=v_sinks, g_mem=v_g_mem, g_c_q=v_g_c_q, g_c_k=v_g_c_k, g_ffn=v_g_ffn, conv_b=v_conv_b)
    small_g = dict(
        g_mix=d_g_mix, g_a_v=d_g_a_v, w_spatial=d_w_s, b_spatial=d_b_t.T,
        g_b_q=d_gq2.reshape(2, B_HEAD_DIM).sum(0), g_b_k=d_gk2.reshape(2, B_HEAD_DIM).sum(0),
        sinks=dsink_rows.sum(0)[:B_HEADS], g_mem=d_g_mem, g_c_q=d_gcq.sum(0), g_c_k=d_gck.sum(0), g_ffn=d_g_ffn,
        conv_b=d_cb3)
    partial = [small_g[k].reshape(small_w[k].shape) for k in small_names] + [d_cw3, loss_acc[0:1]]
    parts = _allgather(partial, "ag_small")
    n_small = len(small_names)
    small_res, (g_cw3, loss_row) = _adamw_small(parts[:n_small], [small_w[k] for k in small_names], [small_m[k] for k in small_names],
                                                [small_v[k] for k in small_names], parts[n_small:], "adamw_small")
    loss = loss_row[0, 0]
    small_out = dict(zip(small_names, small_res))
    c_cw = 2 * F // N_DEV
    g_cw = lax.dynamic_slice(g_cw3, (me // (N_DEV // 2), 0, (me % (N_DEV // 2)) * c_cw), (1, 3, c_cw))[0]
    cw_res = _adamw_plain(g_cw, conv_w[0], m_conv_w[0], v_conv_w[0], "adamw_conv_w")
    big_out = {"conv_w": [g_cw[None]] + [a[None] for a in cw_res]}

    moments = dict(w_in=(m_w_in, v_w_in), w_mem_kv=(m_w_mem_kv, v_w_mem_kv), w_branch_a=(m_w_branch_a, v_w_branch_a),
                   w_branch_b=(m_w_branch_b, v_w_branch_b), w_branch_c=(m_w_branch_c, v_w_branch_c), w_out=(m_w_out, v_w_out),
                   w_up=(m_w_up, v_w_up), w_down=(m_w_down, v_w_down))
    token = (grad_x, small_res[0][0])
    for k in ["w_down", "w_up", "w_out", "w_branch_a", "w_branch_b", "w_branch_c", "w_mem_kv", "w_in"]:
        s, r = reduced[k]
        mk, vk = moments[k][0][0], moments[k][1][0]
        if k == "w_in":
            res = _adamw_big(s, r, chip, big[k], mk.T, vk.T, "adamw_" + k, after=token)
            big_out[k] = [a.T[None] for a in res]
        else:
            res = _adamw_big(s, r, chip, big[k], mk, vk, "adamw_" + k, after=token)
            big_out[k] = [a[None] for a in res]
        token = (res[0],)

    order = ["g_mix", "w_in", "g_a_v", "w_spatial", "b_spatial", "g_b_q", "g_b_k", "sinks", "g_mem", "w_mem_kv", "g_c_q", "g_c_k",
             "w_branch_a", "w_branch_b", "w_branch_c", "w_out", "g_ffn", "w_up", "conv_w", "conv_b", "w_down"]
    res = {**small_out, **big_out}
    outs = [loss, grad_x[None]]
    for field in range(4):
        outs += [res[k][field] for k in order]
    return tuple(outs)
```

```python
import functools

import jax
import jax.numpy as jnp
from jax import lax
from jax.experimental import pallas as pl
from jax.experimental.pallas import tpu as pltpu
from jax.experimental.pallas import tpu_sc as plsc

F32 = jnp.float32
BF = jnp.bfloat16
EPS = 1e-6
NEG = -1e30

N_DEV = 8
CHUNK = 128
A_GROUPS = 4
A_WIDTH = 512
B_HEADS = 16
B_KV_HEADS = 2
B_HEAD_DIM = 64
B_WIDTH = 1024
B_KV_WIDTH = 128
ROPE_DIM = 16
ROPE_THETA = 500000.0
C_HEADS = 4
C_HEAD_DIM = 128
C_WIDTH = 512
GATE_OFF = 2 * A_WIDTH + B_WIDTH + 2 * B_KV_WIDTH + C_WIDTH

ADAM_LR = 0.001
ADAM_B1 = 0.9
ADAM_B2 = 0.999
ADAM_EPS = 1e-08
ADAM_WD = 0.01
ADAM_STEP = 10

VMEM_LIMIT = 48 * 1024 * 1024
MESH = pl.DeviceIdType.MESH


def _pick(n, prefs):
    for p in prefs:
        if p <= n and n % p == 0:
            return p
    return n


def _params(sem):
    return pltpu.CompilerParams(dimension_semantics=sem, vmem_limit_bytes=VMEM_LIMIT)


def _hide(body, n_seen, n_hidden):
    if not n_hidden:
        return body

    def wrapped(*refs):
        return body(*refs[:n_seen], *refs[n_seen + n_hidden:])

    return wrapped


def _hidden_specs(after):
    return [pl.BlockSpec(memory_space=pl.ANY) for _ in after]


def _token(xs, name):
    def body(*refs):
        refs[-1][...] = jnp.zeros_like(refs[-1])

    return pl.pallas_call(body, name=name, in_specs=_hidden_specs(xs), out_shape=jax.ShapeDtypeStruct((8, 128), F32))(*xs)


def _mm(a, b, mode, out_dtype, name, *, resid=None, b_stack=False, a_parts=0, b_parts=0, out_parts=0,
        out_stack=False, tm=1024, tn=1024, tk=2048, after=(), exchange=()):
    if mode == "nn":
        M = a.shape[-2]
        K = a.shape[-1] * max(a_parts, 1)
        N = b.shape[-1] * (N_DEV if b_stack else 1)
        dims = (((1,), (0,)), ((), ()))
    elif mode == "nt":
        M = a.shape[-2]
        K = a.shape[-1] * max(a_parts, 1)
        N = b.shape[-2]
        dims = (((1,), (1,)), ((), ()))
    else:
        K = a.shape[-2]
        M = a.shape[-1]
        N = b.shape[-1] * max(b_parts, 1)
        dims = (((0,), (0,)), ((), ()))
    if b_stack and mode == "nn":
        tn = b.shape[-1]
    if b_stack and mode == "nt":
        tk = b.shape[-1]
    if out_stack:
        tn = N // N_DEV
    tm, tn, tk = _pick(M, (tm,)), _pick(N, (tn,)), _pick(K, (tk,))
    if M % tm or N % tn or K % tk:
        raise ValueError(f"{name}: tiles {tm},{tn},{tk} do not divide {M},{N},{K}")
    nm, nn, nk = M // tm, N // tn, K // tk

    def parts_idx(t, ntile, parts):
        per = ntile // parts
        return t // per, t % per

    if mode in ("nn", "nt"):
        if a_parts:
            a_spec = pl.BlockSpec((None, tm, tk), lambda m, n, k: (parts_idx(k, nk, a_parts)[0], m, parts_idx(k, nk, a_parts)[1]))
        else:
            a_spec = pl.BlockSpec((tm, tk), lambda m, n, k: (m, k))
    else:
        a_spec = pl.BlockSpec((tk, tm), lambda m, n, k: (k, m))
    if mode == "nn":
        if b_stack:
            b_spec = pl.BlockSpec((None, tk, tn), lambda m, n, k: (n, k, 0))
        else:
            b_spec = pl.BlockSpec((tk, tn), lambda m, n, k: (k, n))
    elif mode == "nt":
        if b_stack:
            b_spec = pl.BlockSpec((None, tn, tk), lambda m, n, k: (k, n, 0))
        else:
            b_spec = pl.BlockSpec((tn, tk), lambda m, n, k: (n, k))
    else:
        if b_parts:
            b_spec = pl.BlockSpec((None, tk, tn), lambda m, n, k: (parts_idx(n, nn, b_parts)[0], k, parts_idx(n, nn, b_parts)[1]))
        else:
            b_spec = pl.BlockSpec((tk, tn), lambda m, n, k: (k, n))
    if out_stack:
        out_shape = jax.ShapeDtypeStruct((N_DEV, M, tn), out_dtype)
        o_spec = pl.BlockSpec((None, tm, tn), lambda m, n, k: (n, m, 0))
    elif out_parts:
        out_shape = jax.ShapeDtypeStruct((out_parts, M, N // out_parts), out_dtype)
        o_spec = pl.BlockSpec((None, tm, tn), lambda m, n, k: (parts_idx(n, nn, out_parts)[0], m, parts_idx(n, nn, out_parts)[1]))
    else:
        out_shape = jax.ShapeDtypeStruct((M, N), out_dtype)
        o_spec = pl.BlockSpec((tm, tn), lambda m, n, k: (m, n))
    has_resid = resid is not None

    n_ex = len(exchange)
    n_in = 2 + has_resid + len(after)

    def body(*refs):
        a_ref, b_ref = refs[:2]
        r_ref = refs[2] if has_resid else None
        ex_in = refs[n_in:n_in + n_ex]
        o_ref = refs[n_in + n_ex]
        ex_out = refs[n_in + n_ex + 1:n_in + 2 * n_ex + 1]
        scratch = refs[n_in + 2 * n_ex + 1:]
        m_i, n_i, k = pl.program_id(0), pl.program_id(1), pl.program_id(2)

        def pushes():
            send_sems, recv_sems = scratch[-2:]
            x, y, c = lax.axis_index("x"), lax.axis_index("y"), lax.axis_index("c")
            return [pltpu.make_async_remote_copy(
                src_ref=ex_in[w].at[:, 1 - c], dst_ref=ex_out[w], send_sem=send_sems.at[w], recv_sem=recv_sems.at[w],
                device_id=(x, y, 1 - c), device_id_type=MESH) for w in range(n_ex)]

        if n_ex:
            @pl.when((m_i == 0) & (n_i == 0) & (k == 0))
            def _():
                for cp in pushes():
                    cp.start()

        if nk == 1:
            res = lax.dot_general(a_ref[...], b_ref[...], dims, preferred_element_type=F32)
            if has_resid:
                res = res + r_ref[...]
            o_ref[...] = res.astype(o_ref.dtype)
        else:
            acc = scratch[0]

            @pl.when(k == 0)
            def _():
                acc[...] = jnp.zeros_like(acc)

            acc[...] += lax.dot_general(a_ref[...], b_ref[...], dims, preferred_element_type=F32)

            @pl.when(k == nk - 1)
            def _():
                res = acc[...]
                if has_resid:
                    res = res + r_ref[...]
                o_ref[...] = res.astype(o_ref.dtype)

        if n_ex:
            @pl.when((m_i == nm - 1) & (n_i == nn - 1) & (k == nk - 1))
            def _():
                for cp in pushes():
                    cp.wait()

    in_specs = [a_spec, b_spec]
    args = [a, b]
    if has_resid:
        in_specs.append(pl.BlockSpec((tm, tn), lambda m, n, k: (m, n)))
        args.append(resid)
    in_specs += _hidden_specs(after) + _hidden_specs(exchange)
    args += list(after) + list(exchange)
    scratch_shapes = [pltpu.VMEM((tm, tn), F32)] if nk > 1 else []
    if not n_ex:
        return pl.pallas_call(
            body, name=name, grid=(nm, nn, nk), in_specs=in_specs, out_specs=o_spec, out_shape=out_shape,
            scratch_shapes=scratch_shapes, compiler_params=_params(("parallel", "parallel", "arbitrary")),
        )(*args)
    res = pl.pallas_call(
        body, name=name, grid=(nm, nn, nk), in_specs=in_specs, out_specs=[o_spec] + _hidden_specs(exchange),
        out_shape=[out_shape] + [jax.ShapeDtypeStruct((g.shape[0],) + g.shape[2:], g.dtype) for g in exchange],
        scratch_shapes=scratch_shapes + [pltpu.SemaphoreType.DMA((n_ex,)), pltpu.SemaphoreType.DMA((n_ex,))],
        compiler_params=_params(("arbitrary", "arbitrary", "arbitrary")),
    )(*args)
    return res[0], list(res[1:])


def _rms_fwd(x, g, name):
    R, D = x.shape
    tr = _pick(R, (256,))

    def body(x_ref, g_ref, h_ref, r_ref):
        xv = x_ref[...]
        r = lax.rsqrt(jnp.mean(xv * xv, axis=-1, keepdims=True) + EPS)
        h_ref[...] = (xv * r * g_ref[...]).astype(BF)
        r_ref[...] = r

    return pl.pallas_call(
        body, name=name, grid=(R // tr,),
        in_specs=[pl.BlockSpec((tr, D), lambda i: (i, 0)), pl.BlockSpec((1, D), lambda i: (0, 0))],
        out_specs=[pl.BlockSpec((tr, D), lambda i: (i, 0)), pl.BlockSpec((tr, 1), lambda i: (i, 0))],
        out_shape=[jax.ShapeDtypeStruct((R, D), BF), jax.ShapeDtypeStruct((R, 1), F32)],
        compiler_params=_params(("parallel",)),
    )(x, g)


def _rms_bwd(x, r, g, dh, dres, name, after=()):
    R, D = x.shape
    tr = _pick(R, (256,))
    has_res = dres is not None

    def body(*refs):
        if has_res:
            x_ref, r_ref, g_ref, dh_ref, dres_ref, dx_ref, dxb_ref, dg_ref = refs
        else:
            x_ref, r_ref, g_ref, dh_ref, dx_ref, dxb_ref, dg_ref = refs
        i = pl.program_id(0)
        xv, rv, dhv = x_ref[...], r_ref[...], dh_ref[...]
        gy = dhv * g_ref[...]
        c = jnp.sum(xv * gy, axis=-1, keepdims=True)
        dx = rv * gy - xv * (rv * rv * rv) * (c * (1.0 / D))
        if has_res:
            dx = dx + dres_ref[...]
        dx_ref[...] = dx
        dxb_ref[...] = dx.astype(BF)
        part = jnp.sum(dhv * xv * rv, axis=0, keepdims=True)

        @pl.when(i == 0)
        def _():
            dg_ref[...] = part

        @pl.when(i > 0)
        def _():
            dg_ref[...] += part

    row = pl.BlockSpec((tr, D), lambda i: (i, 0))
    in_specs = [row, pl.BlockSpec((tr, 1), lambda i: (i, 0)), pl.BlockSpec((1, D), lambda i: (0, 0)), row]
    args = [x, r, g, dh]
    if has_res:
        in_specs.append(row)
        args.append(dres)
    return pl.pallas_call(
        _hide(body, len(args), len(after)), name=name, grid=(R // tr,), in_specs=in_specs + _hidden_specs(after),
        out_specs=[row, row, pl.BlockSpec((1, D), lambda i: (0, 0))],
        out_shape=[jax.ShapeDtypeStruct((R, D), F32), jax.ShapeDtypeStruct((R, D), BF), jax.ShapeDtypeStruct((1, D), F32)],
        compiler_params=_params(("arbitrary",)),
    )(*args, *after)


def _a_chunk(us, vs, gvs, ws, bs):
    r_i = lax.broadcasted_iota(jnp.int32, (CHUNK, CHUNK), 0)
    c_i = lax.broadcasted_iota(jnp.int32, (CHUNK, CHUNK), 1)
    causal = r_i >= c_i
    vg = [jax.nn.gelu(v) for v in vs]
    ss = sum(jnp.sum(v * v, axis=-1, keepdims=True) for v in vg)
    r = lax.rsqrt(ss * (1.0 / A_WIDTH) + EPS)
    ys = []
    for g in range(A_GROUPS):
        vn = vg[g] * r * gvs[g]
        w = jnp.where(causal, ws[g], 0.0)
        s = jnp.dot(w.astype(BF), vn.astype(BF), preferred_element_type=F32) + bs[g]
        ys.append(jax.nn.gelu(us[g]) * s)
    return ys


def _a_split(u_ref, v_ref, g_ref, w_ref, b_ref):
    sl = [slice(g * 128, (g + 1) * 128) for g in range(A_GROUPS)]
    return ([u_ref[:, s] for s in sl], [v_ref[:, s] for s in sl], [g_ref[:, s] for s in sl],
            [w_ref[g] for g in range(A_GROUPS)], [b_ref[:, g:g + 1] for g in range(A_GROUPS)])


def _a_specs(S):
    return [pl.BlockSpec((CHUNK, A_WIDTH), lambda n: (n, 0)), pl.BlockSpec((CHUNK, A_WIDTH), lambda n: (n, 1)),
            pl.BlockSpec((1, A_WIDTH), lambda n: (0, 0)), pl.BlockSpec((A_GROUPS, CHUNK, CHUNK), lambda n: (0, 0, 0)),
            pl.BlockSpec((CHUNK, A_GROUPS), lambda n: (0, 0))]


def _a_fwd(proj, g_v, w_s, b_t):
    S = proj.shape[0]

    def body(u_ref, v_ref, g_ref, w_ref, b_ref, y_ref):
        ys = _a_chunk(*_a_split(u_ref, v_ref, g_ref, w_ref, b_ref))
        for g in range(A_GROUPS):
            y_ref[:, g * 128:(g + 1) * 128] = ys[g].astype(BF)

    return pl.pallas_call(
        body, name="a_fwd", grid=(S // CHUNK,), in_specs=_a_specs(S),
        out_specs=pl.BlockSpec((CHUNK, A_WIDTH), lambda n: (n, 0)),
        out_shape=jax.ShapeDtypeStruct((S, A_WIDTH), BF), compiler_params=_params(("parallel",)),
    )(proj, proj, g_v, w_s, b_t)


def _a_bwd(proj, g_v, w_s, b_t, dy, after=()):
    S = proj.shape[0]

    def body(u_ref, v_ref, g_ref, w_ref, b_ref, dy_ref, duv_ref, dg_ref, dw_ref, db_ref):
        n = pl.program_id(0)
        dys = [dy_ref[:, g * 128:(g + 1) * 128] for g in range(A_GROUPS)]
        _, vjp = jax.vjp(_a_chunk, *_a_split(u_ref, v_ref, g_ref, w_ref, b_ref))
        dus, dvs, dgs, dws, dbs = vjp(dys)

        @pl.when(n == 0)
        def _():
            dg_ref[...] = jnp.zeros_like(dg_ref)
            dw_ref[...] = jnp.zeros_like(dw_ref)
            db_ref[...] = jnp.zeros_like(db_ref)

        for g in range(A_GROUPS):
            duv_ref[:, g * 128:(g + 1) * 128] = dus[g].astype(BF)
            duv_ref[:, A_WIDTH + g * 128:A_WIDTH + (g + 1) * 128] = dvs[g].astype(BF)
            dg_ref[:, g * 128:(g + 1) * 128] += dgs[g]
            dw_ref[g] += dws[g]
            db_ref[:, g:g + 1] += dbs[g]

    return pl.pallas_call(
        _hide(body, 6, len(after)), name="a_bwd", grid=(S // CHUNK,),
        in_specs=_a_specs(S) + [pl.BlockSpec((CHUNK, A_WIDTH), lambda n: (n, 0))] + _hidden_specs(after),
        out_specs=[pl.BlockSpec((CHUNK, 2 * A_WIDTH), lambda n: (n, 0)), pl.BlockSpec((1, A_WIDTH), lambda n: (0, 0)),
                   pl.BlockSpec((A_GROUPS, CHUNK, CHUNK), lambda n: (0, 0, 0)), pl.BlockSpec((CHUNK, A_GROUPS), lambda n: (0, 0))],
        out_shape=[jax.ShapeDtypeStruct((S, 2 * A_WIDTH), BF), jax.ShapeDtypeStruct((1, A_WIDTH), F32),
                   jax.ShapeDtypeStruct((A_GROUPS, CHUNK, CHUNK), F32), jax.ShapeDtypeStruct((CHUNK, A_GROUPS), F32)],
        compiler_params=_params(("arbitrary",)),
    )(proj, proj, g_v, w_s, b_t, dy, *after)


def _half_mask(shape, which):
    lane = lax.broadcasted_iota(jnp.int32, shape, len(shape) - 1)
    return (lane >= 64) == (which == 1)


def _pair_norm_rope(x, g, ct, sa, sb):
    lo = _half_mask(x.shape, 0)
    x2 = x * x
    ss_lo = jnp.sum(jnp.where(lo, x2, 0.0), axis=-1, keepdims=True)
    ss_hi = jnp.sum(jnp.where(lo, 0.0, x2), axis=-1, keepdims=True)
    r = jnp.where(lo, lax.rsqrt(ss_lo * (1.0 / B_HEAD_DIM) + EPS), lax.rsqrt(ss_hi * (1.0 / B_HEAD_DIM) + EPS))
    xr = x * r
    xn = xr * g
    out = xn * ct + pltpu.roll(xn, 120, 1) * sa + pltpu.roll(xn, 8, 1) * sb
    return out, xr, r


def _pair_norm_rope_bwd(x, g, ct, sa, sb, dout):
    lo = _half_mask(x.shape, 0)
    _, xr, r = _pair_norm_rope(x, g, ct, sa, sb)
    dxn = dout * ct + pltpu.roll(dout * sa, 8, 1) + pltpu.roll(dout * sb, 120, 1)
    gy = dxn * g
    t = xr * gy
    c_lo = jnp.sum(jnp.where(lo, t, 0.0), axis=-1, keepdims=True)
    c_hi = jnp.sum(jnp.where(lo, 0.0, t), axis=-1, keepdims=True)
    c = jnp.where(lo, c_lo, c_hi)
    dx = r * (gy - xr * c * (1.0 / B_HEAD_DIM))
    dg = jnp.sum(dxn * xr, axis=0, keepdims=True)
    return dx, dg


def _b_pre(proj, gq2, gk2, ct, sa, sb):
    S = proj.shape[0]
    tr = _pick(S, (256,))
    n_pair = B_WIDTH // 128

    def body(q_ref, k_ref, gq_ref, gk_ref, ct_ref, sa_ref, sb_ref, qn_ref, kn_ref):
        ct_v, sa_v, sb_v = ct_ref[...], sa_ref[...], sb_ref[...]
        for p in range(n_pair):
            o, _, _ = _pair_norm_rope(q_ref[:, p * 128:(p + 1) * 128], gq_ref[...], ct_v, sa_v, sb_v)
            qn_ref[:, p * 128:(p + 1) * 128] = o.astype(BF)
        o, _, _ = _pair_norm_rope(k_ref[...], gk_ref[...], ct_v, sa_v, sb_v)
        kn_ref[...] = o.astype(BF)

    tab = pl.BlockSpec((tr, 128), lambda i: (i, 0))
    gsp = pl.BlockSpec((1, 128), lambda i: (0, 0))
    return pl.pallas_call(
        body, name="b_pre", grid=(S // tr,),
        in_specs=[pl.BlockSpec((tr, B_WIDTH), lambda i: (i, 1)), pl.BlockSpec((tr, 128), lambda i: (i, 2 * B_WIDTH // 128)),
                  gsp, gsp, tab, tab, tab],
        out_specs=[pl.BlockSpec((tr, B_WIDTH), lambda i: (i, 0)), tab],
        out_shape=[jax.ShapeDtypeStruct((S, B_WIDTH), BF), jax.ShapeDtypeStruct((S, 128), BF)],
        compiler_params=_params(("parallel",)),
    )(proj, proj, gq2, gk2, ct, sa, sb)


def _b_pre_bwd(proj, gq2, gk2, ct, sa, sb, dqn, dkn, dv):
    S = proj.shape[0]
    tr = _pick(S, (256,))
    n_pair = B_WIDTH // 128

    def body(q_ref, k_ref, gq_ref, gk_ref, ct_ref, sa_ref, sb_ref, dqn_ref, dkn_ref, dv_ref, dqkv_ref, dgq_ref, dgk_ref):
        i = pl.program_id(0)
        ct_v, sa_v, sb_v = ct_ref[...], sa_ref[...], sb_ref[...]
        dgq = jnp.zeros((1, 128), F32)
        for p in range(n_pair):
            sl = slice(p * 128, (p + 1) * 128)
            dx, dg = _pair_norm_rope_bwd(q_ref[:, sl], gq_ref[...], ct_v, sa_v, sb_v, dqn_ref[:, sl])
            dqkv_ref[:, sl] = dx.astype(BF)
            dgq = dgq + dg
        dx, dgk = _pair_norm_rope_bwd(k_ref[...], gk_ref[...], ct_v, sa_v, sb_v, dkn_ref[...])
        dqkv_ref[:, B_WIDTH:B_WIDTH + 128] = dx.astype(BF)
        dqkv_ref[:, B_WIDTH + 128:B_WIDTH + 256] = dv_ref[...].astype(BF)

        @pl.when(i == 0)
        def _():
            dgq_ref[...] = dgq
            dgk_ref[...] = dgk

        @pl.when(i > 0)
        def _():
            dgq_ref[...] += dgq
            dgk_ref[...] += dgk

    tab = pl.BlockSpec((tr, 128), lambda i: (i, 0))
    gsp = pl.BlockSpec((1, 128), lambda i: (0, 0))
    return pl.pallas_call(
        body, name="b_pre_bwd", grid=(S // tr,),
        in_specs=[pl.BlockSpec((tr, B_WIDTH), lambda i: (i, 1)), pl.BlockSpec((tr, 128), lambda i: (i, 2 * B_WIDTH // 128)),
                  gsp, gsp, tab, tab, tab, pl.BlockSpec((tr, B_WIDTH), lambda i: (i, 0)), tab, tab],
        out_specs=[pl.BlockSpec((tr, B_WIDTH + 256), lambda i: (i, 0)), gsp, gsp],
        out_shape=[jax.ShapeDtypeStruct((S, B_WIDTH + 256), BF), jax.ShapeDtypeStruct((1, 128), F32), jax.ShapeDtypeStruct((1, 128), F32)],
        compiler_params=_params(("arbitrary",)),
    )(proj, proj, gq2, gk2, ct, sa, sb, dqn, dkn, dv)


def _b_dup(x2, g):
    d = jnp.where(_half_mask(x2.shape, g), x2, 0.0)
    return (d + pltpu.roll(d, 64, 1)).astype(BF)


PAIRS_PER_GROUP = B_HEADS // B_KV_HEADS // 2
GROUP_ROWS = PAIRS_PER_GROUP * CHUNK


def _b_valid(n):
    row = lax.broadcasted_iota(jnp.int32, (GROUP_ROWS, 2 * CHUNK), 0) & (CHUNK - 1)
    col = lax.broadcasted_iota(jnp.int32, (GROUP_ROWS, 2 * CHUNK), 1)
    rel = row + CHUNK - col
    return (rel >= 0) & (rel < CHUNK) & ((col >= CHUNK) | (n > 0))


def _b_blocks(x2, g):
    xd = _b_dup(x2, g)
    lo = _half_mask(xd.shape, 0)
    zero = jnp.zeros_like(xd)
    return jnp.concatenate([jnp.where(lo, xd, zero), jnp.where(lo, zero, xd)], axis=0)


def _b_sink_col(s_ref, g, hf):
    rb = lax.broadcasted_iota(jnp.int32, (GROUP_ROWS, 1), 0) // CHUNK
    col = jnp.zeros((GROUP_ROWS, 1), F32)
    for pp in range(PAIRS_PER_GROUP):
        col = jnp.where(rb == pp, s_ref[0, 2 * (g * PAIRS_PER_GROUP + pp) + hf], col)
    return col


def _b_probs(qs, kblk, valid, sinks):
    s = lax.dot_general(qs, kblk, (((1,), (1,)), ((), ())), preferred_element_type=F32) * (B_HEAD_DIM ** -0.5)
    out = []
    for hf in range(2):
        sh = jnp.where(valid, s[:, hf * 2 * CHUNK:(hf + 1) * 2 * CHUNK], NEG)
        m = jnp.maximum(jnp.max(sh, axis=-1, keepdims=True), sinks[hf])
        e = jnp.exp(sh - m)
        es = jnp.exp(sinks[hf] - m)
        inv = 1.0 / (jnp.sum(e, axis=-1, keepdims=True) + es)
        out.append((e * inv, es * inv))
    return out


def _b_fold(acc, g):
    lo = _half_mask((2 * CHUNK, 128), 0)
    t = jnp.where(lo, acc[:2 * CHUNK], 0.0) + jnp.where(lo, 0.0, acc[2 * CHUNK:])
    return jnp.where(_half_mask((2 * CHUNK, 128), g), t + pltpu.roll(t, 64, 1), 0.0)


def _b_kv_specs(S):
    prev = lambda n: (jnp.maximum(n - 1, 0), 0)
    cur = lambda n: (n, 0)
    v_col = (2 * B_WIDTH + B_KV_WIDTH) // 128
    return [pl.BlockSpec((CHUNK, 128), prev), pl.BlockSpec((CHUNK, 128), cur),
            pl.BlockSpec((CHUNK, 128), lambda n: (jnp.maximum(n - 1, 0), v_col)), pl.BlockSpec((CHUNK, 128), lambda n: (n, v_col))]


def _b_attn_fwd(qn, kn, proj, sinks):
    S = qn.shape[0]

    def body(s_ref, q_ref, kp_ref, kc_ref, vp_ref, vc_ref, y_ref):
        n = pl.program_id(0)
        valid = _b_valid(n)
        k2 = jnp.concatenate([kp_ref[...], kc_ref[...]], axis=0).astype(F32)
        v2 = jnp.concatenate([vp_ref[...], vc_ref[...]], axis=0)
        for g in range(B_KV_HEADS):
            pairs = [g * PAIRS_PER_GROUP + pp for pp in range(PAIRS_PER_GROUP)]
            qs = jnp.concatenate([q_ref[:, p * 128:(p + 1) * 128] for p in pairs], axis=0)
            probs = _b_probs(qs, _b_blocks(k2, g), valid, [_b_sink_col(s_ref, g, hf) for hf in range(2)])
            pcat = jnp.concatenate([probs[0][0].astype(BF), probs[1][0].astype(BF)], axis=1)
            o = jnp.dot(pcat, _b_blocks(v2, g), preferred_element_type=F32)
            for pp, p in enumerate(pairs):
                y_ref[:, p * 128:(p + 1) * 128] = o[pp * CHUNK:(pp + 1) * CHUNK].astype(BF)

    return pl.pallas_call(
        body, name="b_attn_fwd", grid=(S // CHUNK,),
        in_specs=[pl.BlockSpec(memory_space=pltpu.SMEM), pl.BlockSpec((CHUNK, B_WIDTH), lambda n: (n, 0))] + _b_kv_specs(S),
        out_specs=pl.BlockSpec((CHUNK, B_WIDTH), lambda n: (n, 0)),
        out_shape=jax.ShapeDtypeStruct((S, B_WIDTH), BF), compiler_params=_params(("arbitrary",)),
    )(sinks, qn, kn, kn, proj, proj)


def _b_attn_bwd(qn, kn, proj, sinks, dy, after=()):
    S = qn.shape[0]

    def body(s_ref, q_ref, kp_ref, kc_ref, vp_ref, vc_ref, dy_ref, dq_ref, dk_ref, dv_ref, ds_ref):
        n = pl.program_id(0)

        @pl.when(n == 0)
        def _():
            dk_ref[...] = jnp.zeros_like(dk_ref)
            dv_ref[...] = jnp.zeros_like(dv_ref)
            ds_ref[...] = jnp.zeros_like(ds_ref)

        valid = _b_valid(n)
        k2 = jnp.concatenate([kp_ref[...], kc_ref[...]], axis=0).astype(F32)
        v2 = jnp.concatenate([vp_ref[...], vc_ref[...]], axis=0)
        lane = lax.broadcasted_iota(jnp.int32, (CHUNK, 128), 1)
        dk2 = jnp.zeros((2 * CHUNK, 128), F32)
        dv2 = jnp.zeros((2 * CHUNK, 128), F32)
        dsink = jnp.zeros((CHUNK, 128), F32)
        scale = B_HEAD_DIM ** -0.5
        nt = (((1,), (1,)), ((), ()))
        tn = (((0,), (0,)), ((), ()))
        for g in range(B_KV_HEADS):
            pairs = [g * PAIRS_PER_GROUP + pp for pp in range(PAIRS_PER_GROUP)]
            qs = jnp.concatenate([q_ref[:, p * 128:(p + 1) * 128] for p in pairs], axis=0)
            do = jnp.concatenate([dy_ref[:, p * 128:(p + 1) * 128] for p in pairs], axis=0)
            do_b = do.astype(BF)
            kblk, vblk = _b_blocks(k2, g), _b_blocks(v2, g)
            probs = _b_probs(qs, kblk, valid, [_b_sink_col(s_ref, g, hf) for hf in range(2)])
            pcat = jnp.concatenate([probs[0][0].astype(BF), probs[1][0].astype(BF)], axis=1)
            o = jnp.dot(pcat, vblk, preferred_element_type=F32)
            dp = lax.dot_general(do_b, vblk, nt, preferred_element_type=F32)
            prod = do * o
            ds_halves = []
            for hf in range(2):
                pr, ps = probs[hf]
                delta = jnp.sum(jnp.where(_half_mask(prod.shape, hf), prod, 0.0), axis=-1, keepdims=True)
                ds_halves.append((pr * (dp[:, hf * 2 * CHUNK:(hf + 1) * 2 * CHUNK] - delta) * scale).astype(BF))
                t = -ps * delta
                for pp, p in enumerate(pairs):
                    dsink = dsink + jnp.where(lane == 2 * p + hf, t[pp * CHUNK:(pp + 1) * CHUNK], 0.0)
            dsc = jnp.concatenate(ds_halves, axis=1)
            dq = jnp.dot(dsc, kblk, preferred_element_type=F32)
            for pp, p in enumerate(pairs):
                dq_ref[:, p * 128:(p + 1) * 128] = dq[pp * CHUNK:(pp + 1) * CHUNK]
            dk2 = dk2 + _b_fold(lax.dot_general(dsc, qs, tn, preferred_element_type=F32), g)
            dv2 = dv2 + _b_fold(lax.dot_general(pcat, do_b, tn, preferred_element_type=F32), g)
        ds_ref[...] += dsink
        cur = pl.ds(pl.multiple_of(n * CHUNK, CHUNK), CHUNK)
        dk_ref[cur, :] += dk2[CHUNK:]
        dv_ref[cur, :] += dv2[CHUNK:]

        @pl.when(n > 0)
        def _():
            prv = pl.ds(pl.multiple_of((n - 1) * CHUNK, CHUNK), CHUNK)
            dk_ref[prv, :] += dk2[:CHUNK]
            dv_ref[prv, :] += dv2[:CHUNK]

    full = pl.BlockSpec((S, 128), lambda n: (0, 0))
    return pl.pallas_call(
        _hide(body, 7, len(after)), name="b_attn_bwd", grid=(S // CHUNK,),
        in_specs=[pl.BlockSpec(memory_space=pltpu.SMEM), pl.BlockSpec((CHUNK, B_WIDTH), lambda n: (n, 0))] + _b_kv_specs(S)
        + [pl.BlockSpec((CHUNK, B_WIDTH), lambda n: (n, 0))] + _hidden_specs(after),
        out_specs=[pl.BlockSpec((CHUNK, B_WIDTH), lambda n: (n, 0)), full, full, pl.BlockSpec((CHUNK, 128), lambda n: (0, 0))],
        out_shape=[jax.ShapeDtypeStruct((S, B_WIDTH), F32), jax.ShapeDtypeStruct((S, 128), F32), jax.ShapeDtypeStruct((S, 128), F32),
                   jax.ShapeDtypeStruct((CHUNK, 128), F32)],
        compiler_params=_params(("arbitrary",)),
    )(sinks, qn, kn, kn, proj, proj, dy, *after)


def _c_block(q, k, v, gq, gk):
    qn = q * lax.rsqrt(jnp.mean(q * q, axis=-1, keepdims=True) + EPS) * gq
    kn = k * lax.rsqrt(jnp.mean(k * k, axis=-1, keepdims=True) + EPS) * gk
    s = lax.dot_general(qn.astype(BF), kn.astype(BF), (((1,), (1,)), ((), ())), preferred_element_type=F32) * (C_HEAD_DIM ** -0.5)
    p = jax.nn.softmax(s, axis=-1)
    return jnp.dot(p.astype(BF), v.astype(BF), preferred_element_type=F32)


def _c_specs(S, M, tq):
    q_col = (2 * A_WIDTH + B_WIDTH + 2 * B_KV_WIDTH) // 128
    return [pl.BlockSpec((tq, 128), lambda h, i: (i, q_col + h)), pl.BlockSpec((M, 128), lambda h, i: (0, h)),
            pl.BlockSpec((M, 128), lambda h, i: (0, C_HEADS + h)), pl.BlockSpec((1, 128), lambda h, i: (0, 0)),
            pl.BlockSpec((1, 128), lambda h, i: (0, 0))]


def _c_fwd(proj, kv, gq, gk):
    S, M = proj.shape[0], kv.shape[0]
    tq = _pick(S, (512,))

    def body(q_ref, k_ref, v_ref, gq_ref, gk_ref, y_ref):
        y_ref[...] = _c_block(q_ref[...], k_ref[...], v_ref[...], gq_ref[...], gk_ref[...]).astype(BF)

    return pl.pallas_call(
        body, name="c_fwd", grid=(C_HEADS, S // tq), in_specs=_c_specs(S, M, tq),
        out_specs=pl.BlockSpec((tq, 128), lambda h, i: (i, h)),
        out_shape=jax.ShapeDtypeStruct((S, C_WIDTH), BF), compiler_params=_params(("parallel", "parallel")),
    )(proj, kv, kv, gq, gk)


def _c_bwd(proj, kv, gq, gk, dy):
    S, M = proj.shape[0], kv.shape[0]
    tq = _pick(S, (512,))

    def body(q_ref, k_ref, v_ref, gq_ref, gk_ref, dy_ref, dq_ref, dk_ref, dv_ref, dgq_ref, dgk_ref):
        i = pl.program_id(1)
        _, vjp = jax.vjp(_c_block, q_ref[...], k_ref[...], v_ref[...], gq_ref[...], gk_ref[...])
        dq, dk, dv, dgq, dgk = vjp(dy_ref[...])
        dq_ref[...] = dq.astype(BF)

        @pl.when(i == 0)
        def _():
            dk_ref[...] = dk
            dv_ref[...] = dv
            dgq_ref[...] = dgq
            dgk_ref[...] = dgk

        @pl.when(i > 0)
        def _():
            dk_ref[...] += dk
            dv_ref[...] += dv
            dgq_ref[...] += dgq
            dgk_ref[...] += dgk

    return pl.pallas_call(
        body, name="c_bwd", grid=(C_HEADS, S // tq),
        in_specs=_c_specs(S, M, tq) + [pl.BlockSpec((tq, 128), lambda h, i: (i, h))],
        out_specs=[pl.BlockSpec((tq, 128), lambda h, i: (i, h)), pl.BlockSpec((M, 128), lambda h, i: (0, h)),
                   pl.BlockSpec((M, 128), lambda h, i: (0, h)), pl.BlockSpec((None, 1, 128), lambda h, i: (h, 0, 0)),
                   pl.BlockSpec((None, 1, 128), lambda h, i: (h, 0, 0))],
        out_shape=[jax.ShapeDtypeStruct((S, C_WIDTH), BF), jax.ShapeDtypeStruct((M, C_WIDTH), F32), jax.ShapeDtypeStruct((M, C_WIDTH), F32),
                   jax.ShapeDtypeStruct((C_HEADS, 1, 128), F32), jax.ShapeDtypeStruct((C_HEADS, 1, 128), F32)],
        compiler_params=_params(("parallel", "arbitrary")),
    )(proj, kv, kv, gq, gk, dy)


def _merge_specs(S, D, tm, tn, ks):
    off = GATE_OFF // tn
    nd = D // tn
    gates = [pl.BlockSpec((tm, tn), functools.partial(lambda b, m, n: (m, off + b * nd + n), b)) for b in range(3)]
    ys = [pl.BlockSpec((tm, k), lambda m, n: (m, 0)) for k in ks]
    ws = [pl.BlockSpec((None, k, tn), lambda m, n: (n, 0, 0)) for k in ks]
    return gates, ys, ws


def _merge_fwd(proj, ys, ws):
    S = proj.shape[0]
    tn = ws[0].shape[2]
    D = N_DEV * tn
    ks = [w.shape[1] for w in ws]
    tm = _pick(S, (1024,))
    gates, y_specs, w_specs = _merge_specs(S, D, tm, tn, ks)

    def body(ga_ref, gb_ref, gc_ref, ya_ref, yb_ref, yc_ref, wa_ref, wb_ref, wc_ref, m_ref, za_ref, zb_ref, zc_ref):
        acc = None
        for g_ref, y_ref, w_ref, z_ref in ((ga_ref, ya_ref, wa_ref, za_ref), (gb_ref, yb_ref, wb_ref, zb_ref),
                                           (gc_ref, yc_ref, wc_ref, zc_ref)):
            z = jnp.dot(y_ref[...], w_ref[...], preferred_element_type=F32)
            z_ref[...] = z.astype(BF)
            t = jax.nn.sigmoid(g_ref[...]) * z
            acc = t if acc is None else acc + t
        m_ref[...] = acc.astype(BF)

    tile = pl.BlockSpec((tm, tn), lambda m, n: (m, n))
    return pl.pallas_call(
        body, name="merge_fwd", grid=(S // tm, D // tn), in_specs=gates + y_specs + w_specs,
        out_specs=[tile, tile, tile, tile], out_shape=[jax.ShapeDtypeStruct((S, D), BF)] * 4,
        compiler_params=_params(("parallel", "parallel")),
    )(proj, proj, proj, *ys, *ws)


def _merge_bwd(proj, zs, dm, ws, after=()):
    S = proj.shape[0]
    tn = ws[0].shape[2]
    D = N_DEV * tn
    ks = [w.shape[1] for w in ws]
    tm = _pick(S, (1024,))
    gates, _, w_specs = _merge_specs(S, D, tm, tn, ks)
    nt = (((1,), (1,)), ((), ()))

    def body(ga_ref, gb_ref, gc_ref, za_ref, zb_ref, zc_ref, dm_ref, wa_ref, wb_ref, wc_ref,
             dza_ref, dzb_ref, dzc_ref, dga_ref, dgb_ref, dgc_ref, dya_ref, dyb_ref, dyc_ref):
        n = pl.program_id(1)
        dmv = dm_ref[...]
        for g_ref, z_ref, w_ref, dz_ref, dg_ref, dy_ref in (
                (ga_ref, za_ref, wa_ref, dza_ref, dga_ref, dya_ref), (gb_ref, zb_ref, wb_ref, dzb_ref, dgb_ref, dyb_ref),
                (gc_ref, zc_ref, wc_ref, dzc_ref, dgc_ref, dyc_ref)):
            sg = jax.nn.sigmoid(g_ref[...])
            dz = (sg * dmv).astype(BF)
            dz_ref[...] = dz
            dg_ref[...] = (dmv * z_ref[...].astype(F32) * sg * (1.0 - sg)).astype(BF)
            part = lax.dot_general(dz, w_ref[...], nt, preferred_element_type=F32)

            @pl.when(n == 0)
            def _():
                dy_ref[...] = part

            @pl.when(n > 0)
            def _():
                dy_ref[...] += part

    tile = pl.BlockSpec((tm, tn), lambda m, n: (m, n))
    dys = [pl.BlockSpec((tm, k), lambda m, n: (m, 0)) for k in ks]
    return pl.pallas_call(
        _hide(body, 10, len(after)), name="merge_bwd", grid=(S // tm, D // tn),
        in_specs=gates + [tile, tile, tile, tile] + w_specs + _hidden_specs(after),
        out_specs=[tile] * 6 + dys,
        out_shape=[jax.ShapeDtypeStruct((S, D), BF)] * 6 + [jax.ShapeDtypeStruct((S, k), F32) for k in ks],
        compiler_params=_params(("parallel", "arbitrary")),
    )(proj, proj, proj, *zs, dm, *ws, *after)


PAD = 8


def _stage_shift_down(us_ref, u_ref):
    S = u_ref.shape[1]
    us_ref[:, 0:PAD, :] = jnp.zeros((2, PAD, us_ref.shape[2]), F32)
    us_ref[:, PAD:S + PAD, :] = u_ref[...].astype(F32)


ROWS = 64


def _conv3(us_ref, part, r0, w, b):
    return (us_ref[part, pl.ds(r0 + PAD, ROWS), :] * w[2:3] + us_ref[part, pl.ds(r0 + PAD - 1, ROWS), :] * w[1:2]
            + us_ref[part, pl.ds(r0 + PAD - 2, ROWS), :] * w[0:1] + b)


def _ffn_specs(S, F, tc, c):
    per = c // tc

    def w_spec(half):
        return pl.BlockSpec((None, 3, tc), lambda j: (half * (N_DEV // 2) + j // per, 0, j % per))

    return [pl.BlockSpec((2, S, tc), lambda j: (0, 0, j)), w_spec(0), w_spec(1), pl.BlockSpec((2, 1, tc), lambda j: (0, 0, j))]


def _ffn_tile(F, c):
    tc = 128
    if c % tc or F % tc:
        raise ValueError(f"ffn tile {tc} does not divide {c}, {F}")
    return tc


def _ffn_act_fwd(up3, cws, cb3):
    _, S, F = up3.shape
    c = cws.shape[2]
    tc = _ffn_tile(F, c)

    def body(u_ref, wa_ref, wb_ref, b_ref, o_ref, us_ref):
        _stage_shift_down(us_ref, u_ref)
        wa, wb, ba, bb = wa_ref[...], wb_ref[...], b_ref[0], b_ref[1]

        def step(i, carry):
            r0 = pl.multiple_of(i * ROWS, ROWS)
            ca = _conv3(us_ref, 0, r0, wa, ba)
            cb = _conv3(us_ref, 1, r0, wb, bb)
            o_ref[pl.ds(r0, ROWS), :] = (ca * jax.nn.sigmoid(ca) * cb).astype(BF)
            return carry

        lax.fori_loop(0, S // ROWS, step, 0, unroll=4)

    return pl.pallas_call(
        body, name="ffn_act_fwd", grid=(F // tc,), in_specs=_ffn_specs(S, F, tc, c),
        out_specs=pl.BlockSpec((S, tc), lambda j: (0, j)), out_shape=jax.ShapeDtypeStruct((S, F), BF),
        scratch_shapes=[pltpu.VMEM((2, S + PAD, tc), F32)],
        compiler_params=_params(("parallel",)),
    )(up3, cws, cws, cb3)


def _ffn_act_bwd(up3, cws, cb3, dact, after=()):
    _, S, F = up3.shape
    c = cws.shape[2]
    tc = _ffn_tile(F, c)

    def body(u_ref, wa_ref, wb_ref, b_ref, da_ref, du_ref, dw_ref, db_ref, us_ref, dcs_ref):
        _stage_shift_down(us_ref, u_ref)
        ws = (wa_ref[...], wb_ref[...])
        ba, bb = b_ref[0], b_ref[1]
        dcs_ref[:, S:S + PAD, :] = jnp.zeros((2, PAD, tc), F32)

        def conv_grads(i, carry):
            r0 = pl.multiple_of(i * ROWS, ROWS)
            ca = _conv3(us_ref, 0, r0, ws[0], ba)
            cb = _conv3(us_ref, 1, r0, ws[1], bb)
            sg = jax.nn.sigmoid(ca)
            dav = da_ref[pl.ds(r0, ROWS), :].astype(F32)
            dcs_ref[0, pl.ds(r0, ROWS), :] = dav * cb * sg * (1.0 + ca * (1.0 - sg))
            dcs_ref[1, pl.ds(r0, ROWS), :] = dav * ca * sg
            return carry

        lax.fori_loop(0, S // ROWS, conv_grads, 0, unroll=4)

        def fold(v):
            return jnp.sum(v.reshape(ROWS // 8, 8, tc), axis=0)

        def input_grads(i, acc):
            r0 = pl.multiple_of(i * ROWS, ROWS)
            new = []
            for part in range(2):
                w = ws[part]
                dc = dcs_ref[part, pl.ds(r0, ROWS), :]
                dc1 = dcs_ref[part, pl.ds(r0 + 1, ROWS), :]
                dc2 = dcs_ref[part, pl.ds(r0 + 2, ROWS), :]
                u = us_ref[part, pl.ds(r0 + PAD, ROWS), :]
                du_ref[part, pl.ds(r0, ROWS), :] = (dc * w[2:3] + dc1 * w[1:2] + dc2 * w[0:1]).astype(BF)
                sums = (fold(dc2 * u), fold(dc1 * u), fold(dc * u), fold(dc))
                new += [a + s for a, s in zip(acc[4 * part:4 * part + 4], sums)]
            return tuple(new)

        acc = lax.fori_loop(0, S // ROWS, input_grads, tuple(jnp.zeros((8, tc), F32) for _ in range(8)), unroll=4)
        for part in range(2):
            for j in range(3):
                dw_ref[part, j:j + 1, :] = jnp.sum(acc[4 * part + j], axis=0, keepdims=True)
            db_ref[part] = jnp.sum(acc[4 * part + 3], axis=0, keepdims=True)

    return pl.pallas_call(
        _hide(body, 5, len(after)), name="ffn_act_bwd", grid=(F // tc,),
        in_specs=_ffn_specs(S, F, tc, c) + [pl.BlockSpec((S, tc), lambda j: (0, j))] + _hidden_specs(after),
        out_specs=[pl.BlockSpec((2, S, tc), lambda j: (0, 0, j)), pl.BlockSpec((2, 3, tc), lambda j: (0, 0, j)),
                   pl.BlockSpec((2, 1, tc), lambda j: (0, 0, j))],
        out_shape=[jax.ShapeDtypeStruct((2, S, F), BF), jax.ShapeDtypeStruct((2, 3, F), F32), jax.ShapeDtypeStruct((2, 1, F), F32)],
        scratch_shapes=[pltpu.VMEM((2, S + PAD, tc), F32), pltpu.VMEM((2, S + PAD, tc), F32)],
        compiler_params=_params(("parallel",)),
    )(up3, cws, cws, cb3, dact, *after)


def _residual_rms(a, w, x, g, name, tm=512):
    S, K = a.shape
    D = w.shape[1]
    tm = _pick(S, (tm,))

    def body(a_ref, w_ref, x_ref, g_ref, x1_ref, h_ref, r_ref):
        x1 = jnp.dot(a_ref[...], w_ref[...], preferred_element_type=F32) + x_ref[...]
        r = lax.rsqrt(jnp.mean(x1 * x1, axis=-1, keepdims=True) + EPS)
        x1_ref[...] = x1
        h_ref[...] = (x1 * r * g_ref[...]).astype(BF)
        r_ref[...] = r

    row = pl.BlockSpec((tm, D), lambda i: (i, 0))
    return pl.pallas_call(
        body, name=name, grid=(S // tm,),
        in_specs=[pl.BlockSpec((tm, K), lambda i: (i, 0)), pl.BlockSpec((K, D), lambda i: (0, 0)), row, pl.BlockSpec((1, D), lambda i: (0, 0))],
        out_specs=[row, row, pl.BlockSpec((tm, 1), lambda i: (i, 0))],
        out_shape=[jax.ShapeDtypeStruct((S, D), F32), jax.ShapeDtypeStruct((S, D), BF), jax.ShapeDtypeStruct((S, 1), F32)],
        compiler_params=_params(("parallel",)),
    )(a, w, x, g)


def _out_loss(act, w_down, x1, target, tm=512, tn=1024, tk=1408):
    S, F = act.shape
    D = w_down.shape[1]
    tm, tn, tk = _pick(S, (tm,)), _pick(D, (tn,)), _pick(F, (tk,))
    nm, nn, nk = S // tm, D // tn, F // tk

    def body(a_ref, b_ref, x_ref, t_ref, dy_ref, dyb_ref, l_ref, acc):
        m, n, k = pl.program_id(0), pl.program_id(1), pl.program_id(2)

        @pl.when((m == 0) & (n == 0) & (k == 0))
        def _():
            l_ref[...] = jnp.zeros_like(l_ref)

        @pl.when(k == 0)
        def _():
            acc[...] = jnp.zeros_like(acc)

        acc[...] += jnp.dot(a_ref[...], b_ref[...], preferred_element_type=F32)

        @pl.when(k == nk - 1)
        def _():
            e = acc[...] + x_ref[...] - t_ref[...]
            dy = e * (1.0 / D)
            dy_ref[...] = dy
            dyb_ref[...] = dy.astype(BF)
            l_ref[...] += jnp.sum(jnp.sum(e * e, axis=-1, keepdims=True), axis=0, keepdims=True) * (0.5 / D)

    tile = pl.BlockSpec((tm, tn), lambda m, n, k: (m, n))
    return pl.pallas_call(
        body, name="mm_y_loss", grid=(nm, nn, nk),
        in_specs=[pl.BlockSpec((tm, tk), lambda m, n, k: (m, k)), pl.BlockSpec((tk, tn), lambda m, n, k: (k, n)), tile, tile],
        out_specs=[tile, tile, pl.BlockSpec((8, 128), lambda m, n, k: (0, 0))],
        out_shape=[jax.ShapeDtypeStruct((S, D), F32), jax.ShapeDtypeStruct((S, D), BF), jax.ShapeDtypeStruct((8, 128), F32)],
        scratch_shapes=[pltpu.VMEM((tm, tn), F32)],
        compiler_params=_params(("arbitrary", "arbitrary", "arbitrary")),
    )(act, w_down, x1, target)


def _allgather(shards, name):
    n = len(shards)

    def body(*refs):
        ins, outs = refs[:n], refs[n:2 * n]
        send_sems, recv_sems, local_sems = refs[2 * n:]
        x, y, c = lax.axis_index("x"), lax.axis_index("y"), lax.axis_index("c")
        me, sibling = (x, y, c), (x, y, 1 - c)
        chips = [(1 - x, y), (x, 1 - y), (1 - x, 1 - y)]

        def blk(w, px, py, pc):
            return outs[w].at[4 * px + 2 * py + pc]

        def copy(w, k, block, to, src=None):
            return pltpu.make_async_remote_copy(
                src_ref=blk(w, *block) if src is None else src, dst_ref=blk(w, *block),
                send_sem=send_sems.at[w, k], recv_sem=recv_sems.at[w, k], device_id=to, device_id_type=MESH)

        started = []
        mine = []
        for w in range(n):
            mine.append(pltpu.make_async_copy(ins[w], blk(w, *me), local_sems.at[w]))
            mine[-1].start()
            first = [copy(w, 0, me, sibling, src=ins[w])]
            first += [copy(w, 1 + j, me, (*chip, c), src=ins[w]) for j, chip in enumerate(chips)]
            for cp in first:
                cp.start()
            started += first
        for w in range(n):
            for j, chip in enumerate(chips):
                copy(w, 1 + j, (*chip, c), me).wait_recv()
                fwd = copy(w, 4 + j, (*chip, c), sibling)
                fwd.start()
                started.append(fwd)
        for w in range(n):
            copy(w, 0, sibling, me).wait_recv()
            for j, chip in enumerate(chips):
                copy(w, 4 + j, (*chip, 1 - c), me).wait_recv()
        for cp in started:
            cp.wait_send()
        for cp in mine:
            cp.wait()

    whole = pl.BlockSpec(memory_space=pltpu.VMEM)
    outs = pl.pallas_call(
        body, name=name, in_specs=[whole] * n, out_specs=[whole] * n,
        out_shape=[jax.ShapeDtypeStruct((N_DEV,) + s.shape, s.dtype) for s in shards],
        scratch_shapes=[pltpu.SemaphoreType.DMA((n, 7)), pltpu.SemaphoreType.DMA((n, 7)), pltpu.SemaphoreType.DMA((n,))],
    )(*shards)
    return list(outs)


def _allgather_seq(shards, name, collective_id, after=()):
    n = len(shards)
    n_after = len(after)

    halves = [s.shape[0] % 32 == 0 for s in shards]
    n_sem = 8
    to_diagonal = not all(halves)

    def body(*refs):
        ins, outs = refs[:n], refs[n + n_after:2 * n + n_after]
        send_sems, recv_sems, local_sems = refs[2 * n + n_after:]
        x, y, c = lax.axis_index("x"), lax.axis_index("y"), lax.axis_index("c")
        me, sibling = (x, y, c), (x, y, 1 - c)
        x_nb, y_nb, diag = (1 - x, y, c), (x, 1 - y, c), (1 - x, 1 - y, c)
        peers = [sibling, x_nb, y_nb] + ([diag] if to_diagonal else [])
        barrier = pltpu.get_barrier_semaphore()
        for peer in peers:
            pl.semaphore_signal(barrier, inc=1, device_id=peer, device_id_type=MESH)
        pl.semaphore_wait(barrier, len(peers))

        def blk(w, dev, rows=None):
            ref = outs[w].at[4 * dev[0] + 2 * dev[1] + dev[2]]
            return ref if rows is None else ref.at[rows]

        def copy(w, k, block, to, src=None, rows=None):
            return pltpu.make_async_remote_copy(
                src_ref=blk(w, block, rows) if src is None else src, dst_ref=blk(w, block, rows),
                send_sem=send_sems.at[n_sem * w + k], recv_sem=recv_sems.at[n_sem * w + k], device_id=to, device_id_type=MESH)

        def top(w):
            return pl.ds(0, shards[w].shape[0] // 2)

        def bottom(w):
            return pl.ds(shards[w].shape[0] // 2, shards[w].shape[0] // 2)

        started = []
        mine = []
        for w in range(n):
            mine.append(pltpu.make_async_copy(ins[w], blk(w, me), local_sems.at[w]))
            mine[-1].start()
            first = [copy(w, 0, me, sibling, src=ins[w]), copy(w, 1, me, x_nb, src=ins[w]), copy(w, 2, me, y_nb, src=ins[w])]
            if not halves[w]:
                first.append(copy(w, 3, me, diag, src=ins[w]))
            for cp in first:
                cp.start()
            started += first
        for w in range(n):
            copy(w, 1, x_nb, me).wait_recv()
            onward = [copy(w, 5, x_nb, sibling)] + ([copy(w, 3, x_nb, y_nb, rows=top(w))] if halves[w] else [])
            copy(w, 2, y_nb, me).wait_recv()
            onward += [copy(w, 6, y_nb, sibling)] + ([copy(w, 4, y_nb, x_nb, rows=bottom(w))] if halves[w] else [])
            for cp in onward:
                cp.start()
            started += onward
        for w in range(n):
            if halves[w]:
                copy(w, 3, diag, me, rows=top(w)).wait_recv()
                copy(w, 4, diag, me, rows=bottom(w)).wait_recv()
            else:
                copy(w, 3, diag, me).wait_recv()
            fwd = copy(w, 7, diag, sibling)
            fwd.start()
            started.append(fwd)
        for w in range(n):
            for k, dev in ((0, sibling), (5, (1 - x, y, 1 - c)), (6, (x, 1 - y, 1 - c)), (7, (1 - x, 1 - y, 1 - c))):
                copy(w, k, dev, me).wait_recv()
        for cp in started:
            cp.wait_send()
        for cp in mine:
            cp.wait()

    outs = pl.kernel(
        body, name=name, out_type=[jax.ShapeDtypeStruct((N_DEV,) + s.shape, s.dtype) for s in shards],
        mesh=plsc.ScalarSubcoreMesh(axis_name="seq", num_cores=1),
        scratch_types=[pltpu.SemaphoreType.DMA((n_sem * n,)), pltpu.SemaphoreType.DMA((n_sem * n,)), pltpu.SemaphoreType.DMA((n,))],
        compiler_params=pltpu.CompilerParams(collective_id=collective_id),
    )(*shards, *after)
    return list(outs)


def _chip_exchange(sums, name, collective_id):
    n = len(sums)

    def body(*refs):
        ins, outs = refs[:n], refs[n:2 * n]
        send_sems, recv_sems = refs[2 * n:]
        x, y, c = lax.axis_index("x"), lax.axis_index("y"), lax.axis_index("c")
        chips = [(1 - x, y), (x, 1 - y), (1 - x, 1 - y)]
        barrier = pltpu.get_barrier_semaphore()
        for px, py in chips:
            pl.semaphore_signal(barrier, inc=1, device_id=(px, py, c), device_id_type=MESH)
        pl.semaphore_wait(barrier, 3)
        copies = []
        for w in range(n):
            for k, (px, py) in enumerate(chips):
                copies.append(pltpu.make_async_remote_copy(
                    src_ref=ins[w].at[2 * px + py], dst_ref=outs[w].at[k], send_sem=send_sems.at[3 * w + k],
                    recv_sem=recv_sems.at[3 * w + k], device_id=(px, py, c), device_id_type=MESH))
        for cp in copies:
            cp.start()
        for cp in copies:
            cp.wait()

    outs = pl.kernel(
        body, name=name, out_type=[jax.ShapeDtypeStruct((3,) + s.shape[1:], s.dtype) for s in sums],
        mesh=plsc.ScalarSubcoreMesh(axis_name="seq", num_cores=1),
        scratch_types=[pltpu.SemaphoreType.DMA((3 * n,)), pltpu.SemaphoreType.DMA((3 * n,))],
        compiler_params=pltpu.CompilerParams(collective_id=collective_id),
    )(*sums)
    return list(outs)


def _row_tile(r, c, elems=256 * 1024):
    want = max(8, elems // c)
    for t in range(min(want, r) // 8 * 8, 0, -8):
        if r % t == 0:
            return t
    return r


def _pair_add(g4, recv, core, name, after=()):
    _, _, r, c = g4.shape
    tr = _row_tile(r, c, 1024 * 1024)

    def body(core_ref, a_ref, b_ref, o_ref):
        o_ref[...] = (a_ref[...].astype(F32) + b_ref[...].astype(F32)).astype(BF)

    return pl.pallas_call(
        _hide(body, 3, len(after)), name=name,
        grid_spec=pltpu.PrefetchScalarGridSpec(
            num_scalar_prefetch=1, grid=(4, r // tr),
            in_specs=[pl.BlockSpec((None, None, tr, c), lambda p, i, s: (p, s[0], i, 0)),
                      pl.BlockSpec((None, tr, c), lambda p, i, s: (p, i, 0))] + _hidden_specs(after),
            out_specs=pl.BlockSpec((None, tr, c), lambda p, i, s: (p, i, 0))),
        out_shape=jax.ShapeDtypeStruct((4, r, c), BF), compiler_params=_params(("parallel", "parallel")),
    )(core, g4, recv, *after)


def _adam_math(w, g, m, v):
    m = ADAM_B1 * m + (1.0 - ADAM_B1) * g
    v = ADAM_B2 * v + (1.0 - ADAM_B2) * (g * g)
    m_hat = m / (1.0 - ADAM_B1 ** ADAM_STEP)
    v_hat = v / (1.0 - ADAM_B2 ** ADAM_STEP)
    delta = -ADAM_LR * (m_hat / (jnp.sqrt(v_hat) + ADAM_EPS) + ADAM_WD * w)
    return delta, m, v


def _adamw_big(sums, recv, chip, w, m, v, name, after=()):
    r, c = w.shape
    tr = _row_tile(r, c, 512 * 1024)

    def body(chip_ref, s_ref, r_ref, w_ref, m_ref, v_ref, g_out, d_out, m_out, v_out):
        g = s_ref[...].astype(F32) + r_ref[0].astype(F32)
        g = g + r_ref[1].astype(F32)
        g = g + r_ref[2].astype(F32)
        delta, mn, vn = _adam_math(w_ref[...], g, m_ref[...], v_ref[...])
        g_out[...] = g
        d_out[...] = delta
        m_out[...] = mn
        v_out[...] = vn

    row = pl.BlockSpec((tr, c), lambda i, s: (i, 0))
    return pl.pallas_call(
        _hide(body, 6, len(after)), name=name,
        grid_spec=pltpu.PrefetchScalarGridSpec(
            num_scalar_prefetch=1, grid=(r // tr,),
            in_specs=[pl.BlockSpec((None, tr, c), lambda i, s: (s[0], i, 0)), pl.BlockSpec((3, tr, c), lambda i, s: (0, i, 0)),
                      row, row, row] + _hidden_specs(after),
            out_specs=[row, row, row, row]),
        out_shape=[jax.ShapeDtypeStruct((r, c), F32)] * 4, compiler_params=_params(("parallel",)),
    )(chip, sums, recv, w, m, v, *after)


def _adamw_small(parts, ws, ms, vs, extra_parts, name):
    n, ne = len(ws), len(extra_parts)

    def total(p_ref):
        g = p_ref[0]
        for d in range(1, N_DEV):
            g = g + p_ref[d]
        return g

    def body(*refs):
        p_refs, w_refs, m_refs, v_refs = refs[:n], refs[n:2 * n], refs[2 * n:3 * n], refs[3 * n:4 * n]
        e_refs = refs[4 * n:4 * n + ne]
        outs = refs[4 * n + ne:]
        for i in range(n):
            g = total(p_refs[i])
            delta, mn, vn = _adam_math(w_refs[i][...], g, m_refs[i][...], v_refs[i][...])
            outs[4 * i][...] = g
            outs[4 * i + 1][...] = delta
            outs[4 * i + 2][...] = mn
            outs[4 * i + 3][...] = vn
        for i in range(ne):
            outs[4 * n + i][...] = total(e_refs[i])

    out_shape = []
    for w in ws:
        out_shape += [jax.ShapeDtypeStruct(w.shape, F32)] * 4
    out_shape += [jax.ShapeDtypeStruct(e.shape[1:], F32) for e in extra_parts]
    res = pl.pallas_call(body, name=name, out_shape=out_shape,
                         compiler_params=pltpu.CompilerParams(vmem_limit_bytes=VMEM_LIMIT))(*parts, *ws, *ms, *vs, *extra_parts)
    return [res[4 * i:4 * i + 4] for i in range(n)], list(res[4 * n:])


def _adamw_plain(g, w, m, v, name):
    def body(g_ref, w_ref, m_ref, v_ref, d_out, m_out, v_out):
        delta, mn, vn = _adam_math(w_ref[...], g_ref[...], m_ref[...], v_ref[...])
        d_out[...] = delta
        m_out[...] = mn
        v_out[...] = vn

    return pl.pallas_call(body, name=name, out_shape=[jax.ShapeDtypeStruct(w.shape, F32)] * 3)(g, w, m, v)


def kernel(x, mem, positions, g_mix, w_in, g_a_v, w_spatial, b_spatial, g_b_q, g_b_k, sinks, g_mem, w_mem_kv, g_c_q, g_c_k, w_branch_a, w_branch_b, w_branch_c, w_out, g_ffn, w_up, conv_w, conv_b, w_down, loss_target, m_g_mix, m_w_in, m_g_a_v, m_w_spatial, m_b_spatial, m_g_b_q, m_g_b_k, m_sinks, m_g_mem, m_w_mem_kv, m_g_c_q, m_g_c_k, m_w_branch_a, m_w_branch_b, m_w_branch_c, m_w_out, m_g_ffn, m_w_up, m_conv_w, m_conv_b, m_w_down, v_g_mix, v_w_in, v_g_a_v, v_w_spatial, v_b_spatial, v_g_b_q, v_g_b_k, v_sinks, v_g_mem, v_w_mem_kv, v_g_c_q, v_g_c_k, v_w_branch_a, v_w_branch_b, v_w_branch_c, v_w_out, v_g_ffn, v_w_up, v_conv_w, v_conv_b, v_w_down):
    S, D = x.shape[1], x.shape[2]
    M = mem.shape[1]
    F = w_down.shape[1] * N_DEV
    in_cols = w_in.shape[2] * N_DEV
    ax, ay, ac = lax.axis_index("x"), lax.axis_index("y"), lax.axis_index("c")
    core = jnp.reshape(ac, (1,)).astype(jnp.int32)
    chip = jnp.reshape(2 * ax + ay, (1,)).astype(jnp.int32)
    me = 4 * ax + 2 * ay + ac

    x2, mem2, tgt2 = x[0], mem[0], loss_target[0]

    big = dict(w_in=w_in[0].T, w_mem_kv=w_mem_kv[0], w_branch_a=w_branch_a[0], w_branch_b=w_branch_b[0],
               w_branch_c=w_branch_c[0], w_out=w_out[0], w_up=w_up[0], w_down=w_down[0])
    names = list(big)
    cast = {k: big[k].astype(BF) for k in names}
    W = {}
    cb3 = conv_b.reshape(2, 1, F)
    W["w_in"], = _allgather_seq([cast["w_in"]], "ag_seq0", 0)
    w_in_t = W["w_in"].reshape(in_cols, D)
    grp1 = ["w_mem_kv", "w_branch_a", "w_branch_b", "w_branch_c", "w_out"]
    res1 = _allgather_seq([cast[k] for k in grp1] + [conv_w[0]], "ag_seq1", 1, after=(_token((w_in_t,), "tok_w_in"),))
    W.update(zip(grp1, res1))
    cw3 = res1[-1]
    w_kv_f = W["w_mem_kv"].reshape(D, 2 * C_WIDTH)
    w_out_f = W["w_out"].reshape(D, D)

    half = ROPE_DIM // 2
    inv = ROPE_THETA ** (-jnp.arange(half, dtype=F32) / half)
    ang = positions[0].astype(F32)[:, None] * inv
    cos, sin = jnp.cos(ang), jnp.sin(ang)
    one, zero = jnp.ones((S, B_HEAD_DIM - ROPE_DIM), F32), jnp.zeros((S, B_HEAD_DIM - ROPE_DIM), F32)
    z8 = jnp.zeros((S, half), F32)
    ct = jnp.tile(jnp.concatenate([cos, cos, one], axis=1), (1, 2))
    sa = jnp.tile(jnp.concatenate([-sin, z8, zero], axis=1), (1, 2))
    sb = jnp.tile(jnp.concatenate([z8, sin, zero], axis=1), (1, 2))
    gq2, gk2 = jnp.tile(g_b_q, (1, 2)), jnp.tile(g_b_k, (1, 2))
    b_t = b_spatial[0].T

    h, rstd1 = _rms_fwd(x2, g_mix, "rms1_fwd")
    proj = _mm(h, w_in_t, "nt", F32, "mm_proj", tn=1280)
    y_a = _a_fwd(proj, g_a_v, w_spatial[0], b_t)
    W["w_up"], = _allgather_seq([cast["w_up"]], "ag_seq2", 2, after=(_token((W["w_out"], proj), "tok_group1"),))
    qn, kn = _b_pre(proj, gq2, gk2, ct, sa, sb)
    y_b = _b_attn_fwd(qn, kn, proj, sinks)
    mem_h, rstd_m = _rms_fwd(mem2, g_mem, "rmsmem_fwd")
    kv = _mm(mem_h, w_kv_f, "nn", F32, "mm_kv", after=(y_b,))
    y_c = _c_fwd(proj, kv, g_c_q, g_c_k)
    w_branches = [W["w_branch_a"], W["w_branch_b"], W["w_branch_c"]]
    merged, z_a, z_b, z_c = _merge_fwd(proj, [y_a, y_b, y_c], w_branches)
    x1, h2, rstd2 = _residual_rms(merged, w_out_f, x2, g_ffn, "mm_x1_rms2")
    W["w_down"], = _allgather_seq([cast["w_down"]], "ag_seq3", 3, after=(W["w_up"], h2))
    w_down_f = W["w_down"].reshape(F, D)
    up3 = _mm(h2, W["w_up"], "nn", BF, "mm_up", b_stack=True, out_parts=2)
    act = _ffn_act_fwd(up3, cw3, cb3)
    dy, dy_b, loss_acc = _out_loss(act, w_down_f, x1, tgt2)

    reduced = {}

    def as4(g):
        return g.reshape(4, 2, g.shape[1], g.shape[2])

    def finish_group(gi, keys, g4, from_sibling):
        sums = [_pair_add(a, b, core, "rs_add_" + k) for k, a, b in zip(keys, g4, from_sibling)]
        from_chips = _chip_exchange(sums, f"rs_chip{gi}", 4 + gi)
        reduced.update(zip(keys, zip(sums, from_chips)))
        return tuple(sums)

    d_act = _mm(dy_b, w_down_f, "nt", BF, "mm_dact", tn=1408)
    g_down = _mm(act, dy_b, "tn", BF, "mm_gdown", tm=1408)
    d_up3, d_cw3, d_cb3 = _ffn_act_bwd(up3, cw3, cb3, d_act, after=(g_down,))
    grp0 = [as4(g_down.reshape(N_DEV, F // N_DEV, D))]
    g_up, sib0 = _mm(h2, d_up3, "tn", BF, "mm_gup", b_parts=2, out_stack=True, exchange=grp0)
    sums0 = finish_group(0, ["w_down"], grp0, sib0)
    grp1 = [as4(g_up)]
    d_h2, sib1 = _mm(d_up3, W["w_up"], "nt", F32, "mm_dh2", a_parts=2, b_stack=True, tm=2048, after=sums0, exchange=grp1)
    sums1 = finish_group(1, ["w_up"], grp1, sib1)
    dx1, dx1_b, d_g_ffn = _rms_bwd(x1, rstd2, g_ffn, d_h2, dy, "rms2_bwd", after=sums1)
    g_out = _mm(merged, dx1_b, "tn", BF, "mm_gout")
    grp2 = [as4(g_out.reshape(N_DEV, D // N_DEV, D))]
    d_merged, sib2 = _mm(dx1_b, w_out_f, "nt", F32, "mm_dmerged", exchange=grp2)
    sums2 = finish_group(2, ["w_out"], grp2, sib2)
    dz_a, dz_b, dz_c, dga, dgb, dgc, dy_a, dy_b_, dy_c = _merge_bwd(proj, [z_a, z_b, z_c], d_merged, w_branches, after=sums2)
    g_ba = _mm(y_a, dz_a, "tn", BF, "mm_gba", out_stack=True)
    g_bb = _mm(y_b, dz_b, "tn", BF, "mm_gbb", out_stack=True)
    g_bc = _mm(y_c, dz_c, "tn", BF, "mm_gbc", out_stack=True)
    d_uv, d_g_a_v, d_w_s, d_b_t = _a_bwd(proj, g_a_v, w_spatial[0], b_t, dy_a, after=(g_ba, g_bb, g_bc))
    dqn, dkn, dv_b, dsink_rows = _b_attn_bwd(qn, kn, proj, sinks, dy_b_)
    d_qkv, d_gq2, d_gk2 = _b_pre_bwd(proj, gq2, gk2, ct, sa, sb, dqn, dkn, dv_b)
    dq_c, dk_c, dv_c, d_gcq, d_gck = _c_bwd(proj, kv, g_c_q, g_c_k, dy_c)
    dkv_b = jnp.concatenate([dk_c, dv_c], axis=1).astype(BF)
    d_memh = _mm(dkv_b, w_kv_f, "nt", F32, "mm_dmemh")
    g_kv = _mm(mem_h, dkv_b, "tn", BF, "mm_gkv")
    _, _, d_g_mem = _rms_bwd(mem2, rstd_m, g_mem, d_memh, None, "rmsmem_bwd")
    dproj = jnp.concatenate([d_uv, d_qkv, dq_c, dga, dgb, dgc], axis=1)
    grp3 = [as4(g_ba), as4(g_bb), as4(g_bc)]
    g_in, sib3 = _mm(dproj, h, "tn", BF, "mm_gin", tm=1280, exchange=grp3)
    sums3 = finish_group(3, ["w_branch_a", "w_branch_b", "w_branch_c"], grp3, sib3)
    grp4 = [as4(g_in.reshape(N_DEV, in_cols // N_DEV, D)), as4(g_kv.reshape(N_DEV, D // N_DEV, 2 * C_WIDTH))]
    d_h, sib4 = _mm(dproj, w_in_t, "nn", F32, "mm_dh", tm=2048, tk=1280, after=sums3, exchange=grp4)
    sums4 = finish_group(4, ["w_in", "w_mem_kv"], grp4, sib4)
    grad_x, _, d_g_mix = _rms_bwd(x2, rstd1, g_mix, d_h, dx1, "rms1_bwd", after=sums4)

    small_names =["g_mix", "g_a_v", "w_spatial", "b_spatial", "g_b_q", "g_b_k", "sinks", "g_mem", "g_c_q", "g_c_k", "g_ffn", "conv_b"]
    small_w = dict(g_mix=g_mix, g_a_v=g_a_v, w_spatial=w_spatial, b_spatial=b_spatial, g_b_q=g_b_q, g_b_k=g_b_k, sinks=sinks,
                   g_mem=g_mem, g_c_q=g_c_q, g_c_k=g_c_k, g_ffn=g_ffn, conv_b=conv_b)
    small_m = dict(g_mix=m_g_mix, g_a_v=m_g_a_v, w_spatial=m_w_spatial, b_spatial=m_b_spatial, g_b_q=m_g_b_q, g_b_k=m_g_b_k,
                   sinks=m_sinks, g_mem=m_g_mem, g_c_q=m_g_c_q, g_c_k=m_g_c_k, g_ffn=m_g_ffn, conv_b=m_conv_b)
    small_v = dict(g_mix=v_g_mix, g_a_v=v_g_a_v, w_spatial=v_w_spatial, b_spatial=v_b_spatial, g_b_q=v_g_b_q, g_b_k=v_g_b_k,
                   sinks=v_sinks, g_mem=v_g_mem, g_c_q=v_g_c_q, g_c_k=v_g_c_k, g_ffn=v_g_ffn, conv_b=v_conv_b)
    small_g = dict(
        g_mix=d_g_mix, g_a_v=d_g_a_v, w_spatial=d_w_s, b_spatial=d_b_t.T,
        g_b_q=d_gq2.reshape(2, B_HEAD_DIM).sum(0), g_b_k=d_gk2.reshape(2, B_HEAD_DIM).sum(0),
        sinks=dsink_rows.sum(0)[:B_HEADS], g_mem=d_g_mem, g_c_q=d_gcq.sum(0), g_c_k=d_gck.sum(0), g_ffn=d_g_ffn,
        conv_b=d_cb3)
    partial = [small_g[k].reshape(small_w[k].shape) for k in small_names] + [d_cw3, loss_acc[0:1]]
    parts = _allgather(partial, "ag_small")
    n_small = len(small_names)
    small_res, (g_cw3, loss_row) = _adamw_small(parts[:n_small], [small_w[k] for k in small_names], [small_m[k] for k in small_names],
                                                [small_v[k] for k in small_names], parts[n_small:], "adamw_small")
    loss = loss_row[0, 0]
    small_out = dict(zip(small_names, small_res))
    c_cw = 2 * F // N_DEV
    g_cw = lax.dynamic_slice(g_cw3, (me // (N_DEV // 2), 0, (me % (N_DEV // 2)) * c_cw), (1, 3, c_cw))[0]
    cw_res = _adamw_plain(g_cw, conv_w[0], m_conv_w[0], v_conv_w[0], "adamw_conv_w")
    big_out = {"conv_w": [g_cw[None]] + [a[None] for a in cw_res]}

    moments = dict(w_in=(m_w_in, v_w_in), w_mem_kv=(m_w_mem_kv, v_w_mem_kv), w_branch_a=(m_w_branch_a, v_w_branch_a),
                   w_branch_b=(m_w_branch_b, v_w_branch_b), w_branch_c=(m_w_branch_c, v_w_branch_c), w_out=(m_w_out, v_w_out),
                   w_up=(m_w_up, v_w_up), w_down=(m_w_down, v_w_down))
    token = (grad_x, small_res[0][0])
    for k in ["w_down", "w_up", "w_out", "w_branch_a", "w_branch_b", "w_branch_c", "w_mem_kv", "w_in"]:
        s, r = reduced[k]
        mk, vk = moments[k][0][0], moments[k][1][0]
        if k == "w_in":
            res = _adamw_big(s, r, chip, big[k], mk.T, vk.T, "adamw_" + k, after=token)
            big_out[k] = [a.T[None] for a in res]
        else:
            res = _adamw_big(s, r, chip, big[k], mk, vk, "adamw_" + k, after=token)
            big_out[k] = [a[None] for a in res]
        token = (res[0],)

    order = ["g_mix", "w_in", "g_a_v", "w_spatial", "b_spatial", "g_b_q", "g_b_k", "sinks", "g_mem", "w_mem_kv", "g_c_q", "g_c_k",
             "w_branch_a", "w_branch_b", "w_branch_c", "w_out", "g_ffn", "w_up", "conv_w", "conv_b", "w_down"]
    res = {**small_out, **big_out}
    outs = [loss, grad_x[None]]
    for field in range(4):
        outs += [res[k][field] for k in order]
    return tuple(outs)
```

```python
import functools

import jax
import jax.numpy as jnp
from jax import lax
from jax.experimental import pallas as pl
from jax.experimental.pallas import tpu as pltpu
from jax.experimental.pallas import tpu_sc as plsc

F32 = jnp.float32
BF = jnp.bfloat16
EPS = 1e-6
NEG = -1e30

N_DEV = 8
CHUNK = 128
A_GROUPS = 4
A_WIDTH = 512
B_HEADS = 16
B_KV_HEADS = 2
B_HEAD_DIM = 64
B_WIDTH = 1024
B_KV_WIDTH = 128
ROPE_DIM = 16
ROPE_THETA = 500000.0
C_HEADS = 4
C_HEAD_DIM = 128
C_WIDTH = 512
GATE_OFF = 2 * A_WIDTH + B_WIDTH + 2 * B_KV_WIDTH + C_WIDTH

ADAM_LR = 0.001
ADAM_B1 = 0.9
ADAM_B2 = 0.999
ADAM_EPS = 1e-08
ADAM_WD = 0.01
ADAM_STEP = 10

VMEM_LIMIT = 48 * 1024 * 1024
MESH = pl.DeviceIdType.MESH


def _pick(n, prefs):
    for p in prefs:
        if p <= n and n % p == 0:
            return p
    return n


def _params(sem):
    return pltpu.CompilerParams(dimension_semantics=sem, vmem_limit_bytes=VMEM_LIMIT)


def _hide(body, n_seen, n_hidden):
    if not n_hidden:
        return body

    def wrapped(*refs):
        return body(*refs[:n_seen], *refs[n_seen + n_hidden:])

    return wrapped


def _hidden_specs(after):
    return [pl.BlockSpec(memory_space=pl.ANY) for _ in after]


def _token(xs, name):
    def body(*refs):
        refs[-1][...] = jnp.zeros_like(refs[-1])

    return pl.pallas_call(body, name=name, in_specs=_hidden_specs(xs), out_shape=jax.ShapeDtypeStruct((8, 128), F32))(*xs)


def _mm(a, b, mode, out_dtype, name, *, resid=None, b_stack=False, a_parts=0, b_parts=0, out_parts=0,
        out_stack=False, tm=1024, tn=1024, tk=2048, after=(), exchange=()):
    if mode == "nn":
        M = a.shape[-2]
        K = a.shape[-1] * max(a_parts, 1)
        N = b.shape[-1] * (N_DEV if b_stack else 1)
        dims = (((1,), (0,)), ((), ()))
    elif mode == "nt":
        M = a.shape[-2]
        K = a.shape[-1] * max(a_parts, 1)
        N = b.shape[-2]
        dims = (((1,), (1,)), ((), ()))
    else:
        K = a.shape[-2]
        M = a.shape[-1]
        N = b.shape[-1] * max(b_parts, 1)
        dims = (((0,), (0,)), ((), ()))
    if b_stack and mode == "nn":
        tn = b.shape[-1]
    if b_stack and mode == "nt":
        tk = b.shape[-1]
    if out_stack:
        tn = N // N_DEV
    tm, tn, tk = _pick(M, (tm,)), _pick(N, (tn,)), _pick(K, (tk,))
    if M % tm or N % tn or K % tk:
        raise ValueError(f"{name}: tiles {tm},{tn},{tk} do not divide {M},{N},{K}")
    nm, nn, nk = M // tm, N // tn, K // tk

    def parts_idx(t, ntile, parts):
        per = ntile // parts
        return t // per, t % per

    if mode in ("nn", "nt"):
        if a_parts:
            a_spec = pl.BlockSpec((None, tm, tk), lambda m, n, k: (parts_idx(k, nk, a_parts)[0], m, parts_idx(k, nk, a_parts)[1]))
        else:
            a_spec = pl.BlockSpec((tm, tk), lambda m, n, k: (m, k))
    else:
        a_spec = pl.BlockSpec((tk, tm), lambda m, n, k: (k, m))
    if mode == "nn":
        if b_stack:
            b_spec = pl.BlockSpec((None, tk, tn), lambda m, n, k: (n, k, 0))
        else:
            b_spec = pl.BlockSpec((tk, tn), lambda m, n, k: (k, n))
    elif mode == "nt":
        if b_stack:
            b_spec = pl.BlockSpec((None, tn, tk), lambda m, n, k: (k, n, 0))
        else:
            b_spec = pl.BlockSpec((tn, tk), lambda m, n, k: (n, k))
    else:
        if b_parts:
            b_spec = pl.BlockSpec((None, tk, tn), lambda m, n, k: (parts_idx(n, nn, b_parts)[0], k, parts_idx(n, nn, b_parts)[1]))
        else:
            b_spec = pl.BlockSpec((tk, tn), lambda m, n, k: (k, n))
    if out_stack:
        out_shape = jax.ShapeDtypeStruct((N_DEV, M, tn), out_dtype)
        o_spec = pl.BlockSpec((None, tm, tn), lambda m, n, k: (n, m, 0))
    elif out_parts:
        out_shape = jax.ShapeDtypeStruct((out_parts, M, N // out_parts), out_dtype)
        o_spec = pl.BlockSpec((None, tm, tn), lambda m, n, k: (parts_idx(n, nn, out_parts)[0], m, parts_idx(n, nn, out_parts)[1]))
    else:
        out_shape = jax.ShapeDtypeStruct((M, N), out_dtype)
        o_spec = pl.BlockSpec((tm, tn), lambda m, n, k: (m, n))
    has_resid = resid is not None

    n_ex = len(exchange)
    n_in = 2 + has_resid + len(after)

    def body(*refs):
        a_ref, b_ref = refs[:2]
        r_ref = refs[2] if has_resid else None
        ex_in = refs[n_in:n_in + n_ex]
        o_ref = refs[n_in + n_ex]
        ex_out = refs[n_in + n_ex + 1:n_in + 2 * n_ex + 1]
        scratch = refs[n_in + 2 * n_ex + 1:]
        m_i, n_i, k = pl.program_id(0), pl.program_id(1), pl.program_id(2)

        def pushes():
            send_sems, recv_sems = scratch[-2:]
            x, y, c = lax.axis_index("x"), lax.axis_index("y"), lax.axis_index("c")
            return [pltpu.make_async_remote_copy(
                src_ref=ex_in[w].at[:, 1 - c], dst_ref=ex_out[w], send_sem=send_sems.at[w], recv_sem=recv_sems.at[w],
                device_id=(x, y, 1 - c), device_id_type=MESH) for w in range(n_ex)]

        if n_ex:
            @pl.when((m_i == 0) & (n_i == 0) & (k == 0))
            def _():
                for cp in pushes():
                    cp.start()

        if nk == 1:
            res = lax.dot_general(a_ref[...], b_ref[...], dims, preferred_element_type=F32)
            if has_resid:
                res = res + r_ref[...]
            o_ref[...] = res.astype(o_ref.dtype)
        else:
            acc = scratch[0]

            @pl.when(k == 0)
            def _():
                acc[...] = jnp.zeros_like(acc)

            acc[...] += lax.dot_general(a_ref[...], b_ref[...], dims, preferred_element_type=F32)

            @pl.when(k == nk - 1)
            def _():
                res = acc[...]
                if has_resid:
                    res = res + r_ref[...]
                o_ref[...] = res.astype(o_ref.dtype)

        if n_ex:
            @pl.when((m_i == nm - 1) & (n_i == nn - 1) & (k == nk - 1))
            def _():
                for cp in pushes():
                    cp.wait()

    in_specs = [a_spec, b_spec]
    args = [a, b]
    if has_resid:
        in_specs.append(pl.BlockSpec((tm, tn), lambda m, n, k: (m, n)))
        args.append(resid)
    in_specs += _hidden_specs(after) + _hidden_specs(exchange)
    args += list(after) + list(exchange)
    scratch_shapes = [pltpu.VMEM((tm, tn), F32)] if nk > 1 else []
    if not n_ex:
        return pl.pallas_call(
            body, name=name, grid=(nm, nn, nk), in_specs=in_specs, out_specs=o_spec, out_shape=out_shape,
            scratch_shapes=scratch_shapes, compiler_params=_params(("parallel", "parallel", "arbitrary")),
        )(*args)
    res = pl.pallas_call(
        body, name=name, grid=(nm, nn, nk), in_specs=in_specs, out_specs=[o_spec] + _hidden_specs(exchange),
        out_shape=[out_shape] + [jax.ShapeDtypeStruct((g.shape[0],) + g.shape[2:], g.dtype) for g in exchange],
        scratch_shapes=scratch_shapes + [pltpu.SemaphoreType.DMA((n_ex,)), pltpu.SemaphoreType.DMA((n_ex,))],
        compiler_params=_params(("arbitrary", "arbitrary", "arbitrary")),
    )(*args)
    return res[0], list(res[1:])


def _rms_fwd(x, g, name):
    R, D = x.shape
    tr = _pick(R, (256,))

    def body(x_ref, g_ref, h_ref, r_ref):
        xv = x_ref[...]
        r = lax.rsqrt(jnp.mean(xv * xv, axis=-1, keepdims=True) + EPS)
        h_ref[...] = (xv * r * g_ref[...]).astype(BF)
        r_ref[...] = r

    return pl.pallas_call(
        body, name=name, grid=(R // tr,),
        in_specs=[pl.BlockSpec((tr, D), lambda i: (i, 0)), pl.BlockSpec((1, D), lambda i: (0, 0))],
        out_specs=[pl.BlockSpec((tr, D), lambda i: (i, 0)), pl.BlockSpec((tr, 1), lambda i: (i, 0))],
        out_shape=[jax.ShapeDtypeStruct((R, D), BF), jax.ShapeDtypeStruct((R, 1), F32)],
        compiler_params=_params(("parallel",)),
    )(x, g)


def _rms_bwd(x, r, g, dh, dres, name, after=()):
    R, D = x.shape
    tr = _pick(R, (256,))
    has_res = dres is not None

    def body(*refs):
        if has_res:
            x_ref, r_ref, g_ref, dh_ref, dres_ref, dx_ref, dxb_ref, dg_ref = refs
        else:
            x_ref, r_ref, g_ref, dh_ref, dx_ref, dxb_ref, dg_ref = refs
        i = pl.program_id(0)
        xv, rv, dhv = x_ref[...], r_ref[...], dh_ref[...]
        gy = dhv * g_ref[...]
        c = jnp.sum(xv * gy, axis=-1, keepdims=True)
        dx = rv * gy - xv * (rv * rv * rv) * (c * (1.0 / D))
        if has_res:
            dx = dx + dres_ref[...]
        dx_ref[...] = dx
        dxb_ref[...] = dx.astype(BF)
        part = jnp.sum(dhv * xv * rv, axis=0, keepdims=True)

        @pl.when(i == 0)
        def _():
            dg_ref[...] = part

        @pl.when(i > 0)
        def _():
            dg_ref[...] += part

    row = pl.BlockSpec((tr, D), lambda i: (i, 0))
    in_specs = [row, pl.BlockSpec((tr, 1), lambda i: (i, 0)), pl.BlockSpec((1, D), lambda i: (0, 0)), row]
    args = [x, r, g, dh]
    if has_res:
        in_specs.append(row)
        args.append(dres)
    return pl.pallas_call(
        _hide(body, len(args), len(after)), name=name, grid=(R // tr,), in_specs=in_specs + _hidden_specs(after),
        out_specs=[row, row, pl.BlockSpec((1, D), lambda i: (0, 0))],
        out_shape=[jax.ShapeDtypeStruct((R, D), F32), jax.ShapeDtypeStruct((R, D), BF), jax.ShapeDtypeStruct((1, D), F32)],
        compiler_params=_params(("arbitrary",)),
    )(*args, *after)


def _a_chunk(us, vs, gvs, ws, bs):
    r_i = lax.broadcasted_iota(jnp.int32, (CHUNK, CHUNK), 0)
    c_i = lax.broadcasted_iota(jnp.int32, (CHUNK, CHUNK), 1)
    causal = r_i >= c_i
    vg = [jax.nn.gelu(v) for v in vs]
    ss = sum(jnp.sum(v * v, axis=-1, keepdims=True) for v in vg)
    r = lax.rsqrt(ss * (1.0 / A_WIDTH) + EPS)
    ys = []
    for g in range(A_GROUPS):
        vn = vg[g] * r * gvs[g]
        w = jnp.where(causal, ws[g], 0.0)
        s = jnp.dot(w.astype(BF), vn.astype(BF), preferred_element_type=F32) + bs[g]
        ys.append(jax.nn.gelu(us[g]) * s)
    return ys


def _a_split(u_ref, v_ref, g_ref, w_ref, b_ref):
    sl = [slice(g * 128, (g + 1) * 128) for g in range(A_GROUPS)]
    return ([u_ref[:, s] for s in sl], [v_ref[:, s] for s in sl], [g_ref[:, s] for s in sl],
            [w_ref[g] for g in range(A_GROUPS)], [b_ref[:, g:g + 1] for g in range(A_GROUPS)])


def _a_specs(S):
    return [pl.BlockSpec((CHUNK, A_WIDTH), lambda n: (n, 0)), pl.BlockSpec((CHUNK, A_WIDTH), lambda n: (n, 1)),
            pl.BlockSpec((1, A_WIDTH), lambda n: (0, 0)), pl.BlockSpec((A_GROUPS, CHUNK, CHUNK), lambda n: (0, 0, 0)),
            pl.BlockSpec((CHUNK, A_GROUPS), lambda n: (0, 0))]


def _a_fwd(proj, g_v, w_s, b_t):
    S = proj.shape[0]

    def body(u_ref, v_ref, g_ref, w_ref, b_ref, y_ref):
        ys = _a_chunk(*_a_split(u_ref, v_ref, g_ref, w_ref, b_ref))
        for g in range(A_GROUPS):
            y_ref[:, g * 128:(g + 1) * 128] = ys[g].astype(BF)

    return pl.pallas_call(
        body, name="a_fwd", grid=(S // CHUNK,), in_specs=_a_specs(S),
        out_specs=pl.BlockSpec((CHUNK, A_WIDTH), lambda n: (n, 0)),
        out_shape=jax.ShapeDtypeStruct((S, A_WIDTH), BF), compiler_params=_params(("parallel",)),
    )(proj, proj, g_v, w_s, b_t)


def _a_bwd(proj, g_v, w_s, b_t, dy, after=()):
    S = proj.shape[0]

    def body(u_ref, v_ref, g_ref, w_ref, b_ref, dy_ref, duv_ref, dg_ref, dw_ref, db_ref):
        n = pl.program_id(0)
        dys = [dy_ref[:, g * 128:(g + 1) * 128] for g in range(A_GROUPS)]
        _, vjp = jax.vjp(_a_chunk, *_a_split(u_ref, v_ref, g_ref, w_ref, b_ref))
        dus, dvs, dgs, dws, dbs = vjp(dys)

        @pl.when(n == 0)
        def _():
            dg_ref[...] = jnp.zeros_like(dg_ref)
            dw_ref[...] = jnp.zeros_like(dw_ref)
            db_ref[...] = jnp.zeros_like(db_ref)

        for g in range(A_GROUPS):
            duv_ref[:, g * 128:(g + 1) * 128] = dus[g].astype(BF)
            duv_ref[:, A_WIDTH + g * 128:A_WIDTH + (g + 1) * 128] = dvs[g].astype(BF)
            dg_ref[:, g * 128:(g + 1) * 128] += dgs[g]
            dw_ref[g] += dws[g]
            db_ref[:, g:g + 1] += dbs[g]

    return pl.pallas_call(
        _hide(body, 6, len(after)), name="a_bwd", grid=(S // CHUNK,),
        in_specs=_a_specs(S) + [pl.BlockSpec((CHUNK, A_WIDTH), lambda n: (n, 0))] + _hidden_specs(after),
        out_specs=[pl.BlockSpec((CHUNK, 2 * A_WIDTH), lambda n: (n, 0)), pl.BlockSpec((1, A_WIDTH), lambda n: (0, 0)),
                   pl.BlockSpec((A_GROUPS, CHUNK, CHUNK), lambda n: (0, 0, 0)), pl.BlockSpec((CHUNK, A_GROUPS), lambda n: (0, 0))],
        out_shape=[jax.ShapeDtypeStruct((S, 2 * A_WIDTH), BF), jax.ShapeDtypeStruct((1, A_WIDTH), F32),
                   jax.ShapeDtypeStruct((A_GROUPS, CHUNK, CHUNK), F32), jax.ShapeDtypeStruct((CHUNK, A_GROUPS), F32)],
        compiler_params=_params(("arbitrary",)),
    )(proj, proj, g_v, w_s, b_t, dy, *after)


def _half_mask(shape, which):
    lane = lax.broadcasted_iota(jnp.int32, shape, len(shape) - 1)
    return (lane >= 64) == (which == 1)


def _pair_norm_rope(x, g, ct, sa, sb):
    lo = _half_mask(x.shape, 0)
    x2 = x * x
    ss_lo = jnp.sum(jnp.where(lo, x2, 0.0), axis=-1, keepdims=True)
    ss_hi = jnp.sum(jnp.where(lo, 0.0, x2), axis=-1, keepdims=True)
    r = jnp.where(lo, lax.rsqrt(ss_lo * (1.0 / B_HEAD_DIM) + EPS), lax.rsqrt(ss_hi * (1.0 / B_HEAD_DIM) + EPS))
    xr = x * r
    xn = xr * g
    out = xn * ct + pltpu.roll(xn, 120, 1) * sa + pltpu.roll(xn, 8, 1) * sb
    return out, xr, r


def _pair_norm_rope_bwd(x, g, ct, sa, sb, dout):
    lo = _half_mask(x.shape, 0)
    _, xr, r = _pair_norm_rope(x, g, ct, sa, sb)
    dxn = dout * ct + pltpu.roll(dout * sa, 8, 1) + pltpu.roll(dout * sb, 120, 1)
    gy = dxn * g
    t = xr * gy
    c_lo = jnp.sum(jnp.where(lo, t, 0.0), axis=-1, keepdims=True)
    c_hi = jnp.sum(jnp.where(lo, 0.0, t), axis=-1, keepdims=True)
    c = jnp.where(lo, c_lo, c_hi)
    dx = r * (gy - xr * c * (1.0 / B_HEAD_DIM))
    dg = jnp.sum(dxn * xr, axis=0, keepdims=True)
    return dx, dg


def _b_pre(proj, gq2, gk2, ct, sa, sb):
    S = proj.shape[0]
    tr = _pick(S, (256,))
    n_pair = B_WIDTH // 128

    def body(q_ref, k_ref, gq_ref, gk_ref, ct_ref, sa_ref, sb_ref, qn_ref, kn_ref):
        ct_v, sa_v, sb_v = ct_ref[...], sa_ref[...], sb_ref[...]
        for p in range(n_pair):
            o, _, _ = _pair_norm_rope(q_ref[:, p * 128:(p + 1) * 128], gq_ref[...], ct_v, sa_v, sb_v)
            qn_ref[:, p * 128:(p + 1) * 128] = o.astype(BF)
        o, _, _ = _pair_norm_rope(k_ref[...], gk_ref[...], ct_v, sa_v, sb_v)
        kn_ref[...] = o.astype(BF)

    tab = pl.BlockSpec((tr, 128), lambda i: (i, 0))
    gsp = pl.BlockSpec((1, 128), lambda i: (0, 0))
    return pl.pallas_call(
        body, name="b_pre", grid=(S // tr,),
        in_specs=[pl.BlockSpec((tr, B_WIDTH), lambda i: (i, 1)), pl.BlockSpec((tr, 128), lambda i: (i, 2 * B_WIDTH // 128)),
                  gsp, gsp, tab, tab, tab],
        out_specs=[pl.BlockSpec((tr, B_WIDTH), lambda i: (i, 0)), tab],
        out_shape=[jax.ShapeDtypeStruct((S, B_WIDTH), BF), jax.ShapeDtypeStruct((S, 128), BF)],
        compiler_params=_params(("parallel",)),
    )(proj, proj, gq2, gk2, ct, sa, sb)


def _b_pre_bwd(proj, gq2, gk2, ct, sa, sb, dqn, dkn, dv):
    S = proj.shape[0]
    tr = _pick(S, (256,))
    n_pair = B_WIDTH // 128

    def body(q_ref, k_ref, gq_ref, gk_ref, ct_ref, sa_ref, sb_ref, dqn_ref, dkn_ref, dv_ref, dqkv_ref, dgq_ref, dgk_ref):
        i = pl.program_id(0)
        ct_v, sa_v, sb_v = ct_ref[...], sa_ref[...], sb_ref[...]
        dgq = jnp.zeros((1, 128), F32)
        for p in range(n_pair):
            sl = slice(p * 128, (p + 1) * 128)
            dx, dg = _pair_norm_rope_bwd(q_ref[:, sl], gq_ref[...], ct_v, sa_v, sb_v, dqn_ref[:, sl])
            dqkv_ref[:, sl] = dx.astype(BF)
            dgq = dgq + dg
        dx, dgk = _pair_norm_rope_bwd(k_ref[...], gk_ref[...], ct_v, sa_v, sb_v, dkn_ref[...])
        dqkv_ref[:, B_WIDTH:B_WIDTH + 128] = dx.astype(BF)
        dqkv_ref[:, B_WIDTH + 128:B_WIDTH + 256] = dv_ref[...].astype(BF)

        @pl.when(i == 0)
        def _():
            dgq_ref[...] = dgq
            dgk_ref[...] = dgk

        @pl.when(i > 0)
        def _():
            dgq_ref[...] += dgq
            dgk_ref[...] += dgk

    tab = pl.BlockSpec((tr, 128), lambda i: (i, 0))
    gsp = pl.BlockSpec((1, 128), lambda i: (0, 0))
    return pl.pallas_call(
        body, name="b_pre_bwd", grid=(S // tr,),
        in_specs=[pl.BlockSpec((tr, B_WIDTH), lambda i: (i, 1)), pl.BlockSpec((tr, 128), lambda i: (i, 2 * B_WIDTH // 128)),
                  gsp, gsp, tab, tab, tab, pl.BlockSpec((tr, B_WIDTH), lambda i: (i, 0)), tab, tab],
        out_specs=[pl.BlockSpec((tr, B_WIDTH + 256), lambda i: (i, 0)), gsp, gsp],
        out_shape=[jax.ShapeDtypeStruct((S, B_WIDTH + 256), BF), jax.ShapeDtypeStruct((1, 128), F32), jax.ShapeDtypeStruct((1, 128), F32)],
        compiler_params=_params(("arbitrary",)),
    )(proj, proj, gq2, gk2, ct, sa, sb, dqn, dkn, dv)


def _b_dup(x2, g):
    d = jnp.where(_half_mask(x2.shape, g), x2, 0.0)
    return (d + pltpu.roll(d, 64, 1)).astype(BF)


PAIRS_PER_GROUP = B_HEADS // B_KV_HEADS // 2
GROUP_ROWS = PAIRS_PER_GROUP * CHUNK


def _b_valid(n):
    row = lax.broadcasted_iota(jnp.int32, (GROUP_ROWS, 2 * CHUNK), 0) & (CHUNK - 1)
    col = lax.broadcasted_iota(jnp.int32, (GROUP_ROWS, 2 * CHUNK), 1)
    rel = row + CHUNK - col
    return (rel >= 0) & (rel < CHUNK) & ((col >= CHUNK) | (n > 0))


def _b_blocks(x2, g):
    xd = _b_dup(x2, g)
    lo = _half_mask(xd.shape, 0)
    zero = jnp.zeros_like(xd)
    return jnp.concatenate([jnp.where(lo, xd, zero), jnp.where(lo, zero, xd)], axis=0)


def _b_sink_col(s_ref, g, hf):
    rb = lax.broadcasted_iota(jnp.int32, (GROUP_ROWS, 1), 0) // CHUNK
    col = jnp.zeros((GROUP_ROWS, 1), F32)
    for pp in range(PAIRS_PER_GROUP):
        col = jnp.where(rb == pp, s_ref[0, 2 * (g * PAIRS_PER_GROUP + pp) + hf], col)
    return col


def _b_probs(qs, kblk, valid, sinks):
    s = lax.dot_general(qs, kblk, (((1,), (1,)), ((), ())), preferred_element_type=F32) * (B_HEAD_DIM ** -0.5)
    out = []
    for hf in range(2):
        sh = jnp.where(valid, s[:, hf * 2 * CHUNK:(hf + 1) * 2 * CHUNK], NEG)
        m = jnp.maximum(jnp.max(sh, axis=-1, keepdims=True), sinks[hf])
        e = jnp.exp(sh - m)
        es = jnp.exp(sinks[hf] - m)
        inv = 1.0 / (jnp.sum(e, axis=-1, keepdims=True) + es)
        out.append((e * inv, es * inv))
    return out


def _b_fold(acc, g):
    lo = _half_mask((2 * CHUNK, 128), 0)
    t = jnp.where(lo, acc[:2 * CHUNK], 0.0) + jnp.where(lo, 0.0, acc[2 * CHUNK:])
    return jnp.where(_half_mask((2 * CHUNK, 128), g), t + pltpu.roll(t, 64, 1), 0.0)


def _b_kv_specs(S):
    prev = lambda n: (jnp.maximum(n - 1, 0), 0)
    cur = lambda n: (n, 0)
    v_col = (2 * B_WIDTH + B_KV_WIDTH) // 128
    return [pl.BlockSpec((CHUNK, 128), prev), pl.BlockSpec((CHUNK, 128), cur),
            pl.BlockSpec((CHUNK, 128), lambda n: (jnp.maximum(n - 1, 0), v_col)), pl.BlockSpec((CHUNK, 128), lambda n: (n, v_col))]


def _b_attn_fwd(qn, kn, proj, sinks):
    S = qn.shape[0]

    def body(s_ref, q_ref, kp_ref, kc_ref, vp_ref, vc_ref, y_ref):
        n = pl.program_id(0)
        valid = _b_valid(n)
        k2 = jnp.concatenate([kp_ref[...], kc_ref[...]], axis=0).astype(F32)
        v2 = jnp.concatenate([vp_ref[...], vc_ref[...]], axis=0)
        for g in range(B_KV_HEADS):
            pairs = [g * PAIRS_PER_GROUP + pp for pp in range(PAIRS_PER_GROUP)]
            qs = jnp.concatenate([q_ref[:, p * 128:(p + 1) * 128] for p in pairs], axis=0)
            probs = _b_probs(qs, _b_blocks(k2, g), valid, [_b_sink_col(s_ref, g, hf) for hf in range(2)])
            pcat = jnp.concatenate([probs[0][0].astype(BF), probs[1][0].astype(BF)], axis=1)
            o = jnp.dot(pcat, _b_blocks(v2, g), preferred_element_type=F32)
            for pp, p in enumerate(pairs):
                y_ref[:, p * 128:(p + 1) * 128] = o[pp * CHUNK:(pp + 1) * CHUNK].astype(BF)

    return pl.pallas_call(
        body, name="b_attn_fwd", grid=(S // CHUNK,),
        in_specs=[pl.BlockSpec(memory_space=pltpu.SMEM), pl.BlockSpec((CHUNK, B_WIDTH), lambda n: (n, 0))] + _b_kv_specs(S),
        out_specs=pl.BlockSpec((CHUNK, B_WIDTH), lambda n: (n, 0)),
        out_shape=jax.ShapeDtypeStruct((S, B_WIDTH), BF), compiler_params=_params(("arbitrary",)),
    )(sinks, qn, kn, kn, proj, proj)


def _b_attn_bwd(qn, kn, proj, sinks, dy, after=()):
    S = qn.shape[0]

    def body(s_ref, q_ref, kp_ref, kc_ref, vp_ref, vc_ref, dy_ref, dq_ref, dk_ref, dv_ref, ds_ref):
        n = pl.program_id(0)

        @pl.when(n == 0)
        def _():
            dk_ref[...] = jnp.zeros_like(dk_ref)
            dv_ref[...] = jnp.zeros_like(dv_ref)
            ds_ref[...] = jnp.zeros_like(ds_ref)

        valid = _b_valid(n)
        k2 = jnp.concatenate([kp_ref[...], kc_ref[...]], axis=0).astype(F32)
        v2 = jnp.concatenate([vp_ref[...], vc_ref[...]], axis=0)
        lane = lax.broadcasted_iota(jnp.int32, (CHUNK, 128), 1)
        dk2 = jnp.zeros((2 * CHUNK, 128), F32)
        dv2 = jnp.zeros((2 * CHUNK, 128), F32)
        dsink = jnp.zeros((CHUNK, 128), F32)
        scale = B_HEAD_DIM ** -0.5
        nt = (((1,), (1,)), ((), ()))
        tn = (((0,), (0,)), ((), ()))
        for g in range(B_KV_HEADS):
            pairs = [g * PAIRS_PER_GROUP + pp for pp in range(PAIRS_PER_GROUP)]
            qs = jnp.concatenate([q_ref[:, p * 128:(p + 1) * 128] for p in pairs], axis=0)
            do = jnp.concatenate([dy_ref[:, p * 128:(p + 1) * 128] for p in pairs], axis=0)
            do_b = do.astype(BF)
            kblk, vblk = _b_blocks(k2, g), _b_blocks(v2, g)
            probs = _b_probs(qs, kblk, valid, [_b_sink_col(s_ref, g, hf) for hf in range(2)])
            pcat = jnp.concatenate([probs[0][0].astype(BF), probs[1][0].astype(BF)], axis=1)
            o = jnp.dot(pcat, vblk, preferred_element_type=F32)
            dp = lax.dot_general(do_b, vblk, nt, preferred_element_type=F32)
            prod = do * o
            ds_halves = []
            for hf in range(2):
                pr, ps = probs[hf]
                delta = jnp.sum(jnp.where(_half_mask(prod.shape, hf), prod, 0.0), axis=-1, keepdims=True)
                ds_halves.append((pr * (dp[:, hf * 2 * CHUNK:(hf + 1) * 2 * CHUNK] - delta) * scale).astype(BF))
                t = -ps * delta
                for pp, p in enumerate(pairs):
                    dsink = dsink + jnp.where(lane == 2 * p + hf, t[pp * CHUNK:(pp + 1) * CHUNK], 0.0)
            dsc = jnp.concatenate(ds_halves, axis=1)
            dq = jnp.dot(dsc, kblk, preferred_element_type=F32)
            for pp, p in enumerate(pairs):
                dq_ref[:, p * 128:(p + 1) * 128] = dq[pp * CHUNK:(pp + 1) * CHUNK]
            dk2 = dk2 + _b_fold(lax.dot_general(dsc, qs, tn, preferred_element_type=F32), g)
            dv2 = dv2 + _b_fold(lax.dot_general(pcat, do_b, tn, preferred_element_type=F32), g)
        ds_ref[...] += dsink
        cur = pl.ds(pl.multiple_of(n * CHUNK, CHUNK), CHUNK)
        dk_ref[cur, :] += dk2[CHUNK:]
        dv_ref[cur, :] += dv2[CHUNK:]

        @pl.when(n > 0)
        def _():
            prv = pl.ds(pl.multiple_of((n - 1) * CHUNK, CHUNK), CHUNK)
            dk_ref[prv, :] += dk2[:CHUNK]
            dv_ref[prv, :] += dv2[:CHUNK]

    full = pl.BlockSpec((S, 128), lambda n: (0, 0))
    return pl.pallas_call(
        _hide(body, 7, len(after)), name="b_attn_bwd", grid=(S // CHUNK,),
        in_specs=[pl.BlockSpec(memory_space=pltpu.SMEM), pl.BlockSpec((CHUNK, B_WIDTH), lambda n: (n, 0))] + _b_kv_specs(S)
        + [pl.BlockSpec((CHUNK, B_WIDTH), lambda n: (n, 0))] + _hidden_specs(after),
        out_specs=[pl.BlockSpec((CHUNK, B_WIDTH), lambda n: (n, 0)), full, full, pl.BlockSpec((CHUNK, 128), lambda n: (0, 0))],
        out_shape=[jax.ShapeDtypeStruct((S, B_WIDTH), F32), jax.ShapeDtypeStruct((S, 128), F32), jax.ShapeDtypeStruct((S, 128), F32),
                   jax.ShapeDtypeStruct((CHUNK, 128), F32)],
        compiler_params=_params(("arbitrary",)),
    )(sinks, qn, kn, kn, proj, proj, dy, *after)


def _c_block(q, k, v, gq, gk):
    qn = q * lax.rsqrt(jnp.mean(q * q, axis=-1, keepdims=True) + EPS) * gq
    kn = k * lax.rsqrt(jnp.mean(k * k, axis=-1, keepdims=True) + EPS) * gk
    s = lax.dot_general(qn.astype(BF), kn.astype(BF), (((1,), (1,)), ((), ())), preferred_element_type=F32) * (C_HEAD_DIM ** -0.5)
    p = jax.nn.softmax(s, axis=-1)
    return jnp.dot(p.astype(BF), v.astype(BF), preferred_element_type=F32)


def _c_specs(S, M, tq):
    q_col = (2 * A_WIDTH + B_WIDTH + 2 * B_KV_WIDTH) // 128
    return [pl.BlockSpec((tq, 128), lambda h, i: (i, q_col + h)), pl.BlockSpec((M, 128), lambda h, i: (0, h)),
            pl.BlockSpec((M, 128), lambda h, i: (0, C_HEADS + h)), pl.BlockSpec((1, 128), lambda h, i: (0, 0)),
            pl.BlockSpec((1, 128), lambda h, i: (0, 0))]


def _c_fwd(proj, kv, gq, gk):
    S, M = proj.shape[0], kv.shape[0]
    tq = _pick(S, (512,))

    def body(q_ref, k_ref, v_ref, gq_ref, gk_ref, y_ref):
        y_ref[...] = _c_block(q_ref[...], k_ref[...], v_ref[...], gq_ref[...], gk_ref[...]).astype(BF)

    return pl.pallas_call(
        body, name="c_fwd", grid=(C_HEADS, S // tq), in_specs=_c_specs(S, M, tq),
        out_specs=pl.BlockSpec((tq, 128), lambda h, i: (i, h)),
        out_shape=jax.ShapeDtypeStruct((S, C_WIDTH), BF), compiler_params=_params(("parallel", "parallel")),
    )(proj, kv, kv, gq, gk)


def _c_bwd(proj, kv, gq, gk, dy):
    S, M = proj.shape[0], kv.shape[0]
    tq = _pick(S, (512,))

    def body(q_ref, k_ref, v_ref, gq_ref, gk_ref, dy_ref, dq_ref, dk_ref, dv_ref, dgq_ref, dgk_ref):
        i = pl.program_id(1)
        _, vjp = jax.vjp(_c_block, q_ref[...], k_ref[...], v_ref[...], gq_ref[...], gk_ref[...])
        dq, dk, dv, dgq, dgk = vjp(dy_ref[...])
        dq_ref[...] = dq.astype(BF)

        @pl.when(i == 0)
        def _():
            dk_ref[...] = dk
            dv_ref[...] = dv
            dgq_ref[...] = dgq
            dgk_ref[...] = dgk

        @pl.when(i > 0)
        def _():
            dk_ref[...] += dk
            dv_ref[...] += dv
            dgq_ref[...] += dgq
            dgk_ref[...] += dgk

    return pl.pallas_call(
        body, name="c_bwd", grid=(C_HEADS, S // tq),
        in_specs=_c_specs(S, M, tq) + [pl.BlockSpec((tq, 128), lambda h, i: (i, h))],
        out_specs=[pl.BlockSpec((tq, 128), lambda h, i: (i, h)), pl.BlockSpec((M, 128), lambda h, i: (0, h)),
                   pl.BlockSpec((M, 128), lambda h, i: (0, h)), pl.BlockSpec((None, 1, 128), lambda h, i: (h, 0, 0)),
                   pl.BlockSpec((None, 1, 128), lambda h, i: (h, 0, 0))],
        out_shape=[jax.ShapeDtypeStruct((S, C_WIDTH), BF), jax.ShapeDtypeStruct((M, C_WIDTH), F32), jax.ShapeDtypeStruct((M, C_WIDTH), F32),
                   jax.ShapeDtypeStruct((C_HEADS, 1, 128), F32), jax.ShapeDtypeStruct((C_HEADS, 1, 128), F32)],
        compiler_params=_params(("parallel", "arbitrary")),
    )(proj, kv, kv, gq, gk, dy)


def _merge_specs(S, D, tm, tn, ks):
    off = GATE_OFF // tn
    nd = D // tn
    gates = [pl.BlockSpec((tm, tn), functools.partial(lambda b, m, n: (m, off + b * nd + n), b)) for b in range(3)]
    ys = [pl.BlockSpec((tm, k), lambda m, n: (m, 0)) for k in ks]
    ws = [pl.BlockSpec((None, k, tn), lambda m, n: (n, 0, 0)) for k in ks]
    return gates, ys, ws


def _merge_fwd(proj, ys, ws):
    S = proj.shape[0]
    tn = ws[0].shape[2]
    D = N_DEV * tn
    ks = [w.shape[1] for w in ws]
    tm = _pick(S, (1024,))
    gates, y_specs, w_specs = _merge_specs(S, D, tm, tn, ks)

    def body(ga_ref, gb_ref, gc_ref, ya_ref, yb_ref, yc_ref, wa_ref, wb_ref, wc_ref, m_ref, za_ref, zb_ref, zc_ref):
        acc = None
        for g_ref, y_ref, w_ref, z_ref in ((ga_ref, ya_ref, wa_ref, za_ref), (gb_ref, yb_ref, wb_ref, zb_ref),
                                           (gc_ref, yc_ref, wc_ref, zc_ref)):
            z = jnp.dot(y_ref[...], w_ref[...], preferred_element_type=F32)
            z_ref[...] = z.astype(BF)
            t = jax.nn.sigmoid(g_ref[...]) * z
            acc = t if acc is None else acc + t
        m_ref[...] = acc.astype(BF)

    tile = pl.BlockSpec((tm, tn), lambda m, n: (m, n))
    return pl.pallas_call(
        body, name="merge_fwd", grid=(S // tm, D // tn), in_specs=gates + y_specs + w_specs,
        out_specs=[tile, tile, tile, tile], out_shape=[jax.ShapeDtypeStruct((S, D), BF)] * 4,
        compiler_params=_params(("parallel", "parallel")),
    )(proj, proj, proj, *ys, *ws)


def _merge_bwd(proj, zs, dm, ws, after=()):
    S = proj.shape[0]
    tn = ws[0].shape[2]
    D = N_DEV * tn
    ks = [w.shape[1] for w in ws]
    tm = _pick(S, (1024,))
    gates, _, w_specs = _merge_specs(S, D, tm, tn, ks)
    nt = (((1,), (1,)), ((), ()))

    def body(ga_ref, gb_ref, gc_ref, za_ref, zb_ref, zc_ref, dm_ref, wa_ref, wb_ref, wc_ref,
             dza_ref, dzb_ref, dzc_ref, dga_ref, dgb_ref, dgc_ref, dya_ref, dyb_ref, dyc_ref):
        n = pl.program_id(1)
        dmv = dm_ref[...]
        for g_ref, z_ref, w_ref, dz_ref, dg_ref, dy_ref in (
                (ga_ref, za_ref, wa_ref, dza_ref, dga_ref, dya_ref), (gb_ref, zb_ref, wb_ref, dzb_ref, dgb_ref, dyb_ref),
                (gc_ref, zc_ref, wc_ref, dzc_ref, dgc_ref, dyc_ref)):
            sg = jax.nn.sigmoid(g_ref[...])
            dz = (sg * dmv).astype(BF)
            dz_ref[...] = dz
            dg_ref[...] = (dmv * z_ref[...].astype(F32) * sg * (1.0 - sg)).astype(BF)
            part = lax.dot_general(dz, w_ref[...], nt, preferred_element_type=F32)

            @pl.when(n == 0)
            def _():
                dy_ref[...] = part

            @pl.when(n > 0)
            def _():
                dy_ref[...] += part

    tile = pl.BlockSpec((tm, tn), lambda m, n: (m, n))
    dys = [pl.BlockSpec((tm, k), lambda m, n: (m, 0)) for k in ks]
    return pl.pallas_call(
        _hide(body, 10, len(after)), name="merge_bwd", grid=(S // tm, D // tn),
        in_specs=gates + [tile, tile, tile, tile] + w_specs + _hidden_specs(after),
        out_specs=[tile] * 6 + dys,
        out_shape=[jax.ShapeDtypeStruct((S, D), BF)] * 6 + [jax.ShapeDtypeStruct((S, k), F32) for k in ks],
        compiler_params=_params(("parallel", "arbitrary")),
    )(proj, proj, proj, *zs, dm, *ws, *after)


PAD = 8


def _stage_shift_down(us_ref, u_ref):
    S = u_ref.shape[1]
    us_ref[:, 0:PAD, :] = jnp.zeros((2, PAD, us_ref.shape[2]), F32)
    us_ref[:, PAD:S + PAD, :] = u_ref[...].astype(F32)


ROWS = 64


def _conv3(us_ref, part, r0, w, b):
    return (us_ref[part, pl.ds(r0 + PAD, ROWS), :] * w[2:3] + us_ref[part, pl.ds(r0 + PAD - 1, ROWS), :] * w[1:2]
            + us_ref[part, pl.ds(r0 + PAD - 2, ROWS), :] * w[0:1] + b)


def _ffn_specs(S, F, tc, c):
    per = c // tc

    def w_spec(half):
        return pl.BlockSpec((None, 3, tc), lambda j: (half * (N_DEV // 2) + j // per, 0, j % per))

    return [pl.BlockSpec((2, S, tc), lambda j: (0, 0, j)), w_spec(0), w_spec(1), pl.BlockSpec((2, 1, tc), lambda j: (0, 0, j))]


def _ffn_tile(F, c):
    tc = 128
    if c % tc or F % tc:
        raise ValueError(f"ffn tile {tc} does not divide {c}, {F}")
    return tc


def _ffn_act_fwd(up3, cws, cb3):
    _, S, F = up3.shape
    c = cws.shape[2]
    tc = _ffn_tile(F, c)

    def body(u_ref, wa_ref, wb_ref, b_ref, o_ref, us_ref):
        _stage_shift_down(us_ref, u_ref)
        wa, wb, ba, bb = wa_ref[...], wb_ref[...], b_ref[0], b_ref[1]

        def step(i, carry):
            r0 = pl.multiple_of(i * ROWS, ROWS)
            ca = _conv3(us_ref, 0, r0, wa, ba)
            cb = _conv3(us_ref, 1, r0, wb, bb)
            o_ref[pl.ds(r0, ROWS), :] = (ca * jax.nn.sigmoid(ca) * cb).astype(BF)
            return carry

        lax.fori_loop(0, S // ROWS, step, 0, unroll=4)

    return pl.pallas_call(
        body, name="ffn_act_fwd", grid=(F // tc,), in_specs=_ffn_specs(S, F, tc, c),
        out_specs=pl.BlockSpec((S, tc), lambda j: (0, j)), out_shape=jax.ShapeDtypeStruct((S, F), BF),
        scratch_shapes=[pltpu.VMEM((2, S + PAD, tc), F32)],
        compiler_params=_params(("parallel",)),
    )(up3, cws, cws, cb3)


def _ffn_act_bwd(up3, cws, cb3, dact, after=()):
    _, S, F = up3.shape
    c = cws.shape[2]
    tc = _ffn_tile(F, c)

    def body(u_ref, wa_ref, wb_ref, b_ref, da_ref, du_ref, dw_ref, db_ref, us_ref, dcs_ref):
        _stage_shift_down(us_ref, u_ref)
        ws = (wa_ref[...], wb_ref[...])
        ba, bb = b_ref[0], b_ref[1]
        dcs_ref[:, S:S + PAD, :] = jnp.zeros((2, PAD, tc), F32)

        def conv_grads(i, carry):
            r0 = pl.multiple_of(i * ROWS, ROWS)
            ca = _conv3(us_ref, 0, r0, ws[0], ba)
            cb = _conv3(us_ref, 1, r0, ws[1], bb)
            sg = jax.nn.sigmoid(ca)
            dav = da_ref[pl.ds(r0, ROWS), :].astype(F32)
            dcs_ref[0, pl.ds(r0, ROWS), :] = dav * cb * sg * (1.0 + ca * (1.0 - sg))
            dcs_ref[1, pl.ds(r0, ROWS), :] = dav * ca * sg
            return carry

        lax.fori_loop(0, S // ROWS, conv_grads, 0, unroll=4)

        def fold(v):
            return jnp.sum(v.reshape(ROWS // 8, 8, tc), axis=0)

        def input_grads(i, acc):
            r0 = pl.multiple_of(i * ROWS, ROWS)
            new = []
            for part in range(2):
                w = ws[part]
                dc = dcs_ref[part, pl.ds(r0, ROWS), :]
                dc1 = dcs_ref[part, pl.ds(r0 + 1, ROWS), :]
                dc2 = dcs_ref[part, pl.ds(r0 + 2, ROWS), :]
                u = us_ref[part, pl.ds(r0 + PAD, ROWS), :]
                du_ref[part, pl.ds(r0, ROWS), :] = (dc * w[2:3] + dc1 * w[1:2] + dc2 * w[0:1]).astype(BF)
                sums = (fold(dc2 * u), fold(dc1 * u), fold(dc * u), fold(dc))
                new += [a + s for a, s in zip(acc[4 * part:4 * part + 4], sums)]
            return tuple(new)

        acc = lax.fori_loop(0, S // ROWS, input_grads, tuple(jnp.zeros((8, tc), F32) for _ in range(8)), unroll=4)
        for part in range(2):
            for j in range(3):
                dw_ref[part, j:j + 1, :] = jnp.sum(acc[4 * part + j], axis=0, keepdims=True)
            db_ref[part] = jnp.sum(acc[4 * part + 3], axis=0, keepdims=True)

    return pl.pallas_call(
        _hide(body, 5, len(after)), name="ffn_act_bwd", grid=(F // tc,),
        in_specs=_ffn_specs(S, F, tc, c) + [pl.BlockSpec((S, tc), lambda j: (0, j))] + _hidden_specs(after),
        out_specs=[pl.BlockSpec((2, S, tc), lambda j: (0, 0, j)), pl.BlockSpec((2, 3, tc), lambda j: (0, 0, j)),
                   pl.BlockSpec((2, 1, tc), lambda j: (0, 0, j))],
        out_shape=[jax.ShapeDtypeStruct((2, S, F), BF), jax.ShapeDtypeStruct((2, 3, F), F32), jax.ShapeDtypeStruct((2, 1, F), F32)],
        scratch_shapes=[pltpu.VMEM((2, S + PAD, tc), F32), pltpu.VMEM((2, S + PAD, tc), F32)],
        compiler_params=_params(("parallel",)),
    )(up3, cws, cws, cb3, dact, *after)


def _residual_rms(a, w, x, g, name, tm=512):
    S, K = a.shape
    D = w.shape[1]
    tm = _pick(S, (tm,))

    def body(a_ref, w_ref, x_ref, g_ref, x1_ref, h_ref, r_ref):
        x1 = jnp.dot(a_ref[...], w_ref[...], preferred_element_type=F32) + x_ref[...]
        r = lax.rsqrt(jnp.mean(x1 * x1, axis=-1, keepdims=True) + EPS)
        x1_ref[...] = x1
        h_ref[...] = (x1 * r * g_ref[...]).astype(BF)
        r_ref[...] = r

    row = pl.BlockSpec((tm, D), lambda i: (i, 0))
    return pl.pallas_call(
        body, name=name, grid=(S // tm,),
        in_specs=[pl.BlockSpec((tm, K), lambda i: (i, 0)), pl.BlockSpec((K, D), lambda i: (0, 0)), row, pl.BlockSpec((1, D), lambda i: (0, 0))],
        out_specs=[row, row, pl.BlockSpec((tm, 1), lambda i: (i, 0))],
        out_shape=[jax.ShapeDtypeStruct((S, D), F32), jax.ShapeDtypeStruct((S, D), BF), jax.ShapeDtypeStruct((S, 1), F32)],
        compiler_params=_params(("parallel",)),
    )(a, w, x, g)


def _out_loss(act, w_down, x1, target, tm=512, tn=1024, tk=1408):
    S, F = act.shape
    D = w_down.shape[1]
    tm, tn, tk = _pick(S, (tm,)), _pick(D, (tn,)), _pick(F, (tk,))
    nm, nn, nk = S // tm, D // tn, F // tk

    def body(a_ref, b_ref, x_ref, t_ref, dy_ref, dyb_ref, l_ref, acc):
        m, n, k = pl.program_id(0), pl.program_id(1), pl.program_id(2)

        @pl.when((m == 0) & (n == 0) & (k == 0))
        def _():
            l_ref[...] = jnp.zeros_like(l_ref)

        @pl.when(k == 0)
        def _():
            acc[...] = jnp.zeros_like(acc)

        acc[...] += jnp.dot(a_ref[...], b_ref[...], preferred_element_type=F32)

        @pl.when(k == nk - 1)
        def _():
            e = acc[...] + x_ref[...] - t_ref[...]
            dy = e * (1.0 / D)
            dy_ref[...] = dy
            dyb_ref[...] = dy.astype(BF)
            l_ref[...] += jnp.sum(jnp.sum(e * e, axis=-1, keepdims=True), axis=0, keepdims=True) * (0.5 / D)

    tile = pl.BlockSpec((tm, tn), lambda m, n, k: (m, n))
    return pl.pallas_call(
        body, name="mm_y_loss", grid=(nm, nn, nk),
        in_specs=[pl.BlockSpec((tm, tk), lambda m, n, k: (m, k)), pl.BlockSpec((tk, tn), lambda m, n, k: (k, n)), tile, tile],
        out_specs=[tile, tile, pl.BlockSpec((8, 128), lambda m, n, k: (0, 0))],
        out_shape=[jax.ShapeDtypeStruct((S, D), F32), jax.ShapeDtypeStruct((S, D), BF), jax.ShapeDtypeStruct((8, 128), F32)],
        scratch_shapes=[pltpu.VMEM((tm, tn), F32)],
        compiler_params=_params(("arbitrary", "arbitrary", "arbitrary")),
    )(act, w_down, x1, target)


def _allgather(shards, name):
    n = len(shards)

    def body(*refs):
        ins, outs = refs[:n], refs[n:2 * n]
        send_sems, recv_sems, local_sems = refs[2 * n:]
        x, y, c = lax.axis_index("x"), lax.axis_index("y"), lax.axis_index("c")
        me, sibling = (x, y, c), (x, y, 1 - c)
        chips = [(1 - x, y), (x, 1 - y), (1 - x, 1 - y)]

        def blk(w, px, py, pc):
            return outs[w].at[4 * px + 2 * py + pc]

        def copy(w, k, block, to, src=None):
            return pltpu.make_async_remote_copy(
                src_ref=blk(w, *block) if src is None else src, dst_ref=blk(w, *block),
                send_sem=send_sems.at[w, k], recv_sem=recv_sems.at[w, k], device_id=to, device_id_type=MESH)

        started = []
        mine = []
        for w in range(n):
            mine.append(pltpu.make_async_copy(ins[w], blk(w, *me), local_sems.at[w]))
            mine[-1].start()
            first = [copy(w, 0, me, sibling, src=ins[w])]
            first += [copy(w, 1 + j, me, (*chip, c), src=ins[w]) for j, chip in enumerate(chips)]
            for cp in first:
                cp.start()
            started += first
        for w in range(n):
            for j, chip in enumerate(chips):
                copy(w, 1 + j, (*chip, c), me).wait_recv()
                fwd = copy(w, 4 + j, (*chip, c), sibling)
                fwd.start()
                started.append(fwd)
        for w in range(n):
            copy(w, 0, sibling, me).wait_recv()
            for j, chip in enumerate(chips):
                copy(w, 4 + j, (*chip, 1 - c), me).wait_recv()
        for cp in started:
            cp.wait_send()
        for cp in mine:
            cp.wait()

    whole = pl.BlockSpec(memory_space=pltpu.VMEM)
    outs = pl.pallas_call(
        body, name=name, in_specs=[whole] * n, out_specs=[whole] * n,
        out_shape=[jax.ShapeDtypeStruct((N_DEV,) + s.shape, s.dtype) for s in shards],
        scratch_shapes=[pltpu.SemaphoreType.DMA((n, 7)), pltpu.SemaphoreType.DMA((n, 7)), pltpu.SemaphoreType.DMA((n,))],
    )(*shards)
    return list(outs)


def _allgather_seq(shards, name, collective_id, after=()):
    n = len(shards)
    n_after = len(after)

    halves = [s.shape[0] % 32 == 0 for s in shards]
    n_sem = 8
    to_diagonal = not all(halves)

    def body(*refs):
        ins, outs = refs[:n], refs[n + n_after:2 * n + n_after]
        send_sems, recv_sems, local_sems = refs[2 * n + n_after:]
        x, y, c = lax.axis_index("x"), lax.axis_index("y"), lax.axis_index("c")
        me, sibling = (x, y, c), (x, y, 1 - c)
        x_nb, y_nb, diag = (1 - x, y, c), (x, 1 - y, c), (1 - x, 1 - y, c)
        peers = [sibling, x_nb, y_nb] + ([diag] if to_diagonal else [])
        barrier = pltpu.get_barrier_semaphore()
        for peer in peers:
            pl.semaphore_signal(barrier, inc=1, device_id=peer, device_id_type=MESH)
        pl.semaphore_wait(barrier, len(peers))

        def blk(w, dev, rows=None):
            ref = outs[w].at[4 * dev[0] + 2 * dev[1] + dev[2]]
            return ref if rows is None else ref.at[rows]

        def copy(w, k, block, to, src=None, rows=None):
            return pltpu.make_async_remote_copy(
                src_ref=blk(w, block, rows) if src is None else src, dst_ref=blk(w, block, rows),
                send_sem=send_sems.at[n_sem * w + k], recv_sem=recv_sems.at[n_sem * w + k], device_id=to, device_id_type=MESH)

        def top(w):
            return pl.ds(0, shards[w].shape[0] // 2)

        def bottom(w):
            return pl.ds(shards[w].shape[0] // 2, shards[w].shape[0] // 2)

        started = []
        mine = []
        for w in range(n):
            mine.append(pltpu.make_async_copy(ins[w], blk(w, me), local_sems.at[w]))
            mine[-1].start()
            first = [copy(w, 0, me, sibling, src=ins[w]), copy(w, 1, me, x_nb, src=ins[w]), copy(w, 2, me, y_nb, src=ins[w])]
            if not halves[w]:
                first.append(copy(w, 3, me, diag, src=ins[w]))
            for cp in first:
                cp.start()
            started += first
        for w in range(n):
            copy(w, 1, x_nb, me).wait_recv()
            onward = [copy(w, 5, x_nb, sibling)] + ([copy(w, 3, x_nb, y_nb, rows=top(w))] if halves[w] else [])
            copy(w, 2, y_nb, me).wait_recv()
            onward += [copy(w, 6, y_nb, sibling)] + ([copy(w, 4, y_nb, x_nb, rows=bottom(w))] if halves[w] else [])
            for cp in onward:
                cp.start()
            started += onward
        for w in range(n):
            if halves[w]:
                copy(w, 3, diag, me, rows=top(w)).wait_recv()
                copy(w, 4, diag, me, rows=bottom(w)).wait_recv()
            else:
                copy(w, 3, diag, me).wait_recv()
            fwd = copy(w, 7, diag, sibling)
            fwd.start()
            started.append(fwd)
        for w in range(n):
            for k, dev in ((0, sibling), (5, (1 - x, y, 1 - c)), (6, (x, 1 - y, 1 - c)), (7, (1 - x, 1 - y, 1 - c))):
                copy(w, k, dev, me).wait_recv()
        for cp in started:
            cp.wait_send()
        for cp in mine:
            cp.wait()

    outs = pl.kernel(
        body, name=name, out_type=[jax.ShapeDtypeStruct((N_DEV,) + s.shape, s.dtype) for s in shards],
        mesh=plsc.ScalarSubcoreMesh(axis_name="seq", num_cores=1),
        scratch_types=[pltpu.SemaphoreType.DMA((n_sem * n,)), pltpu.SemaphoreType.DMA((n_sem * n,)), pltpu.SemaphoreType.DMA((n,))],
        compiler_params=pltpu.CompilerParams(collective_id=collective_id),
    )(*shards, *after)
    return list(outs)


def _chip_exchange(sums, name, collective_id):
    n = len(sums)

    def body(*refs):
        ins, outs = refs[:n], refs[n:2 * n]
        send_sems, recv_sems = refs[2 * n:]
        x, y, c = lax.axis_index("x"), lax.axis_index("y"), lax.axis_index("c")
        chips = [(1 - x, y), (x, 1 - y), (1 - x, 1 - y)]
        barrier = pltpu.get_barrier_semaphore()
        for px, py in chips:
            pl.semaphore_signal(barrier, inc=1, device_id=(px, py, c), device_id_type=MESH)
        pl.semaphore_wait(barrier, 3)
        copies = []
        for w in range(n):
            for k, (px, py) in enumerate(chips):
                copies.append(pltpu.make_async_remote_copy(
                    src_ref=ins[w].at[2 * px + py], dst_ref=outs[w].at[k], send_sem=send_sems.at[3 * w + k],
                    recv_sem=recv_sems.at[3 * w + k], device_id=(px, py, c), device_id_type=MESH))
        for cp in copies:
            cp.start()
        for cp in copies:
            cp.wait()

    outs = pl.kernel(
        body, name=name, out_type=[jax.ShapeDtypeStruct((3,) + s.shape[1:], s.dtype) for s in sums],
        mesh=plsc.ScalarSubcoreMesh(axis_name="seq", num_cores=1),
        scratch_types=[pltpu.SemaphoreType.DMA((3 * n,)), pltpu.SemaphoreType.DMA((3 * n,))],
        compiler_params=pltpu.CompilerParams(collective_id=collective_id),
    )(*sums)
    return list(outs)


def _row_tile(r, c, elems=256 * 1024):
    want = max(8, elems // c)
    for t in range(min(want, r) // 8 * 8, 0, -8):
        if r % t == 0:
            return t
    return r


def _pair_add(g4, recv, core, name, after=()):
    _, _, r, c = g4.shape
    tr = _row_tile(r, c, 1024 * 1024)

    def body(core_ref, a_ref, b_ref, o_ref):
        o_ref[...] = (a_ref[...].astype(F32) + b_ref[...].astype(F32)).astype(BF)

    return pl.pallas_call(
        _hide(body, 3, len(after)), name=name,
        grid_spec=pltpu.PrefetchScalarGridSpec(
            num_scalar_prefetch=1, grid=(4, r // tr),
            in_specs=[pl.BlockSpec((None, None, tr, c), lambda p, i, s: (p, s[0], i, 0)),
                      pl.BlockSpec((None, tr, c), lambda p, i, s: (p, i, 0))] + _hidden_specs(after),
            out_specs=pl.BlockSpec((None, tr, c), lambda p, i, s: (p, i, 0))),
        out_shape=jax.ShapeDtypeStruct((4, r, c), BF), compiler_params=_params(("parallel", "parallel")),
    )(core, g4, recv, *after)


def _adam_math(w, g, m, v):
    m = ADAM_B1 * m + (1.0 - ADAM_B1) * g
    v = ADAM_B2 * v + (1.0 - ADAM_B2) * (g * g)
    m_hat = m / (1.0 - ADAM_B1 ** ADAM_STEP)
    v_hat = v / (1.0 - ADAM_B2 ** ADAM_STEP)
    delta = -ADAM_LR * (m_hat / (jnp.sqrt(v_hat) + ADAM_EPS) + ADAM_WD * w)
    return delta, m, v


def _adamw_big(sums, recv, chip, w, m, v, name, after=()):
    r, c = w.shape
    tr = _row_tile(r, c, 512 * 1024)

    def body(chip_ref, s_ref, r_ref, w_ref, m_ref, v_ref, g_out, d_out, m_out, v_out):
        g = s_ref[...].astype(F32) + r_ref[0].astype(F32)
        g = g + r_ref[1].astype(F32)
        g = g + r_ref[2].astype(F32)
        delta, mn, vn = _adam_math(w_ref[...], g, m_ref[...], v_ref[...])
        g_out[...] = g
        d_out[...] = delta
        m_out[...] = mn
        v_out[...] = vn

    row = pl.BlockSpec((tr, c), lambda i, s: (i, 0))
    return pl.pallas_call(
        _hide(body, 6, len(after)), name=name,
        grid_spec=pltpu.PrefetchScalarGridSpec(
            num_scalar_prefetch=1, grid=(r // tr,),
            in_specs=[pl.BlockSpec((None, tr, c), lambda i, s: (s[0], i, 0)), pl.BlockSpec((3, tr, c), lambda i, s: (0, i, 0)),
                      row, row, row] + _hidden_specs(after),
            out_specs=[row, row, row, row]),
        out_shape=[jax.ShapeDtypeStruct((r, c), F32)] * 4, compiler_params=_params(("parallel",)),
    )(chip, sums, recv, w, m, v, *after)


def _adamw_small(parts, ws, ms, vs, extra_parts, name):
    n, ne = len(ws), len(extra_parts)

    def total(p_ref):
        g = p_ref[0]
        for d in range(1, N_DEV):
            g = g + p_ref[d]
        return g

    def body(*refs):
        p_refs, w_refs, m_refs, v_refs = refs[:n], refs[n:2 * n], refs[2 * n:3 * n], refs[3 * n:4 * n]
        e_refs = refs[4 * n:4 * n + ne]
        outs = refs[4 * n + ne:]
        for i in range(n):
            g = total(p_refs[i])
            delta, mn, vn = _adam_math(w_refs[i][...], g, m_refs[i][...], v_refs[i][...])
            outs[4 * i][...] = g
            outs[4 * i + 1][...] = delta
            outs[4 * i + 2][...] = mn
            outs[4 * i + 3][...] = vn
        for i in range(ne):
            outs[4 * n + i][...] = total(e_refs[i])

    out_shape = []
    for w in ws:
        out_shape += [jax.ShapeDtypeStruct(w.shape, F32)] * 4
    out_shape += [jax.ShapeDtypeStruct(e.shape[1:], F32) for e in extra_parts]
    res = pl.pallas_call(body, name=name, out_shape=out_shape,
                         compiler_params=pltpu.CompilerParams(vmem_limit_bytes=VMEM_LIMIT))(*parts, *ws, *ms, *vs, *extra_parts)
    return [res[4 * i:4 * i + 4] for i in range(n)], list(res[4 * n:])


def _adamw_plain(g, w, m, v, name):
    def body(g_ref, w_ref, m_ref, v_ref, d_out, m_out, v_out):
        delta, mn, vn = _adam_math(w_ref[...], g_ref[...], m_ref[...], v_ref[...])
        d_out[...] = delta
        m_out[...] = mn
        v_out[...] = vn

    return pl.pallas_call(body, name=name, out_shape=[jax.ShapeDtypeStruct(w.shape, F32)] * 3)(g, w, m, v)


def kernel(x, mem, positions, g_mix, w_in, g_a_v, w_spatial, b_spatial, g_b_q, g_b_k, sinks, g_mem, w_mem_kv, g_c_q, g_c_k, w_branch_a, w_branch_b, w_branch_c, w_out, g_ffn, w_up, conv_w, conv_b, w_down, loss_target, m_g_mix, m_w_in, m_g_a_v, m_w_spatial, m_b_spatial, m_g_b_q, m_g_b_k, m_sinks, m_g_mem, m_w_mem_kv, m_g_c_q, m_g_c_k, m_w_branch_a, m_w_branch_b, m_w_branch_c, m_w_out, m_g_ffn, m_w_up, m_conv_w, m_conv_b, m_w_down, v_g_mix, v_w_in, v_g_a_v, v_w_spatial, v_b_spatial, v_g_b_q, v_g_b_k, v_sinks, v_g_mem, v_w_mem_kv, v_g_c_q, v_g_c_k, v_w_branch_a, v_w_branch_b, v_w_branch_c, v_w_out, v_g_ffn, v_w_up, v_conv_w, v_conv_b, v_w_down):
    S, D = x.shape[1], x.shape[2]
    M = mem.shape[1]
    F = w_down.shape[1] * N_DEV
    in_cols = w_in.shape[2] * N_DEV
    ax, ay, ac = lax.axis_index("x"), lax.axis_index("y"), lax.axis_index("c")
    core = jnp.reshape(ac, (1,)).astype(jnp.int32)
    chip = jnp.reshape(2 * ax + ay, (1,)).astype(jnp.int32)
    me = 4 * ax + 2 * ay + ac

    x2, mem2, tgt2 = x[0], mem[0], loss_target[0]

    big = dict(w_in=w_in[0].T, w_mem_kv=w_mem_kv[0], w_branch_a=w_branch_a[0], w_branch_b=w_branch_b[0],
               w_branch_c=w_branch_c[0], w_out=w_out[0], w_up=w_up[0], w_down=w_down[0])
    names = list(big)
    cast = {k: big[k].astype(BF) for k in names}
    W = {}
    cb3 = conv_b.reshape(2, 1, F)
    W["w_in"], = _allgather_seq([cast["w_in"]], "ag_seq0", 0)
    w_in_t = W["w_in"].reshape(in_cols, D)
    grp1 = ["w_mem_kv", "w_branch_a", "w_branch_b", "w_branch_c", "w_out"]
    res1 = _allgather_seq([cast[k] for k in grp1] + [conv_w[0]], "ag_seq1", 1, after=(_token((w_in_t,), "tok_w_in"),))
    W.update(zip(grp1, res1))
    cw3 = res1[-1]
    w_kv_f = W["w_mem_kv"].reshape(D, 2 * C_WIDTH)
    w_out_f = W["w_out"].reshape(D, D)

    half = ROPE_DIM // 2
    inv = ROPE_THETA ** (-jnp.arange(half, dtype=F32) / half)
    ang = positions[0].astype(F32)[:, None] * inv
    cos, sin = jnp.cos(ang), jnp.sin(ang)
    one, zero = jnp.ones((S, B_HEAD_DIM - ROPE_DIM), F32), jnp.zeros((S, B_HEAD_DIM - ROPE_DIM), F32)
    z8 = jnp.zeros((S, half), F32)
    ct = jnp.tile(jnp.concatenate([cos, cos, one], axis=1), (1, 2))
    sa = jnp.tile(jnp.concatenate([-sin, z8, zero], axis=1), (1, 2))
    sb = jnp.tile(jnp.concatenate([z8, sin, zero], axis=1), (1, 2))
    gq2, gk2 = jnp.tile(g_b_q, (1, 2)), jnp.tile(g_b_k, (1, 2))
    b_t = b_spatial[0].T

    h, rstd1 = _rms_fwd(x2, g_mix, "rms1_fwd")
    proj = _mm(h, w_in_t, "nt", F32, "mm_proj", tn=1280)
    y_a = _a_fwd(proj, g_a_v, w_spatial[0], b_t)
    W["w_up"], = _allgather_seq([cast["w_up"]], "ag_seq2", 2, after=(_token((W["w_out"], proj), "tok_group1"),))
    qn, kn = _b_pre(proj, gq2, gk2, ct, sa, sb)
    y_b = _b_attn_fwd(qn, kn, proj, sinks)
    mem_h, rstd_m = _rms_fwd(mem2, g_mem, "rmsmem_fwd")
    kv = _mm(mem_h, w_kv_f, "nn", F32, "mm_kv", after=(y_b,))
    y_c = _c_fwd(proj, kv, g_c_q, g_c_k)
    w_branches = [W["w_branch_a"], W["w_branch_b"], W["w_branch_c"]]
    merged, z_a, z_b, z_c = _merge_fwd(proj, [y_a, y_b, y_c], w_branches)
    x1, h2, rstd2 = _residual_rms(merged, w_out_f, x2, g_ffn, "mm_x1_rms2")
    W["w_down"], = _allgather_seq([cast["w_down"]], "ag_seq3", 3, after=(W["w_up"], h2))
    w_down_f = W["w_down"].reshape(F, D)
    up3 = _mm(h2, W["w_up"], "nn", BF, "mm_up", b_stack=True, out_parts=2)
    act = _ffn_act_fwd(up3, cw3, cb3)
    dy, dy_b, loss_acc = _out_loss(act, w_down_f, x1, tgt2)

    reduced = {}

    def as4(g):
        return g.reshape(4, 2, g.shape[1], g.shape[2])

    def finish_group(gi, keys, g4, from_sibling):
        sums = [_pair_add(a, b, core, "rs_add_" + k) for k, a, b in zip(keys, g4, from_sibling)]
        from_chips = _chip_exchange(sums, f"rs_chip{gi}", 4 + gi)
        reduced.update(zip(keys, zip(sums, from_chips)))
        return tuple(sums)

    d_act = _mm(dy_b, w_down_f, "nt", BF, "mm_dact", tn=1408)
    g_down = _mm(act, dy_b, "tn", BF, "mm_gdown", tm=1408)
    d_up3, d_cw3, d_cb3 = _ffn_act_bwd(up3, cw3, cb3, d_act, after=(g_down,))
    grp0 = [as4(g_down.reshape(N_DEV, F // N_DEV, D))]
    g_up, sib0 = _mm(h2, d_up3, "tn", BF, "mm_gup", b_parts=2, out_stack=True, exchange=grp0)
    sums0 = finish_group(0, ["w_down"], grp0, sib0)
    grp1 = [as4(g_up)]
    d_h2, sib1 = _mm(d_up3, W["w_up"], "nt", F32, "mm_dh2", a_parts=2, b_stack=True, tm=2048, after=sums0, exchange=grp1)
    sums1 = finish_group(1, ["w_up"], grp1, sib1)
    dx1, dx1_b, d_g_ffn = _rms_bwd(x1, rstd2, g_ffn, d_h2, dy, "rms2_bwd", after=sums1)
    g_out = _mm(merged, dx1_b, "tn", BF, "mm_gout")
    grp2 = [as4(g_out.reshape(N_DEV, D // N_DEV, D))]
    d_merged, sib2 = _mm(dx1_b, w_out_f, "nt", F32, "mm_dmerged", exchange=grp2)
    sums2 = finish_group(2, ["w_out"], grp2, sib2)
    dz_a, dz_b, dz_c, dga, dgb, dgc, dy_a, dy_b_, dy_c = _merge_bwd(proj, [z_a, z_b, z_c], d_merged, w_branches, after=sums2)
    g_ba = _mm(y_a, dz_a, "tn", BF, "mm_gba", out_stack=True)
    g_bb = _mm(y_b, dz_b, "tn", BF, "mm_gbb", out_stack=True)
    g_bc = _mm(y_c, dz_c, "tn", BF, "mm_gbc", out_stack=True)
    d_uv, d_g_a_v, d_w_s, d_b_t = _a_bwd(proj, g_a_v, w_spatial[0], b_t, dy_a, after=(g_ba, g_bb, g_bc))
    dqn, dkn, dv_b, dsink_rows = _b_attn_bwd(qn, kn, proj, sinks, dy_b_)
    d_qkv, d_gq2, d_gk2 = _b_pre_bwd(proj, gq2, gk2, ct, sa, sb, dqn, dkn, dv_b)
    dq_c, dk_c, dv_c, d_gcq, d_gck = _c_bwd(proj, kv, g_c_q, g_c_k, dy_c)
    dkv_b = jnp.concatenate([dk_c, dv_c], axis=1).astype(BF)
    d_memh = _mm(dkv_b, w_kv_f, "nt", F32, "mm_dmemh")
    g_kv = _mm(mem_h, dkv_b, "tn", BF, "mm_gkv")
    _, _, d_g_mem = _rms_bwd(mem2, rstd_m, g_mem, d_memh, None, "rmsmem_bwd")
    dproj = jnp.concatenate([d_uv, d_qkv, dq_c, dga, dgb, dgc], axis=1)
    small_names = ["g_mix", "g_a_v", "w_spatial", "b_spatial", "g_b_q", "g_b_k", "sinks", "g_mem", "g_c_q", "g_c_k", "g_ffn", "conv_b"]
    small_w = dict(g_mix=g_mix, g_a_v=g_a_v, w_spatial=w_spatial, b_spatial=b_spatial, g_b_q=g_b_q, g_b_k=g_b_k, sinks=sinks,
                   g_mem=g_mem, g_c_q=g_c_q, g_c_k=g_c_k, g_ffn=g_ffn, conv_b=conv_b)
    small_g = dict(
        g_a_v=d_g_a_v, w_spatial=d_w_s, b_spatial=d_b_t.T,
        g_b_q=d_gq2.reshape(2, B_HEAD_DIM).sum(0), g_b_k=d_gk2.reshape(2, B_HEAD_DIM).sum(0),
        sinks=dsink_rows.sum(0)[:B_HEADS], g_mem=d_g_mem, g_c_q=d_gcq.sum(0), g_c_k=d_gck.sum(0), g_ffn=d_g_ffn,
        conv_b=d_cb3)
    early = [small_g[k].reshape(small_w[k].shape) for k in small_names[1:]] + [d_cw3, loss_acc[0:1]]
    early_parts = _allgather(early, "ag_small")
    grp3 = [as4(g_ba), as4(g_bb), as4(g_bc)]
    g_in, sib3 = _mm(dproj, h, "tn", BF, "mm_gin", tm=1280, after=(early_parts[0],), exchange=grp3)
    sums3 = finish_group(3, ["w_branch_a", "w_branch_b", "w_branch_c"], grp3, sib3)
    grp4 = [as4(g_in.reshape(N_DEV, in_cols // N_DEV, D)), as4(g_kv.reshape(N_DEV, D // N_DEV, 2 * C_WIDTH))]
    d_h, sib4 = _mm(dproj, w_in_t, "nn", F32, "mm_dh", tm=2048, tk=1280, after=sums3, exchange=grp4)
    sums4 = finish_group(4, ["w_in", "w_mem_kv"], grp4, sib4)
    grad_x, _, d_g_mix = _rms_bwd(x2, rstd1, g_mix, d_h, dx1, "rms1_bwd", after=sums4)

    small_m = dict(g_mix=m_g_mix, g_a_v=m_g_a_v, w_spatial=m_w_spatial, b_spatial=m_b_spatial, g_b_q=m_g_b_q, g_b_k=m_g_b_k,
                   sinks=m_sinks, g_mem=m_g_mem, g_c_q=m_g_c_q, g_c_k=m_g_c_k, g_ffn=m_g_ffn, conv_b=m_conv_b)
    small_v = dict(g_mix=v_g_mix, g_a_v=v_g_a_v, w_spatial=v_w_spatial, b_spatial=v_b_spatial, g_b_q=v_g_b_q, g_b_k=v_g_b_k,
                   sinks=v_sinks, g_mem=v_g_mem, g_c_q=v_g_c_q, g_c_k=v_g_c_k, g_ffn=v_g_ffn, conv_b=v_conv_b)
    parts = _allgather([d_g_mix], "ag_g_mix") + early_parts
    n_small = len(small_names)
    small_res, (g_cw3, loss_row) = _adamw_small(parts[:n_small], [small_w[k] for k in small_names], [small_m[k] for k in small_names],
                                                [small_v[k] for k in small_names], parts[n_small:], "adamw_small")
    loss = loss_row[0, 0]
    small_out = dict(zip(small_names, small_res))
    c_cw = 2 * F // N_DEV
    g_cw = lax.dynamic_slice(g_cw3, (me // (N_DEV // 2), 0, (me % (N_DEV // 2)) * c_cw), (1, 3, c_cw))[0]
    cw_res = _adamw_plain(g_cw, conv_w[0], m_conv_w[0], v_conv_w[0], "adamw_conv_w")
    big_out = {"conv_w": [g_cw[None]] + [a[None] for a in cw_res]}

    moments = dict(w_in=(m_w_in, v_w_in), w_mem_kv=(m_w_mem_kv, v_w_mem_kv), w_branch_a=(m_w_branch_a, v_w_branch_a),
                   w_branch_b=(m_w_branch_b, v_w_branch_b), w_branch_c=(m_w_branch_c, v_w_branch_c), w_out=(m_w_out, v_w_out),
                   w_up=(m_w_up, v_w_up), w_down=(m_w_down, v_w_down))
    token = (grad_x, small_res[0][0])
    for k in ["w_down", "w_up", "w_out", "w_branch_a", "w_branch_b", "w_branch_c", "w_mem_kv", "w_in"]:
        s, r = reduced[k]
        mk, vk = moments[k][0][0], moments[k][1][0]
        if k == "w_in":
            res = _adamw_big(s, r, chip, big[k], mk.T, vk.T, "adamw_" + k, after=token)
            big_out[k] = [a.T[None] for a in res]
        else:
            res = _adamw_big(s, r, chip, big[k], mk, vk, "adamw_" + k, after=token)
            big_out[k] = [a[None] for a in res]
        token = (res[0],)

    order = ["g_mix", "w_in", "g_a_v", "w_spatial", "b_spatial", "g_b_q", "g_b_k", "sinks", "g_mem", "w_mem_kv", "g_c_q", "g_c_k",
             "w_branch_a", "w_branch_b", "w_branch_c", "w_out", "g_ffn", "w_up", "conv_w", "conv_b", "w_down"]
    res = {**small_out, **big_out}
    outs = [loss, grad_x[None]]
    for field in range(4):
        outs += [res[k][field] for k in order]
    return tuple(outs)
```

```python
import functools

import jax
import jax.numpy as jnp
from jax import lax
from jax.experimental import pallas as pl
from jax.experimental.pallas import tpu as pltpu
from jax.experimental.pallas import tpu_sc as plsc

F32 = jnp.float32
BF = jnp.bfloat16
EPS = 1e-6
NEG = -1e30

N_DEV = 8
CHUNK = 128
A_GROUPS = 4
A_WIDTH = 512
B_HEADS = 16
B_KV_HEADS = 2
B_HEAD_DIM = 64
B_WIDTH = 1024
B_KV_WIDTH = 128
ROPE_DIM = 16
ROPE_THETA = 500000.0
C_HEADS = 4
C_HEAD_DIM = 128
C_WIDTH = 512
GATE_OFF = 2 * A_WIDTH + B_WIDTH + 2 * B_KV_WIDTH + C_WIDTH

ADAM_LR = 0.001
ADAM_B1 = 0.9
ADAM_B2 = 0.999
ADAM_EPS = 1e-08
ADAM_WD = 0.01
ADAM_STEP = 10

VMEM_LIMIT = 48 * 1024 * 1024
MESH = pl.DeviceIdType.MESH


def _pick(n, prefs):
    for p in prefs:
        if p <= n and n % p == 0:
            return p
    return n


def _params(sem):
    return pltpu.CompilerParams(dimension_semantics=sem, vmem_limit_bytes=VMEM_LIMIT)


def _hide(body, n_seen, n_hidden):
    if not n_hidden:
        return body

    def wrapped(*refs):
        return body(*refs[:n_seen], *refs[n_seen + n_hidden:])

    return wrapped


def _hidden_specs(after):
    return [pl.BlockSpec(memory_space=pl.ANY) for _ in after]


def _token(xs, name):
    def body(*refs):
        refs[-1][...] = jnp.zeros_like(refs[-1])

    return pl.pallas_call(body, name=name, in_specs=_hidden_specs(xs), out_shape=jax.ShapeDtypeStruct((8, 128), F32))(*xs)


def _mm(a, b, mode, out_dtype, name, *, resid=None, b_stack=False, a_parts=0, b_parts=0, out_parts=0,
        out_stack=False, tm=1024, tn=1024, tk=2048, after=(), exchange=()):
    if mode == "nn":
        M = a.shape[-2]
        K = a.shape[-1] * max(a_parts, 1)
        N = b.shape[-1] * (N_DEV if b_stack else 1)
        dims = (((1,), (0,)), ((), ()))
    elif mode == "nt":
        M = a.shape[-2]
        K = a.shape[-1] * max(a_parts, 1)
        N = b.shape[-2]
        dims = (((1,), (1,)), ((), ()))
    else:
        K = a.shape[-2]
        M = a.shape[-1]
        N = b.shape[-1] * max(b_parts, 1)
        dims = (((0,), (0,)), ((), ()))
    if b_stack and mode == "nn":
        tn = b.shape[-1]
    if b_stack and mode == "nt":
        tk = b.shape[-1]
    if out_stack:
        tn = N // N_DEV
    tm, tn, tk = _pick(M, (tm,)), _pick(N, (tn,)), _pick(K, (tk,))
    if M % tm or N % tn or K % tk:
        raise ValueError(f"{name}: tiles {tm},{tn},{tk} do not divide {M},{N},{K}")
    nm, nn, nk = M // tm, N // tn, K // tk

    def parts_idx(t, ntile, parts):
        per = ntile // parts
        return t // per, t % per

    if mode in ("nn", "nt"):
        if a_parts:
            a_spec = pl.BlockSpec((None, tm, tk), lambda m, n, k: (parts_idx(k, nk, a_parts)[0], m, parts_idx(k, nk, a_parts)[1]))
        else:
            a_spec = pl.BlockSpec((tm, tk), lambda m, n, k: (m, k))
    else:
        a_spec = pl.BlockSpec((tk, tm), lambda m, n, k: (k, m))
    if mode == "nn":
        if b_stack:
            b_spec = pl.BlockSpec((None, tk, tn), lambda m, n, k: (n, k, 0))
        else:
            b_spec = pl.BlockSpec((tk, tn), lambda m, n, k: (k, n))
    elif mode == "nt":
        if b_stack:
            b_spec = pl.BlockSpec((None, tn, tk), lambda m, n, k: (k, n, 0))
        else:
            b_spec = pl.BlockSpec((tn, tk), lambda m, n, k: (n, k))
    else:
        if b_parts:
            b_spec = pl.BlockSpec((None, tk, tn), lambda m, n, k: (parts_idx(n, nn, b_parts)[0], k, parts_idx(n, nn, b_parts)[1]))
        else:
            b_spec = pl.BlockSpec((tk, tn), lambda m, n, k: (k, n))
    if out_stack:
        out_shape = jax.ShapeDtypeStruct((N_DEV, M, tn), out_dtype)
        o_spec = pl.BlockSpec((None, tm, tn), lambda m, n, k: (n, m, 0))
    elif out_parts:
        out_shape = jax.ShapeDtypeStruct((out_parts, M, N // out_parts), out_dtype)
        o_spec = pl.BlockSpec((None, tm, tn), lambda m, n, k: (parts_idx(n, nn, out_parts)[0], m, parts_idx(n, nn, out_parts)[1]))
    else:
        out_shape = jax.ShapeDtypeStruct((M, N), out_dtype)
        o_spec = pl.BlockSpec((tm, tn), lambda m, n, k: (m, n))
    has_resid = resid is not None

    n_ex = len(exchange)
    n_in = 2 + has_resid + len(after)

    def body(*refs):
        a_ref, b_ref = refs[:2]
        r_ref = refs[2] if has_resid else None
        ex_in = refs[n_in:n_in + n_ex]
        o_ref = refs[n_in + n_ex]
        ex_out = refs[n_in + n_ex + 1:n_in + 2 * n_ex + 1]
        scratch = refs[n_in + 2 * n_ex + 1:]
        m_i, n_i, k = pl.program_id(0), pl.program_id(1), pl.program_id(2)

        def pushes():
            send_sems, recv_sems = scratch[-2:]
            x, y, c = lax.axis_index("x"), lax.axis_index("y"), lax.axis_index("c")
            return [pltpu.make_async_remote_copy(
                src_ref=ex_in[w].at[:, 1 - c], dst_ref=ex_out[w], send_sem=send_sems.at[w], recv_sem=recv_sems.at[w],
                device_id=(x, y, 1 - c), device_id_type=MESH) for w in range(n_ex)]

        if n_ex:
            @pl.when((m_i == 0) & (n_i == 0) & (k == 0))
            def _():
                for cp in pushes():
                    cp.start()

        if nk == 1:
            res = lax.dot_general(a_ref[...], b_ref[...], dims, preferred_element_type=F32)
            if has_resid:
                res = res + r_ref[...]
            o_ref[...] = res.astype(o_ref.dtype)
        else:
            acc = scratch[0]

            @pl.when(k == 0)
            def _():
                acc[...] = jnp.zeros_like(acc)

            acc[...] += lax.dot_general(a_ref[...], b_ref[...], dims, preferred_element_type=F32)

            @pl.when(k == nk - 1)
            def _():
                res = acc[...]
                if has_resid:
                    res = res + r_ref[...]
                o_ref[...] = res.astype(o_ref.dtype)

        if n_ex:
            @pl.when((m_i == nm - 1) & (n_i == nn - 1) & (k == nk - 1))
            def _():
                for cp in pushes():
                    cp.wait()

    in_specs = [a_spec, b_spec]
    args = [a, b]
    if has_resid:
        in_specs.append(pl.BlockSpec((tm, tn), lambda m, n, k: (m, n)))
        args.append(resid)
    in_specs += _hidden_specs(after) + _hidden_specs(exchange)
    args += list(after) + list(exchange)
    scratch_shapes = [pltpu.VMEM((tm, tn), F32)] if nk > 1 else []
    if not n_ex:
        return pl.pallas_call(
            body, name=name, grid=(nm, nn, nk), in_specs=in_specs, out_specs=o_spec, out_shape=out_shape,
            scratch_shapes=scratch_shapes, compiler_params=_params(("parallel", "parallel", "arbitrary")),
        )(*args)
    res = pl.pallas_call(
        body, name=name, grid=(nm, nn, nk), in_specs=in_specs, out_specs=[o_spec] + _hidden_specs(exchange),
        out_shape=[out_shape] + [jax.ShapeDtypeStruct((g.shape[0],) + g.shape[2:], g.dtype) for g in exchange],
        scratch_shapes=scratch_shapes + [pltpu.SemaphoreType.DMA((n_ex,)), pltpu.SemaphoreType.DMA((n_ex,))],
        compiler_params=_params(("arbitrary", "arbitrary", "arbitrary")),
    )(*args)
    return res[0], list(res[1:])


def _rms_fwd(x, g, name):
    R, D = x.shape
    tr = _pick(R, (256,))

    def body(x_ref, g_ref, h_ref, r_ref):
        xv = x_ref[...]
        r = lax.rsqrt(jnp.mean(xv * xv, axis=-1, keepdims=True) + EPS)
        h_ref[...] = (xv * r * g_ref[...]).astype(BF)
        r_ref[...] = r

    return pl.pallas_call(
        body, name=name, grid=(R // tr,),
        in_specs=[pl.BlockSpec((tr, D), lambda i: (i, 0)), pl.BlockSpec((1, D), lambda i: (0, 0))],
        out_specs=[pl.BlockSpec((tr, D), lambda i: (i, 0)), pl.BlockSpec((tr, 1), lambda i: (i, 0))],
        out_shape=[jax.ShapeDtypeStruct((R, D), BF), jax.ShapeDtypeStruct((R, 1), F32)],
        compiler_params=_params(("parallel",)),
    )(x, g)


def _rms_bwd(x, r, g, dh, dres, name, after=()):
    R, D = x.shape
    tr = _pick(R, (256,))
    has_res = dres is not None

    def body(*refs):
        if has_res:
            x_ref, r_ref, g_ref, dh_ref, dres_ref, dx_ref, dxb_ref, dg_ref = refs
        else:
            x_ref, r_ref, g_ref, dh_ref, dx_ref, dxb_ref, dg_ref = refs
        i = pl.program_id(0)
        xv, rv, dhv = x_ref[...], r_ref[...], dh_ref[...]
        gy = dhv * g_ref[...]
        c = jnp.sum(xv * gy, axis=-1, keepdims=True)
        dx = rv * gy - xv * (rv * rv * rv) * (c * (1.0 / D))
        if has_res:
            dx = dx + dres_ref[...]
        dx_ref[...] = dx
        dxb_ref[...] = dx.astype(BF)
        part = jnp.sum(dhv * xv * rv, axis=0, keepdims=True)

        @pl.when(i == 0)
        def _():
            dg_ref[...] = part

        @pl.when(i > 0)
        def _():
            dg_ref[...] += part

    row = pl.BlockSpec((tr, D), lambda i: (i, 0))
    in_specs = [row, pl.BlockSpec((tr, 1), lambda i: (i, 0)), pl.BlockSpec((1, D), lambda i: (0, 0)), row]
    args = [x, r, g, dh]
    if has_res:
        in_specs.append(row)
        args.append(dres)
    return pl.pallas_call(
        _hide(body, len(args), len(after)), name=name, grid=(R // tr,), in_specs=in_specs + _hidden_specs(after),
        out_specs=[row, row, pl.BlockSpec((1, D), lambda i: (0, 0))],
        out_shape=[jax.ShapeDtypeStruct((R, D), F32), jax.ShapeDtypeStruct((R, D), BF), jax.ShapeDtypeStruct((1, D), F32)],
        compiler_params=_params(("arbitrary",)),
    )(*args, *after)


def _a_chunk(us, vs, gvs, ws, bs):
    r_i = lax.broadcasted_iota(jnp.int32, (CHUNK, CHUNK), 0)
    c_i = lax.broadcasted_iota(jnp.int32, (CHUNK, CHUNK), 1)
    causal = r_i >= c_i
    vg = [jax.nn.gelu(v) for v in vs]
    ss = sum(jnp.sum(v * v, axis=-1, keepdims=True) for v in vg)
    r = lax.rsqrt(ss * (1.0 / A_WIDTH) + EPS)
    ys = []
    for g in range(A_GROUPS):
        vn = vg[g] * r * gvs[g]
        w = jnp.where(causal, ws[g], 0.0)
        s = jnp.dot(w.astype(BF), vn.astype(BF), preferred_element_type=F32) + bs[g]
        ys.append(jax.nn.gelu(us[g]) * s)
    return ys


def _a_split(u_ref, v_ref, g_ref, w_ref, b_ref):
    sl = [slice(g * 128, (g + 1) * 128) for g in range(A_GROUPS)]
    return ([u_ref[:, s] for s in sl], [v_ref[:, s] for s in sl], [g_ref[:, s] for s in sl],
            [w_ref[g] for g in range(A_GROUPS)], [b_ref[:, g:g + 1] for g in range(A_GROUPS)])


def _a_specs(S):
    return [pl.BlockSpec((CHUNK, A_WIDTH), lambda n: (n, 0)), pl.BlockSpec((CHUNK, A_WIDTH), lambda n: (n, 1)),
            pl.BlockSpec((1, A_WIDTH), lambda n: (0, 0)), pl.BlockSpec((A_GROUPS, CHUNK, CHUNK), lambda n: (0, 0, 0)),
            pl.BlockSpec((CHUNK, A_GROUPS), lambda n: (0, 0))]


def _a_fwd(proj, g_v, w_s, b_t):
    S = proj.shape[0]

    def body(u_ref, v_ref, g_ref, w_ref, b_ref, y_ref):
        ys = _a_chunk(*_a_split(u_ref, v_ref, g_ref, w_ref, b_ref))
        for g in range(A_GROUPS):
            y_ref[:, g * 128:(g + 1) * 128] = ys[g].astype(BF)

    return pl.pallas_call(
        body, name="a_fwd", grid=(S // CHUNK,), in_specs=_a_specs(S),
        out_specs=pl.BlockSpec((CHUNK, A_WIDTH), lambda n: (n, 0)),
        out_shape=jax.ShapeDtypeStruct((S, A_WIDTH), BF), compiler_params=_params(("parallel",)),
    )(proj, proj, g_v, w_s, b_t)


def _a_bwd(proj, g_v, w_s, b_t, dy, after=()):
    S = proj.shape[0]

    def body(u_ref, v_ref, g_ref, w_ref, b_ref, dy_ref, duv_ref, dg_ref, dw_ref, db_ref):
        n = pl.program_id(0)
        dys = [dy_ref[:, g * 128:(g + 1) * 128] for g in range(A_GROUPS)]
        _, vjp = jax.vjp(_a_chunk, *_a_split(u_ref, v_ref, g_ref, w_ref, b_ref))
        dus, dvs, dgs, dws, dbs = vjp(dys)

        @pl.when(n == 0)
        def _():
            dg_ref[...] = jnp.zeros_like(dg_ref)
            dw_ref[...] = jnp.zeros_like(dw_ref)
            db_ref[...] = jnp.zeros_like(db_ref)

        for g in range(A_GROUPS):
            duv_ref[:, g * 128:(g + 1) * 128] = dus[g].astype(BF)
            duv_ref[:, A_WIDTH + g * 128:A_WIDTH + (g + 1) * 128] = dvs[g].astype(BF)
            dg_ref[:, g * 128:(g + 1) * 128] += dgs[g]
            dw_ref[g] += dws[g]
            db_ref[:, g:g + 1] += dbs[g]

    return pl.pallas_call(
        _hide(body, 6, len(after)), name="a_bwd", grid=(S // CHUNK,),
        in_specs=_a_specs(S) + [pl.BlockSpec((CHUNK, A_WIDTH), lambda n: (n, 0))] + _hidden_specs(after),
        out_specs=[pl.BlockSpec((CHUNK, 2 * A_WIDTH), lambda n: (n, 0)), pl.BlockSpec((1, A_WIDTH), lambda n: (0, 0)),
                   pl.BlockSpec((A_GROUPS, CHUNK, CHUNK), lambda n: (0, 0, 0)), pl.BlockSpec((CHUNK, A_GROUPS), lambda n: (0, 0))],
        out_shape=[jax.ShapeDtypeStruct((S, 2 * A_WIDTH), BF), jax.ShapeDtypeStruct((1, A_WIDTH), F32),
                   jax.ShapeDtypeStruct((A_GROUPS, CHUNK, CHUNK), F32), jax.ShapeDtypeStruct((CHUNK, A_GROUPS), F32)],
        compiler_params=_params(("arbitrary",)),
    )(proj, proj, g_v, w_s, b_t, dy, *after)


def _half_mask(shape, which):
    lane = lax.broadcasted_iota(jnp.int32, shape, len(shape) - 1)
    return (lane >= 64) == (which == 1)


def _pair_norm_rope(x, g, ct, sa, sb):
    lo = _half_mask(x.shape, 0)
    x2 = x * x
    ss_lo = jnp.sum(jnp.where(lo, x2, 0.0), axis=-1, keepdims=True)
    ss_hi = jnp.sum(jnp.where(lo, 0.0, x2), axis=-1, keepdims=True)
    r = jnp.where(lo, lax.rsqrt(ss_lo * (1.0 / B_HEAD_DIM) + EPS), lax.rsqrt(ss_hi * (1.0 / B_HEAD_DIM) + EPS))
    xr = x * r
    xn = xr * g
    out = xn * ct + pltpu.roll(xn, 120, 1) * sa + pltpu.roll(xn, 8, 1) * sb
    return out, xr, r


def _pair_norm_rope_bwd(x, g, ct, sa, sb, dout):
    lo = _half_mask(x.shape, 0)
    _, xr, r = _pair_norm_rope(x, g, ct, sa, sb)
    dxn = dout * ct + pltpu.roll(dout * sa, 8, 1) + pltpu.roll(dout * sb, 120, 1)
    gy = dxn * g
    t = xr * gy
    c_lo = jnp.sum(jnp.where(lo, t, 0.0), axis=-1, keepdims=True)
    c_hi = jnp.sum(jnp.where(lo, 0.0, t), axis=-1, keepdims=True)
    c = jnp.where(lo, c_lo, c_hi)
    dx = r * (gy - xr * c * (1.0 / B_HEAD_DIM))
    dg = jnp.sum(dxn * xr, axis=0, keepdims=True)
    return dx, dg


def _b_pre(proj, gq2, gk2, ct, sa, sb):
    S = proj.shape[0]
    tr = _pick(S, (256,))
    n_pair = B_WIDTH // 128

    def body(q_ref, k_ref, gq_ref, gk_ref, ct_ref, sa_ref, sb_ref, qn_ref, kn_ref):
        ct_v, sa_v, sb_v = ct_ref[...], sa_ref[...], sb_ref[...]
        for p in range(n_pair):
            o, _, _ = _pair_norm_rope(q_ref[:, p * 128:(p + 1) * 128], gq_ref[...], ct_v, sa_v, sb_v)
            qn_ref[:, p * 128:(p + 1) * 128] = o.astype(BF)
        o, _, _ = _pair_norm_rope(k_ref[...], gk_ref[...], ct_v, sa_v, sb_v)
        kn_ref[...] = o.astype(BF)

    tab = pl.BlockSpec((tr, 128), lambda i: (i, 0))
    gsp = pl.BlockSpec((1, 128), lambda i: (0, 0))
    return pl.pallas_call(
        body, name="b_pre", grid=(S // tr,),
        in_specs=[pl.BlockSpec((tr, B_WIDTH), lambda i: (i, 1)), pl.BlockSpec((tr, 128), lambda i: (i, 2 * B_WIDTH // 128)),
                  gsp, gsp, tab, tab, tab],
        out_specs=[pl.BlockSpec((tr, B_WIDTH), lambda i: (i, 0)), tab],
        out_shape=[jax.ShapeDtypeStruct((S, B_WIDTH), BF), jax.ShapeDtypeStruct((S, 128), BF)],
        compiler_params=_params(("parallel",)),
    )(proj, proj, gq2, gk2, ct, sa, sb)


def _b_pre_bwd(proj, gq2, gk2, ct, sa, sb, dqn, dkn, dv):
    S = proj.shape[0]
    tr = _pick(S, (256,))
    n_pair = B_WIDTH // 128

    def body(q_ref, k_ref, gq_ref, gk_ref, ct_ref, sa_ref, sb_ref, dqn_ref, dkn_ref, dv_ref, dqkv_ref, dgq_ref, dgk_ref):
        i = pl.program_id(0)
        ct_v, sa_v, sb_v = ct_ref[...], sa_ref[...], sb_ref[...]
        dgq = jnp.zeros((1, 128), F32)
        for p in range(n_pair):
            sl = slice(p * 128, (p + 1) * 128)
            dx, dg = _pair_norm_rope_bwd(q_ref[:, sl], gq_ref[...], ct_v, sa_v, sb_v, dqn_ref[:, sl])
            dqkv_ref[:, sl] = dx.astype(BF)
            dgq = dgq + dg
        dx, dgk = _pair_norm_rope_bwd(k_ref[...], gk_ref[...], ct_v, sa_v, sb_v, dkn_ref[...])
        dqkv_ref[:, B_WIDTH:B_WIDTH + 128] = dx.astype(BF)
        dqkv_ref[:, B_WIDTH + 128:B_WIDTH + 256] = dv_ref[...].astype(BF)

        @pl.when(i == 0)
        def _():
            dgq_ref[...] = dgq
            dgk_ref[...] = dgk

        @pl.when(i > 0)
        def _():
            dgq_ref[...] += dgq
            dgk_ref[...] += dgk

    tab = pl.BlockSpec((tr, 128), lambda i: (i, 0))
    gsp = pl.BlockSpec((1, 128), lambda i: (0, 0))
    return pl.pallas_call(
        body, name="b_pre_bwd", grid=(S // tr,),
        in_specs=[pl.BlockSpec((tr, B_WIDTH), lambda i: (i, 1)), pl.BlockSpec((tr, 128), lambda i: (i, 2 * B_WIDTH // 128)),
                  gsp, gsp, tab, tab, tab, pl.BlockSpec((tr, B_WIDTH), lambda i: (i, 0)), tab, tab],
        out_specs=[pl.BlockSpec((tr, B_WIDTH + 256), lambda i: (i, 0)), gsp, gsp],
        out_shape=[jax.ShapeDtypeStruct((S, B_WIDTH + 256), BF), jax.ShapeDtypeStruct((1, 128), F32), jax.ShapeDtypeStruct((1, 128), F32)],
        compiler_params=_params(("arbitrary",)),
    )(proj, proj, gq2, gk2, ct, sa, sb, dqn, dkn, dv)


def _b_dup(x2, g):
    d = jnp.where(_half_mask(x2.shape, g), x2, 0.0)
    return (d + pltpu.roll(d, 64, 1)).astype(BF)


PAIRS_PER_GROUP = B_HEADS // B_KV_HEADS // 2
GROUP_ROWS = PAIRS_PER_GROUP * CHUNK


def _b_valid(n):
    row = lax.broadcasted_iota(jnp.int32, (GROUP_ROWS, 2 * CHUNK), 0) & (CHUNK - 1)
    col = lax.broadcasted_iota(jnp.int32, (GROUP_ROWS, 2 * CHUNK), 1)
    rel = row + CHUNK - col
    return (rel >= 0) & (rel < CHUNK) & ((col >= CHUNK) | (n > 0))


def _b_blocks(x2, g):
    xd = _b_dup(x2, g)
    lo = _half_mask(xd.shape, 0)
    zero = jnp.zeros_like(xd)
    return jnp.concatenate([jnp.where(lo, xd, zero), jnp.where(lo, zero, xd)], axis=0)


def _b_sink_col(s_ref, g, hf):
    rb = lax.broadcasted_iota(jnp.int32, (GROUP_ROWS, 1), 0) // CHUNK
    col = jnp.zeros((GROUP_ROWS, 1), F32)
    for pp in range(PAIRS_PER_GROUP):
        col = jnp.where(rb == pp, s_ref[0, 2 * (g * PAIRS_PER_GROUP + pp) + hf], col)
    return col


def _b_probs(qs, kblk, valid, sinks):
    s = lax.dot_general(qs, kblk, (((1,), (1,)), ((), ())), preferred_element_type=F32) * (B_HEAD_DIM ** -0.5)
    out = []
    for hf in range(2):
        sh = jnp.where(valid, s[:, hf * 2 * CHUNK:(hf + 1) * 2 * CHUNK], NEG)
        m = jnp.maximum(jnp.max(sh, axis=-1, keepdims=True), sinks[hf])
        e = jnp.exp(sh - m)
        es = jnp.exp(sinks[hf] - m)
        inv = 1.0 / (jnp.sum(e, axis=-1, keepdims=True) + es)
        out.append((e * inv, es * inv))
    return out


def _b_fold(acc, g):
    lo = _half_mask((2 * CHUNK, 128), 0)
    t = jnp.where(lo, acc[:2 * CHUNK], 0.0) + jnp.where(lo, 0.0, acc[2 * CHUNK:])
    return jnp.where(_half_mask((2 * CHUNK, 128), g), t + pltpu.roll(t, 64, 1), 0.0)


def _b_kv_specs(S):
    prev = lambda n: (jnp.maximum(n - 1, 0), 0)
    cur = lambda n: (n, 0)
    v_col = (2 * B_WIDTH + B_KV_WIDTH) // 128
    return [pl.BlockSpec((CHUNK, 128), prev), pl.BlockSpec((CHUNK, 128), cur),
            pl.BlockSpec((CHUNK, 128), lambda n: (jnp.maximum(n - 1, 0), v_col)), pl.BlockSpec((CHUNK, 128), lambda n: (n, v_col))]


def _b_attn_fwd(qn, kn, proj, sinks):
    S = qn.shape[0]

    def body(s_ref, q_ref, kp_ref, kc_ref, vp_ref, vc_ref, y_ref):
        n = pl.program_id(0)
        valid = _b_valid(n)
        k2 = jnp.concatenate([kp_ref[...], kc_ref[...]], axis=0).astype(F32)
        v2 = jnp.concatenate([vp_ref[...], vc_ref[...]], axis=0)
        for g in range(B_KV_HEADS):
            pairs = [g * PAIRS_PER_GROUP + pp for pp in range(PAIRS_PER_GROUP)]
            qs = jnp.concatenate([q_ref[:, p * 128:(p + 1) * 128] for p in pairs], axis=0)
            probs = _b_probs(qs, _b_blocks(k2, g), valid, [_b_sink_col(s_ref, g, hf) for hf in range(2)])
            pcat = jnp.concatenate([probs[0][0].astype(BF), probs[1][0].astype(BF)], axis=1)
            o = jnp.dot(pcat, _b_blocks(v2, g), preferred_element_type=F32)
            for pp, p in enumerate(pairs):
                y_ref[:, p * 128:(p + 1) * 128] = o[pp * CHUNK:(pp + 1) * CHUNK].astype(BF)

    return pl.pallas_call(
        body, name="b_attn_fwd", grid=(S // CHUNK,),
        in_specs=[pl.BlockSpec(memory_space=pltpu.SMEM), pl.BlockSpec((CHUNK, B_WIDTH), lambda n: (n, 0))] + _b_kv_specs(S),
        out_specs=pl.BlockSpec((CHUNK, B_WIDTH), lambda n: (n, 0)),
        out_shape=jax.ShapeDtypeStruct((S, B_WIDTH), BF), compiler_params=_params(("arbitrary",)),
    )(sinks, qn, kn, kn, proj, proj)


def _b_attn_bwd(qn, kn, proj, sinks, dy, after=()):
    S = qn.shape[0]

    def body(s_ref, q_ref, kp_ref, kc_ref, vp_ref, vc_ref, dy_ref, dq_ref, dk_ref, dv_ref, ds_ref):
        n = pl.program_id(0)

        @pl.when(n == 0)
        def _():
            dk_ref[...] = jnp.zeros_like(dk_ref)
            dv_ref[...] = jnp.zeros_like(dv_ref)
            ds_ref[...] = jnp.zeros_like(ds_ref)

        valid = _b_valid(n)
        k2 = jnp.concatenate([kp_ref[...], kc_ref[...]], axis=0).astype(F32)
        v2 = jnp.concatenate([vp_ref[...], vc_ref[...]], axis=0)
        lane = lax.broadcasted_iota(jnp.int32, (CHUNK, 128), 1)
        dk2 = jnp.zeros((2 * CHUNK, 128), F32)
        dv2 = jnp.zeros((2 * CHUNK, 128), F32)
        dsink = jnp.zeros((CHUNK, 128), F32)
        scale = B_HEAD_DIM ** -0.5
        nt = (((1,), (1,)), ((), ()))
        tn = (((0,), (0,)), ((), ()))
        for g in range(B_KV_HEADS):
            pairs = [g * PAIRS_PER_GROUP + pp for pp in range(PAIRS_PER_GROUP)]
            qs = jnp.concatenate([q_ref[:, p * 128:(p + 1) * 128] for p in pairs], axis=0)
            do = jnp.concatenate([dy_ref[:, p * 128:(p + 1) * 128] for p in pairs], axis=0)
            do_b = do.astype(BF)
            kblk, vblk = _b_blocks(k2, g), _b_blocks(v2, g)
            probs = _b_probs(qs, kblk, valid, [_b_sink_col(s_ref, g, hf) for hf in range(2)])
            pcat = jnp.concatenate([probs[0][0].astype(BF), probs[1][0].astype(BF)], axis=1)
            o = jnp.dot(pcat, vblk, preferred_element_type=F32)
            dp = lax.dot_general(do_b, vblk, nt, preferred_element_type=F32)
            prod = do * o
            ds_halves = []
            for hf in range(2):
                pr, ps = probs[hf]
                delta = jnp.sum(jnp.where(_half_mask(prod.shape, hf), prod, 0.0), axis=-1, keepdims=True)
                ds_halves.append((pr * (dp[:, hf * 2 * CHUNK:(hf + 1) * 2 * CHUNK] - delta) * scale).astype(BF))
                t = -ps * delta
                for pp, p in enumerate(pairs):
                    dsink = dsink + jnp.where(lane == 2 * p + hf, t[pp * CHUNK:(pp + 1) * CHUNK], 0.0)
            dsc = jnp.concatenate(ds_halves, axis=1)
            dq = jnp.dot(dsc, kblk, preferred_element_type=F32)
            for pp, p in enumerate(pairs):
                dq_ref[:, p * 128:(p + 1) * 128] = dq[pp * CHUNK:(pp + 1) * CHUNK]
            dk2 = dk2 + _b_fold(lax.dot_general(dsc, qs, tn, preferred_element_type=F32), g)
            dv2 = dv2 + _b_fold(lax.dot_general(pcat, do_b, tn, preferred_element_type=F32), g)
        ds_ref[...] += dsink
        cur = pl.ds(pl.multiple_of(n * CHUNK, CHUNK), CHUNK)
        dk_ref[cur, :] += dk2[CHUNK:]
        dv_ref[cur, :] += dv2[CHUNK:]

        @pl.when(n > 0)
        def _():
            prv = pl.ds(pl.multiple_of((n - 1) * CHUNK, CHUNK), CHUNK)
            dk_ref[prv, :] += dk2[:CHUNK]
            dv_ref[prv, :] += dv2[:CHUNK]

    full = pl.BlockSpec((S, 128), lambda n: (0, 0))
    return pl.pallas_call(
        _hide(body, 7, len(after)), name="b_attn_bwd", grid=(S // CHUNK,),
        in_specs=[pl.BlockSpec(memory_space=pltpu.SMEM), pl.BlockSpec((CHUNK, B_WIDTH), lambda n: (n, 0))] + _b_kv_specs(S)
        + [pl.BlockSpec((CHUNK, B_WIDTH), lambda n: (n, 0))] + _hidden_specs(after),
        out_specs=[pl.BlockSpec((CHUNK, B_WIDTH), lambda n: (n, 0)), full, full, pl.BlockSpec((CHUNK, 128), lambda n: (0, 0))],
        out_shape=[jax.ShapeDtypeStruct((S, B_WIDTH), F32), jax.ShapeDtypeStruct((S, 128), F32), jax.ShapeDtypeStruct((S, 128), F32),
                   jax.ShapeDtypeStruct((CHUNK, 128), F32)],
        compiler_params=_params(("arbitrary",)),
    )(sinks, qn, kn, kn, proj, proj, dy, *after)


def _c_block(q, k, v, gq, gk):
    qn = q * lax.rsqrt(jnp.mean(q * q, axis=-1, keepdims=True) + EPS) * gq
    kn = k * lax.rsqrt(jnp.mean(k * k, axis=-1, keepdims=True) + EPS) * gk
    s = lax.dot_general(qn.astype(BF), kn.astype(BF), (((1,), (1,)), ((), ())), preferred_element_type=F32) * (C_HEAD_DIM ** -0.5)
    p = jax.nn.softmax(s, axis=-1)
    return jnp.dot(p.astype(BF), v.astype(BF), preferred_element_type=F32)


def _c_specs(S, M, tq):
    q_col = (2 * A_WIDTH + B_WIDTH + 2 * B_KV_WIDTH) // 128
    return [pl.BlockSpec((tq, 128), lambda h, i: (i, q_col + h)), pl.BlockSpec((M, 128), lambda h, i: (0, h)),
            pl.BlockSpec((M, 128), lambda h, i: (0, C_HEADS + h)), pl.BlockSpec((1, 128), lambda h, i: (0, 0)),
            pl.BlockSpec((1, 128), lambda h, i: (0, 0))]


def _c_fwd(proj, kv, gq, gk):
    S, M = proj.shape[0], kv.shape[0]
    tq = _pick(S, (512,))

    def body(q_ref, k_ref, v_ref, gq_ref, gk_ref, y_ref):
        y_ref[...] = _c_block(q_ref[...], k_ref[...], v_ref[...], gq_ref[...], gk_ref[...]).astype(BF)

    return pl.pallas_call(
        body, name="c_fwd", grid=(C_HEADS, S // tq), in_specs=_c_specs(S, M, tq),
        out_specs=pl.BlockSpec((tq, 128), lambda h, i: (i, h)),
        out_shape=jax.ShapeDtypeStruct((S, C_WIDTH), BF), compiler_params=_params(("parallel", "parallel")),
    )(proj, kv, kv, gq, gk)


def _c_bwd(proj, kv, gq, gk, dy):
    S, M = proj.shape[0], kv.shape[0]
    tq = _pick(S, (512,))

    def body(q_ref, k_ref, v_ref, gq_ref, gk_ref, dy_ref, dq_ref, dk_ref, dv_ref, dgq_ref, dgk_ref):
        i = pl.program_id(1)
        _, vjp = jax.vjp(_c_block, q_ref[...], k_ref[...], v_ref[...], gq_ref[...], gk_ref[...])
        dq, dk, dv, dgq, dgk = vjp(dy_ref[...])
        dq_ref[...] = dq.astype(BF)

        @pl.when(i == 0)
        def _():
            dk_ref[...] = dk
            dv_ref[...] = dv
            dgq_ref[...] = dgq
            dgk_ref[...] = dgk

        @pl.when(i > 0)
        def _():
            dk_ref[...] += dk
            dv_ref[...] += dv
            dgq_ref[...] += dgq
            dgk_ref[...] += dgk

    return pl.pallas_call(
        body, name="c_bwd", grid=(C_HEADS, S // tq),
        in_specs=_c_specs(S, M, tq) + [pl.BlockSpec((tq, 128), lambda h, i: (i, h))],
        out_specs=[pl.BlockSpec((tq, 128), lambda h, i: (i, h)), pl.BlockSpec((M, 128), lambda h, i: (0, h)),
                   pl.BlockSpec((M, 128), lambda h, i: (0, h)), pl.BlockSpec((None, 1, 128), lambda h, i: (h, 0, 0)),
                   pl.BlockSpec((None, 1, 128), lambda h, i: (h, 0, 0))],
        out_shape=[jax.ShapeDtypeStruct((S, C_WIDTH), BF), jax.ShapeDtypeStruct((M, C_WIDTH), F32), jax.ShapeDtypeStruct((M, C_WIDTH), F32),
                   jax.ShapeDtypeStruct((C_HEADS, 1, 128), F32), jax.ShapeDtypeStruct((C_HEADS, 1, 128), F32)],
        compiler_params=_params(("parallel", "arbitrary")),
    )(proj, kv, kv, gq, gk, dy)


def _merge_specs(S, D, tm, tn, ks):
    off = GATE_OFF // tn
    nd = D // tn
    gates = [pl.BlockSpec((tm, tn), functools.partial(lambda b, m, n: (m, off + b * nd + n), b)) for b in range(3)]
    ys = [pl.BlockSpec((tm, k), lambda m, n: (m, 0)) for k in ks]
    ws = [pl.BlockSpec((None, k, tn), lambda m, n: (n, 0, 0)) for k in ks]
    return gates, ys, ws


def _merge_fwd(proj, ys, ws):
    S = proj.shape[0]
    tn = ws[0].shape[2]
    D = N_DEV * tn
    ks = [w.shape[1] for w in ws]
    tm = _pick(S, (1024,))
    gates, y_specs, w_specs = _merge_specs(S, D, tm, tn, ks)

    def body(ga_ref, gb_ref, gc_ref, ya_ref, yb_ref, yc_ref, wa_ref, wb_ref, wc_ref, m_ref, za_ref, zb_ref, zc_ref):
        acc = None
        for g_ref, y_ref, w_ref, z_ref in ((ga_ref, ya_ref, wa_ref, za_ref), (gb_ref, yb_ref, wb_ref, zb_ref),
                                           (gc_ref, yc_ref, wc_ref, zc_ref)):
            z = jnp.dot(y_ref[...], w_ref[...], preferred_element_type=F32)
            z_ref[...] = z.astype(BF)
            t = jax.nn.sigmoid(g_ref[...]) * z
            acc = t if acc is None else acc + t
        m_ref[...] = acc.astype(BF)

    tile = pl.BlockSpec((tm, tn), lambda m, n: (m, n))
    return pl.pallas_call(
        body, name="merge_fwd", grid=(S // tm, D // tn), in_specs=gates + y_specs + w_specs,
        out_specs=[tile, tile, tile, tile], out_shape=[jax.ShapeDtypeStruct((S, D), BF)] * 4,
        compiler_params=_params(("parallel", "parallel")),
    )(proj, proj, proj, *ys, *ws)


def _merge_bwd(proj, zs, dm, ws, after=()):
    S = proj.shape[0]
    tn = ws[0].shape[2]
    D = N_DEV * tn
    ks = [w.shape[1] for w in ws]
    tm = _pick(S, (1024,))
    gates, _, w_specs = _merge_specs(S, D, tm, tn, ks)
    nt = (((1,), (1,)), ((), ()))

    def body(ga_ref, gb_ref, gc_ref, za_ref, zb_ref, zc_ref, dm_ref, wa_ref, wb_ref, wc_ref,
             dza_ref, dzb_ref, dzc_ref, dga_ref, dgb_ref, dgc_ref, dya_ref, dyb_ref, dyc_ref):
        n = pl.program_id(1)
        dmv = dm_ref[...]
        for g_ref, z_ref, w_ref, dz_ref, dg_ref, dy_ref in (
                (ga_ref, za_ref, wa_ref, dza_ref, dga_ref, dya_ref), (gb_ref, zb_ref, wb_ref, dzb_ref, dgb_ref, dyb_ref),
                (gc_ref, zc_ref, wc_ref, dzc_ref, dgc_ref, dyc_ref)):
            sg = jax.nn.sigmoid(g_ref[...])
            dz = (sg * dmv).astype(BF)
            dz_ref[...] = dz
            dg_ref[...] = (dmv * z_ref[...].astype(F32) * sg * (1.0 - sg)).astype(BF)
            part = lax.dot_general(dz, w_ref[...], nt, preferred_element_type=F32)

            @pl.when(n == 0)
            def _():
                dy_ref[...] = part

            @pl.when(n > 0)
            def _():
                dy_ref[...] += part

    tile = pl.BlockSpec((tm, tn), lambda m, n: (m, n))
    dys = [pl.BlockSpec((tm, k), lambda m, n: (m, 0)) for k in ks]
    return pl.pallas_call(
        _hide(body, 10, len(after)), name="merge_bwd", grid=(S // tm, D // tn),
        in_specs=gates + [tile, tile, tile, tile] + w_specs + _hidden_specs(after),
        out_specs=[tile] * 6 + dys,
        out_shape=[jax.ShapeDtypeStruct((S, D), BF)] * 6 + [jax.ShapeDtypeStruct((S, k), F32) for k in ks],
        compiler_params=_params(("parallel", "arbitrary")),
    )(proj, proj, proj, *zs, dm, *ws, *after)


PAD = 8


def _stage_shift_down(us_ref, u_ref):
    S = u_ref.shape[1]
    us_ref[:, 0:PAD, :] = jnp.zeros((2, PAD, us_ref.shape[2]), F32)
    us_ref[:, PAD:S + PAD, :] = u_ref[...].astype(F32)


ROWS = 64


def _conv3(us_ref, part, r0, w, b):
    return (us_ref[part, pl.ds(r0 + PAD, ROWS), :] * w[2:3] + us_ref[part, pl.ds(r0 + PAD - 1, ROWS), :] * w[1:2]
            + us_ref[part, pl.ds(r0 + PAD - 2, ROWS), :] * w[0:1] + b)


def _ffn_specs(S, F, tc, c):
    per = c // tc

    def w_spec(half):
        return pl.BlockSpec((None, 3, tc), lambda j: (half * (N_DEV // 2) + j // per, 0, j % per))

    return [pl.BlockSpec((2, S, tc), lambda j: (0, 0, j)), w_spec(0), w_spec(1), pl.BlockSpec((2, 1, tc), lambda j: (0, 0, j))]


def _ffn_tile(F, c):
    tc = 128
    if c % tc or F % tc:
        raise ValueError(f"ffn tile {tc} does not divide {c}, {F}")
    return tc


def _ffn_act_fwd(up3, cws, cb3):
    _, S, F = up3.shape
    c = cws.shape[2]
    tc = _ffn_tile(F, c)

    def body(u_ref, wa_ref, wb_ref, b_ref, o_ref, us_ref):
        _stage_shift_down(us_ref, u_ref)
        wa, wb, ba, bb = wa_ref[...], wb_ref[...], b_ref[0], b_ref[1]

        def step(i, carry):
            r0 = pl.multiple_of(i * ROWS, ROWS)
            ca = _conv3(us_ref, 0, r0, wa, ba)
            cb = _conv3(us_ref, 1, r0, wb, bb)
            o_ref[pl.ds(r0, ROWS), :] = (ca * jax.nn.sigmoid(ca) * cb).astype(BF)
            return carry

        lax.fori_loop(0, S // ROWS, step, 0, unroll=4)

    return pl.pallas_call(
        body, name="ffn_act_fwd", grid=(F // tc,), in_specs=_ffn_specs(S, F, tc, c),
        out_specs=pl.BlockSpec((S, tc), lambda j: (0, j)), out_shape=jax.ShapeDtypeStruct((S, F), BF),
        scratch_shapes=[pltpu.VMEM((2, S + PAD, tc), F32)],
        compiler_params=_params(("parallel",)),
    )(up3, cws, cws, cb3)


def _ffn_act_bwd(up3, cws, cb3, dact, after=()):
    _, S, F = up3.shape
    c = cws.shape[2]
    tc = _ffn_tile(F, c)

    def body(u_ref, wa_ref, wb_ref, b_ref, da_ref, du_ref, dw_ref, db_ref, us_ref, dcs_ref):
        _stage_shift_down(us_ref, u_ref)
        ws = (wa_ref[...], wb_ref[...])
        ba, bb = b_ref[0], b_ref[1]
        dcs_ref[:, S:S + PAD, :] = jnp.zeros((2, PAD, tc), F32)

        def conv_grads(i, carry):
            r0 = pl.multiple_of(i * ROWS, ROWS)
            ca = _conv3(us_ref, 0, r0, ws[0], ba)
            cb = _conv3(us_ref, 1, r0, ws[1], bb)
            sg = jax.nn.sigmoid(ca)
            dav = da_ref[pl.ds(r0, ROWS), :].astype(F32)
            dcs_ref[0, pl.ds(r0, ROWS), :] = dav * cb * sg * (1.0 + ca * (1.0 - sg))
            dcs_ref[1, pl.ds(r0, ROWS), :] = dav * ca * sg
            return carry

        lax.fori_loop(0, S // ROWS, conv_grads, 0, unroll=4)

        def fold(v):
            return jnp.sum(v.reshape(ROWS // 8, 8, tc), axis=0)

        def input_grads(i, acc):
            r0 = pl.multiple_of(i * ROWS, ROWS)
            new = []
            for part in range(2):
                w = ws[part]
                dc = dcs_ref[part, pl.ds(r0, ROWS), :]
                dc1 = dcs_ref[part, pl.ds(r0 + 1, ROWS), :]
                dc2 = dcs_ref[part, pl.ds(r0 + 2, ROWS), :]
                u = us_ref[part, pl.ds(r0 + PAD, ROWS), :]
                du_ref[part, pl.ds(r0, ROWS), :] = (dc * w[2:3] + dc1 * w[1:2] + dc2 * w[0:1]).astype(BF)
                sums = (fold(dc2 * u), fold(dc1 * u), fold(dc * u), fold(dc))
                new += [a + s for a, s in zip(acc[4 * part:4 * part + 4], sums)]
            return tuple(new)

        acc = lax.fori_loop(0, S // ROWS, input_grads, tuple(jnp.zeros((8, tc), F32) for _ in range(8)), unroll=4)
        for part in range(2):
            for j in range(3):
                dw_ref[part, j:j + 1, :] = jnp.sum(acc[4 * part + j], axis=0, keepdims=True)
            db_ref[part] = jnp.sum(acc[4 * part + 3], axis=0, keepdims=True)

    return pl.pallas_call(
        _hide(body, 5, len(after)), name="ffn_act_bwd", grid=(F // tc,),
        in_specs=_ffn_specs(S, F, tc, c) + [pl.BlockSpec((S, tc), lambda j: (0, j))] + _hidden_specs(after),
        out_specs=[pl.BlockSpec((2, S, tc), lambda j: (0, 0, j)), pl.BlockSpec((2, 3, tc), lambda j: (0, 0, j)),
                   pl.BlockSpec((2, 1, tc), lambda j: (0, 0, j))],
        out_shape=[jax.ShapeDtypeStruct((2, S, F), BF), jax.ShapeDtypeStruct((2, 3, F), F32), jax.ShapeDtypeStruct((2, 1, F), F32)],
        scratch_shapes=[pltpu.VMEM((2, S + PAD, tc), F32), pltpu.VMEM((2, S + PAD, tc), F32)],
        compiler_params=_params(("parallel",)),
    )(up3, cws, cws, cb3, dact, *after)


def _residual_rms(a, w, x, g, name, tm=512):
    S, K = a.shape
    D = w.shape[1]
    tm = _pick(S, (tm,))

    def body(a_ref, w_ref, x_ref, g_ref, x1_ref, h_ref, r_ref):
        x1 = jnp.dot(a_ref[...], w_ref[...], preferred_element_type=F32) + x_ref[...]
        r = lax.rsqrt(jnp.mean(x1 * x1, axis=-1, keepdims=True) + EPS)
        x1_ref[...] = x1
        h_ref[...] = (x1 * r * g_ref[...]).astype(BF)
        r_ref[...] = r

    row = pl.BlockSpec((tm, D), lambda i: (i, 0))
    return pl.pallas_call(
        body, name=name, grid=(S // tm,),
        in_specs=[pl.BlockSpec((tm, K), lambda i: (i, 0)), pl.BlockSpec((K, D), lambda i: (0, 0)), row, pl.BlockSpec((1, D), lambda i: (0, 0))],
        out_specs=[row, row, pl.BlockSpec((tm, 1), lambda i: (i, 0))],
        out_shape=[jax.ShapeDtypeStruct((S, D), F32), jax.ShapeDtypeStruct((S, D), BF), jax.ShapeDtypeStruct((S, 1), F32)],
        compiler_params=_params(("parallel",)),
    )(a, w, x, g)


def _out_loss(act, w_down, x1, target, tm=1024, tn=1024, tk=1408):
    S, F = act.shape
    D = w_down.shape[1]
    tm, tn, tk = _pick(S, (tm,)), _pick(D, (tn,)), _pick(F, (tk,))
    nm, nn, nk = S // tm, D // tn, F // tk

    def body(a_ref, b_ref, x_ref, t_ref, dy_ref, dyb_ref, l_ref, acc):
        m, n, k = pl.program_id(0), pl.program_id(1), pl.program_id(2)

        @pl.when((m == 0) & (n == 0) & (k == 0))
        def _():
            l_ref[...] = jnp.zeros_like(l_ref)

        @pl.when(k == 0)
        def _():
            acc[...] = jnp.zeros_like(acc)

        acc[...] += jnp.dot(a_ref[...], b_ref[...], preferred_element_type=F32)

        @pl.when(k == nk - 1)
        def _():
            e = acc[...] + x_ref[...] - t_ref[...]
            dy = e * (1.0 / D)
            dy_ref[...] = dy
            dyb_ref[...] = dy.astype(BF)
            l_ref[...] += jnp.sum(jnp.sum(e * e, axis=-1, keepdims=True), axis=0, keepdims=True) * (0.5 / D)

    tile = pl.BlockSpec((tm, tn), lambda m, n, k: (m, n))
    return pl.pallas_call(
        body, name="mm_y_loss", grid=(nm, nn, nk),
        in_specs=[pl.BlockSpec((tm, tk), lambda m, n, k: (m, k)), pl.BlockSpec((tk, tn), lambda m, n, k: (k, n)), tile, tile],
        out_specs=[tile, tile, pl.BlockSpec((8, 128), lambda m, n, k: (0, 0))],
        out_shape=[jax.ShapeDtypeStruct((S, D), F32), jax.ShapeDtypeStruct((S, D), BF), jax.ShapeDtypeStruct((8, 128), F32)],
        scratch_shapes=[pltpu.VMEM((tm, tn), F32)],
        compiler_params=_params(("arbitrary", "arbitrary", "arbitrary")),
    )(act, w_down, x1, target)


def _allgather(shards, name):
    n = len(shards)

    def body(*refs):
        ins, outs = refs[:n], refs[n:2 * n]
        send_sems, recv_sems, local_sems = refs[2 * n:]
        x, y, c = lax.axis_index("x"), lax.axis_index("y"), lax.axis_index("c")
        me, sibling = (x, y, c), (x, y, 1 - c)
        chips = [(1 - x, y), (x, 1 - y), (1 - x, 1 - y)]

        def blk(w, px, py, pc):
            return outs[w].at[4 * px + 2 * py + pc]

        def copy(w, k, block, to, src=None):
            return pltpu.make_async_remote_copy(
                src_ref=blk(w, *block) if src is None else src, dst_ref=blk(w, *block),
                send_sem=send_sems.at[w, k], recv_sem=recv_sems.at[w, k], device_id=to, device_id_type=MESH)

        started = []
        mine = []
        for w in range(n):
            mine.append(pltpu.make_async_copy(ins[w], blk(w, *me), local_sems.at[w]))
            mine[-1].start()
            first = [copy(w, 0, me, sibling, src=ins[w])]
            first += [copy(w, 1 + j, me, (*chip, c), src=ins[w]) for j, chip in enumerate(chips)]
            for cp in first:
                cp.start()
            started += first
        for w in range(n):
            for j, chip in enumerate(chips):
                copy(w, 1 + j, (*chip, c), me).wait_recv()
                fwd = copy(w, 4 + j, (*chip, c), sibling)
                fwd.start()
                started.append(fwd)
        for w in range(n):
            copy(w, 0, sibling, me).wait_recv()
            for j, chip in enumerate(chips):
                copy(w, 4 + j, (*chip, 1 - c), me).wait_recv()
        for cp in started:
            cp.wait_send()
        for cp in mine:
            cp.wait()

    whole = pl.BlockSpec(memory_space=pltpu.VMEM)
    outs = pl.pallas_call(
        body, name=name, in_specs=[whole] * n, out_specs=[whole] * n,
        out_shape=[jax.ShapeDtypeStruct((N_DEV,) + s.shape, s.dtype) for s in shards],
        scratch_shapes=[pltpu.SemaphoreType.DMA((n, 7)), pltpu.SemaphoreType.DMA((n, 7)), pltpu.SemaphoreType.DMA((n,))],
    )(*shards)
    return list(outs)


def _allgather_seq(shards, name, collective_id, after=()):
    n = len(shards)
    n_after = len(after)

    halves = [s.shape[0] % 32 == 0 for s in shards]
    n_sem = 8
    to_diagonal = not all(halves)

    def body(*refs):
        ins, outs = refs[:n], refs[n + n_after:2 * n + n_after]
        send_sems, recv_sems, local_sems = refs[2 * n + n_after:]
        x, y, c = lax.axis_index("x"), lax.axis_index("y"), lax.axis_index("c")
        me, sibling = (x, y, c), (x, y, 1 - c)
        x_nb, y_nb, diag = (1 - x, y, c), (x, 1 - y, c), (1 - x, 1 - y, c)
        peers = [sibling, x_nb, y_nb] + ([diag] if to_diagonal else [])
        barrier = pltpu.get_barrier_semaphore()
        for peer in peers:
            pl.semaphore_signal(barrier, inc=1, device_id=peer, device_id_type=MESH)
        pl.semaphore_wait(barrier, len(peers))

        def blk(w, dev, rows=None):
            ref = outs[w].at[4 * dev[0] + 2 * dev[1] + dev[2]]
            return ref if rows is None else ref.at[rows]

        def copy(w, k, block, to, src=None, rows=None):
            return pltpu.make_async_remote_copy(
                src_ref=blk(w, block, rows) if src is None else src, dst_ref=blk(w, block, rows),
                send_sem=send_sems.at[n_sem * w + k], recv_sem=recv_sems.at[n_sem * w + k], device_id=to, device_id_type=MESH)

        def top(w):
            return pl.ds(0, shards[w].shape[0] // 2)

        def bottom(w):
            return pl.ds(shards[w].shape[0] // 2, shards[w].shape[0] // 2)

        started = []
        mine = []
        for w in range(n):
            mine.append(pltpu.make_async_copy(ins[w], blk(w, me), local_sems.at[w]))
            mine[-1].start()
            first = [copy(w, 0, me, sibling, src=ins[w]), copy(w, 1, me, x_nb, src=ins[w]), copy(w, 2, me, y_nb, src=ins[w])]
            if not halves[w]:
                first.append(copy(w, 3, me, diag, src=ins[w]))
            for cp in first:
                cp.start()
            started += first
        for w in range(n):
            copy(w, 1, x_nb, me).wait_recv()
            onward = [copy(w, 5, x_nb, sibling)] + ([copy(w, 3, x_nb, y_nb, rows=top(w))] if halves[w] else [])
            copy(w, 2, y_nb, me).wait_recv()
            onward += [copy(w, 6, y_nb, sibling)] + ([copy(w, 4, y_nb, x_nb, rows=bottom(w))] if halves[w] else [])
            for cp in onward:
                cp.start()
            started += onward
        for w in range(n):
            if halves[w]:
                copy(w, 3, diag, me, rows=top(w)).wait_recv()
                copy(w, 4, diag, me, rows=bottom(w)).wait_recv()
            else:
                copy(w, 3, diag, me).wait_recv()
            fwd = copy(w, 7, diag, sibling)
            fwd.start()
            started.append(fwd)
        for w in range(n):
            for k, dev in ((0, sibling), (5, (1 - x, y, 1 - c)), (6, (x, 1 - y, 1 - c)), (7, (1 - x, 1 - y, 1 - c))):
                copy(w, k, dev, me).wait_recv()
        for cp in started:
            cp.wait_send()
        for cp in mine:
            cp.wait()

    outs = pl.kernel(
        body, name=name, out_type=[jax.ShapeDtypeStruct((N_DEV,) + s.shape, s.dtype) for s in shards],
        mesh=plsc.ScalarSubcoreMesh(axis_name="seq", num_cores=1),
        scratch_types=[pltpu.SemaphoreType.DMA((n_sem * n,)), pltpu.SemaphoreType.DMA((n_sem * n,)), pltpu.SemaphoreType.DMA((n,))],
        compiler_params=pltpu.CompilerParams(collective_id=collective_id),
    )(*shards, *after)
    return list(outs)


def _chip_exchange(sums, name, collective_id):
    n = len(sums)

    def body(*refs):
        ins, outs = refs[:n], refs[n:2 * n]
        send_sems, recv_sems = refs[2 * n:]
        x, y, c = lax.axis_index("x"), lax.axis_index("y"), lax.axis_index("c")
        chips = [(1 - x, y), (x, 1 - y), (1 - x, 1 - y)]
        barrier = pltpu.get_barrier_semaphore()
        for px, py in chips:
            pl.semaphore_signal(barrier, inc=1, device_id=(px, py, c), device_id_type=MESH)
        pl.semaphore_wait(barrier, 3)
        copies = []
        for w in range(n):
            for k, (px, py) in enumerate(chips):
                copies.append(pltpu.make_async_remote_copy(
                    src_ref=ins[w].at[2 * px + py], dst_ref=outs[w].at[k], send_sem=send_sems.at[3 * w + k],
                    recv_sem=recv_sems.at[3 * w + k], device_id=(px, py, c), device_id_type=MESH))
        for cp in copies:
            cp.start()
        for cp in copies:
            cp.wait()

    outs = pl.kernel(
        body, name=name, out_type=[jax.ShapeDtypeStruct((3,) + s.shape[1:], s.dtype) for s in sums],
        mesh=plsc.ScalarSubcoreMesh(axis_name="seq", num_cores=1),
        scratch_types=[pltpu.SemaphoreType.DMA((3 * n,)), pltpu.SemaphoreType.DMA((3 * n,))],
        compiler_params=pltpu.CompilerParams(collective_id=collective_id),
    )(*sums)
    return list(outs)


def _row_tile(r, c, elems=256 * 1024):
    want = max(8, elems // c)
    for t in range(min(want, r) // 8 * 8, 0, -8):
        if r % t == 0:
            return t
    return r


def _pair_add(g4, recv, core, name, after=()):
    _, _, r, c = g4.shape
    tr = _row_tile(r, c, 1024 * 1024)

    def body(core_ref, a_ref, b_ref, o_ref):
        o_ref[...] = (a_ref[...].astype(F32) + b_ref[...].astype(F32)).astype(BF)

    return pl.pallas_call(
        _hide(body, 3, len(after)), name=name,
        grid_spec=pltpu.PrefetchScalarGridSpec(
            num_scalar_prefetch=1, grid=(4, r // tr),
            in_specs=[pl.BlockSpec((None, None, tr, c), lambda p, i, s: (p, s[0], i, 0)),
                      pl.BlockSpec((None, tr, c), lambda p, i, s: (p, i, 0))] + _hidden_specs(after),
            out_specs=pl.BlockSpec((None, tr, c), lambda p, i, s: (p, i, 0))),
        out_shape=jax.ShapeDtypeStruct((4, r, c), BF), compiler_params=_params(("parallel", "parallel")),
    )(core, g4, recv, *after)


def _adam_math(w, g, m, v):
    m = ADAM_B1 * m + (1.0 - ADAM_B1) * g
    v = ADAM_B2 * v + (1.0 - ADAM_B2) * (g * g)
    m_hat = m / (1.0 - ADAM_B1 ** ADAM_STEP)
    v_hat = v / (1.0 - ADAM_B2 ** ADAM_STEP)
    delta = -ADAM_LR * (m_hat / (jnp.sqrt(v_hat) + ADAM_EPS) + ADAM_WD * w)
    return delta, m, v


def _adamw_big(sums, recv, chip, w, m, v, name, after=()):
    r, c = w.shape
    tr = _row_tile(r, c, 512 * 1024)

    def body(chip_ref, s_ref, r_ref, w_ref, m_ref, v_ref, g_out, d_out, m_out, v_out):
        g = s_ref[...].astype(F32) + r_ref[0].astype(F32)
        g = g + r_ref[1].astype(F32)
        g = g + r_ref[2].astype(F32)
        delta, mn, vn = _adam_math(w_ref[...], g, m_ref[...], v_ref[...])
        g_out[...] = g
        d_out[...] = delta
        m_out[...] = mn
        v_out[...] = vn

    row = pl.BlockSpec((tr, c), lambda i, s: (i, 0))
    return pl.pallas_call(
        _hide(body, 6, len(after)), name=name,
        grid_spec=pltpu.PrefetchScalarGridSpec(
            num_scalar_prefetch=1, grid=(r // tr,),
            in_specs=[pl.BlockSpec((None, tr, c), lambda i, s: (s[0], i, 0)), pl.BlockSpec((3, tr, c), lambda i, s: (0, i, 0)),
                      row, row, row] + _hidden_specs(after),
            out_specs=[row, row, row, row]),
        out_shape=[jax.ShapeDtypeStruct((r, c), F32)] * 4, compiler_params=_params(("parallel",)),
    )(chip, sums, recv, w, m, v, *after)


def _adamw_small(parts, ws, ms, vs, extra_parts, name):
    n, ne = len(ws), len(extra_parts)

    def total(p_ref):
        g = p_ref[0]
        for d in range(1, N_DEV):
            g = g + p_ref[d]
        return g

    def body(*refs):
        p_refs, w_refs, m_refs, v_refs = refs[:n], refs[n:2 * n], refs[2 * n:3 * n], refs[3 * n:4 * n]
        e_refs = refs[4 * n:4 * n + ne]
        outs = refs[4 * n + ne:]
        for i in range(n):
            g = total(p_refs[i])
            delta, mn, vn = _adam_math(w_refs[i][...], g, m_refs[i][...], v_refs[i][...])
            outs[4 * i][...] = g
            outs[4 * i + 1][...] = delta
            outs[4 * i + 2][...] = mn
            outs[4 * i + 3][...] = vn
        for i in range(ne):
            outs[4 * n + i][...] = total(e_refs[i])

    out_shape = []
    for w in ws:
        out_shape += [jax.ShapeDtypeStruct(w.shape, F32)] * 4
    out_shape += [jax.ShapeDtypeStruct(e.shape[1:], F32) for e in extra_parts]
    res = pl.pallas_call(body, name=name, out_shape=out_shape,
                         compiler_params=pltpu.CompilerParams(vmem_limit_bytes=VMEM_LIMIT))(*parts, *ws, *ms, *vs, *extra_parts)
    return [res[4 * i:4 * i + 4] for i in range(n)], list(res[4 * n:])


def _adamw_plain(g, w, m, v, name):
    def body(g_ref, w_ref, m_ref, v_ref, d_out, m_out, v_out):
        delta, mn, vn = _adam_math(w_ref[...], g_ref[...], m_ref[...], v_ref[...])
        d_out[...] = delta
        m_out[...] = mn
        v_out[...] = vn

    return pl.pallas_call(body, name=name, out_shape=[jax.ShapeDtypeStruct(w.shape, F32)] * 3)(g, w, m, v)


def kernel(x, mem, positions, g_mix, w_in, g_a_v, w_spatial, b_spatial, g_b_q, g_b_k, sinks, g_mem, w_mem_kv, g_c_q, g_c_k, w_branch_a, w_branch_b, w_branch_c, w_out, g_ffn, w_up, conv_w, conv_b, w_down, loss_target, m_g_mix, m_w_in, m_g_a_v, m_w_spatial, m_b_spatial, m_g_b_q, m_g_b_k, m_sinks, m_g_mem, m_w_mem_kv, m_g_c_q, m_g_c_k, m_w_branch_a, m_w_branch_b, m_w_branch_c, m_w_out, m_g_ffn, m_w_up, m_conv_w, m_conv_b, m_w_down, v_g_mix, v_w_in, v_g_a_v, v_w_spatial, v_b_spatial, v_g_b_q, v_g_b_k, v_sinks, v_g_mem, v_w_mem_kv, v_g_c_q, v_g_c_k, v_w_branch_a, v_w_branch_b, v_w_branch_c, v_w_out, v_g_ffn, v_w_up, v_conv_w, v_conv_b, v_w_down):
    S, D = x.shape[1], x.shape[2]
    M = mem.shape[1]
    F = w_down.shape[1] * N_DEV
    in_cols = w_in.shape[2] * N_DEV
    ax, ay, ac = lax.axis_index("x"), lax.axis_index("y"), lax.axis_index("c")
    core = jnp.reshape(ac, (1,)).astype(jnp.int32)
    chip = jnp.reshape(2 * ax + ay, (1,)).astype(jnp.int32)
    me = 4 * ax + 2 * ay + ac

    x2, mem2, tgt2 = x[0], mem[0], loss_target[0]

    big = dict(w_in=w_in[0].T, w_mem_kv=w_mem_kv[0], w_branch_a=w_branch_a[0], w_branch_b=w_branch_b[0],
               w_branch_c=w_branch_c[0], w_out=w_out[0], w_up=w_up[0], w_down=w_down[0])
    names = list(big)
    cast = {k: big[k].astype(BF) for k in names}
    W = {}
    cb3 = conv_b.reshape(2, 1, F)
    W["w_in"], = _allgather_seq([cast["w_in"]], "ag_seq0", 0)
    w_in_t = W["w_in"].reshape(in_cols, D)
    grp1 = ["w_mem_kv", "w_branch_a", "w_branch_b", "w_branch_c", "w_out"]
    res1 = _allgather_seq([cast[k] for k in grp1] + [conv_w[0]], "ag_seq1", 1, after=(_token((w_in_t,), "tok_w_in"),))
    W.update(zip(grp1, res1))
    cw3 = res1[-1]
    w_kv_f = W["w_mem_kv"].reshape(D, 2 * C_WIDTH)
    w_out_f = W["w_out"].reshape(D, D)

    half = ROPE_DIM // 2
    inv = ROPE_THETA ** (-jnp.arange(half, dtype=F32) / half)
    ang = positions[0].astype(F32)[:, None] * inv
    cos, sin = jnp.cos(ang), jnp.sin(ang)
    one, zero = jnp.ones((S, B_HEAD_DIM - ROPE_DIM), F32), jnp.zeros((S, B_HEAD_DIM - ROPE_DIM), F32)
    z8 = jnp.zeros((S, half), F32)
    ct = jnp.tile(jnp.concatenate([cos, cos, one], axis=1), (1, 2))
    sa = jnp.tile(jnp.concatenate([-sin, z8, zero], axis=1), (1, 2))
    sb = jnp.tile(jnp.concatenate([z8, sin, zero], axis=1), (1, 2))
    gq2, gk2 = jnp.tile(g_b_q, (1, 2)), jnp.tile(g_b_k, (1, 2))
    b_t = b_spatial[0].T

    h, rstd1 = _rms_fwd(x2, g_mix, "rms1_fwd")
    proj = _mm(h, w_in_t, "nt", F32, "mm_proj", tn=1280)
    y_a = _a_fwd(proj, g_a_v, w_spatial[0], b_t)
    W["w_up"], = _allgather_seq([cast["w_up"]], "ag_seq2", 2, after=(_token((W["w_out"], proj), "tok_group1"),))
    qn, kn = _b_pre(proj, gq2, gk2, ct, sa, sb)
    y_b = _b_attn_fwd(qn, kn, proj, sinks)
    mem_h, rstd_m = _rms_fwd(mem2, g_mem, "rmsmem_fwd")
    kv = _mm(mem_h, w_kv_f, "nn", F32, "mm_kv", after=(y_b,))
    y_c = _c_fwd(proj, kv, g_c_q, g_c_k)
    w_branches = [W["w_branch_a"], W["w_branch_b"], W["w_branch_c"]]
    merged, z_a, z_b, z_c = _merge_fwd(proj, [y_a, y_b, y_c], w_branches)
    x1, h2, rstd2 = _residual_rms(merged, w_out_f, x2, g_ffn, "mm_x1_rms2")
    W["w_down"], = _allgather_seq([cast["w_down"]], "ag_seq3", 3, after=(W["w_up"], h2))
    w_down_f = W["w_down"].reshape(F, D)
    up3 = _mm(h2, W["w_up"], "nn", BF, "mm_up", b_stack=True, out_parts=2)
    act = _ffn_act_fwd(up3, cw3, cb3)
    dy, dy_b, loss_acc = _out_loss(act, w_down_f, x1, tgt2)

    reduced = {}

    def as4(g):
        return g.reshape(4, 2, g.shape[1], g.shape[2])

    def finish_group(gi, keys, g4, from_sibling):
        sums = [_pair_add(a, b, core, "rs_add_" + k) for k, a, b in zip(keys, g4, from_sibling)]
        from_chips = _chip_exchange(sums, f"rs_chip{gi}", 4 + gi)
        reduced.update(zip(keys, zip(sums, from_chips)))
        return tuple(sums)

    d_act = _mm(dy_b, w_down_f, "nt", BF, "mm_dact", tn=1408)
    g_down = _mm(act, dy_b, "tn", BF, "mm_gdown", tm=1408)
    d_up3, d_cw3, d_cb3 = _ffn_act_bwd(up3, cw3, cb3, d_act, after=(g_down,))
    grp0 = [as4(g_down.reshape(N_DEV, F // N_DEV, D))]
    g_up, sib0 = _mm(h2, d_up3, "tn", BF, "mm_gup", b_parts=2, out_stack=True, exchange=grp0)
    sums0 = finish_group(0, ["w_down"], grp0, sib0)
    grp1 = [as4(g_up)]
    d_h2, sib1 = _mm(d_up3, W["w_up"], "nt", F32, "mm_dh2", a_parts=2, b_stack=True, tm=2048, after=sums0, exchange=grp1)
    sums1 = finish_group(1, ["w_up"], grp1, sib1)
    dx1, dx1_b, d_g_ffn = _rms_bwd(x1, rstd2, g_ffn, d_h2, dy, "rms2_bwd", after=sums1)
    g_out = _mm(merged, dx1_b, "tn", BF, "mm_gout")
    grp2 = [as4(g_out.reshape(N_DEV, D // N_DEV, D))]
    d_merged, sib2 = _mm(dx1_b, w_out_f, "nt", F32, "mm_dmerged", exchange=grp2)
    sums2 = finish_group(2, ["w_out"], grp2, sib2)
    dz_a, dz_b, dz_c, dga, dgb, dgc, dy_a, dy_b_, dy_c = _merge_bwd(proj, [z_a, z_b, z_c], d_merged, w_branches, after=sums2)
    g_ba = _mm(y_a, dz_a, "tn", BF, "mm_gba", out_stack=True)
    g_bb = _mm(y_b, dz_b, "tn", BF, "mm_gbb", out_stack=True)
    g_bc = _mm(y_c, dz_c, "tn", BF, "mm_gbc", out_stack=True)
    d_uv, d_g_a_v, d_w_s, d_b_t = _a_bwd(proj, g_a_v, w_spatial[0], b_t, dy_a, after=(g_ba, g_bb, g_bc))
    dqn, dkn, dv_b, dsink_rows = _b_attn_bwd(qn, kn, proj, sinks, dy_b_)
    d_qkv, d_gq2, d_gk2 = _b_pre_bwd(proj, gq2, gk2, ct, sa, sb, dqn, dkn, dv_b)
    dq_c, dk_c, dv_c, d_gcq, d_gck = _c_bwd(proj, kv, g_c_q, g_c_k, dy_c)
    dkv_b = jnp.concatenate([dk_c, dv_c], axis=1).astype(BF)
    d_memh = _mm(dkv_b, w_kv_f, "nt", F32, "mm_dmemh")
    g_kv = _mm(mem_h, dkv_b, "tn", BF, "mm_gkv")
    _, _, d_g_mem = _rms_bwd(mem2, rstd_m, g_mem, d_memh, None, "rmsmem_bwd")
    dproj = jnp.concatenate([d_uv, d_qkv, dq_c, dga, dgb, dgc], axis=1)
    grp3 = [as4(g_ba), as4(g_bb), as4(g_bc)]
    g_in, sib3 = _mm(dproj, h, "tn", BF, "mm_gin", tm=1280, exchange=grp3)
    sums3 = finish_group(3, ["w_branch_a", "w_branch_b", "w_branch_c"], grp3, sib3)
    grp4 = [as4(g_in.reshape(N_DEV, in_cols // N_DEV, D)), as4(g_kv.reshape(N_DEV, D // N_DEV, 2 * C_WIDTH))]
    d_h, sib4 = _mm(dproj, w_in_t, "nn", F32, "mm_dh", tm=2048, tk=1280, after=sums3, exchange=grp4)
    sums4 = finish_group(4, ["w_in", "w_mem_kv"], grp4, sib4)
    grad_x, _, d_g_mix = _rms_bwd(x2, rstd1, g_mix, d_h, dx1, "rms1_bwd", after=sums4)

    small_names =["g_mix", "g_a_v", "w_spatial", "b_spatial", "g_b_q", "g_b_k", "sinks", "g_mem", "g_c_q", "g_c_k", "g_ffn", "conv_b"]
    small_w = dict(g_mix=g_mix, g_a_v=g_a_v, w_spatial=w_spatial, b_spatial=b_spatial, g_b_q=g_b_q, g_b_k=g_b_k, sinks=sinks,
                   g_mem=g_mem, g_c_q=g_c_q, g_c_k=g_c_k, g_ffn=g_ffn, conv_b=conv_b)
    small_m = dict(g_mix=m_g_mix, g_a_v=m_g_a_v, w_spatial=m_w_spatial, b_spatial=m_b_spatial, g_b_q=m_g_b_q, g_b_k=m_g_b_k,
                   sinks=m_sinks, g_mem=m_g_mem, g_c_q=m_g_c_q, g_c_k=m_g_c_k, g_ffn=m_g_ffn, conv_b=m_conv_b)
    small_v = dict(g_mix=v_g_mix, g_a_v=v_g_a_v, w_spatial=v_w_spatial, b_spatial=v_b_spatial, g_b_q=v_g_b_q, g_b_k=v_g_b_k,
                   sinks=v_sinks, g_mem=v_g_mem, g_c_q=v_g_c_q, g_c_k=v_g_c_k, g_ffn=v_g_ffn, conv_b=v_conv_b)
    small_g = dict(
        g_mix=d_g_mix, g_a_v=d_g_a_v, w_spatial=d_w_s, b_spatial=d_b_t.T,
        g_b_q=d_gq2.reshape(2, B_HEAD_DIM).sum(0), g_b_k=d_gk2.reshape(2, B_HEAD_DIM).sum(0),
        sinks=dsink_rows.sum(0)[:B_HEADS], g_mem=d_g_mem, g_c_q=d_gcq.sum(0), g_c_k=d_gck.sum(0), g_ffn=d_g_ffn,
        conv_b=d_cb3)
    partial = [small_g[k].reshape(small_w[k].shape) for k in small_names] + [d_cw3, loss_acc[0:1]]
    parts = _allgather(partial, "ag_small")
    n_small = len(small_names)
    small_res, (g_cw3, loss_row) = _adamw_small(parts[:n_small], [small_w[k] for k in small_names], [small_m[k] for k in small_names],
                                                [small_v[k] for k in small_names], parts[n_small:], "adamw_small")
    loss = loss_row[0, 0]
    small_out = dict(zip(small_names, small_res))
    c_cw = 2 * F // N_DEV
    g_cw = lax.dynamic_slice(g_cw3, (me // (N_DEV // 2), 0, (me % (N_DEV // 2)) * c_cw), (1, 3, c_cw))[0]
    cw_res = _adamw_plain(g_cw, conv_w[0], m_conv_w[0], v_conv_w[0], "adamw_conv_w")
    big_out = {"conv_w": [g_cw[None]] + [a[None] for a in cw_res]}

    moments = dict(w_in=(m_w_in, v_w_in), w_mem_kv=(m_w_mem_kv, v_w_mem_kv), w_branch_a=(m_w_branch_a, v_w_branch_a),
                   w_branch_b=(m_w_branch_b, v_w_branch_b), w_branch_c=(m_w_branch_c, v_w_branch_c), w_out=(m_w_out, v_w_out),
                   w_up=(m_w_up, v_w_up), w_down=(m_w_down, v_w_down))
    token = (grad_x, small_res[0][0])
    for k in ["w_down", "w_up", "w_out", "w_branch_a", "w_branch_b", "w_branch_c", "w_mem_kv", "w_in"]:
        s, r = reduced[k]
        mk, vk = moments[k][0][0], moments[k][1][0]
        if k == "w_in":
            res = _adamw_big(s, r, chip, big[k], mk.T, vk.T, "adamw_" + k, after=token)
            big_out[k] = [a.T[None] for a in res]
        else:
            res = _adamw_big(s, r, chip, big[k], mk, vk, "adamw_" + k, after=token)
            big_out[k] = [a[None] for a in res]
        token = (res[0],)

    order = ["g_mix", "w_in", "g_a_v", "w_spatial", "b_spatial", "g_b_q", "g_b_k", "sinks", "g_mem", "w_mem_kv", "g_c_q", "g_c_k",
             "w_branch_a", "w_branch_b", "w_branch_c", "w_out", "g_ffn", "w_up", "conv_w", "conv_b", "w_down"]
    res = {**small_out, **big_out}
    outs = [loss, grad_x[None]]
    for field in range(4):
        outs += [res[k][field] for k in order]
    return tuple(outs)
```

```python
import functools

import jax
import jax.numpy as jnp
from jax import lax
from jax.experimental import pallas as pl
from jax.experimental.pallas import tpu as pltpu
from jax.experimental.pallas import tpu_sc as plsc

F32 = jnp.float32
BF = jnp.bfloat16
EPS = 1e-6
NEG = -1e30

N_DEV = 8
CHUNK = 128
A_GROUPS = 4
A_WIDTH = 512
B_HEADS = 16
B_KV_HEADS = 2
B_HEAD_DIM = 64
B_WIDTH = 1024
B_KV_WIDTH = 128
ROPE_DIM = 16
ROPE_THETA = 500000.0
C_HEADS = 4
C_HEAD_DIM = 128
C_WIDTH = 512
GATE_OFF = 2 * A_WIDTH + B_WIDTH + 2 * B_KV_WIDTH + C_WIDTH

ADAM_LR = 0.001
ADAM_B1 = 0.9
ADAM_B2 = 0.999
ADAM_EPS = 1e-08
ADAM_WD = 0.01
ADAM_STEP = 10

VMEM_LIMIT = 48 * 1024 * 1024
MESH = pl.DeviceIdType.MESH


def _pick(n, prefs):
    for p in prefs:
        if p <= n and n % p == 0:
            return p
    return n


def _params(sem):
    return pltpu.CompilerParams(dimension_semantics=sem, vmem_limit_bytes=VMEM_LIMIT)


def _hide(body, n_seen, n_hidden):
    if not n_hidden:
        return body

    def wrapped(*refs):
        return body(*refs[:n_seen], *refs[n_seen + n_hidden:])

    return wrapped


def _hidden_specs(after):
    return [pl.BlockSpec(memory_space=pl.ANY) for _ in after]


def _token(xs, name):
    def body(*refs):
        refs[-1][...] = jnp.zeros_like(refs[-1])

    return pl.pallas_call(body, name=name, in_specs=_hidden_specs(xs), out_shape=jax.ShapeDtypeStruct((8, 128), F32))(*xs)


def _mm(a, b, mode, out_dtype, name, *, resid=None, b_stack=False, a_parts=0, b_parts=0, out_parts=0,
        out_stack=False, tm=1024, tn=1024, tk=2048, after=(), exchange=()):
    if mode == "nn":
        M = a.shape[-2]
        K = a.shape[-1] * max(a_parts, 1)
        N = b.shape[-1] * (N_DEV if b_stack else 1)
        dims = (((1,), (0,)), ((), ()))
    elif mode == "nt":
        M = a.shape[-2]
        K = a.shape[-1] * max(a_parts, 1)
        N = b.shape[-2]
        dims = (((1,), (1,)), ((), ()))
    else:
        K = a.shape[-2]
        M = a.shape[-1]
        N = b.shape[-1] * max(b_parts, 1)
        dims = (((0,), (0,)), ((), ()))
    if b_stack and mode == "nn":
        tn = b.shape[-1]
    if b_stack and mode == "nt":
        tk = b.shape[-1]
    if out_stack:
        tn = N // N_DEV
    tm, tn, tk = _pick(M, (tm,)), _pick(N, (tn,)), _pick(K, (tk,))
    if M % tm or N % tn or K % tk:
        raise ValueError(f"{name}: tiles {tm},{tn},{tk} do not divide {M},{N},{K}")
    nm, nn, nk = M // tm, N // tn, K // tk

    def parts_idx(t, ntile, parts):
        per = ntile // parts
        return t // per, t % per

    if mode in ("nn", "nt"):
        if a_parts:
            a_spec = pl.BlockSpec((None, tm, tk), lambda m, n, k: (parts_idx(k, nk, a_parts)[0], m, parts_idx(k, nk, a_parts)[1]))
        else:
            a_spec = pl.BlockSpec((tm, tk), lambda m, n, k: (m, k))
    else:
        a_spec = pl.BlockSpec((tk, tm), lambda m, n, k: (k, m))
    if mode == "nn":
        if b_stack:
            b_spec = pl.BlockSpec((None, tk, tn), lambda m, n, k: (n, k, 0))
        else:
            b_spec = pl.BlockSpec((tk, tn), lambda m, n, k: (k, n))
    elif mode == "nt":
        if b_stack:
            b_spec = pl.BlockSpec((None, tn, tk), lambda m, n, k: (k, n, 0))
        else:
            b_spec = pl.BlockSpec((tn, tk), lambda m, n, k: (n, k))
    else:
        if b_parts:
            b_spec = pl.BlockSpec((None, tk, tn), lambda m, n, k: (parts_idx(n, nn, b_parts)[0], k, parts_idx(n, nn, b_parts)[1]))
        else:
            b_spec = pl.BlockSpec((tk, tn), lambda m, n, k: (k, n))
    if out_stack:
        out_shape = jax.ShapeDtypeStruct((N_DEV, M, tn), out_dtype)
        o_spec = pl.BlockSpec((None, tm, tn), lambda m, n, k: (n, m, 0))
    elif out_parts:
        out_shape = jax.ShapeDtypeStruct((out_parts, M, N // out_parts), out_dtype)
        o_spec = pl.BlockSpec((None, tm, tn), lambda m, n, k: (parts_idx(n, nn, out_parts)[0], m, parts_idx(n, nn, out_parts)[1]))
    else:
        out_shape = jax.ShapeDtypeStruct((M, N), out_dtype)
        o_spec = pl.BlockSpec((tm, tn), lambda m, n, k: (m, n))
    has_resid = resid is not None

    n_ex = len(exchange)
    n_in = 2 + has_resid + len(after)

    def body(*refs):
        a_ref, b_ref = refs[:2]
        r_ref = refs[2] if has_resid else None
        ex_in = refs[n_in:n_in + n_ex]
        o_ref = refs[n_in + n_ex]
        ex_out = refs[n_in + n_ex + 1:n_in + 2 * n_ex + 1]
        scratch = refs[n_in + 2 * n_ex + 1:]
        m_i, n_i, k = pl.program_id(0), pl.program_id(1), pl.program_id(2)

        def pushes():
            send_sems, recv_sems = scratch[-2:]
            x, y, c = lax.axis_index("x"), lax.axis_index("y"), lax.axis_index("c")
            return [pltpu.make_async_remote_copy(
                src_ref=ex_in[w].at[:, 1 - c], dst_ref=ex_out[w], send_sem=send_sems.at[w], recv_sem=recv_sems.at[w],
                device_id=(x, y, 1 - c), device_id_type=MESH) for w in range(n_ex)]

        if n_ex:
            @pl.when((m_i == 0) & (n_i == 0) & (k == 0))
            def _():
                for cp in pushes():
                    cp.start()

        if nk == 1:
            res = lax.dot_general(a_ref[...], b_ref[...], dims, preferred_element_type=F32)
            if has_resid:
                res = res + r_ref[...]
            o_ref[...] = res.astype(o_ref.dtype)
        else:
            acc = scratch[0]

            @pl.when(k == 0)
            def _():
                acc[...] = jnp.zeros_like(acc)

            acc[...] += lax.dot_general(a_ref[...], b_ref[...], dims, preferred_element_type=F32)

            @pl.when(k == nk - 1)
            def _():
                res = acc[...]
                if has_resid:
                    res = res + r_ref[...]
                o_ref[...] = res.astype(o_ref.dtype)

        if n_ex:
            @pl.when((m_i == nm - 1) & (n_i == nn - 1) & (k == nk - 1))
            def _():
                for cp in pushes():
                    cp.wait()

    in_specs = [a_spec, b_spec]
    args = [a, b]
    if has_resid:
        in_specs.append(pl.BlockSpec((tm, tn), lambda m, n, k: (m, n)))
        args.append(resid)
    in_specs += _hidden_specs(after) + _hidden_specs(exchange)
    args += list(after) + list(exchange)
    scratch_shapes = [pltpu.VMEM((tm, tn), F32)] if nk > 1 else []
    if not n_ex:
        return pl.pallas_call(
            body, name=name, grid=(nm, nn, nk), in_specs=in_specs, out_specs=o_spec, out_shape=out_shape,
            scratch_shapes=scratch_shapes, compiler_params=_params(("parallel", "parallel", "arbitrary")),
        )(*args)
    res = pl.pallas_call(
        body, name=name, grid=(nm, nn, nk), in_specs=in_specs, out_specs=[o_spec] + _hidden_specs(exchange),
        out_shape=[out_shape] + [jax.ShapeDtypeStruct((g.shape[0],) + g.shape[2:], g.dtype) for g in exchange],
        scratch_shapes=scratch_shapes + [pltpu.SemaphoreType.DMA((n_ex,)), pltpu.SemaphoreType.DMA((n_ex,))],
        compiler_params=_params(("arbitrary", "arbitrary", "arbitrary")),
    )(*args)
    return res[0], list(res[1:])


def _rms_fwd(x, g, name):
    R, D = x.shape
    tr = _pick(R, (256,))

    def body(x_ref, g_ref, h_ref, r_ref):
        xv = x_ref[...]
        r = lax.rsqrt(jnp.mean(xv * xv, axis=-1, keepdims=True) + EPS)
        h_ref[...] = (xv * r * g_ref[...]).astype(BF)
        r_ref[...] = r

    return pl.pallas_call(
        body, name=name, grid=(R // tr,),
        in_specs=[pl.BlockSpec((tr, D), lambda i: (i, 0)), pl.BlockSpec((1, D), lambda i: (0, 0))],
        out_specs=[pl.BlockSpec((tr, D), lambda i: (i, 0)), pl.BlockSpec((tr, 1), lambda i: (i, 0))],
        out_shape=[jax.ShapeDtypeStruct((R, D), BF), jax.ShapeDtypeStruct((R, 1), F32)],
        compiler_params=_params(("parallel",)),
    )(x, g)


def _rms_bwd(x, r, g, dh, dres, name, after=()):
    R, D = x.shape
    tr = _pick(R, (256,))
    has_res = dres is not None

    def body(*refs):
        if has_res:
            x_ref, r_ref, g_ref, dh_ref, dres_ref, dx_ref, dxb_ref, dg_ref = refs
        else:
            x_ref, r_ref, g_ref, dh_ref, dx_ref, dxb_ref, dg_ref = refs
        i = pl.program_id(0)
        xv, rv, dhv = x_ref[...], r_ref[...], dh_ref[...]
        gy = dhv * g_ref[...]
        c = jnp.sum(xv * gy, axis=-1, keepdims=True)
        dx = rv * gy - xv * (rv * rv * rv) * (c * (1.0 / D))
        if has_res:
            dx = dx + dres_ref[...]
        dx_ref[...] = dx
        dxb_ref[...] = dx.astype(BF)
        part = jnp.sum(dhv * xv * rv, axis=0, keepdims=True)

        @pl.when(i == 0)
        def _():
            dg_ref[...] = part

        @pl.when(i > 0)
        def _():
            dg_ref[...] += part

    row = pl.BlockSpec((tr, D), lambda i: (i, 0))
    in_specs = [row, pl.BlockSpec((tr, 1), lambda i: (i, 0)), pl.BlockSpec((1, D), lambda i: (0, 0)), row]
    args = [x, r, g, dh]
    if has_res:
        in_specs.append(row)
        args.append(dres)
    return pl.pallas_call(
        _hide(body, len(args), len(after)), name=name, grid=(R // tr,), in_specs=in_specs + _hidden_specs(after),
        out_specs=[row, row, pl.BlockSpec((1, D), lambda i: (0, 0))],
        out_shape=[jax.ShapeDtypeStruct((R, D), F32), jax.ShapeDtypeStruct((R, D), BF), jax.ShapeDtypeStruct((1, D), F32)],
        compiler_params=_params(("arbitrary",)),
    )(*args, *after)


def _a_chunk(us, vs, gvs, ws, bs):
    r_i = lax.broadcasted_iota(jnp.int32, (CHUNK, CHUNK), 0)
    c_i = lax.broadcasted_iota(jnp.int32, (CHUNK, CHUNK), 1)
    causal = r_i >= c_i
    vg = [jax.nn.gelu(v) for v in vs]
    ss = sum(jnp.sum(v * v, axis=-1, keepdims=True) for v in vg)
    r = lax.rsqrt(ss * (1.0 / A_WIDTH) + EPS)
    ys = []
    for g in range(A_GROUPS):
        vn = vg[g] * r * gvs[g]
        w = jnp.where(causal, ws[g], 0.0)
        s = jnp.dot(w.astype(BF), vn.astype(BF), preferred_element_type=F32) + bs[g]
        ys.append(jax.nn.gelu(us[g]) * s)
    return ys


def _a_split(u_ref, v_ref, g_ref, w_ref, b_ref):
    sl = [slice(g * 128, (g + 1) * 128) for g in range(A_GROUPS)]
    return ([u_ref[:, s] for s in sl], [v_ref[:, s] for s in sl], [g_ref[:, s] for s in sl],
            [w_ref[g] for g in range(A_GROUPS)], [b_ref[:, g:g + 1] for g in range(A_GROUPS)])


def _a_specs(S):
    return [pl.BlockSpec((CHUNK, A_WIDTH), lambda n: (n, 0)), pl.BlockSpec((CHUNK, A_WIDTH), lambda n: (n, 1)),
            pl.BlockSpec((1, A_WIDTH), lambda n: (0, 0)), pl.BlockSpec((A_GROUPS, CHUNK, CHUNK), lambda n: (0, 0, 0)),
            pl.BlockSpec((CHUNK, A_GROUPS), lambda n: (0, 0))]


def _a_fwd(proj, g_v, w_s, b_t):
    S = proj.shape[0]

    def body(u_ref, v_ref, g_ref, w_ref, b_ref, y_ref):
        ys = _a_chunk(*_a_split(u_ref, v_ref, g_ref, w_ref, b_ref))
        for g in range(A_GROUPS):
            y_ref[:, g * 128:(g + 1) * 128] = ys[g].astype(BF)

    return pl.pallas_call(
        body, name="a_fwd", grid=(S // CHUNK,), in_specs=_a_specs(S),
        out_specs=pl.BlockSpec((CHUNK, A_WIDTH), lambda n: (n, 0)),
        out_shape=jax.ShapeDtypeStruct((S, A_WIDTH), BF), compiler_params=_params(("parallel",)),
    )(proj, proj, g_v, w_s, b_t)


def _a_bwd(proj, g_v, w_s, b_t, dy, after=()):
    S = proj.shape[0]

    def body(u_ref, v_ref, g_ref, w_ref, b_ref, dy_ref, duv_ref, dg_ref, dw_ref, db_ref):
        n = pl.program_id(0)
        dys = [dy_ref[:, g * 128:(g + 1) * 128] for g in range(A_GROUPS)]
        _, vjp = jax.vjp(_a_chunk, *_a_split(u_ref, v_ref, g_ref, w_ref, b_ref))
        dus, dvs, dgs, dws, dbs = vjp(dys)

        @pl.when(n == 0)
        def _():
            dg_ref[...] = jnp.zeros_like(dg_ref)
            dw_ref[...] = jnp.zeros_like(dw_ref)
            db_ref[...] = jnp.zeros_like(db_ref)

        for g in range(A_GROUPS):
            duv_ref[:, g * 128:(g + 1) * 128] = dus[g].astype(BF)
            duv_ref[:, A_WIDTH + g * 128:A_WIDTH + (g + 1) * 128] = dvs[g].astype(BF)
            dg_ref[:, g * 128:(g + 1) * 128] += dgs[g]
            dw_ref[g] += dws[g]
            db_ref[:, g:g + 1] += dbs[g]

    return pl.pallas_call(
        _hide(body, 6, len(after)), name="a_bwd", grid=(S // CHUNK,),
        in_specs=_a_specs(S) + [pl.BlockSpec((CHUNK, A_WIDTH), lambda n: (n, 0))] + _hidden_specs(after),
        out_specs=[pl.BlockSpec((CHUNK, 2 * A_WIDTH), lambda n: (n, 0)), pl.BlockSpec((1, A_WIDTH), lambda n: (0, 0)),
                   pl.BlockSpec((A_GROUPS, CHUNK, CHUNK), lambda n: (0, 0, 0)), pl.BlockSpec((CHUNK, A_GROUPS), lambda n: (0, 0))],
        out_shape=[jax.ShapeDtypeStruct((S, 2 * A_WIDTH), BF), jax.ShapeDtypeStruct((1, A_WIDTH), F32),
                   jax.ShapeDtypeStruct((A_GROUPS, CHUNK, CHUNK), F32), jax.ShapeDtypeStruct((CHUNK, A_GROUPS), F32)],
        compiler_params=_params(("arbitrary",)),
    )(proj, proj, g_v, w_s, b_t, dy, *after)


def _half_mask(shape, which):
    lane = lax.broadcasted_iota(jnp.int32, shape, len(shape) - 1)
    return (lane >= 64) == (which == 1)


def _pair_norm_rope(x, g, ct, sa, sb):
    lo = _half_mask(x.shape, 0)
    x2 = x * x
    ss_lo = jnp.sum(jnp.where(lo, x2, 0.0), axis=-1, keepdims=True)
    ss_hi = jnp.sum(jnp.where(lo, 0.0, x2), axis=-1, keepdims=True)
    r = jnp.where(lo, lax.rsqrt(ss_lo * (1.0 / B_HEAD_DIM) + EPS), lax.rsqrt(ss_hi * (1.0 / B_HEAD_DIM) + EPS))
    xr = x * r
    xn = xr * g
    out = xn * ct + pltpu.roll(xn, 120, 1) * sa + pltpu.roll(xn, 8, 1) * sb
    return out, xr, r


def _pair_norm_rope_bwd(x, g, ct, sa, sb, dout):
    lo = _half_mask(x.shape, 0)
    _, xr, r = _pair_norm_rope(x, g, ct, sa, sb)
    dxn = dout * ct + pltpu.roll(dout * sa, 8, 1) + pltpu.roll(dout * sb, 120, 1)
    gy = dxn * g
    t = xr * gy
    c_lo = jnp.sum(jnp.where(lo, t, 0.0), axis=-1, keepdims=True)
    c_hi = jnp.sum(jnp.where(lo, 0.0, t), axis=-1, keepdims=True)
    c = jnp.where(lo, c_lo, c_hi)
    dx = r * (gy - xr * c * (1.0 / B_HEAD_DIM))
    dg = jnp.sum(dxn * xr, axis=0, keepdims=True)
    return dx, dg


def _b_pre(proj, gq2, gk2, ct, sa, sb):
    S = proj.shape[0]
    tr = _pick(S, (256,))
    n_pair = B_WIDTH // 128

    def body(q_ref, k_ref, gq_ref, gk_ref, ct_ref, sa_ref, sb_ref, qn_ref, kn_ref):
        ct_v, sa_v, sb_v = ct_ref[...], sa_ref[...], sb_ref[...]
        for p in range(n_pair):
            o, _, _ = _pair_norm_rope(q_ref[:, p * 128:(p + 1) * 128], gq_ref[...], ct_v, sa_v, sb_v)
            qn_ref[:, p * 128:(p + 1) * 128] = o.astype(BF)
        o, _, _ = _pair_norm_rope(k_ref[...], gk_ref[...], ct_v, sa_v, sb_v)
        kn_ref[...] = o.astype(BF)

    tab = pl.BlockSpec((tr, 128), lambda i: (i, 0))
    gsp = pl.BlockSpec((1, 128), lambda i: (0, 0))
    return pl.pallas_call(
        body, name="b_pre", grid=(S // tr,),
        in_specs=[pl.BlockSpec((tr, B_WIDTH), lambda i: (i, 1)), pl.BlockSpec((tr, 128), lambda i: (i, 2 * B_WIDTH // 128)),
                  gsp, gsp, tab, tab, tab],
        out_specs=[pl.BlockSpec((tr, B_WIDTH), lambda i: (i, 0)), tab],
        out_shape=[jax.ShapeDtypeStruct((S, B_WIDTH), BF), jax.ShapeDtypeStruct((S, 128), BF)],
        compiler_params=_params(("parallel",)),
    )(proj, proj, gq2, gk2, ct, sa, sb)


def _b_pre_bwd(proj, gq2, gk2, ct, sa, sb, dqn, dkn, dv):
    S = proj.shape[0]
    tr = _pick(S, (256,))
    n_pair = B_WIDTH // 128

    def body(q_ref, k_ref, gq_ref, gk_ref, ct_ref, sa_ref, sb_ref, dqn_ref, dkn_ref, dv_ref, dqkv_ref, dgq_ref, dgk_ref):
        i = pl.program_id(0)
        ct_v, sa_v, sb_v = ct_ref[...], sa_ref[...], sb_ref[...]
        dgq = jnp.zeros((1, 128), F32)
        for p in range(n_pair):
            sl = slice(p * 128, (p + 1) * 128)
            dx, dg = _pair_norm_rope_bwd(q_ref[:, sl], gq_ref[...], ct_v, sa_v, sb_v, dqn_ref[:, sl])
            dqkv_ref[:, sl] = dx.astype(BF)
            dgq = dgq + dg
        dx, dgk = _pair_norm_rope_bwd(k_ref[...], gk_ref[...], ct_v, sa_v, sb_v, dkn_ref[...])
        dqkv_ref[:, B_WIDTH:B_WIDTH + 128] = dx.astype(BF)
        dqkv_ref[:, B_WIDTH + 128:B_WIDTH + 256] = dv_ref[...].astype(BF)

        @pl.when(i == 0)
        def _():
            dgq_ref[...] = dgq
            dgk_ref[...] = dgk

        @pl.when(i > 0)
        def _():
            dgq_ref[...] += dgq
            dgk_ref[...] += dgk

    tab = pl.BlockSpec((tr, 128), lambda i: (i, 0))
    gsp = pl.BlockSpec((1, 128), lambda i: (0, 0))
    return pl.pallas_call(
        body, name="b_pre_bwd", grid=(S // tr,),
        in_specs=[pl.BlockSpec((tr, B_WIDTH), lambda i: (i, 1)), pl.BlockSpec((tr, 128), lambda i: (i, 2 * B_WIDTH // 128)),
                  gsp, gsp, tab, tab, tab, pl.BlockSpec((tr, B_WIDTH), lambda i: (i, 0)), tab, tab],
        out_specs=[pl.BlockSpec((tr, B_WIDTH + 256), lambda i: (i, 0)), gsp, gsp],
        out_shape=[jax.ShapeDtypeStruct((S, B_WIDTH + 256), BF), jax.ShapeDtypeStruct((1, 128), F32), jax.ShapeDtypeStruct((1, 128), F32)],
        compiler_params=_params(("arbitrary",)),
    )(proj, proj, gq2, gk2, ct, sa, sb, dqn, dkn, dv)


def _b_dup(x2, g):
    d = jnp.where(_half_mask(x2.shape, g), x2, 0.0)
    return (d + pltpu.roll(d, 64, 1)).astype(BF)


PAIRS_PER_GROUP = B_HEADS // B_KV_HEADS // 2
GROUP_ROWS = PAIRS_PER_GROUP * CHUNK


def _b_valid(n):
    row = lax.broadcasted_iota(jnp.int32, (GROUP_ROWS, 2 * CHUNK), 0) & (CHUNK - 1)
    col = lax.broadcasted_iota(jnp.int32, (GROUP_ROWS, 2 * CHUNK), 1)
    rel = row + CHUNK - col
    return (rel >= 0) & (rel < CHUNK) & ((col >= CHUNK) | (n > 0))


def _b_blocks(x2, g):
    xd = _b_dup(x2, g)
    lo = _half_mask(xd.shape, 0)
    zero = jnp.zeros_like(xd)
    return jnp.concatenate([jnp.where(lo, xd, zero), jnp.where(lo, zero, xd)], axis=0)


def _b_sink_col(s_ref, g, hf):
    rb = lax.broadcasted_iota(jnp.int32, (GROUP_ROWS, 1), 0) // CHUNK
    col = jnp.zeros((GROUP_ROWS, 1), F32)
    for pp in range(PAIRS_PER_GROUP):
        col = jnp.where(rb == pp, s_ref[0, 2 * (g * PAIRS_PER_GROUP + pp) + hf], col)
    return col


def _b_probs(qs, kblk, valid, sinks):
    s = lax.dot_general(qs, kblk, (((1,), (1,)), ((), ())), preferred_element_type=F32) * (B_HEAD_DIM ** -0.5)
    out = []
    for hf in range(2):
        sh = jnp.where(valid, s[:, hf * 2 * CHUNK:(hf + 1) * 2 * CHUNK], NEG)
        m = jnp.maximum(jnp.max(sh, axis=-1, keepdims=True), sinks[hf])
        e = jnp.exp(sh - m)
        es = jnp.exp(sinks[hf] - m)
        inv = 1.0 / (jnp.sum(e, axis=-1, keepdims=True) + es)
        out.append((e * inv, es * inv))
    return out


def _b_fold(acc, g):
    lo = _half_mask((2 * CHUNK, 128), 0)
    t = jnp.where(lo, acc[:2 * CHUNK], 0.0) + jnp.where(lo, 0.0, acc[2 * CHUNK:])
    return jnp.where(_half_mask((2 * CHUNK, 128), g), t + pltpu.roll(t, 64, 1), 0.0)


def _b_kv_specs(S):
    prev = lambda n: (jnp.maximum(n - 1, 0), 0)
    cur = lambda n: (n, 0)
    v_col = (2 * B_WIDTH + B_KV_WIDTH) // 128
    return [pl.BlockSpec((CHUNK, 128), prev), pl.BlockSpec((CHUNK, 128), cur),
            pl.BlockSpec((CHUNK, 128), lambda n: (jnp.maximum(n - 1, 0), v_col)), pl.BlockSpec((CHUNK, 128), lambda n: (n, v_col))]


def _b_attn_fwd(qn, kn, proj, sinks):
    S = qn.shape[0]

    def body(s_ref, q_ref, kp_ref, kc_ref, vp_ref, vc_ref, y_ref):
        n = pl.program_id(0)
        valid = _b_valid(n)
        k2 = jnp.concatenate([kp_ref[...], kc_ref[...]], axis=0).astype(F32)
        v2 = jnp.concatenate([vp_ref[...], vc_ref[...]], axis=0)
        for g in range(B_KV_HEADS):
            pairs = [g * PAIRS_PER_GROUP + pp for pp in range(PAIRS_PER_GROUP)]
            qs = jnp.concatenate([q_ref[:, p * 128:(p + 1) * 128] for p in pairs], axis=0)
            probs = _b_probs(qs, _b_blocks(k2, g), valid, [_b_sink_col(s_ref, g, hf) for hf in range(2)])
            pcat = jnp.concatenate([probs[0][0].astype(BF), probs[1][0].astype(BF)], axis=1)
            o = jnp.dot(pcat, _b_blocks(v2, g), preferred_element_type=F32)
            for pp, p in enumerate(pairs):
                y_ref[:, p * 128:(p + 1) * 128] = o[pp * CHUNK:(pp + 1) * CHUNK].astype(BF)

    return pl.pallas_call(
        body, name="b_attn_fwd", grid=(S // CHUNK,),
        in_specs=[pl.BlockSpec(memory_space=pltpu.SMEM), pl.BlockSpec((CHUNK, B_WIDTH), lambda n: (n, 0))] + _b_kv_specs(S),
        out_specs=pl.BlockSpec((CHUNK, B_WIDTH), lambda n: (n, 0)),
        out_shape=jax.ShapeDtypeStruct((S, B_WIDTH), BF), compiler_params=_params(("arbitrary",)),
    )(sinks, qn, kn, kn, proj, proj)


def _b_attn_bwd(qn, kn, proj, sinks, dy, after=()):
    S = qn.shape[0]

    def body(s_ref, q_ref, kp_ref, kc_ref, vp_ref, vc_ref, dy_ref, dq_ref, dk_ref, dv_ref, ds_ref):
        n = pl.program_id(0)

        @pl.when(n == 0)
        def _():
            dk_ref[...] = jnp.zeros_like(dk_ref)
            dv_ref[...] = jnp.zeros_like(dv_ref)
            ds_ref[...] = jnp.zeros_like(ds_ref)

        valid = _b_valid(n)
        k2 = jnp.concatenate([kp_ref[...], kc_ref[...]], axis=0).astype(F32)
        v2 = jnp.concatenate([vp_ref[...], vc_ref[...]], axis=0)
        lane = lax.broadcasted_iota(jnp.int32, (CHUNK, 128), 1)
        dk2 = jnp.zeros((2 * CHUNK, 128), F32)
        dv2 = jnp.zeros((2 * CHUNK, 128), F32)
        dsink = jnp.zeros((CHUNK, 128), F32)
        scale = B_HEAD_DIM ** -0.5
        nt = (((1,), (1,)), ((), ()))
        tn = (((0,), (0,)), ((), ()))
        for g in range(B_KV_HEADS):
            pairs = [g * PAIRS_PER_GROUP + pp for pp in range(PAIRS_PER_GROUP)]
            qs = jnp.concatenate([q_ref[:, p * 128:(p + 1) * 128] for p in pairs], axis=0)
            do = jnp.concatenate([dy_ref[:, p * 128:(p + 1) * 128] for p in pairs], axis=0)
            do_b = do.astype(BF)
            kblk, vblk = _b_blocks(k2, g), _b_blocks(v2, g)
            probs = _b_probs(qs, kblk, valid, [_b_sink_col(s_ref, g, hf) for hf in range(2)])
            pcat = jnp.concatenate([probs[0][0].astype(BF), probs[1][0].astype(BF)], axis=1)
            o = jnp.dot(pcat, vblk, preferred_element_type=F32)
            dp = lax.dot_general(do_b, vblk, nt, preferred_element_type=F32)
            prod = do * o
            ds_halves = []
            for hf in range(2):
                pr, ps = probs[hf]
                delta = jnp.sum(jnp.where(_half_mask(prod.shape, hf), prod, 0.0), axis=-1, keepdims=True)
                ds_halves.append((pr * (dp[:, hf * 2 * CHUNK:(hf + 1) * 2 * CHUNK] - delta) * scale).astype(BF))
                t = -ps * delta
                for pp, p in enumerate(pairs):
                    dsink = dsink + jnp.where(lane == 2 * p + hf, t[pp * CHUNK:(pp + 1) * CHUNK], 0.0)
            dsc = jnp.concatenate(ds_halves, axis=1)
            dq = jnp.dot(dsc, kblk, preferred_element_type=F32)
            for pp, p in enumerate(pairs):
                dq_ref[:, p * 128:(p + 1) * 128] = dq[pp * CHUNK:(pp + 1) * CHUNK]
            dk2 = dk2 + _b_fold(lax.dot_general(dsc, qs, tn, preferred_element_type=F32), g)
            dv2 = dv2 + _b_fold(lax.dot_general(pcat, do_b, tn, preferred_element_type=F32), g)
        ds_ref[...] += dsink
        cur = pl.ds(pl.multiple_of(n * CHUNK, CHUNK), CHUNK)
        dk_ref[cur, :] += dk2[CHUNK:]
        dv_ref[cur, :] += dv2[CHUNK:]

        @pl.when(n > 0)
        def _():
            prv = pl.ds(pl.multiple_of((n - 1) * CHUNK, CHUNK), CHUNK)
            dk_ref[prv, :] += dk2[:CHUNK]
            dv_ref[prv, :] += dv2[:CHUNK]

    full = pl.BlockSpec((S, 128), lambda n: (0, 0))
    return pl.pallas_call(
        _hide(body, 7, len(after)), name="b_attn_bwd", grid=(S // CHUNK,),
        in_specs=[pl.BlockSpec(memory_space=pltpu.SMEM), pl.BlockSpec((CHUNK, B_WIDTH), lambda n: (n, 0))] + _b_kv_specs(S)
        + [pl.BlockSpec((CHUNK, B_WIDTH), lambda n: (n, 0))] + _hidden_specs(after),
        out_specs=[pl.BlockSpec((CHUNK, B_WIDTH), lambda n: (n, 0)), full, full, pl.BlockSpec((CHUNK, 128), lambda n: (0, 0))],
        out_shape=[jax.ShapeDtypeStruct((S, B_WIDTH), F32), jax.ShapeDtypeStruct((S, 128), F32), jax.ShapeDtypeStruct((S, 128), F32),
                   jax.ShapeDtypeStruct((CHUNK, 128), F32)],
        compiler_params=_params(("arbitrary",)),
    )(sinks, qn, kn, kn, proj, proj, dy, *after)


def _c_block(q, k, v, gq, gk):
    qn = q * lax.rsqrt(jnp.mean(q * q, axis=-1, keepdims=True) + EPS) * gq
    kn = k * lax.rsqrt(jnp.mean(k * k, axis=-1, keepdims=True) + EPS) * gk
    s = lax.dot_general(qn.astype(BF), kn.astype(BF), (((1,), (1,)), ((), ())), preferred_element_type=F32) * (C_HEAD_DIM ** -0.5)
    p = jax.nn.softmax(s, axis=-1)
    return jnp.dot(p.astype(BF), v.astype(BF), preferred_element_type=F32)


def _c_specs(S, M, tq):
    q_col = (2 * A_WIDTH + B_WIDTH + 2 * B_KV_WIDTH) // 128
    return [pl.BlockSpec((tq, 128), lambda h, i: (i, q_col + h)), pl.BlockSpec((M, 128), lambda h, i: (0, h)),
            pl.BlockSpec((M, 128), lambda h, i: (0, C_HEADS + h)), pl.BlockSpec((1, 128), lambda h, i: (0, 0)),
            pl.BlockSpec((1, 128), lambda h, i: (0, 0))]


def _c_fwd(proj, kv, gq, gk):
    S, M = proj.shape[0], kv.shape[0]
    tq = _pick(S, (512,))

    def body(q_ref, k_ref, v_ref, gq_ref, gk_ref, y_ref):
        y_ref[...] = _c_block(q_ref[...], k_ref[...], v_ref[...], gq_ref[...], gk_ref[...]).astype(BF)

    return pl.pallas_call(
        body, name="c_fwd", grid=(C_HEADS, S // tq), in_specs=_c_specs(S, M, tq),
        out_specs=pl.BlockSpec((tq, 128), lambda h, i: (i, h)),
        out_shape=jax.ShapeDtypeStruct((S, C_WIDTH), BF), compiler_params=_params(("parallel", "parallel")),
    )(proj, kv, kv, gq, gk)


def _c_bwd(proj, kv, gq, gk, dy):
    S, M = proj.shape[0], kv.shape[0]
    tq = _pick(S, (512,))

    def body(q_ref, k_ref, v_ref, gq_ref, gk_ref, dy_ref, dq_ref, dk_ref, dv_ref, dgq_ref, dgk_ref):
        i = pl.program_id(1)
        _, vjp = jax.vjp(_c_block, q_ref[...], k_ref[...], v_ref[...], gq_ref[...], gk_ref[...])
        dq, dk, dv, dgq, dgk = vjp(dy_ref[...])
        dq_ref[...] = dq.astype(BF)

        @pl.when(i == 0)
        def _():
            dk_ref[...] = dk
            dv_ref[...] = dv
            dgq_ref[...] = dgq
            dgk_ref[...] = dgk

        @pl.when(i > 0)
        def _():
            dk_ref[...] += dk
            dv_ref[...] += dv
            dgq_ref[...] += dgq
            dgk_ref[...] += dgk

    return pl.pallas_call(
        body, name="c_bwd", grid=(C_HEADS, S // tq),
        in_specs=_c_specs(S, M, tq) + [pl.BlockSpec((tq, 128), lambda h, i: (i, h))],
        out_specs=[pl.BlockSpec((tq, 128), lambda h, i: (i, h)), pl.BlockSpec((M, 128), lambda h, i: (0, h)),
                   pl.BlockSpec((M, 128), lambda h, i: (0, h)), pl.BlockSpec((None, 1, 128), lambda h, i: (h, 0, 0)),
                   pl.BlockSpec((None, 1, 128), lambda h, i: (h, 0, 0))],
        out_shape=[jax.ShapeDtypeStruct((S, C_WIDTH), BF), jax.ShapeDtypeStruct((M, C_WIDTH), F32), jax.ShapeDtypeStruct((M, C_WIDTH), F32),
                   jax.ShapeDtypeStruct((C_HEADS, 1, 128), F32), jax.ShapeDtypeStruct((C_HEADS, 1, 128), F32)],
        compiler_params=_params(("parallel", "arbitrary")),
    )(proj, kv, kv, gq, gk, dy)


def _merge_specs(S, D, tm, tn, ks):
    off = GATE_OFF // tn
    nd = D // tn
    gates = [pl.BlockSpec((tm, tn), functools.partial(lambda b, m, n: (m, off + b * nd + n), b)) for b in range(3)]
    ys = [pl.BlockSpec((tm, k), lambda m, n: (m, 0)) for k in ks]
    ws = [pl.BlockSpec((None, k, tn), lambda m, n: (n, 0, 0)) for k in ks]
    return gates, ys, ws


def _merge_fwd(proj, ys, ws):
    S = proj.shape[0]
    tn = ws[0].shape[2]
    D = N_DEV * tn
    ks = [w.shape[1] for w in ws]
    tm = _pick(S, (1024,))
    gates, y_specs, w_specs = _merge_specs(S, D, tm, tn, ks)

    def body(ga_ref, gb_ref, gc_ref, ya_ref, yb_ref, yc_ref, wa_ref, wb_ref, wc_ref, m_ref, za_ref, zb_ref, zc_ref):
        acc = None
        for g_ref, y_ref, w_ref, z_ref in ((ga_ref, ya_ref, wa_ref, za_ref), (gb_ref, yb_ref, wb_ref, zb_ref),
                                           (gc_ref, yc_ref, wc_ref, zc_ref)):
            z = jnp.dot(y_ref[...], w_ref[...], preferred_element_type=F32)
            z_ref[...] = z.astype(BF)
            t = jax.nn.sigmoid(g_ref[...]) * z
            acc = t if acc is None else acc + t
        m_ref[...] = acc.astype(BF)

    tile = pl.BlockSpec((tm, tn), lambda m, n: (m, n))
    return pl.pallas_call(
        body, name="merge_fwd", grid=(S // tm, D // tn), in_specs=gates + y_specs + w_specs,
        out_specs=[tile, tile, tile, tile], out_shape=[jax.ShapeDtypeStruct((S, D), BF)] * 4,
        compiler_params=_params(("parallel", "parallel")),
    )(proj, proj, proj, *ys, *ws)


def _merge_bwd(proj, zs, dm, ws, after=()):
    S = proj.shape[0]
    tn = ws[0].shape[2]
    D = N_DEV * tn
    ks = [w.shape[1] for w in ws]
    tm = _pick(S, (1024,))
    gates, _, w_specs = _merge_specs(S, D, tm, tn, ks)
    nt = (((1,), (1,)), ((), ()))

    def body(ga_ref, gb_ref, gc_ref, za_ref, zb_ref, zc_ref, dm_ref, wa_ref, wb_ref, wc_ref,
             dza_ref, dzb_ref, dzc_ref, dga_ref, dgb_ref, dgc_ref, dya_ref, dyb_ref, dyc_ref):
        n = pl.program_id(1)
        dmv = dm_ref[...]
        for g_ref, z_ref, w_ref, dz_ref, dg_ref, dy_ref in (
                (ga_ref, za_ref, wa_ref, dza_ref, dga_ref, dya_ref), (gb_ref, zb_ref, wb_ref, dzb_ref, dgb_ref, dyb_ref),
                (gc_ref, zc_ref, wc_ref, dzc_ref, dgc_ref, dyc_ref)):
            sg = jax.nn.sigmoid(g_ref[...])
            dz = (sg * dmv).astype(BF)
            dz_ref[...] = dz
            dg_ref[...] = (dmv * z_ref[...].astype(F32) * sg * (1.0 - sg)).astype(BF)
            part = lax.dot_general(dz, w_ref[...], nt, preferred_element_type=F32)

            @pl.when(n == 0)
            def _():
                dy_ref[...] = part

            @pl.when(n > 0)
            def _():
                dy_ref[...] += part

    tile = pl.BlockSpec((tm, tn), lambda m, n: (m, n))
    dys = [pl.BlockSpec((tm, k), lambda m, n: (m, 0)) for k in ks]
    return pl.pallas_call(
        _hide(body, 10, len(after)), name="merge_bwd", grid=(S // tm, D // tn),
        in_specs=gates + [tile, tile, tile, tile] + w_specs + _hidden_specs(after),
        out_specs=[tile] * 6 + dys,
        out_shape=[jax.ShapeDtypeStruct((S, D), BF)] * 6 + [jax.ShapeDtypeStruct((S, k), F32) for k in ks],
        compiler_params=_params(("parallel", "arbitrary")),
    )(proj, proj, proj, *zs, dm, *ws, *after)


PAD = 8


def _stage_shift_down(us_ref, u_ref):
    S = u_ref.shape[1]
    us_ref[:, 0:PAD, :] = jnp.zeros((2, PAD, us_ref.shape[2]), F32)
    us_ref[:, PAD:S + PAD, :] = u_ref[...].astype(F32)


ROWS = 64


def _conv3(us_ref, part, r0, w, b):
    return (us_ref[part, pl.ds(r0 + PAD, ROWS), :] * w[2:3] + us_ref[part, pl.ds(r0 + PAD - 1, ROWS), :] * w[1:2]
            + us_ref[part, pl.ds(r0 + PAD - 2, ROWS), :] * w[0:1] + b)


def _ffn_specs(S, F, tc, c):
    per = c // tc

    def w_spec(half):
        return pl.BlockSpec((None, 3, tc), lambda j: (half * (N_DEV // 2) + j // per, 0, j % per))

    return [pl.BlockSpec((2, S, tc), lambda j: (0, 0, j)), w_spec(0), w_spec(1), pl.BlockSpec((2, 1, tc), lambda j: (0, 0, j))]


def _ffn_tile(F, c):
    tc = 128
    if c % tc or F % tc:
        raise ValueError(f"ffn tile {tc} does not divide {c}, {F}")
    return tc


def _ffn_act_fwd(up3, cws, cb3):
    _, S, F = up3.shape
    c = cws.shape[2]
    tc = _ffn_tile(F, c)

    def body(u_ref, wa_ref, wb_ref, b_ref, o_ref, us_ref):
        _stage_shift_down(us_ref, u_ref)
        wa, wb, ba, bb = wa_ref[...], wb_ref[...], b_ref[0], b_ref[1]

        def step(i, carry):
            r0 = pl.multiple_of(i * ROWS, ROWS)
            ca = _conv3(us_ref, 0, r0, wa, ba)
            cb = _conv3(us_ref, 1, r0, wb, bb)
            o_ref[pl.ds(r0, ROWS), :] = (ca * jax.nn.sigmoid(ca) * cb).astype(BF)
            return carry

        lax.fori_loop(0, S // ROWS, step, 0, unroll=4)

    return pl.pallas_call(
        body, name="ffn_act_fwd", grid=(F // tc,), in_specs=_ffn_specs(S, F, tc, c),
        out_specs=pl.BlockSpec((S, tc), lambda j: (0, j)), out_shape=jax.ShapeDtypeStruct((S, F), BF),
        scratch_shapes=[pltpu.VMEM((2, S + PAD, tc), F32)],
        compiler_params=_params(("parallel",)),
    )(up3, cws, cws, cb3)


def _ffn_act_bwd(up3, cws, cb3, dact, after=()):
    _, S, F = up3.shape
    c = cws.shape[2]
    tc = _ffn_tile(F, c)

    def body(u_ref, wa_ref, wb_ref, b_ref, da_ref, du_ref, dw_ref, db_ref, us_ref, dcs_ref):
        _stage_shift_down(us_ref, u_ref)
        ws = (wa_ref[...], wb_ref[...])
        ba, bb = b_ref[0], b_ref[1]
        dcs_ref[:, S:S + PAD, :] = jnp.zeros((2, PAD, tc), F32)

        def conv_grads(i, carry):
            r0 = pl.multiple_of(i * ROWS, ROWS)
            ca = _conv3(us_ref, 0, r0, ws[0], ba)
            cb = _conv3(us_ref, 1, r0, ws[1], bb)
            sg = jax.nn.sigmoid(ca)
            dav = da_ref[pl.ds(r0, ROWS), :].astype(F32)
            dcs_ref[0, pl.ds(r0, ROWS), :] = dav * cb * sg * (1.0 + ca * (1.0 - sg))
            dcs_ref[1, pl.ds(r0, ROWS), :] = dav * ca * sg
            return carry

        lax.fori_loop(0, S // ROWS, conv_grads, 0, unroll=4)

        def fold(v):
            return jnp.sum(v.reshape(ROWS // 8, 8, tc), axis=0)

        def input_grads(i, acc):
            r0 = pl.multiple_of(i * ROWS, ROWS)
            new = []
            for part in range(2):
                w = ws[part]
                dc = dcs_ref[part, pl.ds(r0, ROWS), :]
                dc1 = dcs_ref[part, pl.ds(r0 + 1, ROWS), :]
                dc2 = dcs_ref[part, pl.ds(r0 + 2, ROWS), :]
                u = us_ref[part, pl.ds(r0 + PAD, ROWS), :]
                du_ref[part, pl.ds(r0, ROWS), :] = (dc * w[2:3] + dc1 * w[1:2] + dc2 * w[0:1]).astype(BF)
                sums = (fold(dc2 * u), fold(dc1 * u), fold(dc * u), fold(dc))
                new += [a + s for a, s in zip(acc[4 * part:4 * part + 4], sums)]
            return tuple(new)

        acc = lax.fori_loop(0, S // ROWS, input_grads, tuple(jnp.zeros((8, tc), F32) for _ in range(8)), unroll=4)
        for part in range(2):
            for j in range(3):
                dw_ref[part, j:j + 1, :] = jnp.sum(acc[4 * part + j], axis=0, keepdims=True)
            db_ref[part] = jnp.sum(acc[4 * part + 3], axis=0, keepdims=True)

    return pl.pallas_call(
        _hide(body, 5, len(after)), name="ffn_act_bwd", grid=(F // tc,),
        in_specs=_ffn_specs(S, F, tc, c) + [pl.BlockSpec((S, tc), lambda j: (0, j))] + _hidden_specs(after),
        out_specs=[pl.BlockSpec((2, S, tc), lambda j: (0, 0, j)), pl.BlockSpec((2, 3, tc), lambda j: (0, 0, j)),
                   pl.BlockSpec((2, 1, tc), lambda j: (0, 0, j))],
        out_shape=[jax.ShapeDtypeStruct((2, S, F), BF), jax.ShapeDtypeStruct((2, 3, F), F32), jax.ShapeDtypeStruct((2, 1, F), F32)],
        scratch_shapes=[pltpu.VMEM((2, S + PAD, tc), F32), pltpu.VMEM((2, S + PAD, tc), F32)],
        compiler_params=_params(("parallel",)),
    )(up3, cws, cws, cb3, dact, *after)


def _residual_rms(a, w, x, g, name, tm=512):
    S, K = a.shape
    D = w.shape[1]
    tm = _pick(S, (tm,))

    def body(a_ref, w_ref, x_ref, g_ref, x1_ref, h_ref, r_ref):
        x1 = jnp.dot(a_ref[...], w_ref[...], preferred_element_type=F32) + x_ref[...]
        r = lax.rsqrt(jnp.mean(x1 * x1, axis=-1, keepdims=True) + EPS)
        x1_ref[...] = x1
        h_ref[...] = (x1 * r * g_ref[...]).astype(BF)
        r_ref[...] = r

    row = pl.BlockSpec((tm, D), lambda i: (i, 0))
    return pl.pallas_call(
        body, name=name, grid=(S // tm,),
        in_specs=[pl.BlockSpec((tm, K), lambda i: (i, 0)), pl.BlockSpec((K, D), lambda i: (0, 0)), row, pl.BlockSpec((1, D), lambda i: (0, 0))],
        out_specs=[row, row, pl.BlockSpec((tm, 1), lambda i: (i, 0))],
        out_shape=[jax.ShapeDtypeStruct((S, D), F32), jax.ShapeDtypeStruct((S, D), BF), jax.ShapeDtypeStruct((S, 1), F32)],
        compiler_params=_params(("parallel",)),
    )(a, w, x, g)


def _out_loss(act, w_down, x1, target, tm=1024, tn=1024, tk=1408):
    S, F = act.shape
    D = w_down.shape[1]
    tm, tn, tk = _pick(S, (tm,)), _pick(D, (tn,)), _pick(F, (tk,))
    nm, nn, nk = S // tm, D // tn, F // tk

    def body(a_ref, b_ref, x_ref, t_ref, dy_ref, dyb_ref, l_ref, acc):
        m, n, k = pl.program_id(0), pl.program_id(1), pl.program_id(2)

        @pl.when((m == 0) & (n == 0) & (k == 0))
        def _():
            l_ref[...] = jnp.zeros_like(l_ref)

        @pl.when(k == 0)
        def _():
            acc[...] = jnp.zeros_like(acc)

        acc[...] += jnp.dot(a_ref[...], b_ref[...], preferred_element_type=F32)

        @pl.when(k == nk - 1)
        def _():
            e = acc[...] + x_ref[...] - t_ref[...]
            dy = e * (1.0 / D)
            dy_ref[...] = dy
            dyb_ref[...] = dy.astype(BF)
            l_ref[...] += jnp.sum(jnp.sum(e * e, axis=-1, keepdims=True), axis=0, keepdims=True) * (0.5 / D)

    tile = pl.BlockSpec((tm, tn), lambda m, n, k: (m, n))
    return pl.pallas_call(
        body, name="mm_y_loss", grid=(nm, nn, nk),
        in_specs=[pl.BlockSpec((tm, tk), lambda m, n, k: (m, k)), pl.BlockSpec((tk, tn), lambda m, n, k: (k, n)), tile, tile],
        out_specs=[tile, tile, pl.BlockSpec((8, 128), lambda m, n, k: (0, 0))],
        out_shape=[jax.ShapeDtypeStruct((S, D), F32), jax.ShapeDtypeStruct((S, D), BF), jax.ShapeDtypeStruct((8, 128), F32)],
        scratch_shapes=[pltpu.VMEM((tm, tn), F32)],
        compiler_params=_params(("arbitrary", "arbitrary", "arbitrary")),
    )(act, w_down, x1, target)


def _allgather(shards, name):
    n = len(shards)

    def body(*refs):
        ins, outs = refs[:n], refs[n:2 * n]
        send_sems, recv_sems, local_sems = refs[2 * n:]
        x, y, c = lax.axis_index("x"), lax.axis_index("y"), lax.axis_index("c")
        me, sibling = (x, y, c), (x, y, 1 - c)
        chips = [(1 - x, y), (x, 1 - y), (1 - x, 1 - y)]

        def blk(w, px, py, pc):
            return outs[w].at[4 * px + 2 * py + pc]

        def copy(w, k, block, to, src=None):
            return pltpu.make_async_remote_copy(
                src_ref=blk(w, *block) if src is None else src, dst_ref=blk(w, *block),
                send_sem=send_sems.at[w, k], recv_sem=recv_sems.at[w, k], device_id=to, device_id_type=MESH)

        started = []
        mine = []
        for w in range(n):
            mine.append(pltpu.make_async_copy(ins[w], blk(w, *me), local_sems.at[w]))
            mine[-1].start()
            first = [copy(w, 0, me, sibling, src=ins[w])]
            first += [copy(w, 1 + j, me, (*chip, c), src=ins[w]) for j, chip in enumerate(chips)]
            for cp in first:
                cp.start()
            started += first
        for w in range(n):
            for j, chip in enumerate(chips):
                copy(w, 1 + j, (*chip, c), me).wait_recv()
                fwd = copy(w, 4 + j, (*chip, c), sibling)
                fwd.start()
                started.append(fwd)
        for w in range(n):
            copy(w, 0, sibling, me).wait_recv()
            for j, chip in enumerate(chips):
                copy(w, 4 + j, (*chip, 1 - c), me).wait_recv()
        for cp in started:
            cp.wait_send()
        for cp in mine:
            cp.wait()

    whole = pl.BlockSpec(memory_space=pltpu.VMEM)
    outs = pl.pallas_call(
        body, name=name, in_specs=[whole] * n, out_specs=[whole] * n,
        out_shape=[jax.ShapeDtypeStruct((N_DEV,) + s.shape, s.dtype) for s in shards],
        scratch_shapes=[pltpu.SemaphoreType.DMA((n, 7)), pltpu.SemaphoreType.DMA((n, 7)), pltpu.SemaphoreType.DMA((n,))],
    )(*shards)
    return list(outs)


def _allgather_seq(shards, name, collective_id, after=()):
    n = len(shards)
    n_after = len(after)

    halves = [s.shape[0] % 32 == 0 for s in shards]
    n_sem = 8
    to_diagonal = not all(halves)

    def body(*refs):
        ins, outs = refs[:n], refs[n + n_after:2 * n + n_after]
        send_sems, recv_sems, local_sems = refs[2 * n + n_after:]
        x, y, c = lax.axis_index("x"), lax.axis_index("y"), lax.axis_index("c")
        me, sibling = (x, y, c), (x, y, 1 - c)
        x_nb, y_nb, diag = (1 - x, y, c), (x, 1 - y, c), (1 - x, 1 - y, c)
        peers = [sibling, x_nb, y_nb] + ([diag] if to_diagonal else [])
        barrier = pltpu.get_barrier_semaphore()
        for peer in peers:
            pl.semaphore_signal(barrier, inc=1, device_id=peer, device_id_type=MESH)
        pl.semaphore_wait(barrier, len(peers))

        def blk(w, dev, rows=None):
            ref = outs[w].at[4 * dev[0] + 2 * dev[1] + dev[2]]
            return ref if rows is None else ref.at[rows]

        def copy(w, k, block, to, src=None, rows=None):
            return pltpu.make_async_remote_copy(
                src_ref=blk(w, block, rows) if src is None else src, dst_ref=blk(w, block, rows),
                send_sem=send_sems.at[n_sem * w + k], recv_sem=recv_sems.at[n_sem * w + k], device_id=to, device_id_type=MESH)

        def top(w):
            return pl.ds(0, shards[w].shape[0] // 2)

        def bottom(w):
            return pl.ds(shards[w].shape[0] // 2, shards[w].shape[0] // 2)

        started = []
        mine = []
        for w in range(n):
            mine.append(pltpu.make_async_copy(ins[w], blk(w, me), local_sems.at[w]))
            mine[-1].start()
            first = [copy(w, 0, me, sibling, src=ins[w]), copy(w, 1, me, x_nb, src=ins[w]), copy(w, 2, me, y_nb, src=ins[w])]
            if not halves[w]:
                first.append(copy(w, 3, me, diag, src=ins[w]))
            for cp in first:
                cp.start()
            started += first
        for w in range(n):
            copy(w, 1, x_nb, me).wait_recv()
            onward = [copy(w, 5, x_nb, sibling)] + ([copy(w, 3, x_nb, y_nb, rows=top(w))] if halves[w] else [])
            copy(w, 2, y_nb, me).wait_recv()
            onward += [copy(w, 6, y_nb, sibling)] + ([copy(w, 4, y_nb, x_nb, rows=bottom(w))] if halves[w] else [])
            for cp in onward:
                cp.start()
            started += onward
        for w in range(n):
            if halves[w]:
                copy(w, 3, diag, me, rows=top(w)).wait_recv()
                copy(w, 4, diag, me, rows=bottom(w)).wait_recv()
            else:
                copy(w, 3, diag, me).wait_recv()
            fwd = copy(w, 7, diag, sibling)
            fwd.start()
            started.append(fwd)
        for w in range(n):
            for k, dev in ((0, sibling), (5, (1 - x, y, 1 - c)), (6, (x, 1 - y, 1 - c)), (7, (1 - x, 1 - y, 1 - c))):
                copy(w, k, dev, me).wait_recv()
        for cp in started:
            cp.wait_send()
        for cp in mine:
            cp.wait()

    outs = pl.kernel(
        body, name=name, out_type=[jax.ShapeDtypeStruct((N_DEV,) + s.shape, s.dtype) for s in shards],
        mesh=plsc.ScalarSubcoreMesh(axis_name="seq", num_cores=1),
        scratch_types=[pltpu.SemaphoreType.DMA((n_sem * n,)), pltpu.SemaphoreType.DMA((n_sem * n,)), pltpu.SemaphoreType.DMA((n,))],
        compiler_params=pltpu.CompilerParams(collective_id=collective_id),
    )(*shards, *after)
    return list(outs)


def _sibling_exchange(grads, name):
    n = len(grads)

    def body(*refs):
        ins, outs = refs[:n], refs[n:2 * n]
        send_sems, recv_sems = refs[2 * n:]
        x, y, c = lax.axis_index("x"), lax.axis_index("y"), lax.axis_index("c")
        copies = [pltpu.make_async_remote_copy(
            src_ref=ins[w].at[:, 1 - c], dst_ref=outs[w], send_sem=send_sems.at[w], recv_sem=recv_sems.at[w],
            device_id=(x, y, 1 - c), device_id_type=MESH) for w in range(n)]
        for cp in copies:
            cp.start()
        for cp in copies:
            cp.wait()

    outs = pl.pallas_call(
        body, name=name, in_specs=_hidden_specs(grads), out_specs=_hidden_specs(grads),
        out_shape=[jax.ShapeDtypeStruct((g.shape[0],) + g.shape[2:], g.dtype) for g in grads],
        scratch_shapes=[pltpu.SemaphoreType.DMA((n,)), pltpu.SemaphoreType.DMA((n,))],
    )(*grads)
    return list(outs)


def _chip_exchange(sums, name, collective_id):
    n = len(sums)

    def body(*refs):
        ins, outs = refs[:n], refs[n:2 * n]
        send_sems, recv_sems = refs[2 * n:]
        x, y, c = lax.axis_index("x"), lax.axis_index("y"), lax.axis_index("c")
        chips = [(1 - x, y), (x, 1 - y), (1 - x, 1 - y)]
        barrier = pltpu.get_barrier_semaphore()
        for px, py in chips:
            pl.semaphore_signal(barrier, inc=1, device_id=(px, py, c), device_id_type=MESH)
        pl.semaphore_wait(barrier, 3)
        copies = []
        for w in range(n):
            for k, (px, py) in enumerate(chips):
                copies.append(pltpu.make_async_remote_copy(
                    src_ref=ins[w].at[2 * px + py], dst_ref=outs[w].at[k], send_sem=send_sems.at[3 * w + k],
                    recv_sem=recv_sems.at[3 * w + k], device_id=(px, py, c), device_id_type=MESH))
        for cp in copies:
            cp.start()
        for cp in copies:
            cp.wait()

    outs = pl.kernel(
        body, name=name, out_type=[jax.ShapeDtypeStruct((3,) + s.shape[1:], s.dtype) for s in sums],
        mesh=plsc.ScalarSubcoreMesh(axis_name="seq", num_cores=1),
        scratch_types=[pltpu.SemaphoreType.DMA((3 * n,)), pltpu.SemaphoreType.DMA((3 * n,))],
        compiler_params=pltpu.CompilerParams(collective_id=collective_id),
    )(*sums)
    return list(outs)


def _row_tile(r, c, elems=256 * 1024):
    want = max(8, elems // c)
    for t in range(min(want, r) // 8 * 8, 0, -8):
        if r % t == 0:
            return t
    return r


def _pair_add(g4, recv, core, name, after=()):
    _, _, r, c = g4.shape
    tr = _row_tile(r, c, 1024 * 1024)

    def body(core_ref, a_ref, b_ref, o_ref):
        o_ref[...] = (a_ref[...].astype(F32) + b_ref[...].astype(F32)).astype(BF)

    return pl.pallas_call(
        _hide(body, 3, len(after)), name=name,
        grid_spec=pltpu.PrefetchScalarGridSpec(
            num_scalar_prefetch=1, grid=(4, r // tr),
            in_specs=[pl.BlockSpec((None, None, tr, c), lambda p, i, s: (p, s[0], i, 0)),
                      pl.BlockSpec((None, tr, c), lambda p, i, s: (p, i, 0))] + _hidden_specs(after),
            out_specs=pl.BlockSpec((None, tr, c), lambda p, i, s: (p, i, 0))),
        out_shape=jax.ShapeDtypeStruct((4, r, c), BF), compiler_params=_params(("parallel", "parallel")),
    )(core, g4, recv, *after)


def _adam_math(w, g, m, v):
    m = ADAM_B1 * m + (1.0 - ADAM_B1) * g
    v = ADAM_B2 * v + (1.0 - ADAM_B2) * (g * g)
    m_hat = m / (1.0 - ADAM_B1 ** ADAM_STEP)
    v_hat = v / (1.0 - ADAM_B2 ** ADAM_STEP)
    delta = -ADAM_LR * (m_hat / (jnp.sqrt(v_hat) + ADAM_EPS) + ADAM_WD * w)
    return delta, m, v


def _adamw_big(sums, recv, chip, w, m, v, name, after=()):
    r, c = w.shape
    tr = _row_tile(r, c, 512 * 1024)

    def body(chip_ref, s_ref, r_ref, w_ref, m_ref, v_ref, g_out, d_out, m_out, v_out):
        g = s_ref[...].astype(F32) + r_ref[0].astype(F32)
        g = g + r_ref[1].astype(F32)
        g = g + r_ref[2].astype(F32)
        delta, mn, vn = _adam_math(w_ref[...], g, m_ref[...], v_ref[...])
        g_out[...] = g
        d_out[...] = delta
        m_out[...] = mn
        v_out[...] = vn

    row = pl.BlockSpec((tr, c), lambda i, s: (i, 0))
    return pl.pallas_call(
        _hide(body, 6, len(after)), name=name,
        grid_spec=pltpu.PrefetchScalarGridSpec(
            num_scalar_prefetch=1, grid=(r // tr,),
            in_specs=[pl.BlockSpec((None, tr, c), lambda i, s: (s[0], i, 0)), pl.BlockSpec((3, tr, c), lambda i, s: (0, i, 0)),
                      row, row, row] + _hidden_specs(after),
            out_specs=[row, row, row, row]),
        out_shape=[jax.ShapeDtypeStruct((r, c), F32)] * 4, compiler_params=_params(("parallel",)),
    )(chip, sums, recv, w, m, v, *after)


def _adamw_small(parts, ws, ms, vs, extra_parts, name, after=()):
    n, ne = len(ws), len(extra_parts)

    def total(p_ref):
        g = p_ref[0]
        for d in range(1, N_DEV):
            g = g + p_ref[d]
        return g

    def body(*refs):
        p_refs, w_refs, m_refs, v_refs = refs[:n], refs[n:2 * n], refs[2 * n:3 * n], refs[3 * n:4 * n]
        e_refs = refs[4 * n:4 * n + ne]
        outs = refs[4 * n + ne:]
        for i in range(n):
            g = total(p_refs[i])
            delta, mn, vn = _adam_math(w_refs[i][...], g, m_refs[i][...], v_refs[i][...])
            outs[4 * i][...] = g
            outs[4 * i + 1][...] = delta
            outs[4 * i + 2][...] = mn
            outs[4 * i + 3][...] = vn
        for i in range(ne):
            outs[4 * n + i][...] = total(e_refs[i])

    out_shape = []
    for w in ws:
        out_shape += [jax.ShapeDtypeStruct(w.shape, F32)] * 4
    out_shape += [jax.ShapeDtypeStruct(e.shape[1:], F32) for e in extra_parts]
    args = [*parts, *ws, *ms, *vs, *extra_parts]
    res = pl.pallas_call(_hide(body, len(args), len(after)), name=name, out_shape=out_shape,
                         in_specs=[pl.BlockSpec(memory_space=pltpu.VMEM)] * len(args) + _hidden_specs(after),
                         compiler_params=pltpu.CompilerParams(vmem_limit_bytes=VMEM_LIMIT))(*args, *after)
    return [res[4 * i:4 * i + 4] for i in range(n)], list(res[4 * n:])


def _adamw_plain(g, w, m, v, name):
    def body(g_ref, w_ref, m_ref, v_ref, d_out, m_out, v_out):
        delta, mn, vn = _adam_math(w_ref[...], g_ref[...], m_ref[...], v_ref[...])
        d_out[...] = delta
        m_out[...] = mn
        v_out[...] = vn

    return pl.pallas_call(body, name=name, out_shape=[jax.ShapeDtypeStruct(w.shape, F32)] * 3)(g, w, m, v)


def kernel(x, mem, positions, g_mix, w_in, g_a_v, w_spatial, b_spatial, g_b_q, g_b_k, sinks, g_mem, w_mem_kv, g_c_q, g_c_k, w_branch_a, w_branch_b, w_branch_c, w_out, g_ffn, w_up, conv_w, conv_b, w_down, loss_target, m_g_mix, m_w_in, m_g_a_v, m_w_spatial, m_b_spatial, m_g_b_q, m_g_b_k, m_sinks, m_g_mem, m_w_mem_kv, m_g_c_q, m_g_c_k, m_w_branch_a, m_w_branch_b, m_w_branch_c, m_w_out, m_g_ffn, m_w_up, m_conv_w, m_conv_b, m_w_down, v_g_mix, v_w_in, v_g_a_v, v_w_spatial, v_b_spatial, v_g_b_q, v_g_b_k, v_sinks, v_g_mem, v_w_mem_kv, v_g_c_q, v_g_c_k, v_w_branch_a, v_w_branch_b, v_w_branch_c, v_w_out, v_g_ffn, v_w_up, v_conv_w, v_conv_b, v_w_down):
    S, D = x.shape[1], x.shape[2]
    M = mem.shape[1]
    F = w_down.shape[1] * N_DEV
    in_cols = w_in.shape[2] * N_DEV
    ax, ay, ac = lax.axis_index("x"), lax.axis_index("y"), lax.axis_index("c")
    core = jnp.reshape(ac, (1,)).astype(jnp.int32)
    chip = jnp.reshape(2 * ax + ay, (1,)).astype(jnp.int32)
    me = 4 * ax + 2 * ay + ac

    x2, mem2, tgt2 = x[0], mem[0], loss_target[0]

    big = dict(w_in=w_in[0].T, w_mem_kv=w_mem_kv[0], w_branch_a=w_branch_a[0], w_branch_b=w_branch_b[0],
               w_branch_c=w_branch_c[0], w_out=w_out[0], w_up=w_up[0], w_down=w_down[0])
    names = list(big)
    cast = {k: big[k].astype(BF) for k in names}
    W = {}
    cb3 = conv_b.reshape(2, 1, F)
    W["w_in"], = _allgather_seq([cast["w_in"]], "ag_seq0", 0)
    w_in_t = W["w_in"].reshape(in_cols, D)
    grp1 = ["w_mem_kv", "w_branch_a", "w_branch_b", "w_branch_c", "w_out"]
    res1 = _allgather_seq([cast[k] for k in grp1] + [conv_w[0]], "ag_seq1", 1, after=(_token((w_in_t,), "tok_w_in"),))
    W.update(zip(grp1, res1))
    cw3 = res1[-1]
    w_kv_f = W["w_mem_kv"].reshape(D, 2 * C_WIDTH)
    w_out_f = W["w_out"].reshape(D, D)

    half = ROPE_DIM // 2
    inv = ROPE_THETA ** (-jnp.arange(half, dtype=F32) / half)
    ang = positions[0].astype(F32)[:, None] * inv
    cos, sin = jnp.cos(ang), jnp.sin(ang)
    one, zero = jnp.ones((S, B_HEAD_DIM - ROPE_DIM), F32), jnp.zeros((S, B_HEAD_DIM - ROPE_DIM), F32)
    z8 = jnp.zeros((S, half), F32)
    ct = jnp.tile(jnp.concatenate([cos, cos, one], axis=1), (1, 2))
    sa = jnp.tile(jnp.concatenate([-sin, z8, zero], axis=1), (1, 2))
    sb = jnp.tile(jnp.concatenate([z8, sin, zero], axis=1), (1, 2))
    gq2, gk2 = jnp.tile(g_b_q, (1, 2)), jnp.tile(g_b_k, (1, 2))
    b_t = b_spatial[0].T

    h, rstd1 = _rms_fwd(x2, g_mix, "rms1_fwd")
    proj = _mm(h, w_in_t, "nt", F32, "mm_proj", tn=1280)
    y_a = _a_fwd(proj, g_a_v, w_spatial[0], b_t)
    W["w_up"], = _allgather_seq([cast["w_up"]], "ag_seq2", 2, after=(_token((W["w_out"], proj), "tok_group1"),))
    qn, kn = _b_pre(proj, gq2, gk2, ct, sa, sb)
    y_b = _b_attn_fwd(qn, kn, proj, sinks)
    mem_h, rstd_m = _rms_fwd(mem2, g_mem, "rmsmem_fwd")
    kv = _mm(mem_h, w_kv_f, "nn", F32, "mm_kv", after=(y_b,))
    y_c = _c_fwd(proj, kv, g_c_q, g_c_k)
    w_branches = [W["w_branch_a"], W["w_branch_b"], W["w_branch_c"]]
    merged, z_a, z_b, z_c = _merge_fwd(proj, [y_a, y_b, y_c], w_branches)
    x1, h2, rstd2 = _residual_rms(merged, w_out_f, x2, g_ffn, "mm_x1_rms2")
    W["w_down"], = _allgather_seq([cast["w_down"]], "ag_seq3", 3, after=(W["w_up"], h2))
    w_down_f = W["w_down"].reshape(F, D)
    up3 = _mm(h2, W["w_up"], "nn", BF, "mm_up", b_stack=True, out_parts=2)
    act = _ffn_act_fwd(up3, cw3, cb3)
    dy, dy_b, loss_acc = _out_loss(act, w_down_f, x1, tgt2)

    reduced = {}

    def as4(g):
        return g.reshape(4, 2, g.shape[1], g.shape[2])

    def finish_group(gi, keys, g4, from_sibling):
        sums = [_pair_add(a, b, core, "rs_add_" + k) for k, a, b in zip(keys, g4, from_sibling)]
        from_chips = _chip_exchange(sums, f"rs_chip{gi}", 4 + gi)
        reduced.update(zip(keys, zip(sums, from_chips)))
        return tuple(sums)

    d_act = _mm(dy_b, w_down_f, "nt", BF, "mm_dact", tn=1408)
    g_down = _mm(act, dy_b, "tn", BF, "mm_gdown", tm=1408)
    d_up3, d_cw3, d_cb3 = _ffn_act_bwd(up3, cw3, cb3, d_act, after=(g_down,))
    grp0 = [as4(g_down.reshape(N_DEV, F // N_DEV, D))]
    g_up, sib0 = _mm(h2, d_up3, "tn", BF, "mm_gup", b_parts=2, out_stack=True, exchange=grp0)
    sums0 = finish_group(0, ["w_down"], grp0, sib0)
    grp1 = [as4(g_up)]
    d_h2, sib1 = _mm(d_up3, W["w_up"], "nt", F32, "mm_dh2", a_parts=2, b_stack=True, tm=2048, after=sums0, exchange=grp1)
    sums1 = finish_group(1, ["w_up"], grp1, sib1)
    dx1, dx1_b, d_g_ffn = _rms_bwd(x1, rstd2, g_ffn, d_h2, dy, "rms2_bwd", after=sums1)
    g_out = _mm(merged, dx1_b, "tn", BF, "mm_gout")
    grp2 = [as4(g_out.reshape(N_DEV, D // N_DEV, D))]
    d_merged, sib2 = _mm(dx1_b, w_out_f, "nt", F32, "mm_dmerged", exchange=grp2)
    sums2 = finish_group(2, ["w_out"], grp2, sib2)
    dz_a, dz_b, dz_c, dga, dgb, dgc, dy_a, dy_b_, dy_c = _merge_bwd(proj, [z_a, z_b, z_c], d_merged, w_branches, after=sums2)
    g_ba = _mm(y_a, dz_a, "tn", BF, "mm_gba", out_stack=True)
    g_bb = _mm(y_b, dz_b, "tn", BF, "mm_gbb", out_stack=True)
    g_bc = _mm(y_c, dz_c, "tn", BF, "mm_gbc", out_stack=True)
    d_uv, d_g_a_v, d_w_s, d_b_t = _a_bwd(proj, g_a_v, w_spatial[0], b_t, dy_a, after=(g_ba, g_bb, g_bc))
    dqn, dkn, dv_b, dsink_rows = _b_attn_bwd(qn, kn, proj, sinks, dy_b_)
    d_qkv, d_gq2, d_gk2 = _b_pre_bwd(proj, gq2, gk2, ct, sa, sb, dqn, dkn, dv_b)
    dq_c, dk_c, dv_c, d_gcq, d_gck = _c_bwd(proj, kv, g_c_q, g_c_k, dy_c)
    dkv_b = jnp.concatenate([dk_c, dv_c], axis=1).astype(BF)
    d_memh = _mm(dkv_b, w_kv_f, "nt", F32, "mm_dmemh")
    g_kv = _mm(mem_h, dkv_b, "tn", BF, "mm_gkv")
    _, _, d_g_mem = _rms_bwd(mem2, rstd_m, g_mem, d_memh, None, "rmsmem_bwd")
    dproj = jnp.concatenate([d_uv, d_qkv, dq_c, dga, dgb, dgc], axis=1)
    grp3 = [as4(g_ba), as4(g_bb), as4(g_bc)]
    g_in, sib3 = _mm(dproj, h, "tn", BF, "mm_gin", tm=1280, exchange=grp3)
    sums3 = finish_group(3, ["w_branch_a", "w_branch_b", "w_branch_c"], grp3, sib3)
    grp4 = [as4(g_in.reshape(N_DEV, in_cols // N_DEV, D)), as4(g_kv.reshape(N_DEV, D // N_DEV, 2 * C_WIDTH))]
    sums4 = finish_group(4, ["w_in", "w_mem_kv"], grp4, _sibling_exchange(grp4, "rs_sib4"))
    d_h = _mm(dproj, w_in_t, "nn", F32, "mm_dh", tm=2048, tk=1280, after=sums3 + sums4)
    grad_x, _, d_g_mix = _rms_bwd(x2, rstd1, g_mix, d_h, dx1, "rms1_bwd", after=sums4)

    small_names =["g_mix", "g_a_v", "w_spatial", "b_spatial", "g_b_q", "g_b_k", "sinks", "g_mem", "g_c_q", "g_c_k", "g_ffn", "conv_b"]
    small_w = dict(g_mix=g_mix, g_a_v=g_a_v, w_spatial=w_spatial, b_spatial=b_spatial, g_b_q=g_b_q, g_b_k=g_b_k, sinks=sinks,
                   g_mem=g_mem, g_c_q=g_c_q, g_c_k=g_c_k, g_ffn=g_ffn, conv_b=conv_b)
    small_m = dict(g_mix=m_g_mix, g_a_v=m_g_a_v, w_spatial=m_w_spatial, b_spatial=m_b_spatial, g_b_q=m_g_b_q, g_b_k=m_g_b_k,
                   sinks=m_sinks, g_mem=m_g_mem, g_c_q=m_g_c_q, g_c_k=m_g_c_k, g_ffn=m_g_ffn, conv_b=m_conv_b)
    small_v = dict(g_mix=v_g_mix, g_a_v=v_g_a_v, w_spatial=v_w_spatial, b_spatial=v_b_spatial, g_b_q=v_g_b_q, g_b_k=v_g_b_k,
                   sinks=v_sinks, g_mem=v_g_mem, g_c_q=v_g_c_q, g_c_k=v_g_c_k, g_ffn=v_g_ffn, conv_b=v_conv_b)
    small_g = dict(
        g_mix=d_g_mix, g_a_v=d_g_a_v, w_spatial=d_w_s, b_spatial=d_b_t.T,
        g_b_q=d_gq2.reshape(2, B_HEAD_DIM).sum(0), g_b_k=d_gk2.reshape(2, B_HEAD_DIM).sum(0),
        sinks=dsink_rows.sum(0)[:B_HEADS], g_mem=d_g_mem, g_c_q=d_gcq.sum(0), g_c_k=d_gck.sum(0), g_ffn=d_g_ffn,
        conv_b=d_cb3)
    partial = [small_g[k].reshape(small_w[k].shape) for k in small_names] + [d_cw3, loss_acc[0:1]]
    parts = _allgather_seq(partial, "ag_small", 9)
    n_small = len(small_names)
    big_out = {}

    moments = dict(w_in=(m_w_in, v_w_in), w_mem_kv=(m_w_mem_kv, v_w_mem_kv), w_branch_a=(m_w_branch_a, v_w_branch_a),
                   w_branch_b=(m_w_branch_b, v_w_branch_b), w_branch_c=(m_w_branch_c, v_w_branch_c), w_out=(m_w_out, v_w_out),
                   w_up=(m_w_up, v_w_up), w_down=(m_w_down, v_w_down))
    token = (grad_x,)
    for k in ["w_down", "w_up", "w_out", "w_branch_a", "w_branch_b", "w_branch_c", "w_mem_kv", "w_in"]:
        s, r = reduced[k]
        mk, vk = moments[k][0][0], moments[k][1][0]
        if k == "w_in":
            res = _adamw_big(s, r, chip, big[k], mk.T, vk.T, "adamw_" + k, after=token)
            big_out[k] = [a.T[None] for a in res]
        else:
            res = _adamw_big(s, r, chip, big[k], mk, vk, "adamw_" + k, after=token)
            big_out[k] = [a[None] for a in res]
        token = (res[0],)

    small_res, (g_cw3, loss_row) = _adamw_small(parts[:n_small], [small_w[k] for k in small_names], [small_m[k] for k in small_names],
                                                [small_v[k] for k in small_names], parts[n_small:], "adamw_small", after=token)
    loss = loss_row[0, 0]
    small_out = dict(zip(small_names, small_res))
    c_cw = 2 * F // N_DEV
    g_cw = lax.dynamic_slice(g_cw3, (me // (N_DEV // 2), 0, (me % (N_DEV // 2)) * c_cw), (1, 3, c_cw))[0]
    cw_res = _adamw_plain(g_cw, conv_w[0], m_conv_w[0], v_conv_w[0], "adamw_conv_w")
    big_out["conv_w"] = [g_cw[None]] + [a[None] for a in cw_res]

    order = ["g_mix", "w_in", "g_a_v", "w_spatial", "b_spatial", "g_b_q", "g_b_k", "sinks", "g_mem", "w_mem_kv", "g_c_q", "g_c_k",
             "w_branch_a", "w_branch_b", "w_branch_c", "w_out", "g_ffn", "w_up", "conv_w", "conv_b", "w_down"]
    res = {**small_out, **big_out}
    outs = [loss, grad_x[None]]
    for field in range(4):
        outs += [res[k][field] for k in order]
    return tuple(outs)
```

```python
import functools

import jax
import jax.numpy as jnp
from jax import lax
from jax.experimental import pallas as pl
from jax.experimental.pallas import tpu as pltpu
from jax.experimental.pallas import tpu_sc as plsc

F32 = jnp.float32
BF = jnp.bfloat16
EPS = 1e-6
NEG = -1e30

N_DEV = 8
CHUNK = 128
A_GROUPS = 4
A_WIDTH = 512
B_HEADS = 16
B_KV_HEADS = 2
B_HEAD_DIM = 64
B_WIDTH = 1024
B_KV_WIDTH = 128
ROPE_DIM = 16
ROPE_THETA = 500000.0
C_HEADS = 4
C_HEAD_DIM = 128
C_WIDTH = 512
GATE_OFF = 2 * A_WIDTH + B_WIDTH + 2 * B_KV_WIDTH + C_WIDTH

ADAM_LR = 0.001
ADAM_B1 = 0.9
ADAM_B2 = 0.999
ADAM_EPS = 1e-08
ADAM_WD = 0.01
ADAM_STEP = 10

VMEM_LIMIT = 48 * 1024 * 1024
MESH = pl.DeviceIdType.MESH


def _pick(n, prefs):
    for p in prefs:
        if p <= n and n % p == 0:
            return p
    return n


def _params(sem):
    return pltpu.CompilerParams(dimension_semantics=sem, vmem_limit_bytes=VMEM_LIMIT)


def _hide(body, n_seen, n_hidden):
    if not n_hidden:
        return body

    def wrapped(*refs):
        return body(*refs[:n_seen], *refs[n_seen + n_hidden:])

    return wrapped


def _hidden_specs(after):
    return [pl.BlockSpec(memory_space=pl.ANY) for _ in after]


def _token(xs, name):
    def body(*refs):
        refs[-1][...] = jnp.zeros_like(refs[-1])

    return pl.pallas_call(body, name=name, in_specs=_hidden_specs(xs), out_shape=jax.ShapeDtypeStruct((8, 128), F32))(*xs)


def _mm(a, b, mode, out_dtype, name, *, resid=None, b_stack=False, a_parts=0, b_parts=0, out_parts=0,
        out_stack=False, tm=1024, tn=1024, tk=2048, after=(), exchange=()):
    if mode == "nn":
        M = a.shape[-2]
        K = a.shape[-1] * max(a_parts, 1)
        N = b.shape[-1] * (N_DEV if b_stack else 1)
        dims = (((1,), (0,)), ((), ()))
    elif mode == "nt":
        M = a.shape[-2]
        K = a.shape[-1] * max(a_parts, 1)
        N = b.shape[-2]
        dims = (((1,), (1,)), ((), ()))
    else:
        K = a.shape[-2]
        M = a.shape[-1]
        N = b.shape[-1] * max(b_parts, 1)
        dims = (((0,), (0,)), ((), ()))
    if b_stack and mode == "nn":
        tn = b.shape[-1]
    if b_stack and mode == "nt":
        tk = b.shape[-1]
    if out_stack:
        tn = N // N_DEV
    tm, tn, tk = _pick(M, (tm,)), _pick(N, (tn,)), _pick(K, (tk,))
    if M % tm or N % tn or K % tk:
        raise ValueError(f"{name}: tiles {tm},{tn},{tk} do not divide {M},{N},{K}")
    nm, nn, nk = M // tm, N // tn, K // tk

    def parts_idx(t, ntile, parts):
        per = ntile // parts
        return t // per, t % per

    if mode in ("nn", "nt"):
        if a_parts:
            a_spec = pl.BlockSpec((None, tm, tk), lambda m, n, k: (parts_idx(k, nk, a_parts)[0], m, parts_idx(k, nk, a_parts)[1]))
        else:
            a_spec = pl.BlockSpec((tm, tk), lambda m, n, k: (m, k))
    else:
        a_spec = pl.BlockSpec((tk, tm), lambda m, n, k: (k, m))
    if mode == "nn":
        if b_stack:
            b_spec = pl.BlockSpec((None, tk, tn), lambda m, n, k: (n, k, 0))
        else:
            b_spec = pl.BlockSpec((tk, tn), lambda m, n, k: (k, n))
    elif mode == "nt":
        if b_stack:
            b_spec = pl.BlockSpec((None, tn, tk), lambda m, n, k: (k, n, 0))
        else:
            b_spec = pl.BlockSpec((tn, tk), lambda m, n, k: (n, k))
    else:
        if b_parts:
            b_spec = pl.BlockSpec((None, tk, tn), lambda m, n, k: (parts_idx(n, nn, b_parts)[0], k, parts_idx(n, nn, b_parts)[1]))
        else:
            b_spec = pl.BlockSpec((tk, tn), lambda m, n, k: (k, n))
    if out_stack:
        out_shape = jax.ShapeDtypeStruct((N_DEV, M, tn), out_dtype)
        o_spec = pl.BlockSpec((None, tm, tn), lambda m, n, k: (n, m, 0))
    elif out_parts:
        out_shape = jax.ShapeDtypeStruct((out_parts, M, N // out_parts), out_dtype)
        o_spec = pl.BlockSpec((None, tm, tn), lambda m, n, k: (parts_idx(n, nn, out_parts)[0], m, parts_idx(n, nn, out_parts)[1]))
    else:
        out_shape = jax.ShapeDtypeStruct((M, N), out_dtype)
        o_spec = pl.BlockSpec((tm, tn), lambda m, n, k: (m, n))
    has_resid = resid is not None

    n_ex = len(exchange)
    n_in = 2 + has_resid + len(after)

    def body(*refs):
        a_ref, b_ref = refs[:2]
        r_ref = refs[2] if has_resid else None
        ex_in = refs[n_in:n_in + n_ex]
        o_ref = refs[n_in + n_ex]
        ex_out = refs[n_in + n_ex + 1:n_in + 2 * n_ex + 1]
        scratch = refs[n_in + 2 * n_ex + 1:]
        m_i, n_i, k = pl.program_id(0), pl.program_id(1), pl.program_id(2)

        def pushes():
            send_sems, recv_sems = scratch[-2:]
            x, y, c = lax.axis_index("x"), lax.axis_index("y"), lax.axis_index("c")
            return [pltpu.make_async_remote_copy(
                src_ref=ex_in[w].at[:, 1 - c], dst_ref=ex_out[w], send_sem=send_sems.at[w], recv_sem=recv_sems.at[w],
                device_id=(x, y, 1 - c), device_id_type=MESH) for w in range(n_ex)]

        if n_ex:
            @pl.when((m_i == 0) & (n_i == 0) & (k == 0))
            def _():
                for cp in pushes():
                    cp.start()

        if nk == 1:
            res = lax.dot_general(a_ref[...], b_ref[...], dims, preferred_element_type=F32)
            if has_resid:
                res = res + r_ref[...]
            o_ref[...] = res.astype(o_ref.dtype)
        else:
            acc = scratch[0]

            @pl.when(k == 0)
            def _():
                acc[...] = jnp.zeros_like(acc)

            acc[...] += lax.dot_general(a_ref[...], b_ref[...], dims, preferred_element_type=F32)

            @pl.when(k == nk - 1)
            def _():
                res = acc[...]
                if has_resid:
                    res = res + r_ref[...]
                o_ref[...] = res.astype(o_ref.dtype)

        if n_ex:
            @pl.when((m_i == nm - 1) & (n_i == nn - 1) & (k == nk - 1))
            def _():
                for cp in pushes():
                    cp.wait()

    in_specs = [a_spec, b_spec]
    args = [a, b]
    if has_resid:
        in_specs.append(pl.BlockSpec((tm, tn), lambda m, n, k: (m, n)))
        args.append(resid)
    in_specs += _hidden_specs(after) + _hidden_specs(exchange)
    args += list(after) + list(exchange)
    scratch_shapes = [pltpu.VMEM((tm, tn), F32)] if nk > 1 else []
    if not n_ex:
        return pl.pallas_call(
            body, name=name, grid=(nm, nn, nk), in_specs=in_specs, out_specs=o_spec, out_shape=out_shape,
            scratch_shapes=scratch_shapes, compiler_params=_params(("parallel", "parallel", "arbitrary")),
        )(*args)
    res = pl.pallas_call(
        body, name=name, grid=(nm, nn, nk), in_specs=in_specs, out_specs=[o_spec] + _hidden_specs(exchange),
        out_shape=[out_shape] + [jax.ShapeDtypeStruct((g.shape[0],) + g.shape[2:], g.dtype) for g in exchange],
        scratch_shapes=scratch_shapes + [pltpu.SemaphoreType.DMA((n_ex,)), pltpu.SemaphoreType.DMA((n_ex,))],
        compiler_params=_params(("arbitrary", "arbitrary", "arbitrary")),
    )(*args)
    return res[0], list(res[1:])


def _rms_fwd(x, g, name):
    R, D = x.shape
    tr = _pick(R, (256,))

    def body(x_ref, g_ref, h_ref, r_ref):
        xv = x_ref[...]
        r = lax.rsqrt(jnp.mean(xv * xv, axis=-1, keepdims=True) + EPS)
        h_ref[...] = (xv * r * g_ref[...]).astype(BF)
        r_ref[...] = r

    return pl.pallas_call(
        body, name=name, grid=(R // tr,),
        in_specs=[pl.BlockSpec((tr, D), lambda i: (i, 0)), pl.BlockSpec((1, D), lambda i: (0, 0))],
        out_specs=[pl.BlockSpec((tr, D), lambda i: (i, 0)), pl.BlockSpec((tr, 1), lambda i: (i, 0))],
        out_shape=[jax.ShapeDtypeStruct((R, D), BF), jax.ShapeDtypeStruct((R, 1), F32)],
        compiler_params=_params(("parallel",)),
    )(x, g)


def _rms_bwd(x, r, g, dh, dres, name, after=()):
    R, D = x.shape
    tr = _pick(R, (256,))
    has_res = dres is not None

    def body(*refs):
        if has_res:
            x_ref, r_ref, g_ref, dh_ref, dres_ref, dx_ref, dxb_ref, dg_ref = refs
        else:
            x_ref, r_ref, g_ref, dh_ref, dx_ref, dxb_ref, dg_ref = refs
        i = pl.program_id(0)
        xv, rv, dhv = x_ref[...], r_ref[...], dh_ref[...]
        gy = dhv * g_ref[...]
        c = jnp.sum(xv * gy, axis=-1, keepdims=True)
        dx = rv * gy - xv * (rv * rv * rv) * (c * (1.0 / D))
        if has_res:
            dx = dx + dres_ref[...]
        dx_ref[...] = dx
        dxb_ref[...] = dx.astype(BF)
        part = jnp.sum(dhv * xv * rv, axis=0, keepdims=True)

        @pl.when(i == 0)
        def _():
            dg_ref[...] = part

        @pl.when(i > 0)
        def _():
            dg_ref[...] += part

    row = pl.BlockSpec((tr, D), lambda i: (i, 0))
    in_specs = [row, pl.BlockSpec((tr, 1), lambda i: (i, 0)), pl.BlockSpec((1, D), lambda i: (0, 0)), row]
    args = [x, r, g, dh]
    if has_res:
        in_specs.append(row)
        args.append(dres)
    return pl.pallas_call(
        _hide(body, len(args), len(after)), name=name, grid=(R // tr,), in_specs=in_specs + _hidden_specs(after),
        out_specs=[row, row, pl.BlockSpec((1, D), lambda i: (0, 0))],
        out_shape=[jax.ShapeDtypeStruct((R, D), F32), jax.ShapeDtypeStruct((R, D), BF), jax.ShapeDtypeStruct((1, D), F32)],
        compiler_params=_params(("arbitrary",)),
    )(*args, *after)


def _a_chunk(us, vs, gvs, ws, bs):
    r_i = lax.broadcasted_iota(jnp.int32, (CHUNK, CHUNK), 0)
    c_i = lax.broadcasted_iota(jnp.int32, (CHUNK, CHUNK), 1)
    causal = r_i >= c_i
    vg = [jax.nn.gelu(v) for v in vs]
    ss = sum(jnp.sum(v * v, axis=-1, keepdims=True) for v in vg)
    r = lax.rsqrt(ss * (1.0 / A_WIDTH) + EPS)
    ys = []
    for g in range(A_GROUPS):
        vn = vg[g] * r * gvs[g]
        w = jnp.where(causal, ws[g], 0.0)
        s = jnp.dot(w.astype(BF), vn.astype(BF), preferred_element_type=F32) + bs[g]
        ys.append(jax.nn.gelu(us[g]) * s)
    return ys


def _a_split(u_ref, v_ref, g_ref, w_ref, b_ref):
    sl = [slice(g * 128, (g + 1) * 128) for g in range(A_GROUPS)]
    return ([u_ref[:, s] for s in sl], [v_ref[:, s] for s in sl], [g_ref[:, s] for s in sl],
            [w_ref[g] for g in range(A_GROUPS)], [b_ref[:, g:g + 1] for g in range(A_GROUPS)])


def _a_specs(S):
    return [pl.BlockSpec((CHUNK, A_WIDTH), lambda n: (n, 0)), pl.BlockSpec((CHUNK, A_WIDTH), lambda n: (n, 1)),
            pl.BlockSpec((1, A_WIDTH), lambda n: (0, 0)), pl.BlockSpec((A_GROUPS, CHUNK, CHUNK), lambda n: (0, 0, 0)),
            pl.BlockSpec((CHUNK, A_GROUPS), lambda n: (0, 0))]


def _a_fwd(proj, g_v, w_s, b_t):
    S = proj.shape[0]

    def body(u_ref, v_ref, g_ref, w_ref, b_ref, y_ref):
        ys = _a_chunk(*_a_split(u_ref, v_ref, g_ref, w_ref, b_ref))
        for g in range(A_GROUPS):
            y_ref[:, g * 128:(g + 1) * 128] = ys[g].astype(BF)

    return pl.pallas_call(
        body, name="a_fwd", grid=(S // CHUNK,), in_specs=_a_specs(S),
        out_specs=pl.BlockSpec((CHUNK, A_WIDTH), lambda n: (n, 0)),
        out_shape=jax.ShapeDtypeStruct((S, A_WIDTH), BF), compiler_params=_params(("parallel",)),
    )(proj, proj, g_v, w_s, b_t)


def _a_bwd(proj, g_v, w_s, b_t, dy, after=()):
    S = proj.shape[0]

    def body(u_ref, v_ref, g_ref, w_ref, b_ref, dy_ref, duv_ref, dg_ref, dw_ref, db_ref):
        n = pl.program_id(0)
        dys = [dy_ref[:, g * 128:(g + 1) * 128] for g in range(A_GROUPS)]
        _, vjp = jax.vjp(_a_chunk, *_a_split(u_ref, v_ref, g_ref, w_ref, b_ref))
        dus, dvs, dgs, dws, dbs = vjp(dys)

        @pl.when(n == 0)
        def _():
            dg_ref[...] = jnp.zeros_like(dg_ref)
            dw_ref[...] = jnp.zeros_like(dw_ref)
            db_ref[...] = jnp.zeros_like(db_ref)

        for g in range(A_GROUPS):
            duv_ref[:, g * 128:(g + 1) * 128] = dus[g].astype(BF)
            duv_ref[:, A_WIDTH + g * 128:A_WIDTH + (g + 1) * 128] = dvs[g].astype(BF)
            dg_ref[:, g * 128:(g + 1) * 128] += dgs[g]
            dw_ref[g] += dws[g]
            db_ref[:, g:g + 1] += dbs[g]

    return pl.pallas_call(
        _hide(body, 6, len(after)), name="a_bwd", grid=(S // CHUNK,),
        in_specs=_a_specs(S) + [pl.BlockSpec((CHUNK, A_WIDTH), lambda n: (n, 0))] + _hidden_specs(after),
        out_specs=[pl.BlockSpec((CHUNK, 2 * A_WIDTH), lambda n: (n, 0)), pl.BlockSpec((1, A_WIDTH), lambda n: (0, 0)),
                   pl.BlockSpec((A_GROUPS, CHUNK, CHUNK), lambda n: (0, 0, 0)), pl.BlockSpec((CHUNK, A_GROUPS), lambda n: (0, 0))],
        out_shape=[jax.ShapeDtypeStruct((S, 2 * A_WIDTH), BF), jax.ShapeDtypeStruct((1, A_WIDTH), F32),
                   jax.ShapeDtypeStruct((A_GROUPS, CHUNK, CHUNK), F32), jax.ShapeDtypeStruct((CHUNK, A_GROUPS), F32)],
        compiler_params=_params(("arbitrary",)),
    )(proj, proj, g_v, w_s, b_t, dy, *after)


def _half_mask(shape, which):
    lane = lax.broadcasted_iota(jnp.int32, shape, len(shape) - 1)
    return (lane >= 64) == (which == 1)


def _pair_norm_rope(x, g, ct, sa, sb):
    lo = _half_mask(x.shape, 0)
    x2 = x * x
    ss_lo = jnp.sum(jnp.where(lo, x2, 0.0), axis=-1, keepdims=True)
    ss_hi = jnp.sum(jnp.where(lo, 0.0, x2), axis=-1, keepdims=True)
    r = jnp.where(lo, lax.rsqrt(ss_lo * (1.0 / B_HEAD_DIM) + EPS), lax.rsqrt(ss_hi * (1.0 / B_HEAD_DIM) + EPS))
    xr = x * r
    xn = xr * g
    out = xn * ct + pltpu.roll(xn, 120, 1) * sa + pltpu.roll(xn, 8, 1) * sb
    return out, xr, r


def _pair_norm_rope_bwd(x, g, ct, sa, sb, dout):
    lo = _half_mask(x.shape, 0)
    _, xr, r = _pair_norm_rope(x, g, ct, sa, sb)
    dxn = dout * ct + pltpu.roll(dout * sa, 8, 1) + pltpu.roll(dout * sb, 120, 1)
    gy = dxn * g
    t = xr * gy
    c_lo = jnp.sum(jnp.where(lo, t, 0.0), axis=-1, keepdims=True)
    c_hi = jnp.sum(jnp.where(lo, 0.0, t), axis=-1, keepdims=True)
    c = jnp.where(lo, c_lo, c_hi)
    dx = r * (gy - xr * c * (1.0 / B_HEAD_DIM))
    dg = jnp.sum(dxn * xr, axis=0, keepdims=True)
    return dx, dg


def _b_pre(proj, gq2, gk2, ct, sa, sb):
    S = proj.shape[0]
    tr = _pick(S, (256,))
    n_pair = B_WIDTH // 128

    def body(q_ref, k_ref, gq_ref, gk_ref, ct_ref, sa_ref, sb_ref, qn_ref, kn_ref):
        ct_v, sa_v, sb_v = ct_ref[...], sa_ref[...], sb_ref[...]
        for p in range(n_pair):
            o, _, _ = _pair_norm_rope(q_ref[:, p * 128:(p + 1) * 128], gq_ref[...], ct_v, sa_v, sb_v)
            qn_ref[:, p * 128:(p + 1) * 128] = o.astype(BF)
        o, _, _ = _pair_norm_rope(k_ref[...], gk_ref[...], ct_v, sa_v, sb_v)
        kn_ref[...] = o.astype(BF)

    tab = pl.BlockSpec((tr, 128), lambda i: (i, 0))
    gsp = pl.BlockSpec((1, 128), lambda i: (0, 0))
    return pl.pallas_call(
        body, name="b_pre", grid=(S // tr,),
        in_specs=[pl.BlockSpec((tr, B_WIDTH), lambda i: (i, 1)), pl.BlockSpec((tr, 128), lambda i: (i, 2 * B_WIDTH // 128)),
                  gsp, gsp, tab, tab, tab],
        out_specs=[pl.BlockSpec((tr, B_WIDTH), lambda i: (i, 0)), tab],
        out_shape=[jax.ShapeDtypeStruct((S, B_WIDTH), BF), jax.ShapeDtypeStruct((S, 128), BF)],
        compiler_params=_params(("parallel",)),
    )(proj, proj, gq2, gk2, ct, sa, sb)


def _b_pre_bwd(proj, gq2, gk2, ct, sa, sb, dqn, dkn, dv):
    S = proj.shape[0]
    tr = _pick(S, (256,))
    n_pair = B_WIDTH // 128

    def body(q_ref, k_ref, gq_ref, gk_ref, ct_ref, sa_ref, sb_ref, dqn_ref, dkn_ref, dv_ref, dqkv_ref, dgq_ref, dgk_ref):
        i = pl.program_id(0)
        ct_v, sa_v, sb_v = ct_ref[...], sa_ref[...], sb_ref[...]
        dgq = jnp.zeros((1, 128), F32)
        for p in range(n_pair):
            sl = slice(p * 128, (p + 1) * 128)
            dx, dg = _pair_norm_rope_bwd(q_ref[:, sl], gq_ref[...], ct_v, sa_v, sb_v, dqn_ref[:, sl])
            dqkv_ref[:, sl] = dx.astype(BF)
            dgq = dgq + dg
        dx, dgk = _pair_norm_rope_bwd(k_ref[...], gk_ref[...], ct_v, sa_v, sb_v, dkn_ref[...])
        dqkv_ref[:, B_WIDTH:B_WIDTH + 128] = dx.astype(BF)
        dqkv_ref[:, B_WIDTH + 128:B_WIDTH + 256] = dv_ref[...].astype(BF)

        @pl.when(i == 0)
        def _():
            dgq_ref[...] = dgq
            dgk_ref[...] = dgk

        @pl.when(i > 0)
        def _():
            dgq_ref[...] += dgq
            dgk_ref[...] += dgk

    tab = pl.BlockSpec((tr, 128), lambda i: (i, 0))
    gsp = pl.BlockSpec((1, 128), lambda i: (0, 0))
    return pl.pallas_call(
        body, name="b_pre_bwd", grid=(S // tr,),
        in_specs=[pl.BlockSpec((tr, B_WIDTH), lambda i: (i, 1)), pl.BlockSpec((tr, 128), lambda i: (i, 2 * B_WIDTH // 128)),
                  gsp, gsp, tab, tab, tab, pl.BlockSpec((tr, B_WIDTH), lambda i: (i, 0)), tab, tab],
        out_specs=[pl.BlockSpec((tr, B_WIDTH + 256), lambda i: (i, 0)), gsp, gsp],
        out_shape=[jax.ShapeDtypeStruct((S, B_WIDTH + 256), BF), jax.ShapeDtypeStruct((1, 128), F32), jax.ShapeDtypeStruct((1, 128), F32)],
        compiler_params=_params(("arbitrary",)),
    )(proj, proj, gq2, gk2, ct, sa, sb, dqn, dkn, dv)


def _b_dup(x2, g):
    d = jnp.where(_half_mask(x2.shape, g), x2, 0.0)
    return (d + pltpu.roll(d, 64, 1)).astype(BF)


PAIRS_PER_GROUP = B_HEADS // B_KV_HEADS // 2
GROUP_ROWS = PAIRS_PER_GROUP * CHUNK


def _b_valid(n):
    row = lax.broadcasted_iota(jnp.int32, (GROUP_ROWS, 2 * CHUNK), 0) & (CHUNK - 1)
    col = lax.broadcasted_iota(jnp.int32, (GROUP_ROWS, 2 * CHUNK), 1)
    rel = row + CHUNK - col
    return (rel >= 0) & (rel < CHUNK) & ((col >= CHUNK) | (n > 0))


def _b_blocks(x2, g):
    xd = _b_dup(x2, g)
    lo = _half_mask(xd.shape, 0)
    zero = jnp.zeros_like(xd)
    return jnp.concatenate([jnp.where(lo, xd, zero), jnp.where(lo, zero, xd)], axis=0)


def _b_sink_col(s_ref, g, hf):
    rb = lax.broadcasted_iota(jnp.int32, (GROUP_ROWS, 1), 0) // CHUNK
    col = jnp.zeros((GROUP_ROWS, 1), F32)
    for pp in range(PAIRS_PER_GROUP):
        col = jnp.where(rb == pp, s_ref[0, 2 * (g * PAIRS_PER_GROUP + pp) + hf], col)
    return col


def _b_probs(qs, kblk, valid, sinks):
    s = lax.dot_general(qs, kblk, (((1,), (1,)), ((), ())), preferred_element_type=F32) * (B_HEAD_DIM ** -0.5)
    out = []
    for hf in range(2):
        sh = jnp.where(valid, s[:, hf * 2 * CHUNK:(hf + 1) * 2 * CHUNK], NEG)
        m = jnp.maximum(jnp.max(sh, axis=-1, keepdims=True), sinks[hf])
        e = jnp.exp(sh - m)
        es = jnp.exp(sinks[hf] - m)
        inv = 1.0 / (jnp.sum(e, axis=-1, keepdims=True) + es)
        out.append((e * inv, es * inv))
    return out


def _b_fold(acc, g):
    lo = _half_mask((2 * CHUNK, 128), 0)
    t = jnp.where(lo, acc[:2 * CHUNK], 0.0) + jnp.where(lo, 0.0, acc[2 * CHUNK:])
    return jnp.where(_half_mask((2 * CHUNK, 128), g), t + pltpu.roll(t, 64, 1), 0.0)


def _b_kv_specs(S):
    prev = lambda n: (jnp.maximum(n - 1, 0), 0)
    cur = lambda n: (n, 0)
    v_col = (2 * B_WIDTH + B_KV_WIDTH) // 128
    return [pl.BlockSpec((CHUNK, 128), prev), pl.BlockSpec((CHUNK, 128), cur),
            pl.BlockSpec((CHUNK, 128), lambda n: (jnp.maximum(n - 1, 0), v_col)), pl.BlockSpec((CHUNK, 128), lambda n: (n, v_col))]


def _b_attn_fwd(qn, kn, proj, sinks):
    S = qn.shape[0]

    def body(s_ref, q_ref, kp_ref, kc_ref, vp_ref, vc_ref, y_ref):
        n = pl.program_id(0)
        valid = _b_valid(n)
        k2 = jnp.concatenate([kp_ref[...], kc_ref[...]], axis=0).astype(F32)
        v2 = jnp.concatenate([vp_ref[...], vc_ref[...]], axis=0)
        for g in range(B_KV_HEADS):
            pairs = [g * PAIRS_PER_GROUP + pp for pp in range(PAIRS_PER_GROUP)]
            qs = jnp.concatenate([q_ref[:, p * 128:(p + 1) * 128] for p in pairs], axis=0)
            probs = _b_probs(qs, _b_blocks(k2, g), valid, [_b_sink_col(s_ref, g, hf) for hf in range(2)])
            pcat = jnp.concatenate([probs[0][0].astype(BF), probs[1][0].astype(BF)], axis=1)
            o = jnp.dot(pcat, _b_blocks(v2, g), preferred_element_type=F32)
            for pp, p in enumerate(pairs):
                y_ref[:, p * 128:(p + 1) * 128] = o[pp * CHUNK:(pp + 1) * CHUNK].astype(BF)

    return pl.pallas_call(
        body, name="b_attn_fwd", grid=(S // CHUNK,),
        in_specs=[pl.BlockSpec(memory_space=pltpu.SMEM), pl.BlockSpec((CHUNK, B_WIDTH), lambda n: (n, 0))] + _b_kv_specs(S),
        out_specs=pl.BlockSpec((CHUNK, B_WIDTH), lambda n: (n, 0)),
        out_shape=jax.ShapeDtypeStruct((S, B_WIDTH), BF), compiler_params=_params(("arbitrary",)),
    )(sinks, qn, kn, kn, proj, proj)


def _b_attn_bwd(qn, kn, proj, sinks, dy, after=()):
    S = qn.shape[0]

    def body(s_ref, q_ref, kp_ref, kc_ref, vp_ref, vc_ref, dy_ref, dq_ref, dk_ref, dv_ref, ds_ref):
        n = pl.program_id(0)

        @pl.when(n == 0)
        def _():
            dk_ref[...] = jnp.zeros_like(dk_ref)
            dv_ref[...] = jnp.zeros_like(dv_ref)
            ds_ref[...] = jnp.zeros_like(ds_ref)

        valid = _b_valid(n)
        k2 = jnp.concatenate([kp_ref[...], kc_ref[...]], axis=0).astype(F32)
        v2 = jnp.concatenate([vp_ref[...], vc_ref[...]], axis=0)
        lane = lax.broadcasted_iota(jnp.int32, (CHUNK, 128), 1)
        dk2 = jnp.zeros((2 * CHUNK, 128), F32)
        dv2 = jnp.zeros((2 * CHUNK, 128), F32)
        dsink = jnp.zeros((CHUNK, 128), F32)
        scale = B_HEAD_DIM ** -0.5
        nt = (((1,), (1,)), ((), ()))
        tn = (((0,), (0,)), ((), ()))
        for g in range(B_KV_HEADS):
            pairs = [g * PAIRS_PER_GROUP + pp for pp in range(PAIRS_PER_GROUP)]
            qs = jnp.concatenate([q_ref[:, p * 128:(p + 1) * 128] for p in pairs], axis=0)
            do = jnp.concatenate([dy_ref[:, p * 128:(p + 1) * 128] for p in pairs], axis=0)
            do_b = do.astype(BF)
            kblk, vblk = _b_blocks(k2, g), _b_blocks(v2, g)
            probs = _b_probs(qs, kblk, valid, [_b_sink_col(s_ref, g, hf) for hf in range(2)])
            pcat = jnp.concatenate([probs[0][0].astype(BF), probs[1][0].astype(BF)], axis=1)
            o = jnp.dot(pcat, vblk, preferred_element_type=F32)
            dp = lax.dot_general(do_b, vblk, nt, preferred_element_type=F32)
            prod = do * o
            ds_halves = []
            for hf in range(2):
                pr, ps = probs[hf]
                delta = jnp.sum(jnp.where(_half_mask(prod.shape, hf), prod, 0.0), axis=-1, keepdims=True)
                ds_halves.append((pr * (dp[:, hf * 2 * CHUNK:(hf + 1) * 2 * CHUNK] - delta) * scale).astype(BF))
                t = -ps * delta
                for pp, p in enumerate(pairs):
                    dsink = dsink + jnp.where(lane == 2 * p + hf, t[pp * CHUNK:(pp + 1) * CHUNK], 0.0)
            dsc = jnp.concatenate(ds_halves, axis=1)
            dq = jnp.dot(dsc, kblk, preferred_element_type=F32)
            for pp, p in enumerate(pairs):
                dq_ref[:, p * 128:(p + 1) * 128] = dq[pp * CHUNK:(pp + 1) * CHUNK]
            dk2 = dk2 + _b_fold(lax.dot_general(dsc, qs, tn, preferred_element_type=F32), g)
            dv2 = dv2 + _b_fold(lax.dot_general(pcat, do_b, tn, preferred_element_type=F32), g)
        ds_ref[...] += dsink
        cur = pl.ds(pl.multiple_of(n * CHUNK, CHUNK), CHUNK)
        dk_ref[cur, :] += dk2[CHUNK:]
        dv_ref[cur, :] += dv2[CHUNK:]

        @pl.when(n > 0)
        def _():
            prv = pl.ds(pl.multiple_of((n - 1) * CHUNK, CHUNK), CHUNK)
            dk_ref[prv, :] += dk2[:CHUNK]
            dv_ref[prv, :] += dv2[:CHUNK]

    full = pl.BlockSpec((S, 128), lambda n: (0, 0))
    return pl.pallas_call(
        _hide(body, 7, len(after)), name="b_attn_bwd", grid=(S // CHUNK,),
        in_specs=[pl.BlockSpec(memory_space=pltpu.SMEM), pl.BlockSpec((CHUNK, B_WIDTH), lambda n: (n, 0))] + _b_kv_specs(S)
        + [pl.BlockSpec((CHUNK, B_WIDTH), lambda n: (n, 0))] + _hidden_specs(after),
        out_specs=[pl.BlockSpec((CHUNK, B_WIDTH), lambda n: (n, 0)), full, full, pl.BlockSpec((CHUNK, 128), lambda n: (0, 0))],
        out_shape=[jax.ShapeDtypeStruct((S, B_WIDTH), F32), jax.ShapeDtypeStruct((S, 128), F32), jax.ShapeDtypeStruct((S, 128), F32),
                   jax.ShapeDtypeStruct((CHUNK, 128), F32)],
        compiler_params=_params(("arbitrary",)),
    )(sinks, qn, kn, kn, proj, proj, dy, *after)


def _c_block(q, k, v, gq, gk):
    qn = q * lax.rsqrt(jnp.mean(q * q, axis=-1, keepdims=True) + EPS) * gq
    kn = k * lax.rsqrt(jnp.mean(k * k, axis=-1, keepdims=True) + EPS) * gk
    s = lax.dot_general(qn.astype(BF), kn.astype(BF), (((1,), (1,)), ((), ())), preferred_element_type=F32) * (C_HEAD_DIM ** -0.5)
    p = jax.nn.softmax(s, axis=-1)
    return jnp.dot(p.astype(BF), v.astype(BF), preferred_element_type=F32)


def _c_specs(S, M, tq):
    q_col = (2 * A_WIDTH + B_WIDTH + 2 * B_KV_WIDTH) // 128
    return [pl.BlockSpec((tq, 128), lambda h, i: (i, q_col + h)), pl.BlockSpec((M, 128), lambda h, i: (0, h)),
            pl.BlockSpec((M, 128), lambda h, i: (0, C_HEADS + h)), pl.BlockSpec((1, 128), lambda h, i: (0, 0)),
            pl.BlockSpec((1, 128), lambda h, i: (0, 0))]


def _c_fwd(proj, kv, gq, gk):
    S, M = proj.shape[0], kv.shape[0]
    tq = _pick(S, (512,))

    def body(q_ref, k_ref, v_ref, gq_ref, gk_ref, y_ref):
        y_ref[...] = _c_block(q_ref[...], k_ref[...], v_ref[...], gq_ref[...], gk_ref[...]).astype(BF)

    return pl.pallas_call(
        body, name="c_fwd", grid=(C_HEADS, S // tq), in_specs=_c_specs(S, M, tq),
        out_specs=pl.BlockSpec((tq, 128), lambda h, i: (i, h)),
        out_shape=jax.ShapeDtypeStruct((S, C_WIDTH), BF), compiler_params=_params(("parallel", "parallel")),
    )(proj, kv, kv, gq, gk)


def _c_bwd(proj, kv, gq, gk, dy):
    S, M = proj.shape[0], kv.shape[0]
    tq = _pick(S, (512,))

    def body(q_ref, k_ref, v_ref, gq_ref, gk_ref, dy_ref, dq_ref, dk_ref, dv_ref, dgq_ref, dgk_ref):
        i = pl.program_id(1)
        _, vjp = jax.vjp(_c_block, q_ref[...], k_ref[...], v_ref[...], gq_ref[...], gk_ref[...])
        dq, dk, dv, dgq, dgk = vjp(dy_ref[...])
        dq_ref[...] = dq.astype(BF)

        @pl.when(i == 0)
        def _():
            dk_ref[...] = dk
            dv_ref[...] = dv
            dgq_ref[...] = dgq
            dgk_ref[...] = dgk

        @pl.when(i > 0)
        def _():
            dk_ref[...] += dk
            dv_ref[...] += dv
            dgq_ref[...] += dgq
            dgk_ref[...] += dgk

    return pl.pallas_call(
        body, name="c_bwd", grid=(C_HEADS, S // tq),
        in_specs=_c_specs(S, M, tq) + [pl.BlockSpec((tq, 128), lambda h, i: (i, h))],
        out_specs=[pl.BlockSpec((tq, 128), lambda h, i: (i, h)), pl.BlockSpec((M, 128), lambda h, i: (0, h)),
                   pl.BlockSpec((M, 128), lambda h, i: (0, h)), pl.BlockSpec((None, 1, 128), lambda h, i: (h, 0, 0)),
                   pl.BlockSpec((None, 1, 128), lambda h, i: (h, 0, 0))],
        out_shape=[jax.ShapeDtypeStruct((S, C_WIDTH), BF), jax.ShapeDtypeStruct((M, C_WIDTH), F32), jax.ShapeDtypeStruct((M, C_WIDTH), F32),
                   jax.ShapeDtypeStruct((C_HEADS, 1, 128), F32), jax.ShapeDtypeStruct((C_HEADS, 1, 128), F32)],
        compiler_params=_params(("parallel", "arbitrary")),
    )(proj, kv, kv, gq, gk, dy)


def _merge_specs(S, D, tm, tn, ks):
    off = GATE_OFF // tn
    nd = D // tn
    gates = [pl.BlockSpec((tm, tn), functools.partial(lambda b, m, n: (m, off + b * nd + n), b)) for b in range(3)]
    ys = [pl.BlockSpec((tm, k), lambda m, n: (m, 0)) for k in ks]
    ws = [pl.BlockSpec((None, k, tn), lambda m, n: (n, 0, 0)) for k in ks]
    return gates, ys, ws


def _merge_fwd(proj, ys, ws):
    S = proj.shape[0]
    tn = ws[0].shape[2]
    D = N_DEV * tn
    ks = [w.shape[1] for w in ws]
    tm = _pick(S, (2048,))
    gates, y_specs, w_specs = _merge_specs(S, D, tm, tn, ks)

    def body(ga_ref, gb_ref, gc_ref, ya_ref, yb_ref, yc_ref, wa_ref, wb_ref, wc_ref, m_ref, za_ref, zb_ref, zc_ref):
        acc = None
        for g_ref, y_ref, w_ref, z_ref in ((ga_ref, ya_ref, wa_ref, za_ref), (gb_ref, yb_ref, wb_ref, zb_ref),
                                           (gc_ref, yc_ref, wc_ref, zc_ref)):
            z = jnp.dot(y_ref[...], w_ref[...], preferred_element_type=F32)
            z_ref[...] = z.astype(BF)
            t = jax.nn.sigmoid(g_ref[...]) * z
            acc = t if acc is None else acc + t
        m_ref[...] = acc.astype(BF)

    tile = pl.BlockSpec((tm, tn), lambda m, n: (m, n))
    return pl.pallas_call(
        body, name="merge_fwd", grid=(S // tm, D // tn), in_specs=gates + y_specs + w_specs,
        out_specs=[tile, tile, tile, tile], out_shape=[jax.ShapeDtypeStruct((S, D), BF)] * 4,
        compiler_params=_params(("parallel", "parallel")),
    )(proj, proj, proj, *ys, *ws)


def _merge_bwd(proj, zs, dm, ws, after=()):
    S = proj.shape[0]
    tn = ws[0].shape[2]
    D = N_DEV * tn
    ks = [w.shape[1] for w in ws]
    tm = _pick(S, (1024,))
    gates, _, w_specs = _merge_specs(S, D, tm, tn, ks)
    nt = (((1,), (1,)), ((), ()))

    def body(ga_ref, gb_ref, gc_ref, za_ref, zb_ref, zc_ref, dm_ref, wa_ref, wb_ref, wc_ref,
             dza_ref, dzb_ref, dzc_ref, dga_ref, dgb_ref, dgc_ref, dya_ref, dyb_ref, dyc_ref):
        n = pl.program_id(1)
        dmv = dm_ref[...]
        for g_ref, z_ref, w_ref, dz_ref, dg_ref, dy_ref in (
                (ga_ref, za_ref, wa_ref, dza_ref, dga_ref, dya_ref), (gb_ref, zb_ref, wb_ref, dzb_ref, dgb_ref, dyb_ref),
                (gc_ref, zc_ref, wc_ref, dzc_ref, dgc_ref, dyc_ref)):
            sg = jax.nn.sigmoid(g_ref[...])
            dz = (sg * dmv).astype(BF)
            dz_ref[...] = dz
            dg_ref[...] = (dmv * z_ref[...].astype(F32) * sg * (1.0 - sg)).astype(BF)
            part = lax.dot_general(dz, w_ref[...], nt, preferred_element_type=F32)

            @pl.when(n == 0)
            def _():
                dy_ref[...] = part

            @pl.when(n > 0)
            def _():
                dy_ref[...] += part

    tile = pl.BlockSpec((tm, tn), lambda m, n: (m, n))
    dys = [pl.BlockSpec((tm, k), lambda m, n: (m, 0)) for k in ks]
    return pl.pallas_call(
        _hide(body, 10, len(after)), name="merge_bwd", grid=(S // tm, D // tn),
        in_specs=gates + [tile, tile, tile, tile] + w_specs + _hidden_specs(after),
        out_specs=[tile] * 6 + dys,
        out_shape=[jax.ShapeDtypeStruct((S, D), BF)] * 6 + [jax.ShapeDtypeStruct((S, k), F32) for k in ks],
        compiler_params=_params(("parallel", "arbitrary")),
    )(proj, proj, proj, *zs, dm, *ws, *after)


PAD = 8


def _stage_shift_down(us_ref, u_ref):
    S = u_ref.shape[1]
    us_ref[:, 0:PAD, :] = jnp.zeros((2, PAD, us_ref.shape[2]), F32)
    us_ref[:, PAD:S + PAD, :] = u_ref[...].astype(F32)


ROWS = 64


def _conv3(us_ref, part, r0, w, b):
    return (us_ref[part, pl.ds(r0 + PAD, ROWS), :] * w[2:3] + us_ref[part, pl.ds(r0 + PAD - 1, ROWS), :] * w[1:2]
            + us_ref[part, pl.ds(r0 + PAD - 2, ROWS), :] * w[0:1] + b)


def _ffn_specs(S, F, tc, c):
    per = c // tc

    def w_spec(half):
        return pl.BlockSpec((None, 3, tc), lambda j: (half * (N_DEV // 2) + j // per, 0, j % per))

    return [pl.BlockSpec((2, S, tc), lambda j: (0, 0, j)), w_spec(0), w_spec(1), pl.BlockSpec((2, 1, tc), lambda j: (0, 0, j))]


def _ffn_tile(F, c):
    tc = 128
    if c % tc or F % tc:
        raise ValueError(f"ffn tile {tc} does not divide {c}, {F}")
    return tc


def _ffn_act_fwd(up3, cws, cb3):
    _, S, F = up3.shape
    c = cws.shape[2]
    tc = _ffn_tile(F, c)

    def body(u_ref, wa_ref, wb_ref, b_ref, o_ref, us_ref):
        _stage_shift_down(us_ref, u_ref)
        wa, wb, ba, bb = wa_ref[...], wb_ref[...], b_ref[0], b_ref[1]

        def step(i, carry):
            r0 = pl.multiple_of(i * ROWS, ROWS)
            ca = _conv3(us_ref, 0, r0, wa, ba)
            cb = _conv3(us_ref, 1, r0, wb, bb)
            o_ref[pl.ds(r0, ROWS), :] = (ca * jax.nn.sigmoid(ca) * cb).astype(BF)
            return carry

        lax.fori_loop(0, S // ROWS, step, 0, unroll=4)

    return pl.pallas_call(
        body, name="ffn_act_fwd", grid=(F // tc,), in_specs=_ffn_specs(S, F, tc, c),
        out_specs=pl.BlockSpec((S, tc), lambda j: (0, j)), out_shape=jax.ShapeDtypeStruct((S, F), BF),
        scratch_shapes=[pltpu.VMEM((2, S + PAD, tc), F32)],
        compiler_params=_params(("parallel",)),
    )(up3, cws, cws, cb3)


def _ffn_act_bwd(up3, cws, cb3, dact, after=()):
    _, S, F = up3.shape
    c = cws.shape[2]
    tc = _ffn_tile(F, c)

    def body(u_ref, wa_ref, wb_ref, b_ref, da_ref, du_ref, dw_ref, db_ref, us_ref, dcs_ref):
        _stage_shift_down(us_ref, u_ref)
        ws = (wa_ref[...], wb_ref[...])
        ba, bb = b_ref[0], b_ref[1]
        dcs_ref[:, S:S + PAD, :] = jnp.zeros((2, PAD, tc), F32)

        def conv_grads(i, carry):
            r0 = pl.multiple_of(i * ROWS, ROWS)
            ca = _conv3(us_ref, 0, r0, ws[0], ba)
            cb = _conv3(us_ref, 1, r0, ws[1], bb)
            sg = jax.nn.sigmoid(ca)
            dav = da_ref[pl.ds(r0, ROWS), :].astype(F32)
            dcs_ref[0, pl.ds(r0, ROWS), :] = dav * cb * sg * (1.0 + ca * (1.0 - sg))
            dcs_ref[1, pl.ds(r0, ROWS), :] = dav * ca * sg
            return carry

        lax.fori_loop(0, S // ROWS, conv_grads, 0, unroll=4)

        def fold(v):
            return jnp.sum(v.reshape(ROWS // 8, 8, tc), axis=0)

        def input_grads(i, acc):
            r0 = pl.multiple_of(i * ROWS, ROWS)
            new = []
            for part in range(2):
                w = ws[part]
                dc = dcs_ref[part, pl.ds(r0, ROWS), :]
                dc1 = dcs_ref[part, pl.ds(r0 + 1, ROWS), :]
                dc2 = dcs_ref[part, pl.ds(r0 + 2, ROWS), :]
                u = us_ref[part, pl.ds(r0 + PAD, ROWS), :]
                du_ref[part, pl.ds(r0, ROWS), :] = (dc * w[2:3] + dc1 * w[1:2] + dc2 * w[0:1]).astype(BF)
                sums = (fold(dc2 * u), fold(dc1 * u), fold(dc * u), fold(dc))
                new += [a + s for a, s in zip(acc[4 * part:4 * part + 4], sums)]
            return tuple(new)

        acc = lax.fori_loop(0, S // ROWS, input_grads, tuple(jnp.zeros((8, tc), F32) for _ in range(8)), unroll=4)
        for part in range(2):
            for j in range(3):
                dw_ref[part, j:j + 1, :] = jnp.sum(acc[4 * part + j], axis=0, keepdims=True)
            db_ref[part] = jnp.sum(acc[4 * part + 3], axis=0, keepdims=True)

    return pl.pallas_call(
        _hide(body, 5, len(after)), name="ffn_act_bwd", grid=(F // tc,),
        in_specs=_ffn_specs(S, F, tc, c) + [pl.BlockSpec((S, tc), lambda j: (0, j))] + _hidden_specs(after),
        out_specs=[pl.BlockSpec((2, S, tc), lambda j: (0, 0, j)), pl.BlockSpec((2, 3, tc), lambda j: (0, 0, j)),
                   pl.BlockSpec((2, 1, tc), lambda j: (0, 0, j))],
        out_shape=[jax.ShapeDtypeStruct((2, S, F), BF), jax.ShapeDtypeStruct((2, 3, F), F32), jax.ShapeDtypeStruct((2, 1, F), F32)],
        scratch_shapes=[pltpu.VMEM((2, S + PAD, tc), F32), pltpu.VMEM((2, S + PAD, tc), F32)],
        compiler_params=_params(("parallel",)),
    )(up3, cws, cws, cb3, dact, *after)


def _residual_rms(a, w, x, g, name, tm=512):
    S, K = a.shape
    D = w.shape[1]
    tm = _pick(S, (tm,))

    def body(a_ref, w_ref, x_ref, g_ref, x1_ref, h_ref, r_ref):
        x1 = jnp.dot(a_ref[...], w_ref[...], preferred_element_type=F32) + x_ref[...]
        r = lax.rsqrt(jnp.mean(x1 * x1, axis=-1, keepdims=True) + EPS)
        x1_ref[...] = x1
        h_ref[...] = (x1 * r * g_ref[...]).astype(BF)
        r_ref[...] = r

    row = pl.BlockSpec((tm, D), lambda i: (i, 0))
    return pl.pallas_call(
        body, name=name, grid=(S // tm,),
        in_specs=[pl.BlockSpec((tm, K), lambda i: (i, 0)), pl.BlockSpec((K, D), lambda i: (0, 0)), row, pl.BlockSpec((1, D), lambda i: (0, 0))],
        out_specs=[row, row, pl.BlockSpec((tm, 1), lambda i: (i, 0))],
        out_shape=[jax.ShapeDtypeStruct((S, D), F32), jax.ShapeDtypeStruct((S, D), BF), jax.ShapeDtypeStruct((S, 1), F32)],
        compiler_params=_params(("parallel",)),
    )(a, w, x, g)


def _out_loss(act, w_down, x1, target, tm=1024, tn=1024, tk=1408):
    S, F = act.shape
    D = w_down.shape[1]
    tm, tn, tk = _pick(S, (tm,)), _pick(D, (tn,)), _pick(F, (tk,))
    nm, nn, nk = S // tm, D // tn, F // tk

    def body(a_ref, b_ref, x_ref, t_ref, dy_ref, dyb_ref, l_ref, acc):
        m, n, k = pl.program_id(0), pl.program_id(1), pl.program_id(2)

        @pl.when((m == 0) & (n == 0) & (k == 0))
        def _():
            l_ref[...] = jnp.zeros_like(l_ref)

        @pl.when(k == 0)
        def _():
            acc[...] = jnp.zeros_like(acc)

        acc[...] += jnp.dot(a_ref[...], b_ref[...], preferred_element_type=F32)

        @pl.when(k == nk - 1)
        def _():
            e = acc[...] + x_ref[...] - t_ref[...]
            dy = e * (1.0 / D)
            dy_ref[...] = dy
            dyb_ref[...] = dy.astype(BF)
            l_ref[...] += jnp.sum(jnp.sum(e * e, axis=-1, keepdims=True), axis=0, keepdims=True) * (0.5 / D)

    tile = pl.BlockSpec((tm, tn), lambda m, n, k: (m, n))
    return pl.pallas_call(
        body, name="mm_y_loss", grid=(nm, nn, nk),
        in_specs=[pl.BlockSpec((tm, tk), lambda m, n, k: (m, k)), pl.BlockSpec((tk, tn), lambda m, n, k: (k, n)), tile, tile],
        out_specs=[tile, tile, pl.BlockSpec((8, 128), lambda m, n, k: (0, 0))],
        out_shape=[jax.ShapeDtypeStruct((S, D), F32), jax.ShapeDtypeStruct((S, D), BF), jax.ShapeDtypeStruct((8, 128), F32)],
        scratch_shapes=[pltpu.VMEM((tm, tn), F32)],
        compiler_params=_params(("arbitrary", "arbitrary", "arbitrary")),
    )(act, w_down, x1, target)


def _allgather(shards, name):
    n = len(shards)

    def body(*refs):
        ins, outs = refs[:n], refs[n:2 * n]
        send_sems, recv_sems, local_sems = refs[2 * n:]
        x, y, c = lax.axis_index("x"), lax.axis_index("y"), lax.axis_index("c")
        me, sibling = (x, y, c), (x, y, 1 - c)
        chips = [(1 - x, y), (x, 1 - y), (1 - x, 1 - y)]

        def blk(w, px, py, pc):
            return outs[w].at[4 * px + 2 * py + pc]

        def copy(w, k, block, to, src=None):
            return pltpu.make_async_remote_copy(
                src_ref=blk(w, *block) if src is None else src, dst_ref=blk(w, *block),
                send_sem=send_sems.at[w, k], recv_sem=recv_sems.at[w, k], device_id=to, device_id_type=MESH)

        started = []
        mine = []
        for w in range(n):
            mine.append(pltpu.make_async_copy(ins[w], blk(w, *me), local_sems.at[w]))
            mine[-1].start()
            first = [copy(w, 0, me, sibling, src=ins[w])]
            first += [copy(w, 1 + j, me, (*chip, c), src=ins[w]) for j, chip in enumerate(chips)]
            for cp in first:
                cp.start()
            started += first
        for w in range(n):
            for j, chip in enumerate(chips):
                copy(w, 1 + j, (*chip, c), me).wait_recv()
                fwd = copy(w, 4 + j, (*chip, c), sibling)
                fwd.start()
                started.append(fwd)
        for w in range(n):
            copy(w, 0, sibling, me).wait_recv()
            for j, chip in enumerate(chips):
                copy(w, 4 + j, (*chip, 1 - c), me).wait_recv()
        for cp in started:
            cp.wait_send()
        for cp in mine:
            cp.wait()

    whole = pl.BlockSpec(memory_space=pltpu.VMEM)
    outs = pl.pallas_call(
        body, name=name, in_specs=[whole] * n, out_specs=[whole] * n,
        out_shape=[jax.ShapeDtypeStruct((N_DEV,) + s.shape, s.dtype) for s in shards],
        scratch_shapes=[pltpu.SemaphoreType.DMA((n, 7)), pltpu.SemaphoreType.DMA((n, 7)), pltpu.SemaphoreType.DMA((n,))],
    )(*shards)
    return list(outs)


def _allgather_seq(shards, name, collective_id, after=()):
    n = len(shards)
    n_after = len(after)

    halves = [s.shape[0] % 32 == 0 for s in shards]
    n_sem = 8
    to_diagonal = not all(halves)

    def body(*refs):
        ins, outs = refs[:n], refs[n + n_after:2 * n + n_after]
        send_sems, recv_sems, local_sems = refs[2 * n + n_after:]
        x, y, c = lax.axis_index("x"), lax.axis_index("y"), lax.axis_index("c")
        me, sibling = (x, y, c), (x, y, 1 - c)
        x_nb, y_nb, diag = (1 - x, y, c), (x, 1 - y, c), (1 - x, 1 - y, c)
        peers = [sibling, x_nb, y_nb] + ([diag] if to_diagonal else [])
        barrier = pltpu.get_barrier_semaphore()
        for peer in peers:
            pl.semaphore_signal(barrier, inc=1, device_id=peer, device_id_type=MESH)
        pl.semaphore_wait(barrier, len(peers))

        def blk(w, dev, rows=None):
            ref = outs[w].at[4 * dev[0] + 2 * dev[1] + dev[2]]
            return ref if rows is None else ref.at[rows]

        def copy(w, k, block, to, src=None, rows=None):
            return pltpu.make_async_remote_copy(
                src_ref=blk(w, block, rows) if src is None else src, dst_ref=blk(w, block, rows),
                send_sem=send_sems.at[n_sem * w + k], recv_sem=recv_sems.at[n_sem * w + k], device_id=to, device_id_type=MESH)

        def top(w):
            return pl.ds(0, shards[w].shape[0] // 2)

        def bottom(w):
            return pl.ds(shards[w].shape[0] // 2, shards[w].shape[0] // 2)

        started = []
        mine = []
        for w in range(n):
            mine.append(pltpu.make_async_copy(ins[w], blk(w, me), local_sems.at[w]))
            mine[-1].start()
            first = [copy(w, 0, me, sibling, src=ins[w]), copy(w, 1, me, x_nb, src=ins[w]), copy(w, 2, me, y_nb, src=ins[w])]
            if not halves[w]:
                first.append(copy(w, 3, me, diag, src=ins[w]))
            for cp in first:
                cp.start()
            started += first
        for w in range(n):
            copy(w, 1, x_nb, me).wait_recv()
            onward = [copy(w, 5, x_nb, sibling)] + ([copy(w, 3, x_nb, y_nb, rows=top(w))] if halves[w] else [])
            copy(w, 2, y_nb, me).wait_recv()
            onward += [copy(w, 6, y_nb, sibling)] + ([copy(w, 4, y_nb, x_nb, rows=bottom(w))] if halves[w] else [])
            for cp in onward:
                cp.start()
            started += onward
        for w in range(n):
            if halves[w]:
                copy(w, 3, diag, me, rows=top(w)).wait_recv()
                copy(w, 4, diag, me, rows=bottom(w)).wait_recv()
            else:
                copy(w, 3, diag, me).wait_recv()
            fwd = copy(w, 7, diag, sibling)
            fwd.start()
            started.append(fwd)
        for w in range(n):
            for k, dev in ((0, sibling), (5, (1 - x, y, 1 - c)), (6, (x, 1 - y, 1 - c)), (7, (1 - x, 1 - y, 1 - c))):
                copy(w, k, dev, me).wait_recv()
        for cp in started:
            cp.wait_send()
        for cp in mine:
            cp.wait()

    outs = pl.kernel(
        body, name=name, out_type=[jax.ShapeDtypeStruct((N_DEV,) + s.shape, s.dtype) for s in shards],
        mesh=plsc.ScalarSubcoreMesh(axis_name="seq", num_cores=1),
        scratch_types=[pltpu.SemaphoreType.DMA((n_sem * n,)), pltpu.SemaphoreType.DMA((n_sem * n,)), pltpu.SemaphoreType.DMA((n,))],
        compiler_params=pltpu.CompilerParams(collective_id=collective_id),
    )(*shards, *after)
    return list(outs)


def _sibling_exchange(grads, name):
    n = len(grads)

    def body(*refs):
        ins, outs = refs[:n], refs[n:2 * n]
        send_sems, recv_sems = refs[2 * n:]
        x, y, c = lax.axis_index("x"), lax.axis_index("y"), lax.axis_index("c")
        copies = [pltpu.make_async_remote_copy(
            src_ref=ins[w].at[:, 1 - c], dst_ref=outs[w], send_sem=send_sems.at[w], recv_sem=recv_sems.at[w],
            device_id=(x, y, 1 - c), device_id_type=MESH) for w in range(n)]
        for cp in copies:
            cp.start()
        for cp in copies:
            cp.wait()

    outs = pl.pallas_call(
        body, name=name, in_specs=_hidden_specs(grads), out_specs=_hidden_specs(grads),
        out_shape=[jax.ShapeDtypeStruct((g.shape[0],) + g.shape[2:], g.dtype) for g in grads],
        scratch_shapes=[pltpu.SemaphoreType.DMA((n,)), pltpu.SemaphoreType.DMA((n,))],
    )(*grads)
    return list(outs)


def _chip_exchange(sums, name, collective_id):
    n = len(sums)

    def body(*refs):
        ins, outs = refs[:n], refs[n:2 * n]
        send_sems, recv_sems = refs[2 * n:]
        x, y, c = lax.axis_index("x"), lax.axis_index("y"), lax.axis_index("c")
        chips = [(1 - x, y), (x, 1 - y), (1 - x, 1 - y)]
        barrier = pltpu.get_barrier_semaphore()
        for px, py in chips:
            pl.semaphore_signal(barrier, inc=1, device_id=(px, py, c), device_id_type=MESH)
        pl.semaphore_wait(barrier, 3)
        copies = []
        for w in range(n):
            for k, (px, py) in enumerate(chips):
                copies.append(pltpu.make_async_remote_copy(
                    src_ref=ins[w].at[2 * px + py], dst_ref=outs[w].at[k], send_sem=send_sems.at[3 * w + k],
                    recv_sem=recv_sems.at[3 * w + k], device_id=(px, py, c), device_id_type=MESH))
        for cp in copies:
            cp.start()
        for cp in copies:
            cp.wait()

    outs = pl.kernel(
        body, name=name, out_type=[jax.ShapeDtypeStruct((3,) + s.shape[1:], s.dtype) for s in sums],
        mesh=plsc.ScalarSubcoreMesh(axis_name="seq", num_cores=1),
        scratch_types=[pltpu.SemaphoreType.DMA((3 * n,)), pltpu.SemaphoreType.DMA((3 * n,))],
        compiler_params=pltpu.CompilerParams(collective_id=collective_id),
    )(*sums)
    return list(outs)


def _row_tile(r, c, elems=256 * 1024):
    want = max(8, elems // c)
    for t in range(min(want, r) // 8 * 8, 0, -8):
        if r % t == 0:
            return t
    return r


def _pair_add(g4, recv, core, name, after=()):
    _, _, r, c = g4.shape
    tr = _row_tile(r, c, 1024 * 1024)

    def body(core_ref, a_ref, b_ref, o_ref):
        o_ref[...] = (a_ref[...].astype(F32) + b_ref[...].astype(F32)).astype(BF)

    return pl.pallas_call(
        _hide(body, 3, len(after)), name=name,
        grid_spec=pltpu.PrefetchScalarGridSpec(
            num_scalar_prefetch=1, grid=(4, r // tr),
            in_specs=[pl.BlockSpec((None, None, tr, c), lambda p, i, s: (p, s[0], i, 0)),
                      pl.BlockSpec((None, tr, c), lambda p, i, s: (p, i, 0))] + _hidden_specs(after),
            out_specs=pl.BlockSpec((None, tr, c), lambda p, i, s: (p, i, 0))),
        out_shape=jax.ShapeDtypeStruct((4, r, c), BF), compiler_params=_params(("parallel", "parallel")),
    )(core, g4, recv, *after)


def _adam_math(w, g, m, v):
    m = ADAM_B1 * m + (1.0 - ADAM_B1) * g
    v = ADAM_B2 * v + (1.0 - ADAM_B2) * (g * g)
    m_hat = m / (1.0 - ADAM_B1 ** ADAM_STEP)
    v_hat = v / (1.0 - ADAM_B2 ** ADAM_STEP)
    delta = -ADAM_LR * (m_hat / (jnp.sqrt(v_hat) + ADAM_EPS) + ADAM_WD * w)
    return delta, m, v


def _adamw_big(sums, recv, chip, w, m, v, name, after=()):
    r, c = w.shape
    tr = _row_tile(r, c, 512 * 1024)

    def body(chip_ref, s_ref, r_ref, w_ref, m_ref, v_ref, g_out, d_out, m_out, v_out):
        g = s_ref[...].astype(F32) + r_ref[0].astype(F32)
        g = g + r_ref[1].astype(F32)
        g = g + r_ref[2].astype(F32)
        delta, mn, vn = _adam_math(w_ref[...], g, m_ref[...], v_ref[...])
        g_out[...] = g
        d_out[...] = delta
        m_out[...] = mn
        v_out[...] = vn

    row = pl.BlockSpec((tr, c), lambda i, s: (i, 0))
    return pl.pallas_call(
        _hide(body, 6, len(after)), name=name,
        grid_spec=pltpu.PrefetchScalarGridSpec(
            num_scalar_prefetch=1, grid=(r // tr,),
            in_specs=[pl.BlockSpec((None, tr, c), lambda i, s: (s[0], i, 0)), pl.BlockSpec((3, tr, c), lambda i, s: (0, i, 0)),
                      row, row, row] + _hidden_specs(after),
            out_specs=[row, row, row, row]),
        out_shape=[jax.ShapeDtypeStruct((r, c), F32)] * 4, compiler_params=_params(("parallel",)),
    )(chip, sums, recv, w, m, v, *after)


def _adamw_small(parts, ws, ms, vs, extra_parts, name, after=()):
    n, ne = len(ws), len(extra_parts)

    def total(p_ref):
        g = p_ref[0]
        for d in range(1, N_DEV):
            g = g + p_ref[d]
        return g

    def body(*refs):
        p_refs, w_refs, m_refs, v_refs = refs[:n], refs[n:2 * n], refs[2 * n:3 * n], refs[3 * n:4 * n]
        e_refs = refs[4 * n:4 * n + ne]
        outs = refs[4 * n + ne:]
        for i in range(n):
            g = total(p_refs[i])
            delta, mn, vn = _adam_math(w_refs[i][...], g, m_refs[i][...], v_refs[i][...])
            outs[4 * i][...] = g
            outs[4 * i + 1][...] = delta
            outs[4 * i + 2][...] = mn
            outs[4 * i + 3][...] = vn
        for i in range(ne):
            outs[4 * n + i][...] = total(e_refs[i])

    out_shape = []
    for w in ws:
        out_shape += [jax.ShapeDtypeStruct(w.shape, F32)] * 4
    out_shape += [jax.ShapeDtypeStruct(e.shape[1:], F32) for e in extra_parts]
    args = [*parts, *ws, *ms, *vs, *extra_parts]
    res = pl.pallas_call(_hide(body, len(args), len(after)), name=name, out_shape=out_shape,
                         in_specs=[pl.BlockSpec(memory_space=pltpu.VMEM)] * len(args) + _hidden_specs(after),
                         compiler_params=pltpu.CompilerParams(vmem_limit_bytes=VMEM_LIMIT))(*args, *after)
    return [res[4 * i:4 * i + 4] for i in range(n)], list(res[4 * n:])


def _adamw_plain(g, w, m, v, name):
    def body(g_ref, w_ref, m_ref, v_ref, d_out, m_out, v_out):
        delta, mn, vn = _adam_math(w_ref[...], g_ref[...], m_ref[...], v_ref[...])
        d_out[...] = delta
        m_out[...] = mn
        v_out[...] = vn

    return pl.pallas_call(body, name=name, out_shape=[jax.ShapeDtypeStruct(w.shape, F32)] * 3)(g, w, m, v)


def kernel(x, mem, positions, g_mix, w_in, g_a_v, w_spatial, b_spatial, g_b_q, g_b_k, sinks, g_mem, w_mem_kv, g_c_q, g_c_k, w_branch_a, w_branch_b, w_branch_c, w_out, g_ffn, w_up, conv_w, conv_b, w_down, loss_target, m_g_mix, m_w_in, m_g_a_v, m_w_spatial, m_b_spatial, m_g_b_q, m_g_b_k, m_sinks, m_g_mem, m_w_mem_kv, m_g_c_q, m_g_c_k, m_w_branch_a, m_w_branch_b, m_w_branch_c, m_w_out, m_g_ffn, m_w_up, m_conv_w, m_conv_b, m_w_down, v_g_mix, v_w_in, v_g_a_v, v_w_spatial, v_b_spatial, v_g_b_q, v_g_b_k, v_sinks, v_g_mem, v_w_mem_kv, v_g_c_q, v_g_c_k, v_w_branch_a, v_w_branch_b, v_w_branch_c, v_w_out, v_g_ffn, v_w_up, v_conv_w, v_conv_b, v_w_down):
    S, D = x.shape[1], x.shape[2]
    M = mem.shape[1]
    F = w_down.shape[1] * N_DEV
    in_cols = w_in.shape[2] * N_DEV
    ax, ay, ac = lax.axis_index("x"), lax.axis_index("y"), lax.axis_index("c")
    core = jnp.reshape(ac, (1,)).astype(jnp.int32)
    chip = jnp.reshape(2 * ax + ay, (1,)).astype(jnp.int32)
    me = 4 * ax + 2 * ay + ac

    x2, mem2, tgt2 = x[0], mem[0], loss_target[0]

    big = dict(w_in=w_in[0].T, w_mem_kv=w_mem_kv[0], w_branch_a=w_branch_a[0], w_branch_b=w_branch_b[0],
               w_branch_c=w_branch_c[0], w_out=w_out[0], w_up=w_up[0], w_down=w_down[0])
    names = list(big)
    cast = {k: big[k].astype(BF) for k in names}
    W = {}
    cb3 = conv_b.reshape(2, 1, F)
    W["w_in"], = _allgather_seq([cast["w_in"]], "ag_seq0", 0)
    w_in_t = W["w_in"].reshape(in_cols, D)
    grp1 = ["w_mem_kv", "w_branch_a", "w_branch_b", "w_branch_c", "w_out"]
    res1 = _allgather_seq([cast[k] for k in grp1] + [conv_w[0]], "ag_seq1", 1, after=(_token((w_in_t,), "tok_w_in"),))
    W.update(zip(grp1, res1))
    cw3 = res1[-1]
    w_kv_f = W["w_mem_kv"].reshape(D, 2 * C_WIDTH)
    w_out_f = W["w_out"].reshape(D, D)

    half = ROPE_DIM // 2
    inv = ROPE_THETA ** (-jnp.arange(half, dtype=F32) / half)
    ang = positions[0].astype(F32)[:, None] * inv
    cos, sin = jnp.cos(ang), jnp.sin(ang)
    one, zero = jnp.ones((S, B_HEAD_DIM - ROPE_DIM), F32), jnp.zeros((S, B_HEAD_DIM - ROPE_DIM), F32)
    z8 = jnp.zeros((S, half), F32)
    ct = jnp.tile(jnp.concatenate([cos, cos, one], axis=1), (1, 2))
    sa = jnp.tile(jnp.concatenate([-sin, z8, zero], axis=1), (1, 2))
    sb = jnp.tile(jnp.concatenate([z8, sin, zero], axis=1), (1, 2))
    gq2, gk2 = jnp.tile(g_b_q, (1, 2)), jnp.tile(g_b_k, (1, 2))
    b_t = b_spatial[0].T

    h, rstd1 = _rms_fwd(x2, g_mix, "rms1_fwd")
    proj = _mm(h, w_in_t, "nt", F32, "mm_proj", tn=1280)
    y_a = _a_fwd(proj, g_a_v, w_spatial[0], b_t)
    W["w_up"], = _allgather_seq([cast["w_up"]], "ag_seq2", 2, after=(_token((W["w_out"], proj), "tok_group1"),))
    qn, kn = _b_pre(proj, gq2, gk2, ct, sa, sb)
    y_b = _b_attn_fwd(qn, kn, proj, sinks)
    mem_h, rstd_m = _rms_fwd(mem2, g_mem, "rmsmem_fwd")
    kv = _mm(mem_h, w_kv_f, "nn", F32, "mm_kv", after=(y_b,))
    y_c = _c_fwd(proj, kv, g_c_q, g_c_k)
    w_branches = [W["w_branch_a"], W["w_branch_b"], W["w_branch_c"]]
    merged, z_a, z_b, z_c = _merge_fwd(proj, [y_a, y_b, y_c], w_branches)
    x1, h2, rstd2 = _residual_rms(merged, w_out_f, x2, g_ffn, "mm_x1_rms2")
    W["w_down"], = _allgather_seq([cast["w_down"]], "ag_seq3", 3, after=(W["w_up"], h2))
    w_down_f = W["w_down"].reshape(F, D)
    up3 = _mm(h2, W["w_up"], "nn", BF, "mm_up", b_stack=True, out_parts=2)
    act = _ffn_act_fwd(up3, cw3, cb3)
    dy, dy_b, loss_acc = _out_loss(act, w_down_f, x1, tgt2)

    reduced = {}

    def as4(g):
        return g.reshape(4, 2, g.shape[1], g.shape[2])

    def finish_group(gi, keys, g4, from_sibling):
        sums = [_pair_add(a, b, core, "rs_add_" + k) for k, a, b in zip(keys, g4, from_sibling)]
        from_chips = _chip_exchange(sums, f"rs_chip{gi}", 4 + gi)
        reduced.update(zip(keys, zip(sums, from_chips)))
        return tuple(sums)

    d_act = _mm(dy_b, w_down_f, "nt", BF, "mm_dact", tn=1408)
    g_down = _mm(act, dy_b, "tn", BF, "mm_gdown", tm=1408)
    d_up3, d_cw3, d_cb3 = _ffn_act_bwd(up3, cw3, cb3, d_act, after=(g_down,))
    grp0 = [as4(g_down.reshape(N_DEV, F // N_DEV, D))]
    g_up, sib0 = _mm(h2, d_up3, "tn", BF, "mm_gup", b_parts=2, out_stack=True, exchange=grp0)
    sums0 = finish_group(0, ["w_down"], grp0, sib0)
    grp1 = [as4(g_up)]
    d_h2, sib1 = _mm(d_up3, W["w_up"], "nt", F32, "mm_dh2", a_parts=2, b_stack=True, tm=2048, after=sums0, exchange=grp1)
    sums1 = finish_group(1, ["w_up"], grp1, sib1)
    dx1, dx1_b, d_g_ffn = _rms_bwd(x1, rstd2, g_ffn, d_h2, dy, "rms2_bwd", after=sums1)
    g_out = _mm(merged, dx1_b, "tn", BF, "mm_gout")
    grp2 = [as4(g_out.reshape(N_DEV, D // N_DEV, D))]
    d_merged, sib2 = _mm(dx1_b, w_out_f, "nt", F32, "mm_dmerged", exchange=grp2)
    sums2 = finish_group(2, ["w_out"], grp2, sib2)
    dz_a, dz_b, dz_c, dga, dgb, dgc, dy_a, dy_b_, dy_c = _merge_bwd(proj, [z_a, z_b, z_c], d_merged, w_branches, after=sums2)
    g_ba = _mm(y_a, dz_a, "tn", BF, "mm_gba", out_stack=True)
    g_bb = _mm(y_b, dz_b, "tn", BF, "mm_gbb", out_stack=True)
    g_bc = _mm(y_c, dz_c, "tn", BF, "mm_gbc", out_stack=True)
    d_uv, d_g_a_v, d_w_s, d_b_t = _a_bwd(proj, g_a_v, w_spatial[0], b_t, dy_a, after=(g_ba, g_bb, g_bc))
    dqn, dkn, dv_b, dsink_rows = _b_attn_bwd(qn, kn, proj, sinks, dy_b_)
    d_qkv, d_gq2, d_gk2 = _b_pre_bwd(proj, gq2, gk2, ct, sa, sb, dqn, dkn, dv_b)
    dq_c, dk_c, dv_c, d_gcq, d_gck = _c_bwd(proj, kv, g_c_q, g_c_k, dy_c)
    dkv_b = jnp.concatenate([dk_c, dv_c], axis=1).astype(BF)
    d_memh = _mm(dkv_b, w_kv_f, "nt", F32, "mm_dmemh")
    g_kv = _mm(mem_h, dkv_b, "tn", BF, "mm_gkv")
    _, _, d_g_mem = _rms_bwd(mem2, rstd_m, g_mem, d_memh, None, "rmsmem_bwd")
    dproj = jnp.concatenate([d_uv, d_qkv, dq_c, dga, dgb, dgc], axis=1)
    grp3 = [as4(g_ba), as4(g_bb), as4(g_bc)]
    g_in, sib3 = _mm(dproj, h, "tn", BF, "mm_gin", tm=1280, exchange=grp3)
    sums3 = finish_group(3, ["w_branch_a", "w_branch_b", "w_branch_c"], grp3, sib3)
    grp4 = [as4(g_in.reshape(N_DEV, in_cols // N_DEV, D)), as4(g_kv.reshape(N_DEV, D // N_DEV, 2 * C_WIDTH))]
    sums4 = finish_group(4, ["w_in", "w_mem_kv"], grp4, _sibling_exchange(grp4, "rs_sib4"))
    d_h = _mm(dproj, w_in_t, "nn", F32, "mm_dh", tm=2048, tk=1280, after=sums3 + sums4)
    grad_x, _, d_g_mix = _rms_bwd(x2, rstd1, g_mix, d_h, dx1, "rms1_bwd", after=sums4)

    small_names =["g_mix", "g_a_v", "w_spatial", "b_spatial", "g_b_q", "g_b_k", "sinks", "g_mem", "g_c_q", "g_c_k", "g_ffn", "conv_b"]
    small_w = dict(g_mix=g_mix, g_a_v=g_a_v, w_spatial=w_spatial, b_spatial=b_spatial, g_b_q=g_b_q, g_b_k=g_b_k, sinks=sinks,
                   g_mem=g_mem, g_c_q=g_c_q, g_c_k=g_c_k, g_ffn=g_ffn, conv_b=conv_b)
    small_m = dict(g_mix=m_g_mix, g_a_v=m_g_a_v, w_spatial=m_w_spatial, b_spatial=m_b_spatial, g_b_q=m_g_b_q, g_b_k=m_g_b_k,
                   sinks=m_sinks, g_mem=m_g_mem, g_c_q=m_g_c_q, g_c_k=m_g_c_k, g_ffn=m_g_ffn, conv_b=m_conv_b)
    small_v = dict(g_mix=v_g_mix, g_a_v=v_g_a_v, w_spatial=v_w_spatial, b_spatial=v_b_spatial, g_b_q=v_g_b_q, g_b_k=v_g_b_k,
                   sinks=v_sinks, g_mem=v_g_mem, g_c_q=v_g_c_q, g_c_k=v_g_c_k, g_ffn=v_g_ffn, conv_b=v_conv_b)
    small_g = dict(
        g_mix=d_g_mix, g_a_v=d_g_a_v, w_spatial=d_w_s, b_spatial=d_b_t.T,
        g_b_q=d_gq2.reshape(2, B_HEAD_DIM).sum(0), g_b_k=d_gk2.reshape(2, B_HEAD_DIM).sum(0),
        sinks=dsink_rows.sum(0)[:B_HEADS], g_mem=d_g_mem, g_c_q=d_gcq.sum(0), g_c_k=d_gck.sum(0), g_ffn=d_g_ffn,
        conv_b=d_cb3)
    partial = [small_g[k].reshape(small_w[k].shape) for k in small_names] + [d_cw3, loss_acc[0:1]]
    parts = _allgather_seq(partial, "ag_small", 9)
    n_small = len(small_names)
    big_out = {}

    moments = dict(w_in=(m_w_in, v_w_in), w_mem_kv=(m_w_mem_kv, v_w_mem_kv), w_branch_a=(m_w_branch_a, v_w_branch_a),
                   w_branch_b=(m_w_branch_b, v_w_branch_b), w_branch_c=(m_w_branch_c, v_w_branch_c), w_out=(m_w_out, v_w_out),
                   w_up=(m_w_up, v_w_up), w_down=(m_w_down, v_w_down))
    token = (grad_x,)
    for k in ["w_down", "w_up", "w_out", "w_branch_a", "w_branch_b", "w_branch_c", "w_mem_kv", "w_in"]:
        s, r = reduced[k]
        mk, vk = moments[k][0][0], moments[k][1][0]
        if k == "w_in":
            res = _adamw_big(s, r, chip, big[k], mk.T, vk.T, "adamw_" + k, after=token)
            big_out[k] = [a.T[None] for a in res]
        else:
            res = _adamw_big(s, r, chip, big[k], mk, vk, "adamw_" + k, after=token)
            big_out[k] = [a[None] for a in res]
        token = (res[0],)

    small_res, (g_cw3, loss_row) = _adamw_small(parts[:n_small], [small_w[k] for k in small_names], [small_m[k] for k in small_names],
                                                [small_v[k] for k in small_names], parts[n_small:], "adamw_small", after=token)
    loss = loss_row[0, 0]
    small_out = dict(zip(small_names, small_res))
    c_cw = 2 * F // N_DEV
    g_cw = lax.dynamic_slice(g_cw3, (me // (N_DEV // 2), 0, (me % (N_DEV // 2)) * c_cw), (1, 3, c_cw))[0]
    cw_res = _adamw_plain(g_cw, conv_w[0], m_conv_w[0], v_conv_w[0], "adamw_conv_w")
    big_out["conv_w"] = [g_cw[None]] + [a[None] for a in cw_res]

    order = ["g_mix", "w_in", "g_a_v", "w_spatial", "b_spatial", "g_b_q", "g_b_k", "sinks", "g_mem", "w_mem_kv", "g_c_q", "g_c_k",
             "w_branch_a", "w_branch_b", "w_branch_c", "w_out", "g_ffn", "w_up", "conv_w", "conv_b", "w_down"]
    res = {**small_out, **big_out}
    outs = [loss, grad_x[None]]
    for field in range(4):
        outs += [res[k][field] for k in order]
    return tuple(outs)
```
